```python
import math
import jax, jax.numpy as jnp
from jax import lax
import numpy as np

D_MODEL = 1024
BATCH = 4
SEQ = 4096
DEPTH = 1

N_META = 16
D_MIX = D_MODEL
D_CONV = D_MIX // 2
D_SSM = D_MIX - D_CONV
CONV_WIDTH = 31
SSM_GROUP = 16
N_SSM_GROUPS = D_SSM // SSM_GROUP
SSM_STATE = 64
N_DIR = 2
D_FF = 4 * D_MODEL
D_IN_PROJ = 2 * D_CONV + D_SSM
DT_MIN = 1e-3
DT_MAX = 1e-1
NORM_EPS = 1e-5

kernel_name = 'hymba_conformer_s5_bidir_block'


def _rms_norm(x, g):
    xf = x.astype(jnp.float32)
    y = xf * lax.rsqrt(jnp.mean(jnp.square(xf), axis=-1, keepdims=True) + NORM_EPS)
    return (y * g.astype(jnp.float32)).astype(x.dtype)


def _layer_norm(x, g, b):
    xf = x.astype(jnp.float32)
    xc = xf - jnp.mean(xf, axis=-1, keepdims=True)
    y = xc * lax.rsqrt(jnp.mean(jnp.square(xc), axis=-1, keepdims=True) + NORM_EPS)
    return (y * g.astype(jnp.float32) + b.astype(jnp.float32)).astype(x.dtype)


def _conv_module(v, gate, conv_w, conv_b, ln_g, ln_b):
    u = v * jax.nn.sigmoid(gate)
    pad = CONV_WIDTH // 2
    y = lax.conv_general_dilated(
        u, conv_w[:, None, :].astype(u.dtype), window_strides=(1,),
        padding=[(pad, pad)], dimension_numbers=('NWC', 'WIO', 'NWC'),
        feature_group_count=D_CONV)
    y = y + conv_b.astype(u.dtype)
    return jax.nn.silu(_layer_norm(y, ln_g, ln_b))


def _linear_recurrence_op(left, right):
    a_l, b_l = left
    a_r, b_r = right
    return a_r * a_l, a_r * b_l + b_r


def _s5_direction(u, lam_re, lam_im, log_dt, b_re, b_im, c_re, c_im, reverse):
    f32 = jnp.float32
    lam = lax.complex(lam_re.astype(f32), lam_im.astype(f32))
    dt = jnp.exp(log_dt.astype(f32))[:, None]
    lam_bar = jnp.exp(lam * dt)
    b_mat = lax.complex(b_re.astype(f32), b_im.astype(f32))
    b_bar = ((lam_bar - 1.0) / lam)[..., None] * b_mat
    c_mat = lax.complex(c_re.astype(f32), c_im.astype(f32))
    bu = jnp.einsum('gph,blgh->blgp', b_bar, u.astype(jnp.complex64))
    a = jnp.broadcast_to(lam_bar, bu.shape)
    _, states = lax.associative_scan(_linear_recurrence_op, (a, bu), reverse=reverse, axis=1)
    return jnp.einsum('ghp,blgp->blgh', c_mat, states).real


def _s5_mixer(u, lam_re, lam_im, log_dt, b_re, b_im, c_re, c_im, d_skip, glu_w, glu_b):
    bsz, length, _ = u.shape
    uf = u.astype(jnp.float32)
    ug = uf.reshape(bsz, length, N_SSM_GROUPS, SSM_GROUP)
    y = _s5_direction(ug, lam_re[0], lam_im[0], log_dt[0], b_re[0], b_im[0], c_re[0], c_im[0], False)
    for d in range(1, N_DIR):
        y = y + _s5_direction(ug, lam_re[d], lam_im[d], log_dt[d], b_re[d], b_im[d],
                              c_re[d], c_im[d], True)
    y = y.reshape(bsz, length, D_SSM) + d_skip.astype(jnp.float32) * uf
    y = jax.nn.gelu(y).astype(u.dtype)
    return y * jax.nn.sigmoid(y @ glu_w + glu_b)


def setup_inputs(seed: int = 0) -> dict:
    key = jax.random.key(seed)
    ks = jax.random.split(key, 24)
    f32 = jnp.float32

    def nrm(k, shape, scale):
        return jax.random.normal(k, shape, f32) * scale

    ssm_shape = (DEPTH, N_DIR, N_SSM_GROUPS, SSM_STATE)
    n_idx = jnp.arange(SSM_STATE, dtype=f32)
    return {
        'x': nrm(ks[0], (BATCH, SEQ, D_MODEL), 1.0),
        'meta_tokens': nrm(ks[1], (N_META, D_MODEL), 1.0),
        'norm_mix_g': 1.0 + nrm(ks[2], (DEPTH, D_MODEL), 0.01),
        'w_in': nrm(ks[3], (DEPTH, D_MODEL, D_IN_PROJ), D_MODEL ** -0.5),
        'conv_w': nrm(ks[4], (DEPTH, CONV_WIDTH, D_CONV), CONV_WIDTH ** -0.5),
        'conv_b': nrm(ks[5], (DEPTH, D_CONV), 0.01),
        'conv_ln_g': 1.0 + nrm(ks[6], (DEPTH, D_CONV), 0.01),
        'conv_ln_b': nrm(ks[7], (DEPTH, D_CONV), 0.01),
        'ssm_lam_re': -0.5 + nrm(ks[8], ssm_shape, 0.01),
        'ssm_lam_im': math.pi * n_idx + nrm(ks[9], ssm_shape, 0.01),
        'ssm_log_dt': jax.random.uniform(ks[10], (DEPTH, N_DIR, N_SSM_GROUPS), f32,
                                         math.log(DT_MIN), math.log(DT_MAX)),
        'ssm_b_re': nrm(ks[11], (DEPTH, N_DIR, N_SSM_GROUPS, SSM_STATE, SSM_GROUP), (2 * SSM_GROUP) ** -0.5),
        'ssm_b_im': nrm(ks[12], (DEPTH, N_DIR, N_SSM_GROUPS, SSM_STATE, SSM_GROUP), (2 * SSM_GROUP) ** -0.5),
        'ssm_c_re': nrm(ks[13], (DEPTH, N_DIR, N_SSM_GROUPS, SSM_GROUP, SSM_STATE), SSM_STATE ** -0.5),
        'ssm_c_im': nrm(ks[14], (DEPTH, N_DIR, N_SSM_GROUPS, SSM_GROUP, SSM_STATE), SSM_STATE ** -0.5),
        'ssm_d': nrm(ks[15], (DEPTH, D_SSM), 1.0),
        'ssm_glu_w': nrm(ks[16], (DEPTH, D_SSM, D_SSM), D_SSM ** -0.5),
        'ssm_glu_b': nrm(ks[17], (DEPTH, D_SSM), 0.01),
        'w_out': nrm(ks[18], (DEPTH, D_MIX, D_MODEL), D_MIX ** -0.5),
        'norm_ffn_g': 1.0 + nrm(ks[19], (DEPTH, D_MODEL), 0.01),
        'w_ff1': nrm(ks[20], (DEPTH, D_MODEL, D_FF), D_MODEL ** -0.5),
        'w_ff2': nrm(ks[21], (DEPTH, D_FF, D_MODEL), D_FF ** -0.5),
        'norm_final_g': 1.0 + nrm(ks[22], (D_MODEL,), 0.01),
    }


def reference(x, meta_tokens, norm_mix_g, w_in, conv_w, conv_b, conv_ln_g, conv_ln_b,
              ssm_lam_re, ssm_lam_im, ssm_log_dt, ssm_b_re, ssm_b_im, ssm_c_re, ssm_c_im,
              ssm_d, ssm_glu_w, ssm_glu_b, w_out, norm_ffn_g, w_ff1, w_ff2, norm_final_g):
    bsz = x.shape[0]
    meta = jnp.broadcast_to(meta_tokens[None].astype(x.dtype), (bsz, N_META, D_MODEL))
    h = jnp.concatenate([meta, x], axis=1)
    for layer in range(DEPTH):
        z = _rms_norm(h, norm_mix_g[layer])
        p = z @ w_in[layer]
        conv_v = p[..., :D_CONV]
        conv_gate = p[..., D_CONV:2 * D_CONV]
        ssm_u = p[..., 2 * D_CONV:]
        conv_out = _conv_module(conv_v, conv_gate, conv_w[layer], conv_b[layer],
                                conv_ln_g[layer], conv_ln_b[layer])
        ssm_out = _s5_mixer(ssm_u, ssm_lam_re[layer], ssm_lam_im[layer], ssm_log_dt[layer],
                            ssm_b_re[layer], ssm_b_im[layer], ssm_c_re[layer], ssm_c_im[layer],
                            ssm_d[layer], ssm_glu_w[layer], ssm_glu_b[layer])
        mixed = jnp.concatenate([conv_out, ssm_out.astype(conv_out.dtype)], axis=-1)
        h = h + mixed @ w_out[layer]
        z = _rms_norm(h, norm_ffn_g[layer])
        h = h + jnp.square(jax.nn.relu(z @ w_ff1[layer])) @ w_ff2[layer]
    h = _rms_norm(h, norm_final_g)
    return h[:, N_META:, :]
```

```python
import functools

import jax
import jax.numpy as jnp
from jax import lax
from jax.experimental import pallas as pl
from jax.experimental.pallas import tpu as pltpu

F32 = jnp.float32
BF16 = jnp.bfloat16

D_MODEL = 1024
N_META = 16
D_CONV = 512
D_SSM = 512
CONV_WIDTH = 31
SSM_GROUP = 16
N_GROUPS = D_SSM // SSM_GROUP
SSM_STATE = 64
D_FF = 4096
NORM_EPS = 1e-5

CHUNK = 32
CHUNK_LANES = CHUNK * SSM_GROUP
STATE_LANES = 4 * SSM_STATE
HALF_LANES = 2 * SSM_STATE

VMEM_LIMIT_BYTES = 56 * 1024 * 1024


def _rms(x, g):
    return x * lax.rsqrt(jnp.mean(x * x, axis=-1, keepdims=True) + NORM_EPS) * g


def _in_proj_kernel(x_ref, g_ref, w_ref, uc_ref, us_ref):
    z = _rms(x_ref[...], g_ref[...]).astype(BF16)
    p = jnp.dot(z, w_ref[...], preferred_element_type=F32)
    uc_ref[...] = p[:, :D_CONV] * jax.nn.sigmoid(p[:, D_CONV:2 * D_CONV])
    us_ref[...] = p[:, 2 * D_CONV:].astype(BF16)


def _in_proj(x2d, g, w_bf16, rows):
    n = x2d.shape[0]
    return pl.pallas_call(
        _in_proj_kernel,
        grid=(n // rows,),
        in_specs=[
            pl.BlockSpec((rows, D_MODEL), lambda i: (i, 0)),
            pl.BlockSpec((1, D_MODEL), lambda i: (0, 0)),
            pl.BlockSpec((D_MODEL, 2 * D_CONV + D_SSM), lambda i: (0, 0)),
        ],
        out_specs=[
            pl.BlockSpec((rows, D_CONV), lambda i: (i, 0)),
            pl.BlockSpec((rows, D_SSM), lambda i: (i, 0)),
        ],
        out_shape=[
            jax.ShapeDtypeStruct((n, D_CONV), F32),
            jax.ShapeDtypeStruct((n, D_SSM), BF16),
        ],
        compiler_params=pltpu.CompilerParams(
            dimension_semantics=("parallel",), vmem_limit_bytes=VMEM_LIMIT_BYTES),
        name="in_proj",
    )(x2d, g, w_bf16)


CONV_ROWS = 32


def _conv_kernel(u_ref, um_ref, w_ref, cb_ref, lg_ref, lb_ref, o_ref, pad_ref, *, seq):
    pad_ref[0:N_META, :] = um_ref[...]
    pad_ref[N_META:N_META + seq, :] = u_ref[0]
    pad_ref[N_META + seq:, :] = jnp.zeros((N_META, D_CONV), F32)

    def body(i, carry):
        t0 = pl.multiple_of(i * CONV_ROWS, CONV_ROWS)
        acc = jnp.zeros((CONV_ROWS, D_CONV), F32)
        win = pad_ref[pl.ds(t0, CONV_ROWS + 2 * N_META), :]
        for k in range(CONV_WIDTH):
            acc = acc + win[1 + k:1 + k + CONV_ROWS, :] * w_ref[k:k + 1, :]
        y = acc + cb_ref[...]
        yc = y - jnp.mean(y, axis=-1, keepdims=True)
        yn = yc * lax.rsqrt(jnp.mean(yc * yc, axis=-1, keepdims=True) + NORM_EPS)
        yn = yn * lg_ref[...] + lb_ref[...]
        o_ref[0, pl.ds(t0, CONV_ROWS), :] = (yn * jax.nn.sigmoid(yn)).astype(BF16)
        return carry

    lax.fori_loop(0, seq // CONV_ROWS, body, 0)


def _conv_module(u_conv, u_conv_meta, conv_w, conv_b, ln_g, ln_b):
    bsz, seq, _ = u_conv.shape
    vec = pl.BlockSpec((1, D_CONV), lambda b: (0, 0))
    return pl.pallas_call(
        functools.partial(_conv_kernel, seq=seq),
        grid=(bsz,),
        in_specs=[
            pl.BlockSpec((1, seq, D_CONV), lambda b: (b, 0, 0)),
            pl.BlockSpec((N_META, D_CONV), lambda b: (0, 0)),
            pl.BlockSpec((CONV_WIDTH, D_CONV), lambda b: (0, 0)),
            vec, vec, vec,
        ],
        out_specs=pl.BlockSpec((1, seq, D_CONV), lambda b: (b, 0, 0)),
        out_shape=jax.ShapeDtypeStruct((bsz, seq, D_CONV), BF16),
        scratch_shapes=[pltpu.VMEM((seq + 2 * N_META, D_CONV), F32)],
        compiler_params=pltpu.CompilerParams(
            dimension_semantics=("parallel",), vmem_limit_bytes=VMEM_LIMIT_BYTES),
        name="conv_module",
    )(u_conv, u_conv_meta, conv_w, conv_b, ln_g, ln_b)


def _zoh_kernel(lre_ref, lim_ref, ldt_ref, lbr_ref, lbi_ref, zr_ref, zi_ref):
    lre = lre_ref[...]
    lim = lim_ref[...]
    dt = jnp.exp(ldt_ref[...])
    ea = jnp.exp(lre * dt)
    lbr = ea * jnp.cos(lim * dt)
    lbi = ea * jnp.sin(lim * dt)
    nr = lbr - 1.0
    den = lre * lre + lim * lim
    lbr_ref[...] = lbr
    lbi_ref[...] = lbi
    zr_ref[...] = (nr * lre + lbi * lim) / den
    zi_ref[...] = (lbi * lre - nr * lim) / den


def _zoh(lam_re, lam_im, log_dt):
    shape = jax.ShapeDtypeStruct(lam_re.shape, F32)
    return pl.pallas_call(_zoh_kernel, out_shape=[shape] * 4, name="s5_zoh")(
        lam_re, lam_im, log_dt)


def _lanes(x, n):
    return jnp.concatenate([x] * (n // 128), axis=1)


def _cpow(br, bi, expo, nbits):
    rr = jnp.ones(expo.shape, F32)
    ri = jnp.zeros(expo.shape, F32)
    for k in range(nbits):
        bit = ((expo >> k) & 1) == 1
        nr = rr * br - ri * bi
        ni = rr * bi + ri * br
        rr = jnp.where(bit, nr, rr)
        ri = jnp.where(bit, ni, ri)
        br, bi = br * br - bi * bi, 2.0 * br * bi
    return rr, ri


def _chunk_ops_kernel(csc_ref, cb_ref, rp_ref, toep_ref, wend_ref, wout_ref, lam_ref):
    t = CHUNK
    nbits = t.bit_length()
    rows = STATE_LANES
    lbr = jnp.broadcast_to(csc_ref[0, :, 0:1], (rows, 128))
    lbi = jnp.broadcast_to(csc_ref[0, :, 1:2], (rows, 128))
    zr = jnp.broadcast_to(csc_ref[0, :, 2:3], (rows, 128))
    zi = jnp.broadcast_to(csc_ref[0, :, 3:4], (rows, 128))
    sel = (lax.broadcasted_iota(jnp.int32, (SSM_GROUP, 128), 1) % SSM_GROUP
           == lax.broadcasted_iota(jnp.int32, (SSM_GROUP, 128), 0)).astype(F32)

    def tile(i):
        return jnp.dot(cb_ref[0, i], sel, precision=lax.Precision.HIGHEST,
                       preferred_element_type=F32)

    c_same, c_swap, b_same, b_swap = tile(0), tile(1), tile(2), tile(3)
    row = lax.broadcasted_iota(jnp.int32, (rows, 128), 0)
    is_fwd = row < HALF_LANES
    is_re = (row // SSM_STATE) % 2 == 0
    sgn = jnp.where(is_re, -1.0, 1.0)
    ca = jnp.where(is_re, c_same, -c_same)
    cb = -c_swap
    bb_same = zr * b_same + sgn * zi * b_swap
    bb_swap = zr * b_swap - sgn * zi * b_same

    n1 = 2 * CHUNK_LANES
    taup = lax.broadcasted_iota(jnp.int32, (rows, n1), 1) // SSM_GROUP
    fwd1 = lax.broadcasted_iota(jnp.int32, (rows, n1), 0) < HALF_LANES
    e1 = jnp.where(fwd1, taup - (t - 1), (t - 1) - taup)
    valid = (e1 >= 0) & (taup <= 2 * t - 2)
    p1, p2 = _cpow(_lanes(lbr, n1), _lanes(lbi, n1), jnp.maximum(e1, 0), nbits)
    acat = jnp.where(valid, _lanes(ca, n1) * p1 + _lanes(cb, n1) * p2, 0.0)

    r_lbr, r_lbi, r_zr, r_zi = rp_ref[0, 0], rp_ref[0, 1], rp_ref[0, 2], rp_ref[0, 3]
    lane_r = lax.broadcasted_iota(jnp.int32, (SSM_GROUP, rows), 1)
    sgn_r = jnp.where((lane_r // SSM_STATE) % 2 == 0, -1.0, 1.0)
    b2t = r_zr * rp_ref[0, 4] + sgn_r * r_zi * rp_ref[0, 5]
    kcat = jnp.dot(b2t, acat, precision=lax.Precision.HIGHEST,
                   preferred_element_type=F32)
    for s in range(t):
        off = (t - 1 - s) * SSM_GROUP
        toep_ref[0, s * SSM_GROUP:(s + 1) * SSM_GROUP, :] = (
            kcat[:, off:off + CHUNK_LANES].astype(BF16))

    n2 = CHUNK_LANES
    tt = lax.broadcasted_iota(jnp.int32, (rows, n2), 1) // SSM_GROUP
    fwd2 = lax.broadcasted_iota(jnp.int32, (rows, n2), 0) < HALF_LANES
    lbr2, lbi2 = _lanes(lbr, n2), _lanes(lbi, n2)
    p1, p2 = _cpow(lbr2, lbi2, jnp.where(fwd2, tt + 1, t - tt), nbits)
    wout_ref[0] = (_lanes(ca, n2) * p1 + _lanes(cb, n2) * p2).astype(BF16)

    p1, p2 = _cpow(lbr2, lbi2, jnp.where(fwd2, t - 1 - tt, tt), nbits)
    wend_ref[0] = (p1 * _lanes(bb_same, n2)
                   + _lanes(sgn, n2) * p2 * _lanes(bb_swap, n2)).astype(BF16)

    ar, ai = r_lbr[0:8], r_lbi[0:8]
    for _ in range(t.bit_length() - 1):
        ar, ai = ar * ar - ai * ai, 2.0 * ar * ai
    lam_ref[0, 0:8, :] = ar
    lam_ref[0, 8:16, :] = ai
    del is_fwd


def _chunk_ops(csc, cb16, rp):
    g = csc.shape[0]
    return pl.pallas_call(
        _chunk_ops_kernel,
        grid=(g,),
        in_specs=[
            pl.BlockSpec((1, STATE_LANES, 4), lambda i: (i, 0, 0)),
            pl.BlockSpec((1, 4, STATE_LANES, SSM_GROUP), lambda i: (i, 0, 0, 0)),
            pl.BlockSpec((1, 6, SSM_GROUP, STATE_LANES), lambda i: (i, 0, 0, 0)),
        ],
        out_specs=[
            pl.BlockSpec((1, CHUNK_LANES, CHUNK_LANES), lambda i: (i, 0, 0)),
            pl.BlockSpec((1, STATE_LANES, CHUNK_LANES), lambda i: (i, 0, 0)),
            pl.BlockSpec((1, STATE_LANES, CHUNK_LANES), lambda i: (i, 0, 0)),
            pl.BlockSpec((1, 16, STATE_LANES), lambda i: (i, 0, 0)),
        ],
        out_shape=[
            jax.ShapeDtypeStruct((g, CHUNK_LANES, CHUNK_LANES), BF16),
            jax.ShapeDtypeStruct((g, STATE_LANES, CHUNK_LANES), BF16),
            jax.ShapeDtypeStruct((g, STATE_LANES, CHUNK_LANES), BF16),
            jax.ShapeDtypeStruct((g, 16, STATE_LANES), F32),
        ],
        compiler_params=pltpu.CompilerParams(
            dimension_semantics=("parallel",), vmem_limit_bytes=VMEM_LIMIT_BYTES),
        name="s5_chunk_ops",
    )(csc, cb16, rp)


def _swap_re_im(x):
    return jnp.concatenate(
        [pltpu.roll(x[:, :HALF_LANES], SSM_STATE, axis=1),
         pltpu.roll(x[:, HALF_LANES:], SSM_STATE, axis=1)], axis=1)


def _ssm_kernel(u_ref, um_ref, toep_ref, wend_ref, wout_ref, lam_ref, y_ref, *, n_chunks):
    u = u_ref[0]
    wend = wend_ref[0]
    nt = (((1,), (1,)), ((), ()))
    e = lax.dot_general(u, wend, nt, preferred_element_type=F32)
    x0 = lax.dot_general(um_ref[0], wend, nt, preferred_element_type=F32)[0:1, :]
    n_rows = e.shape[0]
    ar = lam_ref[0, 0:1, :]
    ai = lam_ref[0, 8:9, :]
    lane = lax.broadcasted_iota(jnp.int32, (1, STATE_LANES), 1)
    sgn = jnp.where((lane // SSM_STATE) % 2 == 0, -1.0, 1.0)
    is_fwd = lane < HALF_LANES
    chunk = lax.broadcasted_iota(jnp.int32, (n_rows, 1), 0) % n_chunks

    ax0 = ar * x0 + sgn * ai * _swap_re_im(x0)
    x = e + jnp.where((chunk == 0) & is_fwd, ax0, 0.0)
    sh = 1
    while sh < n_chunks:
        xf = pltpu.roll(x, sh, axis=0)
        xb = pltpu.roll(x, n_rows - sh, axis=0)
        xs = jnp.where(is_fwd, jnp.where(chunk >= sh, xf, 0.0),
                       jnp.where(chunk < n_chunks - sh, xb, 0.0))
        x = x + ar * xs + (sgn * ai) * _swap_re_im(xs)
        ar, ai = ar * ar - ai * ai, 2.0 * ar * ai
        sh *= 2
    xf = pltpu.roll(x, 1, axis=0)
    xb = pltpu.roll(x, n_rows - 1, axis=0)
    xin = jnp.where(is_fwd, jnp.where(chunk >= 1, xf, x0),
                    jnp.where(chunk < n_chunks - 1, xb, 0.0))
    y = jnp.dot(u, toep_ref[0], preferred_element_type=F32)
    y = y + jnp.dot(xin.astype(BF16), wout_ref[0], preferred_element_type=F32)
    y_ref[0] = y.astype(BF16)


def _ssm(u_nat, u_meta, toep, wend, wout, lam, n_chunks):
    g, n_rows, _ = u_nat.shape
    return pl.pallas_call(
        functools.partial(_ssm_kernel, n_chunks=n_chunks),
        grid=(g,),
        in_specs=[
            pl.BlockSpec((1, n_rows, CHUNK_LANES), lambda i: (i, 0, 0)),
            pl.BlockSpec((1, 16, CHUNK_LANES), lambda i: (i, 0, 0)),
            pl.BlockSpec((1, CHUNK_LANES, CHUNK_LANES), lambda i: (i, 0, 0)),
            pl.BlockSpec((1, STATE_LANES, CHUNK_LANES), lambda i: (i, 0, 0)),
            pl.BlockSpec((1, STATE_LANES, CHUNK_LANES), lambda i: (i, 0, 0)),
            pl.BlockSpec((1, 16, STATE_LANES), lambda i: (i, 0, 0)),
        ],
        out_specs=pl.BlockSpec((1, n_rows, CHUNK_LANES), lambda i: (i, 0, 0)),
        out_shape=jax.ShapeDtypeStruct((g, n_rows, CHUNK_LANES), BF16),
        compiler_params=pltpu.CompilerParams(
            dimension_semantics=("parallel",), vmem_limit_bytes=VMEM_LIMIT_BYTES),
        name="s5_mixer",
    )(u_nat, u_meta, toep, wend, wout, lam)


FF_CHUNK = 1024


def _mix_ffn_kernel(x_ref, co_ref, y_ref, us_ref, d_ref, gw_ref, gb_ref, wo_ref,
                    gf_ref, w1_ref, w2_ref, gl_ref, o_ref):
    y = y_ref[...].astype(F32) + d_ref[...] * us_ref[...].astype(F32)
    ge = jax.nn.gelu(y)
    gate = jnp.dot(ge.astype(BF16), gw_ref[...], preferred_element_type=F32) + gb_ref[...]
    s5 = ge * jax.nn.sigmoid(gate)
    mixed = jnp.concatenate([co_ref[...], s5.astype(BF16)], axis=1)
    h = x_ref[...] + jnp.dot(mixed, wo_ref[...], preferred_element_type=F32)
    z = _rms(h, gf_ref[...]).astype(BF16)
    for j in range(D_FF // FF_CHUNK):
        a = jnp.dot(z, w1_ref[:, j * FF_CHUNK:(j + 1) * FF_CHUNK],
                    preferred_element_type=F32)
        a = jnp.square(jnp.maximum(a, 0.0)).astype(BF16)
        h = h + jnp.dot(a, w2_ref[j * FF_CHUNK:(j + 1) * FF_CHUNK, :],
                        preferred_element_type=F32)
    o_ref[...] = _rms(h, gl_ref[...])


def _mix_ffn(x2d, conv_out, y_ssm, u_ssm, d_skip, glu_w, glu_b, w_out, g_ffn, w1, w2,
             g_final, rows):
    n = x2d.shape[0]

    def const(shape):
        return pl.BlockSpec(shape, lambda i: (0, 0), pipeline_mode=pl.Buffered(1))

    return pl.pallas_call(
        _mix_ffn_kernel,
        grid=(n // rows,),
        in_specs=[
            pl.BlockSpec((rows, D_MODEL), lambda i: (i, 0)),
            pl.BlockSpec((rows, D_CONV), lambda i: (i, 0)),
            pl.BlockSpec((rows, D_SSM), lambda i: (i, 0)),
            pl.BlockSpec((rows, D_SSM), lambda i: (i, 0)),
            const((1, D_SSM)),
            const((D_SSM, D_SSM)),
            const((1, D_SSM)),
            const((D_MODEL, D_MODEL)),
            const((1, D_MODEL)),
            const((D_MODEL, D_FF)),
            const((D_FF, D_MODEL)),
            const((1, D_MODEL)),
        ],
        out_specs=pl.BlockSpec((rows, D_MODEL), lambda i: (i, 0)),
        out_shape=jax.ShapeDtypeStruct((n, D_MODEL), F32),
        compiler_params=pltpu.CompilerParams(
            dimension_semantics=("parallel",), vmem_limit_bytes=VMEM_LIMIT_BYTES),
        name="mix_ffn",
    )(x2d, conv_out, y_ssm, u_ssm, d_skip, glu_w, glu_b, w_out, g_ffn, w1, w2, g_final)


def _s5_operator_inputs(lbr, lbi, zr, zi, b_re, b_im, c_re, c_im):
    g, p, h = N_GROUPS, SSM_STATE, SSM_GROUP

    def col_scalar(a):
        a = jnp.transpose(a, (1, 0, 2))
        return jnp.broadcast_to(a[:, :, None, :], (g, 2, 2, p)).reshape(g, 4 * p)

    sc = [col_scalar(a) for a in (lbr, lbi, zr, zi)]
    csc = jnp.stack(sc, axis=-1)

    def col_pair(first, second, perm):
        first = jnp.transpose(first, perm)
        second = jnp.transpose(second, perm)
        return jnp.stack([first, second], axis=2).reshape(g, 4 * p, h)

    c_perm = (1, 0, 3, 2)
    b_perm = (1, 0, 2, 3)
    cb16 = jnp.stack([col_pair(c_re, c_im, c_perm), col_pair(c_im, c_re, c_perm),
                      col_pair(b_re, b_im, b_perm), col_pair(b_im, b_re, b_perm)], axis=1)

    def row_pair(first, second):
        both = jnp.stack([first, second], axis=2)
        return jnp.transpose(both, (1, 4, 0, 2, 3)).reshape(g, h, 4 * p)

    row_sc = [jnp.broadcast_to(a[:, None, :], (g, h, 4 * p)) for a in sc]
    rp = jnp.stack(row_sc + [row_pair(b_re, b_im), row_pair(b_im, b_re)], axis=1)
    return csc, cb16, rp


def kernel(x, meta_tokens, norm_mix_g, w_in, conv_w, conv_b, conv_ln_g, conv_ln_b,
           ssm_lam_re, ssm_lam_im, ssm_log_dt, ssm_b_re, ssm_b_im, ssm_c_re, ssm_c_im,
           ssm_d, ssm_glu_w, ssm_glu_b, w_out, norm_ffn_g, w_ff1, w_ff2, norm_final_g):
    assert w_in.shape[0] == 1, "single-layer block"
    bsz, seq, _ = x.shape
    n_chunks = seq // CHUNK
    g, h = N_GROUPS, SSM_GROUP

    x2d = x.reshape(bsz * seq, D_MODEL)
    g_mix = norm_mix_g[0][None, :]
    w_in_b = w_in[0].astype(BF16)
    u_conv, u_ssm = _in_proj(x2d, g_mix, w_in_b, rows=512)
    u_conv_m, u_ssm_m = _in_proj(meta_tokens, g_mix, w_in_b, rows=N_META)

    conv_out = _conv_module(u_conv.reshape(bsz, seq, D_CONV), u_conv_m, conv_w[0],
                            conv_b[0][None, :], conv_ln_g[0][None, :], conv_ln_b[0][None, :])

    ldt = jnp.broadcast_to(ssm_log_dt[0][..., None], ssm_lam_re[0].shape)
    flat = lambda a: a.reshape(2 * g, SSM_STATE)
    lbr, lbi, zr, zi = [a.reshape(2, g, SSM_STATE) for a in
                        _zoh(flat(ssm_lam_re[0]), flat(ssm_lam_im[0]), flat(ldt))]
    csc, cb16, rp = _s5_operator_inputs(lbr, lbi, zr, zi, ssm_b_re[0], ssm_b_im[0],
                                        ssm_c_re[0], ssm_c_im[0])
    toep, wend, wout, lam = _chunk_ops(csc, cb16, rp)

    u_nat = u_ssm.reshape(bsz, n_chunks, CHUNK, g, h)
    u_nat = jnp.transpose(u_nat, (3, 0, 1, 2, 4)).reshape(g, bsz * n_chunks, CHUNK_LANES)
    u_meta = jnp.transpose(u_ssm_m.reshape(N_META, g, h), (1, 0, 2)).reshape(g, 1, N_META * h)
    u_meta = jnp.pad(u_meta, ((0, 0), (0, 15), (CHUNK_LANES - N_META * h, 0)))
    y_nat = _ssm(u_nat, u_meta, toep, wend, wout, lam, n_chunks)
    y_ssm = jnp.transpose(y_nat.reshape(g, bsz, n_chunks, CHUNK, h), (1, 2, 3, 0, 4))
    y_ssm = y_ssm.reshape(bsz * seq, D_SSM)

    out = _mix_ffn(x2d, conv_out.reshape(bsz * seq, D_CONV), y_ssm, u_ssm,
                   ssm_d[0][None, :], ssm_glu_w[0].astype(BF16), ssm_glu_b[0][None, :],
                   w_out[0].astype(BF16), norm_ffn_g[0][None, :], w_ff1[0].astype(BF16),
                   w_ff2[0].astype(BF16), norm_final_g[None, :], rows=512)
    return out.reshape(bsz, seq, D_MODEL)
```

```python
import functools

import jax
import jax.numpy as jnp
from jax import lax
from jax.experimental import pallas as pl
from jax.experimental.pallas import tpu as pltpu

F32 = jnp.float32
BF16 = jnp.bfloat16

D_MODEL = 1024
N_META = 16
D_CONV = 512
D_SSM = 512
CONV_WIDTH = 31
CONV_PAD = CONV_WIDTH // 2
SSM_GROUP = 16
N_GROUPS = D_SSM // SSM_GROUP
SSM_STATE = 64
D_FF = 4096
NORM_EPS = 1e-5
LANES = 128

CHUNK = 32
CHUNK_LANES = CHUNK * SSM_GROUP
STATE_ROWS = 4 * SSM_STATE
DIR_ROWS = 2 * SSM_STATE
STEPS_PER_VREG = LANES // SSM_GROUP

VMEM_LIMIT_BYTES = 56 * 1024 * 1024


def _rms(x, g):
    return x * lax.rsqrt(jnp.mean(x * x, axis=-1, keepdims=True) + NORM_EPS) * g


def _params(n_axes=1):
    return pltpu.CompilerParams(dimension_semantics=("parallel",) * n_axes,
                                vmem_limit_bytes=VMEM_LIMIT_BYTES)


def _in_proj_step_kernel(x_ref, g_ref, wc_ref, wst_ref, uc_ref, ut_ref):
    rows = x_ref.shape[0] * x_ref.shape[1]
    z = _rms(x_ref[...].reshape(rows, D_MODEL), g_ref[...]).astype(BF16)
    p = jnp.dot(z, wc_ref[...], preferred_element_type=F32)
    uc_ref[0] = p[:, :D_CONV] * jax.nn.sigmoid(p[:, D_CONV:])
    ut = lax.dot_general(wst_ref[...], z, (((1,), (1,)), ((), ())),
                         preferred_element_type=F32)
    ut_ref[...] = ut.astype(BF16).reshape(N_GROUPS, SSM_GROUP, rows)


def _in_proj_step(x3, g, w_conv, w_ssm_t):
    bsz, n_chunks, _ = x3.shape
    rows = bsz * n_chunks
    return pl.pallas_call(
        _in_proj_step_kernel,
        grid=(CHUNK,),
        in_specs=[
            pl.BlockSpec((bsz, n_chunks, D_MODEL), lambda s: (0, 0, s)),
            pl.BlockSpec((1, D_MODEL), lambda s: (0, 0)),
            pl.BlockSpec((D_MODEL, 2 * D_CONV), lambda s: (0, 0)),
            pl.BlockSpec((D_SSM, D_MODEL), lambda s: (0, 0)),
        ],
        out_specs=[
            pl.BlockSpec((1, rows, D_CONV), lambda s: (s, 0, 0)),
            pl.BlockSpec((N_GROUPS, SSM_GROUP, rows), lambda s: (0, s, 0)),
        ],
        out_shape=[
            jax.ShapeDtypeStruct((CHUNK, rows, D_CONV), F32),
            jax.ShapeDtypeStruct((N_GROUPS, CHUNK_LANES, rows), BF16),
        ],
        compiler_params=_params(),
        name="in_proj",
    )(x3, g, w_conv, w_ssm_t)


def _in_proj_meta_kernel(x_ref, g_ref, wc_ref, ws_ref, uc_ref, us_ref):
    z = _rms(x_ref[...], g_ref[...]).astype(BF16)
    p = jnp.dot(z, wc_ref[...], preferred_element_type=F32)
    uc_ref[...] = p[:, :D_CONV] * jax.nn.sigmoid(p[:, D_CONV:])
    us_ref[...] = jnp.dot(z, ws_ref[...], preferred_element_type=F32).astype(BF16)


def _in_proj_meta(meta, g, w_conv, w_ssm):
    return pl.pallas_call(
        _in_proj_meta_kernel,
        out_shape=[jax.ShapeDtypeStruct((N_META, D_CONV), F32),
                   jax.ShapeDtypeStruct((N_META, D_SSM), BF16)],
        name="in_proj_meta",
    )(meta, g, w_conv, w_ssm)


def _conv_kernel(u_ref, um_ref, w_ref, cb_ref, lg_ref, lb_ref, o_ref, cat_ref, acc_ref):
    n_steps, n_chunks, _ = u_ref.shape
    cat_ref[CONV_PAD:CONV_PAD + n_steps] = u_ref[...]
    chunk = lax.broadcasted_iota(jnp.int32, (n_chunks, 1), 0)
    for i in range(CONV_PAD):
        s_prev = n_steps - CONV_PAD + i
        prev = pltpu.roll(u_ref[s_prev], 1, axis=0)
        meta_row = um_ref[N_META - CONV_PAD + i:N_META - CONV_PAD + i + 1, :]
        cat_ref[i] = jnp.where(chunk == 0, meta_row, prev)
        nxt = pltpu.roll(u_ref[i], n_chunks - 1, axis=0)
        cat_ref[n_steps + CONV_PAD + i] = jnp.where(chunk == n_chunks - 1, 0.0, nxt)

    def body(s, carry):
        for c in range(D_CONV // LANES):
            cols = slice(c * LANES, (c + 1) * LANES)
            acc = jnp.zeros((n_chunks, LANES), F32)
            for k in range(CONV_WIDTH):
                acc = acc + cat_ref[s + k, :, cols] * w_ref[k:k + 1, cols]
            acc_ref[:, cols] = acc
        y = acc_ref[...] + cb_ref[...]
        yc = y - jnp.mean(y, axis=-1, keepdims=True)
        yn = yc * lax.rsqrt(jnp.mean(yc * yc, axis=-1, keepdims=True) + NORM_EPS)
        yn = yn * lg_ref[...] + lb_ref[...]
        o_ref[s] = (yn * jax.nn.sigmoid(yn)).astype(BF16)
        return carry

    lax.fori_loop(0, n_steps, body, 0)


def _conv_module(u_conv, u_conv_meta, conv_w, conv_b, ln_g, ln_b, bsz):
    n_steps, rows, _ = u_conv.shape
    n_chunks = rows // bsz
    vec = pl.BlockSpec((1, D_CONV), lambda b: (0, 0))
    blk = pl.BlockSpec((n_steps, n_chunks, D_CONV), lambda b: (0, b, 0))
    return pl.pallas_call(
        _conv_kernel,
        grid=(bsz,),
        in_specs=[
            blk,
            pl.BlockSpec((N_META, D_CONV), lambda b: (0, 0)),
            pl.BlockSpec((CONV_WIDTH, D_CONV), lambda b: (0, 0)),
            vec, vec, vec,
        ],
        out_specs=blk,
        out_shape=jax.ShapeDtypeStruct((n_steps, rows, D_CONV), BF16),
        scratch_shapes=[pltpu.VMEM((n_steps + 2 * CONV_PAD, n_chunks, D_CONV), F32),
                        pltpu.VMEM((n_chunks, D_CONV), F32)],
        compiler_params=_params(),
        name="conv_module",
    )(u_conv, u_conv_meta, conv_w, conv_b, ln_g, ln_b)


def _zoh_kernel(lre_ref, lim_ref, ldt_ref, lbr_ref, lbi_ref, zr_ref, zi_ref):
    lre = lre_ref[...]
    lim = lim_ref[...]
    dt = jnp.exp(ldt_ref[...])
    ea = jnp.exp(lre * dt)
    lbr = ea * jnp.cos(lim * dt)
    lbi = ea * jnp.sin(lim * dt)
    nr = lbr - 1.0
    den = lre * lre + lim * lim
    lbr_ref[...] = lbr
    lbi_ref[...] = lbi
    zr_ref[...] = (nr * lre + lbi * lim) / den
    zi_ref[...] = (lbi * lre - nr * lim) / den


def _zoh(lam_re, lam_im, log_dt):
    shape = jax.ShapeDtypeStruct(lam_re.shape, F32)
    return pl.pallas_call(_zoh_kernel, out_shape=[shape] * 4, name="s5_zoh")(
        lam_re, lam_im, log_dt)


def _lanes(x, n):
    return jnp.concatenate([x] * (n // LANES), axis=1)


def _cmul(ar, ai, br, bi):
    return ar * br - ai * bi, ar * bi + ai * br


def _cpow(br, bi, expo, nbits):
    rr = jnp.ones(expo.shape, F32)
    ri = jnp.zeros(expo.shape, F32)
    for k in range(nbits):
        bit = ((expo >> k) & 1) == 1
        nr, ni = _cmul(rr, ri, br, bi)
        rr = jnp.where(bit, nr, rr)
        ri = jnp.where(bit, ni, ri)
        br, bi = _cmul(br, bi, br, bi)
    return rr, ri


def _chunk_ops_kernel(csc_ref, cb_ref, cr_ref, toep_ref, wend_ref, wout_ref, lam_ref):
    t = CHUNK
    rows = STATE_ROWS
    n_cols = CHUNK_LANES // LANES
    lam = (jnp.broadcast_to(csc_ref[0, :, 0:1], (rows, LANES)),
           jnp.broadcast_to(csc_ref[0, :, 1:2], (rows, LANES)))
    zr = jnp.broadcast_to(csc_ref[0, :, 2:3], (rows, LANES))
    zi = jnp.broadcast_to(csc_ref[0, :, 3:4], (rows, LANES))
    sel = (lax.broadcasted_iota(jnp.int32, (SSM_GROUP, LANES), 1) % SSM_GROUP
           == lax.broadcasted_iota(jnp.int32, (SSM_GROUP, LANES), 0)).astype(F32)

    def tile(i):
        return jnp.dot(cb_ref[0, i], sel, precision=lax.Precision.HIGHEST,
                       preferred_element_type=F32)

    c_same, c_swap, b_same, b_swap = tile(0), tile(1), tile(2), tile(3)
    row = lax.broadcasted_iota(jnp.int32, (rows, LANES), 0)
    is_re = (row // SSM_STATE) % 2 == 0
    sgn = jnp.where(is_re, -1.0, 1.0)
    ca = jnp.where(is_re, c_same, -c_same)
    cb = -c_swap
    bb_same = zr * b_same + sgn * zi * b_swap
    bb_swap = zr * b_swap - sgn * zi * b_same

    pows = {1: lam}
    k = 1
    while k < t:
        pows[2 * k] = _cmul(*pows[k], *pows[k])
        k *= 2
    step_bits = STEPS_PER_VREG.bit_length() - 1

    def col_pow(e):
        out = None
        for bit, val in pows.items():
            if e & bit:
                out = val if out is None else _cmul(*out, *val)
        return out

    i8 = lax.broadcasted_iota(jnp.int32, (rows, LANES), 1) // SSM_GROUP
    asc0 = _cpow(*lam, i8, step_bits)
    desc0 = _cpow(*lam, STEPS_PER_VREG - 1 - i8, step_bits)

    def table(base, exps):
        cols = [base if e == 0 else _cmul(*base, *col_pow(e)) for e in exps]
        return (jnp.concatenate([c[0] for c in cols], axis=1),
                jnp.concatenate([c[1] for c in cols], axis=1))

    asc = table(asc0, [STEPS_PER_VREG * q for q in range(n_cols)])
    desc = table(desc0, [STEPS_PER_VREG * (n_cols - 1 - q) for q in range(n_cols)])
    lam_w = (_lanes(lam[0], CHUNK_LANES), _lanes(lam[1], CHUNK_LANES))
    asc1 = _cmul(*asc, *lam_w)
    desc1 = _cmul(*desc, *lam_w)

    fwd = slice(0, DIR_ROWS)
    bwd = slice(DIR_ROWS, rows)

    def times_b(p, rs):
        return (p[0][rs] * _lanes(bb_same[rs], CHUNK_LANES)
                + _lanes(sgn[rs], CHUNK_LANES) * p[1][rs] * _lanes(bb_swap[rs], CHUNK_LANES))

    def times_c(p, rs):
        return (_lanes(ca[rs], CHUNK_LANES) * p[0][rs]
                + _lanes(cb[rs], CHUNK_LANES) * p[1][rs])

    wend_f = times_b(desc, fwd)
    wend_b = times_b(asc, bwd)
    wend_ref[0, fwd, :] = wend_f.astype(BF16)
    wend_ref[0, bwd, :] = wend_b.astype(BF16)

    lane_w = lax.broadcasted_iota(jnp.int32, (DIR_ROWS, CHUNK_LANES), 1)
    lag0_b = jnp.where(lane_w >= CHUNK_LANES - SSM_GROUP,
                       _lanes(bb_same[bwd], CHUNK_LANES), 0.0)
    bcat = jnp.concatenate(
        [jnp.concatenate([wend_f, jnp.zeros_like(wend_f)], axis=1),
         jnp.concatenate([lag0_b, times_b(asc1, bwd)], axis=1)], axis=0)
    lane_r = lax.broadcasted_iota(jnp.int32, (SSM_GROUP, rows), 1)
    c2 = jnp.where((lane_r // SSM_STATE) % 2 == 0, cr_ref[0], -cr_ref[0])
    kk = jnp.dot(c2, bcat, precision=lax.Precision.HIGHEST,
                 preferred_element_type=F32)
    for tt in range(t):
        off = (t - 1 - tt) * SSM_GROUP
        toep_ref[0, tt * SSM_GROUP:(tt + 1) * SSM_GROUP, :] = (
            kk[:, off:off + CHUNK_LANES].astype(BF16))

    wout = jnp.concatenate([times_c(asc1, fwd), times_c(desc1, bwd)], axis=0)
    wout_ref[0] = wout.T.astype(BF16)

    lam_ref[0, 0] = pows[t][0]
    lam_ref[0, 1] = pows[t][1]


def _chunk_ops(csc, cb16, c_row):
    g = csc.shape[0]
    return pl.pallas_call(
        _chunk_ops_kernel,
        grid=(g,),
        in_specs=[
            pl.BlockSpec((1, STATE_ROWS, 4), lambda i: (i, 0, 0)),
            pl.BlockSpec((1, 4, STATE_ROWS, SSM_GROUP), lambda i: (i, 0, 0, 0)),
            pl.BlockSpec((1, SSM_GROUP, STATE_ROWS), lambda i: (i, 0, 0)),
        ],
        out_specs=[
            pl.BlockSpec((1, CHUNK_LANES, CHUNK_LANES), lambda i: (i, 0, 0)),
            pl.BlockSpec((1, STATE_ROWS, CHUNK_LANES), lambda i: (i, 0, 0)),
            pl.BlockSpec((1, CHUNK_LANES, STATE_ROWS), lambda i: (i, 0, 0)),
            pl.BlockSpec((1, 2, STATE_ROWS, LANES), lambda i: (i, 0, 0, 0)),
        ],
        out_shape=[
            jax.ShapeDtypeStruct((g, CHUNK_LANES, CHUNK_LANES), BF16),
            jax.ShapeDtypeStruct((g, STATE_ROWS, CHUNK_LANES), BF16),
            jax.ShapeDtypeStruct((g, CHUNK_LANES, STATE_ROWS), BF16),
            jax.ShapeDtypeStruct((g, 2, STATE_ROWS, LANES), F32),
        ],
        compiler_params=_params(),
        name="s5_chunk_ops",
    )(csc, cb16, c_row)


def _chunk_scan(er, ei, ar, ai, x0, chunk, n_chunks, forward):
    width = er.shape[1]
    xr, xi = er, ei
    if x0 is not None:
        fr, fi = _cmul(ar, ai, x0[0], x0[1])
        xr = xr + jnp.where(chunk == 0, fr, 0.0)
        xi = xi + jnp.where(chunk == 0, fi, 0.0)

    def shifted(v, sh):
        if forward:
            return jnp.where(chunk >= sh, pltpu.roll(v, sh, axis=1), 0.0)
        return jnp.where(chunk < n_chunks - sh, pltpu.roll(v, width - sh, axis=1), 0.0)

    sh = 1
    while sh < n_chunks:
        sr, si = shifted(xr, sh), shifted(xi, sh)
        xr, xi = xr + ar * sr - ai * si, xi + ar * si + ai * sr
        ar, ai = _cmul(ar, ai, ar, ai)
        sh *= 2
    inr, ini = shifted(xr, 1), shifted(xi, 1)
    if x0 is not None:
        inr = jnp.where(chunk == 0, x0[0], inr)
        ini = jnp.where(chunk == 0, x0[1], ini)
    return inr, ini


def _ssm_kernel(u_ref, um_ref, toep_ref, wend_ref, wout_ref, lam_ref, y_ref, *, n_chunks):
    u = u_ref[0]
    wend = wend_ref[0]
    width = u.shape[1]
    e = jnp.dot(wend, u, preferred_element_type=F32)
    x0 = jnp.dot(wend, um_ref[0], preferred_element_type=F32)[:, 0:1]
    chunk = lax.broadcasted_iota(jnp.int32, (1, width), 1) % n_chunks
    p = SSM_STATE
    ar = _lanes(lam_ref[0, 0], width)
    ai = _lanes(lam_ref[0, 1], width)
    f_in = _chunk_scan(e[0:p], e[p:2 * p], ar[0:p], ai[0:p], (x0[0:p], x0[p:2 * p]),
                       chunk, n_chunks, True)
    b_in = _chunk_scan(e[2 * p:3 * p], e[3 * p:], ar[2 * p:3 * p], ai[2 * p:3 * p], None,
                       chunk, n_chunks, False)
    xin = jnp.concatenate([f_in[0], f_in[1], b_in[0], b_in[1]], axis=0).astype(BF16)
    y = jnp.dot(toep_ref[0], u, preferred_element_type=F32)
    y = y + jnp.dot(wout_ref[0], xin, preferred_element_type=F32)
    y_ref[0] = y.astype(BF16)


def _ssm(u_col, u_meta, toep, wend, wout, lam, n_chunks):
    g, _, width = u_col.shape
    return pl.pallas_call(
        functools.partial(_ssm_kernel, n_chunks=n_chunks),
        grid=(g,),
        in_specs=[
            pl.BlockSpec((1, CHUNK_LANES, width), lambda i: (i, 0, 0)),
            pl.BlockSpec((1, CHUNK_LANES, LANES), lambda i: (i, 0, 0)),
            pl.BlockSpec((1, CHUNK_LANES, CHUNK_LANES), lambda i: (i, 0, 0)),
            pl.BlockSpec((1, STATE_ROWS, CHUNK_LANES), lambda i: (i, 0, 0)),
            pl.BlockSpec((1, CHUNK_LANES, STATE_ROWS), lambda i: (i, 0, 0)),
            pl.BlockSpec((1, 2, STATE_ROWS, LANES), lambda i: (i, 0, 0, 0)),
        ],
        out_specs=pl.BlockSpec((1, CHUNK_LANES, width), lambda i: (i, 0, 0)),
        out_shape=jax.ShapeDtypeStruct((g, CHUNK_LANES, width), BF16),
        compiler_params=_params(),
        name="s5_mixer",
    )(u_col, u_meta, toep, wend, wout, lam)


FF_CHUNK = 1024


def _mix_ffn_kernel(x_ref, co_ref, y_ref, ut_ref, d_ref, gwt_ref, gb_ref, wo_ref,
                    gf_ref, w1_ref, w2_ref, gl_ref, o_ref):
    rows = x_ref.shape[0] * x_ref.shape[1]
    yt = (y_ref[...].reshape(D_SSM, rows).astype(F32)
          + d_ref[...] * ut_ref[...].reshape(D_SSM, rows).astype(F32))
    ge = jax.nn.gelu(yt)
    gate = jnp.dot(gwt_ref[...], ge.astype(BF16), preferred_element_type=F32) + gb_ref[...]
    s5 = (ge * jax.nn.sigmoid(gate)).T.astype(BF16)
    mixed = jnp.concatenate([co_ref[0], s5], axis=1)
    h = x_ref[...].reshape(rows, D_MODEL) + jnp.dot(mixed, wo_ref[...],
                                                    preferred_element_type=F32)
    z = _rms(h, gf_ref[...]).astype(BF16)
    for j in range(D_FF // FF_CHUNK):
        a = jnp.dot(z, w1_ref[:, j * FF_CHUNK:(j + 1) * FF_CHUNK],
                    preferred_element_type=F32)
        a = jnp.square(jnp.maximum(a, 0.0)).astype(BF16)
        h = h + jnp.dot(a, w2_ref[j * FF_CHUNK:(j + 1) * FF_CHUNK, :],
                        preferred_element_type=F32)
    o_ref[...] = _rms(h, gl_ref[...]).reshape(o_ref.shape)


def _mix_ffn(x3, conv_out, y_col, u_col, d_col, glu_wt, glu_b_col, w_out, g_ffn, w1, w2,
             g_final):
    bsz, n_chunks, _ = x3.shape
    rows = bsz * n_chunks

    def const(shape):
        return pl.BlockSpec(shape, lambda s: (0, 0), pipeline_mode=pl.Buffered(1))

    x_blk = pl.BlockSpec((bsz, n_chunks, D_MODEL), lambda s: (0, 0, s))
    col_blk = pl.BlockSpec((N_GROUPS, SSM_GROUP, rows), lambda s: (0, s, 0))
    return pl.pallas_call(
        _mix_ffn_kernel,
        grid=(CHUNK,),
        in_specs=[
            x_blk,
            pl.BlockSpec((1, rows, D_CONV), lambda s: (s, 0, 0)),
            col_blk,
            col_blk,
            const((D_SSM, 1)),
            const((D_SSM, D_SSM)),
            const((D_SSM, 1)),
            const((D_MODEL, D_MODEL)),
            const((1, D_MODEL)),
            const((D_MODEL, D_FF)),
            const((D_FF, D_MODEL)),
            const((1, D_MODEL)),
        ],
        out_specs=x_blk,
        out_shape=jax.ShapeDtypeStruct(x3.shape, F32),
        compiler_params=_params(),
        name="mix_ffn",
    )(x3, conv_out, y_col, u_col, d_col, glu_wt, glu_b_col, w_out, g_ffn, w1, w2, g_final)


def _s5_operator_inputs(lbr, lbi, zr, zi, b_re, b_im, c_re, c_im):
    g, p, h = N_GROUPS, SSM_STATE, SSM_GROUP

    def col_scalar(a):
        a = jnp.transpose(a, (1, 0, 2))
        return jnp.broadcast_to(a[:, :, None, :], (g, 2, 2, p)).reshape(g, 4 * p)

    csc = jnp.stack([col_scalar(a) for a in (lbr, lbi, zr, zi)], axis=-1)

    def col_pair(first, second, perm):
        first = jnp.transpose(first, perm)
        second = jnp.transpose(second, perm)
        return jnp.stack([first, second], axis=2).reshape(g, 4 * p, h)

    c_perm = (1, 0, 3, 2)
    b_perm = (1, 0, 2, 3)
    cb16 = jnp.stack([col_pair(c_re, c_im, c_perm), col_pair(c_im, c_re, c_perm),
                      col_pair(b_re, b_im, b_perm), col_pair(b_im, b_re, b_perm)], axis=1)
    c_row = jnp.transpose(jnp.stack([c_re, c_im], axis=2), (1, 3, 0, 2, 4)).reshape(g, h, 4 * p)
    return csc, cb16, c_row


def kernel(x, meta_tokens, norm_mix_g, w_in, conv_w, conv_b, conv_ln_g, conv_ln_b,
           ssm_lam_re, ssm_lam_im, ssm_log_dt, ssm_b_re, ssm_b_im, ssm_c_re, ssm_c_im,
           ssm_d, ssm_glu_w, ssm_glu_b, w_out, norm_ffn_g, w_ff1, w_ff2, norm_final_g):
    assert w_in.shape[0] == 1, "single-layer block"
    bsz, seq, _ = x.shape
    assert seq % CHUNK == 0 and CHUNK >= N_META
    n_chunks = seq // CHUNK
    g, h = N_GROUPS, SSM_GROUP

    x3 = x.reshape(bsz, n_chunks, CHUNK * D_MODEL)
    g_mix = norm_mix_g[0][None, :]
    w_in_b = w_in[0].astype(BF16)
    w_conv, w_ssm = w_in_b[:, :2 * D_CONV], w_in_b[:, 2 * D_CONV:]
    u_conv, u_col = _in_proj_step(x3, g_mix, w_conv, w_ssm.T)
    u_conv_m, u_ssm_m = _in_proj_meta(meta_tokens, g_mix, w_conv, w_ssm)

    conv_out = _conv_module(u_conv, u_conv_m, conv_w[0], conv_b[0][None, :],
                            conv_ln_g[0][None, :], conv_ln_b[0][None, :], bsz)

    ldt = jnp.broadcast_to(ssm_log_dt[0][..., None], ssm_lam_re[0].shape)
    flat = lambda a: a.reshape(2 * g, SSM_STATE)
    lbr, lbi, zr, zi = [a.reshape(2, g, SSM_STATE) for a in
                        _zoh(flat(ssm_lam_re[0]), flat(ssm_lam_im[0]), flat(ldt))]
    csc, cb16, c_row = _s5_operator_inputs(lbr, lbi, zr, zi, ssm_b_re[0], ssm_b_im[0],
                                           ssm_c_re[0], ssm_c_im[0])
    toep, wend, wout, lam = _chunk_ops(csc, cb16, c_row)

    u_meta = jnp.transpose(u_ssm_m.reshape(N_META, g, h), (1, 0, 2)).reshape(g, N_META * h, 1)
    u_meta = jnp.pad(u_meta, ((0, 0), (CHUNK_LANES - N_META * h, 0), (0, LANES - 1)))
    y_col = _ssm(u_col, u_meta, toep, wend, wout, lam, n_chunks)

    out = _mix_ffn(x3, conv_out, y_col, u_col, ssm_d[0][:, None],
                   ssm_glu_w[0].T.astype(BF16), ssm_glu_b[0][:, None],
                   w_out[0].astype(BF16), norm_ffn_g[0][None, :], w_ff1[0].astype(BF16),
                   w_ff2[0].astype(BF16), norm_final_g[None, :])
    return out.reshape(bsz, seq, D_MODEL)
```

```python
import functools

import jax
import jax.numpy as jnp
from jax import lax
from jax.experimental import pallas as pl
from jax.experimental.pallas import tpu as pltpu

F32 = jnp.float32
BF16 = jnp.bfloat16

D_MODEL = 1024
N_META = 16
D_CONV = 512
D_SSM = 512
CONV_WIDTH = 31
CONV_PAD = CONV_WIDTH // 2
SSM_GROUP = 16
N_GROUPS = D_SSM // SSM_GROUP
SSM_STATE = 64
D_FF = 4096
NORM_EPS = 1e-5
LANES = 128
SUBLANES = 8

CHUNK = 32
CHUNK_LANES = CHUNK * SSM_GROUP
STATE_ROWS = 4 * SSM_STATE
DIR_ROWS = 2 * SSM_STATE
STEPS_PER_VREG = LANES // SSM_GROUP

VMEM_LIMIT_BYTES = 56 * 1024 * 1024


def _rms(x, g):
    return x * lax.rsqrt(jnp.mean(x * x, axis=-1, keepdims=True) + NORM_EPS) * g


def _params(n_axes=1):
    return pltpu.CompilerParams(dimension_semantics=("parallel",) * n_axes,
                                vmem_limit_bytes=VMEM_LIMIT_BYTES)


def _in_proj_step_kernel(x_ref, g_ref, wc_ref, wst_ref, uc_ref, ut_ref):
    n_chunks = x_ref.shape[1]
    x = jnp.concatenate([x_ref[0, :, r, :] for r in range(SUBLANES)], axis=0)
    z = _rms(x, g_ref[...]).astype(BF16)
    p = jnp.dot(z, wc_ref[...], preferred_element_type=F32)
    uc = p[:, :D_CONV] * jax.nn.sigmoid(p[:, D_CONV:])
    uc_ref[...] = uc.reshape(SUBLANES, n_chunks, D_CONV)
    ut = lax.dot_general(wst_ref[...], z, (((1,), (1,)), ((), ())),
                         preferred_element_type=F32).astype(BF16)
    for r in range(SUBLANES):
        ut_ref[:, r * SSM_GROUP:(r + 1) * SSM_GROUP, :] = (
            ut[:, r * n_chunks:(r + 1) * n_chunks].reshape(N_GROUPS, SSM_GROUP, n_chunks))


def _in_proj_step(x4, g, w_conv, w_ssm_t):
    bsz, n_chunks = x4.shape[:2]
    rows = bsz * n_chunks
    return pl.pallas_call(
        _in_proj_step_kernel,
        grid=(CHUNK // SUBLANES, bsz),
        in_specs=[
            pl.BlockSpec((1, n_chunks, SUBLANES, D_MODEL), lambda i, b: (b, 0, i, 0)),
            pl.BlockSpec((1, D_MODEL), lambda i, b: (0, 0)),
            pl.BlockSpec((D_MODEL, 2 * D_CONV), lambda i, b: (0, 0)),
            pl.BlockSpec((D_SSM, D_MODEL), lambda i, b: (0, 0)),
        ],
        out_specs=[
            pl.BlockSpec((SUBLANES, n_chunks, D_CONV), lambda i, b: (i, b, 0)),
            pl.BlockSpec((N_GROUPS, SUBLANES * SSM_GROUP, n_chunks), lambda i, b: (0, i, b)),
        ],
        out_shape=[
            jax.ShapeDtypeStruct((CHUNK, rows, D_CONV), F32),
            jax.ShapeDtypeStruct((N_GROUPS, CHUNK_LANES, rows), BF16),
        ],
        compiler_params=_params(2),
        name="in_proj",
    )(x4, g, w_conv, w_ssm_t)


def _in_proj_meta_kernel(x_ref, g_ref, wc_ref, ws_ref, uc_ref, us_ref):
    z = _rms(x_ref[...], g_ref[...]).astype(BF16)
    p = jnp.dot(z, wc_ref[...], preferred_element_type=F32)
    uc_ref[...] = p[:, :D_CONV] * jax.nn.sigmoid(p[:, D_CONV:])
    us_ref[...] = jnp.dot(z, ws_ref[...], preferred_element_type=F32).astype(BF16)


def _in_proj_meta(meta, g, w_conv, w_ssm):
    return pl.pallas_call(
        _in_proj_meta_kernel,
        out_shape=[jax.ShapeDtypeStruct((N_META, D_CONV), F32),
                   jax.ShapeDtypeStruct((N_META, D_SSM), BF16)],
        name="in_proj_meta",
    )(meta, g, w_conv, w_ssm)


CONV_GROUP = 4
CONV_ROWS = 16


def _conv_kernel(u_ref, um_ref, w_ref, cb_ref, lg_ref, lb_ref, o_ref, cat_ref, acc_ref):
    n_steps, n_chunks, _ = u_ref.shape
    cat_ref[CONV_PAD:CONV_PAD + n_steps] = u_ref[...]
    chunk = lax.broadcasted_iota(jnp.int32, (n_chunks, 1), 0)
    for i in range(CONV_PAD):
        s_prev = n_steps - CONV_PAD + i
        prev = pltpu.roll(u_ref[s_prev], 1, axis=0)
        meta_row = um_ref[N_META - CONV_PAD + i:N_META - CONV_PAD + i + 1, :]
        cat_ref[i] = jnp.where(chunk == 0, meta_row, prev)
        nxt = pltpu.roll(u_ref[i], n_chunks - 1, axis=0)
        cat_ref[n_steps + CONV_PAD + i] = jnp.where(chunk == n_chunks - 1, 0.0, nxt)

    def body(sg, carry):
        s0 = sg * CONV_GROUP
        for c in range(D_CONV // LANES):
            cols = slice(c * LANES, (c + 1) * LANES)
            taps = [jnp.broadcast_to(w_ref[k:k + 1, cols], (SUBLANES, LANES))
                    for k in range(CONV_WIDTH)]

            def rows_body(rt, carry2):
                for sub in range(CONV_ROWS // SUBLANES):
                    r0 = pl.multiple_of(rt * CONV_ROWS + sub * SUBLANES, SUBLANES)
                    accs = [jnp.zeros((SUBLANES, LANES), F32)] * CONV_GROUP
                    for i in range(CONV_WIDTH + CONV_GROUP - 1):
                        d = cat_ref[s0 + i, pl.ds(r0, SUBLANES), cols]
                        for j in range(CONV_GROUP):
                            if 0 <= i - j < CONV_WIDTH:
                                accs[j] = accs[j] + d * taps[i - j]
                    for j in range(CONV_GROUP):
                        acc_ref[j, pl.ds(r0, SUBLANES), cols] = accs[j]
                return carry2

            lax.fori_loop(0, n_chunks // CONV_ROWS, rows_body, 0)
        for j in range(CONV_GROUP):
            y = acc_ref[j] + cb_ref[...]
            yc = y - jnp.mean(y, axis=-1, keepdims=True)
            yn = yc * lax.rsqrt(jnp.mean(yc * yc, axis=-1, keepdims=True) + NORM_EPS)
            yn = yn * lg_ref[...] + lb_ref[...]
            o_ref[s0 + j] = (yn * jax.nn.sigmoid(yn)).astype(BF16)
        return carry

    lax.fori_loop(0, n_steps // CONV_GROUP, body, 0)


def _conv_module(u_conv, u_conv_meta, conv_w, conv_b, ln_g, ln_b, bsz):
    n_steps, rows, _ = u_conv.shape
    n_chunks = rows // bsz
    vec = pl.BlockSpec((1, D_CONV), lambda b: (0, 0))
    blk = pl.BlockSpec((n_steps, n_chunks, D_CONV), lambda b: (0, b, 0))
    return pl.pallas_call(
        _conv_kernel,
        grid=(bsz,),
        in_specs=[
            blk,
            pl.BlockSpec((N_META, D_CONV), lambda b: (0, 0)),
            pl.BlockSpec((CONV_WIDTH, D_CONV), lambda b: (0, 0)),
            vec, vec, vec,
        ],
        out_specs=blk,
        out_shape=jax.ShapeDtypeStruct((n_steps, rows, D_CONV), BF16),
        scratch_shapes=[pltpu.VMEM((n_steps + 2 * CONV_PAD, n_chunks, D_CONV), F32),
                        pltpu.VMEM((CONV_GROUP, n_chunks, D_CONV), F32)],
        compiler_params=_params(),
        name="conv_module",
    )(u_conv, u_conv_meta, conv_w, conv_b, ln_g, ln_b)


def _zoh_kernel(lre_ref, lim_ref, ldt_ref, lbr_ref, lbi_ref, zr_ref, zi_ref):
    lre = lre_ref[...]
    lim = lim_ref[...]
    dt = jnp.exp(ldt_ref[...])
    ea = jnp.exp(lre * dt)
    lbr = ea * jnp.cos(lim * dt)
    lbi = ea * jnp.sin(lim * dt)
    nr = lbr - 1.0
    den = lre * lre + lim * lim
    lbr_ref[...] = lbr
    lbi_ref[...] = lbi
    zr_ref[...] = (nr * lre + lbi * lim) / den
    zi_ref[...] = (lbi * lre - nr * lim) / den


def _zoh(lam_re, lam_im, log_dt):
    shape = jax.ShapeDtypeStruct(lam_re.shape, F32)
    return pl.pallas_call(_zoh_kernel, out_shape=[shape] * 4, name="s5_zoh")(
        lam_re, lam_im, log_dt)


def _lanes(x, n):
    return jnp.concatenate([x] * (n // LANES), axis=1)


def _cmul(ar, ai, br, bi):
    return ar * br - ai * bi, ar * bi + ai * br


def _cpow(br, bi, expo, nbits):
    rr = jnp.ones(expo.shape, F32)
    ri = jnp.zeros(expo.shape, F32)
    for k in range(nbits):
        bit = ((expo >> k) & 1) == 1
        nr, ni = _cmul(rr, ri, br, bi)
        rr = jnp.where(bit, nr, rr)
        ri = jnp.where(bit, ni, ri)
        br, bi = _cmul(br, bi, br, bi)
    return rr, ri


def _chunk_ops_kernel(csc_ref, cb_ref, cr_ref, toep_ref, wend_ref, wout_ref, lam_ref):
    t = CHUNK
    rows = STATE_ROWS
    n_cols = CHUNK_LANES // LANES
    lam = (jnp.broadcast_to(csc_ref[0, :, 0:1], (rows, LANES)),
           jnp.broadcast_to(csc_ref[0, :, 1:2], (rows, LANES)))
    zr = jnp.broadcast_to(csc_ref[0, :, 2:3], (rows, LANES))
    zi = jnp.broadcast_to(csc_ref[0, :, 3:4], (rows, LANES))
    sel = (lax.broadcasted_iota(jnp.int32, (SSM_GROUP, LANES), 1) % SSM_GROUP
           == lax.broadcasted_iota(jnp.int32, (SSM_GROUP, LANES), 0)).astype(F32)

    def tile(i):
        return jnp.dot(cb_ref[0, i], sel, precision=lax.Precision.HIGHEST,
                       preferred_element_type=F32)

    c_same, c_swap, b_same, b_swap = tile(0), tile(1), tile(2), tile(3)
    row = lax.broadcasted_iota(jnp.int32, (rows, LANES), 0)
    is_re = (row // SSM_STATE) % 2 == 0
    sgn = jnp.where(is_re, -1.0, 1.0)
    ca = jnp.where(is_re, c_same, -c_same)
    cb = -c_swap
    bb_same = zr * b_same + sgn * zi * b_swap
    bb_swap = zr * b_swap - sgn * zi * b_same

    pows = {1: lam}
    k = 1
    while k < t:
        pows[2 * k] = _cmul(*pows[k], *pows[k])
        k *= 2
    step_bits = STEPS_PER_VREG.bit_length() - 1

    def col_pow(e):
        out = None
        for bit, val in pows.items():
            if e & bit:
                out = val if out is None else _cmul(*out, *val)
        return out

    i8 = lax.broadcasted_iota(jnp.int32, (rows, LANES), 1) // SSM_GROUP
    asc0 = _cpow(*lam, i8, step_bits)
    desc0 = _cpow(*lam, STEPS_PER_VREG - 1 - i8, step_bits)

    def table(base, exps):
        cols = [base if e == 0 else _cmul(*base, *col_pow(e)) for e in exps]
        return (jnp.concatenate([c[0] for c in cols], axis=1),
                jnp.concatenate([c[1] for c in cols], axis=1))

    asc = table(asc0, [STEPS_PER_VREG * q for q in range(n_cols)])
    desc = table(desc0, [STEPS_PER_VREG * (n_cols - 1 - q) for q in range(n_cols)])
    lam_w = (_lanes(lam[0], CHUNK_LANES), _lanes(lam[1], CHUNK_LANES))
    asc1 = _cmul(*asc, *lam_w)
    desc1 = _cmul(*desc, *lam_w)

    fwd = slice(0, DIR_ROWS)
    bwd = slice(DIR_ROWS, rows)

    def times_b(p, rs):
        return (p[0][rs] * _lanes(bb_same[rs], CHUNK_LANES)
                + _lanes(sgn[rs], CHUNK_LANES) * p[1][rs] * _lanes(bb_swap[rs], CHUNK_LANES))

    def times_c(p, rs):
        return (_lanes(ca[rs], CHUNK_LANES) * p[0][rs]
                + _lanes(cb[rs], CHUNK_LANES) * p[1][rs])

    wend_f = times_b(desc, fwd)
    wend_b = times_b(asc, bwd)
    wend_ref[0, fwd, :] = wend_f.astype(BF16)
    wend_ref[0, bwd, :] = wend_b.astype(BF16)

    lane_w = lax.broadcasted_iota(jnp.int32, (DIR_ROWS, CHUNK_LANES), 1)
    lag0_b = jnp.where(lane_w >= CHUNK_LANES - SSM_GROUP,
                       _lanes(bb_same[bwd], CHUNK_LANES), 0.0)
    bcat = jnp.concatenate(
        [jnp.concatenate([wend_f, jnp.zeros_like(wend_f)], axis=1),
         jnp.concatenate([lag0_b, times_b(asc1, bwd)], axis=1)], axis=0)
    lane_r = lax.broadcasted_iota(jnp.int32, (SSM_GROUP, rows), 1)
    c2 = jnp.where((lane_r // SSM_STATE) % 2 == 0, cr_ref[0], -cr_ref[0])
    kk = jnp.dot(c2, bcat, precision=lax.Precision.HIGHEST,
                 preferred_element_type=F32)
    for tt in range(t):
        off = (t - 1 - tt) * SSM_GROUP
        toep_ref[0, tt * SSM_GROUP:(tt + 1) * SSM_GROUP, :] = (
            kk[:, off:off + CHUNK_LANES].astype(BF16))

    wout = jnp.concatenate([times_c(asc1, fwd), times_c(desc1, bwd)], axis=0)
    wout_ref[0] = wout.T.astype(BF16)

    lam_ref[0, 0] = pows[t][0]
    lam_ref[0, 1] = pows[t][1]


def _chunk_ops(csc, cb16, c_row):
    g = csc.shape[0]
    return pl.pallas_call(
        _chunk_ops_kernel,
        grid=(g,),
        in_specs=[
            pl.BlockSpec((1, STATE_ROWS, 4), lambda i: (i, 0, 0)),
            pl.BlockSpec((1, 4, STATE_ROWS, SSM_GROUP), lambda i: (i, 0, 0, 0)),
            pl.BlockSpec((1, SSM_GROUP, STATE_ROWS), lambda i: (i, 0, 0)),
        ],
        out_specs=[
            pl.BlockSpec((1, CHUNK_LANES, CHUNK_LANES), lambda i: (i, 0, 0)),
            pl.BlockSpec((1, STATE_ROWS, CHUNK_LANES), lambda i: (i, 0, 0)),
            pl.BlockSpec((1, CHUNK_LANES, STATE_ROWS), lambda i: (i, 0, 0)),
            pl.BlockSpec((1, 2, STATE_ROWS, LANES), lambda i: (i, 0, 0, 0)),
        ],
        out_shape=[
            jax.ShapeDtypeStruct((g, CHUNK_LANES, CHUNK_LANES), BF16),
            jax.ShapeDtypeStruct((g, STATE_ROWS, CHUNK_LANES), BF16),
            jax.ShapeDtypeStruct((g, CHUNK_LANES, STATE_ROWS), BF16),
            jax.ShapeDtypeStruct((g, 2, STATE_ROWS, LANES), F32),
        ],
        compiler_params=_params(),
        name="s5_chunk_ops",
    )(csc, cb16, c_row)


SSM_GROUPS_PER_STEP = 2


def _chunk_scan(er, ei, ar, ai, x0, chunk, n_chunks, forward):
    width = er.shape[1]
    xr, xi = er, ei
    if x0 is not None:
        fr, fi = _cmul(ar, ai, x0[0], x0[1])
        xr = xr + jnp.where(chunk == 0, fr, 0.0)
        xi = xi + jnp.where(chunk == 0, fi, 0.0)

    def shifted(v, sh):
        if forward:
            return jnp.where(chunk >= sh, pltpu.roll(v, sh, axis=1), 0.0)
        return jnp.where(chunk < n_chunks - sh, pltpu.roll(v, width - sh, axis=1), 0.0)

    sh = 1
    while sh < n_chunks:
        sr, si = shifted(xr, sh), shifted(xi, sh)
        xr, xi = xr + ar * sr - ai * si, xi + ar * si + ai * sr
        ar, ai = _cmul(ar, ai, ar, ai)
        sh *= 2
    inr, ini = shifted(xr, 1), shifted(xi, 1)
    if x0 is not None:
        inr = jnp.where(chunk == 0, x0[0], inr)
        ini = jnp.where(chunk == 0, x0[1], ini)
    return inr, ini


def _ssm_kernel(u_ref, um_ref, toep_ref, wend_ref, wout_ref, lam_ref, y_ref, *, n_chunks):
    for gi in range(SSM_GROUPS_PER_STEP):
        u = u_ref[gi]
        wend = wend_ref[gi]
        width = u.shape[1]
        e = jnp.dot(wend, u, preferred_element_type=F32)
        x0 = jnp.dot(wend, um_ref[gi], preferred_element_type=F32)[:, 0:1]
        chunk = lax.broadcasted_iota(jnp.int32, (1, width), 1) % n_chunks
        p = SSM_STATE
        ar = _lanes(lam_ref[gi, 0], width)
        ai = _lanes(lam_ref[gi, 1], width)
        f_in = _chunk_scan(e[0:p], e[p:2 * p], ar[0:p], ai[0:p], (x0[0:p], x0[p:2 * p]),
                           chunk, n_chunks, True)
        b_in = _chunk_scan(e[2 * p:3 * p], e[3 * p:], ar[2 * p:3 * p], ai[2 * p:3 * p],
                           None, chunk, n_chunks, False)
        xin = jnp.concatenate([f_in[0], f_in[1], b_in[0], b_in[1]], axis=0).astype(BF16)
        y = jnp.dot(toep_ref[gi], u, preferred_element_type=F32)
        y = y + jnp.dot(wout_ref[gi], xin, preferred_element_type=F32)
        y_ref[gi] = y.astype(BF16)


def _ssm(u_col, u_meta, toep, wend, wout, lam, n_chunks):
    g, _, width = u_col.shape
    per = SSM_GROUPS_PER_STEP
    return pl.pallas_call(
        functools.partial(_ssm_kernel, n_chunks=n_chunks),
        grid=(g // per,),
        in_specs=[
            pl.BlockSpec((per, CHUNK_LANES, width), lambda i: (i, 0, 0)),
            pl.BlockSpec((per, CHUNK_LANES, LANES), lambda i: (i, 0, 0)),
            pl.BlockSpec((per, CHUNK_LANES, CHUNK_LANES), lambda i: (i, 0, 0)),
            pl.BlockSpec((per, STATE_ROWS, CHUNK_LANES), lambda i: (i, 0, 0)),
            pl.BlockSpec((per, CHUNK_LANES, STATE_ROWS), lambda i: (i, 0, 0)),
            pl.BlockSpec((per, 2, STATE_ROWS, LANES), lambda i: (i, 0, 0, 0)),
        ],
        out_specs=pl.BlockSpec((per, CHUNK_LANES, width), lambda i: (i, 0, 0)),
        out_shape=jax.ShapeDtypeStruct((g, CHUNK_LANES, width), BF16),
        compiler_params=_params(),
        name="s5_mixer",
    )(u_col, u_meta, toep, wend, wout, lam)


FF_CHUNK = 1024
MIX_STEPS = 4


def _mix_ffn_kernel(x_ref, co_ref, y_ref, ut_ref, d_ref, gwt_ref, gb_ref, wo_ref,
                    gf_ref, w1_ref, w2_ref, gl_ref, o_ref):
    n_chunks = x_ref.shape[1]
    for part in range(SUBLANES // MIX_STEPS):
        steps = range(part * MIX_STEPS, (part + 1) * MIX_STEPS)

        def col(ref):
            return jnp.concatenate(
                [ref[:, r * SSM_GROUP:(r + 1) * SSM_GROUP, :].reshape(D_SSM, n_chunks)
                 for r in steps], axis=1).astype(F32)

        yt = col(y_ref) + d_ref[...] * col(ut_ref)
        ge = jax.nn.gelu(yt)
        gate = (jnp.dot(gwt_ref[...], ge.astype(BF16), preferred_element_type=F32)
                + gb_ref[...])
        s5 = (ge * jax.nn.sigmoid(gate)).T.astype(BF16)
        co = co_ref[part * MIX_STEPS:(part + 1) * MIX_STEPS].reshape(
            MIX_STEPS * n_chunks, D_CONV)
        x = jnp.concatenate([x_ref[0, :, r, :] for r in steps], axis=0)
        h = x + jnp.dot(jnp.concatenate([co, s5], axis=1), wo_ref[...],
                        preferred_element_type=F32)
        z = _rms(h, gf_ref[...]).astype(BF16)
        for j in range(D_FF // FF_CHUNK):
            a = jnp.dot(z, w1_ref[:, j * FF_CHUNK:(j + 1) * FF_CHUNK],
                        preferred_element_type=F32)
            a = jnp.square(jnp.maximum(a, 0.0)).astype(BF16)
            h = h + jnp.dot(a, w2_ref[j * FF_CHUNK:(j + 1) * FF_CHUNK, :],
                            preferred_element_type=F32)
        out = _rms(h, gl_ref[...])
        for k, r in enumerate(steps):
            o_ref[0, :, r, :] = out[k * n_chunks:(k + 1) * n_chunks, :]


def _mix_ffn(x4, conv_out, y_col, u_col, d_col, glu_wt, glu_b_col, w_out, g_ffn, w1, w2,
             g_final):
    bsz, n_chunks = x4.shape[:2]

    def const(shape):
        return pl.BlockSpec(shape, lambda i, b: (0, 0), pipeline_mode=pl.Buffered(1))

    x_blk = pl.BlockSpec((1, n_chunks, SUBLANES, D_MODEL), lambda i, b: (b, 0, i, 0))
    col_blk = pl.BlockSpec((N_GROUPS, SUBLANES * SSM_GROUP, n_chunks),
                           lambda i, b: (0, i, b))
    return pl.pallas_call(
        _mix_ffn_kernel,
        grid=(CHUNK // SUBLANES, bsz),
        in_specs=[
            x_blk,
            pl.BlockSpec((SUBLANES, n_chunks, D_CONV), lambda i, b: (i, b, 0)),
            col_blk,
            col_blk,
            const((D_SSM, 1)),
            const((D_SSM, D_SSM)),
            const((D_SSM, 1)),
            const((D_MODEL, D_MODEL)),
            const((1, D_MODEL)),
            const((D_MODEL, D_FF)),
            const((D_FF, D_MODEL)),
            const((1, D_MODEL)),
        ],
        out_specs=x_blk,
        out_shape=jax.ShapeDtypeStruct(x4.shape, F32),
        compiler_params=_params(2),
        name="mix_ffn",
    )(x4, conv_out, y_col, u_col, d_col, glu_wt, glu_b_col, w_out, g_ffn, w1, w2, g_final)


def _s5_operator_inputs(lbr, lbi, zr, zi, b_re, b_im, c_re, c_im):
    g, p, h = N_GROUPS, SSM_STATE, SSM_GROUP

    def col_scalar(a):
        a = jnp.transpose(a, (1, 0, 2))
        return jnp.broadcast_to(a[:, :, None, :], (g, 2, 2, p)).reshape(g, 4 * p)

    csc = jnp.stack([col_scalar(a) for a in (lbr, lbi, zr, zi)], axis=-1)

    def col_pair(first, second, perm):
        first = jnp.transpose(first, perm)
        second = jnp.transpose(second, perm)
        return jnp.stack([first, second], axis=2).reshape(g, 4 * p, h)

    c_perm = (1, 0, 3, 2)
    b_perm = (1, 0, 2, 3)
    cb16 = jnp.stack([col_pair(c_re, c_im, c_perm), col_pair(c_im, c_re, c_perm),
                      col_pair(b_re, b_im, b_perm), col_pair(b_im, b_re, b_perm)], axis=1)
    c_row = jnp.transpose(jnp.stack([c_re, c_im], axis=2), (1, 3, 0, 2, 4)).reshape(g, h, 4 * p)
    return csc, cb16, c_row


def kernel(x, meta_tokens, norm_mix_g, w_in, conv_w, conv_b, conv_ln_g, conv_ln_b,
           ssm_lam_re, ssm_lam_im, ssm_log_dt, ssm_b_re, ssm_b_im, ssm_c_re, ssm_c_im,
           ssm_d, ssm_glu_w, ssm_glu_b, w_out, norm_ffn_g, w_ff1, w_ff2, norm_final_g):
    assert w_in.shape[0] == 1, "single-layer block"
    bsz, seq, _ = x.shape
    assert seq % CHUNK == 0 and CHUNK >= N_META
    n_chunks = seq // CHUNK
    g, h = N_GROUPS, SSM_GROUP

    x3 = x.reshape(bsz, n_chunks, CHUNK, D_MODEL)
    g_mix = norm_mix_g[0][None, :]
    w_in_b = w_in[0].astype(BF16)
    w_conv, w_ssm = w_in_b[:, :2 * D_CONV], w_in_b[:, 2 * D_CONV:]
    u_conv, u_col = _in_proj_step(x3, g_mix, w_conv, w_ssm.T)
    u_conv_m, u_ssm_m = _in_proj_meta(meta_tokens, g_mix, w_conv, w_ssm)

    conv_out = _conv_module(u_conv, u_conv_m, conv_w[0], conv_b[0][None, :],
                            conv_ln_g[0][None, :], conv_ln_b[0][None, :], bsz)

    ldt = jnp.broadcast_to(ssm_log_dt[0][..., None], ssm_lam_re[0].shape)
    flat = lambda a: a.reshape(2 * g, SSM_STATE)
    lbr, lbi, zr, zi = [a.reshape(2, g, SSM_STATE) for a in
                        _zoh(flat(ssm_lam_re[0]), flat(ssm_lam_im[0]), flat(ldt))]
    csc, cb16, c_row = _s5_operator_inputs(lbr, lbi, zr, zi, ssm_b_re[0], ssm_b_im[0],
                                           ssm_c_re[0], ssm_c_im[0])
    toep, wend, wout, lam = _chunk_ops(csc, cb16, c_row)

    u_meta = jnp.transpose(u_ssm_m.reshape(N_META, g, h), (1, 0, 2)).reshape(g, N_META * h, 1)
    u_meta = jnp.pad(u_meta, ((0, 0), (CHUNK_LANES - N_META * h, 0), (0, LANES - 1)))
    y_col = _ssm(u_col, u_meta, toep, wend, wout, lam, n_chunks)

    out = _mix_ffn(x3, conv_out, y_col, u_col, ssm_d[0][:, None],
                   ssm_glu_w[0].T.astype(BF16), ssm_glu_b[0][:, None],
                   w_out[0].astype(BF16), norm_ffn_g[0][None, :], w_ff1[0].astype(BF16),
                   w_ff2[0].astype(BF16), norm_final_g[None, :])
    return out.reshape(bsz, seq, D_MODEL)
```

```python
import functools

import jax
import jax.numpy as jnp
from jax import lax
from jax.experimental import pallas as pl
from jax.experimental.pallas import tpu as pltpu

F32 = jnp.float32
BF16 = jnp.bfloat16

D_MODEL = 1024
N_META = 16
D_CONV = 512
D_SSM = 512
CONV_WIDTH = 31
CONV_PAD = CONV_WIDTH // 2
SSM_GROUP = 16
N_GROUPS = D_SSM // SSM_GROUP
SSM_STATE = 64
D_FF = 4096
NORM_EPS = 1e-5
LANES = 128
SUBLANES = 8

CHUNK = 32
CHUNK_LANES = CHUNK * SSM_GROUP
STATE_ROWS = 4 * SSM_STATE
DIR_ROWS = 2 * SSM_STATE
STEPS_PER_VREG = LANES // SSM_GROUP

VMEM_LIMIT_BYTES = 60 * 1024 * 1024


def _rms(x, g):
    return x * lax.rsqrt(jnp.mean(x * x, axis=-1, keepdims=True) + NORM_EPS) * g


def _params(n_axes=1):
    return pltpu.CompilerParams(dimension_semantics=("parallel",) * n_axes,
                                vmem_limit_bytes=VMEM_LIMIT_BYTES)


def _in_proj_step_kernel(x_ref, g_ref, wc_ref, wst_ref, uc_ref, ut_ref):
    n_chunks = x_ref.shape[1]
    x = jnp.concatenate([x_ref[0, :, r, :] for r in range(SUBLANES)], axis=0)
    z = _rms(x, g_ref[...]).astype(BF16)
    p = jnp.dot(z, wc_ref[...], preferred_element_type=F32)
    uc = p[:, :D_CONV] * jax.nn.sigmoid(p[:, D_CONV:])
    uc_ref[...] = uc.reshape(SUBLANES, n_chunks, D_CONV)
    ut = lax.dot_general(wst_ref[...], z, (((1,), (1,)), ((), ())),
                         preferred_element_type=F32).astype(BF16)
    for r in range(SUBLANES):
        ut_ref[:, r * SSM_GROUP:(r + 1) * SSM_GROUP, :] = (
            ut[:, r * n_chunks:(r + 1) * n_chunks].reshape(N_GROUPS, SSM_GROUP, n_chunks))


def _in_proj_step(x4, g, w_conv, w_ssm_t):
    bsz, n_chunks = x4.shape[:2]
    rows = bsz * n_chunks
    return pl.pallas_call(
        _in_proj_step_kernel,
        grid=(CHUNK // SUBLANES, bsz),
        in_specs=[
            pl.BlockSpec((1, n_chunks, SUBLANES, D_MODEL), lambda i, b: (b, 0, i, 0)),
            pl.BlockSpec((1, D_MODEL), lambda i, b: (0, 0)),
            pl.BlockSpec((D_MODEL, 2 * D_CONV), lambda i, b: (0, 0)),
            pl.BlockSpec((D_SSM, D_MODEL), lambda i, b: (0, 0)),
        ],
        out_specs=[
            pl.BlockSpec((SUBLANES, n_chunks, D_CONV), lambda i, b: (i, b, 0)),
            pl.BlockSpec((N_GROUPS, SUBLANES * SSM_GROUP, n_chunks), lambda i, b: (0, i, b)),
        ],
        out_shape=[
            jax.ShapeDtypeStruct((CHUNK, rows, D_CONV), F32),
            jax.ShapeDtypeStruct((N_GROUPS, CHUNK_LANES, rows), BF16),
        ],
        compiler_params=_params(2),
        name="in_proj",
    )(x4, g, w_conv, w_ssm_t)


def _in_proj_meta_kernel(x_ref, g_ref, wc_ref, ws_ref, uc_ref, us_ref):
    z = _rms(x_ref[...], g_ref[...]).astype(BF16)
    p = jnp.dot(z, wc_ref[...], preferred_element_type=F32)
    uc_ref[...] = p[:, :D_CONV] * jax.nn.sigmoid(p[:, D_CONV:])
    us_ref[...] = jnp.dot(z, ws_ref[...], preferred_element_type=F32).astype(BF16)


def _in_proj_meta(meta, g, w_conv, w_ssm):
    return pl.pallas_call(
        _in_proj_meta_kernel,
        out_shape=[jax.ShapeDtypeStruct((N_META, D_CONV), F32),
                   jax.ShapeDtypeStruct((N_META, D_SSM), BF16)],
        name="in_proj_meta",
    )(meta, g, w_conv, w_ssm)


CONV_GROUP = 4
CONV_ROWS = 16


def _conv_kernel(u_ref, um_ref, w_ref, cb_ref, lg_ref, lb_ref, o_ref, cat_ref, acc_ref):
    n_steps, n_chunks, _ = u_ref.shape
    cat_ref[CONV_PAD:CONV_PAD + n_steps] = u_ref[...]
    chunk = lax.broadcasted_iota(jnp.int32, (n_chunks, 1), 0)
    for i in range(CONV_PAD):
        s_prev = n_steps - CONV_PAD + i
        prev = pltpu.roll(u_ref[s_prev], 1, axis=0)
        meta_row = um_ref[N_META - CONV_PAD + i:N_META - CONV_PAD + i + 1, :]
        cat_ref[i] = jnp.where(chunk == 0, meta_row, prev)
        nxt = pltpu.roll(u_ref[i], n_chunks - 1, axis=0)
        cat_ref[n_steps + CONV_PAD + i] = jnp.where(chunk == n_chunks - 1, 0.0, nxt)

    def body(sg, carry):
        s0 = sg * CONV_GROUP
        for c in range(D_CONV // LANES):
            cols = slice(c * LANES, (c + 1) * LANES)
            taps = [jnp.broadcast_to(w_ref[k:k + 1, cols], (SUBLANES, LANES))
                    for k in range(CONV_WIDTH)]

            def rows_body(rt, carry2):
                for sub in range(CONV_ROWS // SUBLANES):
                    r0 = pl.multiple_of(rt * CONV_ROWS + sub * SUBLANES, SUBLANES)
                    accs = [jnp.zeros((SUBLANES, LANES), F32)] * CONV_GROUP
                    for i in range(CONV_WIDTH + CONV_GROUP - 1):
                        d = cat_ref[s0 + i, pl.ds(r0, SUBLANES), cols]
                        for j in range(CONV_GROUP):
                            if 0 <= i - j < CONV_WIDTH:
                                accs[j] = accs[j] + d * taps[i - j]
                    for j in range(CONV_GROUP):
                        acc_ref[j, pl.ds(r0, SUBLANES), cols] = accs[j]
                return carry2

            lax.fori_loop(0, n_chunks // CONV_ROWS, rows_body, 0)
        for j in range(CONV_GROUP):
            y = acc_ref[j] + cb_ref[...]
            yc = y - jnp.mean(y, axis=-1, keepdims=True)
            yn = yc * lax.rsqrt(jnp.mean(yc * yc, axis=-1, keepdims=True) + NORM_EPS)
            yn = yn * lg_ref[...] + lb_ref[...]
            o_ref[s0 + j] = (yn * jax.nn.sigmoid(yn)).astype(BF16)
        return carry

    lax.fori_loop(0, n_steps // CONV_GROUP, body, 0)


def _conv_module(u_conv, u_conv_meta, conv_w, conv_b, ln_g, ln_b, bsz):
    n_steps, rows, _ = u_conv.shape
    n_chunks = rows // bsz
    vec = pl.BlockSpec((1, D_CONV), lambda b: (0, 0))
    blk = pl.BlockSpec((n_steps, n_chunks, D_CONV), lambda b: (0, b, 0))
    return pl.pallas_call(
        _conv_kernel,
        grid=(bsz,),
        in_specs=[
            blk,
            pl.BlockSpec((N_META, D_CONV), lambda b: (0, 0)),
            pl.BlockSpec((CONV_WIDTH, D_CONV), lambda b: (0, 0)),
            vec, vec, vec,
        ],
        out_specs=blk,
        out_shape=jax.ShapeDtypeStruct((n_steps, rows, D_CONV), BF16),
        scratch_shapes=[pltpu.VMEM((n_steps + 2 * CONV_PAD, n_chunks, D_CONV), F32),
                        pltpu.VMEM((CONV_GROUP, n_chunks, D_CONV), F32)],
        compiler_params=_params(),
        name="conv_module",
    )(u_conv, u_conv_meta, conv_w, conv_b, ln_g, ln_b)


def _zoh_kernel(lre_ref, lim_ref, ldt_ref, lbr_ref, lbi_ref, zr_ref, zi_ref):
    lre = lre_ref[...]
    lim = lim_ref[...]
    dt = jnp.exp(ldt_ref[...])
    ea = jnp.exp(lre * dt)
    lbr = ea * jnp.cos(lim * dt)
    lbi = ea * jnp.sin(lim * dt)
    nr = lbr - 1.0
    den = lre * lre + lim * lim
    lbr_ref[...] = lbr
    lbi_ref[...] = lbi
    zr_ref[...] = (nr * lre + lbi * lim) / den
    zi_ref[...] = (lbi * lre - nr * lim) / den


def _zoh(lam_re, lam_im, log_dt):
    shape = jax.ShapeDtypeStruct(lam_re.shape, F32)
    return pl.pallas_call(_zoh_kernel, out_shape=[shape] * 4, name="s5_zoh")(
        lam_re, lam_im, log_dt)


def _lanes(x, n):
    return jnp.concatenate([x] * (n // LANES), axis=1)


def _cmul(ar, ai, br, bi):
    return ar * br - ai * bi, ar * bi + ai * br


def _cpow(br, bi, expo, nbits):
    rr = jnp.ones(expo.shape, F32)
    ri = jnp.zeros(expo.shape, F32)
    for k in range(nbits):
        bit = ((expo >> k) & 1) == 1
        nr, ni = _cmul(rr, ri, br, bi)
        rr = jnp.where(bit, nr, rr)
        ri = jnp.where(bit, ni, ri)
        br, bi = _cmul(br, bi, br, bi)
    return rr, ri


def _chunk_ops_kernel(csc_ref, cb_ref, cr_ref, toep_ref, wend_ref, wout_ref, lam_ref):
    t = CHUNK
    rows = STATE_ROWS
    n_cols = CHUNK_LANES // LANES
    lam = (jnp.broadcast_to(csc_ref[0, :, 0:1], (rows, LANES)),
           jnp.broadcast_to(csc_ref[0, :, 1:2], (rows, LANES)))
    zr = jnp.broadcast_to(csc_ref[0, :, 2:3], (rows, LANES))
    zi = jnp.broadcast_to(csc_ref[0, :, 3:4], (rows, LANES))
    sel = (lax.broadcasted_iota(jnp.int32, (SSM_GROUP, LANES), 1) % SSM_GROUP
           == lax.broadcasted_iota(jnp.int32, (SSM_GROUP, LANES), 0)).astype(F32)

    def tile(i):
        return jnp.dot(cb_ref[0, i], sel, precision=lax.Precision.HIGHEST,
                       preferred_element_type=F32)

    c_same, c_swap, b_same, b_swap = tile(0), tile(1), tile(2), tile(3)
    row = lax.broadcasted_iota(jnp.int32, (rows, LANES), 0)
    is_re = (row // SSM_STATE) % 2 == 0
    sgn = jnp.where(is_re, -1.0, 1.0)
    ca = jnp.where(is_re, c_same, -c_same)
    cb = -c_swap
    bb_same = zr * b_same + sgn * zi * b_swap
    bb_swap = zr * b_swap - sgn * zi * b_same

    pows = {1: lam}
    k = 1
    while k < t:
        pows[2 * k] = _cmul(*pows[k], *pows[k])
        k *= 2
    step_bits = STEPS_PER_VREG.bit_length() - 1

    def col_pow(e):
        out = None
        for bit, val in pows.items():
            if e & bit:
                out = val if out is None else _cmul(*out, *val)
        return out

    i8 = lax.broadcasted_iota(jnp.int32, (rows, LANES), 1) // SSM_GROUP
    asc0 = _cpow(*lam, i8, step_bits)
    desc0 = _cpow(*lam, STEPS_PER_VREG - 1 - i8, step_bits)

    def table(base, exps):
        cols = [base if e == 0 else _cmul(*base, *col_pow(e)) for e in exps]
        return (jnp.concatenate([c[0] for c in cols], axis=1),
                jnp.concatenate([c[1] for c in cols], axis=1))

    asc = table(asc0, [STEPS_PER_VREG * q for q in range(n_cols)])
    desc = table(desc0, [STEPS_PER_VREG * (n_cols - 1 - q) for q in range(n_cols)])
    lam_w = (_lanes(lam[0], CHUNK_LANES), _lanes(lam[1], CHUNK_LANES))
    asc1 = _cmul(*asc, *lam_w)
    desc1 = _cmul(*desc, *lam_w)

    fwd = slice(0, DIR_ROWS)
    bwd = slice(DIR_ROWS, rows)

    def times_b(p, rs):
        return (p[0][rs] * _lanes(bb_same[rs], CHUNK_LANES)
                + _lanes(sgn[rs], CHUNK_LANES) * p[1][rs] * _lanes(bb_swap[rs], CHUNK_LANES))

    def times_c(p, rs):
        return (_lanes(ca[rs], CHUNK_LANES) * p[0][rs]
                + _lanes(cb[rs], CHUNK_LANES) * p[1][rs])

    wend_f = times_b(desc, fwd)
    wend_b = times_b(asc, bwd)
    wend_ref[0, fwd, :] = wend_f.astype(BF16)
    wend_ref[0, bwd, :] = wend_b.astype(BF16)

    lane_w = lax.broadcasted_iota(jnp.int32, (DIR_ROWS, CHUNK_LANES), 1)
    lag0_b = jnp.where(lane_w >= CHUNK_LANES - SSM_GROUP,
                       _lanes(bb_same[bwd], CHUNK_LANES), 0.0)
    bcat = jnp.concatenate(
        [jnp.concatenate([wend_f, jnp.zeros_like(wend_f)], axis=1),
         jnp.concatenate([lag0_b, times_b(asc1, bwd)], axis=1)], axis=0)
    lane_r = lax.broadcasted_iota(jnp.int32, (SSM_GROUP, rows), 1)
    c2 = jnp.where((lane_r // SSM_STATE) % 2 == 0, cr_ref[0], -cr_ref[0])
    kk = jnp.dot(c2, bcat, precision=lax.Precision.HIGHEST,
                 preferred_element_type=F32)
    for tt in range(t):
        off = (t - 1 - tt) * SSM_GROUP
        toep_ref[0, tt * SSM_GROUP:(tt + 1) * SSM_GROUP, :] = (
            kk[:, off:off + CHUNK_LANES].astype(BF16))

    wout = jnp.concatenate([times_c(asc1, fwd), times_c(desc1, bwd)], axis=0)
    wout_ref[0] = wout.T.astype(BF16)

    lam_ref[0, 0] = pows[t][0]
    lam_ref[0, 1] = pows[t][1]


def _chunk_ops(csc, cb16, c_row):
    g = csc.shape[0]
    return pl.pallas_call(
        _chunk_ops_kernel,
        grid=(g,),
        in_specs=[
            pl.BlockSpec((1, STATE_ROWS, 4), lambda i: (i, 0, 0)),
            pl.BlockSpec((1, 4, STATE_ROWS, SSM_GROUP), lambda i: (i, 0, 0, 0)),
            pl.BlockSpec((1, SSM_GROUP, STATE_ROWS), lambda i: (i, 0, 0)),
        ],
        out_specs=[
            pl.BlockSpec((1, CHUNK_LANES, CHUNK_LANES), lambda i: (i, 0, 0)),
            pl.BlockSpec((1, STATE_ROWS, CHUNK_LANES), lambda i: (i, 0, 0)),
            pl.BlockSpec((1, CHUNK_LANES, STATE_ROWS), lambda i: (i, 0, 0)),
            pl.BlockSpec((1, 2, STATE_ROWS, LANES), lambda i: (i, 0, 0, 0)),
        ],
        out_shape=[
            jax.ShapeDtypeStruct((g, CHUNK_LANES, CHUNK_LANES), BF16),
            jax.ShapeDtypeStruct((g, STATE_ROWS, CHUNK_LANES), BF16),
            jax.ShapeDtypeStruct((g, CHUNK_LANES, STATE_ROWS), BF16),
            jax.ShapeDtypeStruct((g, 2, STATE_ROWS, LANES), F32),
        ],
        compiler_params=_params(),
        name="s5_chunk_ops",
    )(csc, cb16, c_row)


SSM_GROUPS_PER_STEP = 2


def _chunk_scan(er, ei, ar, ai, x0, chunk, n_chunks, forward):
    width = er.shape[1]
    xr, xi = er, ei
    if x0 is not None:
        fr, fi = _cmul(ar, ai, x0[0], x0[1])
        xr = xr + jnp.where(chunk == 0, fr, 0.0)
        xi = xi + jnp.where(chunk == 0, fi, 0.0)

    def shifted(v, sh):
        if forward:
            return jnp.where(chunk >= sh, pltpu.roll(v, sh, axis=1), 0.0)
        return jnp.where(chunk < n_chunks - sh, pltpu.roll(v, width - sh, axis=1), 0.0)

    sh = 1
    while sh < n_chunks:
        sr, si = shifted(xr, sh), shifted(xi, sh)
        xr, xi = xr + ar * sr - ai * si, xi + ar * si + ai * sr
        ar, ai = _cmul(ar, ai, ar, ai)
        sh *= 2
    inr, ini = shifted(xr, 1), shifted(xi, 1)
    if x0 is not None:
        inr = jnp.where(chunk == 0, x0[0], inr)
        ini = jnp.where(chunk == 0, x0[1], ini)
    return inr, ini


def _ssm_kernel(u_ref, um_ref, toep_ref, wend_ref, wout_ref, lam_ref, y_ref, *, n_chunks):
    for gi in range(SSM_GROUPS_PER_STEP):
        u = u_ref[gi]
        wend = wend_ref[gi]
        width = u.shape[1]
        e = jnp.dot(wend, u, preferred_element_type=F32)
        x0 = jnp.dot(wend, um_ref[gi], preferred_element_type=F32)[:, 0:1]
        chunk = lax.broadcasted_iota(jnp.int32, (1, width), 1) % n_chunks
        p = SSM_STATE
        ar = _lanes(lam_ref[gi, 0], width)
        ai = _lanes(lam_ref[gi, 1], width)
        f_in = _chunk_scan(e[0:p], e[p:2 * p], ar[0:p], ai[0:p], (x0[0:p], x0[p:2 * p]),
                           chunk, n_chunks, True)
        b_in = _chunk_scan(e[2 * p:3 * p], e[3 * p:], ar[2 * p:3 * p], ai[2 * p:3 * p],
                           None, chunk, n_chunks, False)
        xin = jnp.concatenate([f_in[0], f_in[1], b_in[0], b_in[1]], axis=0).astype(BF16)
        y = jnp.dot(toep_ref[gi], u, preferred_element_type=F32)
        y = y + jnp.dot(wout_ref[gi], xin, preferred_element_type=F32)
        y_ref[gi] = y.astype(BF16)


def _ssm(u_col, u_meta, toep, wend, wout, lam, n_chunks):
    g, _, width = u_col.shape
    per = SSM_GROUPS_PER_STEP
    return pl.pallas_call(
        functools.partial(_ssm_kernel, n_chunks=n_chunks),
        grid=(g // per,),
        in_specs=[
            pl.BlockSpec((per, CHUNK_LANES, width), lambda i: (i, 0, 0)),
            pl.BlockSpec((per, CHUNK_LANES, LANES), lambda i: (i, 0, 0)),
            pl.BlockSpec((per, CHUNK_LANES, CHUNK_LANES), lambda i: (i, 0, 0)),
            pl.BlockSpec((per, STATE_ROWS, CHUNK_LANES), lambda i: (i, 0, 0)),
            pl.BlockSpec((per, CHUNK_LANES, STATE_ROWS), lambda i: (i, 0, 0)),
            pl.BlockSpec((per, 2, STATE_ROWS, LANES), lambda i: (i, 0, 0, 0)),
        ],
        out_specs=pl.BlockSpec((per, CHUNK_LANES, width), lambda i: (i, 0, 0)),
        out_shape=jax.ShapeDtypeStruct((g, CHUNK_LANES, width), BF16),
        compiler_params=_params(),
        name="s5_mixer",
    )(u_col, u_meta, toep, wend, wout, lam)


FF_CHUNK = 1024
MIX_STEPS = 4


def _mix_ffn_kernel(x_ref, co_ref, y_ref, ut_ref, d_ref, gwt_ref, gb_ref, wo_ref,
                    gf_ref, w1_ref, w2_ref, gl_ref, o_ref, h_ref, z_ref, a_ref):
    n_chunks = x_ref.shape[1]
    n_parts = SUBLANES // MIX_STEPS

    def steps_of(part):
        return range(part * MIX_STEPS, (part + 1) * MIX_STEPS)

    def mix(part):
        def col(ref):
            return jnp.concatenate(
                [ref[:, r * SSM_GROUP:(r + 1) * SSM_GROUP, :].reshape(D_SSM, n_chunks)
                 for r in steps_of(part)], axis=1).astype(F32)

        yt = col(y_ref) + d_ref[...] * col(ut_ref)
        ge = jax.nn.gelu(yt)
        gate = (jnp.dot(gwt_ref[...], ge.astype(BF16), preferred_element_type=F32)
                + gb_ref[...])
        s5 = (ge * jax.nn.sigmoid(gate)).T.astype(BF16)
        co = co_ref[part * MIX_STEPS:(part + 1) * MIX_STEPS].reshape(
            MIX_STEPS * n_chunks, D_CONV)
        x = jnp.concatenate([x_ref[0, :, r, :] for r in steps_of(part)], axis=0)
        h = x + jnp.dot(jnp.concatenate([co, s5], axis=1), wo_ref[...],
                        preferred_element_type=F32)
        h_ref[part] = h
        z_ref[part] = _rms(h, gf_ref[...]).astype(BF16)

    def ffn_up(part, j):
        cols = slice(j * FF_CHUNK, (j + 1) * FF_CHUNK)
        a = jnp.dot(z_ref[part], w1_ref[:, cols], preferred_element_type=F32)
        a_ref[:, cols] = jnp.square(jnp.maximum(a, 0.0)).astype(BF16)

    def ffn_down(part):
        h = h_ref[part] + jnp.dot(a_ref[...], w2_ref[...], preferred_element_type=F32)
        out = _rms(h, gl_ref[...])
        for k, r in enumerate(steps_of(part)):
            o_ref[0, :, r, :] = out[k * n_chunks:(k + 1) * n_chunks, :]

    mix(0)
    for part in range(n_parts):
        ffn_up(part, 0)
        if part + 1 < n_parts:
            mix(part + 1)
        for j in range(1, D_FF // FF_CHUNK):
            ffn_up(part, j)
        ffn_down(part)


def _mix_ffn(x4, conv_out, y_col, u_col, d_col, glu_wt, glu_b_col, w_out, g_ffn, w1, w2,
             g_final):
    bsz, n_chunks = x4.shape[:2]

    def const(shape):
        return pl.BlockSpec(shape, lambda i, b: (0, 0), pipeline_mode=pl.Buffered(1))

    x_blk = pl.BlockSpec((1, n_chunks, SUBLANES, D_MODEL), lambda i, b: (b, 0, i, 0))
    col_blk = pl.BlockSpec((N_GROUPS, SUBLANES * SSM_GROUP, n_chunks),
                           lambda i, b: (0, i, b))
    return pl.pallas_call(
        _mix_ffn_kernel,
        grid=(CHUNK // SUBLANES, bsz),
        in_specs=[
            x_blk,
            pl.BlockSpec((SUBLANES, n_chunks, D_CONV), lambda i, b: (i, b, 0)),
            col_blk,
            col_blk,
            const((D_SSM, 1)),
            const((D_SSM, D_SSM)),
            const((D_SSM, 1)),
            const((D_MODEL, D_MODEL)),
            const((1, D_MODEL)),
            const((D_MODEL, D_FF)),
            const((D_FF, D_MODEL)),
            const((1, D_MODEL)),
        ],
        out_specs=x_blk,
        out_shape=jax.ShapeDtypeStruct(x4.shape, F32),
        scratch_shapes=[
            pltpu.VMEM((SUBLANES // MIX_STEPS, MIX_STEPS * n_chunks, D_MODEL), F32),
            pltpu.VMEM((SUBLANES // MIX_STEPS, MIX_STEPS * n_chunks, D_MODEL), BF16),
            pltpu.VMEM((MIX_STEPS * n_chunks, D_FF), BF16),
        ],
        compiler_params=_params(2),
        name="mix_ffn",
    )(x4, conv_out, y_col, u_col, d_col, glu_wt, glu_b_col, w_out, g_ffn, w1, w2, g_final)


def _s5_operator_inputs(lbr, lbi, zr, zi, b_re, b_im, c_re, c_im):
    g, p, h = N_GROUPS, SSM_STATE, SSM_GROUP

    def col_scalar(a):
        a = jnp.transpose(a, (1, 0, 2))
        return jnp.broadcast_to(a[:, :, None, :], (g, 2, 2, p)).reshape(g, 4 * p)

    csc = jnp.stack([col_scalar(a) for a in (lbr, lbi, zr, zi)], axis=-1)

    def col_pair(first, second, perm):
        first = jnp.transpose(first, perm)
        second = jnp.transpose(second, perm)
        return jnp.stack([first, second], axis=2).reshape(g, 4 * p, h)

    c_perm = (1, 0, 3, 2)
    b_perm = (1, 0, 2, 3)
    cb16 = jnp.stack([col_pair(c_re, c_im, c_perm), col_pair(c_im, c_re, c_perm),
                      col_pair(b_re, b_im, b_perm), col_pair(b_im, b_re, b_perm)], axis=1)
    c_row = jnp.transpose(jnp.stack([c_re, c_im], axis=2), (1, 3, 0, 2, 4)).reshape(g, h, 4 * p)
    return csc, cb16, c_row


def kernel(x, meta_tokens, norm_mix_g, w_in, conv_w, conv_b, conv_ln_g, conv_ln_b,
           ssm_lam_re, ssm_lam_im, ssm_log_dt, ssm_b_re, ssm_b_im, ssm_c_re, ssm_c_im,
           ssm_d, ssm_glu_w, ssm_glu_b, w_out, norm_ffn_g, w_ff1, w_ff2, norm_final_g):
    assert w_in.shape[0] == 1, "single-layer block"
    bsz, seq, _ = x.shape
    assert seq % CHUNK == 0 and CHUNK >= N_META
    n_chunks = seq // CHUNK
    g, h = N_GROUPS, SSM_GROUP

    x3 = x.reshape(bsz, n_chunks, CHUNK, D_MODEL)
    g_mix = norm_mix_g[0][None, :]
    w_in_b = w_in[0].astype(BF16)
    w_conv, w_ssm = w_in_b[:, :2 * D_CONV], w_in_b[:, 2 * D_CONV:]
    u_conv, u_col = _in_proj_step(x3, g_mix, w_conv, w_ssm.T)
    u_conv_m, u_ssm_m = _in_proj_meta(meta_tokens, g_mix, w_conv, w_ssm)

    conv_out = _conv_module(u_conv, u_conv_m, conv_w[0], conv_b[0][None, :],
                            conv_ln_g[0][None, :], conv_ln_b[0][None, :], bsz)

    ldt = jnp.broadcast_to(ssm_log_dt[0][..., None], ssm_lam_re[0].shape)
    flat = lambda a: a.reshape(2 * g, SSM_STATE)
    lbr, lbi, zr, zi = [a.reshape(2, g, SSM_STATE) for a in
                        _zoh(flat(ssm_lam_re[0]), flat(ssm_lam_im[0]), flat(ldt))]
    csc, cb16, c_row = _s5_operator_inputs(lbr, lbi, zr, zi, ssm_b_re[0], ssm_b_im[0],
                                           ssm_c_re[0], ssm_c_im[0])
    toep, wend, wout, lam = _chunk_ops(csc, cb16, c_row)

    u_meta = jnp.transpose(u_ssm_m.reshape(N_META, g, h), (1, 0, 2)).reshape(g, N_META * h, 1)
    u_meta = jnp.pad(u_meta, ((0, 0), (CHUNK_LANES - N_META * h, 0), (0, LANES - 1)))
    y_col = _ssm(u_col, u_meta, toep, wend, wout, lam, n_chunks)

    out = _mix_ffn(x3, conv_out, y_col, u_col, ssm_d[0][:, None],
                   ssm_glu_w[0].T.astype(BF16), ssm_glu_b[0][:, None],
                   w_out[0].astype(BF16), norm_ffn_g[0][None, :], w_ff1[0].astype(BF16),
                   w_ff2[0].astype(BF16), norm_final_g[None, :])
    return out.reshape(bsz, seq, D_MODEL)
```

```python
import functools

import jax
import jax.numpy as jnp
from jax import lax
from jax.experimental import pallas as pl
from jax.experimental.pallas import tpu as pltpu

F32 = jnp.float32
BF16 = jnp.bfloat16

D_MODEL = 1024
N_META = 16
D_CONV = 512
D_SSM = 512
CONV_WIDTH = 31
CONV_PAD = CONV_WIDTH // 2
SSM_GROUP = 16
N_GROUPS = D_SSM // SSM_GROUP
SSM_STATE = 64
D_FF = 4096
NORM_EPS = 1e-5
LANES = 128
SUBLANES = 8

CHUNK = 32
CHUNK_LANES = CHUNK * SSM_GROUP
STATE_ROWS = 4 * SSM_STATE
DIR_ROWS = 2 * SSM_STATE
STEPS_PER_VREG = LANES // SSM_GROUP

VMEM_LIMIT_BYTES = 60 * 1024 * 1024


def _rms(x, g):
    return x * lax.rsqrt(jnp.mean(x * x, axis=-1, keepdims=True) + NORM_EPS) * g


def _params(n_axes=1):
    return pltpu.CompilerParams(dimension_semantics=("parallel",) * n_axes,
                                vmem_limit_bytes=VMEM_LIMIT_BYTES)


def _in_proj_step_kernel(x_ref, g_ref, wc_ref, wst_ref, uc_ref, ut_ref):
    n_chunks = x_ref.shape[1]
    x = jnp.concatenate([x_ref[0, :, r, :] for r in range(SUBLANES)], axis=0)
    z = _rms(x, g_ref[...]).astype(BF16)
    p = jnp.dot(z, wc_ref[...], preferred_element_type=F32)
    uc = p[:, :D_CONV] * jax.nn.sigmoid(p[:, D_CONV:])
    uc_ref[...] = uc.reshape(SUBLANES, n_chunks, D_CONV)
    ut = lax.dot_general(wst_ref[...], z, (((1,), (1,)), ((), ())),
                         preferred_element_type=F32).astype(BF16)
    for r in range(SUBLANES):
        ut_ref[:, r * SSM_GROUP:(r + 1) * SSM_GROUP, :] = (
            ut[:, r * n_chunks:(r + 1) * n_chunks].reshape(N_GROUPS, SSM_GROUP, n_chunks))


def _in_proj_step(x4, g, w_conv, w_ssm_t):
    bsz, n_chunks = x4.shape[:2]
    rows = bsz * n_chunks
    return pl.pallas_call(
        _in_proj_step_kernel,
        grid=(CHUNK // SUBLANES, bsz),
        in_specs=[
            pl.BlockSpec((1, n_chunks, SUBLANES, D_MODEL), lambda i, b: (b, 0, i, 0)),
            pl.BlockSpec((1, D_MODEL), lambda i, b: (0, 0)),
            pl.BlockSpec((D_MODEL, 2 * D_CONV), lambda i, b: (0, 0)),
            pl.BlockSpec((D_SSM, D_MODEL), lambda i, b: (0, 0)),
        ],
        out_specs=[
            pl.BlockSpec((SUBLANES, n_chunks, D_CONV), lambda i, b: (i, b, 0)),
            pl.BlockSpec((N_GROUPS, SUBLANES * SSM_GROUP, n_chunks), lambda i, b: (0, i, b)),
        ],
        out_shape=[
            jax.ShapeDtypeStruct((CHUNK, rows, D_CONV), F32),
            jax.ShapeDtypeStruct((N_GROUPS, CHUNK_LANES, rows), BF16),
        ],
        compiler_params=_params(2),
        name="in_proj",
    )(x4, g, w_conv, w_ssm_t)


def _in_proj_meta_kernel(x_ref, g_ref, wc_ref, ws_ref, uc_ref, us_ref):
    z = _rms(x_ref[...], g_ref[...]).astype(BF16)
    p = jnp.dot(z, wc_ref[...], preferred_element_type=F32)
    uc_ref[...] = p[:, :D_CONV] * jax.nn.sigmoid(p[:, D_CONV:])
    us_ref[...] = jnp.dot(z, ws_ref[...], preferred_element_type=F32).astype(BF16)


def _in_proj_meta(meta, g, w_conv, w_ssm):
    return pl.pallas_call(
        _in_proj_meta_kernel,
        out_shape=[jax.ShapeDtypeStruct((N_META, D_CONV), F32),
                   jax.ShapeDtypeStruct((N_META, D_SSM), BF16)],
        name="in_proj_meta",
    )(meta, g, w_conv, w_ssm)


CONV_GROUP = 4
CONV_ROWS = 16


def _conv_kernel(u_ref, um_ref, w_ref, cb_ref, lg_ref, lb_ref, o_ref, cat_ref, acc_ref):
    n_steps, n_chunks, _ = u_ref.shape
    cat_ref[CONV_PAD:CONV_PAD + n_steps] = u_ref[...]
    chunk = lax.broadcasted_iota(jnp.int32, (n_chunks, 1), 0)
    for i in range(CONV_PAD):
        s_prev = n_steps - CONV_PAD + i
        prev = pltpu.roll(u_ref[s_prev], 1, axis=0)
        meta_row = um_ref[N_META - CONV_PAD + i:N_META - CONV_PAD + i + 1, :]
        cat_ref[i] = jnp.where(chunk == 0, meta_row, prev)
        nxt = pltpu.roll(u_ref[i], n_chunks - 1, axis=0)
        cat_ref[n_steps + CONV_PAD + i] = jnp.where(chunk == n_chunks - 1, 0.0, nxt)

    def body(sg, carry):
        s0 = sg * CONV_GROUP
        for c in range(D_CONV // LANES):
            cols = slice(c * LANES, (c + 1) * LANES)
            taps = [jnp.broadcast_to(w_ref[k:k + 1, cols], (SUBLANES, LANES))
                    for k in range(CONV_WIDTH)]

            def rows_body(rt, carry2):
                for sub in range(CONV_ROWS // SUBLANES):
                    r0 = pl.multiple_of(rt * CONV_ROWS + sub * SUBLANES, SUBLANES)
                    accs = [jnp.zeros((SUBLANES, LANES), F32)] * CONV_GROUP
                    for i in range(CONV_WIDTH + CONV_GROUP - 1):
                        d = cat_ref[s0 + i, pl.ds(r0, SUBLANES), cols]
                        for j in range(CONV_GROUP):
                            if 0 <= i - j < CONV_WIDTH:
                                accs[j] = accs[j] + d * taps[i - j]
                    for j in range(CONV_GROUP):
                        acc_ref[j, pl.ds(r0, SUBLANES), cols] = accs[j]
                return carry2

            lax.fori_loop(0, n_chunks // CONV_ROWS, rows_body, 0)
        for j in range(CONV_GROUP):
            y = acc_ref[j] + cb_ref[...]
            yc = y - jnp.mean(y, axis=-1, keepdims=True)
            yn = yc * lax.rsqrt(jnp.mean(yc * yc, axis=-1, keepdims=True) + NORM_EPS)
            yn = yn * lg_ref[...] + lb_ref[...]
            o_ref[s0 + j] = (yn * jax.nn.sigmoid(yn)).astype(BF16)
        return carry

    lax.fori_loop(0, n_steps // CONV_GROUP, body, 0)


def _conv_module(u_conv, u_conv_meta, conv_w, conv_b, ln_g, ln_b, bsz):
    n_steps, rows, _ = u_conv.shape
    n_chunks = rows // bsz
    vec = pl.BlockSpec((1, D_CONV), lambda b: (0, 0))
    blk = pl.BlockSpec((n_steps, n_chunks, D_CONV), lambda b: (0, b, 0))
    return pl.pallas_call(
        _conv_kernel,
        grid=(bsz,),
        in_specs=[
            blk,
            pl.BlockSpec((N_META, D_CONV), lambda b: (0, 0)),
            pl.BlockSpec((CONV_WIDTH, D_CONV), lambda b: (0, 0)),
            vec, vec, vec,
        ],
        out_specs=blk,
        out_shape=jax.ShapeDtypeStruct((n_steps, rows, D_CONV), BF16),
        scratch_shapes=[pltpu.VMEM((n_steps + 2 * CONV_PAD, n_chunks, D_CONV), F32),
                        pltpu.VMEM((CONV_GROUP, n_chunks, D_CONV), F32)],
        compiler_params=_params(),
        name="conv_module",
    )(u_conv, u_conv_meta, conv_w, conv_b, ln_g, ln_b)


def _zoh_kernel(lre_ref, lim_ref, ldt_ref, lbr_ref, lbi_ref, zr_ref, zi_ref):
    lre = lre_ref[...]
    lim = lim_ref[...]
    dt = jnp.exp(ldt_ref[...])
    ea = jnp.exp(lre * dt)
    lbr = ea * jnp.cos(lim * dt)
    lbi = ea * jnp.sin(lim * dt)
    nr = lbr - 1.0
    den = lre * lre + lim * lim
    lbr_ref[...] = lbr
    lbi_ref[...] = lbi
    zr_ref[...] = (nr * lre + lbi * lim) / den
    zi_ref[...] = (lbi * lre - nr * lim) / den


def _zoh(lam_re, lam_im, log_dt):
    shape = jax.ShapeDtypeStruct(lam_re.shape, F32)
    return pl.pallas_call(_zoh_kernel, out_shape=[shape] * 4, name="s5_zoh")(
        lam_re, lam_im, log_dt)


def _lanes(x, n):
    return jnp.concatenate([x] * (n // LANES), axis=1)


def _cmul(ar, ai, br, bi):
    return ar * br - ai * bi, ar * bi + ai * br


def _cpow(br, bi, expo, nbits):
    rr = jnp.ones(expo.shape, F32)
    ri = jnp.zeros(expo.shape, F32)
    for k in range(nbits):
        bit = ((expo >> k) & 1) == 1
        nr, ni = _cmul(rr, ri, br, bi)
        rr = jnp.where(bit, nr, rr)
        ri = jnp.where(bit, ni, ri)
        br, bi = _cmul(br, bi, br, bi)
    return rr, ri


def _build_chunk_ops(gi, csc_ref, cb_ref, cr_ref, toep_ref, wend_ref, wout_ref):
    t = CHUNK
    rows = STATE_ROWS
    n_cols = CHUNK_LANES // LANES
    lam = (jnp.broadcast_to(csc_ref[gi, :, 0:1], (rows, LANES)),
           jnp.broadcast_to(csc_ref[gi, :, 1:2], (rows, LANES)))
    zr = jnp.broadcast_to(csc_ref[gi, :, 2:3], (rows, LANES))
    zi = jnp.broadcast_to(csc_ref[gi, :, 3:4], (rows, LANES))
    sel = (lax.broadcasted_iota(jnp.int32, (SSM_GROUP, LANES), 1) % SSM_GROUP
           == lax.broadcasted_iota(jnp.int32, (SSM_GROUP, LANES), 0)).astype(F32)

    def tile(i):
        return jnp.dot(cb_ref[gi, i], sel, precision=lax.Precision.HIGHEST,
                       preferred_element_type=F32)

    def swap_re_im(a):
        p = SSM_STATE
        return jnp.concatenate([a[p:2 * p], a[0:p], a[3 * p:], a[2 * p:3 * p]], axis=0)

    c_same, b_same = tile(0), tile(1)
    c_swap, b_swap = swap_re_im(c_same), swap_re_im(b_same)
    row = lax.broadcasted_iota(jnp.int32, (rows, LANES), 0)
    is_re = (row // SSM_STATE) % 2 == 0
    sgn = jnp.where(is_re, -1.0, 1.0)
    ca = jnp.where(is_re, c_same, -c_same)
    cb = -c_swap
    bb_same = zr * b_same + sgn * zi * b_swap
    bb_swap = zr * b_swap - sgn * zi * b_same

    pows = {1: lam}
    k = 1
    while k < t:
        pows[2 * k] = _cmul(*pows[k], *pows[k])
        k *= 2
    step_bits = STEPS_PER_VREG.bit_length() - 1

    def col_pow(e):
        out = None
        for bit, val in pows.items():
            if e & bit:
                out = val if out is None else _cmul(*out, *val)
        return out

    i8 = lax.broadcasted_iota(jnp.int32, (rows, LANES), 1) // SSM_GROUP
    asc0 = _cpow(*lam, i8, step_bits)
    desc0 = _cpow(*lam, STEPS_PER_VREG - 1 - i8, step_bits)

    def table(base, exps):
        cols = [base if e == 0 else _cmul(*base, *col_pow(e)) for e in exps]
        return (jnp.concatenate([c[0] for c in cols], axis=1),
                jnp.concatenate([c[1] for c in cols], axis=1))

    asc = table(asc0, [STEPS_PER_VREG * q for q in range(n_cols)])
    desc = table(desc0, [STEPS_PER_VREG * (n_cols - 1 - q) for q in range(n_cols)])
    lam_w = (_lanes(lam[0], CHUNK_LANES), _lanes(lam[1], CHUNK_LANES))
    asc1 = _cmul(*asc, *lam_w)
    desc1 = _cmul(*desc, *lam_w)

    fwd = slice(0, DIR_ROWS)
    bwd = slice(DIR_ROWS, rows)

    def times_b(p, rs):
        return (p[0][rs] * _lanes(bb_same[rs], CHUNK_LANES)
                + _lanes(sgn[rs], CHUNK_LANES) * p[1][rs] * _lanes(bb_swap[rs], CHUNK_LANES))

    def times_c(p, rs):
        return (_lanes(ca[rs], CHUNK_LANES) * p[0][rs]
                + _lanes(cb[rs], CHUNK_LANES) * p[1][rs])

    wend_f = times_b(desc, fwd)
    wend_b = times_b(asc, bwd)
    wend_ref[gi, fwd, :] = wend_f.astype(BF16)
    wend_ref[gi, bwd, :] = wend_b.astype(BF16)

    lane_w = lax.broadcasted_iota(jnp.int32, (DIR_ROWS, CHUNK_LANES), 1)
    lag0_b = jnp.where(lane_w >= CHUNK_LANES - SSM_GROUP,
                       _lanes(bb_same[bwd], CHUNK_LANES), 0.0)
    bcat = jnp.concatenate(
        [jnp.concatenate([wend_f, jnp.zeros_like(wend_f)], axis=1),
         jnp.concatenate([lag0_b, times_b(asc1, bwd)], axis=1)], axis=0)
    lane_r = lax.broadcasted_iota(jnp.int32, (SSM_GROUP, rows), 1)
    c2 = jnp.where((lane_r // SSM_STATE) % 2 == 0, cr_ref[gi], -cr_ref[gi])
    kk = jnp.dot(c2, bcat, precision=lax.Precision.HIGHEST,
                 preferred_element_type=F32)
    for tt in range(t):
        off = (t - 1 - tt) * SSM_GROUP
        toep_ref[gi, tt * SSM_GROUP:(tt + 1) * SSM_GROUP, :] = (
            kk[:, off:off + CHUNK_LANES].astype(BF16))

    wout = jnp.concatenate([times_c(asc1, fwd), times_c(desc1, bwd)], axis=0)
    wout_ref[gi] = wout.T.astype(BF16)
    return pows[t]


SSM_GROUPS_PER_STEP = 2


def _chunk_scan(er, ei, ar, ai, x0, chunk, n_chunks, forward):
    width = er.shape[1]
    xr, xi = er, ei
    if x0 is not None:
        fr, fi = _cmul(ar, ai, x0[0], x0[1])
        xr = xr + jnp.where(chunk == 0, fr, 0.0)
        xi = xi + jnp.where(chunk == 0, fi, 0.0)

    def shifted(v, sh):
        if forward:
            return jnp.where(chunk >= sh, pltpu.roll(v, sh, axis=1), 0.0)
        return jnp.where(chunk < n_chunks - sh, pltpu.roll(v, width - sh, axis=1), 0.0)

    sh = 1
    while sh < n_chunks:
        sr, si = shifted(xr, sh), shifted(xi, sh)
        xr, xi = xr + ar * sr - ai * si, xi + ar * si + ai * sr
        ar, ai = _cmul(ar, ai, ar, ai)
        sh *= 2
    inr, ini = shifted(xr, 1), shifted(xi, 1)
    if x0 is not None:
        inr = jnp.where(chunk == 0, x0[0], inr)
        ini = jnp.where(chunk == 0, x0[1], ini)
    return inr, ini


def _ssm_kernel(u_ref, um_ref, csc_ref, cb_ref, cr_ref, y_ref, toep_ref, wend_ref, wout_ref,
                *, n_chunks):
    for gi in range(SSM_GROUPS_PER_STEP):
        lam_r, lam_i = _build_chunk_ops(gi, csc_ref, cb_ref, cr_ref, toep_ref, wend_ref,
                                        wout_ref)
        u = u_ref[gi]
        wend = wend_ref[gi]
        width = u.shape[1]
        e = jnp.dot(wend, u, preferred_element_type=F32)
        x0 = jnp.dot(wend, um_ref[gi], preferred_element_type=F32)[:, 0:1]
        chunk = lax.broadcasted_iota(jnp.int32, (1, width), 1) % n_chunks
        p = SSM_STATE
        ar = _lanes(lam_r, width)
        ai = _lanes(lam_i, width)
        f_in = _chunk_scan(e[0:p], e[p:2 * p], ar[0:p], ai[0:p], (x0[0:p], x0[p:2 * p]),
                           chunk, n_chunks, True)
        b_in = _chunk_scan(e[2 * p:3 * p], e[3 * p:], ar[2 * p:3 * p], ai[2 * p:3 * p],
                           None, chunk, n_chunks, False)
        xin = jnp.concatenate([f_in[0], f_in[1], b_in[0], b_in[1]], axis=0).astype(BF16)
        y = jnp.dot(toep_ref[gi], u, preferred_element_type=F32)
        y = y + jnp.dot(wout_ref[gi], xin, preferred_element_type=F32)
        y_ref[gi] = y.astype(BF16)


def _ssm(u_col, u_meta, csc, cb16, c_row, n_chunks):
    g, _, width = u_col.shape
    per = SSM_GROUPS_PER_STEP
    return pl.pallas_call(
        functools.partial(_ssm_kernel, n_chunks=n_chunks),
        grid=(g // per,),
        in_specs=[
            pl.BlockSpec((per, CHUNK_LANES, width), lambda i: (i, 0, 0)),
            pl.BlockSpec((per, CHUNK_LANES, LANES), lambda i: (i, 0, 0)),
            pl.BlockSpec((per, STATE_ROWS, 4), lambda i: (i, 0, 0)),
            pl.BlockSpec((per, 2, STATE_ROWS, SSM_GROUP), lambda i: (i, 0, 0, 0)),
            pl.BlockSpec((per, SSM_GROUP, STATE_ROWS), lambda i: (i, 0, 0)),
        ],
        out_specs=pl.BlockSpec((per, CHUNK_LANES, width), lambda i: (i, 0, 0)),
        out_shape=jax.ShapeDtypeStruct((g, CHUNK_LANES, width), BF16),
        scratch_shapes=[
            pltpu.VMEM((per, CHUNK_LANES, CHUNK_LANES), BF16),
            pltpu.VMEM((per, STATE_ROWS, CHUNK_LANES), BF16),
            pltpu.VMEM((per, CHUNK_LANES, STATE_ROWS), BF16),
        ],
        compiler_params=_params(),
        name="s5_mixer",
    )(u_col, u_meta, csc, cb16, c_row)


FF_CHUNK = 1024
MIX_STEPS = 4


def _mix_ffn_kernel(x_ref, co_ref, y_ref, ut_ref, d_ref, gwt_ref, gb_ref, wo_ref,
                    gf_ref, w1_ref, w2_ref, gl_ref, o_ref, h_ref, z_ref, a_ref):
    n_chunks = x_ref.shape[1]
    n_parts = SUBLANES // MIX_STEPS

    def steps_of(part):
        return range(part * MIX_STEPS, (part + 1) * MIX_STEPS)

    def mix(part):
        def col(ref):
            return jnp.concatenate(
                [ref[:, r * SSM_GROUP:(r + 1) * SSM_GROUP, :].reshape(D_SSM, n_chunks)
                 for r in steps_of(part)], axis=1).astype(F32)

        yt = col(y_ref) + d_ref[...] * col(ut_ref)
        ge = jax.nn.gelu(yt)
        gate = (jnp.dot(gwt_ref[...], ge.astype(BF16), preferred_element_type=F32)
                + gb_ref[...])
        s5 = (ge * jax.nn.sigmoid(gate)).T.astype(BF16)
        co = co_ref[part * MIX_STEPS:(part + 1) * MIX_STEPS].reshape(
            MIX_STEPS * n_chunks, D_CONV)
        x = jnp.concatenate([x_ref[0, :, r, :] for r in steps_of(part)], axis=0)
        h = x + jnp.dot(jnp.concatenate([co, s5], axis=1), wo_ref[...],
                        preferred_element_type=F32)
        h_ref[part] = h
        z_ref[part] = _rms(h, gf_ref[...]).astype(BF16)

    def ffn_up(part, j):
        cols = slice(j * FF_CHUNK, (j + 1) * FF_CHUNK)
        a = jnp.dot(z_ref[part], w1_ref[:, cols], preferred_element_type=F32)
        a_ref[:, cols] = jnp.square(jnp.maximum(a, 0.0)).astype(BF16)

    def ffn_down(part):
        h = h_ref[part] + jnp.dot(a_ref[...], w2_ref[...], preferred_element_type=F32)
        out = _rms(h, gl_ref[...])
        for k, r in enumerate(steps_of(part)):
            o_ref[0, :, r, :] = out[k * n_chunks:(k + 1) * n_chunks, :]

    for part in range(n_parts):
        mix(part)
        for j in range(D_FF // FF_CHUNK):
            ffn_up(part, j)
        ffn_down(part)


def _mix_ffn(x4, conv_out, y_col, u_col, d_col, glu_wt, glu_b_col, w_out, g_ffn, w1, w2,
             g_final):
    bsz, n_chunks = x4.shape[:2]

    def const(shape):
        return pl.BlockSpec(shape, lambda i, b: (0, 0), pipeline_mode=pl.Buffered(1))

    x_blk = pl.BlockSpec((1, n_chunks, SUBLANES, D_MODEL), lambda i, b: (b, 0, i, 0))
    col_blk = pl.BlockSpec((N_GROUPS, SUBLANES * SSM_GROUP, n_chunks),
                           lambda i, b: (0, i, b))
    return pl.pallas_call(
        _mix_ffn_kernel,
        grid=(CHUNK // SUBLANES, bsz),
        in_specs=[
            x_blk,
            pl.BlockSpec((SUBLANES, n_chunks, D_CONV), lambda i, b: (i, b, 0)),
            col_blk,
            col_blk,
            const((D_SSM, 1)),
            const((D_SSM, D_SSM)),
            const((D_SSM, 1)),
            const((D_MODEL, D_MODEL)),
            const((1, D_MODEL)),
            const((D_MODEL, D_FF)),
            const((D_FF, D_MODEL)),
            const((1, D_MODEL)),
        ],
        out_specs=x_blk,
        out_shape=jax.ShapeDtypeStruct(x4.shape, F32),
        scratch_shapes=[
            pltpu.VMEM((SUBLANES // MIX_STEPS, MIX_STEPS * n_chunks, D_MODEL), F32),
            pltpu.VMEM((SUBLANES // MIX_STEPS, MIX_STEPS * n_chunks, D_MODEL), BF16),
            pltpu.VMEM((MIX_STEPS * n_chunks, D_FF), BF16),
        ],
        compiler_params=_params(2),
        name="mix_ffn",
    )(x4, conv_out, y_col, u_col, d_col, glu_wt, glu_b_col, w_out, g_ffn, w1, w2, g_final)


def _s5_operator_inputs(lbr, lbi, zr, zi, b_re, b_im, c_re, c_im):
    g, p, h = N_GROUPS, SSM_STATE, SSM_GROUP

    def col_scalar(a):
        a = jnp.transpose(a, (1, 0, 2))
        return jnp.broadcast_to(a[:, :, None, :], (g, 2, 2, p)).reshape(g, 4 * p)

    csc = jnp.stack([col_scalar(a) for a in (lbr, lbi, zr, zi)], axis=-1)

    def col_pair(first, second, perm):
        first = jnp.transpose(first, perm)
        second = jnp.transpose(second, perm)
        return jnp.stack([first, second], axis=2).reshape(g, 4 * p, h)

    c_perm = (1, 0, 3, 2)
    b_perm = (1, 0, 2, 3)
    cb16 = jnp.stack([col_pair(c_re, c_im, c_perm), col_pair(b_re, b_im, b_perm)], axis=1)
    c_row = jnp.transpose(jnp.stack([c_re, c_im], axis=2), (1, 3, 0, 2, 4)).reshape(g, h, 4 * p)
    return csc, cb16, c_row


def kernel(x, meta_tokens, norm_mix_g, w_in, conv_w, conv_b, conv_ln_g, conv_ln_b,
           ssm_lam_re, ssm_lam_im, ssm_log_dt, ssm_b_re, ssm_b_im, ssm_c_re, ssm_c_im,
           ssm_d, ssm_glu_w, ssm_glu_b, w_out, norm_ffn_g, w_ff1, w_ff2, norm_final_g):
    assert w_in.shape[0] == 1, "single-layer block"
    bsz, seq, _ = x.shape
    assert seq % CHUNK == 0 and CHUNK >= N_META
    n_chunks = seq // CHUNK
    g, h = N_GROUPS, SSM_GROUP

    x4 = x.reshape(bsz, n_chunks, CHUNK, D_MODEL)
    g_mix = norm_mix_g[0][None, :]
    w_in_b = w_in[0].astype(BF16)
    w_conv, w_ssm = w_in_b[:, :2 * D_CONV], w_in_b[:, 2 * D_CONV:]
    u_conv, u_col = _in_proj_step(x4, g_mix, w_conv, w_ssm.T)
    u_conv_m, u_ssm_m = _in_proj_meta(meta_tokens, g_mix, w_conv, w_ssm)

    conv_out = _conv_module(u_conv, u_conv_m, conv_w[0], conv_b[0][None, :],
                            conv_ln_g[0][None, :], conv_ln_b[0][None, :], bsz)

    ldt = jnp.broadcast_to(ssm_log_dt[0][..., None], ssm_lam_re[0].shape)
    flat = lambda a: a.reshape(2 * g, SSM_STATE)
    lbr, lbi, zr, zi = [a.reshape(2, g, SSM_STATE) for a in
                        _zoh(flat(ssm_lam_re[0]), flat(ssm_lam_im[0]), flat(ldt))]
    csc, cb16, c_row = _s5_operator_inputs(lbr, lbi, zr, zi, ssm_b_re[0], ssm_b_im[0],
                                           ssm_c_re[0], ssm_c_im[0])

    u_meta = jnp.transpose(u_ssm_m.reshape(N_META, g, h), (1, 0, 2)).reshape(g, N_META * h, 1)
    u_meta = jnp.pad(u_meta, ((0, 0), (CHUNK_LANES - N_META * h, 0), (0, LANES - 1)))
    y_col = _ssm(u_col, u_meta, csc, cb16, c_row, n_chunks)

    out = _mix_ffn(x4, conv_out, y_col, u_col, ssm_d[0][:, None],
                   ssm_glu_w[0].T.astype(BF16), ssm_glu_b[0][:, None],
                   w_out[0].astype(BF16), norm_ffn_g[0][None, :], w_ff1[0].astype(BF16),
                   w_ff2[0].astype(BF16), norm_final_g[None, :])
    return out.reshape(bsz, seq, D_MODEL)
```

```python
import functools

import jax
import jax.numpy as jnp
from jax import lax
from jax.experimental import pallas as pl
from jax.experimental.pallas import tpu as pltpu

F32 = jnp.float32
BF16 = jnp.bfloat16

D_MODEL = 1024
N_META = 16
D_CONV = 512
D_SSM = 512
CONV_WIDTH = 31
CONV_PAD = CONV_WIDTH // 2
SSM_GROUP = 16
N_GROUPS = D_SSM // SSM_GROUP
SSM_STATE = 64
D_FF = 4096
NORM_EPS = 1e-5
LANES = 128
SUBLANES = 8

CHUNK = 32
CHUNK_LANES = CHUNK * SSM_GROUP
STATE_ROWS = 4 * SSM_STATE
DIR_ROWS = 2 * SSM_STATE
STEPS_PER_VREG = LANES // SSM_GROUP

VMEM_LIMIT_BYTES = 60 * 1024 * 1024


def _rms(x, g):
    return x * lax.rsqrt(jnp.mean(x * x, axis=-1, keepdims=True) + NORM_EPS) * g


def _params(n_axes=1):
    return pltpu.CompilerParams(dimension_semantics=("parallel",) * n_axes,
                                vmem_limit_bytes=VMEM_LIMIT_BYTES)


def _in_proj_step_kernel(x_ref, g_ref, w_ref, uc_ref, ut_ref):
    n_chunks = x_ref.shape[1]
    x = jnp.concatenate([x_ref[0, :, r, :] for r in range(SUBLANES)], axis=0)
    z = _rms(x, g_ref[...]).astype(BF16)
    p = jnp.dot(z, w_ref[...], preferred_element_type=F32)
    uc = p[:, :D_CONV] * jax.nn.sigmoid(p[:, D_CONV:2 * D_CONV])
    uc_ref[...] = uc.reshape(SUBLANES, n_chunks, D_CONV)
    ut = p[:, 2 * D_CONV:].T.astype(BF16)
    for r in range(SUBLANES):
        ut_ref[:, r * SSM_GROUP:(r + 1) * SSM_GROUP, :] = (
            ut[:, r * n_chunks:(r + 1) * n_chunks].reshape(N_GROUPS, SSM_GROUP, n_chunks))


def _in_proj_step(x4, g, w_in):
    bsz, n_chunks = x4.shape[:2]
    rows = bsz * n_chunks
    return pl.pallas_call(
        _in_proj_step_kernel,
        grid=(CHUNK // SUBLANES, bsz),
        in_specs=[
            pl.BlockSpec((1, n_chunks, SUBLANES, D_MODEL), lambda i, b: (b, 0, i, 0)),
            pl.BlockSpec((1, D_MODEL), lambda i, b: (0, 0)),
            pl.BlockSpec((D_MODEL, 2 * D_CONV + D_SSM), lambda i, b: (0, 0)),
        ],
        out_specs=[
            pl.BlockSpec((SUBLANES, n_chunks, D_CONV), lambda i, b: (i, b, 0)),
            pl.BlockSpec((N_GROUPS, SUBLANES * SSM_GROUP, n_chunks), lambda i, b: (0, i, b)),
        ],
        out_shape=[
            jax.ShapeDtypeStruct((CHUNK, rows, D_CONV), F32),
            jax.ShapeDtypeStruct((N_GROUPS, CHUNK_LANES, rows), BF16),
        ],
        compiler_params=_params(2),
        name="in_proj",
    )(x4, g, w_in)


def _in_proj_meta_kernel(x_ref, g_ref, w_ref, uc_ref, us_ref):
    z = _rms(x_ref[...], g_ref[...]).astype(BF16)
    p = jnp.dot(z, w_ref[...], preferred_element_type=F32)
    uc_ref[...] = p[:, :D_CONV] * jax.nn.sigmoid(p[:, D_CONV:2 * D_CONV])
    us_ref[...] = p[:, 2 * D_CONV:]


def _in_proj_meta(meta, g, w_in):
    return pl.pallas_call(
        _in_proj_meta_kernel,
        out_shape=[jax.ShapeDtypeStruct((N_META, D_CONV), F32),
                   jax.ShapeDtypeStruct((N_META, D_SSM), F32)],
        name="in_proj_meta",
    )(meta, g, w_in)


CONV_GROUP = 4
CONV_ROWS = 16


def _conv_kernel(u_ref, um_ref, w_ref, cb_ref, lg_ref, lb_ref, o_ref, cat_ref, acc_ref):
    n_steps, n_chunks, _ = u_ref.shape
    cat_ref[CONV_PAD:CONV_PAD + n_steps] = u_ref[...]
    chunk = lax.broadcasted_iota(jnp.int32, (n_chunks, 1), 0)
    for i in range(CONV_PAD):
        s_prev = n_steps - CONV_PAD + i
        prev = pltpu.roll(u_ref[s_prev], 1, axis=0)
        meta_row = um_ref[N_META - CONV_PAD + i:N_META - CONV_PAD + i + 1, :]
        cat_ref[i] = jnp.where(chunk == 0, meta_row, prev)
        nxt = pltpu.roll(u_ref[i], n_chunks - 1, axis=0)
        cat_ref[n_steps + CONV_PAD + i] = jnp.where(chunk == n_chunks - 1, 0.0, nxt)

    def body(sg, carry):
        s0 = sg * CONV_GROUP
        for c in range(D_CONV // LANES):
            cols = slice(c * LANES, (c + 1) * LANES)
            taps = [jnp.broadcast_to(w_ref[k:k + 1, cols], (SUBLANES, LANES))
                    for k in range(CONV_WIDTH)]

            def rows_body(rt, carry2):
                for sub in range(CONV_ROWS // SUBLANES):
                    r0 = pl.multiple_of(rt * CONV_ROWS + sub * SUBLANES, SUBLANES)
                    accs = [jnp.zeros((SUBLANES, LANES), F32)] * CONV_GROUP
                    for i in range(CONV_WIDTH + CONV_GROUP - 1):
                        d = cat_ref[s0 + i, pl.ds(r0, SUBLANES), cols]
                        for j in range(CONV_GROUP):
                            if 0 <= i - j < CONV_WIDTH:
                                accs[j] = accs[j] + d * taps[i - j]
                    for j in range(CONV_GROUP):
                        acc_ref[j, pl.ds(r0, SUBLANES), cols] = accs[j]
                return carry2

            lax.fori_loop(0, n_chunks // CONV_ROWS, rows_body, 0)
        for j in range(CONV_GROUP):
            y = acc_ref[j] + cb_ref[...]
            yc = y - jnp.mean(y, axis=-1, keepdims=True)
            yn = yc * lax.rsqrt(jnp.mean(yc * yc, axis=-1, keepdims=True) + NORM_EPS)
            yn = yn * lg_ref[...] + lb_ref[...]
            o_ref[s0 + j] = (yn * jax.nn.sigmoid(yn)).astype(BF16)
        return carry

    lax.fori_loop(0, n_steps // CONV_GROUP, body, 0)


def _conv_module(u_conv, u_conv_meta, conv_w, conv_b, ln_g, ln_b, bsz):
    n_steps, rows, _ = u_conv.shape
    n_chunks = rows // bsz
    vec = pl.BlockSpec((1, D_CONV), lambda b: (0, 0))
    blk = pl.BlockSpec((n_steps, n_chunks, D_CONV), lambda b: (0, b, 0))
    return pl.pallas_call(
        _conv_kernel,
        grid=(bsz,),
        in_specs=[
            blk,
            pl.BlockSpec((N_META, D_CONV), lambda b: (0, 0)),
            pl.BlockSpec((CONV_WIDTH, D_CONV), lambda b: (0, 0)),
            vec, vec, vec,
        ],
        out_specs=blk,
        out_shape=jax.ShapeDtypeStruct((n_steps, rows, D_CONV), BF16),
        scratch_shapes=[pltpu.VMEM((n_steps + 2 * CONV_PAD, n_chunks, D_CONV), F32),
                        pltpu.VMEM((CONV_GROUP, n_chunks, D_CONV), F32)],
        compiler_params=_params(),
        name="conv_module",
    )(u_conv, u_conv_meta, conv_w, conv_b, ln_g, ln_b)


def _zoh_kernel(lre_ref, lim_ref, ldt_ref, lbr_ref, lbi_ref, zr_ref, zi_ref):
    lre = lre_ref[...]
    lim = lim_ref[...]
    dt = jnp.exp(ldt_ref[...])
    ea = jnp.exp(lre * dt)
    lbr = ea * jnp.cos(lim * dt)
    lbi = ea * jnp.sin(lim * dt)
    nr = lbr - 1.0
    den = lre * lre + lim * lim
    lbr_ref[...] = lbr
    lbi_ref[...] = lbi
    zr_ref[...] = (nr * lre + lbi * lim) / den
    zi_ref[...] = (lbi * lre - nr * lim) / den


def _zoh(lam_re, lam_im, log_dt):
    shape = jax.ShapeDtypeStruct(lam_re.shape, F32)
    return pl.pallas_call(_zoh_kernel, out_shape=[shape] * 4, name="s5_zoh")(
        lam_re, lam_im, log_dt)


def _lanes(x, n):
    return jnp.concatenate([x] * (n // LANES), axis=1)


def _cmul(ar, ai, br, bi):
    return ar * br - ai * bi, ar * bi + ai * br


def _cpow(br, bi, expo, nbits):
    rr = jnp.ones(expo.shape, F32)
    ri = jnp.zeros(expo.shape, F32)
    for k in range(nbits):
        bit = ((expo >> k) & 1) == 1
        nr, ni = _cmul(rr, ri, br, bi)
        rr = jnp.where(bit, nr, rr)
        ri = jnp.where(bit, ni, ri)
        br, bi = _cmul(br, bi, br, bi)
    return rr, ri


def _build_chunk_ops(gi, csc_ref, cc_ref, bc_ref, cr_ref, um_ref, toep_ref, wend_ref,
                     wout_ref):
    t = CHUNK
    rows = STATE_ROWS
    n_cols = CHUNK_LANES // LANES
    lam = (jnp.broadcast_to(csc_ref[gi, :, 0:1], (rows, LANES)),
           jnp.broadcast_to(csc_ref[gi, :, 1:2], (rows, LANES)))
    zr = jnp.broadcast_to(csc_ref[gi, :, 2:3], (rows, LANES))
    zi = jnp.broadcast_to(csc_ref[gi, :, 3:4], (rows, LANES))
    sel = (lax.broadcasted_iota(jnp.int32, (SSM_GROUP, LANES), 1) % SSM_GROUP
           == lax.broadcasted_iota(jnp.int32, (SSM_GROUP, LANES), 0)).astype(F32)

    def tile(ref):
        return jnp.dot(ref[gi], sel, precision=lax.Precision.HIGHEST,
                       preferred_element_type=F32)

    def swap_re_im(a):
        p = SSM_STATE
        return jnp.concatenate([a[p:2 * p], a[0:p], a[3 * p:], a[2 * p:3 * p]], axis=0)

    c_same, b_same = tile(cc_ref), tile(bc_ref)
    c_swap, b_swap = swap_re_im(c_same), swap_re_im(b_same)
    row = lax.broadcasted_iota(jnp.int32, (rows, LANES), 0)
    is_re = (row // SSM_STATE) % 2 == 0
    sgn = jnp.where(is_re, -1.0, 1.0)
    ca = jnp.where(is_re, c_same, -c_same)
    cb = -c_swap
    bb_same = zr * b_same + sgn * zi * b_swap
    bb_swap = zr * b_swap - sgn * zi * b_same

    pows = {1: lam}
    k = 1
    while k < t:
        pows[2 * k] = _cmul(*pows[k], *pows[k])
        k *= 2
    step_bits = STEPS_PER_VREG.bit_length() - 1

    def col_pow(e):
        out = None
        for bit, val in pows.items():
            if e & bit:
                out = val if out is None else _cmul(*out, *val)
        return out

    i8 = lax.broadcasted_iota(jnp.int32, (rows, LANES), 1) // SSM_GROUP
    asc0 = _cpow(*lam, i8, step_bits)
    desc0 = _cpow(*lam, STEPS_PER_VREG - 1 - i8, step_bits)

    def table(base, exps):
        cols = [base if e == 0 else _cmul(*base, *col_pow(e)) for e in exps]
        return (jnp.concatenate([c[0] for c in cols], axis=1),
                jnp.concatenate([c[1] for c in cols], axis=1))

    asc = table(asc0, [STEPS_PER_VREG * q for q in range(n_cols)])
    desc = table(desc0, [STEPS_PER_VREG * (n_cols - 1 - q) for q in range(n_cols)])
    lam_w = (_lanes(lam[0], CHUNK_LANES), _lanes(lam[1], CHUNK_LANES))
    asc1 = _cmul(*asc, *lam_w)
    desc1 = _cmul(*desc, *lam_w)

    fwd = slice(0, DIR_ROWS)
    bwd = slice(DIR_ROWS, rows)

    def times_b(p, rs):
        return (p[0][rs] * _lanes(bb_same[rs], CHUNK_LANES)
                + _lanes(sgn[rs], CHUNK_LANES) * p[1][rs] * _lanes(bb_swap[rs], CHUNK_LANES))

    def times_c(p, rs):
        return (_lanes(ca[rs], CHUNK_LANES) * p[0][rs]
                + _lanes(cb[rs], CHUNK_LANES) * p[1][rs])

    wend_f = times_b(desc, fwd)
    wend_b = times_b(asc, bwd)
    wend_ref[gi, fwd, :] = wend_f.astype(BF16)
    wend_ref[gi, bwd, :] = wend_b.astype(BF16)
    meta_lanes = N_META * SSM_GROUP
    x0 = jnp.sum(wend_f[:, CHUNK_LANES - meta_lanes:] * um_ref[gi], axis=1, keepdims=True)

    lane_w = lax.broadcasted_iota(jnp.int32, (DIR_ROWS, CHUNK_LANES), 1)
    lag0_b = jnp.where(lane_w >= CHUNK_LANES - SSM_GROUP,
                       _lanes(bb_same[bwd], CHUNK_LANES), 0.0)
    bcat = jnp.concatenate(
        [jnp.concatenate([wend_f, jnp.zeros_like(wend_f)], axis=1),
         jnp.concatenate([lag0_b, times_b(asc1, bwd)], axis=1)], axis=0)
    lane_r = lax.broadcasted_iota(jnp.int32, (SSM_GROUP, rows), 1)
    c2 = jnp.where((lane_r // SSM_STATE) % 2 == 0, cr_ref[gi], -cr_ref[gi])
    kk = jnp.dot(c2, bcat, precision=lax.Precision.HIGHEST,
                 preferred_element_type=F32)
    for tt in range(t):
        off = (t - 1 - tt) * SSM_GROUP
        toep_ref[gi, tt * SSM_GROUP:(tt + 1) * SSM_GROUP, :] = (
            kk[:, off:off + CHUNK_LANES].astype(BF16))

    wout = jnp.concatenate([times_c(asc1, fwd), times_c(desc1, bwd)], axis=0)
    wout_ref[gi] = wout.T.astype(BF16)
    return pows[t], x0


SSM_GROUPS_PER_STEP = 2


def _chunk_scan(er, ei, ar, ai, x0, chunk, n_chunks, forward):
    width = er.shape[1]
    xr, xi = er, ei
    if x0 is not None:
        fr, fi = _cmul(ar, ai, x0[0], x0[1])
        xr = xr + jnp.where(chunk == 0, fr, 0.0)
        xi = xi + jnp.where(chunk == 0, fi, 0.0)

    def shifted(v, sh):
        if forward:
            return jnp.where(chunk >= sh, pltpu.roll(v, sh, axis=1), 0.0)
        return jnp.where(chunk < n_chunks - sh, pltpu.roll(v, width - sh, axis=1), 0.0)

    sh = 1
    while sh < n_chunks:
        sr, si = shifted(xr, sh), shifted(xi, sh)
        xr, xi = xr + ar * sr - ai * si, xi + ar * si + ai * sr
        ar, ai = _cmul(ar, ai, ar, ai)
        sh *= 2
    inr, ini = shifted(xr, 1), shifted(xi, 1)
    if x0 is not None:
        inr = jnp.where(chunk == 0, x0[0], inr)
        ini = jnp.where(chunk == 0, x0[1], ini)
    return inr, ini


def _ssm_kernel(u_ref, um_ref, csc_ref, cc_ref, bc_ref, cr_ref, y_ref, toep_ref, wend_ref,
                wout_ref, *, n_chunks):
    for gi in range(SSM_GROUPS_PER_STEP):
        (lam_r, lam_i), x0 = _build_chunk_ops(gi, csc_ref, cc_ref, bc_ref, cr_ref, um_ref,
                                              toep_ref, wend_ref, wout_ref)
        u = u_ref[gi]
        width = u.shape[1]
        e = jnp.dot(wend_ref[gi], u, preferred_element_type=F32)
        chunk = lax.broadcasted_iota(jnp.int32, (1, width), 1) % n_chunks
        p = SSM_STATE
        ar = _lanes(lam_r, width)
        ai = _lanes(lam_i, width)
        f_in = _chunk_scan(e[0:p], e[p:2 * p], ar[0:p], ai[0:p], (x0[0:p], x0[p:2 * p]),
                           chunk, n_chunks, True)
        b_in = _chunk_scan(e[2 * p:3 * p], e[3 * p:], ar[2 * p:3 * p], ai[2 * p:3 * p],
                           None, chunk, n_chunks, False)
        xin = jnp.concatenate([f_in[0], f_in[1], b_in[0], b_in[1]], axis=0).astype(BF16)
        y = jnp.dot(toep_ref[gi], u, preferred_element_type=F32)
        y = y + jnp.dot(wout_ref[gi], xin, preferred_element_type=F32)
        y_ref[gi] = y.astype(BF16)


def _ssm(u_col, u_meta, csc, c_col, b_col, c_row, n_chunks):
    g, _, width = u_col.shape
    per = SSM_GROUPS_PER_STEP
    return pl.pallas_call(
        functools.partial(_ssm_kernel, n_chunks=n_chunks),
        grid=(g // per,),
        in_specs=[
            pl.BlockSpec((per, CHUNK_LANES, width), lambda i: (i, 0, 0)),
            pl.BlockSpec((per, 1, N_META * SSM_GROUP), lambda i: (i, 0, 0)),
            pl.BlockSpec((per, STATE_ROWS, 4), lambda i: (i, 0, 0)),
            pl.BlockSpec((per, STATE_ROWS, SSM_GROUP), lambda i: (i, 0, 0)),
            pl.BlockSpec((per, STATE_ROWS, SSM_GROUP), lambda i: (i, 0, 0)),
            pl.BlockSpec((per, SSM_GROUP, STATE_ROWS), lambda i: (i, 0, 0)),
        ],
        out_specs=pl.BlockSpec((per, CHUNK_LANES, width), lambda i: (i, 0, 0)),
        out_shape=jax.ShapeDtypeStruct((g, CHUNK_LANES, width), BF16),
        scratch_shapes=[
            pltpu.VMEM((per, CHUNK_LANES, CHUNK_LANES), BF16),
            pltpu.VMEM((per, STATE_ROWS, CHUNK_LANES), BF16),
            pltpu.VMEM((per, CHUNK_LANES, STATE_ROWS), BF16),
        ],
        compiler_params=_params(),
        name="s5_mixer",
    )(u_col, u_meta, csc, c_col, b_col, c_row)


FF_CHUNK = 1024
MIX_STEPS = 4


def _mix_ffn_kernel(x_ref, co_ref, y_ref, ut_ref, d_ref, gwt_ref, gb_ref, wo_ref,
                    gf_ref, w1_ref, w2_ref, gl_ref, o_ref, h_ref, z_ref, a_ref):
    n_chunks = x_ref.shape[1]
    n_parts = SUBLANES // MIX_STEPS

    def steps_of(part):
        return range(part * MIX_STEPS, (part + 1) * MIX_STEPS)

    def mix(part):
        def col(ref):
            return jnp.concatenate(
                [ref[:, r * SSM_GROUP:(r + 1) * SSM_GROUP, :].reshape(D_SSM, n_chunks)
                 for r in steps_of(part)], axis=1).astype(F32)

        yt = col(y_ref) + d_ref[...] * col(ut_ref)
        ge = jax.nn.gelu(yt)
        gate = (jnp.dot(gwt_ref[...], ge.astype(BF16), preferred_element_type=F32)
                + gb_ref[...])
        s5 = (ge * jax.nn.sigmoid(gate)).T.astype(BF16)
        co = co_ref[part * MIX_STEPS:(part + 1) * MIX_STEPS].reshape(
            MIX_STEPS * n_chunks, D_CONV)
        x = jnp.concatenate([x_ref[0, :, r, :] for r in steps_of(part)], axis=0)
        h = x + jnp.dot(jnp.concatenate([co, s5], axis=1), wo_ref[...],
                        preferred_element_type=F32)
        h_ref[part] = h
        z_ref[part] = _rms(h, gf_ref[...]).astype(BF16)

    def ffn_up(part, j):
        cols = slice(j * FF_CHUNK, (j + 1) * FF_CHUNK)
        a = jnp.dot(z_ref[part], w1_ref[:, cols], preferred_element_type=F32)
        a_ref[:, cols] = jnp.square(jnp.maximum(a, 0.0)).astype(BF16)

    def ffn_down(part):
        h = h_ref[part] + jnp.dot(a_ref[...], w2_ref[...], preferred_element_type=F32)
        out = _rms(h, gl_ref[...])
        for k, r in enumerate(steps_of(part)):
            o_ref[0, :, r, :] = out[k * n_chunks:(k + 1) * n_chunks, :]

    for part in range(n_parts):
        mix(part)
        for j in range(D_FF // FF_CHUNK):
            ffn_up(part, j)
        ffn_down(part)


def _mix_ffn(x4, conv_out, y_col, u_col, d_col, glu_wt, glu_b_col, w_out, g_ffn, w1, w2,
             g_final):
    bsz, n_chunks = x4.shape[:2]

    def const(shape):
        return pl.BlockSpec(shape, lambda i, b: (0, 0), pipeline_mode=pl.Buffered(1))

    x_blk = pl.BlockSpec((1, n_chunks, SUBLANES, D_MODEL), lambda i, b: (b, 0, i, 0))
    col_blk = pl.BlockSpec((N_GROUPS, SUBLANES * SSM_GROUP, n_chunks),
                           lambda i, b: (0, i, b))
    return pl.pallas_call(
        _mix_ffn_kernel,
        grid=(CHUNK // SUBLANES, bsz),
        in_specs=[
            x_blk,
            pl.BlockSpec((SUBLANES, n_chunks, D_CONV), lambda i, b: (i, b, 0)),
            col_blk,
            col_blk,
            const((D_SSM, 1)),
            const((D_SSM, D_SSM)),
            const((D_SSM, 1)),
            const((D_MODEL, D_MODEL)),
            const((1, D_MODEL)),
            const((D_MODEL, D_FF)),
            const((D_FF, D_MODEL)),
            const((1, D_MODEL)),
        ],
        out_specs=x_blk,
        out_shape=jax.ShapeDtypeStruct(x4.shape, F32),
        scratch_shapes=[
            pltpu.VMEM((SUBLANES // MIX_STEPS, MIX_STEPS * n_chunks, D_MODEL), F32),
            pltpu.VMEM((SUBLANES // MIX_STEPS, MIX_STEPS * n_chunks, D_MODEL), BF16),
            pltpu.VMEM((MIX_STEPS * n_chunks, D_FF), BF16),
        ],
        compiler_params=_params(2),
        name="mix_ffn",
    )(x4, conv_out, y_col, u_col, d_col, glu_wt, glu_b_col, w_out, g_ffn, w1, w2, g_final)


def _s5_operator_inputs(lbr, lbi, zr, zi, b_re, b_im, c_re, c_im):
    g, p, h = N_GROUPS, SSM_STATE, SSM_GROUP

    sc = jnp.transpose(jnp.stack([lbr, lbi, zr, zi], axis=-1), (1, 0, 2, 3))
    csc = jnp.broadcast_to(sc[:, :, None], (g, 2, 2, p, 4)).reshape(g, 4 * p, 4)
    c = jnp.stack([c_re, c_im], axis=0)
    b = jnp.stack([b_re, b_im], axis=0)
    c_col = jnp.transpose(c, (2, 1, 0, 4, 3)).reshape(g, 4 * p, h)
    b_col = jnp.transpose(b, (2, 1, 0, 3, 4)).reshape(g, 4 * p, h)
    c_row = jnp.transpose(c, (2, 3, 1, 0, 4)).reshape(g, h, 4 * p)
    return csc, c_col, b_col, c_row


def kernel(x, meta_tokens, norm_mix_g, w_in, conv_w, conv_b, conv_ln_g, conv_ln_b,
           ssm_lam_re, ssm_lam_im, ssm_log_dt, ssm_b_re, ssm_b_im, ssm_c_re, ssm_c_im,
           ssm_d, ssm_glu_w, ssm_glu_b, w_out, norm_ffn_g, w_ff1, w_ff2, norm_final_g):
    assert w_in.shape[0] == 1, "single-layer block"
    bsz, seq, _ = x.shape
    assert seq % CHUNK == 0 and CHUNK >= N_META
    n_chunks = seq // CHUNK
    g, h = N_GROUPS, SSM_GROUP

    x4 = x.reshape(bsz, n_chunks, CHUNK, D_MODEL)
    g_mix = norm_mix_g[0][None, :]
    w_in_b = w_in[0].astype(BF16)
    u_conv, u_col = _in_proj_step(x4, g_mix, w_in_b)
    u_conv_m, u_ssm_m = _in_proj_meta(meta_tokens, g_mix, w_in_b)

    conv_out = _conv_module(u_conv, u_conv_m, conv_w[0], conv_b[0][None, :],
                            conv_ln_g[0][None, :], conv_ln_b[0][None, :], bsz)

    ldt = jnp.broadcast_to(ssm_log_dt[0][..., None], ssm_lam_re[0].shape)
    flat = lambda a: a.reshape(2 * g, SSM_STATE)
    lbr, lbi, zr, zi = [a.reshape(2, g, SSM_STATE) for a in
                        _zoh(flat(ssm_lam_re[0]), flat(ssm_lam_im[0]), flat(ldt))]
    csc, c_col, b_col, c_row = _s5_operator_inputs(
        lbr, lbi, zr, zi, ssm_b_re[0], ssm_b_im[0], ssm_c_re[0], ssm_c_im[0])

    u_meta = jnp.transpose(u_ssm_m.reshape(N_META, g, h), (1, 0, 2)).reshape(g, 1, N_META * h)
    y_col = _ssm(u_col, u_meta, csc, c_col, b_col, c_row, n_chunks)

    out = _mix_ffn(x4, conv_out, y_col, u_col, ssm_d[0][:, None],
                   ssm_glu_w[0].T.astype(BF16), ssm_glu_b[0][:, None],
                   w_out[0].astype(BF16), norm_ffn_g[0][None, :], w_ff1[0].astype(BF16),
                   w_ff2[0].astype(BF16), norm_final_g[None, :])
    return out.reshape(bsz, seq, D_MODEL)
```

```python
import functools

import jax
import jax.numpy as jnp
from jax import lax
from jax.experimental import pallas as pl
from jax.experimental.pallas import tpu as pltpu

F32 = jnp.float32
BF16 = jnp.bfloat16

D_MODEL = 1024
N_META = 16
D_CONV = 512
D_SSM = 512
CONV_WIDTH = 31
CONV_PAD = CONV_WIDTH // 2
SSM_GROUP = 16
N_GROUPS = D_SSM // SSM_GROUP
SSM_STATE = 64
D_FF = 4096
NORM_EPS = 1e-5
LANES = 128
SUBLANES = 8

CHUNK = 32
CHUNK_LANES = CHUNK * SSM_GROUP
STATE_ROWS = 4 * SSM_STATE
DIR_ROWS = 2 * SSM_STATE
STEPS_PER_VREG = LANES // SSM_GROUP

VMEM_LIMIT_BYTES = 60 * 1024 * 1024


def _rms(x, g):
    return x * lax.rsqrt(jnp.mean(x * x, axis=-1, keepdims=True) + NORM_EPS) * g


def _params(n_axes=1):
    return pltpu.CompilerParams(dimension_semantics=("parallel",) * n_axes,
                                vmem_limit_bytes=VMEM_LIMIT_BYTES)


def _in_proj_step_kernel(x_ref, g_ref, w_ref, uc_ref, ut_ref):
    n_chunks = x_ref.shape[1]
    x = jnp.concatenate([x_ref[0, :, r, :] for r in range(SUBLANES)], axis=0)
    z = _rms(x, g_ref[...]).astype(BF16)
    p = jnp.dot(z, w_ref[...], preferred_element_type=F32)
    uc = p[:, :D_CONV] * jax.nn.sigmoid(p[:, D_CONV:2 * D_CONV])
    uc_ref[...] = uc.reshape(SUBLANES, n_chunks, D_CONV)
    ut = p[:, 2 * D_CONV:].T.astype(BF16)
    for r in range(SUBLANES):
        ut_ref[:, r * SSM_GROUP:(r + 1) * SSM_GROUP, :] = (
            ut[:, r * n_chunks:(r + 1) * n_chunks].reshape(N_GROUPS, SSM_GROUP, n_chunks))


def _in_proj_step(x4, g, w_in):
    bsz, n_chunks = x4.shape[:2]
    rows = bsz * n_chunks
    return pl.pallas_call(
        _in_proj_step_kernel,
        grid=(CHUNK // SUBLANES, bsz),
        in_specs=[
            pl.BlockSpec((1, n_chunks, SUBLANES, D_MODEL), lambda i, b: (b, 0, i, 0)),
            pl.BlockSpec((1, D_MODEL), lambda i, b: (0, 0)),
            pl.BlockSpec((D_MODEL, 2 * D_CONV + D_SSM), lambda i, b: (0, 0)),
        ],
        out_specs=[
            pl.BlockSpec((SUBLANES, n_chunks, D_CONV), lambda i, b: (i, b, 0)),
            pl.BlockSpec((N_GROUPS, SUBLANES * SSM_GROUP, n_chunks), lambda i, b: (0, i, b)),
        ],
        out_shape=[
            jax.ShapeDtypeStruct((CHUNK, rows, D_CONV), F32),
            jax.ShapeDtypeStruct((N_GROUPS, CHUNK_LANES, rows), BF16),
        ],
        compiler_params=_params(2),
        name="in_proj",
    )(x4, g, w_in)


def _in_proj_meta_kernel(x_ref, g_ref, w_ref, uc_ref, us_ref):
    z = _rms(x_ref[...], g_ref[...]).astype(BF16)
    p = jnp.dot(z, w_ref[...], preferred_element_type=F32)
    uc_ref[...] = p[:, :D_CONV] * jax.nn.sigmoid(p[:, D_CONV:2 * D_CONV])
    us_ref[...] = p[:, 2 * D_CONV:]


def _in_proj_meta(meta, g, w_in):
    return pl.pallas_call(
        _in_proj_meta_kernel,
        out_shape=[jax.ShapeDtypeStruct((N_META, D_CONV), F32),
                   jax.ShapeDtypeStruct((N_META, D_SSM), F32)],
        name="in_proj_meta",
    )(meta, g, w_in)


CONV_GROUP = 4
CONV_ROWS = 16


def _conv_kernel(u_ref, um_ref, w_ref, cb_ref, lg_ref, lb_ref, o_ref, cat_ref, acc_ref):
    n_steps, n_chunks, _ = u_ref.shape
    cat_ref[CONV_PAD:CONV_PAD + n_steps] = u_ref[...]
    chunk = lax.broadcasted_iota(jnp.int32, (n_chunks, 1), 0)
    for i in range(CONV_PAD):
        s_prev = n_steps - CONV_PAD + i
        prev = pltpu.roll(u_ref[s_prev], 1, axis=0)
        meta_row = um_ref[N_META - CONV_PAD + i:N_META - CONV_PAD + i + 1, :]
        cat_ref[i] = jnp.where(chunk == 0, meta_row, prev)
        nxt = pltpu.roll(u_ref[i], n_chunks - 1, axis=0)
        cat_ref[n_steps + CONV_PAD + i] = jnp.where(chunk == n_chunks - 1, 0.0, nxt)

    def body(sg, carry):
        s0 = sg * CONV_GROUP
        for c in range(D_CONV // LANES):
            cols = slice(c * LANES, (c + 1) * LANES)
            taps = [jnp.broadcast_to(w_ref[k:k + 1, cols], (SUBLANES, LANES))
                    for k in range(CONV_WIDTH)]

            def rows_body(rt, carry2):
                for sub in range(CONV_ROWS // SUBLANES):
                    r0 = pl.multiple_of(rt * CONV_ROWS + sub * SUBLANES, SUBLANES)
                    accs = [jnp.zeros((SUBLANES, LANES), F32)] * CONV_GROUP
                    for i in range(CONV_WIDTH + CONV_GROUP - 1):
                        d = cat_ref[s0 + i, pl.ds(r0, SUBLANES), cols]
                        for j in range(CONV_GROUP):
                            if 0 <= i - j < CONV_WIDTH:
                                accs[j] = accs[j] + d * taps[i - j]
                    for j in range(CONV_GROUP):
                        acc_ref[j, pl.ds(r0, SUBLANES), cols] = accs[j]
                return carry2

            lax.fori_loop(0, n_chunks // CONV_ROWS, rows_body, 0)
        for j in range(CONV_GROUP):
            y = acc_ref[j] + cb_ref[...]
            yc = y - jnp.mean(y, axis=-1, keepdims=True)
            yn = yc * lax.rsqrt(jnp.mean(yc * yc, axis=-1, keepdims=True) + NORM_EPS)
            yn = yn * lg_ref[...] + lb_ref[...]
            o_ref[s0 + j] = (yn * jax.nn.sigmoid(yn)).astype(BF16)
        return carry

    lax.fori_loop(0, n_steps // CONV_GROUP, body, 0)


def _conv_module(u_conv, u_conv_meta, conv_w, conv_b, ln_g, ln_b, bsz):
    n_steps, rows, _ = u_conv.shape
    n_chunks = rows // bsz
    vec = pl.BlockSpec((1, D_CONV), lambda b: (0, 0))
    blk = pl.BlockSpec((n_steps, n_chunks, D_CONV), lambda b: (0, b, 0))
    return pl.pallas_call(
        _conv_kernel,
        grid=(bsz,),
        in_specs=[
            blk,
            pl.BlockSpec((N_META, D_CONV), lambda b: (0, 0)),
            pl.BlockSpec((CONV_WIDTH, D_CONV), lambda b: (0, 0)),
            vec, vec, vec,
        ],
        out_specs=blk,
        out_shape=jax.ShapeDtypeStruct((n_steps, rows, D_CONV), BF16),
        scratch_shapes=[pltpu.VMEM((n_steps + 2 * CONV_PAD, n_chunks, D_CONV), F32),
                        pltpu.VMEM((CONV_GROUP, n_chunks, D_CONV), F32)],
        compiler_params=_params(),
        name="conv_module",
    )(u_conv, u_conv_meta, conv_w, conv_b, ln_g, ln_b)


def _zoh_kernel(lre_ref, lim_ref, ldt_ref, lbr_ref, lbi_ref, zr_ref, zi_ref):
    lre = lre_ref[...]
    lim = lim_ref[...]
    dt = jnp.exp(ldt_ref[...])
    ea = jnp.exp(lre * dt)
    lbr = ea * jnp.cos(lim * dt)
    lbi = ea * jnp.sin(lim * dt)
    nr = lbr - 1.0
    den = lre * lre + lim * lim
    lbr_ref[...] = lbr
    lbi_ref[...] = lbi
    zr_ref[...] = (nr * lre + lbi * lim) / den
    zi_ref[...] = (lbi * lre - nr * lim) / den


def _zoh(lam_re, lam_im, log_dt):
    shape = jax.ShapeDtypeStruct(lam_re.shape, F32)
    return pl.pallas_call(_zoh_kernel, out_shape=[shape] * 4, name="s5_zoh")(
        lam_re, lam_im, log_dt)


def _lanes(x, n):
    return jnp.concatenate([x] * (n // LANES), axis=1)


def _cmul(ar, ai, br, bi):
    return ar * br - ai * bi, ar * bi + ai * br


def _cpow(br, bi, expo, nbits):
    rr = jnp.ones(expo.shape, F32)
    ri = jnp.zeros(expo.shape, F32)
    for k in range(nbits):
        bit = ((expo >> k) & 1) == 1
        nr, ni = _cmul(rr, ri, br, bi)
        rr = jnp.where(bit, nr, rr)
        ri = jnp.where(bit, ni, ri)
        br, bi = _cmul(br, bi, br, bi)
    return rr, ri


def _build_chunk_ops(gi, csc_ref, cc_ref, bc_ref, cr_ref, um_ref, toep_ref, wend_ref,
                     wout_ref):
    t = CHUNK
    rows = STATE_ROWS
    n_cols = CHUNK_LANES // LANES
    lam = (jnp.broadcast_to(csc_ref[gi, :, 0:1], (rows, LANES)),
           jnp.broadcast_to(csc_ref[gi, :, 1:2], (rows, LANES)))
    zr = jnp.broadcast_to(csc_ref[gi, :, 2:3], (rows, LANES))
    zi = jnp.broadcast_to(csc_ref[gi, :, 3:4], (rows, LANES))
    sel = (lax.broadcasted_iota(jnp.int32, (SSM_GROUP, LANES), 1) % SSM_GROUP
           == lax.broadcasted_iota(jnp.int32, (SSM_GROUP, LANES), 0)).astype(F32)

    def tile(ref):
        return jnp.dot(ref[gi], sel, precision=lax.Precision.HIGHEST,
                       preferred_element_type=F32)

    def swap_re_im(a):
        p = SSM_STATE
        return jnp.concatenate([a[p:2 * p], a[0:p], a[3 * p:], a[2 * p:3 * p]], axis=0)

    c_same, b_same = tile(cc_ref), tile(bc_ref)
    c_swap, b_swap = swap_re_im(c_same), swap_re_im(b_same)
    row = lax.broadcasted_iota(jnp.int32, (rows, LANES), 0)
    is_re = (row // SSM_STATE) % 2 == 0
    sgn = jnp.where(is_re, -1.0, 1.0)
    ca = jnp.where(is_re, c_same, -c_same)
    cb = -c_swap
    bb_same = zr * b_same + sgn * zi * b_swap
    sbb_swap = sgn * (zr * b_swap - sgn * zi * b_same)

    pows = {1: lam}
    k = 1
    while k < t:
        pows[2 * k] = _cmul(*pows[k], *pows[k])
        k *= 2
    step_bits = STEPS_PER_VREG.bit_length() - 1
    fwd = slice(0, DIR_ROWS)
    bwd = slice(DIR_ROWS, rows)
    every = slice(0, rows)

    def rows_of(v, rs):
        return v[0][rs], v[1][rs]

    def column(base, e, rs):
        out = rows_of(base, rs)
        for bit, val in pows.items():
            if e & bit:
                out = _cmul(*out, *rows_of(val, rs))
        return out

    def times_b(p, rs):
        return p[0] * bb_same[rs] + p[1] * sbb_swap[rs]

    def times_c(p, rs):
        return ca[rs] * p[0] + cb[rs] * p[1]

    i8 = lax.broadcasted_iota(jnp.int32, (rows, LANES), 1) // SSM_GROUP
    asc0 = _cpow(*lam, i8, step_bits)
    desc0 = _cpow(*lam, STEPS_PER_VREG - 1 - i8, step_bits)
    asc1_0 = _cmul(*asc0, *lam)
    desc1_0 = _cmul(*desc0, *lam)

    wend_f, wend_b, lag_b, wout_f, wout_b = [], [], [], [], []
    for q in range(n_cols):
        e_asc = STEPS_PER_VREG * q
        e_desc = STEPS_PER_VREG * (n_cols - 1 - q)
        asc1_q = column(asc1_0, e_asc, every)
        wend_f.append(times_b(column(desc0, e_desc, fwd), fwd))
        wend_b.append(times_b(column(asc0, e_asc, bwd), bwd))
        lag_b.append(times_b(rows_of(asc1_q, bwd), bwd))
        wout_f.append(times_c(rows_of(asc1_q, fwd), fwd))
        wout_b.append(times_c(column(desc1_0, e_desc, bwd), bwd))
        cols = slice(q * LANES, (q + 1) * LANES)
        wend_ref[gi, fwd, cols] = wend_f[q].astype(BF16)
        wend_ref[gi, bwd, cols] = wend_b[q].astype(BF16)

    meta_cols = N_META * SSM_GROUP // LANES
    x0 = sum(jnp.sum(wend_f[n_cols - meta_cols + m] * um_ref[gi, :, m * LANES:(m + 1) * LANES],
                     axis=1, keepdims=True) for m in range(meta_cols))

    lane = lax.broadcasted_iota(jnp.int32, (DIR_ROWS, LANES), 1)
    lag0_b = jnp.where(lane >= LANES - SSM_GROUP, bb_same[bwd], 0.0)
    zero = jnp.zeros((DIR_ROWS, LANES), F32)
    bcat = jnp.concatenate(
        [jnp.concatenate(wend_f + [zero] * n_cols, axis=1),
         jnp.concatenate([zero] * (n_cols - 1) + [lag0_b] + lag_b, axis=1)], axis=0)
    lane_r = lax.broadcasted_iota(jnp.int32, (SSM_GROUP, rows), 1)
    c2 = jnp.where((lane_r // SSM_STATE) % 2 == 0, cr_ref[gi], -cr_ref[gi])
    kk = jnp.dot(c2, bcat, precision=lax.Precision.HIGHEST,
                 preferred_element_type=F32)
    for tt in range(t):
        off = (t - 1 - tt) * SSM_GROUP
        toep_ref[gi, tt * SSM_GROUP:(tt + 1) * SSM_GROUP, :] = (
            kk[:, off:off + CHUNK_LANES].astype(BF16))

    wout = jnp.concatenate([jnp.concatenate(wout_f, axis=1),
                            jnp.concatenate(wout_b, axis=1)], axis=0)
    wout_ref[gi] = wout.T.astype(BF16)
    return pows[t], x0


SSM_GROUPS_PER_STEP = 2


def _chunk_scan(er, ei, ar, ai, x0, chunk, n_chunks, forward):
    width = er.shape[1]
    xr, xi = er, ei
    if x0 is not None:
        fr, fi = _cmul(ar, ai, x0[0], x0[1])
        xr = xr + jnp.where(chunk == 0, _lanes(fr, width), 0.0)
        xi = xi + jnp.where(chunk == 0, _lanes(fi, width), 0.0)

    def shifted(v, sh):
        if forward:
            return jnp.where(chunk >= sh, pltpu.roll(v, sh, axis=1), 0.0)
        return jnp.where(chunk < n_chunks - sh, pltpu.roll(v, width - sh, axis=1), 0.0)

    sh = 1
    while sh < n_chunks:
        sr, si = shifted(xr, sh), shifted(xi, sh)
        wr, wi = _lanes(ar, width), _lanes(ai, width)
        xr, xi = xr + wr * sr - wi * si, xi + wr * si + wi * sr
        ar, ai = _cmul(ar, ai, ar, ai)
        sh *= 2
    inr, ini = shifted(xr, 1), shifted(xi, 1)
    if x0 is not None:
        inr = jnp.where(chunk == 0, x0[0], inr)
        ini = jnp.where(chunk == 0, x0[1], ini)
    return inr, ini


def _ssm_kernel(u_ref, um_ref, csc_ref, cc_ref, bc_ref, cr_ref, y_ref, toep_ref, wend_ref,
                wout_ref, *, n_chunks):
    for gi in range(SSM_GROUPS_PER_STEP):
        (lam_r, lam_i), x0 = _build_chunk_ops(gi, csc_ref, cc_ref, bc_ref, cr_ref, um_ref,
                                              toep_ref, wend_ref, wout_ref)
        u = u_ref[gi]
        width = u.shape[1]
        e = jnp.dot(wend_ref[gi], u, preferred_element_type=F32)
        chunk = lax.broadcasted_iota(jnp.int32, (1, width), 1) % n_chunks
        p = SSM_STATE
        ar, ai = lam_r, lam_i
        f_in = _chunk_scan(e[0:p], e[p:2 * p], ar[0:p], ai[0:p], (x0[0:p], x0[p:2 * p]),
                           chunk, n_chunks, True)
        b_in = _chunk_scan(e[2 * p:3 * p], e[3 * p:], ar[2 * p:3 * p], ai[2 * p:3 * p],
                           None, chunk, n_chunks, False)
        xin = jnp.concatenate([f_in[0], f_in[1], b_in[0], b_in[1]], axis=0).astype(BF16)
        y = jnp.dot(toep_ref[gi], u, preferred_element_type=F32)
        y = y + jnp.dot(wout_ref[gi], xin, preferred_element_type=F32)
        y_ref[gi] = y.astype(BF16)


def _ssm(u_col, u_meta, csc, c_col, b_col, c_row, n_chunks):
    g, _, width = u_col.shape
    per = SSM_GROUPS_PER_STEP
    return pl.pallas_call(
        functools.partial(_ssm_kernel, n_chunks=n_chunks),
        grid=(g // per,),
        in_specs=[
            pl.BlockSpec((per, CHUNK_LANES, width), lambda i: (i, 0, 0)),
            pl.BlockSpec((per, 1, N_META * SSM_GROUP), lambda i: (i, 0, 0)),
            pl.BlockSpec((per, STATE_ROWS, 4), lambda i: (i, 0, 0)),
            pl.BlockSpec((per, STATE_ROWS, SSM_GROUP), lambda i: (i, 0, 0)),
            pl.BlockSpec((per, STATE_ROWS, SSM_GROUP), lambda i: (i, 0, 0)),
            pl.BlockSpec((per, SSM_GROUP, STATE_ROWS), lambda i: (i, 0, 0)),
        ],
        out_specs=pl.BlockSpec((per, CHUNK_LANES, width), lambda i: (i, 0, 0)),
        out_shape=jax.ShapeDtypeStruct((g, CHUNK_LANES, width), BF16),
        scratch_shapes=[
            pltpu.VMEM((per, CHUNK_LANES, CHUNK_LANES), BF16),
            pltpu.VMEM((per, STATE_ROWS, CHUNK_LANES), BF16),
            pltpu.VMEM((per, CHUNK_LANES, STATE_ROWS), BF16),
        ],
        compiler_params=_params(),
        name="s5_mixer",
    )(u_col, u_meta, csc, c_col, b_col, c_row)


FF_CHUNK = 1024
MIX_STEPS = 4


def _mix_ffn_kernel(x_ref, co_ref, y_ref, ut_ref, d_ref, gwt_ref, gb_ref, wo_ref,
                    gf_ref, w1_ref, w2_ref, gl_ref, o_ref, h_ref, z_ref, a_ref):
    n_chunks = x_ref.shape[1]
    n_parts = SUBLANES // MIX_STEPS

    def steps_of(part):
        return range(part * MIX_STEPS, (part + 1) * MIX_STEPS)

    def mix(part):
        def col(ref):
            return jnp.concatenate(
                [ref[:, r * SSM_GROUP:(r + 1) * SSM_GROUP, :].reshape(D_SSM, n_chunks)
                 for r in steps_of(part)], axis=1).astype(F32)

        yt = col(y_ref) + d_ref[...] * col(ut_ref)
        ge = jax.nn.gelu(yt)
        gate = (jnp.dot(gwt_ref[...], ge.astype(BF16), preferred_element_type=F32)
                + gb_ref[...])
        s5 = (ge * jax.nn.sigmoid(gate)).T.astype(BF16)
        co = co_ref[part * MIX_STEPS:(part + 1) * MIX_STEPS].reshape(
            MIX_STEPS * n_chunks, D_CONV)
        x = jnp.concatenate([x_ref[0, :, r, :] for r in steps_of(part)], axis=0)
        h = x + jnp.dot(jnp.concatenate([co, s5], axis=1), wo_ref[...],
                        preferred_element_type=F32)
        h_ref[part] = h
        z_ref[part] = _rms(h, gf_ref[...]).astype(BF16)

    def ffn_up(part, j):
        cols = slice(j * FF_CHUNK, (j + 1) * FF_CHUNK)
        a = jnp.dot(z_ref[part], w1_ref[:, cols], preferred_element_type=F32)
        a_ref[:, cols] = jnp.square(jnp.maximum(a, 0.0)).astype(BF16)

    def ffn_down(part):
        h = h_ref[part] + jnp.dot(a_ref[...], w2_ref[...], preferred_element_type=F32)
        out = _rms(h, gl_ref[...])
        for k, r in enumerate(steps_of(part)):
            o_ref[0, :, r, :] = out[k * n_chunks:(k + 1) * n_chunks, :]

    for part in range(n_parts):
        mix(part)
        for j in range(D_FF // FF_CHUNK):
            ffn_up(part, j)
        ffn_down(part)


def _mix_ffn(x4, conv_out, y_col, u_col, d_col, glu_wt, glu_b_col, w_out, g_ffn, w1, w2,
             g_final):
    bsz, n_chunks = x4.shape[:2]

    def const(shape):
        return pl.BlockSpec(shape, lambda i, b: (0, 0), pipeline_mode=pl.Buffered(1))

    x_blk = pl.BlockSpec((1, n_chunks, SUBLANES, D_MODEL), lambda i, b: (b, 0, i, 0))
    col_blk = pl.BlockSpec((N_GROUPS, SUBLANES * SSM_GROUP, n_chunks),
                           lambda i, b: (0, i, b))
    return pl.pallas_call(
        _mix_ffn_kernel,
        grid=(CHUNK // SUBLANES, bsz),
        in_specs=[
            x_blk,
            pl.BlockSpec((SUBLANES, n_chunks, D_CONV), lambda i, b: (i, b, 0)),
            col_blk,
            col_blk,
            const((D_SSM, 1)),
            const((D_SSM, D_SSM)),
            const((D_SSM, 1)),
            const((D_MODEL, D_MODEL)),
            const((1, D_MODEL)),
            const((D_MODEL, D_FF)),
            const((D_FF, D_MODEL)),
            const((1, D_MODEL)),
        ],
        out_specs=x_blk,
        out_shape=jax.ShapeDtypeStruct(x4.shape, F32),
        scratch_shapes=[
            pltpu.VMEM((SUBLANES // MIX_STEPS, MIX_STEPS * n_chunks, D_MODEL), F32),
            pltpu.VMEM((SUBLANES // MIX_STEPS, MIX_STEPS * n_chunks, D_MODEL), BF16),
            pltpu.VMEM((MIX_STEPS * n_chunks, D_FF), BF16),
        ],
        compiler_params=_params(2),
        name="mix_ffn",
    )(x4, conv_out, y_col, u_col, d_col, glu_wt, glu_b_col, w_out, g_ffn, w1, w2, g_final)


def _s5_operator_inputs(lbr, lbi, zr, zi, b_re, b_im, c_re, c_im):
    g, p, h = N_GROUPS, SSM_STATE, SSM_GROUP

    sc = jnp.transpose(jnp.stack([lbr, lbi, zr, zi], axis=-1), (1, 0, 2, 3))
    csc = jnp.broadcast_to(sc[:, :, None], (g, 2, 2, p, 4)).reshape(g, 4 * p, 4)
    c = jnp.stack([c_re, c_im], axis=0)
    b = jnp.stack([b_re, b_im], axis=0)
    c_col = jnp.transpose(c, (2, 1, 0, 4, 3)).reshape(g, 4 * p, h)
    b_col = jnp.transpose(b, (2, 1, 0, 3, 4)).reshape(g, 4 * p, h)
    c_row = jnp.transpose(c, (2, 3, 1, 0, 4)).reshape(g, h, 4 * p)
    return csc, c_col, b_col, c_row


def kernel(x, meta_tokens, norm_mix_g, w_in, conv_w, conv_b, conv_ln_g, conv_ln_b,
           ssm_lam_re, ssm_lam_im, ssm_log_dt, ssm_b_re, ssm_b_im, ssm_c_re, ssm_c_im,
           ssm_d, ssm_glu_w, ssm_glu_b, w_out, norm_ffn_g, w_ff1, w_ff2, norm_final_g):
    assert w_in.shape[0] == 1, "single-layer block"
    bsz, seq, _ = x.shape
    assert seq % CHUNK == 0 and CHUNK >= N_META
    n_chunks = seq // CHUNK
    g, h = N_GROUPS, SSM_GROUP

    x4 = x.reshape(bsz, n_chunks, CHUNK, D_MODEL)
    g_mix = norm_mix_g[0][None, :]
    w_in_b = w_in[0].astype(BF16)
    u_conv, u_col = _in_proj_step(x4, g_mix, w_in_b)
    u_conv_m, u_ssm_m = _in_proj_meta(meta_tokens, g_mix, w_in_b)

    conv_out = _conv_module(u_conv, u_conv_m, conv_w[0], conv_b[0][None, :],
                            conv_ln_g[0][None, :], conv_ln_b[0][None, :], bsz)

    ldt = jnp.broadcast_to(ssm_log_dt[0][..., None], ssm_lam_re[0].shape)
    flat = lambda a: a.reshape(2 * g, SSM_STATE)
    lbr, lbi, zr, zi = [a.reshape(2, g, SSM_STATE) for a in
                        _zoh(flat(ssm_lam_re[0]), flat(ssm_lam_im[0]), flat(ldt))]
    csc, c_col, b_col, c_row = _s5_operator_inputs(
        lbr, lbi, zr, zi, ssm_b_re[0], ssm_b_im[0], ssm_c_re[0], ssm_c_im[0])

    u_meta = jnp.transpose(u_ssm_m.reshape(N_META, g, h), (1, 0, 2)).reshape(g, 1, N_META * h)
    y_col = _ssm(u_col, u_meta, csc, c_col, b_col, c_row, n_chunks)

    out = _mix_ffn(x4, conv_out, y_col, u_col, ssm_d[0][:, None],
                   ssm_glu_w[0].T.astype(BF16), ssm_glu_b[0][:, None],
                   w_out[0].astype(BF16), norm_ffn_g[0][None, :], w_ff1[0].astype(BF16),
                   w_ff2[0].astype(BF16), norm_final_g[None, :])
    return out.reshape(bsz, seq, D_MODEL)
```

```python
import functools

import jax
import jax.numpy as jnp
from jax import lax
from jax.experimental import pallas as pl
from jax.experimental.pallas import tpu as pltpu

F32 = jnp.float32
BF16 = jnp.bfloat16

D_MODEL = 1024
N_META = 16
D_CONV = 512
D_SSM = 512
CONV_WIDTH = 31
CONV_PAD = CONV_WIDTH // 2
SSM_GROUP = 16
N_GROUPS = D_SSM // SSM_GROUP
SSM_STATE = 64
D_FF = 4096
NORM_EPS = 1e-5
LANES = 128
SUBLANES = 8

CHUNK = 32
CHUNK_LANES = CHUNK * SSM_GROUP
STATE_ROWS = 4 * SSM_STATE
DIR_ROWS = 2 * SSM_STATE
STEPS_PER_VREG = LANES // SSM_GROUP

VMEM_LIMIT_BYTES = 60 * 1024 * 1024


def _rms(x, g):
    return x * lax.rsqrt(jnp.mean(x * x, axis=-1, keepdims=True) + NORM_EPS) * g


def _params(n_axes=1):
    return pltpu.CompilerParams(dimension_semantics=("parallel",) * n_axes,
                                vmem_limit_bytes=VMEM_LIMIT_BYTES)


IN_STEPS = 4


def _in_proj_step_kernel(x_ref, g_ref, w_ref, uc_ref, ut_ref):
    n_chunks = x_ref.shape[1]
    for part in range(SUBLANES // IN_STEPS):
        steps = range(part * IN_STEPS, (part + 1) * IN_STEPS)
        x = jnp.concatenate([x_ref[0, :, r, :] for r in steps], axis=0)
        z = _rms(x, g_ref[...]).astype(BF16)
        p = jnp.dot(z, w_ref[...], preferred_element_type=F32)
        uc = p[:, :D_CONV] * jax.nn.sigmoid(p[:, D_CONV:2 * D_CONV])
        uc_ref[part * IN_STEPS:(part + 1) * IN_STEPS] = uc.reshape(IN_STEPS, n_chunks, D_CONV)
        ut = p[:, 2 * D_CONV:].T.astype(BF16)
        for k, r in enumerate(steps):
            ut_ref[:, r * SSM_GROUP:(r + 1) * SSM_GROUP, :] = (
                ut[:, k * n_chunks:(k + 1) * n_chunks].reshape(N_GROUPS, SSM_GROUP, n_chunks))


def _in_proj_step(x4, g, w_in):
    bsz, n_chunks = x4.shape[:2]
    rows = bsz * n_chunks
    return pl.pallas_call(
        _in_proj_step_kernel,
        grid=(CHUNK // SUBLANES, bsz),
        in_specs=[
            pl.BlockSpec((1, n_chunks, SUBLANES, D_MODEL), lambda i, b: (b, 0, i, 0)),
            pl.BlockSpec((1, D_MODEL), lambda i, b: (0, 0)),
            pl.BlockSpec((D_MODEL, 2 * D_CONV + D_SSM), lambda i, b: (0, 0)),
        ],
        out_specs=[
            pl.BlockSpec((SUBLANES, n_chunks, D_CONV), lambda i, b: (i, b, 0)),
            pl.BlockSpec((N_GROUPS, SUBLANES * SSM_GROUP, n_chunks), lambda i, b: (0, i, b)),
        ],
        out_shape=[
            jax.ShapeDtypeStruct((CHUNK, rows, D_CONV), F32),
            jax.ShapeDtypeStruct((N_GROUPS, CHUNK_LANES, rows), BF16),
        ],
        compiler_params=_params(2),
        name="in_proj",
    )(x4, g, w_in)


def _in_proj_meta_kernel(x_ref, g_ref, w_ref, uc_ref, us_ref):
    z = _rms(x_ref[...], g_ref[...]).astype(BF16)
    p = jnp.dot(z, w_ref[...], preferred_element_type=F32)
    uc_ref[...] = p[:, :D_CONV] * jax.nn.sigmoid(p[:, D_CONV:2 * D_CONV])
    us_ref[...] = p[:, 2 * D_CONV:]


def _in_proj_meta(meta, g, w_in):
    return pl.pallas_call(
        _in_proj_meta_kernel,
        out_shape=[jax.ShapeDtypeStruct((N_META, D_CONV), F32),
                   jax.ShapeDtypeStruct((N_META, D_SSM), F32)],
        name="in_proj_meta",
    )(meta, g, w_in)


CONV_GROUP = 4
CONV_ROWS = 16


def _conv_kernel(u_ref, um_ref, w_ref, cb_ref, lg_ref, lb_ref, o_ref, cat_ref, acc_ref):
    n_steps, n_chunks, _ = u_ref.shape
    cat_ref[CONV_PAD:CONV_PAD + n_steps] = u_ref[...]
    chunk = lax.broadcasted_iota(jnp.int32, (n_chunks, 1), 0)
    for i in range(CONV_PAD):
        s_prev = n_steps - CONV_PAD + i
        prev = pltpu.roll(u_ref[s_prev], 1, axis=0)
        meta_row = um_ref[N_META - CONV_PAD + i:N_META - CONV_PAD + i + 1, :]
        cat_ref[i] = jnp.where(chunk == 0, meta_row, prev)
        nxt = pltpu.roll(u_ref[i], n_chunks - 1, axis=0)
        cat_ref[n_steps + CONV_PAD + i] = jnp.where(chunk == n_chunks - 1, 0.0, nxt)

    def body(sg, carry):
        s0 = sg * CONV_GROUP
        for c in range(D_CONV // LANES):
            cols = slice(c * LANES, (c + 1) * LANES)
            taps = [jnp.broadcast_to(w_ref[k:k + 1, cols], (SUBLANES, LANES))
                    for k in range(CONV_WIDTH)]

            def rows_body(rt, carry2):
                for sub in range(CONV_ROWS // SUBLANES):
                    r0 = pl.multiple_of(rt * CONV_ROWS + sub * SUBLANES, SUBLANES)
                    accs = [[jnp.zeros((SUBLANES, LANES), F32)] * 2 for _ in range(CONV_GROUP)]
                    for i in range(CONV_WIDTH + CONV_GROUP - 1):
                        d = cat_ref[s0 + i, pl.ds(r0, SUBLANES), cols]
                        for j in range(CONV_GROUP):
                            k = i - j
                            if 0 <= k < CONV_WIDTH:
                                accs[j][k % 2] = accs[j][k % 2] + d * taps[k]
                    for j in range(CONV_GROUP):
                        acc_ref[j, pl.ds(r0, SUBLANES), cols] = accs[j][0] + accs[j][1]
                return carry2

            lax.fori_loop(0, n_chunks // CONV_ROWS, rows_body, 0)
        for j in range(CONV_GROUP):
            y = acc_ref[j] + cb_ref[...]
            yc = y - jnp.mean(y, axis=-1, keepdims=True)
            yn = yc * lax.rsqrt(jnp.mean(yc * yc, axis=-1, keepdims=True) + NORM_EPS)
            yn = yn * lg_ref[...] + lb_ref[...]
            o_ref[s0 + j] = (yn * jax.nn.sigmoid(yn)).astype(BF16)
        return carry

    lax.fori_loop(0, n_steps // CONV_GROUP, body, 0)


def _conv_module(u_conv, u_conv_meta, conv_w, conv_b, ln_g, ln_b, bsz):
    n_steps, rows, _ = u_conv.shape
    n_chunks = rows // bsz
    vec = pl.BlockSpec((1, D_CONV), lambda b: (0, 0))
    blk = pl.BlockSpec((n_steps, n_chunks, D_CONV), lambda b: (0, b, 0))
    return pl.pallas_call(
        _conv_kernel,
        grid=(bsz,),
        in_specs=[
            blk,
            pl.BlockSpec((N_META, D_CONV), lambda b: (0, 0)),
            pl.BlockSpec((CONV_WIDTH, D_CONV), lambda b: (0, 0)),
            vec, vec, vec,
        ],
        out_specs=blk,
        out_shape=jax.ShapeDtypeStruct((n_steps, rows, D_CONV), BF16),
        scratch_shapes=[pltpu.VMEM((n_steps + 2 * CONV_PAD, n_chunks, D_CONV), F32),
                        pltpu.VMEM((CONV_GROUP, n_chunks, D_CONV), F32)],
        compiler_params=_params(),
        name="conv_module",
    )(u_conv, u_conv_meta, conv_w, conv_b, ln_g, ln_b)


def _zoh_kernel(lre_ref, lim_ref, ldt_ref, lbr_ref, lbi_ref, zr_ref, zi_ref):
    lre = lre_ref[...]
    lim = lim_ref[...]
    dt = jnp.exp(ldt_ref[...])
    ea = jnp.exp(lre * dt)
    lbr = ea * jnp.cos(lim * dt)
    lbi = ea * jnp.sin(lim * dt)
    nr = lbr - 1.0
    den = lre * lre + lim * lim
    lbr_ref[...] = lbr
    lbi_ref[...] = lbi
    zr_ref[...] = (nr * lre + lbi * lim) / den
    zi_ref[...] = (lbi * lre - nr * lim) / den


def _zoh(lam_re, lam_im, log_dt):
    shape = jax.ShapeDtypeStruct(lam_re.shape, F32)
    return pl.pallas_call(_zoh_kernel, out_shape=[shape] * 4, name="s5_zoh")(
        lam_re, lam_im, log_dt)


def _lanes(x, n):
    return jnp.concatenate([x] * (n // LANES), axis=1)


def _cmul(ar, ai, br, bi):
    return ar * br - ai * bi, ar * bi + ai * br


def _cpow(br, bi, expo, nbits):
    rr = jnp.ones(expo.shape, F32)
    ri = jnp.zeros(expo.shape, F32)
    for k in range(nbits):
        bit = ((expo >> k) & 1) == 1
        nr, ni = _cmul(rr, ri, br, bi)
        rr = jnp.where(bit, nr, rr)
        ri = jnp.where(bit, ni, ri)
        br, bi = _cmul(br, bi, br, bi)
    return rr, ri


def _build_chunk_ops(gi, csc_ref, cc_ref, bc_ref, cr_ref, um_ref, toep_ref, wend_ref,
                     wout_ref):
    t = CHUNK
    rows = STATE_ROWS
    n_cols = CHUNK_LANES // LANES
    lam = (jnp.broadcast_to(csc_ref[gi, :, 0:1], (rows, LANES)),
           jnp.broadcast_to(csc_ref[gi, :, 1:2], (rows, LANES)))
    zr = jnp.broadcast_to(csc_ref[gi, :, 2:3], (rows, LANES))
    zi = jnp.broadcast_to(csc_ref[gi, :, 3:4], (rows, LANES))
    sel = (lax.broadcasted_iota(jnp.int32, (SSM_GROUP, LANES), 1) % SSM_GROUP
           == lax.broadcasted_iota(jnp.int32, (SSM_GROUP, LANES), 0)).astype(F32)

    def tile(ref):
        return jnp.dot(ref[gi], sel, precision=lax.Precision.HIGHEST,
                       preferred_element_type=F32)

    def swap_re_im(a):
        p = SSM_STATE
        return jnp.concatenate([a[p:2 * p], a[0:p], a[3 * p:], a[2 * p:3 * p]], axis=0)

    c_same, b_same = tile(cc_ref), tile(bc_ref)
    c_swap, b_swap = swap_re_im(c_same), swap_re_im(b_same)
    row = lax.broadcasted_iota(jnp.int32, (rows, LANES), 0)
    is_re = (row // SSM_STATE) % 2 == 0
    sgn = jnp.where(is_re, -1.0, 1.0)
    ca = jnp.where(is_re, c_same, -c_same)
    cb = -c_swap
    bb_same = zr * b_same + sgn * zi * b_swap
    sbb_swap = sgn * (zr * b_swap - sgn * zi * b_same)

    pows = {1: lam}
    k = 1
    while k < t:
        pows[2 * k] = _cmul(*pows[k], *pows[k])
        k *= 2
    step_bits = STEPS_PER_VREG.bit_length() - 1
    fwd = slice(0, DIR_ROWS)
    bwd = slice(DIR_ROWS, rows)
    every = slice(0, rows)

    def rows_of(v, rs):
        return v[0][rs], v[1][rs]

    def column(base, e, rs):
        out = rows_of(base, rs)
        for bit, val in pows.items():
            if e & bit:
                out = _cmul(*out, *rows_of(val, rs))
        return out

    def times_b(p, rs):
        return p[0] * bb_same[rs] + p[1] * sbb_swap[rs]

    def times_c(p, rs):
        return ca[rs] * p[0] + cb[rs] * p[1]

    i8 = lax.broadcasted_iota(jnp.int32, (rows, LANES), 1) // SSM_GROUP
    asc0 = _cpow(*lam, i8, step_bits)
    desc0 = _cpow(*lam, STEPS_PER_VREG - 1 - i8, step_bits)
    asc1_0 = _cmul(*asc0, *lam)
    desc1_0 = _cmul(*desc0, *lam)

    wend_f, wend_b, lag_b, wout_f, wout_b = [], [], [], [], []
    for q in range(n_cols):
        e_asc = STEPS_PER_VREG * q
        e_desc = STEPS_PER_VREG * (n_cols - 1 - q)
        asc1_q = column(asc1_0, e_asc, every)
        wend_f.append(times_b(column(desc0, e_desc, fwd), fwd))
        wend_b.append(times_b(column(asc0, e_asc, bwd), bwd))
        lag_b.append(times_b(rows_of(asc1_q, bwd), bwd))
        wout_f.append(times_c(rows_of(asc1_q, fwd), fwd))
        wout_b.append(times_c(column(desc1_0, e_desc, bwd), bwd))
        cols = slice(q * LANES, (q + 1) * LANES)
        wend_ref[gi, fwd, cols] = wend_f[q].astype(BF16)
        wend_ref[gi, bwd, cols] = wend_b[q].astype(BF16)

    meta_cols = N_META * SSM_GROUP // LANES
    x0 = sum(jnp.sum(wend_f[n_cols - meta_cols + m] * um_ref[gi, :, m * LANES:(m + 1) * LANES],
                     axis=1, keepdims=True) for m in range(meta_cols))

    lane = lax.broadcasted_iota(jnp.int32, (DIR_ROWS, LANES), 1)
    lag0_b = jnp.where(lane >= LANES - SSM_GROUP, bb_same[bwd], 0.0)
    zero = jnp.zeros((DIR_ROWS, LANES), F32)
    bcat = jnp.concatenate(
        [jnp.concatenate(wend_f + [zero] * n_cols, axis=1),
         jnp.concatenate([zero] * (n_cols - 1) + [lag0_b] + lag_b, axis=1)], axis=0)
    lane_r = lax.broadcasted_iota(jnp.int32, (SSM_GROUP, rows), 1)
    c2 = jnp.where((lane_r // SSM_STATE) % 2 == 0, cr_ref[gi], -cr_ref[gi])
    kk = jnp.dot(c2, bcat, precision=lax.Precision.HIGHEST,
                 preferred_element_type=F32)
    for tt in range(t):
        off = (t - 1 - tt) * SSM_GROUP
        toep_ref[gi, tt * SSM_GROUP:(tt + 1) * SSM_GROUP, :] = (
            kk[:, off:off + CHUNK_LANES].astype(BF16))

    wout = jnp.concatenate([jnp.concatenate(wout_f, axis=1),
                            jnp.concatenate(wout_b, axis=1)], axis=0)
    wout_ref[gi] = wout.T.astype(BF16)
    return pows[t], x0


SSM_GROUPS_PER_STEP = 2


def _chunk_scan(er, ei, ar, ai, x0, chunk, n_chunks, forward):
    width = er.shape[1]
    xr, xi = er, ei
    if x0 is not None:
        fr, fi = _cmul(ar, ai, x0[0], x0[1])
        xr = xr + jnp.where(chunk == 0, _lanes(fr, width), 0.0)
        xi = xi + jnp.where(chunk == 0, _lanes(fi, width), 0.0)

    def shifted(v, sh):
        if forward:
            return jnp.where(chunk >= sh, pltpu.roll(v, sh, axis=1), 0.0)
        return jnp.where(chunk < n_chunks - sh, pltpu.roll(v, width - sh, axis=1), 0.0)

    sh = 1
    while sh < n_chunks:
        sr, si = shifted(xr, sh), shifted(xi, sh)
        wr, wi = _lanes(ar, width), _lanes(ai, width)
        xr, xi = xr + wr * sr - wi * si, xi + wr * si + wi * sr
        ar, ai = _cmul(ar, ai, ar, ai)
        sh *= 2
    inr, ini = shifted(xr, 1), shifted(xi, 1)
    if x0 is not None:
        inr = jnp.where(chunk == 0, x0[0], inr)
        ini = jnp.where(chunk == 0, x0[1], ini)
    return inr, ini


def _ssm_kernel(u_ref, um_ref, csc_ref, cc_ref, bc_ref, cr_ref, y_ref, toep_ref, wend_ref,
                wout_ref, *, n_chunks):
    for gi in range(SSM_GROUPS_PER_STEP):
        (lam_r, lam_i), x0 = _build_chunk_ops(gi, csc_ref, cc_ref, bc_ref, cr_ref, um_ref,
                                              toep_ref, wend_ref, wout_ref)
        u = u_ref[gi]
        width = u.shape[1]
        e = jnp.dot(wend_ref[gi], u, preferred_element_type=F32)
        chunk = lax.broadcasted_iota(jnp.int32, (1, width), 1) % n_chunks
        p = SSM_STATE
        ar, ai = lam_r, lam_i
        f_in = _chunk_scan(e[0:p], e[p:2 * p], ar[0:p], ai[0:p], (x0[0:p], x0[p:2 * p]),
                           chunk, n_chunks, True)
        b_in = _chunk_scan(e[2 * p:3 * p], e[3 * p:], ar[2 * p:3 * p], ai[2 * p:3 * p],
                           None, chunk, n_chunks, False)
        xin = jnp.concatenate([f_in[0], f_in[1], b_in[0], b_in[1]], axis=0).astype(BF16)
        y = jnp.dot(toep_ref[gi], u, preferred_element_type=F32)
        y = y + jnp.dot(wout_ref[gi], xin, preferred_element_type=F32)
        y_ref[gi] = y.astype(BF16)


def _ssm(u_col, u_meta, csc, c_col, b_col, c_row, n_chunks):
    g, _, width = u_col.shape
    per = SSM_GROUPS_PER_STEP
    return pl.pallas_call(
        functools.partial(_ssm_kernel, n_chunks=n_chunks),
        grid=(g // per,),
        in_specs=[
            pl.BlockSpec((per, CHUNK_LANES, width), lambda i: (i, 0, 0)),
            pl.BlockSpec((per, 1, N_META * SSM_GROUP), lambda i: (i, 0, 0)),
            pl.BlockSpec((per, STATE_ROWS, 4), lambda i: (i, 0, 0)),
            pl.BlockSpec((per, STATE_ROWS, SSM_GROUP), lambda i: (i, 0, 0)),
            pl.BlockSpec((per, STATE_ROWS, SSM_GROUP), lambda i: (i, 0, 0)),
            pl.BlockSpec((per, SSM_GROUP, STATE_ROWS), lambda i: (i, 0, 0)),
        ],
        out_specs=pl.BlockSpec((per, CHUNK_LANES, width), lambda i: (i, 0, 0)),
        out_shape=jax.ShapeDtypeStruct((g, CHUNK_LANES, width), BF16),
        scratch_shapes=[
            pltpu.VMEM((per, CHUNK_LANES, CHUNK_LANES), BF16),
            pltpu.VMEM((per, STATE_ROWS, CHUNK_LANES), BF16),
            pltpu.VMEM((per, CHUNK_LANES, STATE_ROWS), BF16),
        ],
        compiler_params=_params(),
        name="s5_mixer",
    )(u_col, u_meta, csc, c_col, b_col, c_row)


FF_CHUNK = 1024
MIX_STEPS = 4


def _mix_ffn_kernel(x_ref, co_ref, y_ref, ut_ref, d_ref, gwt_ref, gb_ref, wo_ref,
                    gf_ref, w1_ref, w2_ref, gl_ref, o_ref, h_ref, z_ref, a_ref):
    n_chunks = x_ref.shape[1]
    n_parts = SUBLANES // MIX_STEPS

    def steps_of(part):
        return range(part * MIX_STEPS, (part + 1) * MIX_STEPS)

    def mix(part):
        def col(ref):
            return jnp.concatenate(
                [ref[:, r * SSM_GROUP:(r + 1) * SSM_GROUP, :].reshape(D_SSM, n_chunks)
                 for r in steps_of(part)], axis=1).astype(F32)

        yt = col(y_ref) + d_ref[...] * col(ut_ref)
        ge = jax.nn.gelu(yt)
        gate = (jnp.dot(gwt_ref[...], ge.astype(BF16), preferred_element_type=F32)
                + gb_ref[...])
        s5 = (ge * jax.nn.sigmoid(gate)).T.astype(BF16)
        co = co_ref[part * MIX_STEPS:(part + 1) * MIX_STEPS].reshape(
            MIX_STEPS * n_chunks, D_CONV)
        x = jnp.concatenate([x_ref[0, :, r, :] for r in steps_of(part)], axis=0)
        h = x + jnp.dot(jnp.concatenate([co, s5], axis=1), wo_ref[...],
                        preferred_element_type=F32)
        h_ref[part] = h
        z_ref[part] = _rms(h, gf_ref[...]).astype(BF16)

    def ffn_up(part, j):
        cols = slice(j * FF_CHUNK, (j + 1) * FF_CHUNK)
        a = jnp.dot(z_ref[part], w1_ref[:, cols], preferred_element_type=F32)
        a_ref[:, cols] = jnp.square(jnp.maximum(a, 0.0)).astype(BF16)

    def ffn_down(part):
        h = h_ref[part] + jnp.dot(a_ref[...], w2_ref[...], preferred_element_type=F32)
        out = _rms(h, gl_ref[...])
        for k, r in enumerate(steps_of(part)):
            o_ref[0, :, r, :] = out[k * n_chunks:(k + 1) * n_chunks, :]

    for part in range(n_parts):
        mix(part)
        for j in range(D_FF // FF_CHUNK):
            ffn_up(part, j)
        ffn_down(part)


def _mix_ffn(x4, conv_out, y_col, u_col, d_col, glu_wt, glu_b_col, w_out, g_ffn, w1, w2,
             g_final):
    bsz, n_chunks = x4.shape[:2]

    def const(shape):
        return pl.BlockSpec(shape, lambda i, b: (0, 0), pipeline_mode=pl.Buffered(1))

    x_blk = pl.BlockSpec((1, n_chunks, SUBLANES, D_MODEL), lambda i, b: (b, 0, i, 0))
    col_blk = pl.BlockSpec((N_GROUPS, SUBLANES * SSM_GROUP, n_chunks),
                           lambda i, b: (0, i, b))
    return pl.pallas_call(
        _mix_ffn_kernel,
        grid=(CHUNK // SUBLANES, bsz),
        in_specs=[
            x_blk,
            pl.BlockSpec((SUBLANES, n_chunks, D_CONV), lambda i, b: (i, b, 0)),
            col_blk,
            col_blk,
            const((D_SSM, 1)),
            const((D_SSM, D_SSM)),
            const((D_SSM, 1)),
            const((D_MODEL, D_MODEL)),
            const((1, D_MODEL)),
            const((D_MODEL, D_FF)),
            const((D_FF, D_MODEL)),
            const((1, D_MODEL)),
        ],
        out_specs=x_blk,
        out_shape=jax.ShapeDtypeStruct(x4.shape, F32),
        scratch_shapes=[
            pltpu.VMEM((SUBLANES // MIX_STEPS, MIX_STEPS * n_chunks, D_MODEL), F32),
            pltpu.VMEM((SUBLANES // MIX_STEPS, MIX_STEPS * n_chunks, D_MODEL), BF16),
            pltpu.VMEM((MIX_STEPS * n_chunks, D_FF), BF16),
        ],
        compiler_params=_params(2),
        name="mix_ffn",
    )(x4, conv_out, y_col, u_col, d_col, glu_wt, glu_b_col, w_out, g_ffn, w1, w2, g_final)


def _s5_operator_inputs(lbr, lbi, zr, zi, b_re, b_im, c_re, c_im):
    g, p, h = N_GROUPS, SSM_STATE, SSM_GROUP

    sc = jnp.transpose(jnp.stack([lbr, lbi, zr, zi], axis=-1), (1, 0, 2, 3))
    csc = jnp.broadcast_to(sc[:, :, None], (g, 2, 2, p, 4)).reshape(g, 4 * p, 4)
    c = jnp.stack([c_re, c_im], axis=0)
    b = jnp.stack([b_re, b_im], axis=0)
    c_col = jnp.transpose(c, (2, 1, 0, 4, 3)).reshape(g, 4 * p, h)
    b_col = jnp.transpose(b, (2, 1, 0, 3, 4)).reshape(g, 4 * p, h)
    c_row = jnp.transpose(c, (2, 3, 1, 0, 4)).reshape(g, h, 4 * p)
    return csc, c_col, b_col, c_row


def kernel(x, meta_tokens, norm_mix_g, w_in, conv_w, conv_b, conv_ln_g, conv_ln_b,
           ssm_lam_re, ssm_lam_im, ssm_log_dt, ssm_b_re, ssm_b_im, ssm_c_re, ssm_c_im,
           ssm_d, ssm_glu_w, ssm_glu_b, w_out, norm_ffn_g, w_ff1, w_ff2, norm_final_g):
    assert w_in.shape[0] == 1, "single-layer block"
    bsz, seq, _ = x.shape
    assert seq % CHUNK == 0 and CHUNK >= N_META
    n_chunks = seq // CHUNK
    g, h = N_GROUPS, SSM_GROUP

    x4 = x.reshape(bsz, n_chunks, CHUNK, D_MODEL)
    g_mix = norm_mix_g[0][None, :]
    w_in_b = w_in[0].astype(BF16)
    u_conv, u_col = _in_proj_step(x4, g_mix, w_in_b)
    u_conv_m, u_ssm_m = _in_proj_meta(meta_tokens, g_mix, w_in_b)

    conv_out = _conv_module(u_conv, u_conv_m, conv_w[0], conv_b[0][None, :],
                            conv_ln_g[0][None, :], conv_ln_b[0][None, :], bsz)

    ldt = jnp.broadcast_to(ssm_log_dt[0][..., None], ssm_lam_re[0].shape)
    flat = lambda a: a.reshape(2 * g, SSM_STATE)
    lbr, lbi, zr, zi = [a.reshape(2, g, SSM_STATE) for a in
                        _zoh(flat(ssm_lam_re[0]), flat(ssm_lam_im[0]), flat(ldt))]
    csc, c_col, b_col, c_row = _s5_operator_inputs(
        lbr, lbi, zr, zi, ssm_b_re[0], ssm_b_im[0], ssm_c_re[0], ssm_c_im[0])

    u_meta = jnp.transpose(u_ssm_m.reshape(N_META, g, h), (1, 0, 2)).reshape(g, 1, N_META * h)
    y_col = _ssm(u_col, u_meta, csc, c_col, b_col, c_row, n_chunks)

    out = _mix_ffn(x4, conv_out, y_col, u_col, ssm_d[0][:, None],
                   ssm_glu_w[0].T.astype(BF16), ssm_glu_b[0][:, None],
                   w_out[0].astype(BF16), norm_ffn_g[0][None, :], w_ff1[0].astype(BF16),
                   w_ff2[0].astype(BF16), norm_final_g[None, :])
    return out.reshape(bsz, seq, D_MODEL)
```

```python
import functools

import jax
import jax.numpy as jnp
from jax import lax
from jax.experimental import pallas as pl
from jax.experimental.pallas import tpu as pltpu

F32 = jnp.float32
BF16 = jnp.bfloat16

D_MODEL = 1024
N_META = 16
D_CONV = 512
D_SSM = 512
CONV_WIDTH = 31
CONV_PAD = CONV_WIDTH // 2
SSM_GROUP = 16
N_GROUPS = D_SSM // SSM_GROUP
SSM_STATE = 64
D_FF = 4096
NORM_EPS = 1e-5
LANES = 128
SUBLANES = 8

CHUNK = 32
CHUNK_LANES = CHUNK * SSM_GROUP
STATE_ROWS = 4 * SSM_STATE
DIR_ROWS = 2 * SSM_STATE
STEPS_PER_VREG = LANES // SSM_GROUP

VMEM_LIMIT_BYTES = 60 * 1024 * 1024


def _rms(x, g):
    return x * lax.rsqrt(jnp.mean(x * x, axis=-1, keepdims=True) + NORM_EPS) * g


def _params(n_axes=1):
    return pltpu.CompilerParams(dimension_semantics=("parallel",) * n_axes,
                                vmem_limit_bytes=VMEM_LIMIT_BYTES)


IN_STEPS = 4


def _in_proj_step_kernel(x_ref, g_ref, w_ref, uc_ref, ut_ref):
    n_chunks = x_ref.shape[1]
    for part in range(SUBLANES // IN_STEPS):
        steps = range(part * IN_STEPS, (part + 1) * IN_STEPS)
        x = jnp.concatenate([x_ref[0, :, r, :] for r in steps], axis=0)
        z = _rms(x, g_ref[...]).astype(BF16)
        p = jnp.dot(z, w_ref[...], preferred_element_type=F32)
        uc = p[:, :D_CONV] * jax.nn.sigmoid(p[:, D_CONV:2 * D_CONV])
        uc_ref[part * IN_STEPS:(part + 1) * IN_STEPS] = uc.reshape(IN_STEPS, n_chunks, D_CONV)
        ut = p[:, 2 * D_CONV:].T.astype(BF16)
        for k, r in enumerate(steps):
            ut_ref[:, r * SSM_GROUP:(r + 1) * SSM_GROUP, :] = (
                ut[:, k * n_chunks:(k + 1) * n_chunks].reshape(N_GROUPS, SSM_GROUP, n_chunks))


def _in_proj_step(x4, g, w_in):
    bsz, n_chunks = x4.shape[:2]
    rows = bsz * n_chunks
    return pl.pallas_call(
        _in_proj_step_kernel,
        grid=(CHUNK // SUBLANES, bsz),
        in_specs=[
            pl.BlockSpec((1, n_chunks, SUBLANES, D_MODEL), lambda i, b: (b, 0, i, 0)),
            pl.BlockSpec((1, D_MODEL), lambda i, b: (0, 0)),
            pl.BlockSpec((D_MODEL, 2 * D_CONV + D_SSM), lambda i, b: (0, 0)),
        ],
        out_specs=[
            pl.BlockSpec((SUBLANES, n_chunks, D_CONV), lambda i, b: (i, b, 0)),
            pl.BlockSpec((N_GROUPS, SUBLANES * SSM_GROUP, n_chunks), lambda i, b: (0, i, b)),
        ],
        out_shape=[
            jax.ShapeDtypeStruct((CHUNK, rows, D_CONV), F32),
            jax.ShapeDtypeStruct((N_GROUPS, CHUNK_LANES, rows), BF16),
        ],
        compiler_params=_params(2),
        name="in_proj",
    )(x4, g, w_in)


def _in_proj_meta_kernel(x_ref, g_ref, w_ref, uc_ref, us_ref):
    z = _rms(x_ref[...], g_ref[...]).astype(BF16)
    p = jnp.dot(z, w_ref[...], preferred_element_type=F32)
    uc_ref[...] = p[:, :D_CONV] * jax.nn.sigmoid(p[:, D_CONV:2 * D_CONV])
    us_ref[...] = p[:, 2 * D_CONV:]


def _in_proj_meta(meta, g, w_in):
    return pl.pallas_call(
        _in_proj_meta_kernel,
        out_shape=[jax.ShapeDtypeStruct((N_META, D_CONV), F32),
                   jax.ShapeDtypeStruct((N_META, D_SSM), F32)],
        name="in_proj_meta",
    )(meta, g, w_in)


CONV_GROUP = 4
CONV_ROWS = 16


def _conv_kernel(u_ref, um_ref, w_ref, cb_ref, lg_ref, lb_ref, o_ref, cat_ref, acc_ref):
    n_steps, n_chunks, _ = u_ref.shape
    cat_ref[CONV_PAD:CONV_PAD + n_steps] = u_ref[...]
    chunk = lax.broadcasted_iota(jnp.int32, (n_chunks, 1), 0)
    for i in range(CONV_PAD):
        s_prev = n_steps - CONV_PAD + i
        prev = pltpu.roll(u_ref[s_prev], 1, axis=0)
        meta_row = um_ref[N_META - CONV_PAD + i:N_META - CONV_PAD + i + 1, :]
        cat_ref[i] = jnp.where(chunk == 0, meta_row, prev)
        nxt = pltpu.roll(u_ref[i], n_chunks - 1, axis=0)
        cat_ref[n_steps + CONV_PAD + i] = jnp.where(chunk == n_chunks - 1, 0.0, nxt)

    def body(sg, carry):
        s0 = sg * CONV_GROUP
        for c in range(D_CONV // LANES):
            cols = slice(c * LANES, (c + 1) * LANES)
            taps = [jnp.broadcast_to(w_ref[k:k + 1, cols], (SUBLANES, LANES))
                    for k in range(CONV_WIDTH)]

            def rows_body(rt, carry2):
                for sub in range(CONV_ROWS // SUBLANES):
                    r0 = pl.multiple_of(rt * CONV_ROWS + sub * SUBLANES, SUBLANES)
                    accs = [[jnp.zeros((SUBLANES, LANES), F32)] * 2 for _ in range(CONV_GROUP)]
                    for i in range(CONV_WIDTH + CONV_GROUP - 1):
                        d = cat_ref[s0 + i, pl.ds(r0, SUBLANES), cols]
                        for j in range(CONV_GROUP):
                            k = i - j
                            if 0 <= k < CONV_WIDTH:
                                accs[j][k % 2] = accs[j][k % 2] + d * taps[k]
                    for j in range(CONV_GROUP):
                        acc_ref[j, pl.ds(r0, SUBLANES), cols] = accs[j][0] + accs[j][1]
                return carry2

            lax.fori_loop(0, n_chunks // CONV_ROWS, rows_body, 0)
        for j in range(CONV_GROUP):
            y = acc_ref[j] + cb_ref[...]
            yc = y - jnp.mean(y, axis=-1, keepdims=True)
            yn = yc * lax.rsqrt(jnp.mean(yc * yc, axis=-1, keepdims=True) + NORM_EPS)
            yn = yn * lg_ref[...] + lb_ref[...]
            o_ref[s0 + j] = (yn * jax.nn.sigmoid(yn)).astype(BF16)
        return carry

    lax.fori_loop(0, n_steps // CONV_GROUP, body, 0)


def _conv_module(u_conv, u_conv_meta, conv_w, conv_b, ln_g, ln_b, bsz):
    n_steps, rows, _ = u_conv.shape
    n_chunks = rows // bsz
    vec = pl.BlockSpec((1, D_CONV), lambda b: (0, 0))
    blk = pl.BlockSpec((n_steps, n_chunks, D_CONV), lambda b: (0, b, 0))
    return pl.pallas_call(
        _conv_kernel,
        grid=(bsz,),
        in_specs=[
            blk,
            pl.BlockSpec((N_META, D_CONV), lambda b: (0, 0)),
            pl.BlockSpec((CONV_WIDTH, D_CONV), lambda b: (0, 0)),
            vec, vec, vec,
        ],
        out_specs=blk,
        out_shape=jax.ShapeDtypeStruct((n_steps, rows, D_CONV), BF16),
        scratch_shapes=[pltpu.VMEM((n_steps + 2 * CONV_PAD, n_chunks, D_CONV), F32),
                        pltpu.VMEM((CONV_GROUP, n_chunks, D_CONV), F32)],
        compiler_params=_params(),
        name="conv_module",
    )(u_conv, u_conv_meta, conv_w, conv_b, ln_g, ln_b)


def _zoh_kernel(lre_ref, lim_ref, ldt_ref, lbr_ref, lbi_ref, zr_ref, zi_ref):
    lre = lre_ref[...]
    lim = lim_ref[...]
    dt = jnp.exp(ldt_ref[...])
    ea = jnp.exp(lre * dt)
    lbr = ea * jnp.cos(lim * dt)
    lbi = ea * jnp.sin(lim * dt)
    nr = lbr - 1.0
    den = lre * lre + lim * lim
    lbr_ref[...] = lbr
    lbi_ref[...] = lbi
    zr_ref[...] = (nr * lre + lbi * lim) / den
    zi_ref[...] = (lbi * lre - nr * lim) / den


def _zoh(lam_re, lam_im, log_dt):
    shape = jax.ShapeDtypeStruct(lam_re.shape, F32)
    return pl.pallas_call(_zoh_kernel, out_shape=[shape] * 4, name="s5_zoh")(
        lam_re, lam_im, log_dt)


def _lanes(x, n):
    return jnp.concatenate([x] * (n // LANES), axis=1)


def _cmul(ar, ai, br, bi):
    return ar * br - ai * bi, ar * bi + ai * br


def _cpow(br, bi, expo, nbits):
    rr = jnp.ones(expo.shape, F32)
    ri = jnp.zeros(expo.shape, F32)
    for k in range(nbits):
        bit = ((expo >> k) & 1) == 1
        nr, ni = _cmul(rr, ri, br, bi)
        rr = jnp.where(bit, nr, rr)
        ri = jnp.where(bit, ni, ri)
        br, bi = _cmul(br, bi, br, bi)
    return rr, ri


def _build_chunk_ops(gi, csc_ref, cc_ref, bc_ref, cr_ref, um_ref, toep_ref, wend_ref,
                     wout_ref):
    t = CHUNK
    rows = STATE_ROWS
    n_cols = CHUNK_LANES // LANES
    lam = (jnp.broadcast_to(csc_ref[gi, :, 0:1], (rows, LANES)),
           jnp.broadcast_to(csc_ref[gi, :, 1:2], (rows, LANES)))
    zr = jnp.broadcast_to(csc_ref[gi, :, 2:3], (rows, LANES))
    zi = jnp.broadcast_to(csc_ref[gi, :, 3:4], (rows, LANES))
    sel = (lax.broadcasted_iota(jnp.int32, (SSM_GROUP, LANES), 1) % SSM_GROUP
           == lax.broadcasted_iota(jnp.int32, (SSM_GROUP, LANES), 0)).astype(F32)

    def tile(ref):
        return jnp.dot(ref[gi], sel, precision=lax.Precision.HIGHEST,
                       preferred_element_type=F32)

    def swap_re_im(a):
        p = SSM_STATE
        return jnp.concatenate([a[p:2 * p], a[0:p], a[3 * p:], a[2 * p:3 * p]], axis=0)

    c_same, b_same = tile(cc_ref), tile(bc_ref)
    c_swap, b_swap = swap_re_im(c_same), swap_re_im(b_same)
    row = lax.broadcasted_iota(jnp.int32, (rows, LANES), 0)
    is_re = (row // SSM_STATE) % 2 == 0
    sgn = jnp.where(is_re, -1.0, 1.0)
    ca = jnp.where(is_re, c_same, -c_same)
    cb = -c_swap
    bb_same = zr * b_same + sgn * zi * b_swap
    sbb_swap = sgn * (zr * b_swap - sgn * zi * b_same)

    pows = {1: lam}
    k = 1
    while k < t:
        pows[2 * k] = _cmul(*pows[k], *pows[k])
        k *= 2
    step_bits = STEPS_PER_VREG.bit_length() - 1
    fwd = slice(0, DIR_ROWS)
    bwd = slice(DIR_ROWS, rows)
    every = slice(0, rows)

    def rows_of(v, rs):
        return v[0][rs], v[1][rs]

    def column(base, e, rs):
        out = rows_of(base, rs)
        for bit, val in pows.items():
            if e & bit:
                out = _cmul(*out, *rows_of(val, rs))
        return out

    def times_b(p, rs):
        return p[0] * bb_same[rs] + p[1] * sbb_swap[rs]

    def times_c(p, rs):
        return ca[rs] * p[0] + cb[rs] * p[1]

    i8 = lax.broadcasted_iota(jnp.int32, (rows, LANES), 1) // SSM_GROUP
    asc0 = _cpow(*lam, i8, step_bits)
    desc0 = _cpow(*lam, STEPS_PER_VREG - 1 - i8, step_bits)
    asc1_0 = _cmul(*asc0, *lam)
    desc1_0 = _cmul(*desc0, *lam)

    wend_f, wend_b, lag_b, wout_f, wout_b = [], [], [], [], []
    for q in range(n_cols):
        e_asc = STEPS_PER_VREG * q
        e_desc = STEPS_PER_VREG * (n_cols - 1 - q)
        asc1_q = column(asc1_0, e_asc, every)
        wend_f.append(times_b(column(desc0, e_desc, fwd), fwd))
        wend_b.append(times_b(column(asc0, e_asc, bwd), bwd))
        lag_b.append(times_b(rows_of(asc1_q, bwd), bwd))
        wout_f.append(times_c(rows_of(asc1_q, fwd), fwd))
        wout_b.append(times_c(column(desc1_0, e_desc, bwd), bwd))
        cols = slice(q * LANES, (q + 1) * LANES)
        wend_ref[gi, fwd, cols] = wend_f[q].astype(BF16)
        wend_ref[gi, bwd, cols] = wend_b[q].astype(BF16)

    meta_cols = N_META * SSM_GROUP // LANES
    x0 = sum(jnp.sum(wend_f[n_cols - meta_cols + m] * um_ref[gi, :, m * LANES:(m + 1) * LANES],
                     axis=1, keepdims=True) for m in range(meta_cols))

    lane = lax.broadcasted_iota(jnp.int32, (DIR_ROWS, LANES), 1)
    lag0_b = jnp.where(lane >= LANES - SSM_GROUP, bb_same[bwd], 0.0)
    zero = jnp.zeros((DIR_ROWS, LANES), F32)
    bcat = jnp.concatenate(
        [jnp.concatenate(wend_f + [zero] * n_cols, axis=1),
         jnp.concatenate([zero] * (n_cols - 1) + [lag0_b] + lag_b, axis=1)], axis=0)
    lane_r = lax.broadcasted_iota(jnp.int32, (SSM_GROUP, rows), 1)
    c2 = jnp.where((lane_r // SSM_STATE) % 2 == 0, cr_ref[gi], -cr_ref[gi])
    kk = jnp.dot(c2, bcat, precision=lax.Precision.HIGHEST,
                 preferred_element_type=F32)
    for tt in range(t):
        off = (t - 1 - tt) * SSM_GROUP
        toep_ref[gi, tt * SSM_GROUP:(tt + 1) * SSM_GROUP, :] = (
            kk[:, off:off + CHUNK_LANES].astype(BF16))

    wout = jnp.concatenate([jnp.concatenate(wout_f, axis=1),
                            jnp.concatenate(wout_b, axis=1)], axis=0)
    wout_ref[gi] = wout.T.astype(BF16)
    return pows[t], x0


SSM_GROUPS_PER_STEP = 2


def _chunk_scan(er, ei, ar, ai, x0, chunk, n_chunks, forward):
    width = er.shape[1]
    xr, xi = er, ei
    if x0 is not None:
        fr, fi = _cmul(ar, ai, x0[0], x0[1])
        xr = xr + jnp.where(chunk == 0, _lanes(fr, width), 0.0)
        xi = xi + jnp.where(chunk == 0, _lanes(fi, width), 0.0)

    def shifted(v, sh):
        if forward:
            return jnp.where(chunk >= sh, pltpu.roll(v, sh, axis=1), 0.0)
        return jnp.where(chunk < n_chunks - sh, pltpu.roll(v, width - sh, axis=1), 0.0)

    sh = 1
    while sh < n_chunks:
        sr, si = shifted(xr, sh), shifted(xi, sh)
        wr, wi = _lanes(ar, width), _lanes(ai, width)
        xr, xi = xr + wr * sr - wi * si, xi + wr * si + wi * sr
        ar, ai = _cmul(ar, ai, ar, ai)
        sh *= 2
    inr, ini = shifted(xr, 1), shifted(xi, 1)
    if x0 is not None:
        inr = jnp.where(chunk == 0, x0[0], inr)
        ini = jnp.where(chunk == 0, x0[1], ini)
    return inr, ini


def _ssm_kernel(u_ref, um_ref, csc_ref, cc_ref, bc_ref, cr_ref, y_ref, toep_ref, wend_ref,
                wout_ref, *, n_chunks):
    for gi in range(SSM_GROUPS_PER_STEP):
        (lam_r, lam_i), x0 = _build_chunk_ops(gi, csc_ref, cc_ref, bc_ref, cr_ref, um_ref,
                                              toep_ref, wend_ref, wout_ref)
        u = u_ref[gi]
        width = u.shape[1]
        e = jnp.dot(wend_ref[gi], u, preferred_element_type=F32)
        chunk = lax.broadcasted_iota(jnp.int32, (1, width), 1) % n_chunks
        p = SSM_STATE
        ar, ai = lam_r, lam_i
        f_in = _chunk_scan(e[0:p], e[p:2 * p], ar[0:p], ai[0:p], (x0[0:p], x0[p:2 * p]),
                           chunk, n_chunks, True)
        b_in = _chunk_scan(e[2 * p:3 * p], e[3 * p:], ar[2 * p:3 * p], ai[2 * p:3 * p],
                           None, chunk, n_chunks, False)
        xin = jnp.concatenate([f_in[0], f_in[1], b_in[0], b_in[1]], axis=0).astype(BF16)
        y = jnp.dot(toep_ref[gi], u, preferred_element_type=F32)
        y = y + jnp.dot(wout_ref[gi], xin, preferred_element_type=F32)
        y_ref[gi] = y.astype(BF16)


def _ssm(u_col, u_meta, csc, c_col, b_col, c_row, n_chunks):
    g, _, width = u_col.shape
    per = SSM_GROUPS_PER_STEP
    return pl.pallas_call(
        functools.partial(_ssm_kernel, n_chunks=n_chunks),
        grid=(g // per,),
        in_specs=[
            pl.BlockSpec((per, CHUNK_LANES, width), lambda i: (i, 0, 0)),
            pl.BlockSpec((per, 1, N_META * SSM_GROUP), lambda i: (i, 0, 0)),
            pl.BlockSpec((per, STATE_ROWS, 4), lambda i: (i, 0, 0)),
            pl.BlockSpec((per, STATE_ROWS, SSM_GROUP), lambda i: (i, 0, 0)),
            pl.BlockSpec((per, STATE_ROWS, SSM_GROUP), lambda i: (i, 0, 0)),
            pl.BlockSpec((per, SSM_GROUP, STATE_ROWS), lambda i: (i, 0, 0)),
        ],
        out_specs=pl.BlockSpec((per, CHUNK_LANES, width), lambda i: (i, 0, 0)),
        out_shape=jax.ShapeDtypeStruct((g, CHUNK_LANES, width), BF16),
        scratch_shapes=[
            pltpu.VMEM((per, CHUNK_LANES, CHUNK_LANES), BF16),
            pltpu.VMEM((per, STATE_ROWS, CHUNK_LANES), BF16),
            pltpu.VMEM((per, CHUNK_LANES, STATE_ROWS), BF16),
        ],
        compiler_params=_params(),
        name="s5_mixer",
    )(u_col, u_meta, csc, c_col, b_col, c_row)


FF_CHUNK = 1024
MIX_STEPS = 4


def _step_copies(hbm_ref, buf_ref, sem_ref, tile, slot, to_hbm):
    bsz = hbm_ref.shape[0]
    slab, b = tile // bsz, tile % bsz
    copies = []
    for r in range(SUBLANES):
        hbm = hbm_ref.at[b, :, slab * SUBLANES + r, :]
        vmem = buf_ref.at[slot, r]
        src, dst = (vmem, hbm) if to_hbm else (hbm, vmem)
        copies.append(pltpu.make_async_copy(src, dst, sem_ref.at[slot, r]))
    return copies


def _mix_ffn_kernel(x_hbm, co_ref, y_ref, ut_ref, d_ref, gwt_ref, gb_ref, wo_ref,
                    gf_ref, w1_ref, w2_ref, gl_ref, o_hbm, xbuf, obuf, sem_in, sem_out,
                    h_ref, z_ref, a_ref):
    n_chunks = x_hbm.shape[1]
    n_parts = SUBLANES // MIX_STEPS
    tile = pl.program_id(0)
    n_tiles = pl.num_programs(0)
    slot = tile % 2

    @pl.when(tile == 0)
    def _():
        for cp in _step_copies(x_hbm, xbuf, sem_in, tile, slot, False):
            cp.start()

    @pl.when(tile + 1 < n_tiles)
    def _():
        for cp in _step_copies(x_hbm, xbuf, sem_in, tile + 1, 1 - slot, False):
            cp.start()

    for cp in _step_copies(x_hbm, xbuf, sem_in, tile, slot, False):
        cp.wait()

    @pl.when(tile >= 2)
    def _():
        for cp in _step_copies(o_hbm, obuf, sem_out, tile - 2, slot, True):
            cp.wait()

    def steps_of(part):
        return range(part * MIX_STEPS, (part + 1) * MIX_STEPS)

    def mix(part):
        def col(ref):
            return jnp.concatenate(
                [ref[:, r * SSM_GROUP:(r + 1) * SSM_GROUP, :].reshape(D_SSM, n_chunks)
                 for r in steps_of(part)], axis=1).astype(F32)

        yt = col(y_ref) + d_ref[...] * col(ut_ref)
        ge = jax.nn.gelu(yt)
        gate = (jnp.dot(gwt_ref[...], ge.astype(BF16), preferred_element_type=F32)
                + gb_ref[...])
        s5 = (ge * jax.nn.sigmoid(gate)).T.astype(BF16)
        co = co_ref[part * MIX_STEPS:(part + 1) * MIX_STEPS].reshape(
            MIX_STEPS * n_chunks, D_CONV)
        x = xbuf[slot, part * MIX_STEPS:(part + 1) * MIX_STEPS].reshape(
            MIX_STEPS * n_chunks, D_MODEL)
        h = x + jnp.dot(jnp.concatenate([co, s5], axis=1), wo_ref[...],
                        preferred_element_type=F32)
        h_ref[part] = h
        z_ref[part] = _rms(h, gf_ref[...]).astype(BF16)

    def ffn_up(part, j):
        cols = slice(j * FF_CHUNK, (j + 1) * FF_CHUNK)
        a = jnp.dot(z_ref[part], w1_ref[:, cols], preferred_element_type=F32)
        a_ref[:, cols] = jnp.square(jnp.maximum(a, 0.0)).astype(BF16)

    def ffn_down(part):
        h = h_ref[part] + jnp.dot(a_ref[...], w2_ref[...], preferred_element_type=F32)
        out = _rms(h, gl_ref[...])
        obuf[slot, part * MIX_STEPS:(part + 1) * MIX_STEPS] = out.reshape(
            MIX_STEPS, n_chunks, D_MODEL)

    for part in range(n_parts):
        mix(part)
        for j in range(D_FF // FF_CHUNK):
            ffn_up(part, j)
        ffn_down(part)

    for cp in _step_copies(o_hbm, obuf, sem_out, tile, slot, True):
        cp.start()

    @pl.when(tile == n_tiles - 1)
    def _():
        for cp in (_step_copies(o_hbm, obuf, sem_out, tile - 1, 1 - slot, True)
                   + _step_copies(o_hbm, obuf, sem_out, tile, slot, True)):
            cp.wait()


def _mix_ffn(x4, conv_out, y_col, u_col, d_col, glu_wt, glu_b_col, w_out, g_ffn, w1, w2,
             g_final):
    bsz, n_chunks = x4.shape[:2]

    def const(shape):
        return pl.BlockSpec(shape, lambda t: (0, 0), pipeline_mode=pl.Buffered(1))

    hbm = pl.BlockSpec(memory_space=pl.ANY)
    col_blk = pl.BlockSpec((N_GROUPS, SUBLANES * SSM_GROUP, n_chunks),
                           lambda t: (0, t // bsz, t % bsz))
    step_buf = pltpu.VMEM((2, SUBLANES, n_chunks, D_MODEL), F32)
    step_sems = pltpu.SemaphoreType.DMA((2, SUBLANES))
    return pl.pallas_call(
        _mix_ffn_kernel,
        grid=(CHUNK // SUBLANES * bsz,),
        in_specs=[
            hbm,
            pl.BlockSpec((SUBLANES, n_chunks, D_CONV), lambda t: (t // bsz, t % bsz, 0)),
            col_blk,
            col_blk,
            const((D_SSM, 1)),
            const((D_SSM, D_SSM)),
            const((D_SSM, 1)),
            const((D_MODEL, D_MODEL)),
            const((1, D_MODEL)),
            const((D_MODEL, D_FF)),
            const((D_FF, D_MODEL)),
            const((1, D_MODEL)),
        ],
        out_specs=hbm,
        out_shape=jax.ShapeDtypeStruct(x4.shape, F32),
        scratch_shapes=[
            step_buf, step_buf, step_sems, step_sems,
            pltpu.VMEM((SUBLANES // MIX_STEPS, MIX_STEPS * n_chunks, D_MODEL), F32),
            pltpu.VMEM((SUBLANES // MIX_STEPS, MIX_STEPS * n_chunks, D_MODEL), BF16),
            pltpu.VMEM((MIX_STEPS * n_chunks, D_FF), BF16),
        ],
        compiler_params=pltpu.CompilerParams(dimension_semantics=("arbitrary",),
                                             vmem_limit_bytes=VMEM_LIMIT_BYTES),
        name="mix_ffn",
    )(x4, conv_out, y_col, u_col, d_col, glu_wt, glu_b_col, w_out, g_ffn, w1, w2, g_final)


def _s5_operator_inputs(lbr, lbi, zr, zi, b_re, b_im, c_re, c_im):
    g, p, h = N_GROUPS, SSM_STATE, SSM_GROUP

    sc = jnp.transpose(jnp.stack([lbr, lbi, zr, zi], axis=-1), (1, 0, 2, 3))
    csc = jnp.broadcast_to(sc[:, :, None], (g, 2, 2, p, 4)).reshape(g, 4 * p, 4)
    c = jnp.stack([c_re, c_im], axis=0)
    b = jnp.stack([b_re, b_im], axis=0)
    c_col = jnp.transpose(c, (2, 1, 0, 4, 3)).reshape(g, 4 * p, h)
    b_col = jnp.transpose(b, (2, 1, 0, 3, 4)).reshape(g, 4 * p, h)
    c_row = jnp.transpose(c, (2, 3, 1, 0, 4)).reshape(g, h, 4 * p)
    return csc, c_col, b_col, c_row


def kernel(x, meta_tokens, norm_mix_g, w_in, conv_w, conv_b, conv_ln_g, conv_ln_b,
           ssm_lam_re, ssm_lam_im, ssm_log_dt, ssm_b_re, ssm_b_im, ssm_c_re, ssm_c_im,
           ssm_d, ssm_glu_w, ssm_glu_b, w_out, norm_ffn_g, w_ff1, w_ff2, norm_final_g):
    assert w_in.shape[0] == 1, "single-layer block"
    bsz, seq, _ = x.shape
    assert seq % CHUNK == 0 and CHUNK >= N_META
    n_chunks = seq // CHUNK
    g, h = N_GROUPS, SSM_GROUP

    x4 = x.reshape(bsz, n_chunks, CHUNK, D_MODEL)
    g_mix = norm_mix_g[0][None, :]
    w_in_b = w_in[0].astype(BF16)
    u_conv, u_col = _in_proj_step(x4, g_mix, w_in_b)
    u_conv_m, u_ssm_m = _in_proj_meta(meta_tokens, g_mix, w_in_b)

    conv_out = _conv_module(u_conv, u_conv_m, conv_w[0], conv_b[0][None, :],
                            conv_ln_g[0][None, :], conv_ln_b[0][None, :], bsz)

    ldt = jnp.broadcast_to(ssm_log_dt[0][..., None], ssm_lam_re[0].shape)
    flat = lambda a: a.reshape(2 * g, SSM_STATE)
    lbr, lbi, zr, zi = [a.reshape(2, g, SSM_STATE) for a in
                        _zoh(flat(ssm_lam_re[0]), flat(ssm_lam_im[0]), flat(ldt))]
    csc, c_col, b_col, c_row = _s5_operator_inputs(
        lbr, lbi, zr, zi, ssm_b_re[0], ssm_b_im[0], ssm_c_re[0], ssm_c_im[0])

    u_meta = jnp.transpose(u_ssm_m.reshape(N_META, g, h), (1, 0, 2)).reshape(g, 1, N_META * h)
    y_col = _ssm(u_col, u_meta, csc, c_col, b_col, c_row, n_chunks)

    out = _mix_ffn(x4, conv_out, y_col, u_col, ssm_d[0][:, None],
                   ssm_glu_w[0].T.astype(BF16), ssm_glu_b[0][:, None],
                   w_out[0].astype(BF16), norm_ffn_g[0][None, :], w_ff1[0].astype(BF16),
                   w_ff2[0].astype(BF16), norm_final_g[None, :])
    return out.reshape(bsz, seq, D_MODEL)
```

```python
import functools

import jax
import jax.numpy as jnp
from jax import lax
from jax.experimental import pallas as pl
from jax.experimental.pallas import tpu as pltpu

F32 = jnp.float32
BF16 = jnp.bfloat16

D_MODEL = 1024
N_META = 16
D_CONV = 512
D_SSM = 512
CONV_WIDTH = 31
CONV_PAD = CONV_WIDTH // 2
SSM_GROUP = 16
N_GROUPS = D_SSM // SSM_GROUP
SSM_STATE = 64
D_FF = 4096
NORM_EPS = 1e-5
LANES = 128
SUBLANES = 8

CHUNK = 32
CHUNK_LANES = CHUNK * SSM_GROUP
STATE_ROWS = 4 * SSM_STATE
DIR_ROWS = 2 * SSM_STATE
STEPS_PER_VREG = LANES // SSM_GROUP

VMEM_LIMIT_BYTES = 60 * 1024 * 1024


def _rms(x, g):
    return x * lax.rsqrt(jnp.mean(x * x, axis=-1, keepdims=True) + NORM_EPS) * g


def _params(n_axes=1):
    return pltpu.CompilerParams(dimension_semantics=("parallel",) * n_axes,
                                vmem_limit_bytes=VMEM_LIMIT_BYTES)


def _step_copies(hbm_ref, buf_ref, sem_ref, tile, slot, to_hbm):
    bsz = hbm_ref.shape[0]
    slab, b = tile // bsz, tile % bsz
    copies = []
    for r in range(SUBLANES):
        hbm = hbm_ref.at[b, :, slab * SUBLANES + r, :]
        vmem = buf_ref.at[slot, r]
        src, dst = (vmem, hbm) if to_hbm else (hbm, vmem)
        copies.append(pltpu.make_async_copy(src, dst, sem_ref.at[slot, r]))
    return copies


def _fetch_steps(x_hbm, xbuf, sem):
    tile = pl.program_id(0)
    slot = tile % 2

    @pl.when(tile == 0)
    def _():
        for cp in _step_copies(x_hbm, xbuf, sem, tile, slot, False):
            cp.start()

    @pl.when(tile + 1 < pl.num_programs(0))
    def _():
        for cp in _step_copies(x_hbm, xbuf, sem, tile + 1, 1 - slot, False):
            cp.start()

    for cp in _step_copies(x_hbm, xbuf, sem, tile, slot, False):
        cp.wait()
    return slot


def _step_tile_params():
    return pltpu.CompilerParams(dimension_semantics=("arbitrary",),
                                vmem_limit_bytes=VMEM_LIMIT_BYTES)


def _in_proj_step_kernel(x_hbm, g_ref, w_ref, uc_ref, ut_ref, xbuf, sem):
    n_chunks = x_hbm.shape[1]
    slot = _fetch_steps(x_hbm, xbuf, sem)
    x = xbuf[slot].reshape(SUBLANES * n_chunks, D_MODEL)
    z = _rms(x, g_ref[...]).astype(BF16)
    p = jnp.dot(z, w_ref[...], preferred_element_type=F32)
    uc = p[:, :D_CONV] * jax.nn.sigmoid(p[:, D_CONV:2 * D_CONV])
    uc_ref[...] = uc.reshape(SUBLANES, n_chunks, D_CONV)
    ut = p[:, 2 * D_CONV:].T.astype(BF16)
    for r in range(SUBLANES):
        ut_ref[:, r * SSM_GROUP:(r + 1) * SSM_GROUP, :] = (
            ut[:, r * n_chunks:(r + 1) * n_chunks].reshape(N_GROUPS, SSM_GROUP, n_chunks))


def _in_proj_step(x4, g, w_in):
    bsz, n_chunks = x4.shape[:2]
    rows = bsz * n_chunks
    return pl.pallas_call(
        _in_proj_step_kernel,
        grid=(CHUNK // SUBLANES * bsz,),
        in_specs=[
            pl.BlockSpec(memory_space=pl.ANY),
            pl.BlockSpec((1, D_MODEL), lambda t: (0, 0)),
            pl.BlockSpec((D_MODEL, 2 * D_CONV + D_SSM), lambda t: (0, 0)),
        ],
        out_specs=[
            pl.BlockSpec((SUBLANES, n_chunks, D_CONV), lambda t: (t // bsz, t % bsz, 0)),
            pl.BlockSpec((N_GROUPS, SUBLANES * SSM_GROUP, n_chunks),
                         lambda t: (0, t // bsz, t % bsz)),
        ],
        out_shape=[
            jax.ShapeDtypeStruct((CHUNK, rows, D_CONV), F32),
            jax.ShapeDtypeStruct((N_GROUPS, CHUNK_LANES, rows), BF16),
        ],
        scratch_shapes=[pltpu.VMEM((2, SUBLANES, n_chunks, D_MODEL), F32),
                        pltpu.SemaphoreType.DMA((2, SUBLANES))],
        compiler_params=_step_tile_params(),
        name="in_proj",
    )(x4, g, w_in)


def _in_proj_meta_kernel(x_ref, g_ref, w_ref, uc_ref, us_ref):
    z = _rms(x_ref[...], g_ref[...]).astype(BF16)
    p = jnp.dot(z, w_ref[...], preferred_element_type=F32)
    uc_ref[...] = p[:, :D_CONV] * jax.nn.sigmoid(p[:, D_CONV:2 * D_CONV])
    us_ref[...] = p[:, 2 * D_CONV:]


def _in_proj_meta(meta, g, w_in):
    return pl.pallas_call(
        _in_proj_meta_kernel,
        out_shape=[jax.ShapeDtypeStruct((N_META, D_CONV), F32),
                   jax.ShapeDtypeStruct((N_META, D_SSM), F32)],
        name="in_proj_meta",
    )(meta, g, w_in)


CONV_GROUP = 4
CONV_ROWS = 16


def _conv_kernel(u_ref, um_ref, w_ref, cb_ref, lg_ref, lb_ref, o_ref, cat_ref, acc_ref):
    n_steps, n_chunks, _ = u_ref.shape
    cat_ref[CONV_PAD:CONV_PAD + n_steps] = u_ref[...]
    chunk = lax.broadcasted_iota(jnp.int32, (n_chunks, 1), 0)
    for i in range(CONV_PAD):
        s_prev = n_steps - CONV_PAD + i
        prev = pltpu.roll(u_ref[s_prev], 1, axis=0)
        meta_row = um_ref[N_META - CONV_PAD + i:N_META - CONV_PAD + i + 1, :]
        cat_ref[i] = jnp.where(chunk == 0, meta_row, prev)
        nxt = pltpu.roll(u_ref[i], n_chunks - 1, axis=0)
        cat_ref[n_steps + CONV_PAD + i] = jnp.where(chunk == n_chunks - 1, 0.0, nxt)

    def body(sg, carry):
        s0 = sg * CONV_GROUP
        for c in range(D_CONV // LANES):
            cols = slice(c * LANES, (c + 1) * LANES)
            taps = [jnp.broadcast_to(w_ref[k:k + 1, cols], (SUBLANES, LANES))
                    for k in range(CONV_WIDTH)]

            def rows_body(rt, carry2):
                for sub in range(CONV_ROWS // SUBLANES):
                    r0 = pl.multiple_of(rt * CONV_ROWS + sub * SUBLANES, SUBLANES)
                    accs = [jnp.zeros((SUBLANES, LANES), F32)] * CONV_GROUP
                    for i in range(CONV_WIDTH + CONV_GROUP - 1):
                        d = cat_ref[s0 + i, pl.ds(r0, SUBLANES), cols]
                        for j in range(CONV_GROUP):
                            if 0 <= i - j < CONV_WIDTH:
                                accs[j] = accs[j] + d * taps[i - j]
                    for j in range(CONV_GROUP):
                        acc_ref[j, pl.ds(r0, SUBLANES), cols] = accs[j]
                return carry2

            lax.fori_loop(0, n_chunks // CONV_ROWS, rows_body, 0)
        for j in range(CONV_GROUP):
            y = acc_ref[j] + cb_ref[...]
            yc = y - jnp.mean(y, axis=-1, keepdims=True)
            yn = yc * lax.rsqrt(jnp.mean(yc * yc, axis=-1, keepdims=True) + NORM_EPS)
            yn = yn * lg_ref[...] + lb_ref[...]
            o_ref[s0 + j] = (yn * jax.nn.sigmoid(yn)).astype(BF16)
        return carry

    lax.fori_loop(0, n_steps // CONV_GROUP, body, 0)


def _conv_module(u_conv, u_conv_meta, conv_w, conv_b, ln_g, ln_b, bsz):
    n_steps, rows, _ = u_conv.shape
    n_chunks = rows // bsz
    vec = pl.BlockSpec((1, D_CONV), lambda b: (0, 0))
    blk = pl.BlockSpec((n_steps, n_chunks, D_CONV), lambda b: (0, b, 0))
    return pl.pallas_call(
        _conv_kernel,
        grid=(bsz,),
        in_specs=[
            blk,
            pl.BlockSpec((N_META, D_CONV), lambda b: (0, 0)),
            pl.BlockSpec((CONV_WIDTH, D_CONV), lambda b: (0, 0)),
            vec, vec, vec,
        ],
        out_specs=blk,
        out_shape=jax.ShapeDtypeStruct((n_steps, rows, D_CONV), BF16),
        scratch_shapes=[pltpu.VMEM((n_steps + 2 * CONV_PAD, n_chunks, D_CONV), F32),
                        pltpu.VMEM((CONV_GROUP, n_chunks, D_CONV), F32)],
        compiler_params=_params(),
        name="conv_module",
    )(u_conv, u_conv_meta, conv_w, conv_b, ln_g, ln_b)


def _zoh_kernel(lre_ref, lim_ref, ldt_ref, lbr_ref, lbi_ref, zr_ref, zi_ref):
    lre = lre_ref[...]
    lim = lim_ref[...]
    dt = jnp.exp(ldt_ref[...])
    ea = jnp.exp(lre * dt)
    lbr = ea * jnp.cos(lim * dt)
    lbi = ea * jnp.sin(lim * dt)
    nr = lbr - 1.0
    den = lre * lre + lim * lim
    lbr_ref[...] = lbr
    lbi_ref[...] = lbi
    zr_ref[...] = (nr * lre + lbi * lim) / den
    zi_ref[...] = (lbi * lre - nr * lim) / den


def _zoh(lam_re, lam_im, log_dt):
    shape = jax.ShapeDtypeStruct(lam_re.shape, F32)
    return pl.pallas_call(_zoh_kernel, out_shape=[shape] * 4, name="s5_zoh")(
        lam_re, lam_im, log_dt)


def _lanes(x, n):
    return jnp.concatenate([x] * (n // LANES), axis=1)


def _cmul(ar, ai, br, bi):
    return ar * br - ai * bi, ar * bi + ai * br


def _cpow(br, bi, expo, nbits):
    rr = jnp.ones(expo.shape, F32)
    ri = jnp.zeros(expo.shape, F32)
    for k in range(nbits):
        bit = ((expo >> k) & 1) == 1
        nr, ni = _cmul(rr, ri, br, bi)
        rr = jnp.where(bit, nr, rr)
        ri = jnp.where(bit, ni, ri)
        br, bi = _cmul(br, bi, br, bi)
    return rr, ri


def _build_chunk_ops(gi, csc_ref, cc_ref, bc_ref, cr_ref, um_ref, toep_ref, wend_ref,
                     wout_ref):
    t = CHUNK
    rows = STATE_ROWS
    n_cols = CHUNK_LANES // LANES
    lam = (jnp.broadcast_to(csc_ref[gi, :, 0:1], (rows, LANES)),
           jnp.broadcast_to(csc_ref[gi, :, 1:2], (rows, LANES)))
    zr = jnp.broadcast_to(csc_ref[gi, :, 2:3], (rows, LANES))
    zi = jnp.broadcast_to(csc_ref[gi, :, 3:4], (rows, LANES))
    sel = (lax.broadcasted_iota(jnp.int32, (SSM_GROUP, LANES), 1) % SSM_GROUP
           == lax.broadcasted_iota(jnp.int32, (SSM_GROUP, LANES), 0)).astype(F32)

    def tile(ref):
        return jnp.dot(ref[gi], sel, precision=lax.Precision.HIGHEST,
                       preferred_element_type=F32)

    def swap_re_im(a):
        p = SSM_STATE
        return jnp.concatenate([a[p:2 * p], a[0:p], a[3 * p:], a[2 * p:3 * p]], axis=0)

    c_same, b_same = tile(cc_ref), tile(bc_ref)
    c_swap, b_swap = swap_re_im(c_same), swap_re_im(b_same)
    row = lax.broadcasted_iota(jnp.int32, (rows, LANES), 0)
    is_re = (row // SSM_STATE) % 2 == 0
    sgn = jnp.where(is_re, -1.0, 1.0)
    ca = jnp.where(is_re, c_same, -c_same)
    cb = -c_swap
    bb_same = zr * b_same + sgn * zi * b_swap
    sbb_swap = sgn * (zr * b_swap - sgn * zi * b_same)

    pows = {1: lam}
    k = 1
    while k < t:
        pows[2 * k] = _cmul(*pows[k], *pows[k])
        k *= 2
    step_bits = STEPS_PER_VREG.bit_length() - 1
    fwd = slice(0, DIR_ROWS)
    bwd = slice(DIR_ROWS, rows)
    every = slice(0, rows)

    def rows_of(v, rs):
        return v[0][rs], v[1][rs]

    def column(base, e, rs):
        out = rows_of(base, rs)
        for bit, val in pows.items():
            if e & bit:
                out = _cmul(*out, *rows_of(val, rs))
        return out

    def times_b(p, rs):
        return p[0] * bb_same[rs] + p[1] * sbb_swap[rs]

    def times_c(p, rs):
        return ca[rs] * p[0] + cb[rs] * p[1]

    i8 = lax.broadcasted_iota(jnp.int32, (rows, LANES), 1) // SSM_GROUP
    asc0 = _cpow(*lam, i8, step_bits)
    desc0 = _cpow(*lam, STEPS_PER_VREG - 1 - i8, step_bits)
    asc1_0 = _cmul(*asc0, *lam)
    desc1_0 = _cmul(*desc0, *lam)

    wend_f, wend_b, lag_b, wout_f, wout_b = [], [], [], [], []
    for q in range(n_cols):
        e_asc = STEPS_PER_VREG * q
        e_desc = STEPS_PER_VREG * (n_cols - 1 - q)
        asc1_q = column(asc1_0, e_asc, every)
        wend_f.append(times_b(column(desc0, e_desc, fwd), fwd))
        wend_b.append(times_b(column(asc0, e_asc, bwd), bwd))
        lag_b.append(times_b(rows_of(asc1_q, bwd), bwd))
        wout_f.append(times_c(rows_of(asc1_q, fwd), fwd))
        wout_b.append(times_c(column(desc1_0, e_desc, bwd), bwd))
        cols = slice(q * LANES, (q + 1) * LANES)
        wend_ref[gi, fwd, cols] = wend_f[q].astype(BF16)
        wend_ref[gi, bwd, cols] = wend_b[q].astype(BF16)

    meta_cols = N_META * SSM_GROUP // LANES
    x0 = sum(jnp.sum(wend_f[n_cols - meta_cols + m] * um_ref[gi, :, m * LANES:(m + 1) * LANES],
                     axis=1, keepdims=True) for m in range(meta_cols))

    lane = lax.broadcasted_iota(jnp.int32, (DIR_ROWS, LANES), 1)
    lag0_b = jnp.where(lane >= LANES - SSM_GROUP, bb_same[bwd], 0.0)
    zero = jnp.zeros((DIR_ROWS, LANES), F32)
    bcat = jnp.concatenate(
        [jnp.concatenate(wend_f + [zero] * n_cols, axis=1),
         jnp.concatenate([zero] * (n_cols - 1) + [lag0_b] + lag_b, axis=1)], axis=0)
    lane_r = lax.broadcasted_iota(jnp.int32, (SSM_GROUP, rows), 1)
    c2 = jnp.where((lane_r // SSM_STATE) % 2 == 0, cr_ref[gi], -cr_ref[gi])
    kk = jnp.dot(c2, bcat, precision=lax.Precision.HIGHEST,
                 preferred_element_type=F32)
    for tt in range(t):
        off = (t - 1 - tt) * SSM_GROUP
        toep_ref[gi, tt * SSM_GROUP:(tt + 1) * SSM_GROUP, :] = (
            kk[:, off:off + CHUNK_LANES].astype(BF16))

    wout = jnp.concatenate([jnp.concatenate(wout_f, axis=1),
                            jnp.concatenate(wout_b, axis=1)], axis=0)
    wout_ref[gi] = wout.T.astype(BF16)
    return pows[t], x0


SSM_GROUPS_PER_STEP = 2


def _chunk_scan(er, ei, ar, ai, x0, chunk, n_chunks, forward):
    width = er.shape[1]
    xr, xi = er, ei
    if x0 is not None:
        fr, fi = _cmul(ar, ai, x0[0], x0[1])
        xr = xr + jnp.where(chunk == 0, _lanes(fr, width), 0.0)
        xi = xi + jnp.where(chunk == 0, _lanes(fi, width), 0.0)

    def shifted(v, sh):
        if forward:
            return jnp.where(chunk >= sh, pltpu.roll(v, sh, axis=1), 0.0)
        return jnp.where(chunk < n_chunks - sh, pltpu.roll(v, width - sh, axis=1), 0.0)

    sh = 1
    while sh < n_chunks:
        sr, si = shifted(xr, sh), shifted(xi, sh)
        wr, wi = _lanes(ar, width), _lanes(ai, width)
        xr, xi = xr + wr * sr - wi * si, xi + wr * si + wi * sr
        ar, ai = _cmul(ar, ai, ar, ai)
        sh *= 2
    inr, ini = shifted(xr, 1), shifted(xi, 1)
    if x0 is not None:
        inr = jnp.where(chunk == 0, x0[0], inr)
        ini = jnp.where(chunk == 0, x0[1], ini)
    return inr, ini


def _ssm_kernel(u_ref, um_ref, csc_ref, cc_ref, bc_ref, cr_ref, y_ref, toep_ref, wend_ref,
                wout_ref, *, n_chunks):
    for gi in range(SSM_GROUPS_PER_STEP):
        (lam_r, lam_i), x0 = _build_chunk_ops(gi, csc_ref, cc_ref, bc_ref, cr_ref, um_ref,
                                              toep_ref, wend_ref, wout_ref)
        u = u_ref[gi]
        width = u.shape[1]
        e = jnp.dot(wend_ref[gi], u, preferred_element_type=F32)
        chunk = lax.broadcasted_iota(jnp.int32, (1, width), 1) % n_chunks
        p = SSM_STATE
        ar, ai = lam_r, lam_i
        f_in = _chunk_scan(e[0:p], e[p:2 * p], ar[0:p], ai[0:p], (x0[0:p], x0[p:2 * p]),
                           chunk, n_chunks, True)
        b_in = _chunk_scan(e[2 * p:3 * p], e[3 * p:], ar[2 * p:3 * p], ai[2 * p:3 * p],
                           None, chunk, n_chunks, False)
        xin = jnp.concatenate([f_in[0], f_in[1], b_in[0], b_in[1]], axis=0).astype(BF16)
        y = jnp.dot(toep_ref[gi], u, preferred_element_type=F32)
        y = y + jnp.dot(wout_ref[gi], xin, preferred_element_type=F32)
        y_ref[gi] = y.astype(BF16)


def _ssm(u_col, u_meta, csc, c_col, b_col, c_row, n_chunks):
    g, _, width = u_col.shape
    per = SSM_GROUPS_PER_STEP
    return pl.pallas_call(
        functools.partial(_ssm_kernel, n_chunks=n_chunks),
        grid=(g // per,),
        in_specs=[
            pl.BlockSpec((per, CHUNK_LANES, width), lambda i: (i, 0, 0)),
            pl.BlockSpec((per, 1, N_META * SSM_GROUP), lambda i: (i, 0, 0)),
            pl.BlockSpec((per, STATE_ROWS, 4), lambda i: (i, 0, 0)),
            pl.BlockSpec((per, STATE_ROWS, SSM_GROUP), lambda i: (i, 0, 0)),
            pl.BlockSpec((per, STATE_ROWS, SSM_GROUP), lambda i: (i, 0, 0)),
            pl.BlockSpec((per, SSM_GROUP, STATE_ROWS), lambda i: (i, 0, 0)),
        ],
        out_specs=pl.BlockSpec((per, CHUNK_LANES, width), lambda i: (i, 0, 0)),
        out_shape=jax.ShapeDtypeStruct((g, CHUNK_LANES, width), BF16),
        scratch_shapes=[
            pltpu.VMEM((per, CHUNK_LANES, CHUNK_LANES), BF16),
            pltpu.VMEM((per, STATE_ROWS, CHUNK_LANES), BF16),
            pltpu.VMEM((per, CHUNK_LANES, STATE_ROWS), BF16),
        ],
        compiler_params=_params(),
        name="s5_mixer",
    )(u_col, u_meta, csc, c_col, b_col, c_row)


FF_CHUNK = 1024
MIX_STEPS = 4


def _mix_ffn_kernel(x_hbm, co_ref, y_ref, ut_ref, d_ref, gwt_ref, gb_ref, wo_ref,
                    gf_ref, w1_ref, w2_ref, gl_ref, o_hbm, xbuf, obuf, sem_in, sem_out,
                    h_ref, z_ref, a_ref):
    n_chunks = x_hbm.shape[1]
    n_parts = SUBLANES // MIX_STEPS
    tile = pl.program_id(0)
    n_tiles = pl.num_programs(0)
    slot = _fetch_steps(x_hbm, xbuf, sem_in)

    @pl.when(tile >= 2)
    def _():
        for cp in _step_copies(o_hbm, obuf, sem_out, tile - 2, slot, True):
            cp.wait()

    def steps_of(part):
        return range(part * MIX_STEPS, (part + 1) * MIX_STEPS)

    def mix(part):
        def col(ref):
            return jnp.concatenate(
                [ref[:, r * SSM_GROUP:(r + 1) * SSM_GROUP, :].reshape(D_SSM, n_chunks)
                 for r in steps_of(part)], axis=1).astype(F32)

        yt = col(y_ref) + d_ref[...] * col(ut_ref)
        ge = jax.nn.gelu(yt)
        gate = (jnp.dot(gwt_ref[...], ge.astype(BF16), preferred_element_type=F32)
                + gb_ref[...])
        s5 = (ge * jax.nn.sigmoid(gate)).T.astype(BF16)
        co = co_ref[part * MIX_STEPS:(part + 1) * MIX_STEPS].reshape(
            MIX_STEPS * n_chunks, D_CONV)
        x = xbuf[slot, part * MIX_STEPS:(part + 1) * MIX_STEPS].reshape(
            MIX_STEPS * n_chunks, D_MODEL)
        h = x + jnp.dot(jnp.concatenate([co, s5], axis=1), wo_ref[...],
                        preferred_element_type=F32)
        h_ref[part] = h
        z_ref[part] = _rms(h, gf_ref[...]).astype(BF16)

    def ffn_up(part, j):
        cols = slice(j * FF_CHUNK, (j + 1) * FF_CHUNK)
        a = jnp.dot(z_ref[part], w1_ref[:, cols], preferred_element_type=F32)
        a_ref[:, cols] = jnp.square(jnp.maximum(a, 0.0)).astype(BF16)

    def ffn_down(part):
        h = h_ref[part] + jnp.dot(a_ref[...], w2_ref[...], preferred_element_type=F32)
        out = _rms(h, gl_ref[...])
        obuf[slot, part * MIX_STEPS:(part + 1) * MIX_STEPS] = out.reshape(
            MIX_STEPS, n_chunks, D_MODEL)

    for part in range(n_parts):
        mix(part)
        for j in range(D_FF // FF_CHUNK):
            ffn_up(part, j)
        ffn_down(part)

    for cp in _step_copies(o_hbm, obuf, sem_out, tile, slot, True):
        cp.start()

    @pl.when(tile == n_tiles - 1)
    def _():
        for cp in (_step_copies(o_hbm, obuf, sem_out, tile - 1, 1 - slot, True)
                   + _step_copies(o_hbm, obuf, sem_out, tile, slot, True)):
            cp.wait()


def _mix_ffn(x4, conv_out, y_col, u_col, d_col, glu_wt, glu_b_col, w_out, g_ffn, w1, w2,
             g_final):
    bsz, n_chunks = x4.shape[:2]

    def const(shape):
        return pl.BlockSpec(shape, lambda t: (0, 0), pipeline_mode=pl.Buffered(1))

    hbm = pl.BlockSpec(memory_space=pl.ANY)
    col_blk = pl.BlockSpec((N_GROUPS, SUBLANES * SSM_GROUP, n_chunks),
                           lambda t: (0, t // bsz, t % bsz))
    step_buf = pltpu.VMEM((2, SUBLANES, n_chunks, D_MODEL), F32)
    step_sems = pltpu.SemaphoreType.DMA((2, SUBLANES))
    return pl.pallas_call(
        _mix_ffn_kernel,
        grid=(CHUNK // SUBLANES * bsz,),
        in_specs=[
            hbm,
            pl.BlockSpec((SUBLANES, n_chunks, D_CONV), lambda t: (t // bsz, t % bsz, 0)),
            col_blk,
            col_blk,
            const((D_SSM, 1)),
            const((D_SSM, D_SSM)),
            const((D_SSM, 1)),
            const((D_MODEL, D_MODEL)),
            const((1, D_MODEL)),
            const((D_MODEL, D_FF)),
            const((D_FF, D_MODEL)),
            const((1, D_MODEL)),
        ],
        out_specs=hbm,
        out_shape=jax.ShapeDtypeStruct(x4.shape, F32),
        scratch_shapes=[
            step_buf, step_buf, step_sems, step_sems,
            pltpu.VMEM((SUBLANES // MIX_STEPS, MIX_STEPS * n_chunks, D_MODEL), F32),
            pltpu.VMEM((SUBLANES // MIX_STEPS, MIX_STEPS * n_chunks, D_MODEL), BF16),
            pltpu.VMEM((MIX_STEPS * n_chunks, D_FF), BF16),
        ],
        compiler_params=_step_tile_params(),
        name="mix_ffn",
    )(x4, conv_out, y_col, u_col, d_col, glu_wt, glu_b_col, w_out, g_ffn, w1, w2, g_final)


def _s5_operator_inputs(lbr, lbi, zr, zi, b_re, b_im, c_re, c_im):
    g, p, h = N_GROUPS, SSM_STATE, SSM_GROUP

    sc = jnp.transpose(jnp.stack([lbr, lbi, zr, zi], axis=-1), (1, 0, 2, 3))
    csc = jnp.broadcast_to(sc[:, :, None], (g, 2, 2, p, 4)).reshape(g, 4 * p, 4)
    c = jnp.stack([c_re, c_im], axis=0)
    b = jnp.stack([b_re, b_im], axis=0)
    c_col = jnp.transpose(c, (2, 1, 0, 4, 3)).reshape(g, 4 * p, h)
    b_col = jnp.transpose(b, (2, 1, 0, 3, 4)).reshape(g, 4 * p, h)
    c_row = jnp.transpose(c, (2, 3, 1, 0, 4)).reshape(g, h, 4 * p)
    return csc, c_col, b_col, c_row


def kernel(x, meta_tokens, norm_mix_g, w_in, conv_w, conv_b, conv_ln_g, conv_ln_b,
           ssm_lam_re, ssm_lam_im, ssm_log_dt, ssm_b_re, ssm_b_im, ssm_c_re, ssm_c_im,
           ssm_d, ssm_glu_w, ssm_glu_b, w_out, norm_ffn_g, w_ff1, w_ff2, norm_final_g):
    assert w_in.shape[0] == 1, "single-layer block"
    bsz, seq, _ = x.shape
    assert seq % CHUNK == 0 and CHUNK >= N_META
    n_chunks = seq // CHUNK
    g, h = N_GROUPS, SSM_GROUP

    x4 = x.reshape(bsz, n_chunks, CHUNK, D_MODEL)
    g_mix = norm_mix_g[0][None, :]
    w_in_b = w_in[0].astype(BF16)
    u_conv, u_col = _in_proj_step(x4, g_mix, w_in_b)
    u_conv_m, u_ssm_m = _in_proj_meta(meta_tokens, g_mix, w_in_b)

    conv_out = _conv_module(u_conv, u_conv_m, conv_w[0], conv_b[0][None, :],
                            conv_ln_g[0][None, :], conv_ln_b[0][None, :], bsz)

    ldt = jnp.broadcast_to(ssm_log_dt[0][..., None], ssm_lam_re[0].shape)
    flat = lambda a: a.reshape(2 * g, SSM_STATE)
    lbr, lbi, zr, zi = [a.reshape(2, g, SSM_STATE) for a in
                        _zoh(flat(ssm_lam_re[0]), flat(ssm_lam_im[0]), flat(ldt))]
    csc, c_col, b_col, c_row = _s5_operator_inputs(
        lbr, lbi, zr, zi, ssm_b_re[0], ssm_b_im[0], ssm_c_re[0], ssm_c_im[0])

    u_meta = jnp.transpose(u_ssm_m.reshape(N_META, g, h), (1, 0, 2)).reshape(g, 1, N_META * h)
    y_col = _ssm(u_col, u_meta, csc, c_col, b_col, c_row, n_chunks)

    out = _mix_ffn(x4, conv_out, y_col, u_col, ssm_d[0][:, None],
                   ssm_glu_w[0].T.astype(BF16), ssm_glu_b[0][:, None],
                   w_out[0].astype(BF16), norm_ffn_g[0][None, :], w_ff1[0].astype(BF16),
                   w_ff2[0].astype(BF16), norm_final_g[None, :])
    return out.reshape(bsz, seq, D_MODEL)
```

```python
import functools

import jax
import jax.numpy as jnp
from jax import lax
from jax.experimental import pallas as pl
from jax.experimental.pallas import tpu as pltpu

F32 = jnp.float32
BF16 = jnp.bfloat16

D_MODEL = 1024
N_META = 16
D_CONV = 512
D_SSM = 512
CONV_WIDTH = 31
CONV_PAD = CONV_WIDTH // 2
SSM_GROUP = 16
N_GROUPS = D_SSM // SSM_GROUP
SSM_STATE = 64
D_FF = 4096
NORM_EPS = 1e-5
LANES = 128
SUBLANES = 8

CHUNK = 32
CHUNK_LANES = CHUNK * SSM_GROUP
STATE_ROWS = 4 * SSM_STATE
DIR_ROWS = 2 * SSM_STATE
STEPS_PER_VREG = LANES // SSM_GROUP

VMEM_LIMIT_BYTES = 60 * 1024 * 1024


def _rms(x, g):
    return x * lax.rsqrt(jnp.mean(x * x, axis=-1, keepdims=True) + NORM_EPS) * g


def _params(n_axes=1):
    return pltpu.CompilerParams(dimension_semantics=("parallel",) * n_axes,
                                vmem_limit_bytes=VMEM_LIMIT_BYTES)


def _step_copies(hbm_ref, buf_ref, sem_ref, tile, slot, to_hbm):
    bsz = hbm_ref.shape[0]
    slab, b = tile // bsz, tile % bsz
    copies = []
    for r in range(SUBLANES):
        hbm = hbm_ref.at[b, :, slab * SUBLANES + r, :]
        vmem = buf_ref.at[slot, r]
        src, dst = (vmem, hbm) if to_hbm else (hbm, vmem)
        copies.append(pltpu.make_async_copy(src, dst, sem_ref.at[slot, r]))
    return copies


def _fetch_steps(x_hbm, xbuf, sem):
    tile = pl.program_id(0)
    slot = tile % 2

    @pl.when(tile == 0)
    def _():
        for cp in _step_copies(x_hbm, xbuf, sem, tile, slot, False):
            cp.start()

    @pl.when(tile + 1 < pl.num_programs(0))
    def _():
        for cp in _step_copies(x_hbm, xbuf, sem, tile + 1, 1 - slot, False):
            cp.start()

    for cp in _step_copies(x_hbm, xbuf, sem, tile, slot, False):
        cp.wait()
    return slot


def _step_tile_params():
    return pltpu.CompilerParams(dimension_semantics=("arbitrary",),
                                vmem_limit_bytes=VMEM_LIMIT_BYTES)


def _in_proj_step_kernel(x_hbm, g_ref, w_ref, uc_ref, ut_ref, xbuf, sem):
    n_chunks = x_hbm.shape[1]
    slot = _fetch_steps(x_hbm, xbuf, sem)
    x = xbuf[slot].reshape(SUBLANES * n_chunks, D_MODEL)
    z = _rms(x, g_ref[...]).astype(BF16)
    p = jnp.dot(z, w_ref[...], preferred_element_type=F32)
    uc = p[:, :D_CONV] * jax.nn.sigmoid(p[:, D_CONV:2 * D_CONV])
    uc_ref[...] = uc.reshape(SUBLANES, n_chunks, D_CONV)
    ut = p[:, 2 * D_CONV:].T.astype(BF16)
    for r in range(SUBLANES):
        ut_ref[:, r * SSM_GROUP:(r + 1) * SSM_GROUP, :] = (
            ut[:, r * n_chunks:(r + 1) * n_chunks].reshape(N_GROUPS, SSM_GROUP, n_chunks))


def _in_proj_step(x4, g, w_in):
    bsz, n_chunks = x4.shape[:2]
    rows = bsz * n_chunks
    return pl.pallas_call(
        _in_proj_step_kernel,
        grid=(CHUNK // SUBLANES * bsz,),
        in_specs=[
            pl.BlockSpec(memory_space=pl.ANY),
            pl.BlockSpec((1, D_MODEL), lambda t: (0, 0)),
            pl.BlockSpec((D_MODEL, 2 * D_CONV + D_SSM), lambda t: (0, 0)),
        ],
        out_specs=[
            pl.BlockSpec((SUBLANES, n_chunks, D_CONV), lambda t: (t // bsz, t % bsz, 0)),
            pl.BlockSpec((N_GROUPS, SUBLANES * SSM_GROUP, n_chunks),
                         lambda t: (0, t // bsz, t % bsz)),
        ],
        out_shape=[
            jax.ShapeDtypeStruct((CHUNK, rows, D_CONV), F32),
            jax.ShapeDtypeStruct((N_GROUPS, CHUNK_LANES, rows), BF16),
        ],
        scratch_shapes=[pltpu.VMEM((2, SUBLANES, n_chunks, D_MODEL), F32),
                        pltpu.SemaphoreType.DMA((2, SUBLANES))],
        compiler_params=_step_tile_params(),
        name="in_proj",
    )(x4, g, w_in)


def _in_proj_meta_kernel(x_ref, g_ref, w_ref, uc_ref, us_ref):
    z = _rms(x_ref[...], g_ref[...]).astype(BF16)
    p = jnp.dot(z, w_ref[...], preferred_element_type=F32)
    uc_ref[...] = p[:, :D_CONV] * jax.nn.sigmoid(p[:, D_CONV:2 * D_CONV])
    us_ref[...] = p[:, 2 * D_CONV:]


def _in_proj_meta(meta, g, w_in):
    return pl.pallas_call(
        _in_proj_meta_kernel,
        out_shape=[jax.ShapeDtypeStruct((N_META, D_CONV), F32),
                   jax.ShapeDtypeStruct((N_META, D_SSM), F32)],
        name="in_proj_meta",
    )(meta, g, w_in)


CONV_GROUP = 4
CONV_ROWS = 16


def _conv_kernel(u_ref, um_ref, w_ref, cb_ref, lg_ref, lb_ref, o_ref, cat_ref, acc_ref):
    n_steps, n_chunks, _ = u_ref.shape
    cat_ref[CONV_PAD:CONV_PAD + n_steps] = u_ref[...]
    chunk = lax.broadcasted_iota(jnp.int32, (n_chunks, 1), 0)
    for i in range(CONV_PAD):
        s_prev = n_steps - CONV_PAD + i
        prev = pltpu.roll(u_ref[s_prev], 1, axis=0)
        meta_row = um_ref[N_META - CONV_PAD + i:N_META - CONV_PAD + i + 1, :]
        cat_ref[i] = jnp.where(chunk == 0, meta_row, prev)
        nxt = pltpu.roll(u_ref[i], n_chunks - 1, axis=0)
        cat_ref[n_steps + CONV_PAD + i] = jnp.where(chunk == n_chunks - 1, 0.0, nxt)

    def body(sg, carry):
        s0 = sg * CONV_GROUP
        for c in range(D_CONV // LANES):
            cols = slice(c * LANES, (c + 1) * LANES)
            taps = [jnp.broadcast_to(w_ref[k:k + 1, cols], (SUBLANES, LANES))
                    for k in range(CONV_WIDTH)]

            def rows_body(rt, carry2):
                for sub in range(CONV_ROWS // SUBLANES):
                    r0 = pl.multiple_of(rt * CONV_ROWS + sub * SUBLANES, SUBLANES)
                    accs = [jnp.zeros((SUBLANES, LANES), F32)] * CONV_GROUP
                    for i in range(CONV_WIDTH + CONV_GROUP - 1):
                        d = cat_ref[s0 + i, pl.ds(r0, SUBLANES), cols]
                        for j in range(CONV_GROUP):
                            if 0 <= i - j < CONV_WIDTH:
                                accs[j] = accs[j] + d * taps[i - j]
                    for j in range(CONV_GROUP):
                        acc_ref[j, pl.ds(r0, SUBLANES), cols] = accs[j]
                return carry2

            lax.fori_loop(0, n_chunks // CONV_ROWS, rows_body, 0)
        for j in range(CONV_GROUP):
            y = acc_ref[j] + cb_ref[...]
            yc = y - jnp.mean(y, axis=-1, keepdims=True)
            yn = yc * lax.rsqrt(jnp.mean(yc * yc, axis=-1, keepdims=True) + NORM_EPS)
            yn = yn * lg_ref[...] + lb_ref[...]
            o_ref[s0 + j] = (yn * jax.nn.sigmoid(yn)).astype(BF16)
        return carry

    lax.fori_loop(0, n_steps // CONV_GROUP, body, 0)


def _conv_module(u_conv, u_conv_meta, conv_w, conv_b, ln_g, ln_b, bsz):
    n_steps, rows, _ = u_conv.shape
    n_chunks = rows // bsz
    vec = pl.BlockSpec((1, D_CONV), lambda b: (0, 0))
    blk = pl.BlockSpec((n_steps, n_chunks, D_CONV), lambda b: (0, b, 0))
    return pl.pallas_call(
        _conv_kernel,
        grid=(bsz,),
        in_specs=[
            blk,
            pl.BlockSpec((N_META, D_CONV), lambda b: (0, 0)),
            pl.BlockSpec((CONV_WIDTH, D_CONV), lambda b: (0, 0)),
            vec, vec, vec,
        ],
        out_specs=blk,
        out_shape=jax.ShapeDtypeStruct((n_steps, rows, D_CONV), BF16),
        scratch_shapes=[pltpu.VMEM((n_steps + 2 * CONV_PAD, n_chunks, D_CONV), F32),
                        pltpu.VMEM((CONV_GROUP, n_chunks, D_CONV), F32)],
        compiler_params=_params(),
        name="conv_module",
    )(u_conv, u_conv_meta, conv_w, conv_b, ln_g, ln_b)


def _zoh_kernel(lre_ref, lim_ref, ldt_ref, lbr_ref, lbi_ref, zr_ref, zi_ref):
    lre = lre_ref[...]
    lim = lim_ref[...]
    dt = jnp.exp(ldt_ref[...])
    ea = jnp.exp(lre * dt)
    lbr = ea * jnp.cos(lim * dt)
    lbi = ea * jnp.sin(lim * dt)
    nr = lbr - 1.0
    den = lre * lre + lim * lim
    lbr_ref[...] = lbr
    lbi_ref[...] = lbi
    zr_ref[...] = (nr * lre + lbi * lim) / den
    zi_ref[...] = (lbi * lre - nr * lim) / den


def _zoh(lam_re, lam_im, log_dt):
    shape = jax.ShapeDtypeStruct(lam_re.shape, F32)
    return pl.pallas_call(_zoh_kernel, out_shape=[shape] * 4, name="s5_zoh")(
        lam_re, lam_im, log_dt)


def _lanes(x, n):
    return jnp.concatenate([x] * (n // LANES), axis=1)


def _cmul(ar, ai, br, bi):
    return ar * br - ai * bi, ar * bi + ai * br


def _cpow(br, bi, expo, nbits):
    rr = jnp.ones(expo.shape, F32)
    ri = jnp.zeros(expo.shape, F32)
    for k in range(nbits):
        bit = ((expo >> k) & 1) == 1
        nr, ni = _cmul(rr, ri, br, bi)
        rr = jnp.where(bit, nr, rr)
        ri = jnp.where(bit, ni, ri)
        br, bi = _cmul(br, bi, br, bi)
    return rr, ri


def _build_chunk_ops(gi, csc_ref, cc_ref, bc_ref, cr_ref, um_ref, toep_ref, wend_ref,
                     wout_ref):
    t = CHUNK
    rows = STATE_ROWS
    n_cols = CHUNK_LANES // LANES
    lam = (jnp.broadcast_to(csc_ref[gi, :, 0:1], (rows, LANES)),
           jnp.broadcast_to(csc_ref[gi, :, 1:2], (rows, LANES)))
    zr = jnp.broadcast_to(csc_ref[gi, :, 2:3], (rows, LANES))
    zi = jnp.broadcast_to(csc_ref[gi, :, 3:4], (rows, LANES))
    sel = (lax.broadcasted_iota(jnp.int32, (SSM_GROUP, LANES), 1) % SSM_GROUP
           == lax.broadcasted_iota(jnp.int32, (SSM_GROUP, LANES), 0)).astype(F32)

    def tile(ref):
        return jnp.dot(ref[gi], sel, precision=lax.Precision.HIGHEST,
                       preferred_element_type=F32)

    def swap_re_im(a):
        p = SSM_STATE
        return jnp.concatenate([a[p:2 * p], a[0:p], a[3 * p:], a[2 * p:3 * p]], axis=0)

    c_same, b_same = tile(cc_ref), tile(bc_ref)
    c_swap, b_swap = swap_re_im(c_same), swap_re_im(b_same)
    row = lax.broadcasted_iota(jnp.int32, (rows, LANES), 0)
    is_re = (row // SSM_STATE) % 2 == 0
    sgn = jnp.where(is_re, -1.0, 1.0)
    ca = jnp.where(is_re, c_same, -c_same)
    cb = -c_swap
    bb_same = zr * b_same + sgn * zi * b_swap
    sbb_swap = sgn * (zr * b_swap - sgn * zi * b_same)

    pows = {1: lam}
    k = 1
    while k < t:
        pows[2 * k] = _cmul(*pows[k], *pows[k])
        k *= 2
    step_bits = STEPS_PER_VREG.bit_length() - 1
    fwd = slice(0, DIR_ROWS)
    bwd = slice(DIR_ROWS, rows)
    every = slice(0, rows)

    def rows_of(v, rs):
        return v[0][rs], v[1][rs]

    def column(base, e, rs):
        out = rows_of(base, rs)
        for bit, val in pows.items():
            if e & bit:
                out = _cmul(*out, *rows_of(val, rs))
        return out

    def times_b(p, rs):
        return p[0] * bb_same[rs] + p[1] * sbb_swap[rs]

    def times_c(p, rs):
        return ca[rs] * p[0] + cb[rs] * p[1]

    i8 = lax.broadcasted_iota(jnp.int32, (rows, LANES), 1) // SSM_GROUP
    asc0 = _cpow(*lam, i8, step_bits)
    desc0 = _cpow(*lam, STEPS_PER_VREG - 1 - i8, step_bits)
    asc1_0 = _cmul(*asc0, *lam)
    desc1_0 = _cmul(*desc0, *lam)

    wend_f, wend_b, lag_b, wout_f, wout_b = [], [], [], [], []
    for q in range(n_cols):
        e_asc = STEPS_PER_VREG * q
        e_desc = STEPS_PER_VREG * (n_cols - 1 - q)
        asc1_q = column(asc1_0, e_asc, every)
        wend_f.append(times_b(column(desc0, e_desc, fwd), fwd))
        wend_b.append(times_b(column(asc0, e_asc, bwd), bwd))
        lag_b.append(times_b(rows_of(asc1_q, bwd), bwd))
        wout_f.append(times_c(rows_of(asc1_q, fwd), fwd))
        wout_b.append(times_c(column(desc1_0, e_desc, bwd), bwd))
        cols = slice(q * LANES, (q + 1) * LANES)
        wend_ref[gi, fwd, cols] = wend_f[q].astype(BF16)
        wend_ref[gi, bwd, cols] = wend_b[q].astype(BF16)

    meta_cols = N_META * SSM_GROUP // LANES
    x0 = sum(jnp.sum(wend_f[n_cols - meta_cols + m] * um_ref[gi, :, m * LANES:(m + 1) * LANES],
                     axis=1, keepdims=True) for m in range(meta_cols))

    lane = lax.broadcasted_iota(jnp.int32, (DIR_ROWS, LANES), 1)
    lag0_b = jnp.where(lane >= LANES - SSM_GROUP, bb_same[bwd], 0.0)
    zero = jnp.zeros((DIR_ROWS, LANES), F32)
    bcat = jnp.concatenate(
        [jnp.concatenate(wend_f + [zero] * n_cols, axis=1),
         jnp.concatenate([zero] * (n_cols - 1) + [lag0_b] + lag_b, axis=1)], axis=0)
    lane_r = lax.broadcasted_iota(jnp.int32, (SSM_GROUP, rows), 1)
    c2 = jnp.where((lane_r // SSM_STATE) % 2 == 0, cr_ref[gi], -cr_ref[gi])
    kk = jnp.dot(c2, bcat, precision=lax.Precision.HIGHEST,
                 preferred_element_type=F32)
    for tt in range(t):
        off = (t - 1 - tt) * SSM_GROUP
        toep_ref[gi, tt * SSM_GROUP:(tt + 1) * SSM_GROUP, :] = (
            kk[:, off:off + CHUNK_LANES].astype(BF16))

    wout = jnp.concatenate([jnp.concatenate(wout_f, axis=1),
                            jnp.concatenate(wout_b, axis=1)], axis=0)
    wout_ref[gi] = wout.T.astype(BF16)
    return pows[t], x0


SSM_GROUPS_PER_STEP = 2


def _chunk_scan(er, ei, ar, ai, x0, chunk, n_chunks, forward):
    width = er.shape[1]
    xr, xi = er, ei
    if x0 is not None:
        fr, fi = _cmul(ar, ai, x0[0], x0[1])
        xr = xr + jnp.where(chunk == 0, _lanes(fr, width), 0.0)
        xi = xi + jnp.where(chunk == 0, _lanes(fi, width), 0.0)

    def shifted(v, sh):
        if forward:
            return jnp.where(chunk >= sh, pltpu.roll(v, sh, axis=1), 0.0)
        return jnp.where(chunk < n_chunks - sh, pltpu.roll(v, width - sh, axis=1), 0.0)

    sh = 1
    while sh < n_chunks:
        sr, si = shifted(xr, sh), shifted(xi, sh)
        wr, wi = _lanes(ar, width), _lanes(ai, width)
        xr, xi = xr + wr * sr - wi * si, xi + wr * si + wi * sr
        ar, ai = _cmul(ar, ai, ar, ai)
        sh *= 2
    inr, ini = shifted(xr, 1), shifted(xi, 1)
    if x0 is not None:
        inr = jnp.where(chunk == 0, x0[0], inr)
        ini = jnp.where(chunk == 0, x0[1], ini)
    return inr, ini


def _ssm_kernel(u_ref, um_ref, csc_ref, cc_ref, bc_ref, cr_ref, y_ref, toep_ref, wend_ref,
                wout_ref, *, n_chunks):
    for gi in range(SSM_GROUPS_PER_STEP):
        (lam_r, lam_i), x0 = _build_chunk_ops(gi, csc_ref, cc_ref, bc_ref, cr_ref, um_ref,
                                              toep_ref, wend_ref, wout_ref)
        u = u_ref[gi]
        width = u.shape[1]
        e = jnp.dot(wend_ref[gi], u, preferred_element_type=F32)
        chunk = lax.broadcasted_iota(jnp.int32, (1, width), 1) % n_chunks
        p = SSM_STATE
        ar, ai = lam_r, lam_i
        f_in = _chunk_scan(e[0:p], e[p:2 * p], ar[0:p], ai[0:p], (x0[0:p], x0[p:2 * p]),
                           chunk, n_chunks, True)
        b_in = _chunk_scan(e[2 * p:3 * p], e[3 * p:], ar[2 * p:3 * p], ai[2 * p:3 * p],
                           None, chunk, n_chunks, False)
        xin = jnp.concatenate([f_in[0], f_in[1], b_in[0], b_in[1]], axis=0).astype(BF16)
        y = jnp.dot(toep_ref[gi], u, preferred_element_type=F32)
        y = y + jnp.dot(wout_ref[gi], xin, preferred_element_type=F32)
        y_ref[gi] = y.astype(BF16)


def _ssm(u_col, u_meta, csc, c_col, b_col, c_row, n_chunks):
    g, _, width = u_col.shape
    per = SSM_GROUPS_PER_STEP
    return pl.pallas_call(
        functools.partial(_ssm_kernel, n_chunks=n_chunks),
        grid=(g // per,),
        in_specs=[
            pl.BlockSpec((per, CHUNK_LANES, width), lambda i: (i, 0, 0)),
            pl.BlockSpec((per, 1, N_META * SSM_GROUP), lambda i: (i, 0, 0)),
            pl.BlockSpec((per, STATE_ROWS, 4), lambda i: (i, 0, 0)),
            pl.BlockSpec((per, STATE_ROWS, SSM_GROUP), lambda i: (i, 0, 0)),
            pl.BlockSpec((per, STATE_ROWS, SSM_GROUP), lambda i: (i, 0, 0)),
            pl.BlockSpec((per, SSM_GROUP, STATE_ROWS), lambda i: (i, 0, 0)),
        ],
        out_specs=pl.BlockSpec((per, CHUNK_LANES, width), lambda i: (i, 0, 0)),
        out_shape=jax.ShapeDtypeStruct((g, CHUNK_LANES, width), BF16),
        scratch_shapes=[
            pltpu.VMEM((per, CHUNK_LANES, CHUNK_LANES), BF16),
            pltpu.VMEM((per, STATE_ROWS, CHUNK_LANES), BF16),
            pltpu.VMEM((per, CHUNK_LANES, STATE_ROWS), BF16),
        ],
        compiler_params=_params(),
        name="s5_mixer",
    )(u_col, u_meta, csc, c_col, b_col, c_row)


FF_CHUNK = 1024
MIX_STEPS = 2


def _mix_ffn_kernel(x_hbm, co_ref, y_ref, ut_ref, d_ref, gwt_ref, gb_ref, wo_ref,
                    gf_ref, w1_ref, w2_ref, gl_ref, o_hbm, xbuf, obuf, sem_in, sem_out,
                    h_ref, z_ref, a_ref):
    n_chunks = x_hbm.shape[1]
    n_parts = SUBLANES // MIX_STEPS
    tile = pl.program_id(0)
    n_tiles = pl.num_programs(0)
    slot = _fetch_steps(x_hbm, xbuf, sem_in)

    @pl.when(tile >= 2)
    def _():
        for cp in _step_copies(o_hbm, obuf, sem_out, tile - 2, slot, True):
            cp.wait()

    def steps_of(part):
        return range(part * MIX_STEPS, (part + 1) * MIX_STEPS)

    def mix(part):
        def col(ref):
            return jnp.concatenate(
                [ref[:, r * SSM_GROUP:(r + 1) * SSM_GROUP, :].reshape(D_SSM, n_chunks)
                 for r in steps_of(part)], axis=1).astype(F32)

        yt = col(y_ref) + d_ref[...] * col(ut_ref)
        ge = jax.nn.gelu(yt)
        gate = (jnp.dot(gwt_ref[...], ge.astype(BF16), preferred_element_type=F32)
                + gb_ref[...])
        s5 = (ge * jax.nn.sigmoid(gate)).T.astype(BF16)
        co = co_ref[part * MIX_STEPS:(part + 1) * MIX_STEPS].reshape(
            MIX_STEPS * n_chunks, D_CONV)
        x = xbuf[slot, part * MIX_STEPS:(part + 1) * MIX_STEPS].reshape(
            MIX_STEPS * n_chunks, D_MODEL)
        h = x + jnp.dot(jnp.concatenate([co, s5], axis=1), wo_ref[...],
                        preferred_element_type=F32)
        h_ref[part] = h
        z_ref[part] = _rms(h, gf_ref[...]).astype(BF16)

    def ffn_up(part, j):
        cols = slice(j * FF_CHUNK, (j + 1) * FF_CHUNK)
        a = jnp.dot(z_ref[part], w1_ref[:, cols], preferred_element_type=F32)
        a_ref[:, cols] = jnp.square(jnp.maximum(a, 0.0)).astype(BF16)

    def ffn_down(part):
        h = h_ref[part] + jnp.dot(a_ref[...], w2_ref[...], preferred_element_type=F32)
        out = _rms(h, gl_ref[...])
        obuf[slot, part * MIX_STEPS:(part + 1) * MIX_STEPS] = out.reshape(
            MIX_STEPS, n_chunks, D_MODEL)

    for part in range(n_parts):
        mix(part)
        for j in range(D_FF // FF_CHUNK):
            ffn_up(part, j)
        ffn_down(part)

    for cp in _step_copies(o_hbm, obuf, sem_out, tile, slot, True):
        cp.start()

    @pl.when(tile == n_tiles - 1)
    def _():
        for cp in (_step_copies(o_hbm, obuf, sem_out, tile - 1, 1 - slot, True)
                   + _step_copies(o_hbm, obuf, sem_out, tile, slot, True)):
            cp.wait()


def _mix_ffn(x4, conv_out, y_col, u_col, d_col, glu_wt, glu_b_col, w_out, g_ffn, w1, w2,
             g_final):
    bsz, n_chunks = x4.shape[:2]

    def const(shape):
        return pl.BlockSpec(shape, lambda t: (0, 0), pipeline_mode=pl.Buffered(1))

    hbm = pl.BlockSpec(memory_space=pl.ANY)
    col_blk = pl.BlockSpec((N_GROUPS, SUBLANES * SSM_GROUP, n_chunks),
                           lambda t: (0, t // bsz, t % bsz))
    step_buf = pltpu.VMEM((2, SUBLANES, n_chunks, D_MODEL), F32)
    step_sems = pltpu.SemaphoreType.DMA((2, SUBLANES))
    return pl.pallas_call(
        _mix_ffn_kernel,
        grid=(CHUNK // SUBLANES * bsz,),
        in_specs=[
            hbm,
            pl.BlockSpec((SUBLANES, n_chunks, D_CONV), lambda t: (t // bsz, t % bsz, 0)),
            col_blk,
            col_blk,
            const((D_SSM, 1)),
            const((D_SSM, D_SSM)),
            const((D_SSM, 1)),
            const((D_MODEL, D_MODEL)),
            const((1, D_MODEL)),
            const((D_MODEL, D_FF)),
            const((D_FF, D_MODEL)),
            const((1, D_MODEL)),
        ],
        out_specs=hbm,
        out_shape=jax.ShapeDtypeStruct(x4.shape, F32),
        scratch_shapes=[
            step_buf, step_buf, step_sems, step_sems,
            pltpu.VMEM((SUBLANES // MIX_STEPS, MIX_STEPS * n_chunks, D_MODEL), F32),
            pltpu.VMEM((SUBLANES // MIX_STEPS, MIX_STEPS * n_chunks, D_MODEL), BF16),
            pltpu.VMEM((MIX_STEPS * n_chunks, D_FF), BF16),
        ],
        compiler_params=_step_tile_params(),
        name="mix_ffn",
    )(x4, conv_out, y_col, u_col, d_col, glu_wt, glu_b_col, w_out, g_ffn, w1, w2, g_final)


def _s5_operator_inputs(lbr, lbi, zr, zi, b_re, b_im, c_re, c_im):
    g, p, h = N_GROUPS, SSM_STATE, SSM_GROUP

    sc = jnp.transpose(jnp.stack([lbr, lbi, zr, zi], axis=-1), (1, 0, 2, 3))
    csc = jnp.broadcast_to(sc[:, :, None], (g, 2, 2, p, 4)).reshape(g, 4 * p, 4)
    c = jnp.stack([c_re, c_im], axis=0)
    b = jnp.stack([b_re, b_im], axis=0)
    c_col = jnp.transpose(c, (2, 1, 0, 4, 3)).reshape(g, 4 * p, h)
    b_col = jnp.transpose(b, (2, 1, 0, 3, 4)).reshape(g, 4 * p, h)
    c_row = jnp.transpose(c, (2, 3, 1, 0, 4)).reshape(g, h, 4 * p)
    return csc, c_col, b_col, c_row


def kernel(x, meta_tokens, norm_mix_g, w_in, conv_w, conv_b, conv_ln_g, conv_ln_b,
           ssm_lam_re, ssm_lam_im, ssm_log_dt, ssm_b_re, ssm_b_im, ssm_c_re, ssm_c_im,
           ssm_d, ssm_glu_w, ssm_glu_b, w_out, norm_ffn_g, w_ff1, w_ff2, norm_final_g):
    assert w_in.shape[0] == 1, "single-layer block"
    bsz, seq, _ = x.shape
    assert seq % CHUNK == 0 and CHUNK >= N_META
    n_chunks = seq // CHUNK
    g, h = N_GROUPS, SSM_GROUP

    x4 = x.reshape(bsz, n_chunks, CHUNK, D_MODEL)
    g_mix = norm_mix_g[0][None, :]
    w_in_b = w_in[0].astype(BF16)
    u_conv, u_col = _in_proj_step(x4, g_mix, w_in_b)
    u_conv_m, u_ssm_m = _in_proj_meta(meta_tokens, g_mix, w_in_b)

    conv_out = _conv_module(u_conv, u_conv_m, conv_w[0], conv_b[0][None, :],
                            conv_ln_g[0][None, :], conv_ln_b[0][None, :], bsz)

    ldt = jnp.broadcast_to(ssm_log_dt[0][..., None], ssm_lam_re[0].shape)
    flat = lambda a: a.reshape(2 * g, SSM_STATE)
    lbr, lbi, zr, zi = [a.reshape(2, g, SSM_STATE) for a in
                        _zoh(flat(ssm_lam_re[0]), flat(ssm_lam_im[0]), flat(ldt))]
    csc, c_col, b_col, c_row = _s5_operator_inputs(
        lbr, lbi, zr, zi, ssm_b_re[0], ssm_b_im[0], ssm_c_re[0], ssm_c_im[0])

    u_meta = jnp.transpose(u_ssm_m.reshape(N_META, g, h), (1, 0, 2)).reshape(g, 1, N_META * h)
    y_col = _ssm(u_col, u_meta, csc, c_col, b_col, c_row, n_chunks)

    out = _mix_ffn(x4, conv_out, y_col, u_col, ssm_d[0][:, None],
                   ssm_glu_w[0].T.astype(BF16), ssm_glu_b[0][:, None],
                   w_out[0].astype(BF16), norm_ffn_g[0][None, :], w_ff1[0].astype(BF16),
                   w_ff2[0].astype(BF16), norm_final_g[None, :])
    return out.reshape(bsz, seq, D_MODEL)
```

```python
import functools

import jax
import jax.numpy as jnp
from jax import lax
from jax.experimental import pallas as pl
from jax.experimental.pallas import tpu as pltpu

F32 = jnp.float32
BF16 = jnp.bfloat16

D_MODEL = 1024
N_META = 16
D_CONV = 512
D_SSM = 512
CONV_WIDTH = 31
CONV_PAD = CONV_WIDTH // 2
SSM_GROUP = 16
N_GROUPS = D_SSM // SSM_GROUP
SSM_STATE = 64
D_FF = 4096
NORM_EPS = 1e-5
LANES = 128
SUBLANES = 8

CHUNK = 32
CHUNK_LANES = CHUNK * SSM_GROUP
STATE_ROWS = 4 * SSM_STATE
DIR_ROWS = 2 * SSM_STATE
STEPS_PER_VREG = LANES // SSM_GROUP

VMEM_LIMIT_BYTES = 60 * 1024 * 1024


def _rms(x, g):
    return x * lax.rsqrt(jnp.mean(x * x, axis=-1, keepdims=True) + NORM_EPS) * g


def _params(n_axes=1):
    return pltpu.CompilerParams(dimension_semantics=("parallel",) * n_axes,
                                vmem_limit_bytes=VMEM_LIMIT_BYTES)


def _step_copies(hbm_ref, buf_ref, sem_ref, tile, slot, to_hbm):
    bsz = hbm_ref.shape[0]
    slab, b = tile // bsz, tile % bsz
    copies = []
    for r in range(SUBLANES):
        hbm = hbm_ref.at[b, :, slab * SUBLANES + r, :]
        vmem = buf_ref.at[slot, r]
        src, dst = (vmem, hbm) if to_hbm else (hbm, vmem)
        copies.append(pltpu.make_async_copy(src, dst, sem_ref.at[slot, r]))
    return copies


def _fetch_steps(x_hbm, xbuf, sem):
    tile = pl.program_id(0)
    slot = tile % 2

    @pl.when(tile == 0)
    def _():
        for cp in _step_copies(x_hbm, xbuf, sem, tile, slot, False):
            cp.start()

    @pl.when(tile + 1 < pl.num_programs(0))
    def _():
        for cp in _step_copies(x_hbm, xbuf, sem, tile + 1, 1 - slot, False):
            cp.start()

    for cp in _step_copies(x_hbm, xbuf, sem, tile, slot, False):
        cp.wait()
    return slot


def _step_tile_params():
    return pltpu.CompilerParams(dimension_semantics=("arbitrary",),
                                vmem_limit_bytes=VMEM_LIMIT_BYTES)


def _in_proj_step_kernel(x_hbm, g_ref, w_ref, uc_ref, ut_ref, xbuf, sem):
    n_chunks = x_hbm.shape[1]
    slot = _fetch_steps(x_hbm, xbuf, sem)
    x = xbuf[slot].reshape(SUBLANES * n_chunks, D_MODEL)
    z = _rms(x, g_ref[...]).astype(BF16)
    p = jnp.dot(z, w_ref[...], preferred_element_type=F32)
    uc = p[:, :D_CONV] * jax.nn.sigmoid(p[:, D_CONV:2 * D_CONV])
    uc_ref[...] = uc.reshape(SUBLANES, n_chunks, D_CONV)
    ut = p[:, 2 * D_CONV:].T.astype(BF16)
    for r in range(SUBLANES):
        ut_ref[:, r * SSM_GROUP:(r + 1) * SSM_GROUP, :] = (
            ut[:, r * n_chunks:(r + 1) * n_chunks].reshape(N_GROUPS, SSM_GROUP, n_chunks))


def _in_proj_step(x4, g, w_in):
    bsz, n_chunks = x4.shape[:2]
    rows = bsz * n_chunks
    return pl.pallas_call(
        _in_proj_step_kernel,
        grid=(CHUNK // SUBLANES * bsz,),
        in_specs=[
            pl.BlockSpec(memory_space=pl.ANY),
            pl.BlockSpec((1, D_MODEL), lambda t: (0, 0)),
            pl.BlockSpec((D_MODEL, 2 * D_CONV + D_SSM), lambda t: (0, 0)),
        ],
        out_specs=[
            pl.BlockSpec((SUBLANES, n_chunks, D_CONV), lambda t: (t // bsz, t % bsz, 0)),
            pl.BlockSpec((N_GROUPS, SUBLANES * SSM_GROUP, n_chunks),
                         lambda t: (0, t // bsz, t % bsz)),
        ],
        out_shape=[
            jax.ShapeDtypeStruct((CHUNK, rows, D_CONV), F32),
            jax.ShapeDtypeStruct((N_GROUPS, CHUNK_LANES, rows), BF16),
        ],
        scratch_shapes=[pltpu.VMEM((2, SUBLANES, n_chunks, D_MODEL), F32),
                        pltpu.SemaphoreType.DMA((2, SUBLANES))],
        compiler_params=_step_tile_params(),
        name="in_proj",
    )(x4, g, w_in)


def _in_proj_meta_kernel(x_ref, g_ref, w_ref, uc_ref, us_ref):
    z = _rms(x_ref[...], g_ref[...]).astype(BF16)
    p = jnp.dot(z, w_ref[...], preferred_element_type=F32)
    uc_ref[...] = p[:, :D_CONV] * jax.nn.sigmoid(p[:, D_CONV:2 * D_CONV])
    us_ref[...] = p[:, 2 * D_CONV:]


def _in_proj_meta(meta, g, w_in):
    return pl.pallas_call(
        _in_proj_meta_kernel,
        out_shape=[jax.ShapeDtypeStruct((N_META, D_CONV), F32),
                   jax.ShapeDtypeStruct((N_META, D_SSM), F32)],
        name="in_proj_meta",
    )(meta, g, w_in)


CONV_GROUP = 4
CONV_ROWS = 16


def _conv_kernel(u_ref, um_ref, w_ref, cb_ref, lg_ref, lb_ref, o_ref, cat_ref, acc_ref):
    n_steps, n_chunks, _ = u_ref.shape
    cat_ref[CONV_PAD:CONV_PAD + n_steps] = u_ref[...]
    chunk = lax.broadcasted_iota(jnp.int32, (n_chunks, 1), 0)
    for i in range(CONV_PAD):
        s_prev = n_steps - CONV_PAD + i
        prev = pltpu.roll(u_ref[s_prev], 1, axis=0)
        meta_row = um_ref[N_META - CONV_PAD + i:N_META - CONV_PAD + i + 1, :]
        cat_ref[i] = jnp.where(chunk == 0, meta_row, prev)
        nxt = pltpu.roll(u_ref[i], n_chunks - 1, axis=0)
        cat_ref[n_steps + CONV_PAD + i] = jnp.where(chunk == n_chunks - 1, 0.0, nxt)

    def body(sg, carry):
        s0 = sg * CONV_GROUP
        for c in range(D_CONV // LANES):
            cols = slice(c * LANES, (c + 1) * LANES)
            taps = [jnp.broadcast_to(w_ref[k:k + 1, cols], (SUBLANES, LANES))
                    for k in range(CONV_WIDTH)]

            def rows_body(rt, carry2):
                for sub in range(CONV_ROWS // SUBLANES):
                    r0 = pl.multiple_of(rt * CONV_ROWS + sub * SUBLANES, SUBLANES)
                    accs = [jnp.zeros((SUBLANES, LANES), F32)] * CONV_GROUP
                    for i in range(CONV_WIDTH + CONV_GROUP - 1):
                        d = cat_ref[s0 + i, pl.ds(r0, SUBLANES), cols]
                        for j in range(CONV_GROUP):
                            if 0 <= i - j < CONV_WIDTH:
                                accs[j] = accs[j] + d * taps[i - j]
                    for j in range(CONV_GROUP):
                        acc_ref[j, pl.ds(r0, SUBLANES), cols] = accs[j]
                return carry2

            lax.fori_loop(0, n_chunks // CONV_ROWS, rows_body, 0)
        for j in range(CONV_GROUP):
            y = acc_ref[j] + cb_ref[...]
            yc = y - jnp.mean(y, axis=-1, keepdims=True)
            yn = yc * lax.rsqrt(jnp.mean(yc * yc, axis=-1, keepdims=True) + NORM_EPS)
            yn = yn * lg_ref[...] + lb_ref[...]
            o_ref[s0 + j] = (yn * jax.nn.sigmoid(yn)).astype(BF16)
        return carry

    lax.fori_loop(0, n_steps // CONV_GROUP, body, 0)


def _conv_module(u_conv, u_conv_meta, conv_w, conv_b, ln_g, ln_b, bsz):
    n_steps, rows, _ = u_conv.shape
    n_chunks = rows // bsz
    vec = pl.BlockSpec((1, D_CONV), lambda b: (0, 0))
    blk = pl.BlockSpec((n_steps, n_chunks, D_CONV), lambda b: (0, b, 0))
    return pl.pallas_call(
        _conv_kernel,
        grid=(bsz,),
        in_specs=[
            blk,
            pl.BlockSpec((N_META, D_CONV), lambda b: (0, 0)),
            pl.BlockSpec((CONV_WIDTH, D_CONV), lambda b: (0, 0)),
            vec, vec, vec,
        ],
        out_specs=blk,
        out_shape=jax.ShapeDtypeStruct((n_steps, rows, D_CONV), BF16),
        scratch_shapes=[pltpu.VMEM((n_steps + 2 * CONV_PAD, n_chunks, D_CONV), F32),
                        pltpu.VMEM((CONV_GROUP, n_chunks, D_CONV), F32)],
        compiler_params=_params(),
        name="conv_module",
    )(u_conv, u_conv_meta, conv_w, conv_b, ln_g, ln_b)


def _zoh_kernel(lre_ref, lim_ref, ldt_ref, lbr_ref, lbi_ref, zr_ref, zi_ref):
    lre = lre_ref[...]
    lim = lim_ref[...]
    dt = jnp.exp(ldt_ref[...])
    ea = jnp.exp(lre * dt)
    lbr = ea * jnp.cos(lim * dt)
    lbi = ea * jnp.sin(lim * dt)
    nr = lbr - 1.0
    den = lre * lre + lim * lim
    lbr_ref[...] = lbr
    lbi_ref[...] = lbi
    zr_ref[...] = (nr * lre + lbi * lim) / den
    zi_ref[...] = (lbi * lre - nr * lim) / den


def _zoh(lam_re, lam_im, log_dt):
    shape = jax.ShapeDtypeStruct(lam_re.shape, F32)
    return pl.pallas_call(_zoh_kernel, out_shape=[shape] * 4, name="s5_zoh")(
        lam_re, lam_im, log_dt)


def _lanes(x, n):
    return jnp.concatenate([x] * (n // LANES), axis=1)


def _cmul(ar, ai, br, bi):
    return ar * br - ai * bi, ar * bi + ai * br


def _cpow(br, bi, expo, nbits):
    rr = jnp.ones(expo.shape, F32)
    ri = jnp.zeros(expo.shape, F32)
    for k in range(nbits):
        bit = ((expo >> k) & 1) == 1
        nr, ni = _cmul(rr, ri, br, bi)
        rr = jnp.where(bit, nr, rr)
        ri = jnp.where(bit, ni, ri)
        br, bi = _cmul(br, bi, br, bi)
    return rr, ri


def _build_chunk_ops(gi, csc_ref, cc_ref, bc_ref, cr_ref, um_ref, toep_ref, wend_ref,
                     wout_ref):
    t = CHUNK
    rows = STATE_ROWS
    n_cols = CHUNK_LANES // LANES
    lam = (jnp.broadcast_to(csc_ref[gi, :, 0:1], (rows, LANES)),
           jnp.broadcast_to(csc_ref[gi, :, 1:2], (rows, LANES)))
    zr = jnp.broadcast_to(csc_ref[gi, :, 2:3], (rows, LANES))
    zi = jnp.broadcast_to(csc_ref[gi, :, 3:4], (rows, LANES))
    sel = (lax.broadcasted_iota(jnp.int32, (SSM_GROUP, LANES), 1) % SSM_GROUP
           == lax.broadcasted_iota(jnp.int32, (SSM_GROUP, LANES), 0)).astype(F32)

    def tile(ref):
        return jnp.dot(ref[gi], sel, precision=lax.Precision.HIGHEST,
                       preferred_element_type=F32)

    def swap_re_im(a):
        p = SSM_STATE
        return jnp.concatenate([a[p:2 * p], a[0:p], a[3 * p:], a[2 * p:3 * p]], axis=0)

    c_same, b_same = tile(cc_ref), tile(bc_ref)
    c_swap, b_swap = swap_re_im(c_same), swap_re_im(b_same)
    row = lax.broadcasted_iota(jnp.int32, (rows, LANES), 0)
    is_re = (row // SSM_STATE) % 2 == 0
    sgn = jnp.where(is_re, -1.0, 1.0)
    ca = jnp.where(is_re, c_same, -c_same)
    cb = -c_swap
    bb_same = zr * b_same + sgn * zi * b_swap
    sbb_swap = sgn * (zr * b_swap - sgn * zi * b_same)

    pows = {1: lam}
    k = 1
    while k < t:
        pows[2 * k] = _cmul(*pows[k], *pows[k])
        k *= 2
    step_bits = STEPS_PER_VREG.bit_length() - 1
    fwd = slice(0, DIR_ROWS)
    bwd = slice(DIR_ROWS, rows)
    every = slice(0, rows)

    def rows_of(v, rs):
        return v[0][rs], v[1][rs]

    def column(base, e, rs):
        out = rows_of(base, rs)
        for bit, val in pows.items():
            if e & bit:
                out = _cmul(*out, *rows_of(val, rs))
        return out

    def times_b(p, rs):
        return p[0] * bb_same[rs] + p[1] * sbb_swap[rs]

    def times_c(p, rs):
        return ca[rs] * p[0] + cb[rs] * p[1]

    i8 = lax.broadcasted_iota(jnp.int32, (rows, LANES), 1) // SSM_GROUP
    asc0 = _cpow(*lam, i8, step_bits)
    desc0 = _cpow(*lam, STEPS_PER_VREG - 1 - i8, step_bits)
    asc1_0 = _cmul(*asc0, *lam)
    desc1_0 = _cmul(*desc0, *lam)

    wend_f, wend_b, lag_b, wout_f, wout_b = [], [], [], [], []
    for q in range(n_cols):
        e_asc = STEPS_PER_VREG * q
        e_desc = STEPS_PER_VREG * (n_cols - 1 - q)
        asc1_q = column(asc1_0, e_asc, every)
        wend_f.append(times_b(column(desc0, e_desc, fwd), fwd))
        wend_b.append(times_b(column(asc0, e_asc, bwd), bwd))
        lag_b.append(times_b(rows_of(asc1_q, bwd), bwd))
        wout_f.append(times_c(rows_of(asc1_q, fwd), fwd))
        wout_b.append(times_c(column(desc1_0, e_desc, bwd), bwd))
        cols = slice(q * LANES, (q + 1) * LANES)
        wend_ref[gi, fwd, cols] = wend_f[q].astype(BF16)
        wend_ref[gi, bwd, cols] = wend_b[q].astype(BF16)

    meta_cols = N_META * SSM_GROUP // LANES
    x0 = sum(jnp.sum(wend_f[n_cols - meta_cols + m] * um_ref[gi, :, m * LANES:(m + 1) * LANES],
                     axis=1, keepdims=True) for m in range(meta_cols))

    lane = lax.broadcasted_iota(jnp.int32, (DIR_ROWS, LANES), 1)
    lag0_b = jnp.where(lane >= LANES - SSM_GROUP, bb_same[bwd], 0.0)
    zero = jnp.zeros((DIR_ROWS, LANES), F32)
    bcat = jnp.concatenate(
        [jnp.concatenate(wend_f + [zero] * n_cols, axis=1),
         jnp.concatenate([zero] * (n_cols - 1) + [lag0_b] + lag_b, axis=1)], axis=0)
    lane_r = lax.broadcasted_iota(jnp.int32, (SSM_GROUP, rows), 1)
    c2 = jnp.where((lane_r // SSM_STATE) % 2 == 0, cr_ref[gi], -cr_ref[gi])
    kk = jnp.dot(c2, bcat, precision=lax.Precision.HIGHEST,
                 preferred_element_type=F32)
    for tt in range(t):
        off = (t - 1 - tt) * SSM_GROUP
        toep_ref[gi, tt * SSM_GROUP:(tt + 1) * SSM_GROUP, :] = (
            kk[:, off:off + CHUNK_LANES].astype(BF16))

    wout = jnp.concatenate([jnp.concatenate(wout_f, axis=1),
                            jnp.concatenate(wout_b, axis=1)], axis=0)
    wout_ref[gi] = wout.T.astype(BF16)
    return pows[t], x0


SSM_GROUPS_PER_STEP = 2


def _chunk_scan(er, ei, ar, ai, x0, chunk, n_chunks, forward):
    width = er.shape[1]
    xr, xi = er, ei
    if x0 is not None:
        fr, fi = _cmul(ar, ai, x0[0], x0[1])
        xr = xr + jnp.where(chunk == 0, _lanes(fr, width), 0.0)
        xi = xi + jnp.where(chunk == 0, _lanes(fi, width), 0.0)

    def shifted(v, sh):
        if forward:
            return jnp.where(chunk >= sh, pltpu.roll(v, sh, axis=1), 0.0)
        return jnp.where(chunk < n_chunks - sh, pltpu.roll(v, width - sh, axis=1), 0.0)

    sh = 1
    while sh < n_chunks:
        sr, si = shifted(xr, sh), shifted(xi, sh)
        wr, wi = _lanes(ar, width), _lanes(ai, width)
        xr, xi = xr + wr * sr - wi * si, xi + wr * si + wi * sr
        ar, ai = _cmul(ar, ai, ar, ai)
        sh *= 2
    inr, ini = shifted(xr, 1), shifted(xi, 1)
    if x0 is not None:
        inr = jnp.where(chunk == 0, x0[0], inr)
        ini = jnp.where(chunk == 0, x0[1], ini)
    return inr, ini


def _ssm_kernel(u_ref, um_ref, csc_ref, cc_ref, bc_ref, cr_ref, y_ref, toep_ref, wend_ref,
                wout_ref, *, n_chunks):
    for gi in range(SSM_GROUPS_PER_STEP):
        (lam_r, lam_i), x0 = _build_chunk_ops(gi, csc_ref, cc_ref, bc_ref, cr_ref, um_ref,
                                              toep_ref, wend_ref, wout_ref)
        u = u_ref[gi]
        width = u.shape[1]
        e = jnp.dot(wend_ref[gi], u, preferred_element_type=F32)
        chunk = lax.broadcasted_iota(jnp.int32, (1, width), 1) % n_chunks
        p = SSM_STATE
        ar, ai = lam_r, lam_i
        f_in = _chunk_scan(e[0:p], e[p:2 * p], ar[0:p], ai[0:p], (x0[0:p], x0[p:2 * p]),
                           chunk, n_chunks, True)
        b_in = _chunk_scan(e[2 * p:3 * p], e[3 * p:], ar[2 * p:3 * p], ai[2 * p:3 * p],
                           None, chunk, n_chunks, False)
        xin = jnp.concatenate([f_in[0], f_in[1], b_in[0], b_in[1]], axis=0).astype(BF16)
        y = jnp.dot(toep_ref[gi], u, preferred_element_type=F32)
        y = y + jnp.dot(wout_ref[gi], xin, preferred_element_type=F32)
        y_ref[gi] = y.astype(BF16)


def _ssm(u_col, u_meta, csc, c_col, b_col, c_row, n_chunks):
    g, _, width = u_col.shape
    per = SSM_GROUPS_PER_STEP
    return pl.pallas_call(
        functools.partial(_ssm_kernel, n_chunks=n_chunks),
        grid=(g // per,),
        in_specs=[
            pl.BlockSpec((per, CHUNK_LANES, width), lambda i: (i, 0, 0)),
            pl.BlockSpec((per, 1, N_META * SSM_GROUP), lambda i: (i, 0, 0)),
            pl.BlockSpec((per, STATE_ROWS, 4), lambda i: (i, 0, 0)),
            pl.BlockSpec((per, STATE_ROWS, SSM_GROUP), lambda i: (i, 0, 0)),
            pl.BlockSpec((per, STATE_ROWS, SSM_GROUP), lambda i: (i, 0, 0)),
            pl.BlockSpec((per, SSM_GROUP, STATE_ROWS), lambda i: (i, 0, 0)),
        ],
        out_specs=pl.BlockSpec((per, CHUNK_LANES, width), lambda i: (i, 0, 0)),
        out_shape=jax.ShapeDtypeStruct((g, CHUNK_LANES, width), BF16),
        scratch_shapes=[
            pltpu.VMEM((per, CHUNK_LANES, CHUNK_LANES), BF16),
            pltpu.VMEM((per, STATE_ROWS, CHUNK_LANES), BF16),
            pltpu.VMEM((per, CHUNK_LANES, STATE_ROWS), BF16),
        ],
        compiler_params=_params(),
        name="s5_mixer",
    )(u_col, u_meta, csc, c_col, b_col, c_row)


FF_CHUNK = 1024
MIX_STEPS = 8


def _mix_ffn_kernel(x_hbm, co_ref, y_ref, ut_ref, d_ref, gwt_ref, gb_ref, wo_ref,
                    gf_ref, w1_ref, w2_ref, gl_ref, o_hbm, xbuf, obuf, sem_in, sem_out,
                    h_ref, z_ref, a_ref):
    n_chunks = x_hbm.shape[1]
    n_parts = SUBLANES // MIX_STEPS
    tile = pl.program_id(0)
    n_tiles = pl.num_programs(0)
    slot = _fetch_steps(x_hbm, xbuf, sem_in)

    @pl.when(tile >= 2)
    def _():
        for cp in _step_copies(o_hbm, obuf, sem_out, tile - 2, slot, True):
            cp.wait()

    def steps_of(part):
        return range(part * MIX_STEPS, (part + 1) * MIX_STEPS)

    def mix(part):
        def col(ref):
            return jnp.concatenate(
                [ref[:, r * SSM_GROUP:(r + 1) * SSM_GROUP, :].reshape(D_SSM, n_chunks)
                 for r in steps_of(part)], axis=1).astype(F32)

        yt = col(y_ref) + d_ref[...] * col(ut_ref)
        ge = jax.nn.gelu(yt)
        gate = (jnp.dot(gwt_ref[...], ge.astype(BF16), preferred_element_type=F32)
                + gb_ref[...])
        s5 = (ge * jax.nn.sigmoid(gate)).T.astype(BF16)
        co = co_ref[part * MIX_STEPS:(part + 1) * MIX_STEPS].reshape(
            MIX_STEPS * n_chunks, D_CONV)
        x = xbuf[slot, part * MIX_STEPS:(part + 1) * MIX_STEPS].reshape(
            MIX_STEPS * n_chunks, D_MODEL)
        h = x + jnp.dot(jnp.concatenate([co, s5], axis=1), wo_ref[...],
                        preferred_element_type=F32)
        h_ref[part] = h
        z_ref[part] = _rms(h, gf_ref[...]).astype(BF16)

    def ffn_up(part, j):
        cols = slice(j * FF_CHUNK, (j + 1) * FF_CHUNK)
        a = jnp.dot(z_ref[part], w1_ref[:, cols], preferred_element_type=F32)
        a_ref[:, cols] = jnp.square(jnp.maximum(a, 0.0)).astype(BF16)

    def ffn_down(part):
        h = h_ref[part] + jnp.dot(a_ref[...], w2_ref[...], preferred_element_type=F32)
        out = _rms(h, gl_ref[...])
        obuf[slot, part * MIX_STEPS:(part + 1) * MIX_STEPS] = out.reshape(
            MIX_STEPS, n_chunks, D_MODEL)

    for part in range(n_parts):
        mix(part)
        for j in range(D_FF // FF_CHUNK):
            ffn_up(part, j)
        ffn_down(part)

    for cp in _step_copies(o_hbm, obuf, sem_out, tile, slot, True):
        cp.start()

    @pl.when(tile == n_tiles - 1)
    def _():
        for cp in (_step_copies(o_hbm, obuf, sem_out, tile - 1, 1 - slot, True)
                   + _step_copies(o_hbm, obuf, sem_out, tile, slot, True)):
            cp.wait()


def _mix_ffn(x4, conv_out, y_col, u_col, d_col, glu_wt, glu_b_col, w_out, g_ffn, w1, w2,
             g_final):
    bsz, n_chunks = x4.shape[:2]

    def const(shape):
        return pl.BlockSpec(shape, lambda t: (0, 0), pipeline_mode=pl.Buffered(1))

    hbm = pl.BlockSpec(memory_space=pl.ANY)
    col_blk = pl.BlockSpec((N_GROUPS, SUBLANES * SSM_GROUP, n_chunks),
                           lambda t: (0, t // bsz, t % bsz))
    step_buf = pltpu.VMEM((2, SUBLANES, n_chunks, D_MODEL), F32)
    step_sems = pltpu.SemaphoreType.DMA((2, SUBLANES))
    return pl.pallas_call(
        _mix_ffn_kernel,
        grid=(CHUNK // SUBLANES * bsz,),
        in_specs=[
            hbm,
            pl.BlockSpec((SUBLANES, n_chunks, D_CONV), lambda t: (t // bsz, t % bsz, 0)),
            col_blk,
            col_blk,
            const((D_SSM, 1)),
            const((D_SSM, D_SSM)),
            const((D_SSM, 1)),
            const((D_MODEL, D_MODEL)),
            const((1, D_MODEL)),
            const((D_MODEL, D_FF)),
            const((D_FF, D_MODEL)),
            const((1, D_MODEL)),
        ],
        out_specs=hbm,
        out_shape=jax.ShapeDtypeStruct(x4.shape, F32),
        scratch_shapes=[
            step_buf, step_buf, step_sems, step_sems,
            pltpu.VMEM((SUBLANES // MIX_STEPS, MIX_STEPS * n_chunks, D_MODEL), F32),
            pltpu.VMEM((SUBLANES // MIX_STEPS, MIX_STEPS * n_chunks, D_MODEL), BF16),
            pltpu.VMEM((MIX_STEPS * n_chunks, D_FF), BF16),
        ],
        compiler_params=_step_tile_params(),
        name="mix_ffn",
    )(x4, conv_out, y_col, u_col, d_col, glu_wt, glu_b_col, w_out, g_ffn, w1, w2, g_final)


def _s5_operator_inputs(lbr, lbi, zr, zi, b_re, b_im, c_re, c_im):
    g, p, h = N_GROUPS, SSM_STATE, SSM_GROUP

    sc = jnp.transpose(jnp.stack([lbr, lbi, zr, zi], axis=-1), (1, 0, 2, 3))
    csc = jnp.broadcast_to(sc[:, :, None], (g, 2, 2, p, 4)).reshape(g, 4 * p, 4)
    c = jnp.stack([c_re, c_im], axis=0)
    b = jnp.stack([b_re, b_im], axis=0)
    c_col = jnp.transpose(c, (2, 1, 0, 4, 3)).reshape(g, 4 * p, h)
    b_col = jnp.transpose(b, (2, 1, 0, 3, 4)).reshape(g, 4 * p, h)
    c_row = jnp.transpose(c, (2, 3, 1, 0, 4)).reshape(g, h, 4 * p)
    return csc, c_col, b_col, c_row


def kernel(x, meta_tokens, norm_mix_g, w_in, conv_w, conv_b, conv_ln_g, conv_ln_b,
           ssm_lam_re, ssm_lam_im, ssm_log_dt, ssm_b_re, ssm_b_im, ssm_c_re, ssm_c_im,
           ssm_d, ssm_glu_w, ssm_glu_b, w_out, norm_ffn_g, w_ff1, w_ff2, norm_final_g):
    assert w_in.shape[0] == 1, "single-layer block"
    bsz, seq, _ = x.shape
    assert seq % CHUNK == 0 and CHUNK >= N_META
    n_chunks = seq // CHUNK
    g, h = N_GROUPS, SSM_GROUP

    x4 = x.reshape(bsz, n_chunks, CHUNK, D_MODEL)
    g_mix = norm_mix_g[0][None, :]
    w_in_b = w_in[0].astype(BF16)
    u_conv, u_col = _in_proj_step(x4, g_mix, w_in_b)
    u_conv_m, u_ssm_m = _in_proj_meta(meta_tokens, g_mix, w_in_b)

    conv_out = _conv_module(u_conv, u_conv_m, conv_w[0], conv_b[0][None, :],
                            conv_ln_g[0][None, :], conv_ln_b[0][None, :], bsz)

    ldt = jnp.broadcast_to(ssm_log_dt[0][..., None], ssm_lam_re[0].shape)
    flat = lambda a: a.reshape(2 * g, SSM_STATE)
    lbr, lbi, zr, zi = [a.reshape(2, g, SSM_STATE) for a in
                        _zoh(flat(ssm_lam_re[0]), flat(ssm_lam_im[0]), flat(ldt))]
    csc, c_col, b_col, c_row = _s5_operator_inputs(
        lbr, lbi, zr, zi, ssm_b_re[0], ssm_b_im[0], ssm_c_re[0], ssm_c_im[0])

    u_meta = jnp.transpose(u_ssm_m.reshape(N_META, g, h), (1, 0, 2)).reshape(g, 1, N_META * h)
    y_col = _ssm(u_col, u_meta, csc, c_col, b_col, c_row, n_chunks)

    out = _mix_ffn(x4, conv_out, y_col, u_col, ssm_d[0][:, None],
                   ssm_glu_w[0].T.astype(BF16), ssm_glu_b[0][:, None],
                   w_out[0].astype(BF16), norm_ffn_g[0][None, :], w_ff1[0].astype(BF16),
                   w_ff2[0].astype(BF16), norm_final_g[None, :])
    return out.reshape(bsz, seq, D_MODEL)
```

```python
import functools

import jax
import jax.numpy as jnp
from jax import lax
from jax.experimental import pallas as pl
from jax.experimental.pallas import tpu as pltpu

F32 = jnp.float32
BF16 = jnp.bfloat16

D_MODEL = 1024
N_META = 16
D_CONV = 512
D_SSM = 512
CONV_WIDTH = 31
CONV_PAD = CONV_WIDTH // 2
SSM_GROUP = 16
N_GROUPS = D_SSM // SSM_GROUP
SSM_STATE = 64
D_FF = 4096
NORM_EPS = 1e-5
LANES = 128
SUBLANES = 8

CHUNK = 32
CHUNK_LANES = CHUNK * SSM_GROUP
STATE_ROWS = 4 * SSM_STATE
DIR_ROWS = 2 * SSM_STATE
STEPS_PER_VREG = LANES // SSM_GROUP

VMEM_LIMIT_BYTES = 60 * 1024 * 1024


def _rms(x, g):
    return x * lax.rsqrt(jnp.mean(x * x, axis=-1, keepdims=True) + NORM_EPS) * g


def _params(n_axes=1):
    return pltpu.CompilerParams(dimension_semantics=("parallel",) * n_axes,
                                vmem_limit_bytes=VMEM_LIMIT_BYTES)


def _step_copies(hbm_ref, buf_ref, sem_ref, tile, slot, to_hbm):
    bsz = hbm_ref.shape[0]
    slab, b = tile // bsz, tile % bsz
    copies = []
    for r in range(SUBLANES):
        hbm = hbm_ref.at[b, :, slab * SUBLANES + r, :]
        vmem = buf_ref.at[slot, r]
        src, dst = (vmem, hbm) if to_hbm else (hbm, vmem)
        copies.append(pltpu.make_async_copy(src, dst, sem_ref.at[slot, r]))
    return copies


def _fetch_steps(x_hbm, xbuf, sem):
    tile = pl.program_id(0)
    slot = tile % 2

    @pl.when(tile == 0)
    def _():
        for cp in _step_copies(x_hbm, xbuf, sem, tile, slot, False):
            cp.start()

    @pl.when(tile + 1 < pl.num_programs(0))
    def _():
        for cp in _step_copies(x_hbm, xbuf, sem, tile + 1, 1 - slot, False):
            cp.start()

    for cp in _step_copies(x_hbm, xbuf, sem, tile, slot, False):
        cp.wait()
    return slot


def _step_tile_params():
    return pltpu.CompilerParams(dimension_semantics=("arbitrary",),
                                vmem_limit_bytes=VMEM_LIMIT_BYTES)


def _in_proj_step_kernel(x_hbm, g_ref, w_ref, uc_ref, ut_ref, xbuf, sem):
    n_chunks = x_hbm.shape[1]
    slot = _fetch_steps(x_hbm, xbuf, sem)
    x = xbuf[slot].reshape(SUBLANES * n_chunks, D_MODEL)
    z = _rms(x, g_ref[...]).astype(BF16)
    p = jnp.dot(z, w_ref[...], preferred_element_type=F32)
    uc = p[:, :D_CONV] * jax.nn.sigmoid(p[:, D_CONV:2 * D_CONV])
    uc_ref[...] = uc.reshape(SUBLANES, n_chunks, D_CONV)
    ut = p[:, 2 * D_CONV:].T.astype(BF16)
    for r in range(SUBLANES):
        ut_ref[:, r * SSM_GROUP:(r + 1) * SSM_GROUP, :] = (
            ut[:, r * n_chunks:(r + 1) * n_chunks].reshape(N_GROUPS, SSM_GROUP, n_chunks))


def _in_proj_step(x4, g, w_in):
    bsz, n_chunks = x4.shape[:2]
    rows = bsz * n_chunks
    return pl.pallas_call(
        _in_proj_step_kernel,
        grid=(CHUNK // SUBLANES * bsz,),
        in_specs=[
            pl.BlockSpec(memory_space=pl.ANY),
            pl.BlockSpec((1, D_MODEL), lambda t: (0, 0)),
            pl.BlockSpec((D_MODEL, 2 * D_CONV + D_SSM), lambda t: (0, 0)),
        ],
        out_specs=[
            pl.BlockSpec((SUBLANES, n_chunks, D_CONV), lambda t: (t // bsz, t % bsz, 0)),
            pl.BlockSpec((N_GROUPS, SUBLANES * SSM_GROUP, n_chunks),
                         lambda t: (0, t // bsz, t % bsz)),
        ],
        out_shape=[
            jax.ShapeDtypeStruct((CHUNK, rows, D_CONV), F32),
            jax.ShapeDtypeStruct((N_GROUPS, CHUNK_LANES, rows), BF16),
        ],
        scratch_shapes=[pltpu.VMEM((2, SUBLANES, n_chunks, D_MODEL), F32),
                        pltpu.SemaphoreType.DMA((2, SUBLANES))],
        compiler_params=_step_tile_params(),
        name="in_proj",
    )(x4, g, w_in)


def _in_proj_meta_kernel(x_ref, g_ref, w_ref, uc_ref, us_ref):
    z = _rms(x_ref[...], g_ref[...]).astype(BF16)
    p = jnp.dot(z, w_ref[...], preferred_element_type=F32)
    uc_ref[...] = p[:, :D_CONV] * jax.nn.sigmoid(p[:, D_CONV:2 * D_CONV])
    us_ref[...] = p[:, 2 * D_CONV:]


def _in_proj_meta(meta, g, w_in):
    return pl.pallas_call(
        _in_proj_meta_kernel,
        out_shape=[jax.ShapeDtypeStruct((N_META, D_CONV), F32),
                   jax.ShapeDtypeStruct((N_META, D_SSM), F32)],
        name="in_proj_meta",
    )(meta, g, w_in)


CONV_GROUP = 4
CONV_ROWS = 64


def _conv_kernel(u_ref, um_ref, w_ref, cb_ref, lg_ref, lb_ref, o_ref, cat_ref, acc_ref):
    n_steps, n_chunks, _ = u_ref.shape
    cat_ref[CONV_PAD:CONV_PAD + n_steps] = u_ref[...]
    chunk = lax.broadcasted_iota(jnp.int32, (n_chunks, 1), 0)
    for i in range(CONV_PAD):
        s_prev = n_steps - CONV_PAD + i
        prev = pltpu.roll(u_ref[s_prev], 1, axis=0)
        meta_row = um_ref[N_META - CONV_PAD + i:N_META - CONV_PAD + i + 1, :]
        cat_ref[i] = jnp.where(chunk == 0, meta_row, prev)
        nxt = pltpu.roll(u_ref[i], n_chunks - 1, axis=0)
        cat_ref[n_steps + CONV_PAD + i] = jnp.where(chunk == n_chunks - 1, 0.0, nxt)

    def body(sg, carry):
        s0 = sg * CONV_GROUP
        for c in range(D_CONV // LANES):
            cols = slice(c * LANES, (c + 1) * LANES)
            n_sub = CONV_ROWS // SUBLANES

            def rows_body(rt, carry2):
                r0 = [pl.multiple_of(rt * CONV_ROWS + sub * SUBLANES, SUBLANES)
                      for sub in range(n_sub)]
                accs = [[jnp.zeros((SUBLANES, LANES), F32)] * CONV_GROUP for _ in range(n_sub)]
                for i in range(CONV_WIDTH + CONV_GROUP - 1):
                    ds = [cat_ref[s0 + i, pl.ds(r0[sub], SUBLANES), cols]
                          for sub in range(n_sub)]
                    for j in range(CONV_GROUP):
                        if 0 <= i - j < CONV_WIDTH:
                            tap = jnp.broadcast_to(w_ref[i - j:i - j + 1, cols],
                                                   (SUBLANES, LANES))
                            for sub in range(n_sub):
                                accs[sub][j] = accs[sub][j] + ds[sub] * tap
                for sub in range(n_sub):
                    for j in range(CONV_GROUP):
                        acc_ref[j, pl.ds(r0[sub], SUBLANES), cols] = accs[sub][j]
                return carry2

            lax.fori_loop(0, n_chunks // CONV_ROWS, rows_body, 0)
        for j in range(CONV_GROUP):
            y = acc_ref[j] + cb_ref[...]
            yc = y - jnp.mean(y, axis=-1, keepdims=True)
            yn = yc * lax.rsqrt(jnp.mean(yc * yc, axis=-1, keepdims=True) + NORM_EPS)
            yn = yn * lg_ref[...] + lb_ref[...]
            o_ref[s0 + j] = (yn * jax.nn.sigmoid(yn)).astype(BF16)
        return carry

    lax.fori_loop(0, n_steps // CONV_GROUP, body, 0)


def _conv_module(u_conv, u_conv_meta, conv_w, conv_b, ln_g, ln_b, bsz):
    n_steps, rows, _ = u_conv.shape
    n_chunks = rows // bsz
    vec = pl.BlockSpec((1, D_CONV), lambda b: (0, 0))
    blk = pl.BlockSpec((n_steps, n_chunks, D_CONV), lambda b: (0, b, 0))
    return pl.pallas_call(
        _conv_kernel,
        grid=(bsz,),
        in_specs=[
            blk,
            pl.BlockSpec((N_META, D_CONV), lambda b: (0, 0)),
            pl.BlockSpec((CONV_WIDTH, D_CONV), lambda b: (0, 0)),
            vec, vec, vec,
        ],
        out_specs=blk,
        out_shape=jax.ShapeDtypeStruct((n_steps, rows, D_CONV), BF16),
        scratch_shapes=[pltpu.VMEM((n_steps + 2 * CONV_PAD, n_chunks, D_CONV), F32),
                        pltpu.VMEM((CONV_GROUP, n_chunks, D_CONV), F32)],
        compiler_params=_params(),
        name="conv_module",
    )(u_conv, u_conv_meta, conv_w, conv_b, ln_g, ln_b)


def _zoh_kernel(lre_ref, lim_ref, ldt_ref, lbr_ref, lbi_ref, zr_ref, zi_ref):
    lre = lre_ref[...]
    lim = lim_ref[...]
    dt = jnp.exp(ldt_ref[...])
    ea = jnp.exp(lre * dt)
    lbr = ea * jnp.cos(lim * dt)
    lbi = ea * jnp.sin(lim * dt)
    nr = lbr - 1.0
    den = lre * lre + lim * lim
    lbr_ref[...] = lbr
    lbi_ref[...] = lbi
    zr_ref[...] = (nr * lre + lbi * lim) / den
    zi_ref[...] = (lbi * lre - nr * lim) / den


def _zoh(lam_re, lam_im, log_dt):
    shape = jax.ShapeDtypeStruct(lam_re.shape, F32)
    return pl.pallas_call(_zoh_kernel, out_shape=[shape] * 4, name="s5_zoh")(
        lam_re, lam_im, log_dt)


def _lanes(x, n):
    return jnp.concatenate([x] * (n // LANES), axis=1)


def _cmul(ar, ai, br, bi):
    return ar * br - ai * bi, ar * bi + ai * br


def _cpow(br, bi, expo, nbits):
    rr = jnp.ones(expo.shape, F32)
    ri = jnp.zeros(expo.shape, F32)
    for k in range(nbits):
        bit = ((expo >> k) & 1) == 1
        nr, ni = _cmul(rr, ri, br, bi)
        rr = jnp.where(bit, nr, rr)
        ri = jnp.where(bit, ni, ri)
        br, bi = _cmul(br, bi, br, bi)
    return rr, ri


def _build_chunk_ops(gi, csc_ref, cc_ref, bc_ref, cr_ref, um_ref, toep_ref, wend_ref,
                     wout_ref):
    t = CHUNK
    rows = STATE_ROWS
    n_cols = CHUNK_LANES // LANES
    lam = (jnp.broadcast_to(csc_ref[gi, :, 0:1], (rows, LANES)),
           jnp.broadcast_to(csc_ref[gi, :, 1:2], (rows, LANES)))
    zr = jnp.broadcast_to(csc_ref[gi, :, 2:3], (rows, LANES))
    zi = jnp.broadcast_to(csc_ref[gi, :, 3:4], (rows, LANES))
    sel = (lax.broadcasted_iota(jnp.int32, (SSM_GROUP, LANES), 1) % SSM_GROUP
           == lax.broadcasted_iota(jnp.int32, (SSM_GROUP, LANES), 0)).astype(F32)

    def tile(ref):
        return jnp.dot(ref[gi], sel, precision=lax.Precision.HIGHEST,
                       preferred_element_type=F32)

    def swap_re_im(a):
        p = SSM_STATE
        return jnp.concatenate([a[p:2 * p], a[0:p], a[3 * p:], a[2 * p:3 * p]], axis=0)

    c_same, b_same = tile(cc_ref), tile(bc_ref)
    c_swap, b_swap = swap_re_im(c_same), swap_re_im(b_same)
    row = lax.broadcasted_iota(jnp.int32, (rows, LANES), 0)
    is_re = (row // SSM_STATE) % 2 == 0
    sgn = jnp.where(is_re, -1.0, 1.0)
    ca = jnp.where(is_re, c_same, -c_same)
    cb = -c_swap
    bb_same = zr * b_same + sgn * zi * b_swap
    sbb_swap = sgn * (zr * b_swap - sgn * zi * b_same)

    pows = {1: lam}
    k = 1
    while k < t:
        pows[2 * k] = _cmul(*pows[k], *pows[k])
        k *= 2
    step_bits = STEPS_PER_VREG.bit_length() - 1
    fwd = slice(0, DIR_ROWS)
    bwd = slice(DIR_ROWS, rows)
    every = slice(0, rows)

    def rows_of(v, rs):
        return v[0][rs], v[1][rs]

    def column(base, e, rs):
        out = rows_of(base, rs)
        for bit, val in pows.items():
            if e & bit:
                out = _cmul(*out, *rows_of(val, rs))
        return out

    def times_b(p, rs):
        return p[0] * bb_same[rs] + p[1] * sbb_swap[rs]

    def times_c(p, rs):
        return ca[rs] * p[0] + cb[rs] * p[1]

    i8 = lax.broadcasted_iota(jnp.int32, (rows, LANES), 1) // SSM_GROUP
    asc0 = _cpow(*lam, i8, step_bits)
    desc0 = _cpow(*lam, STEPS_PER_VREG - 1 - i8, step_bits)
    asc1_0 = _cmul(*asc0, *lam)
    desc1_0 = _cmul(*desc0, *lam)

    wend_f, wend_b, lag_b, wout_f, wout_b = [], [], [], [], []
    for q in range(n_cols):
        e_asc = STEPS_PER_VREG * q
        e_desc = STEPS_PER_VREG * (n_cols - 1 - q)
        asc1_q = column(asc1_0, e_asc, every)
        wend_f.append(times_b(column(desc0, e_desc, fwd), fwd))
        wend_b.append(times_b(column(asc0, e_asc, bwd), bwd))
        lag_b.append(times_b(rows_of(asc1_q, bwd), bwd))
        wout_f.append(times_c(rows_of(asc1_q, fwd), fwd))
        wout_b.append(times_c(column(desc1_0, e_desc, bwd), bwd))
        cols = slice(q * LANES, (q + 1) * LANES)
        wend_ref[gi, fwd, cols] = wend_f[q].astype(BF16)
        wend_ref[gi, bwd, cols] = wend_b[q].astype(BF16)

    meta_cols = N_META * SSM_GROUP // LANES
    x0 = sum(jnp.sum(wend_f[n_cols - meta_cols + m] * um_ref[gi, :, m * LANES:(m + 1) * LANES],
                     axis=1, keepdims=True) for m in range(meta_cols))

    lane = lax.broadcasted_iota(jnp.int32, (DIR_ROWS, LANES), 1)
    lag0_b = jnp.where(lane >= LANES - SSM_GROUP, bb_same[bwd], 0.0)
    zero = jnp.zeros((DIR_ROWS, LANES), F32)
    bcat = jnp.concatenate(
        [jnp.concatenate(wend_f + [zero] * n_cols, axis=1),
         jnp.concatenate([zero] * (n_cols - 1) + [lag0_b] + lag_b, axis=1)], axis=0)
    lane_r = lax.broadcasted_iota(jnp.int32, (SSM_GROUP, rows), 1)
    c2 = jnp.where((lane_r // SSM_STATE) % 2 == 0, cr_ref[gi], -cr_ref[gi])
    kk = jnp.dot(c2, bcat, precision=lax.Precision.HIGHEST,
                 preferred_element_type=F32)
    for tt in range(t):
        off = (t - 1 - tt) * SSM_GROUP
        toep_ref[gi, tt * SSM_GROUP:(tt + 1) * SSM_GROUP, :] = (
            kk[:, off:off + CHUNK_LANES].astype(BF16))

    wout = jnp.concatenate([jnp.concatenate(wout_f, axis=1),
                            jnp.concatenate(wout_b, axis=1)], axis=0)
    wout_ref[gi] = wout.T.astype(BF16)
    return pows[t], x0


SSM_GROUPS_PER_STEP = 2


def _chunk_scan(er, ei, ar, ai, x0, chunk, n_chunks, forward):
    width = er.shape[1]
    xr, xi = er, ei
    if x0 is not None:
        fr, fi = _cmul(ar, ai, x0[0], x0[1])
        xr = xr + jnp.where(chunk == 0, _lanes(fr, width), 0.0)
        xi = xi + jnp.where(chunk == 0, _lanes(fi, width), 0.0)

    def shifted(v, sh):
        if forward:
            return jnp.where(chunk >= sh, pltpu.roll(v, sh, axis=1), 0.0)
        return jnp.where(chunk < n_chunks - sh, pltpu.roll(v, width - sh, axis=1), 0.0)

    sh = 1
    while sh < n_chunks:
        sr, si = shifted(xr, sh), shifted(xi, sh)
        wr, wi = _lanes(ar, width), _lanes(ai, width)
        xr, xi = xr + wr * sr - wi * si, xi + wr * si + wi * sr
        ar, ai = _cmul(ar, ai, ar, ai)
        sh *= 2
    inr, ini = shifted(xr, 1), shifted(xi, 1)
    if x0 is not None:
        inr = jnp.where(chunk == 0, x0[0], inr)
        ini = jnp.where(chunk == 0, x0[1], ini)
    return inr, ini


def _ssm_kernel(u_ref, um_ref, csc_ref, cc_ref, bc_ref, cr_ref, y_ref, toep_ref, wend_ref,
                wout_ref, *, n_chunks):
    for gi in range(SSM_GROUPS_PER_STEP):
        (lam_r, lam_i), x0 = _build_chunk_ops(gi, csc_ref, cc_ref, bc_ref, cr_ref, um_ref,
                                              toep_ref, wend_ref, wout_ref)
        u = u_ref[gi]
        width = u.shape[1]
        e = jnp.dot(wend_ref[gi], u, preferred_element_type=F32)
        chunk = lax.broadcasted_iota(jnp.int32, (1, width), 1) % n_chunks
        p = SSM_STATE
        ar, ai = lam_r, lam_i
        f_in = _chunk_scan(e[0:p], e[p:2 * p], ar[0:p], ai[0:p], (x0[0:p], x0[p:2 * p]),
                           chunk, n_chunks, True)
        b_in = _chunk_scan(e[2 * p:3 * p], e[3 * p:], ar[2 * p:3 * p], ai[2 * p:3 * p],
                           None, chunk, n_chunks, False)
        xin = jnp.concatenate([f_in[0], f_in[1], b_in[0], b_in[1]], axis=0).astype(BF16)
        y = jnp.dot(toep_ref[gi], u, preferred_element_type=F32)
        y = y + jnp.dot(wout_ref[gi], xin, preferred_element_type=F32)
        y_ref[gi] = y.astype(BF16)


def _ssm(u_col, u_meta, csc, c_col, b_col, c_row, n_chunks):
    g, _, width = u_col.shape
    per = SSM_GROUPS_PER_STEP
    return pl.pallas_call(
        functools.partial(_ssm_kernel, n_chunks=n_chunks),
        grid=(g // per,),
        in_specs=[
            pl.BlockSpec((per, CHUNK_LANES, width), lambda i: (i, 0, 0)),
            pl.BlockSpec((per, 1, N_META * SSM_GROUP), lambda i: (i, 0, 0)),
            pl.BlockSpec((per, STATE_ROWS, 4), lambda i: (i, 0, 0)),
            pl.BlockSpec((per, STATE_ROWS, SSM_GROUP), lambda i: (i, 0, 0)),
            pl.BlockSpec((per, STATE_ROWS, SSM_GROUP), lambda i: (i, 0, 0)),
            pl.BlockSpec((per, SSM_GROUP, STATE_ROWS), lambda i: (i, 0, 0)),
        ],
        out_specs=pl.BlockSpec((per, CHUNK_LANES, width), lambda i: (i, 0, 0)),
        out_shape=jax.ShapeDtypeStruct((g, CHUNK_LANES, width), BF16),
        scratch_shapes=[
            pltpu.VMEM((per, CHUNK_LANES, CHUNK_LANES), BF16),
            pltpu.VMEM((per, STATE_ROWS, CHUNK_LANES), BF16),
            pltpu.VMEM((per, CHUNK_LANES, STATE_ROWS), BF16),
        ],
        compiler_params=_params(),
        name="s5_mixer",
    )(u_col, u_meta, csc, c_col, b_col, c_row)


FF_CHUNK = 1024
MIX_STEPS = 4


def _mix_ffn_kernel(x_hbm, co_ref, y_ref, ut_ref, d_ref, gwt_ref, gb_ref, wo_ref,
                    gf_ref, w1_ref, w2_ref, gl_ref, o_hbm, xbuf, obuf, sem_in, sem_out,
                    h_ref, z_ref, a_ref):
    n_chunks = x_hbm.shape[1]
    n_groups = SUBLANES // MIX_STEPS
    tile = pl.program_id(0)
    n_tiles = pl.num_programs(0)
    slot = _fetch_steps(x_hbm, xbuf, sem_in)

    @pl.when(tile >= 2)
    def _():
        for cp in _step_copies(o_hbm, obuf, sem_out, tile - 2, slot, True):
            cp.wait()

    group_rows = MIX_STEPS * n_chunks

    def mix(grp):
        steps = range(grp * MIX_STEPS, (grp + 1) * MIX_STEPS)
        rows = slice(grp * group_rows, (grp + 1) * group_rows)

        def col(ref):
            return jnp.concatenate(
                [ref[:, r * SSM_GROUP:(r + 1) * SSM_GROUP, :].reshape(D_SSM, n_chunks)
                 for r in steps], axis=1).astype(F32)

        yt = col(y_ref) + d_ref[...] * col(ut_ref)
        ge = jax.nn.gelu(yt)
        gate = (jnp.dot(gwt_ref[...], ge.astype(BF16), preferred_element_type=F32)
                + gb_ref[...])
        s5 = (ge * jax.nn.sigmoid(gate)).T.astype(BF16)
        co = co_ref[grp * MIX_STEPS:(grp + 1) * MIX_STEPS].reshape(group_rows, D_CONV)
        x = xbuf[slot, grp * MIX_STEPS:(grp + 1) * MIX_STEPS].reshape(group_rows, D_MODEL)
        h = x + jnp.dot(jnp.concatenate([co, s5], axis=1), wo_ref[...],
                        preferred_element_type=F32)
        h_ref[rows] = h
        z_ref[rows] = _rms(h, gf_ref[...]).astype(BF16)

    def ffn_up(rows, j):
        cols = slice(j * FF_CHUNK, (j + 1) * FF_CHUNK)
        a = jnp.dot(z_ref[rows], w1_ref[:, cols], preferred_element_type=F32)
        a_ref[rows, cols] = jnp.square(jnp.maximum(a, 0.0)).astype(BF16)

    for grp in range(n_groups):
        mix(grp)
        ffn_up(slice(grp * group_rows, (grp + 1) * group_rows), 0)
    for j in range(1, D_FF // FF_CHUNK):
        ffn_up(slice(None), j)
    h = h_ref[...] + jnp.dot(a_ref[...], w2_ref[...], preferred_element_type=F32)
    obuf[slot] = _rms(h, gl_ref[...]).reshape(SUBLANES, n_chunks, D_MODEL)

    for cp in _step_copies(o_hbm, obuf, sem_out, tile, slot, True):
        cp.start()

    @pl.when(tile == n_tiles - 1)
    def _():
        for cp in (_step_copies(o_hbm, obuf, sem_out, tile - 1, 1 - slot, True)
                   + _step_copies(o_hbm, obuf, sem_out, tile, slot, True)):
            cp.wait()


def _mix_ffn(x4, conv_out, y_col, u_col, d_col, glu_wt, glu_b_col, w_out, g_ffn, w1, w2,
             g_final):
    bsz, n_chunks = x4.shape[:2]

    def const(shape):
        return pl.BlockSpec(shape, lambda t: (0, 0), pipeline_mode=pl.Buffered(1))

    hbm = pl.BlockSpec(memory_space=pl.ANY)
    col_blk = pl.BlockSpec((N_GROUPS, SUBLANES * SSM_GROUP, n_chunks),
                           lambda t: (0, t // bsz, t % bsz))
    step_buf = pltpu.VMEM((2, SUBLANES, n_chunks, D_MODEL), F32)
    step_sems = pltpu.SemaphoreType.DMA((2, SUBLANES))
    return pl.pallas_call(
        _mix_ffn_kernel,
        grid=(CHUNK // SUBLANES * bsz,),
        in_specs=[
            hbm,
            pl.BlockSpec((SUBLANES, n_chunks, D_CONV), lambda t: (t // bsz, t % bsz, 0)),
            col_blk,
            col_blk,
            const((D_SSM, 1)),
            const((D_SSM, D_SSM)),
            const((D_SSM, 1)),
            const((D_MODEL, D_MODEL)),
            const((1, D_MODEL)),
            const((D_MODEL, D_FF)),
            const((D_FF, D_MODEL)),
            const((1, D_MODEL)),
        ],
        out_specs=hbm,
        out_shape=jax.ShapeDtypeStruct(x4.shape, F32),
        scratch_shapes=[
            step_buf, step_buf, step_sems, step_sems,
            pltpu.VMEM((SUBLANES * n_chunks, D_MODEL), F32),
            pltpu.VMEM((SUBLANES * n_chunks, D_MODEL), BF16),
            pltpu.VMEM((SUBLANES * n_chunks, D_FF), BF16),
        ],
        compiler_params=_step_tile_params(),
        name="mix_ffn",
    )(x4, conv_out, y_col, u_col, d_col, glu_wt, glu_b_col, w_out, g_ffn, w1, w2, g_final)


def _s5_operator_inputs(lbr, lbi, zr, zi, b_re, b_im, c_re, c_im):
    g, p, h = N_GROUPS, SSM_STATE, SSM_GROUP

    sc = jnp.transpose(jnp.stack([lbr, lbi, zr, zi], axis=-1), (1, 0, 2, 3))
    csc = jnp.broadcast_to(sc[:, :, None], (g, 2, 2, p, 4)).reshape(g, 4 * p, 4)
    c = jnp.stack([c_re, c_im], axis=0)
    b = jnp.stack([b_re, b_im], axis=0)
    c_col = jnp.transpose(c, (2, 1, 0, 4, 3)).reshape(g, 4 * p, h)
    b_col = jnp.transpose(b, (2, 1, 0, 3, 4)).reshape(g, 4 * p, h)
    c_row = jnp.transpose(c, (2, 3, 1, 0, 4)).reshape(g, h, 4 * p)
    return csc, c_col, b_col, c_row


def kernel(x, meta_tokens, norm_mix_g, w_in, conv_w, conv_b, conv_ln_g, conv_ln_b,
           ssm_lam_re, ssm_lam_im, ssm_log_dt, ssm_b_re, ssm_b_im, ssm_c_re, ssm_c_im,
           ssm_d, ssm_glu_w, ssm_glu_b, w_out, norm_ffn_g, w_ff1, w_ff2, norm_final_g):
    assert w_in.shape[0] == 1, "single-layer block"
    bsz, seq, _ = x.shape
    assert seq % CHUNK == 0 and CHUNK >= N_META
    n_chunks = seq // CHUNK
    g, h = N_GROUPS, SSM_GROUP

    x4 = x.reshape(bsz, n_chunks, CHUNK, D_MODEL)
    g_mix = norm_mix_g[0][None, :]
    w_in_b = w_in[0].astype(BF16)
    u_conv, u_col = _in_proj_step(x4, g_mix, w_in_b)
    u_conv_m, u_ssm_m = _in_proj_meta(meta_tokens, g_mix, w_in_b)

    conv_out = _conv_module(u_conv, u_conv_m, conv_w[0], conv_b[0][None, :],
                            conv_ln_g[0][None, :], conv_ln_b[0][None, :], bsz)

    ldt = jnp.broadcast_to(ssm_log_dt[0][..., None], ssm_lam_re[0].shape)
    flat = lambda a: a.reshape(2 * g, SSM_STATE)
    lbr, lbi, zr, zi = [a.reshape(2, g, SSM_STATE) for a in
                        _zoh(flat(ssm_lam_re[0]), flat(ssm_lam_im[0]), flat(ldt))]
    csc, c_col, b_col, c_row = _s5_operator_inputs(
        lbr, lbi, zr, zi, ssm_b_re[0], ssm_b_im[0], ssm_c_re[0], ssm_c_im[0])

    u_meta = jnp.transpose(u_ssm_m.reshape(N_META, g, h), (1, 0, 2)).reshape(g, 1, N_META * h)
    y_col = _ssm(u_col, u_meta, csc, c_col, b_col, c_row, n_chunks)

    out = _mix_ffn(x4, conv_out, y_col, u_col, ssm_d[0][:, None],
                   ssm_glu_w[0].T.astype(BF16), ssm_glu_b[0][:, None],
                   w_out[0].astype(BF16), norm_ffn_g[0][None, :], w_ff1[0].astype(BF16),
                   w_ff2[0].astype(BF16), norm_final_g[None, :])
    return out.reshape(bsz, seq, D_MODEL)
```

```python
import functools

import jax
import jax.numpy as jnp
from jax import lax
from jax.experimental import pallas as pl
from jax.experimental.pallas import tpu as pltpu

F32 = jnp.float32
BF16 = jnp.bfloat16

D_MODEL = 1024
N_META = 16
D_CONV = 512
D_SSM = 512
CONV_WIDTH = 31
CONV_PAD = CONV_WIDTH // 2
SSM_GROUP = 16
N_GROUPS = D_SSM // SSM_GROUP
SSM_STATE = 64
D_FF = 4096
NORM_EPS = 1e-5
LANES = 128
SUBLANES = 8

CHUNK = 32
CHUNK_LANES = CHUNK * SSM_GROUP
STATE_ROWS = 4 * SSM_STATE
DIR_ROWS = 2 * SSM_STATE
STEPS_PER_VREG = LANES // SSM_GROUP

VMEM_LIMIT_BYTES = 60 * 1024 * 1024


def _rms(x, g):
    return x * lax.rsqrt(jnp.mean(x * x, axis=-1, keepdims=True) + NORM_EPS) * g


def _params(n_axes=1):
    return pltpu.CompilerParams(dimension_semantics=("parallel",) * n_axes,
                                vmem_limit_bytes=VMEM_LIMIT_BYTES)


def _step_copies(hbm_ref, buf_ref, sem_ref, tile, slot, to_hbm):
    bsz = hbm_ref.shape[0]
    slab, b = tile // bsz, tile % bsz
    copies = []
    for r in range(SUBLANES):
        hbm = hbm_ref.at[b, :, slab * SUBLANES + r, :]
        vmem = buf_ref.at[slot, r]
        src, dst = (vmem, hbm) if to_hbm else (hbm, vmem)
        copies.append(pltpu.make_async_copy(src, dst, sem_ref.at[slot, r]))
    return copies


def _fetch_steps(x_hbm, xbuf, sem):
    tile = pl.program_id(0)
    slot = tile % 2

    @pl.when(tile == 0)
    def _():
        for cp in _step_copies(x_hbm, xbuf, sem, tile, slot, False):
            cp.start()

    @pl.when(tile + 1 < pl.num_programs(0))
    def _():
        for cp in _step_copies(x_hbm, xbuf, sem, tile + 1, 1 - slot, False):
            cp.start()

    for cp in _step_copies(x_hbm, xbuf, sem, tile, slot, False):
        cp.wait()
    return slot


def _step_tile_params():
    return pltpu.CompilerParams(dimension_semantics=("arbitrary",),
                                vmem_limit_bytes=VMEM_LIMIT_BYTES)


def _in_proj_step_kernel(x_hbm, g_ref, w_ref, uc_ref, ut_ref, xbuf, sem):
    n_chunks = x_hbm.shape[1]
    slot = _fetch_steps(x_hbm, xbuf, sem)
    x = xbuf[slot].reshape(SUBLANES * n_chunks, D_MODEL)
    z = _rms(x, g_ref[...]).astype(BF16)
    p = jnp.dot(z, w_ref[...], preferred_element_type=F32)
    uc = p[:, :D_CONV] * jax.nn.sigmoid(p[:, D_CONV:2 * D_CONV])
    uc_ref[...] = uc.reshape(SUBLANES, n_chunks, D_CONV)
    ut = p[:, 2 * D_CONV:].T.astype(BF16)
    for r in range(SUBLANES):
        ut_ref[:, r * SSM_GROUP:(r + 1) * SSM_GROUP, :] = (
            ut[:, r * n_chunks:(r + 1) * n_chunks].reshape(N_GROUPS, SSM_GROUP, n_chunks))


def _in_proj_step(x4, g, w_in):
    bsz, n_chunks = x4.shape[:2]
    rows = bsz * n_chunks
    return pl.pallas_call(
        _in_proj_step_kernel,
        grid=(CHUNK // SUBLANES * bsz,),
        in_specs=[
            pl.BlockSpec(memory_space=pl.ANY),
            pl.BlockSpec((1, D_MODEL), lambda t: (0, 0)),
            pl.BlockSpec((D_MODEL, 2 * D_CONV + D_SSM), lambda t: (0, 0)),
        ],
        out_specs=[
            pl.BlockSpec((SUBLANES, n_chunks, D_CONV), lambda t: (t // bsz, t % bsz, 0)),
            pl.BlockSpec((N_GROUPS, SUBLANES * SSM_GROUP, n_chunks),
                         lambda t: (0, t // bsz, t % bsz)),
        ],
        out_shape=[
            jax.ShapeDtypeStruct((CHUNK, rows, D_CONV), F32),
            jax.ShapeDtypeStruct((N_GROUPS, CHUNK_LANES, rows), BF16),
        ],
        scratch_shapes=[pltpu.VMEM((2, SUBLANES, n_chunks, D_MODEL), F32),
                        pltpu.SemaphoreType.DMA((2, SUBLANES))],
        compiler_params=_step_tile_params(),
        name="in_proj",
    )(x4, g, w_in)


def _in_proj_meta_kernel(x_ref, g_ref, w_ref, uc_ref, us_ref):
    z = _rms(x_ref[...], g_ref[...]).astype(BF16)
    p = jnp.dot(z, w_ref[...], preferred_element_type=F32)
    uc_ref[...] = p[:, :D_CONV] * jax.nn.sigmoid(p[:, D_CONV:2 * D_CONV])
    us_ref[...] = p[:, 2 * D_CONV:]


def _in_proj_meta(meta, g, w_in):
    return pl.pallas_call(
        _in_proj_meta_kernel,
        out_shape=[jax.ShapeDtypeStruct((N_META, D_CONV), F32),
                   jax.ShapeDtypeStruct((N_META, D_SSM), F32)],
        name="in_proj_meta",
    )(meta, g, w_in)


CONV_GROUP = 4
CONV_ROWS = 64


def _conv_kernel(u_ref, um_ref, w_ref, cb_ref, lg_ref, lb_ref, o_ref, cat_ref, acc_ref):
    n_steps, n_chunks, _ = u_ref.shape
    cat_ref[CONV_PAD:CONV_PAD + n_steps] = u_ref[...]
    chunk = lax.broadcasted_iota(jnp.int32, (n_chunks, 1), 0)
    for i in range(CONV_PAD):
        s_prev = n_steps - CONV_PAD + i
        prev = pltpu.roll(u_ref[s_prev], 1, axis=0)
        meta_row = um_ref[N_META - CONV_PAD + i:N_META - CONV_PAD + i + 1, :]
        cat_ref[i] = jnp.where(chunk == 0, meta_row, prev)
        nxt = pltpu.roll(u_ref[i], n_chunks - 1, axis=0)
        cat_ref[n_steps + CONV_PAD + i] = jnp.where(chunk == n_chunks - 1, 0.0, nxt)

    def body(sg, carry):
        s0 = sg * CONV_GROUP
        for c in range(D_CONV // LANES):
            cols = slice(c * LANES, (c + 1) * LANES)
            n_sub = CONV_ROWS // SUBLANES

            def rows_body(rt, carry2):
                r0 = [pl.multiple_of(rt * CONV_ROWS + sub * SUBLANES, SUBLANES)
                      for sub in range(n_sub)]
                accs = [[jnp.zeros((SUBLANES, LANES), F32)] * CONV_GROUP for _ in range(n_sub)]
                for i in range(CONV_WIDTH + CONV_GROUP - 1):
                    ds = [cat_ref[s0 + i, pl.ds(r0[sub], SUBLANES), cols]
                          for sub in range(n_sub)]
                    for j in range(CONV_GROUP):
                        if 0 <= i - j < CONV_WIDTH:
                            tap = jnp.broadcast_to(w_ref[i - j:i - j + 1, cols],
                                                   (SUBLANES, LANES))
                            for sub in range(n_sub):
                                accs[sub][j] = accs[sub][j] + ds[sub] * tap
                for sub in range(n_sub):
                    for j in range(CONV_GROUP):
                        acc_ref[j, pl.ds(r0[sub], SUBLANES), cols] = accs[sub][j]
                return carry2

            lax.fori_loop(0, n_chunks // CONV_ROWS, rows_body, 0)
        for j in range(CONV_GROUP):
            y = acc_ref[j] + cb_ref[...]
            yc = y - jnp.mean(y, axis=-1, keepdims=True)
            yn = yc * lax.rsqrt(jnp.mean(yc * yc, axis=-1, keepdims=True) + NORM_EPS)
            yn = yn * lg_ref[...] + lb_ref[...]
            o_ref[s0 + j] = (yn * jax.nn.sigmoid(yn)).astype(BF16)
        return carry

    lax.fori_loop(0, n_steps // CONV_GROUP, body, 0)


def _conv_module(u_conv, u_conv_meta, conv_w, conv_b, ln_g, ln_b, bsz):
    n_steps, rows, _ = u_conv.shape
    n_chunks = rows // bsz
    vec = pl.BlockSpec((1, D_CONV), lambda b: (0, 0))
    blk = pl.BlockSpec((n_steps, n_chunks, D_CONV), lambda b: (0, b, 0))
    return pl.pallas_call(
        _conv_kernel,
        grid=(bsz,),
        in_specs=[
            blk,
            pl.BlockSpec((N_META, D_CONV), lambda b: (0, 0)),
            pl.BlockSpec((CONV_WIDTH, D_CONV), lambda b: (0, 0)),
            vec, vec, vec,
        ],
        out_specs=blk,
        out_shape=jax.ShapeDtypeStruct((n_steps, rows, D_CONV), BF16),
        scratch_shapes=[pltpu.VMEM((n_steps + 2 * CONV_PAD, n_chunks, D_CONV), F32),
                        pltpu.VMEM((CONV_GROUP, n_chunks, D_CONV), F32)],
        compiler_params=_params(),
        name="conv_module",
    )(u_conv, u_conv_meta, conv_w, conv_b, ln_g, ln_b)


def _zoh_kernel(lre_ref, lim_ref, ldt_ref, lbr_ref, lbi_ref, zr_ref, zi_ref):
    lre = lre_ref[...]
    lim = lim_ref[...]
    dt = jnp.exp(ldt_ref[...])
    ea = jnp.exp(lre * dt)
    lbr = ea * jnp.cos(lim * dt)
    lbi = ea * jnp.sin(lim * dt)
    nr = lbr - 1.0
    den = lre * lre + lim * lim
    lbr_ref[...] = lbr
    lbi_ref[...] = lbi
    zr_ref[...] = (nr * lre + lbi * lim) / den
    zi_ref[...] = (lbi * lre - nr * lim) / den


def _zoh(lam_re, lam_im, log_dt):
    shape = jax.ShapeDtypeStruct(lam_re.shape, F32)
    return pl.pallas_call(_zoh_kernel, out_shape=[shape] * 4, name="s5_zoh")(
        lam_re, lam_im, log_dt)


def _lanes(x, n):
    return jnp.concatenate([x] * (n // LANES), axis=1)


def _cmul(ar, ai, br, bi):
    return ar * br - ai * bi, ar * bi + ai * br


def _cpow(br, bi, expo, nbits):
    rr = jnp.ones(expo.shape, F32)
    ri = jnp.zeros(expo.shape, F32)
    for k in range(nbits):
        bit = ((expo >> k) & 1) == 1
        nr, ni = _cmul(rr, ri, br, bi)
        rr = jnp.where(bit, nr, rr)
        ri = jnp.where(bit, ni, ri)
        br, bi = _cmul(br, bi, br, bi)
    return rr, ri


def _build_chunk_ops(gi, csc_ref, cc_ref, bc_ref, cr_ref, um_ref, toep_ref, wend_ref,
                     wout_ref):
    t = CHUNK
    rows = STATE_ROWS
    n_cols = CHUNK_LANES // LANES
    lam = (jnp.broadcast_to(csc_ref[gi, :, 0:1], (rows, LANES)),
           jnp.broadcast_to(csc_ref[gi, :, 1:2], (rows, LANES)))
    zr = jnp.broadcast_to(csc_ref[gi, :, 2:3], (rows, LANES))
    zi = jnp.broadcast_to(csc_ref[gi, :, 3:4], (rows, LANES))
    sel = (lax.broadcasted_iota(jnp.int32, (SSM_GROUP, LANES), 1) % SSM_GROUP
           == lax.broadcasted_iota(jnp.int32, (SSM_GROUP, LANES), 0)).astype(F32)

    def tile(ref):
        return jnp.dot(ref[gi], sel, precision=lax.Precision.HIGHEST,
                       preferred_element_type=F32)

    def swap_re_im(a):
        p = SSM_STATE
        return jnp.concatenate([a[p:2 * p], a[0:p], a[3 * p:], a[2 * p:3 * p]], axis=0)

    c_same, b_same = tile(cc_ref), tile(bc_ref)
    c_swap, b_swap = swap_re_im(c_same), swap_re_im(b_same)
    row = lax.broadcasted_iota(jnp.int32, (rows, LANES), 0)
    is_re = (row // SSM_STATE) % 2 == 0
    sgn = jnp.where(is_re, -1.0, 1.0)
    ca = jnp.where(is_re, c_same, -c_same)
    cb = -c_swap
    bb_same = zr * b_same + sgn * zi * b_swap
    sbb_swap = sgn * (zr * b_swap - sgn * zi * b_same)

    pows = {1: lam}
    k = 1
    while k < t:
        pows[2 * k] = _cmul(*pows[k], *pows[k])
        k *= 2
    step_bits = STEPS_PER_VREG.bit_length() - 1
    fwd = slice(0, DIR_ROWS)
    bwd = slice(DIR_ROWS, rows)
    every = slice(0, rows)

    def rows_of(v, rs):
        return v[0][rs], v[1][rs]

    def column(base, e, rs):
        out = rows_of(base, rs)
        for bit, val in pows.items():
            if e & bit:
                out = _cmul(*out, *rows_of(val, rs))
        return out

    def times_b(p, rs):
        return p[0] * bb_same[rs] + p[1] * sbb_swap[rs]

    def times_c(p, rs):
        return ca[rs] * p[0] + cb[rs] * p[1]

    i8 = lax.broadcasted_iota(jnp.int32, (rows, LANES), 1) // SSM_GROUP
    asc0 = _cpow(*lam, i8, step_bits)
    desc0 = _cpow(*lam, STEPS_PER_VREG - 1 - i8, step_bits)
    asc1_0 = _cmul(*asc0, *lam)
    desc1_0 = _cmul(*desc0, *lam)

    wend_f, wend_b, lag_b, wout_f, wout_b = [], [], [], [], []
    for q in range(n_cols):
        e_asc = STEPS_PER_VREG * q
        e_desc = STEPS_PER_VREG * (n_cols - 1 - q)
        asc1_q = column(asc1_0, e_asc, every)
        wend_f.append(times_b(column(desc0, e_desc, fwd), fwd))
        wend_b.append(times_b(column(asc0, e_asc, bwd), bwd))
        lag_b.append(times_b(rows_of(asc1_q, bwd), bwd))
        wout_f.append(times_c(rows_of(asc1_q, fwd), fwd))
        wout_b.append(times_c(column(desc1_0, e_desc, bwd), bwd))
        cols = slice(q * LANES, (q + 1) * LANES)
        wend_ref[gi, fwd, cols] = wend_f[q].astype(BF16)
        wend_ref[gi, bwd, cols] = wend_b[q].astype(BF16)

    meta_cols = N_META * SSM_GROUP // LANES
    x0 = sum(jnp.sum(wend_f[n_cols - meta_cols + m] * um_ref[gi, :, m * LANES:(m + 1) * LANES],
                     axis=1, keepdims=True) for m in range(meta_cols))

    lane = lax.broadcasted_iota(jnp.int32, (DIR_ROWS, LANES), 1)
    lag0_b = jnp.where(lane >= LANES - SSM_GROUP, bb_same[bwd], 0.0)
    zero = jnp.zeros((DIR_ROWS, LANES), F32)
    bcat = jnp.concatenate(
        [jnp.concatenate(wend_f + [zero] * n_cols, axis=1),
         jnp.concatenate([zero] * (n_cols - 1) + [lag0_b] + lag_b, axis=1)], axis=0)
    lane_r = lax.broadcasted_iota(jnp.int32, (SSM_GROUP, rows), 1)
    c2 = jnp.where((lane_r // SSM_STATE) % 2 == 0, cr_ref[gi], -cr_ref[gi])
    kk = jnp.dot(c2, bcat, precision=lax.Precision.HIGHEST,
                 preferred_element_type=F32)
    for tt in range(t):
        off = (t - 1 - tt) * SSM_GROUP
        toep_ref[gi, tt * SSM_GROUP:(tt + 1) * SSM_GROUP, :] = (
            kk[:, off:off + CHUNK_LANES].astype(BF16))

    wout = jnp.concatenate([jnp.concatenate(wout_f, axis=1),
                            jnp.concatenate(wout_b, axis=1)], axis=0)
    wout_ref[gi] = wout.T.astype(BF16)
    return pows[t], x0


SSM_GROUPS_PER_STEP = 2


def _chunk_scan(er, ei, ar, ai, x0, chunk, n_chunks, forward):
    width = er.shape[1]
    xr, xi = er, ei
    if x0 is not None:
        fr, fi = _cmul(ar, ai, x0[0], x0[1])
        xr = xr + jnp.where(chunk == 0, _lanes(fr, width), 0.0)
        xi = xi + jnp.where(chunk == 0, _lanes(fi, width), 0.0)

    def shifted(v, sh):
        if forward:
            return jnp.where(chunk >= sh, pltpu.roll(v, sh, axis=1), 0.0)
        return jnp.where(chunk < n_chunks - sh, pltpu.roll(v, width - sh, axis=1), 0.0)

    sh = 1
    while sh < n_chunks:
        sr, si = shifted(xr, sh), shifted(xi, sh)
        wr, wi = _lanes(ar, width), _lanes(ai, width)
        xr, xi = xr + wr * sr - wi * si, xi + wr * si + wi * sr
        ar, ai = _cmul(ar, ai, ar, ai)
        sh *= 2
    inr, ini = shifted(xr, 1), shifted(xi, 1)
    if x0 is not None:
        inr = jnp.where(chunk == 0, x0[0], inr)
        ini = jnp.where(chunk == 0, x0[1], ini)
    return inr, ini


def _ssm_kernel(u_ref, um_ref, csc_ref, cc_ref, bc_ref, cr_ref, y_ref, toep_ref, wend_ref,
                wout_ref, *, n_chunks):
    for gi in range(SSM_GROUPS_PER_STEP):
        (lam_r, lam_i), x0 = _build_chunk_ops(gi, csc_ref, cc_ref, bc_ref, cr_ref, um_ref,
                                              toep_ref, wend_ref, wout_ref)
        u = u_ref[gi]
        width = u.shape[1]
        e = jnp.dot(wend_ref[gi], u, preferred_element_type=F32)
        chunk = lax.broadcasted_iota(jnp.int32, (1, width), 1) % n_chunks
        p = SSM_STATE
        ar, ai = lam_r, lam_i
        f_in = _chunk_scan(e[0:p], e[p:2 * p], ar[0:p], ai[0:p], (x0[0:p], x0[p:2 * p]),
                           chunk, n_chunks, True)
        b_in = _chunk_scan(e[2 * p:3 * p], e[3 * p:], ar[2 * p:3 * p], ai[2 * p:3 * p],
                           None, chunk, n_chunks, False)
        xin = jnp.concatenate([f_in[0], f_in[1], b_in[0], b_in[1]], axis=0).astype(BF16)
        y = jnp.dot(toep_ref[gi], u, preferred_element_type=F32)
        y = y + jnp.dot(wout_ref[gi], xin, preferred_element_type=F32)
        y_ref[gi] = y.astype(BF16)


def _ssm(u_col, u_meta, csc, c_col, b_col, c_row, n_chunks):
    g, _, width = u_col.shape
    per = SSM_GROUPS_PER_STEP
    return pl.pallas_call(
        functools.partial(_ssm_kernel, n_chunks=n_chunks),
        grid=(g // per,),
        in_specs=[
            pl.BlockSpec((per, CHUNK_LANES, width), lambda i: (i, 0, 0)),
            pl.BlockSpec((per, 1, N_META * SSM_GROUP), lambda i: (i, 0, 0)),
            pl.BlockSpec((per, STATE_ROWS, 4), lambda i: (i, 0, 0)),
            pl.BlockSpec((per, STATE_ROWS, SSM_GROUP), lambda i: (i, 0, 0)),
            pl.BlockSpec((per, STATE_ROWS, SSM_GROUP), lambda i: (i, 0, 0)),
            pl.BlockSpec((per, SSM_GROUP, STATE_ROWS), lambda i: (i, 0, 0)),
        ],
        out_specs=pl.BlockSpec((per, CHUNK_LANES, width), lambda i: (i, 0, 0)),
        out_shape=jax.ShapeDtypeStruct((g, CHUNK_LANES, width), BF16),
        scratch_shapes=[
            pltpu.VMEM((per, CHUNK_LANES, CHUNK_LANES), BF16),
            pltpu.VMEM((per, STATE_ROWS, CHUNK_LANES), BF16),
            pltpu.VMEM((per, CHUNK_LANES, STATE_ROWS), BF16),
        ],
        compiler_params=_params(),
        name="s5_mixer",
    )(u_col, u_meta, csc, c_col, b_col, c_row)


FF_CHUNK = 1024


MIX_SLOTS = 3


def _mix_ffn_kernel(x_hbm, co_ref, y_ref, ut_ref, d_ref, gwt_ref, gb_ref, wo_ref,
                    gf_ref, w1_ref, w2_ref, gl_ref, o_hbm, buf, sem_in, sem_out,
                    mixed_ref, h_ref, z_ref, a_ref):
    n_chunks = x_hbm.shape[1]
    rows = SUBLANES * n_chunks
    step = pl.program_id(0)
    n_tiles = pl.num_programs(0) - 1
    tile = step - 1

    def mix(dst):
        def col(ref):
            return jnp.concatenate(
                [ref[:, r * SSM_GROUP:(r + 1) * SSM_GROUP, :].reshape(D_SSM, n_chunks)
                 for r in range(SUBLANES)], axis=1).astype(F32)

        yt = col(y_ref) + d_ref[...] * col(ut_ref)
        ge = jax.nn.gelu(yt)
        gate = (jnp.dot(gwt_ref[...], ge.astype(BF16), preferred_element_type=F32)
                + gb_ref[...])
        mixed_ref[dst, :, :D_CONV] = co_ref[...].reshape(rows, D_CONV)
        mixed_ref[dst, :, D_CONV:] = (ge * jax.nn.sigmoid(gate)).T.astype(BF16)

    @pl.when(step == 0)
    def _():
        for cp in _step_copies(x_hbm, buf, sem_in, 0, 0, False):
            cp.start()
        mix(0)

    @pl.when(step >= 1)
    def _():
        slot = tile % MIX_SLOTS
        nxt = (tile + 1) % MIX_SLOTS

        @pl.when(tile >= 2)
        def _():
            for cp in _step_copies(o_hbm, buf, sem_out, tile - 2, nxt, True):
                cp.wait()

        @pl.when(tile + 1 < n_tiles)
        def _():
            for cp in _step_copies(x_hbm, buf, sem_in, tile + 1, nxt, False):
                cp.start()

        for cp in _step_copies(x_hbm, buf, sem_in, tile, slot, False):
            cp.wait()

        mix(step % 2)
        h = buf[slot].reshape(rows, D_MODEL) + jnp.dot(
            mixed_ref[tile % 2], wo_ref[...], preferred_element_type=F32)
        z_ref[...] = _rms(h, gf_ref[...]).astype(BF16)
        h_ref[...] = h
        half = D_FF // 2
        for k in range(2):
            for j in range(half // FF_CHUNK):
                cols = slice(k * half + j * FF_CHUNK, k * half + (j + 1) * FF_CHUNK)
                a = jnp.dot(z_ref[...], w1_ref[:, cols], preferred_element_type=F32)
                a_ref[:, j * FF_CHUNK:(j + 1) * FF_CHUNK] = (
                    jnp.square(jnp.maximum(a, 0.0)).astype(BF16))
            h_ref[...] += jnp.dot(a_ref[...], w2_ref[k * half:(k + 1) * half, :],
                                  preferred_element_type=F32)
        buf[slot] = _rms(h_ref[...], gl_ref[...]).reshape(SUBLANES, n_chunks, D_MODEL)

        for cp in _step_copies(o_hbm, buf, sem_out, tile, slot, True):
            cp.start()

        @pl.when(tile == n_tiles - 1)
        def _():
            for cp in (_step_copies(o_hbm, buf, sem_out, tile - 1, (tile - 1) % MIX_SLOTS, True)
                       + _step_copies(o_hbm, buf, sem_out, tile, slot, True)):
                cp.wait()


def _mix_ffn(x4, conv_out, y_col, u_col, d_col, glu_wt, glu_b_col, w_out, g_ffn, w1, w2,
             g_final):
    bsz, n_chunks = x4.shape[:2]

    def const(shape):
        return pl.BlockSpec(shape, lambda t: (0, 0), pipeline_mode=pl.Buffered(1))

    n_tiles = CHUNK // SUBLANES * bsz
    hbm = pl.BlockSpec(memory_space=pl.ANY)

    def slab(s):
        return jnp.minimum(s, n_tiles - 1) // bsz

    def batch(s):
        return jnp.minimum(s, n_tiles - 1) % bsz

    col_blk = pl.BlockSpec((N_GROUPS, SUBLANES * SSM_GROUP, n_chunks),
                           lambda s: (0, slab(s), batch(s)))
    step_sems = pltpu.SemaphoreType.DMA((MIX_SLOTS, SUBLANES))
    return pl.pallas_call(
        _mix_ffn_kernel,
        grid=(n_tiles + 1,),
        in_specs=[
            hbm,
            pl.BlockSpec((SUBLANES, n_chunks, D_CONV), lambda s: (slab(s), batch(s), 0)),
            col_blk,
            col_blk,
            const((D_SSM, 1)),
            const((D_SSM, D_SSM)),
            const((D_SSM, 1)),
            const((D_MODEL, D_MODEL)),
            const((1, D_MODEL)),
            const((D_MODEL, D_FF)),
            const((D_FF, D_MODEL)),
            const((1, D_MODEL)),
        ],
        out_specs=hbm,
        out_shape=jax.ShapeDtypeStruct(x4.shape, F32),
        scratch_shapes=[
            pltpu.VMEM((MIX_SLOTS, SUBLANES, n_chunks, D_MODEL), F32),
            step_sems, step_sems,
            pltpu.VMEM((2, SUBLANES * n_chunks, D_MODEL), BF16),
            pltpu.VMEM((SUBLANES * n_chunks, D_MODEL), F32),
            pltpu.VMEM((SUBLANES * n_chunks, D_MODEL), BF16),
            pltpu.VMEM((SUBLANES * n_chunks, D_FF // 2), BF16),
        ],
        compiler_params=_step_tile_params(),
        name="mix_ffn",
    )(x4, conv_out, y_col, u_col, d_col, glu_wt, glu_b_col, w_out, g_ffn, w1, w2, g_final)


def _s5_operator_inputs(lbr, lbi, zr, zi, b_re, b_im, c_re, c_im):
    g, p, h = N_GROUPS, SSM_STATE, SSM_GROUP

    sc = jnp.transpose(jnp.stack([lbr, lbi, zr, zi], axis=-1), (1, 0, 2, 3))
    csc = jnp.broadcast_to(sc[:, :, None], (g, 2, 2, p, 4)).reshape(g, 4 * p, 4)
    c = jnp.stack([c_re, c_im], axis=0)
    b = jnp.stack([b_re, b_im], axis=0)
    c_col = jnp.transpose(c, (2, 1, 0, 4, 3)).reshape(g, 4 * p, h)
    b_col = jnp.transpose(b, (2, 1, 0, 3, 4)).reshape(g, 4 * p, h)
    c_row = jnp.transpose(c, (2, 3, 1, 0, 4)).reshape(g, h, 4 * p)
    return csc, c_col, b_col, c_row


def kernel(x, meta_tokens, norm_mix_g, w_in, conv_w, conv_b, conv_ln_g, conv_ln_b,
           ssm_lam_re, ssm_lam_im, ssm_log_dt, ssm_b_re, ssm_b_im, ssm_c_re, ssm_c_im,
           ssm_d, ssm_glu_w, ssm_glu_b, w_out, norm_ffn_g, w_ff1, w_ff2, norm_final_g):
    assert w_in.shape[0] == 1, "single-layer block"
    bsz, seq, _ = x.shape
    assert seq % CHUNK == 0 and CHUNK >= N_META
    n_chunks = seq // CHUNK
    g, h = N_GROUPS, SSM_GROUP

    x4 = x.reshape(bsz, n_chunks, CHUNK, D_MODEL)
    g_mix = norm_mix_g[0][None, :]
    w_in_b = w_in[0].astype(BF16)
    u_conv, u_col = _in_proj_step(x4, g_mix, w_in_b)
    u_conv_m, u_ssm_m = _in_proj_meta(meta_tokens, g_mix, w_in_b)

    conv_out = _conv_module(u_conv, u_conv_m, conv_w[0], conv_b[0][None, :],
                            conv_ln_g[0][None, :], conv_ln_b[0][None, :], bsz)

    ldt = jnp.broadcast_to(ssm_log_dt[0][..., None], ssm_lam_re[0].shape)
    flat = lambda a: a.reshape(2 * g, SSM_STATE)
    lbr, lbi, zr, zi = [a.reshape(2, g, SSM_STATE) for a in
                        _zoh(flat(ssm_lam_re[0]), flat(ssm_lam_im[0]), flat(ldt))]
    csc, c_col, b_col, c_row = _s5_operator_inputs(
        lbr, lbi, zr, zi, ssm_b_re[0], ssm_b_im[0], ssm_c_re[0], ssm_c_im[0])

    u_meta = jnp.transpose(u_ssm_m.reshape(N_META, g, h), (1, 0, 2)).reshape(g, 1, N_META * h)
    y_col = _ssm(u_col, u_meta, csc, c_col, b_col, c_row, n_chunks)

    out = _mix_ffn(x4, conv_out, y_col, u_col, ssm_d[0][:, None],
                   ssm_glu_w[0].T.astype(BF16), ssm_glu_b[0][:, None],
                   w_out[0].astype(BF16), norm_ffn_g[0][None, :], w_ff1[0].astype(BF16),
                   w_ff2[0].astype(BF16), norm_final_g[None, :])
    return out.reshape(bsz, seq, D_MODEL)
```

```python
import functools

import jax
import jax.numpy as jnp
from jax import lax
from jax.experimental import pallas as pl
from jax.experimental.pallas import tpu as pltpu

F32 = jnp.float32
BF16 = jnp.bfloat16

D_MODEL = 1024
N_META = 16
D_CONV = 512
D_SSM = 512
CONV_WIDTH = 31
CONV_PAD = CONV_WIDTH // 2
SSM_GROUP = 16
N_GROUPS = D_SSM // SSM_GROUP
SSM_STATE = 64
D_FF = 4096
NORM_EPS = 1e-5
LANES = 128
SUBLANES = 8

CHUNK = 32
CHUNK_LANES = CHUNK * SSM_GROUP
STATE_ROWS = 4 * SSM_STATE
DIR_ROWS = 2 * SSM_STATE
STEPS_PER_VREG = LANES // SSM_GROUP

VMEM_LIMIT_BYTES = 60 * 1024 * 1024


def _rms(x, g):
    return x * lax.rsqrt(jnp.mean(x * x, axis=-1, keepdims=True) + NORM_EPS) * g


def _params(n_axes=1):
    return pltpu.CompilerParams(dimension_semantics=("parallel",) * n_axes,
                                vmem_limit_bytes=VMEM_LIMIT_BYTES)


def _step_copies(hbm_ref, buf_ref, sem_ref, tile, slot, to_hbm):
    bsz = hbm_ref.shape[0]
    slab, b = tile // bsz, tile % bsz
    copies = []
    for r in range(SUBLANES):
        hbm = hbm_ref.at[b, :, slab * SUBLANES + r, :]
        vmem = buf_ref.at[slot, r]
        src, dst = (vmem, hbm) if to_hbm else (hbm, vmem)
        copies.append(pltpu.make_async_copy(src, dst, sem_ref.at[slot, r]))
    return copies


def _fetch_steps(x_hbm, xbuf, sem):
    tile = pl.program_id(0)
    slot = tile % 2

    @pl.when(tile == 0)
    def _():
        for cp in _step_copies(x_hbm, xbuf, sem, tile, slot, False):
            cp.start()

    @pl.when(tile + 1 < pl.num_programs(0))
    def _():
        for cp in _step_copies(x_hbm, xbuf, sem, tile + 1, 1 - slot, False):
            cp.start()

    for cp in _step_copies(x_hbm, xbuf, sem, tile, slot, False):
        cp.wait()
    return slot


def _step_tile_params():
    return pltpu.CompilerParams(dimension_semantics=("arbitrary",),
                                vmem_limit_bytes=VMEM_LIMIT_BYTES)


def _in_proj_step_kernel(x_hbm, g_ref, w_ref, uc_ref, ut_ref, xbuf, sem):
    n_chunks = x_hbm.shape[1]
    slot = _fetch_steps(x_hbm, xbuf, sem)
    x = xbuf[slot].reshape(SUBLANES * n_chunks, D_MODEL)
    z = _rms(x, g_ref[...]).astype(BF16)
    p = jnp.dot(z, w_ref[...], preferred_element_type=F32)
    uc = p[:, :D_CONV] * jax.nn.sigmoid(p[:, D_CONV:2 * D_CONV])
    uc_ref[...] = uc.reshape(SUBLANES, n_chunks, D_CONV)
    ut = p[:, 2 * D_CONV:].T.astype(BF16)
    for r in range(SUBLANES):
        ut_ref[:, r * SSM_GROUP:(r + 1) * SSM_GROUP, :] = (
            ut[:, r * n_chunks:(r + 1) * n_chunks].reshape(N_GROUPS, SSM_GROUP, n_chunks))


def _in_proj_step(x4, g, w_in):
    bsz, n_chunks = x4.shape[:2]
    rows = bsz * n_chunks
    return pl.pallas_call(
        _in_proj_step_kernel,
        grid=(CHUNK // SUBLANES * bsz,),
        in_specs=[
            pl.BlockSpec(memory_space=pl.ANY),
            pl.BlockSpec((1, D_MODEL), lambda t: (0, 0)),
            pl.BlockSpec((D_MODEL, 2 * D_CONV + D_SSM), lambda t: (0, 0)),
        ],
        out_specs=[
            pl.BlockSpec((SUBLANES, n_chunks, D_CONV), lambda t: (t // bsz, t % bsz, 0)),
            pl.BlockSpec((N_GROUPS, SUBLANES * SSM_GROUP, n_chunks),
                         lambda t: (0, t // bsz, t % bsz)),
        ],
        out_shape=[
            jax.ShapeDtypeStruct((CHUNK, rows, D_CONV), F32),
            jax.ShapeDtypeStruct((N_GROUPS, CHUNK_LANES, rows), BF16),
        ],
        scratch_shapes=[pltpu.VMEM((2, SUBLANES, n_chunks, D_MODEL), F32),
                        pltpu.SemaphoreType.DMA((2, SUBLANES))],
        compiler_params=_step_tile_params(),
        name="in_proj",
    )(x4, g, w_in)


def _in_proj_meta_kernel(x_ref, g_ref, w_ref, uc_ref, us_ref):
    z = _rms(x_ref[...], g_ref[...]).astype(BF16)
    p = jnp.dot(z, w_ref[...], preferred_element_type=F32)
    uc_ref[...] = p[:, :D_CONV] * jax.nn.sigmoid(p[:, D_CONV:2 * D_CONV])
    us_ref[...] = p[:, 2 * D_CONV:]


def _in_proj_meta(meta, g, w_in):
    return pl.pallas_call(
        _in_proj_meta_kernel,
        out_shape=[jax.ShapeDtypeStruct((N_META, D_CONV), F32),
                   jax.ShapeDtypeStruct((N_META, D_SSM), F32)],
        name="in_proj_meta",
    )(meta, g, w_in)


CONV_GROUP = 4
CONV_ROWS = 64


def _conv_kernel(u_ref, um_ref, w_ref, cb_ref, lg_ref, lb_ref, o_ref, cat_ref, acc_ref):
    n_steps, n_chunks, _ = u_ref.shape
    cat_ref[CONV_PAD:CONV_PAD + n_steps] = u_ref[...]
    chunk = lax.broadcasted_iota(jnp.int32, (n_chunks, 1), 0)
    for i in range(CONV_PAD):
        s_prev = n_steps - CONV_PAD + i
        prev = pltpu.roll(u_ref[s_prev], 1, axis=0)
        meta_row = um_ref[N_META - CONV_PAD + i:N_META - CONV_PAD + i + 1, :]
        cat_ref[i] = jnp.where(chunk == 0, meta_row, prev)
        nxt = pltpu.roll(u_ref[i], n_chunks - 1, axis=0)
        cat_ref[n_steps + CONV_PAD + i] = jnp.where(chunk == n_chunks - 1, 0.0, nxt)

    def body(sg, carry):
        s0 = sg * CONV_GROUP
        for c in range(D_CONV // LANES):
            cols = slice(c * LANES, (c + 1) * LANES)
            n_sub = CONV_ROWS // SUBLANES

            def rows_body(rt, carry2):
                r0 = [pl.multiple_of(rt * CONV_ROWS + sub * SUBLANES, SUBLANES)
                      for sub in range(n_sub)]
                accs = [[jnp.zeros((SUBLANES, LANES), F32)] * CONV_GROUP for _ in range(n_sub)]
                for i in range(CONV_WIDTH + CONV_GROUP - 1):
                    ds = [cat_ref[s0 + i, pl.ds(r0[sub], SUBLANES), cols]
                          for sub in range(n_sub)]
                    for j in range(CONV_GROUP):
                        if 0 <= i - j < CONV_WIDTH:
                            tap = jnp.broadcast_to(w_ref[i - j:i - j + 1, cols],
                                                   (SUBLANES, LANES))
                            for sub in range(n_sub):
                                accs[sub][j] = accs[sub][j] + ds[sub] * tap
                for sub in range(n_sub):
                    for j in range(CONV_GROUP):
                        acc_ref[j, pl.ds(r0[sub], SUBLANES), cols] = accs[sub][j]
                return carry2

            lax.fori_loop(0, n_chunks // CONV_ROWS, rows_body, 0)
        for j in range(CONV_GROUP):
            y = acc_ref[j] + cb_ref[...]
            yc = y - jnp.mean(y, axis=-1, keepdims=True)
            yn = yc * lax.rsqrt(jnp.mean(yc * yc, axis=-1, keepdims=True) + NORM_EPS)
            yn = yn * lg_ref[...] + lb_ref[...]
            o_ref[s0 + j] = (yn * jax.nn.sigmoid(yn)).astype(BF16)
        return carry

    lax.fori_loop(0, n_steps // CONV_GROUP, body, 0)


def _conv_module(u_conv, u_conv_meta, conv_w, conv_b, ln_g, ln_b, bsz):
    n_steps, rows, _ = u_conv.shape
    n_chunks = rows // bsz
    vec = pl.BlockSpec((1, D_CONV), lambda b: (0, 0))
    blk = pl.BlockSpec((n_steps, n_chunks, D_CONV), lambda b: (0, b, 0))
    return pl.pallas_call(
        _conv_kernel,
        grid=(bsz,),
        in_specs=[
            blk,
            pl.BlockSpec((N_META, D_CONV), lambda b: (0, 0)),
            pl.BlockSpec((CONV_WIDTH, D_CONV), lambda b: (0, 0)),
            vec, vec, vec,
        ],
        out_specs=blk,
        out_shape=jax.ShapeDtypeStruct((n_steps, rows, D_CONV), BF16),
        scratch_shapes=[pltpu.VMEM((n_steps + 2 * CONV_PAD, n_chunks, D_CONV), F32),
                        pltpu.VMEM((CONV_GROUP, n_chunks, D_CONV), F32)],
        compiler_params=_params(),
        name="conv_module",
    )(u_conv, u_conv_meta, conv_w, conv_b, ln_g, ln_b)


def _zoh_kernel(lre_ref, lim_ref, ldt_ref, lbr_ref, lbi_ref, zr_ref, zi_ref):
    lre = lre_ref[...]
    lim = lim_ref[...]
    dt = jnp.exp(ldt_ref[...])
    ea = jnp.exp(lre * dt)
    lbr = ea * jnp.cos(lim * dt)
    lbi = ea * jnp.sin(lim * dt)
    nr = lbr - 1.0
    den = lre * lre + lim * lim
    lbr_ref[...] = lbr
    lbi_ref[...] = lbi
    zr_ref[...] = (nr * lre + lbi * lim) / den
    zi_ref[...] = (lbi * lre - nr * lim) / den


def _zoh(lam_re, lam_im, log_dt):
    shape = jax.ShapeDtypeStruct(lam_re.shape, F32)
    return pl.pallas_call(_zoh_kernel, out_shape=[shape] * 4, name="s5_zoh")(
        lam_re, lam_im, log_dt)


def _lanes(x, n):
    return jnp.concatenate([x] * (n // LANES), axis=1)


def _cmul(ar, ai, br, bi):
    return ar * br - ai * bi, ar * bi + ai * br


def _cpow(br, bi, expo, nbits):
    rr = jnp.ones(expo.shape, F32)
    ri = jnp.zeros(expo.shape, F32)
    for k in range(nbits):
        bit = ((expo >> k) & 1) == 1
        nr, ni = _cmul(rr, ri, br, bi)
        rr = jnp.where(bit, nr, rr)
        ri = jnp.where(bit, ni, ri)
        br, bi = _cmul(br, bi, br, bi)
    return rr, ri


def _build_chunk_ops(gi, csc_ref, cc_ref, bc_ref, cr_ref, um_ref, toep_ref, wend_ref,
                     wout_ref):
    t = CHUNK
    rows = STATE_ROWS
    n_cols = CHUNK_LANES // LANES
    lam = (jnp.broadcast_to(csc_ref[gi, :, 0:1], (rows, LANES)),
           jnp.broadcast_to(csc_ref[gi, :, 1:2], (rows, LANES)))
    zr = jnp.broadcast_to(csc_ref[gi, :, 2:3], (rows, LANES))
    zi = jnp.broadcast_to(csc_ref[gi, :, 3:4], (rows, LANES))
    sel = (lax.broadcasted_iota(jnp.int32, (SSM_GROUP, LANES), 1) % SSM_GROUP
           == lax.broadcasted_iota(jnp.int32, (SSM_GROUP, LANES), 0)).astype(F32)

    def tile(ref):
        return jnp.dot(ref[gi], sel, precision=lax.Precision.HIGHEST,
                       preferred_element_type=F32)

    def swap_re_im(a):
        p = SSM_STATE
        return jnp.concatenate([a[p:2 * p], a[0:p], a[3 * p:], a[2 * p:3 * p]], axis=0)

    c_same, b_same = tile(cc_ref), tile(bc_ref)
    c_swap, b_swap = swap_re_im(c_same), swap_re_im(b_same)
    row = lax.broadcasted_iota(jnp.int32, (rows, LANES), 0)
    is_re = (row // SSM_STATE) % 2 == 0
    sgn = jnp.where(is_re, -1.0, 1.0)
    ca = jnp.where(is_re, c_same, -c_same)
    cb = -c_swap
    bb_same = zr * b_same + sgn * zi * b_swap
    sbb_swap = sgn * (zr * b_swap - sgn * zi * b_same)

    pows = {1: lam}
    k = 1
    while k < t:
        pows[2 * k] = _cmul(*pows[k], *pows[k])
        k *= 2
    step_bits = STEPS_PER_VREG.bit_length() - 1
    fwd = slice(0, DIR_ROWS)
    bwd = slice(DIR_ROWS, rows)
    every = slice(0, rows)

    def rows_of(v, rs):
        return v[0][rs], v[1][rs]

    def column(base, e, rs):
        out = rows_of(base, rs)
        for bit, val in pows.items():
            if e & bit:
                out = _cmul(*out, *rows_of(val, rs))
        return out

    def times_b(p, rs):
        return p[0] * bb_same[rs] + p[1] * sbb_swap[rs]

    def times_c(p, rs):
        return ca[rs] * p[0] + cb[rs] * p[1]

    i8 = lax.broadcasted_iota(jnp.int32, (rows, LANES), 1) // SSM_GROUP
    asc0 = _cpow(*lam, i8, step_bits)
    desc0 = _cpow(*lam, STEPS_PER_VREG - 1 - i8, step_bits)
    asc1_0 = _cmul(*asc0, *lam)
    desc1_0 = _cmul(*desc0, *lam)

    wend_f, wend_b, lag_b, wout_f, wout_b = [], [], [], [], []
    for q in range(n_cols):
        e_asc = STEPS_PER_VREG * q
        e_desc = STEPS_PER_VREG * (n_cols - 1 - q)
        asc1_q = column(asc1_0, e_asc, every)
        wend_f.append(times_b(column(desc0, e_desc, fwd), fwd))
        wend_b.append(times_b(column(asc0, e_asc, bwd), bwd))
        lag_b.append(times_b(rows_of(asc1_q, bwd), bwd))
        wout_f.append(times_c(rows_of(asc1_q, fwd), fwd))
        wout_b.append(times_c(column(desc1_0, e_desc, bwd), bwd))
        cols = slice(q * LANES, (q + 1) * LANES)
        wend_ref[gi, fwd, cols] = wend_f[q].astype(BF16)
        wend_ref[gi, bwd, cols] = wend_b[q].astype(BF16)

    meta_cols = N_META * SSM_GROUP // LANES
    x0 = sum(jnp.sum(wend_f[n_cols - meta_cols + m] * um_ref[gi, :, m * LANES:(m + 1) * LANES],
                     axis=1, keepdims=True) for m in range(meta_cols))

    lane = lax.broadcasted_iota(jnp.int32, (DIR_ROWS, LANES), 1)
    lag0_b = jnp.where(lane >= LANES - SSM_GROUP, bb_same[bwd], 0.0)
    zero = jnp.zeros((DIR_ROWS, LANES), F32)
    bcat = jnp.concatenate(
        [jnp.concatenate(wend_f + [zero] * n_cols, axis=1),
         jnp.concatenate([zero] * (n_cols - 1) + [lag0_b] + lag_b, axis=1)], axis=0)
    lane_r = lax.broadcasted_iota(jnp.int32, (SSM_GROUP, rows), 1)
    c2 = jnp.where((lane_r // SSM_STATE) % 2 == 0, cr_ref[gi], -cr_ref[gi])
    kk = jnp.dot(c2, bcat, precision=lax.Precision.HIGHEST,
                 preferred_element_type=F32)
    for tt in range(t):
        off = (t - 1 - tt) * SSM_GROUP
        toep_ref[gi, tt * SSM_GROUP:(tt + 1) * SSM_GROUP, :] = (
            kk[:, off:off + CHUNK_LANES].astype(BF16))

    wout = jnp.concatenate([jnp.concatenate(wout_f, axis=1),
                            jnp.concatenate(wout_b, axis=1)], axis=0)
    wout_ref[gi] = wout.T.astype(BF16)
    return pows[t], x0


SSM_GROUPS_PER_STEP = 2


def _chunk_scan(er, ei, ar, ai, x0, chunk, n_chunks, forward):
    width = er.shape[1]
    xr, xi = er, ei
    if x0 is not None:
        fr, fi = _cmul(ar, ai, x0[0], x0[1])
        xr = xr + jnp.where(chunk == 0, _lanes(fr, width), 0.0)
        xi = xi + jnp.where(chunk == 0, _lanes(fi, width), 0.0)

    def shifted(v, sh):
        if forward:
            return jnp.where(chunk >= sh, pltpu.roll(v, sh, axis=1), 0.0)
        return jnp.where(chunk < n_chunks - sh, pltpu.roll(v, width - sh, axis=1), 0.0)

    sh = 1
    while sh < n_chunks:
        sr, si = shifted(xr, sh), shifted(xi, sh)
        wr, wi = _lanes(ar, width), _lanes(ai, width)
        xr, xi = xr + wr * sr - wi * si, xi + wr * si + wi * sr
        ar, ai = _cmul(ar, ai, ar, ai)
        sh *= 2
    inr, ini = shifted(xr, 1), shifted(xi, 1)
    if x0 is not None:
        inr = jnp.where(chunk == 0, x0[0], inr)
        ini = jnp.where(chunk == 0, x0[1], ini)
    return inr, ini


def _ssm_kernel(u_ref, um_ref, csc_ref, cc_ref, bc_ref, cr_ref, y_ref, toep_ref, wend_ref,
                wout_ref, *, n_chunks):
    for gi in range(SSM_GROUPS_PER_STEP):
        (lam_r, lam_i), x0 = _build_chunk_ops(gi, csc_ref, cc_ref, bc_ref, cr_ref, um_ref,
                                              toep_ref, wend_ref, wout_ref)
        u = u_ref[gi]
        width = u.shape[1]
        e = jnp.dot(wend_ref[gi], u, preferred_element_type=F32)
        chunk = lax.broadcasted_iota(jnp.int32, (1, width), 1) % n_chunks
        p = SSM_STATE
        ar, ai = lam_r, lam_i
        f_in = _chunk_scan(e[0:p], e[p:2 * p], ar[0:p], ai[0:p], (x0[0:p], x0[p:2 * p]),
                           chunk, n_chunks, True)
        b_in = _chunk_scan(e[2 * p:3 * p], e[3 * p:], ar[2 * p:3 * p], ai[2 * p:3 * p],
                           None, chunk, n_chunks, False)
        xin = jnp.concatenate([f_in[0], f_in[1], b_in[0], b_in[1]], axis=0).astype(BF16)
        y = jnp.dot(toep_ref[gi], u, preferred_element_type=F32)
        y = y + jnp.dot(wout_ref[gi], xin, preferred_element_type=F32)
        y_ref[gi] = y.astype(BF16)


def _ssm(u_col, u_meta, csc, c_col, b_col, c_row, n_chunks):
    g, _, width = u_col.shape
    per = SSM_GROUPS_PER_STEP
    return pl.pallas_call(
        functools.partial(_ssm_kernel, n_chunks=n_chunks),
        grid=(g // per,),
        in_specs=[
            pl.BlockSpec((per, CHUNK_LANES, width), lambda i: (i, 0, 0)),
            pl.BlockSpec((per, 1, N_META * SSM_GROUP), lambda i: (i, 0, 0)),
            pl.BlockSpec((per, STATE_ROWS, 4), lambda i: (i, 0, 0)),
            pl.BlockSpec((per, STATE_ROWS, SSM_GROUP), lambda i: (i, 0, 0)),
            pl.BlockSpec((per, STATE_ROWS, SSM_GROUP), lambda i: (i, 0, 0)),
            pl.BlockSpec((per, SSM_GROUP, STATE_ROWS), lambda i: (i, 0, 0)),
        ],
        out_specs=pl.BlockSpec((per, CHUNK_LANES, width), lambda i: (i, 0, 0)),
        out_shape=jax.ShapeDtypeStruct((g, CHUNK_LANES, width), BF16),
        scratch_shapes=[
            pltpu.VMEM((per, CHUNK_LANES, CHUNK_LANES), BF16),
            pltpu.VMEM((per, STATE_ROWS, CHUNK_LANES), BF16),
            pltpu.VMEM((per, CHUNK_LANES, STATE_ROWS), BF16),
        ],
        compiler_params=_params(),
        name="s5_mixer",
    )(u_col, u_meta, csc, c_col, b_col, c_row)


FF_CHUNK = 1024


MIX_SLOTS = 3


def _mix_ffn_kernel(x_hbm, co_ref, y_ref, ut_ref, d_ref, gwt_ref, gb_ref, wo_ref,
                    gf_ref, w1_ref, w2_ref, gl_ref, o_hbm, buf, sem_in, sem_out,
                    mixed_ref, h_ref, z_ref, a_ref):
    n_chunks = x_hbm.shape[1]
    rows = SUBLANES * n_chunks
    step = pl.program_id(0)
    n_tiles = pl.num_programs(0) - 1
    tile = step - 1

    def mix_stages(dst):
        n_half = 2
        per = SUBLANES // n_half
        state = {}

        def col(ref, k):
            return jnp.concatenate(
                [ref[:, r * SSM_GROUP:(r + 1) * SSM_GROUP, :].reshape(D_SSM, n_chunks)
                 for r in range(k * per, (k + 1) * per)], axis=1).astype(F32)

        def gelu_stage(k):
            def run():
                state["ge", k] = jax.nn.gelu(col(y_ref, k) + d_ref[...] * col(ut_ref, k))
            return run

        def glu_stage(k):
            def run():
                ge = state["ge", k]
                gate = (jnp.dot(gwt_ref[...], ge.astype(BF16), preferred_element_type=F32)
                        + gb_ref[...])
                state["s5t", k] = ge * jax.nn.sigmoid(gate)
            return run

        def store_stage(k):
            def run():
                rs = slice(k * per * n_chunks, (k + 1) * per * n_chunks)
                mixed_ref[dst, rs, :D_CONV] = co_ref[k * per:(k + 1) * per].reshape(
                    per * n_chunks, D_CONV)
                mixed_ref[dst, rs, D_CONV:] = state["s5t", k].T.astype(BF16)
            return run

        return [f(k) for f in (gelu_stage, glu_stage, store_stage) for k in range(n_half)]

    @pl.when(step == 0)
    def _():
        for cp in _step_copies(x_hbm, buf, sem_in, 0, 0, False):
            cp.start()
        for stage in mix_stages(0):
            stage()

    @pl.when(step >= 1)
    def _():
        slot = tile % MIX_SLOTS
        nxt = (tile + 1) % MIX_SLOTS

        @pl.when(tile >= 2)
        def _():
            for cp in _step_copies(o_hbm, buf, sem_out, tile - 2, nxt, True):
                cp.wait()

        @pl.when(tile + 1 < n_tiles)
        def _():
            for cp in _step_copies(x_hbm, buf, sem_in, tile + 1, nxt, False):
                cp.start()

        for cp in _step_copies(x_hbm, buf, sem_in, tile, slot, False):
            cp.wait()

        stages = mix_stages(step % 2)
        h = buf[slot].reshape(rows, D_MODEL) + jnp.dot(
            mixed_ref[tile % 2], wo_ref[...], preferred_element_type=F32)
        z_ref[...] = _rms(h, gf_ref[...]).astype(BF16)
        h_ref[...] = h
        half = D_FF // 2
        for k in range(2):
            for j in range(half // FF_CHUNK):
                cols = slice(k * half + j * FF_CHUNK, k * half + (j + 1) * FF_CHUNK)
                a = jnp.dot(z_ref[...], w1_ref[:, cols], preferred_element_type=F32)
                a_ref[:, j * FF_CHUNK:(j + 1) * FF_CHUNK] = (
                    jnp.square(jnp.maximum(a, 0.0)).astype(BF16))
                stages.pop(0)()
            h_ref[...] += jnp.dot(a_ref[...], w2_ref[k * half:(k + 1) * half, :],
                                  preferred_element_type=F32)
            stages.pop(0)()
        assert not stages
        buf[slot] = _rms(h_ref[...], gl_ref[...]).reshape(SUBLANES, n_chunks, D_MODEL)

        for cp in _step_copies(o_hbm, buf, sem_out, tile, slot, True):
            cp.start()

        @pl.when(tile == n_tiles - 1)
        def _():
            for cp in (_step_copies(o_hbm, buf, sem_out, tile - 1, (tile - 1) % MIX_SLOTS, True)
                       + _step_copies(o_hbm, buf, sem_out, tile, slot, True)):
                cp.wait()


def _mix_ffn(x4, conv_out, y_col, u_col, d_col, glu_wt, glu_b_col, w_out, g_ffn, w1, w2,
             g_final):
    bsz, n_chunks = x4.shape[:2]

    def const(shape):
        return pl.BlockSpec(shape, lambda t: (0, 0), pipeline_mode=pl.Buffered(1))

    n_tiles = CHUNK // SUBLANES * bsz
    hbm = pl.BlockSpec(memory_space=pl.ANY)

    def slab(s):
        return jnp.minimum(s, n_tiles - 1) // bsz

    def batch(s):
        return jnp.minimum(s, n_tiles - 1) % bsz

    col_blk = pl.BlockSpec((N_GROUPS, SUBLANES * SSM_GROUP, n_chunks),
                           lambda s: (0, slab(s), batch(s)))
    step_sems = pltpu.SemaphoreType.DMA((MIX_SLOTS, SUBLANES))
    return pl.pallas_call(
        _mix_ffn_kernel,
        grid=(n_tiles + 1,),
        in_specs=[
            hbm,
            pl.BlockSpec((SUBLANES, n_chunks, D_CONV), lambda s: (slab(s), batch(s), 0)),
            col_blk,
            col_blk,
            const((D_SSM, 1)),
            const((D_SSM, D_SSM)),
            const((D_SSM, 1)),
            const((D_MODEL, D_MODEL)),
            const((1, D_MODEL)),
            const((D_MODEL, D_FF)),
            const((D_FF, D_MODEL)),
            const((1, D_MODEL)),
        ],
        out_specs=hbm,
        out_shape=jax.ShapeDtypeStruct(x4.shape, F32),
        scratch_shapes=[
            pltpu.VMEM((MIX_SLOTS, SUBLANES, n_chunks, D_MODEL), F32),
            step_sems, step_sems,
            pltpu.VMEM((2, SUBLANES * n_chunks, D_MODEL), BF16),
            pltpu.VMEM((SUBLANES * n_chunks, D_MODEL), F32),
            pltpu.VMEM((SUBLANES * n_chunks, D_MODEL), BF16),
            pltpu.VMEM((SUBLANES * n_chunks, D_FF // 2), BF16),
        ],
        compiler_params=_step_tile_params(),
        name="mix_ffn",
    )(x4, conv_out, y_col, u_col, d_col, glu_wt, glu_b_col, w_out, g_ffn, w1, w2, g_final)


def _s5_operator_inputs(lbr, lbi, zr, zi, b_re, b_im, c_re, c_im):
    g, p, h = N_GROUPS, SSM_STATE, SSM_GROUP

    sc = jnp.transpose(jnp.stack([lbr, lbi, zr, zi], axis=-1), (1, 0, 2, 3))
    csc = jnp.broadcast_to(sc[:, :, None], (g, 2, 2, p, 4)).reshape(g, 4 * p, 4)
    c = jnp.stack([c_re, c_im], axis=0)
    b = jnp.stack([b_re, b_im], axis=0)
    c_col = jnp.transpose(c, (2, 1, 0, 4, 3)).reshape(g, 4 * p, h)
    b_col = jnp.transpose(b, (2, 1, 0, 3, 4)).reshape(g, 4 * p, h)
    c_row = jnp.transpose(c, (2, 3, 1, 0, 4)).reshape(g, h, 4 * p)
    return csc, c_col, b_col, c_row


def kernel(x, meta_tokens, norm_mix_g, w_in, conv_w, conv_b, conv_ln_g, conv_ln_b,
           ssm_lam_re, ssm_lam_im, ssm_log_dt, ssm_b_re, ssm_b_im, ssm_c_re, ssm_c_im,
           ssm_d, ssm_glu_w, ssm_glu_b, w_out, norm_ffn_g, w_ff1, w_ff2, norm_final_g):
    assert w_in.shape[0] == 1, "single-layer block"
    bsz, seq, _ = x.shape
    assert seq % CHUNK == 0 and CHUNK >= N_META
    n_chunks = seq // CHUNK
    g, h = N_GROUPS, SSM_GROUP

    x4 = x.reshape(bsz, n_chunks, CHUNK, D_MODEL)
    g_mix = norm_mix_g[0][None, :]
    w_in_b = w_in[0].astype(BF16)
    u_conv, u_col = _in_proj_step(x4, g_mix, w_in_b)
    u_conv_m, u_ssm_m = _in_proj_meta(meta_tokens, g_mix, w_in_b)

    conv_out = _conv_module(u_conv, u_conv_m, conv_w[0], conv_b[0][None, :],
                            conv_ln_g[0][None, :], conv_ln_b[0][None, :], bsz)

    ldt = jnp.broadcast_to(ssm_log_dt[0][..., None], ssm_lam_re[0].shape)
    flat = lambda a: a.reshape(2 * g, SSM_STATE)
    lbr, lbi, zr, zi = [a.reshape(2, g, SSM_STATE) for a in
                        _zoh(flat(ssm_lam_re[0]), flat(ssm_lam_im[0]), flat(ldt))]
    csc, c_col, b_col, c_row = _s5_operator_inputs(
        lbr, lbi, zr, zi, ssm_b_re[0], ssm_b_im[0], ssm_c_re[0], ssm_c_im[0])

    u_meta = jnp.transpose(u_ssm_m.reshape(N_META, g, h), (1, 0, 2)).reshape(g, 1, N_META * h)
    y_col = _ssm(u_col, u_meta, csc, c_col, b_col, c_row, n_chunks)

    out = _mix_ffn(x4, conv_out, y_col, u_col, ssm_d[0][:, None],
                   ssm_glu_w[0].T.astype(BF16), ssm_glu_b[0][:, None],
                   w_out[0].astype(BF16), norm_ffn_g[0][None, :], w_ff1[0].astype(BF16),
                   w_ff2[0].astype(BF16), norm_final_g[None, :])
    return out.reshape(bsz, seq, D_MODEL)
```

```python
import functools

import jax
import jax.numpy as jnp
from jax import lax
from jax.experimental import pallas as pl
from jax.experimental.pallas import tpu as pltpu

F32 = jnp.float32
BF16 = jnp.bfloat16

D_MODEL = 1024
N_META = 16
D_CONV = 512
D_SSM = 512
CONV_WIDTH = 31
CONV_PAD = CONV_WIDTH // 2
SSM_GROUP = 16
N_GROUPS = D_SSM // SSM_GROUP
SSM_STATE = 64
D_FF = 4096
NORM_EPS = 1e-5
LANES = 128
SUBLANES = 8

CHUNK = 32
CHUNK_LANES = CHUNK * SSM_GROUP
STATE_ROWS = 4 * SSM_STATE
DIR_ROWS = 2 * SSM_STATE
STEPS_PER_VREG = LANES // SSM_GROUP

VMEM_LIMIT_BYTES = 60 * 1024 * 1024


def _rms(x, g):
    return x * lax.rsqrt(jnp.mean(x * x, axis=-1, keepdims=True) + NORM_EPS) * g


def _params(n_axes=1):
    return pltpu.CompilerParams(dimension_semantics=("parallel",) * n_axes,
                                vmem_limit_bytes=VMEM_LIMIT_BYTES)


def _step_copies(hbm_ref, buf_ref, sem_ref, tile, slot, to_hbm):
    bsz = hbm_ref.shape[0]
    slab, b = tile // bsz, tile % bsz
    copies = []
    for r in range(SUBLANES):
        hbm = hbm_ref.at[b, :, slab * SUBLANES + r, :]
        vmem = buf_ref.at[slot, r]
        src, dst = (vmem, hbm) if to_hbm else (hbm, vmem)
        copies.append(pltpu.make_async_copy(src, dst, sem_ref.at[slot, r]))
    return copies


def _fetch_steps(x_hbm, xbuf, sem):
    tile = pl.program_id(0)
    slot = tile % 2

    @pl.when(tile == 0)
    def _():
        for cp in _step_copies(x_hbm, xbuf, sem, tile, slot, False):
            cp.start()

    @pl.when(tile + 1 < pl.num_programs(0))
    def _():
        for cp in _step_copies(x_hbm, xbuf, sem, tile + 1, 1 - slot, False):
            cp.start()

    for cp in _step_copies(x_hbm, xbuf, sem, tile, slot, False):
        cp.wait()
    return slot


def _step_tile_params():
    return pltpu.CompilerParams(dimension_semantics=("arbitrary",),
                                vmem_limit_bytes=VMEM_LIMIT_BYTES)


def _in_proj_step_kernel(x_hbm, g_ref, w_ref, uc_ref, ut_ref, xbuf, sem):
    n_chunks = x_hbm.shape[1]
    slot = _fetch_steps(x_hbm, xbuf, sem)
    x = xbuf[slot].reshape(SUBLANES * n_chunks, D_MODEL)
    z = _rms(x, g_ref[...]).astype(BF16)
    p = jnp.dot(z, w_ref[...], preferred_element_type=F32)
    uc = p[:, :D_CONV] * jax.nn.sigmoid(p[:, D_CONV:2 * D_CONV])
    uc_ref[...] = uc.reshape(SUBLANES, n_chunks, D_CONV)
    ut = p[:, 2 * D_CONV:].T.astype(BF16)
    for r in range(SUBLANES):
        ut_ref[:, r * SSM_GROUP:(r + 1) * SSM_GROUP, :] = (
            ut[:, r * n_chunks:(r + 1) * n_chunks].reshape(N_GROUPS, SSM_GROUP, n_chunks))


def _in_proj_step(x4, g, w_in):
    bsz, n_chunks = x4.shape[:2]
    rows = bsz * n_chunks
    return pl.pallas_call(
        _in_proj_step_kernel,
        grid=(CHUNK // SUBLANES * bsz,),
        in_specs=[
            pl.BlockSpec(memory_space=pl.ANY),
            pl.BlockSpec((1, D_MODEL), lambda t: (0, 0)),
            pl.BlockSpec((D_MODEL, 2 * D_CONV + D_SSM), lambda t: (0, 0)),
        ],
        out_specs=[
            pl.BlockSpec((SUBLANES, n_chunks, D_CONV), lambda t: (t // bsz, t % bsz, 0)),
            pl.BlockSpec((N_GROUPS, SUBLANES * SSM_GROUP, n_chunks),
                         lambda t: (0, t // bsz, t % bsz)),
        ],
        out_shape=[
            jax.ShapeDtypeStruct((CHUNK, rows, D_CONV), F32),
            jax.ShapeDtypeStruct((N_GROUPS, CHUNK_LANES, rows), BF16),
        ],
        scratch_shapes=[pltpu.VMEM((2, SUBLANES, n_chunks, D_MODEL), F32),
                        pltpu.SemaphoreType.DMA((2, SUBLANES))],
        compiler_params=_step_tile_params(),
        name="in_proj",
    )(x4, g, w_in)


def _in_proj_meta_kernel(x_ref, g_ref, w_ref, uc_ref, us_ref):
    z = _rms(x_ref[...], g_ref[...]).astype(BF16)
    p = jnp.dot(z, w_ref[...], preferred_element_type=F32)
    uc_ref[...] = p[:, :D_CONV] * jax.nn.sigmoid(p[:, D_CONV:2 * D_CONV])
    us_ref[...] = p[:, 2 * D_CONV:]


def _in_proj_meta(meta, g, w_in):
    return pl.pallas_call(
        _in_proj_meta_kernel,
        out_shape=[jax.ShapeDtypeStruct((N_META, D_CONV), F32),
                   jax.ShapeDtypeStruct((N_META, D_SSM), F32)],
        name="in_proj_meta",
    )(meta, g, w_in)


CONV_GROUP = 4
CONV_ROWS = 64


def _conv_kernel(u_ref, um_ref, w_ref, o_ref, cat_ref, acc_ref):
    n_steps, n_chunks, _ = u_ref.shape
    cat_ref[CONV_PAD:CONV_PAD + n_steps] = u_ref[...]
    chunk = lax.broadcasted_iota(jnp.int32, (n_chunks, 1), 0)
    for i in range(CONV_PAD):
        s_prev = n_steps - CONV_PAD + i
        prev = pltpu.roll(u_ref[s_prev], 1, axis=0)
        meta_row = um_ref[N_META - CONV_PAD + i:N_META - CONV_PAD + i + 1, :]
        cat_ref[i] = jnp.where(chunk == 0, meta_row, prev)
        nxt = pltpu.roll(u_ref[i], n_chunks - 1, axis=0)
        cat_ref[n_steps + CONV_PAD + i] = jnp.where(chunk == n_chunks - 1, 0.0, nxt)

    def body(sg, carry):
        s0 = sg * CONV_GROUP
        for c in range(D_CONV // LANES):
            cols = slice(c * LANES, (c + 1) * LANES)
            n_sub = CONV_ROWS // SUBLANES

            def rows_body(rt, carry2):
                r0 = [pl.multiple_of(rt * CONV_ROWS + sub * SUBLANES, SUBLANES)
                      for sub in range(n_sub)]
                accs = [[jnp.zeros((SUBLANES, LANES), F32)] * CONV_GROUP for _ in range(n_sub)]
                for i in range(CONV_WIDTH + CONV_GROUP - 1):
                    ds = [cat_ref[s0 + i, pl.ds(r0[sub], SUBLANES), cols]
                          for sub in range(n_sub)]
                    for j in range(CONV_GROUP):
                        if 0 <= i - j < CONV_WIDTH:
                            tap = jnp.broadcast_to(w_ref[i - j:i - j + 1, cols],
                                                   (SUBLANES, LANES))
                            for sub in range(n_sub):
                                accs[sub][j] = accs[sub][j] + ds[sub] * tap
                for sub in range(n_sub):
                    for j in range(CONV_GROUP):
                        acc_ref[j, pl.ds(r0[sub], SUBLANES), cols] = accs[sub][j]
                return carry2

            lax.fori_loop(0, n_chunks // CONV_ROWS, rows_body, 0)
        for j in range(CONV_GROUP):
            o_ref[s0 + j] = acc_ref[j].astype(BF16)
        return carry

    lax.fori_loop(0, n_steps // CONV_GROUP, body, 0)


def _conv_taps(u_conv, u_conv_meta, conv_w, bsz):
    n_steps, rows, _ = u_conv.shape
    n_chunks = rows // bsz
    blk = pl.BlockSpec((n_steps, n_chunks, D_CONV), lambda b: (0, b, 0))
    return pl.pallas_call(
        _conv_kernel,
        grid=(bsz,),
        in_specs=[
            blk,
            pl.BlockSpec((N_META, D_CONV), lambda b: (0, 0)),
            pl.BlockSpec((CONV_WIDTH, D_CONV), lambda b: (0, 0)),
        ],
        out_specs=blk,
        out_shape=jax.ShapeDtypeStruct((n_steps, rows, D_CONV), BF16),
        scratch_shapes=[pltpu.VMEM((n_steps + 2 * CONV_PAD, n_chunks, D_CONV), F32),
                        pltpu.VMEM((CONV_GROUP, n_chunks, D_CONV), F32)],
        compiler_params=_params(),
        name="conv_taps",
    )(u_conv, u_conv_meta, conv_w)


def _zoh_kernel(lre_ref, lim_ref, ldt_ref, lbr_ref, lbi_ref, zr_ref, zi_ref):
    lre = lre_ref[...]
    lim = lim_ref[...]
    dt = jnp.exp(ldt_ref[...])
    ea = jnp.exp(lre * dt)
    lbr = ea * jnp.cos(lim * dt)
    lbi = ea * jnp.sin(lim * dt)
    nr = lbr - 1.0
    den = lre * lre + lim * lim
    lbr_ref[...] = lbr
    lbi_ref[...] = lbi
    zr_ref[...] = (nr * lre + lbi * lim) / den
    zi_ref[...] = (lbi * lre - nr * lim) / den


def _zoh(lam_re, lam_im, log_dt):
    shape = jax.ShapeDtypeStruct(lam_re.shape, F32)
    return pl.pallas_call(_zoh_kernel, out_shape=[shape] * 4, name="s5_zoh")(
        lam_re, lam_im, log_dt)


def _lanes(x, n):
    return jnp.concatenate([x] * (n // LANES), axis=1)


def _cmul(ar, ai, br, bi):
    return ar * br - ai * bi, ar * bi + ai * br


def _cpow(br, bi, expo, nbits):
    rr = jnp.ones(expo.shape, F32)
    ri = jnp.zeros(expo.shape, F32)
    for k in range(nbits):
        bit = ((expo >> k) & 1) == 1
        nr, ni = _cmul(rr, ri, br, bi)
        rr = jnp.where(bit, nr, rr)
        ri = jnp.where(bit, ni, ri)
        br, bi = _cmul(br, bi, br, bi)
    return rr, ri


def _build_chunk_ops(gi, csc_ref, cc_ref, bc_ref, cr_ref, um_ref, toep_ref, wend_ref,
                     wout_ref):
    t = CHUNK
    rows = STATE_ROWS
    n_cols = CHUNK_LANES // LANES
    lam = (jnp.broadcast_to(csc_ref[gi, :, 0:1], (rows, LANES)),
           jnp.broadcast_to(csc_ref[gi, :, 1:2], (rows, LANES)))
    zr = jnp.broadcast_to(csc_ref[gi, :, 2:3], (rows, LANES))
    zi = jnp.broadcast_to(csc_ref[gi, :, 3:4], (rows, LANES))
    sel = (lax.broadcasted_iota(jnp.int32, (SSM_GROUP, LANES), 1) % SSM_GROUP
           == lax.broadcasted_iota(jnp.int32, (SSM_GROUP, LANES), 0)).astype(F32)

    def tile(ref):
        return jnp.dot(ref[gi], sel, precision=lax.Precision.HIGHEST,
                       preferred_element_type=F32)

    def swap_re_im(a):
        p = SSM_STATE
        return jnp.concatenate([a[p:2 * p], a[0:p], a[3 * p:], a[2 * p:3 * p]], axis=0)

    c_same, b_same = tile(cc_ref), tile(bc_ref)
    c_swap, b_swap = swap_re_im(c_same), swap_re_im(b_same)
    row = lax.broadcasted_iota(jnp.int32, (rows, LANES), 0)
    is_re = (row // SSM_STATE) % 2 == 0
    sgn = jnp.where(is_re, -1.0, 1.0)
    ca = jnp.where(is_re, c_same, -c_same)
    cb = -c_swap
    bb_same = zr * b_same + sgn * zi * b_swap
    sbb_swap = sgn * (zr * b_swap - sgn * zi * b_same)

    pows = {1: lam}
    k = 1
    while k < t:
        pows[2 * k] = _cmul(*pows[k], *pows[k])
        k *= 2
    step_bits = STEPS_PER_VREG.bit_length() - 1
    fwd = slice(0, DIR_ROWS)
    bwd = slice(DIR_ROWS, rows)
    every = slice(0, rows)

    def rows_of(v, rs):
        return v[0][rs], v[1][rs]

    def column(base, e, rs):
        out = rows_of(base, rs)
        for bit, val in pows.items():
            if e & bit:
                out = _cmul(*out, *rows_of(val, rs))
        return out

    def times_b(p, rs):
        return p[0] * bb_same[rs] + p[1] * sbb_swap[rs]

    def times_c(p, rs):
        return ca[rs] * p[0] + cb[rs] * p[1]

    i8 = lax.broadcasted_iota(jnp.int32, (rows, LANES), 1) // SSM_GROUP
    asc0 = _cpow(*lam, i8, step_bits)
    desc0 = _cpow(*lam, STEPS_PER_VREG - 1 - i8, step_bits)
    asc1_0 = _cmul(*asc0, *lam)
    desc1_0 = _cmul(*desc0, *lam)

    wend_f, wend_b, lag_b, wout_f, wout_b = [], [], [], [], []
    for q in range(n_cols):
        e_asc = STEPS_PER_VREG * q
        e_desc = STEPS_PER_VREG * (n_cols - 1 - q)
        asc1_q = column(asc1_0, e_asc, every)
        wend_f.append(times_b(column(desc0, e_desc, fwd), fwd))
        wend_b.append(times_b(column(asc0, e_asc, bwd), bwd))
        lag_b.append(times_b(rows_of(asc1_q, bwd), bwd))
        wout_f.append(times_c(rows_of(asc1_q, fwd), fwd))
        wout_b.append(times_c(column(desc1_0, e_desc, bwd), bwd))
        cols = slice(q * LANES, (q + 1) * LANES)
        wend_ref[gi, fwd, cols] = wend_f[q].astype(BF16)
        wend_ref[gi, bwd, cols] = wend_b[q].astype(BF16)

    meta_cols = N_META * SSM_GROUP // LANES
    x0 = sum(jnp.sum(wend_f[n_cols - meta_cols + m] * um_ref[gi, :, m * LANES:(m + 1) * LANES],
                     axis=1, keepdims=True) for m in range(meta_cols))

    lane = lax.broadcasted_iota(jnp.int32, (DIR_ROWS, LANES), 1)
    lag0_b = jnp.where(lane >= LANES - SSM_GROUP, bb_same[bwd], 0.0)
    zero = jnp.zeros((DIR_ROWS, LANES), F32)
    bcat = jnp.concatenate(
        [jnp.concatenate(wend_f + [zero] * n_cols, axis=1),
         jnp.concatenate([zero] * (n_cols - 1) + [lag0_b] + lag_b, axis=1)], axis=0)
    lane_r = lax.broadcasted_iota(jnp.int32, (SSM_GROUP, rows), 1)
    c2 = jnp.where((lane_r // SSM_STATE) % 2 == 0, cr_ref[gi], -cr_ref[gi])
    kk = jnp.dot(c2, bcat, precision=lax.Precision.HIGHEST,
                 preferred_element_type=F32)
    for tt in range(t):
        off = (t - 1 - tt) * SSM_GROUP
        toep_ref[gi, tt * SSM_GROUP:(tt + 1) * SSM_GROUP, :] = (
            kk[:, off:off + CHUNK_LANES].astype(BF16))

    wout = jnp.concatenate([jnp.concatenate(wout_f, axis=1),
                            jnp.concatenate(wout_b, axis=1)], axis=0)
    wout_ref[gi] = wout.T.astype(BF16)
    return pows[t], x0


SSM_GROUPS_PER_STEP = 2


def _chunk_scan(er, ei, ar, ai, x0, chunk, n_chunks, forward):
    width = er.shape[1]
    xr, xi = er, ei
    if x0 is not None:
        fr, fi = _cmul(ar, ai, x0[0], x0[1])
        xr = xr + jnp.where(chunk == 0, _lanes(fr, width), 0.0)
        xi = xi + jnp.where(chunk == 0, _lanes(fi, width), 0.0)

    def shifted(v, sh):
        if forward:
            return jnp.where(chunk >= sh, pltpu.roll(v, sh, axis=1), 0.0)
        return jnp.where(chunk < n_chunks - sh, pltpu.roll(v, width - sh, axis=1), 0.0)

    sh = 1
    while sh < n_chunks:
        sr, si = shifted(xr, sh), shifted(xi, sh)
        wr, wi = _lanes(ar, width), _lanes(ai, width)
        xr, xi = xr + wr * sr - wi * si, xi + wr * si + wi * sr
        ar, ai = _cmul(ar, ai, ar, ai)
        sh *= 2
    inr, ini = shifted(xr, 1), shifted(xi, 1)
    if x0 is not None:
        inr = jnp.where(chunk == 0, x0[0], inr)
        ini = jnp.where(chunk == 0, x0[1], ini)
    return inr, ini


def _ssm_kernel(u_ref, um_ref, csc_ref, cc_ref, bc_ref, cr_ref, y_ref, toep_ref, wend_ref,
                wout_ref, *, n_chunks):
    for gi in range(SSM_GROUPS_PER_STEP):
        (lam_r, lam_i), x0 = _build_chunk_ops(gi, csc_ref, cc_ref, bc_ref, cr_ref, um_ref,
                                              toep_ref, wend_ref, wout_ref)
        u = u_ref[gi]
        width = u.shape[1]
        e = jnp.dot(wend_ref[gi], u, preferred_element_type=F32)
        chunk = lax.broadcasted_iota(jnp.int32, (1, width), 1) % n_chunks
        p = SSM_STATE
        ar, ai = lam_r, lam_i
        f_in = _chunk_scan(e[0:p], e[p:2 * p], ar[0:p], ai[0:p], (x0[0:p], x0[p:2 * p]),
                           chunk, n_chunks, True)
        b_in = _chunk_scan(e[2 * p:3 * p], e[3 * p:], ar[2 * p:3 * p], ai[2 * p:3 * p],
                           None, chunk, n_chunks, False)
        xin = jnp.concatenate([f_in[0], f_in[1], b_in[0], b_in[1]], axis=0).astype(BF16)
        y = jnp.dot(toep_ref[gi], u, preferred_element_type=F32)
        y = y + jnp.dot(wout_ref[gi], xin, preferred_element_type=F32)
        y_ref[gi] = y.astype(BF16)


def _ssm(u_col, u_meta, csc, c_col, b_col, c_row, n_chunks):
    g, _, width = u_col.shape
    per = SSM_GROUPS_PER_STEP
    return pl.pallas_call(
        functools.partial(_ssm_kernel, n_chunks=n_chunks),
        grid=(g // per,),
        in_specs=[
            pl.BlockSpec((per, CHUNK_LANES, width), lambda i: (i, 0, 0)),
            pl.BlockSpec((per, 1, N_META * SSM_GROUP), lambda i: (i, 0, 0)),
            pl.BlockSpec((per, STATE_ROWS, 4), lambda i: (i, 0, 0)),
            pl.BlockSpec((per, STATE_ROWS, SSM_GROUP), lambda i: (i, 0, 0)),
            pl.BlockSpec((per, STATE_ROWS, SSM_GROUP), lambda i: (i, 0, 0)),
            pl.BlockSpec((per, SSM_GROUP, STATE_ROWS), lambda i: (i, 0, 0)),
        ],
        out_specs=pl.BlockSpec((per, CHUNK_LANES, width), lambda i: (i, 0, 0)),
        out_shape=jax.ShapeDtypeStruct((g, CHUNK_LANES, width), BF16),
        scratch_shapes=[
            pltpu.VMEM((per, CHUNK_LANES, CHUNK_LANES), BF16),
            pltpu.VMEM((per, STATE_ROWS, CHUNK_LANES), BF16),
            pltpu.VMEM((per, CHUNK_LANES, STATE_ROWS), BF16),
        ],
        compiler_params=_params(),
        name="s5_mixer",
    )(u_col, u_meta, csc, c_col, b_col, c_row)


FF_CHUNK = 1024


MIX_SLOTS = 3


def _mix_ffn_kernel(x_hbm, co_ref, y_ref, ut_ref, cb_ref, lg_ref, lb_ref, d_ref, gwt_ref,
                    gb_ref, wo_ref, gf_ref, w1_ref, w2_ref, gl_ref, o_hbm, buf, sem_in,
                    sem_out, mixed_ref, h_ref, z_ref, a_ref):
    n_chunks = x_hbm.shape[1]
    rows = SUBLANES * n_chunks
    step = pl.program_id(0)
    n_tiles = pl.num_programs(0) - 1
    tile = step - 1

    def mix_stages(dst):
        n_half = 2
        per = SUBLANES // n_half
        state = {}

        def col(ref, k):
            return jnp.concatenate(
                [ref[:, r * SSM_GROUP:(r + 1) * SSM_GROUP, :].reshape(D_SSM, n_chunks)
                 for r in range(k * per, (k + 1) * per)], axis=1).astype(F32)

        def gelu_stage(k):
            def run():
                state["ge", k] = jax.nn.gelu(col(y_ref, k) + d_ref[...] * col(ut_ref, k))
            return run

        def glu_stage(k):
            def run():
                ge = state["ge", k]
                gate = (jnp.dot(gwt_ref[...], ge.astype(BF16), preferred_element_type=F32)
                        + gb_ref[...])
                state["s5t", k] = ge * jax.nn.sigmoid(gate)
            return run

        def store_stage(k):
            def run():
                rs = slice(k * per * n_chunks, (k + 1) * per * n_chunks)
                y = (co_ref[k * per:(k + 1) * per].reshape(per * n_chunks, D_CONV).astype(F32)
                     + cb_ref[...])
                yc = y - jnp.mean(y, axis=-1, keepdims=True)
                yn = yc * lax.rsqrt(jnp.mean(yc * yc, axis=-1, keepdims=True) + NORM_EPS)
                yn = yn * lg_ref[...] + lb_ref[...]
                mixed_ref[dst, rs, :D_CONV] = (yn * jax.nn.sigmoid(yn)).astype(BF16)
                mixed_ref[dst, rs, D_CONV:] = state["s5t", k].T.astype(BF16)
            return run

        return [f(k) for f in (gelu_stage, glu_stage, store_stage) for k in range(n_half)]

    @pl.when(step == 0)
    def _():
        for cp in _step_copies(x_hbm, buf, sem_in, 0, 0, False):
            cp.start()
        for stage in mix_stages(0):
            stage()

    @pl.when(step >= 1)
    def _():
        slot = tile % MIX_SLOTS
        nxt = (tile + 1) % MIX_SLOTS

        @pl.when(tile >= 2)
        def _():
            for cp in _step_copies(o_hbm, buf, sem_out, tile - 2, nxt, True):
                cp.wait()

        @pl.when(tile + 1 < n_tiles)
        def _():
            for cp in _step_copies(x_hbm, buf, sem_in, tile + 1, nxt, False):
                cp.start()

        for cp in _step_copies(x_hbm, buf, sem_in, tile, slot, False):
            cp.wait()

        stages = mix_stages(step % 2)
        h = buf[slot].reshape(rows, D_MODEL) + jnp.dot(
            mixed_ref[tile % 2], wo_ref[...], preferred_element_type=F32)
        z_ref[...] = _rms(h, gf_ref[...]).astype(BF16)
        h_ref[...] = h
        half = D_FF // 2
        for k in range(2):
            for j in range(half // FF_CHUNK):
                cols = slice(k * half + j * FF_CHUNK, k * half + (j + 1) * FF_CHUNK)
                a = jnp.dot(z_ref[...], w1_ref[:, cols], preferred_element_type=F32)
                a_ref[:, j * FF_CHUNK:(j + 1) * FF_CHUNK] = (
                    jnp.square(jnp.maximum(a, 0.0)).astype(BF16))
                stages.pop(0)()
            if k == 0:
                h_ref[...] += jnp.dot(a_ref[...], w2_ref[:half, :],
                                      preferred_element_type=F32)
                stages.pop(0)()
        for r in range(2):
            rs = slice(r * rows // 2, (r + 1) * rows // 2)
            hr = h_ref[rs] + jnp.dot(a_ref[rs], w2_ref[half:, :], preferred_element_type=F32)
            buf[slot, r * SUBLANES // 2:(r + 1) * SUBLANES // 2] = _rms(
                hr, gl_ref[...]).reshape(SUBLANES // 2, n_chunks, D_MODEL)
            if r == 0:
                stages.pop(0)()
        assert not stages

        for cp in _step_copies(o_hbm, buf, sem_out, tile, slot, True):
            cp.start()

        @pl.when(tile == n_tiles - 1)
        def _():
            for cp in (_step_copies(o_hbm, buf, sem_out, tile - 1, (tile - 1) % MIX_SLOTS, True)
                       + _step_copies(o_hbm, buf, sem_out, tile, slot, True)):
                cp.wait()


def _mix_ffn(x4, conv_raw, y_col, u_col, conv_b, ln_g, ln_b, d_col, glu_wt, glu_b_col, w_out,
             g_ffn, w1, w2, g_final):
    bsz, n_chunks = x4.shape[:2]

    def const(shape):
        return pl.BlockSpec(shape, lambda t: (0, 0), pipeline_mode=pl.Buffered(1))

    n_tiles = CHUNK // SUBLANES * bsz
    hbm = pl.BlockSpec(memory_space=pl.ANY)

    def slab(s):
        return jnp.minimum(s, n_tiles - 1) // bsz

    def batch(s):
        return jnp.minimum(s, n_tiles - 1) % bsz

    col_blk = pl.BlockSpec((N_GROUPS, SUBLANES * SSM_GROUP, n_chunks),
                           lambda s: (0, slab(s), batch(s)))
    step_sems = pltpu.SemaphoreType.DMA((MIX_SLOTS, SUBLANES))
    return pl.pallas_call(
        _mix_ffn_kernel,
        grid=(n_tiles + 1,),
        in_specs=[
            hbm,
            pl.BlockSpec((SUBLANES, n_chunks, D_CONV), lambda s: (slab(s), batch(s), 0)),
            col_blk,
            col_blk,
            const((1, D_CONV)),
            const((1, D_CONV)),
            const((1, D_CONV)),
            const((D_SSM, 1)),
            const((D_SSM, D_SSM)),
            const((D_SSM, 1)),
            const((D_MODEL, D_MODEL)),
            const((1, D_MODEL)),
            const((D_MODEL, D_FF)),
            const((D_FF, D_MODEL)),
            const((1, D_MODEL)),
        ],
        out_specs=hbm,
        out_shape=jax.ShapeDtypeStruct(x4.shape, F32),
        scratch_shapes=[
            pltpu.VMEM((MIX_SLOTS, SUBLANES, n_chunks, D_MODEL), F32),
            step_sems, step_sems,
            pltpu.VMEM((2, SUBLANES * n_chunks, D_MODEL), BF16),
            pltpu.VMEM((SUBLANES * n_chunks, D_MODEL), F32),
            pltpu.VMEM((SUBLANES * n_chunks, D_MODEL), BF16),
            pltpu.VMEM((SUBLANES * n_chunks, D_FF // 2), BF16),
        ],
        compiler_params=_step_tile_params(),
        name="mix_ffn",
    )(x4, conv_raw, y_col, u_col, conv_b, ln_g, ln_b, d_col, glu_wt, glu_b_col, w_out, g_ffn,
      w1, w2, g_final)


def _s5_operator_inputs(lbr, lbi, zr, zi, b_re, b_im, c_re, c_im):
    g, p, h = N_GROUPS, SSM_STATE, SSM_GROUP

    sc = jnp.transpose(jnp.stack([lbr, lbi, zr, zi], axis=-1), (1, 0, 2, 3))
    csc = jnp.broadcast_to(sc[:, :, None], (g, 2, 2, p, 4)).reshape(g, 4 * p, 4)
    c = jnp.stack([c_re, c_im], axis=0)
    b = jnp.stack([b_re, b_im], axis=0)
    c_col = jnp.transpose(c, (2, 1, 0, 4, 3)).reshape(g, 4 * p, h)
    b_col = jnp.transpose(b, (2, 1, 0, 3, 4)).reshape(g, 4 * p, h)
    c_row = jnp.transpose(c, (2, 3, 1, 0, 4)).reshape(g, h, 4 * p)
    return csc, c_col, b_col, c_row


def kernel(x, meta_tokens, norm_mix_g, w_in, conv_w, conv_b, conv_ln_g, conv_ln_b,
           ssm_lam_re, ssm_lam_im, ssm_log_dt, ssm_b_re, ssm_b_im, ssm_c_re, ssm_c_im,
           ssm_d, ssm_glu_w, ssm_glu_b, w_out, norm_ffn_g, w_ff1, w_ff2, norm_final_g):
    assert w_in.shape[0] == 1, "single-layer block"
    bsz, seq, _ = x.shape
    assert seq % CHUNK == 0 and CHUNK >= N_META
    n_chunks = seq // CHUNK
    g, h = N_GROUPS, SSM_GROUP

    x4 = x.reshape(bsz, n_chunks, CHUNK, D_MODEL)
    g_mix = norm_mix_g[0][None, :]
    w_in_b = w_in[0].astype(BF16)
    u_conv, u_col = _in_proj_step(x4, g_mix, w_in_b)
    u_conv_m, u_ssm_m = _in_proj_meta(meta_tokens, g_mix, w_in_b)

    conv_raw = _conv_taps(u_conv, u_conv_m, conv_w[0], bsz)

    ldt = jnp.broadcast_to(ssm_log_dt[0][..., None], ssm_lam_re[0].shape)
    flat = lambda a: a.reshape(2 * g, SSM_STATE)
    lbr, lbi, zr, zi = [a.reshape(2, g, SSM_STATE) for a in
                        _zoh(flat(ssm_lam_re[0]), flat(ssm_lam_im[0]), flat(ldt))]
    csc, c_col, b_col, c_row = _s5_operator_inputs(
        lbr, lbi, zr, zi, ssm_b_re[0], ssm_b_im[0], ssm_c_re[0], ssm_c_im[0])

    u_meta = jnp.transpose(u_ssm_m.reshape(N_META, g, h), (1, 0, 2)).reshape(g, 1, N_META * h)
    y_col = _ssm(u_col, u_meta, csc, c_col, b_col, c_row, n_chunks)

    out = _mix_ffn(x4, conv_raw, y_col, u_col, conv_b[0][None, :], conv_ln_g[0][None, :],
                   conv_ln_b[0][None, :], ssm_d[0][:, None],
                   ssm_glu_w[0].T.astype(BF16), ssm_glu_b[0][:, None],
                   w_out[0].astype(BF16), norm_ffn_g[0][None, :], w_ff1[0].astype(BF16),
                   w_ff2[0].astype(BF16), norm_final_g[None, :])
    return out.reshape(bsz, seq, D_MODEL)
```

```python
import functools

import jax
import jax.numpy as jnp
from jax import lax
from jax.experimental import pallas as pl
from jax.experimental.pallas import tpu as pltpu

F32 = jnp.float32
BF16 = jnp.bfloat16

D_MODEL = 1024
N_META = 16
D_CONV = 512
D_SSM = 512
CONV_WIDTH = 31
CONV_PAD = CONV_WIDTH // 2
SSM_GROUP = 16
N_GROUPS = D_SSM // SSM_GROUP
SSM_STATE = 64
D_FF = 4096
NORM_EPS = 1e-5
LANES = 128
SUBLANES = 8

CHUNK = 32
CHUNK_LANES = CHUNK * SSM_GROUP
STATE_ROWS = 4 * SSM_STATE
DIR_ROWS = 2 * SSM_STATE
STEPS_PER_VREG = LANES // SSM_GROUP

VMEM_LIMIT_BYTES = 60 * 1024 * 1024


def _rms(x, g):
    return x * lax.rsqrt(jnp.mean(x * x, axis=-1, keepdims=True) + NORM_EPS) * g


def _params(n_axes=1):
    return pltpu.CompilerParams(dimension_semantics=("parallel",) * n_axes,
                                vmem_limit_bytes=VMEM_LIMIT_BYTES)


def _step_copies(hbm_ref, buf_ref, sem_ref, tile, slot, to_hbm):
    bsz = hbm_ref.shape[0]
    slab, b = tile // bsz, tile % bsz
    copies = []
    for r in range(SUBLANES):
        hbm = hbm_ref.at[b, :, slab * SUBLANES + r, :]
        vmem = buf_ref.at[slot, r]
        src, dst = (vmem, hbm) if to_hbm else (hbm, vmem)
        copies.append(pltpu.make_async_copy(src, dst, sem_ref.at[slot, r]))
    return copies


def _fetch_steps(x_hbm, xbuf, sem):
    tile = pl.program_id(0)
    slot = tile % 2

    @pl.when(tile == 0)
    def _():
        for cp in _step_copies(x_hbm, xbuf, sem, tile, slot, False):
            cp.start()

    @pl.when(tile + 1 < pl.num_programs(0))
    def _():
        for cp in _step_copies(x_hbm, xbuf, sem, tile + 1, 1 - slot, False):
            cp.start()

    for cp in _step_copies(x_hbm, xbuf, sem, tile, slot, False):
        cp.wait()
    return slot


def _step_tile_params():
    return pltpu.CompilerParams(dimension_semantics=("arbitrary",),
                                vmem_limit_bytes=VMEM_LIMIT_BYTES)


def _in_proj_step_kernel(x_hbm, g_ref, w_ref, uc_ref, ut_ref, xbuf, sem):
    n_chunks = x_hbm.shape[1]
    slot = _fetch_steps(x_hbm, xbuf, sem)
    x = xbuf[slot].reshape(SUBLANES * n_chunks, D_MODEL)
    z = _rms(x, g_ref[...]).astype(BF16)
    p = jnp.dot(z, w_ref[...], preferred_element_type=F32)
    uc = p[:, :D_CONV] * jax.nn.sigmoid(p[:, D_CONV:2 * D_CONV])
    uc_ref[...] = uc.reshape(SUBLANES, n_chunks, D_CONV)
    ut = p[:, 2 * D_CONV:].T.astype(BF16)
    for r in range(SUBLANES):
        ut_ref[:, r * SSM_GROUP:(r + 1) * SSM_GROUP, :] = (
            ut[:, r * n_chunks:(r + 1) * n_chunks].reshape(N_GROUPS, SSM_GROUP, n_chunks))


def _in_proj_step(x4, g, w_in):
    bsz, n_chunks = x4.shape[:2]
    rows = bsz * n_chunks
    return pl.pallas_call(
        _in_proj_step_kernel,
        grid=(CHUNK // SUBLANES * bsz,),
        in_specs=[
            pl.BlockSpec(memory_space=pl.ANY),
            pl.BlockSpec((1, D_MODEL), lambda t: (0, 0)),
            pl.BlockSpec((D_MODEL, 2 * D_CONV + D_SSM), lambda t: (0, 0)),
        ],
        out_specs=[
            pl.BlockSpec((SUBLANES, n_chunks, D_CONV), lambda t: (t // bsz, t % bsz, 0)),
            pl.BlockSpec((N_GROUPS, SUBLANES * SSM_GROUP, n_chunks),
                         lambda t: (0, t // bsz, t % bsz)),
        ],
        out_shape=[
            jax.ShapeDtypeStruct((CHUNK, rows, D_CONV), F32),
            jax.ShapeDtypeStruct((N_GROUPS, CHUNK_LANES, rows), BF16),
        ],
        scratch_shapes=[pltpu.VMEM((2, SUBLANES, n_chunks, D_MODEL), F32),
                        pltpu.SemaphoreType.DMA((2, SUBLANES))],
        compiler_params=_step_tile_params(),
        name="in_proj",
    )(x4, g, w_in)


def _in_proj_meta_kernel(x_ref, g_ref, w_ref, uc_ref, us_ref):
    z = _rms(x_ref[...], g_ref[...]).astype(BF16)
    p = jnp.dot(z, w_ref[...], preferred_element_type=F32)
    uc_ref[...] = p[:, :D_CONV] * jax.nn.sigmoid(p[:, D_CONV:2 * D_CONV])
    us_ref[...] = p[:, 2 * D_CONV:]


def _in_proj_meta(meta, g, w_in):
    return pl.pallas_call(
        _in_proj_meta_kernel,
        out_shape=[jax.ShapeDtypeStruct((N_META, D_CONV), F32),
                   jax.ShapeDtypeStruct((N_META, D_SSM), F32)],
        name="in_proj_meta",
    )(meta, g, w_in)


CONV_GROUP = 4
CONV_ROWS = 64


def _conv_kernel(u_ref, um_ref, w_ref, cb_ref, lg_ref, lb_ref, o_ref, cat_ref, acc_ref):
    n_steps, n_chunks, _ = u_ref.shape
    cat_ref[CONV_PAD:CONV_PAD + n_steps] = u_ref[...]
    chunk = lax.broadcasted_iota(jnp.int32, (n_chunks, 1), 0)
    for i in range(CONV_PAD):
        s_prev = n_steps - CONV_PAD + i
        prev = pltpu.roll(u_ref[s_prev], 1, axis=0)
        meta_row = um_ref[N_META - CONV_PAD + i:N_META - CONV_PAD + i + 1, :]
        cat_ref[i] = jnp.where(chunk == 0, meta_row, prev)
        nxt = pltpu.roll(u_ref[i], n_chunks - 1, axis=0)
        cat_ref[n_steps + CONV_PAD + i] = jnp.where(chunk == n_chunks - 1, 0.0, nxt)

    def body(sg, carry):
        s0 = sg * CONV_GROUP
        for c in range(D_CONV // LANES):
            cols = slice(c * LANES, (c + 1) * LANES)
            n_sub = CONV_ROWS // SUBLANES

            def rows_body(rt, carry2):
                r0 = [pl.multiple_of(rt * CONV_ROWS + sub * SUBLANES, SUBLANES)
                      for sub in range(n_sub)]
                accs = [[jnp.zeros((SUBLANES, LANES), F32)] * CONV_GROUP for _ in range(n_sub)]
                for i in range(CONV_WIDTH + CONV_GROUP - 1):
                    ds = [cat_ref[s0 + i, pl.ds(r0[sub], SUBLANES), cols]
                          for sub in range(n_sub)]
                    for j in range(CONV_GROUP):
                        if 0 <= i - j < CONV_WIDTH:
                            tap = jnp.broadcast_to(w_ref[i - j:i - j + 1, cols],
                                                   (SUBLANES, LANES))
                            for sub in range(n_sub):
                                accs[sub][j] = accs[sub][j] + ds[sub] * tap
                for sub in range(n_sub):
                    for j in range(CONV_GROUP):
                        acc_ref[j, pl.ds(r0[sub], SUBLANES), cols] = accs[sub][j]
                return carry2

            lax.fori_loop(0, n_chunks // CONV_ROWS, rows_body, 0)
        for j in range(CONV_GROUP):
            y = acc_ref[j] + cb_ref[...]
            yc = y - jnp.mean(y, axis=-1, keepdims=True)
            yn = yc * lax.rsqrt(jnp.mean(yc * yc, axis=-1, keepdims=True) + NORM_EPS)
            yn = yn * lg_ref[...] + lb_ref[...]
            o_ref[s0 + j] = (yn * jax.nn.sigmoid(yn)).astype(BF16)
        return carry

    lax.fori_loop(0, n_steps // CONV_GROUP, body, 0)


def _conv_module(u_conv, u_conv_meta, conv_w, conv_b, ln_g, ln_b, bsz):
    n_steps, rows, _ = u_conv.shape
    n_chunks = rows // bsz
    vec = pl.BlockSpec((1, D_CONV), lambda b: (0, 0))
    blk = pl.BlockSpec((n_steps, n_chunks, D_CONV), lambda b: (0, b, 0))
    return pl.pallas_call(
        _conv_kernel,
        grid=(bsz,),
        in_specs=[
            blk,
            pl.BlockSpec((N_META, D_CONV), lambda b: (0, 0)),
            pl.BlockSpec((CONV_WIDTH, D_CONV), lambda b: (0, 0)),
            vec, vec, vec,
        ],
        out_specs=blk,
        out_shape=jax.ShapeDtypeStruct((n_steps, rows, D_CONV), BF16),
        scratch_shapes=[pltpu.VMEM((n_steps + 2 * CONV_PAD, n_chunks, D_CONV), F32),
                        pltpu.VMEM((CONV_GROUP, n_chunks, D_CONV), F32)],
        compiler_params=_params(),
        name="conv_module",
    )(u_conv, u_conv_meta, conv_w, conv_b, ln_g, ln_b)


def _zoh_kernel(lre_ref, lim_ref, ldt_ref, lbr_ref, lbi_ref, zr_ref, zi_ref):
    lre = lre_ref[...]
    lim = lim_ref[...]
    dt = jnp.exp(ldt_ref[...])
    ea = jnp.exp(lre * dt)
    lbr = ea * jnp.cos(lim * dt)
    lbi = ea * jnp.sin(lim * dt)
    nr = lbr - 1.0
    den = lre * lre + lim * lim
    lbr_ref[...] = lbr
    lbi_ref[...] = lbi
    zr_ref[...] = (nr * lre + lbi * lim) / den
    zi_ref[...] = (lbi * lre - nr * lim) / den


def _zoh(lam_re, lam_im, log_dt):
    shape = jax.ShapeDtypeStruct(lam_re.shape, F32)
    return pl.pallas_call(_zoh_kernel, out_shape=[shape] * 4, name="s5_zoh")(
        lam_re, lam_im, log_dt)


def _lanes(x, n):
    return jnp.concatenate([x] * (n // LANES), axis=1)


def _cmul(ar, ai, br, bi):
    return ar * br - ai * bi, ar * bi + ai * br


def _cpow(br, bi, expo, nbits):
    rr = jnp.ones(expo.shape, F32)
    ri = jnp.zeros(expo.shape, F32)
    for k in range(nbits):
        bit = ((expo >> k) & 1) == 1
        nr, ni = _cmul(rr, ri, br, bi)
        rr = jnp.where(bit, nr, rr)
        ri = jnp.where(bit, ni, ri)
        br, bi = _cmul(br, bi, br, bi)
    return rr, ri


def _build_chunk_ops(gi, csc_ref, cc_ref, bc_ref, cr_ref, um_ref, toep_ref, wend_ref,
                     wout_ref):
    t = CHUNK
    rows = STATE_ROWS
    n_cols = CHUNK_LANES // LANES
    lam = (jnp.broadcast_to(csc_ref[gi, :, 0:1], (rows, LANES)),
           jnp.broadcast_to(csc_ref[gi, :, 1:2], (rows, LANES)))
    zr = jnp.broadcast_to(csc_ref[gi, :, 2:3], (rows, LANES))
    zi = jnp.broadcast_to(csc_ref[gi, :, 3:4], (rows, LANES))
    sel = (lax.broadcasted_iota(jnp.int32, (SSM_GROUP, LANES), 1) % SSM_GROUP
           == lax.broadcasted_iota(jnp.int32, (SSM_GROUP, LANES), 0)).astype(F32)

    def tile(ref):
        return jnp.dot(ref[gi], sel, precision=lax.Precision.HIGHEST,
                       preferred_element_type=F32)

    def swap_re_im(a):
        p = SSM_STATE
        return jnp.concatenate([a[p:2 * p], a[0:p], a[3 * p:], a[2 * p:3 * p]], axis=0)

    c_same, b_same = tile(cc_ref), tile(bc_ref)
    c_swap, b_swap = swap_re_im(c_same), swap_re_im(b_same)
    row = lax.broadcasted_iota(jnp.int32, (rows, LANES), 0)
    is_re = (row // SSM_STATE) % 2 == 0
    sgn = jnp.where(is_re, -1.0, 1.0)
    ca = jnp.where(is_re, c_same, -c_same)
    cb = -c_swap
    bb_same = zr * b_same + sgn * zi * b_swap
    sbb_swap = sgn * (zr * b_swap - sgn * zi * b_same)

    pows = {1: lam}
    k = 1
    while k < t:
        pows[2 * k] = _cmul(*pows[k], *pows[k])
        k *= 2
    step_bits = STEPS_PER_VREG.bit_length() - 1
    fwd = slice(0, DIR_ROWS)
    bwd = slice(DIR_ROWS, rows)
    every = slice(0, rows)

    def rows_of(v, rs):
        return v[0][rs], v[1][rs]

    def column(base, e, rs):
        out = rows_of(base, rs)
        for bit, val in pows.items():
            if e & bit:
                out = _cmul(*out, *rows_of(val, rs))
        return out

    def times_b(p, rs):
        return p[0] * bb_same[rs] + p[1] * sbb_swap[rs]

    def times_c(p, rs):
        return ca[rs] * p[0] + cb[rs] * p[1]

    i8 = lax.broadcasted_iota(jnp.int32, (rows, LANES), 1) // SSM_GROUP
    asc0 = _cpow(*lam, i8, step_bits)
    desc0 = _cpow(*lam, STEPS_PER_VREG - 1 - i8, step_bits)
    asc1_0 = _cmul(*asc0, *lam)
    desc1_0 = _cmul(*desc0, *lam)

    wend_f, wend_b, lag_b, wout_f, wout_b = [], [], [], [], []
    for q in range(n_cols):
        e_asc = STEPS_PER_VREG * q
        e_desc = STEPS_PER_VREG * (n_cols - 1 - q)
        asc1_q = column(asc1_0, e_asc, every)
        wend_f.append(times_b(column(desc0, e_desc, fwd), fwd))
        wend_b.append(times_b(column(asc0, e_asc, bwd), bwd))
        lag_b.append(times_b(rows_of(asc1_q, bwd), bwd))
        wout_f.append(times_c(rows_of(asc1_q, fwd), fwd))
        wout_b.append(times_c(column(desc1_0, e_desc, bwd), bwd))
        cols = slice(q * LANES, (q + 1) * LANES)
        wend_ref[gi, fwd, cols] = wend_f[q].astype(BF16)
        wend_ref[gi, bwd, cols] = wend_b[q].astype(BF16)

    meta_cols = N_META * SSM_GROUP // LANES
    x0 = sum(jnp.sum(wend_f[n_cols - meta_cols + m] * um_ref[gi, :, m * LANES:(m + 1) * LANES],
                     axis=1, keepdims=True) for m in range(meta_cols))

    lane = lax.broadcasted_iota(jnp.int32, (DIR_ROWS, LANES), 1)
    lag0_b = jnp.where(lane >= LANES - SSM_GROUP, bb_same[bwd], 0.0)
    zero = jnp.zeros((DIR_ROWS, LANES), F32)
    bcat = jnp.concatenate(
        [jnp.concatenate(wend_f + [zero] * n_cols, axis=1),
         jnp.concatenate([zero] * (n_cols - 1) + [lag0_b] + lag_b, axis=1)], axis=0)
    lane_r = lax.broadcasted_iota(jnp.int32, (SSM_GROUP, rows), 1)
    c2 = jnp.where((lane_r // SSM_STATE) % 2 == 0, cr_ref[gi], -cr_ref[gi])
    kk = jnp.dot(c2, bcat, precision=lax.Precision.HIGHEST,
                 preferred_element_type=F32)
    for tt in range(t):
        off = (t - 1 - tt) * SSM_GROUP
        toep_ref[gi, tt * SSM_GROUP:(tt + 1) * SSM_GROUP, :] = (
            kk[:, off:off + CHUNK_LANES].astype(BF16))

    wout = jnp.concatenate([jnp.concatenate(wout_f, axis=1),
                            jnp.concatenate(wout_b, axis=1)], axis=0)
    wout_ref[gi] = wout.T.astype(BF16)
    return pows[t], x0


SSM_GROUPS_PER_STEP = 2


def _chunk_scan(er, ei, ar, ai, x0, chunk, n_chunks, forward):
    width = er.shape[1]
    xr, xi = er, ei
    if x0 is not None:
        fr, fi = _cmul(ar, ai, x0[0], x0[1])
        xr = xr + jnp.where(chunk == 0, _lanes(fr, width), 0.0)
        xi = xi + jnp.where(chunk == 0, _lanes(fi, width), 0.0)

    def shifted(v, sh):
        if forward:
            return jnp.where(chunk >= sh, pltpu.roll(v, sh, axis=1), 0.0)
        return jnp.where(chunk < n_chunks - sh, pltpu.roll(v, width - sh, axis=1), 0.0)

    sh = 1
    while sh < n_chunks:
        sr, si = shifted(xr, sh), shifted(xi, sh)
        wr, wi = _lanes(ar, width), _lanes(ai, width)
        xr, xi = xr + wr * sr - wi * si, xi + wr * si + wi * sr
        ar, ai = _cmul(ar, ai, ar, ai)
        sh *= 2
    inr, ini = shifted(xr, 1), shifted(xi, 1)
    if x0 is not None:
        inr = jnp.where(chunk == 0, x0[0], inr)
        ini = jnp.where(chunk == 0, x0[1], ini)
    return inr, ini


def _ssm_kernel(u_ref, um_ref, csc_ref, cc_ref, bc_ref, cr_ref, y_ref, toep_ref, wend_ref,
                wout_ref, yi_ref, *, n_chunks):
    for gi in range(SSM_GROUPS_PER_STEP):
        (lam_r, lam_i), x0 = _build_chunk_ops(gi, csc_ref, cc_ref, bc_ref, cr_ref, um_ref,
                                              toep_ref, wend_ref, wout_ref)
        u = u_ref[gi]
        width = u.shape[1]
        e = jnp.dot(wend_ref[gi], u, preferred_element_type=F32)
        yi_ref[gi] = jnp.dot(toep_ref[gi], u, preferred_element_type=F32)
        chunk =lax.broadcasted_iota(jnp.int32, (1, width), 1) % n_chunks
        p = SSM_STATE
        ar, ai = lam_r, lam_i
        f_in = _chunk_scan(e[0:p], e[p:2 * p], ar[0:p], ai[0:p], (x0[0:p], x0[p:2 * p]),
                           chunk, n_chunks, True)
        b_in = _chunk_scan(e[2 * p:3 * p], e[3 * p:], ar[2 * p:3 * p], ai[2 * p:3 * p],
                           None, chunk, n_chunks, False)
        xin = jnp.concatenate([f_in[0], f_in[1], b_in[0], b_in[1]], axis=0).astype(BF16)
        y = yi_ref[gi] + jnp.dot(wout_ref[gi], xin, preferred_element_type=F32)
        y_ref[gi] = y.astype(BF16)


def _ssm(u_col, u_meta, csc, c_col, b_col, c_row, n_chunks):
    g, _, width = u_col.shape
    per = SSM_GROUPS_PER_STEP
    return pl.pallas_call(
        functools.partial(_ssm_kernel, n_chunks=n_chunks),
        grid=(g // per,),
        in_specs=[
            pl.BlockSpec((per, CHUNK_LANES, width), lambda i: (i, 0, 0)),
            pl.BlockSpec((per, 1, N_META * SSM_GROUP), lambda i: (i, 0, 0)),
            pl.BlockSpec((per, STATE_ROWS, 4), lambda i: (i, 0, 0)),
            pl.BlockSpec((per, STATE_ROWS, SSM_GROUP), lambda i: (i, 0, 0)),
            pl.BlockSpec((per, STATE_ROWS, SSM_GROUP), lambda i: (i, 0, 0)),
            pl.BlockSpec((per, SSM_GROUP, STATE_ROWS), lambda i: (i, 0, 0)),
        ],
        out_specs=pl.BlockSpec((per, CHUNK_LANES, width), lambda i: (i, 0, 0)),
        out_shape=jax.ShapeDtypeStruct((g, CHUNK_LANES, width), BF16),
        scratch_shapes=[
            pltpu.VMEM((per, CHUNK_LANES, CHUNK_LANES), BF16),
            pltpu.VMEM((per, STATE_ROWS, CHUNK_LANES), BF16),
            pltpu.VMEM((per, CHUNK_LANES, STATE_ROWS), BF16),
            pltpu.VMEM((per, CHUNK_LANES, width), F32),
        ],
        compiler_params=_params(),
        name="s5_mixer",
    )(u_col, u_meta, csc, c_col, b_col, c_row)


FF_CHUNK = 1024
MIX_PER = 4


MIX_SLOTS = 3


def _mix_ffn_kernel(x_hbm, co_ref, y_ref, ut_ref, d_ref, gwt_ref, gb_ref, wo_ref,
                    gf_ref, w1_ref, w2_ref, gl_ref, o_hbm, buf, sem_in, sem_out,
                    mixed_ref, h_ref, z_ref, a_ref):
    n_chunks = x_hbm.shape[1]
    rows = SUBLANES * n_chunks
    step = pl.program_id(0)
    n_tiles = pl.num_programs(0) - 1
    tile = step - 1

    def mix_stages(dst):
        per = MIX_PER
        n_half = SUBLANES // per
        state = {}

        def col(ref, k):
            return jnp.concatenate(
                [ref[:, r * SSM_GROUP:(r + 1) * SSM_GROUP, :].reshape(D_SSM, n_chunks)
                 for r in range(k * per, (k + 1) * per)], axis=1).astype(F32)

        def gelu_stage(k):
            def run():
                state["ge", k] = jax.nn.gelu(col(y_ref, k) + d_ref[...] * col(ut_ref, k))
            return run

        def glu_stage(k):
            def run():
                ge = state["ge", k]
                gate = (jnp.dot(gwt_ref[...], ge.astype(BF16), preferred_element_type=F32)
                        + gb_ref[...])
                state["s5t", k] = ge * jax.nn.sigmoid(gate)
            return run

        def store_stage(k):
            def run():
                rs = slice(k * per * n_chunks, (k + 1) * per * n_chunks)
                mixed_ref[dst, rs, :D_CONV] = co_ref[k * per:(k + 1) * per].reshape(
                    per * n_chunks, D_CONV)
                mixed_ref[dst, rs, D_CONV:] = state["s5t", k].T.astype(BF16)
            return run

        return [f(k) for f in (gelu_stage, glu_stage, store_stage) for k in range(n_half)]

    @pl.when(step == 0)
    def _():
        for cp in _step_copies(x_hbm, buf, sem_in, 0, 0, False):
            cp.start()
        for stage in mix_stages(0):
            stage()

    @pl.when(step >= 1)
    def _():
        slot = tile % MIX_SLOTS
        nxt = (tile + 1) % MIX_SLOTS

        @pl.when(tile >= 2)
        def _():
            for cp in _step_copies(o_hbm, buf, sem_out, tile - 2, nxt, True):
                cp.wait()

        @pl.when(tile + 1 < n_tiles)
        def _():
            for cp in _step_copies(x_hbm, buf, sem_in, tile + 1, nxt, False):
                cp.start()

        for cp in _step_copies(x_hbm, buf, sem_in, tile, slot, False):
            cp.wait()

        stages = mix_stages(step % 2)
        half = D_FF // 2
        n_slots = D_FF // FF_CHUNK + 2
        n_stages = len(stages)
        done = [0, 0]

        def fill_slot():
            done[0] += 1
            while done[1] * n_slots < done[0] * n_stages:
                stages.pop(0)()
                done[1] += 1

        h = buf[slot].reshape(rows, D_MODEL) + jnp.dot(
            mixed_ref[tile % 2], wo_ref[...], preferred_element_type=F32)
        z_ref[...] = _rms(h, gf_ref[...]).astype(BF16)
        h_ref[...] = h
        for k in range(2):
            for j in range(half // FF_CHUNK):
                cols = slice(k * half + j * FF_CHUNK, k * half + (j + 1) * FF_CHUNK)
                a = jnp.dot(z_ref[...], w1_ref[:, cols], preferred_element_type=F32)
                a_ref[:, j * FF_CHUNK:(j + 1) * FF_CHUNK] = (
                    jnp.square(jnp.maximum(a, 0.0)).astype(BF16))
                fill_slot()
            h_ref[...] += jnp.dot(a_ref[...], w2_ref[k * half:(k + 1) * half, :],
                                  preferred_element_type=F32)
            fill_slot()
        assert not stages
        buf[slot] = _rms(h_ref[...], gl_ref[...]).reshape(SUBLANES, n_chunks, D_MODEL)

        for cp in _step_copies(o_hbm, buf, sem_out, tile, slot, True):
            cp.start()

        @pl.when(tile == n_tiles - 1)
        def _():
            for cp in (_step_copies(o_hbm, buf, sem_out, tile - 1, (tile - 1) % MIX_SLOTS, True)
                       + _step_copies(o_hbm, buf, sem_out, tile, slot, True)):
                cp.wait()


def _mix_ffn(x4, conv_out, y_col, u_col, d_col, glu_wt, glu_b_col, w_out, g_ffn, w1, w2,
             g_final):
    bsz, n_chunks = x4.shape[:2]

    def const(shape):
        return pl.BlockSpec(shape, lambda t: (0, 0), pipeline_mode=pl.Buffered(1))

    n_tiles = CHUNK // SUBLANES * bsz
    hbm = pl.BlockSpec(memory_space=pl.ANY)

    def slab(s):
        return jnp.minimum(s, n_tiles - 1) // bsz

    def batch(s):
        return jnp.minimum(s, n_tiles - 1) % bsz

    col_blk = pl.BlockSpec((N_GROUPS, SUBLANES * SSM_GROUP, n_chunks),
                           lambda s: (0, slab(s), batch(s)))
    step_sems = pltpu.SemaphoreType.DMA((MIX_SLOTS, SUBLANES))
    return pl.pallas_call(
        _mix_ffn_kernel,
        grid=(n_tiles + 1,),
        in_specs=[
            hbm,
            pl.BlockSpec((SUBLANES, n_chunks, D_CONV), lambda s: (slab(s), batch(s), 0)),
            col_blk,
            col_blk,
            const((D_SSM, 1)),
            const((D_SSM, D_SSM)),
            const((D_SSM, 1)),
            const((D_MODEL, D_MODEL)),
            const((1, D_MODEL)),
            const((D_MODEL, D_FF)),
            const((D_FF, D_MODEL)),
            const((1, D_MODEL)),
        ],
        out_specs=hbm,
        out_shape=jax.ShapeDtypeStruct(x4.shape, F32),
        scratch_shapes=[
            pltpu.VMEM((MIX_SLOTS, SUBLANES, n_chunks, D_MODEL), F32),
            step_sems, step_sems,
            pltpu.VMEM((2, SUBLANES * n_chunks, D_MODEL), BF16),
            pltpu.VMEM((SUBLANES * n_chunks, D_MODEL), F32),
            pltpu.VMEM((SUBLANES * n_chunks, D_MODEL), BF16),
            pltpu.VMEM((SUBLANES * n_chunks, D_FF // 2), BF16),
        ],
        compiler_params=_step_tile_params(),
        name="mix_ffn",
    )(x4, conv_out, y_col, u_col, d_col, glu_wt, glu_b_col, w_out, g_ffn, w1, w2, g_final)


def _s5_operator_inputs(lbr, lbi, zr, zi, b_re, b_im, c_re, c_im):
    g, p, h = N_GROUPS, SSM_STATE, SSM_GROUP

    sc = jnp.transpose(jnp.stack([lbr, lbi, zr, zi], axis=-1), (1, 0, 2, 3))
    csc = jnp.broadcast_to(sc[:, :, None], (g, 2, 2, p, 4)).reshape(g, 4 * p, 4)
    c = jnp.stack([c_re, c_im], axis=0)
    b = jnp.stack([b_re, b_im], axis=0)
    c_col = jnp.transpose(c, (2, 1, 0, 4, 3)).reshape(g, 4 * p, h)
    b_col = jnp.transpose(b, (2, 1, 0, 3, 4)).reshape(g, 4 * p, h)
    c_row = jnp.transpose(c, (2, 3, 1, 0, 4)).reshape(g, h, 4 * p)
    return csc, c_col, b_col, c_row


def kernel(x, meta_tokens, norm_mix_g, w_in, conv_w, conv_b, conv_ln_g, conv_ln_b,
           ssm_lam_re, ssm_lam_im, ssm_log_dt, ssm_b_re, ssm_b_im, ssm_c_re, ssm_c_im,
           ssm_d, ssm_glu_w, ssm_glu_b, w_out, norm_ffn_g, w_ff1, w_ff2, norm_final_g):
    assert w_in.shape[0] == 1, "single-layer block"
    bsz, seq, _ = x.shape
    assert seq % CHUNK == 0 and CHUNK >= N_META
    n_chunks = seq // CHUNK
    g, h = N_GROUPS, SSM_GROUP

    x4 = x.reshape(bsz, n_chunks, CHUNK, D_MODEL)
    g_mix = norm_mix_g[0][None, :]
    w_in_b = w_in[0].astype(BF16)
    u_conv, u_col = _in_proj_step(x4, g_mix, w_in_b)
    u_conv_m, u_ssm_m = _in_proj_meta(meta_tokens, g_mix, w_in_b)

    conv_out = _conv_module(u_conv, u_conv_m, conv_w[0], conv_b[0][None, :],
                            conv_ln_g[0][None, :], conv_ln_b[0][None, :], bsz)

    ldt = jnp.broadcast_to(ssm_log_dt[0][..., None], ssm_lam_re[0].shape)
    flat = lambda a: a.reshape(2 * g, SSM_STATE)
    lbr, lbi, zr, zi = [a.reshape(2, g, SSM_STATE) for a in
                        _zoh(flat(ssm_lam_re[0]), flat(ssm_lam_im[0]), flat(ldt))]
    csc, c_col, b_col, c_row = _s5_operator_inputs(
        lbr, lbi, zr, zi, ssm_b_re[0], ssm_b_im[0], ssm_c_re[0], ssm_c_im[0])

    u_meta = jnp.transpose(u_ssm_m.reshape(N_META, g, h), (1, 0, 2)).reshape(g, 1, N_META * h)
    y_col = _ssm(u_col, u_meta, csc, c_col, b_col, c_row, n_chunks)

    out = _mix_ffn(x4, conv_out, y_col, u_col, ssm_d[0][:, None],
                   ssm_glu_w[0].T.astype(BF16), ssm_glu_b[0][:, None],
                   w_out[0].astype(BF16), norm_ffn_g[0][None, :], w_ff1[0].astype(BF16),
                   w_ff2[0].astype(BF16), norm_final_g[None, :])
    return out.reshape(bsz, seq, D_MODEL)
```

```python
import functools

import jax
import jax.numpy as jnp
from jax import lax
from jax.experimental import pallas as pl
from jax.experimental.pallas import tpu as pltpu

F32 = jnp.float32
BF16 = jnp.bfloat16

D_MODEL = 1024
N_META = 16
D_CONV = 512
D_SSM = 512
CONV_WIDTH = 31
CONV_PAD = CONV_WIDTH // 2
SSM_GROUP = 16
N_GROUPS = D_SSM // SSM_GROUP
SSM_STATE = 64
D_FF = 4096
NORM_EPS = 1e-5
LANES = 128
SUBLANES = 8

CHUNK = 32
CHUNK_LANES = CHUNK * SSM_GROUP
STATE_ROWS = 4 * SSM_STATE
DIR_ROWS = 2 * SSM_STATE
STEPS_PER_VREG = LANES // SSM_GROUP

VMEM_LIMIT_BYTES = 60 * 1024 * 1024


def _rms(x, g):
    return x * lax.rsqrt(jnp.mean(x * x, axis=-1, keepdims=True) + NORM_EPS) * g


def _params(n_axes=1):
    return pltpu.CompilerParams(dimension_semantics=("parallel",) * n_axes,
                                vmem_limit_bytes=VMEM_LIMIT_BYTES)


def _step_copies(hbm_ref, buf_ref, sem_ref, tile, slot, to_hbm):
    bsz = hbm_ref.shape[0]
    slab, b = tile // bsz, tile % bsz
    copies = []
    for r in range(SUBLANES):
        hbm = hbm_ref.at[b, :, slab * SUBLANES + r, :]
        vmem = buf_ref.at[slot, r]
        src, dst = (vmem, hbm) if to_hbm else (hbm, vmem)
        copies.append(pltpu.make_async_copy(src, dst, sem_ref.at[slot, r]))
    return copies


def _fetch_steps(x_hbm, xbuf, sem):
    tile = pl.program_id(0)
    slot = tile % 2

    @pl.when(tile == 0)
    def _():
        for cp in _step_copies(x_hbm, xbuf, sem, tile, slot, False):
            cp.start()

    @pl.when(tile + 1 < pl.num_programs(0))
    def _():
        for cp in _step_copies(x_hbm, xbuf, sem, tile + 1, 1 - slot, False):
            cp.start()

    for cp in _step_copies(x_hbm, xbuf, sem, tile, slot, False):
        cp.wait()
    return slot


def _step_tile_params():
    return pltpu.CompilerParams(dimension_semantics=("arbitrary",),
                                vmem_limit_bytes=VMEM_LIMIT_BYTES)


def _in_proj_step_kernel(x_hbm, meta_ref, g_ref, w_ref, uc_ref, ut_ref, ucm_ref, usm_ref,
                         xbuf, sem):
    n_chunks = x_hbm.shape[1]
    slot = _fetch_steps(x_hbm, xbuf, sem)

    @pl.when(pl.program_id(0) == 0)
    def _():
        zm = _rms(meta_ref[...], g_ref[...]).astype(BF16)
        pm = jnp.dot(zm, w_ref[...], preferred_element_type=F32)
        ucm_ref[...] = pm[:, :D_CONV] * jax.nn.sigmoid(pm[:, D_CONV:2 * D_CONV])
        usm_ref[...] = pm[:, 2 * D_CONV:]

    x = xbuf[slot].reshape(SUBLANES * n_chunks, D_MODEL)
    z = _rms(x, g_ref[...]).astype(BF16)
    p = jnp.dot(z, w_ref[...], preferred_element_type=F32)
    uc = p[:, :D_CONV] * jax.nn.sigmoid(p[:, D_CONV:2 * D_CONV])
    uc_ref[...] = uc.reshape(SUBLANES, n_chunks, D_CONV)
    ut = p[:, 2 * D_CONV:].T.astype(BF16)
    for r in range(SUBLANES):
        ut_ref[:, r * SSM_GROUP:(r + 1) * SSM_GROUP, :] = (
            ut[:, r * n_chunks:(r + 1) * n_chunks].reshape(N_GROUPS, SSM_GROUP, n_chunks))


def _in_proj_step(x4, meta, g, w_in):
    bsz, n_chunks = x4.shape[:2]
    rows = bsz * n_chunks
    return pl.pallas_call(
        _in_proj_step_kernel,
        grid=(CHUNK // SUBLANES * bsz,),
        in_specs=[
            pl.BlockSpec(memory_space=pl.ANY),
            pl.BlockSpec((N_META, D_MODEL), lambda t: (0, 0)),
            pl.BlockSpec((1, D_MODEL), lambda t: (0, 0)),
            pl.BlockSpec((D_MODEL, 2 * D_CONV + D_SSM), lambda t: (0, 0)),
        ],
        out_specs=[
            pl.BlockSpec((SUBLANES, n_chunks, D_CONV), lambda t: (t // bsz, t % bsz, 0)),
            pl.BlockSpec((N_GROUPS, SUBLANES * SSM_GROUP, n_chunks),
                         lambda t: (0, t // bsz, t % bsz)),
            pl.BlockSpec((N_META, D_CONV), lambda t: (0, 0)),
            pl.BlockSpec((N_META, D_SSM), lambda t: (0, 0)),
        ],
        out_shape=[
            jax.ShapeDtypeStruct((CHUNK, rows, D_CONV), F32),
            jax.ShapeDtypeStruct((N_GROUPS, CHUNK_LANES, rows), BF16),
            jax.ShapeDtypeStruct((N_META, D_CONV), F32),
            jax.ShapeDtypeStruct((N_META, D_SSM), F32),
        ],
        scratch_shapes=[pltpu.VMEM((2, SUBLANES, n_chunks, D_MODEL), F32),
                        pltpu.SemaphoreType.DMA((2, SUBLANES))],
        compiler_params=_step_tile_params(),
        name="in_proj",
    )(x4, meta, g, w_in)


CONV_GROUP = 4
CONV_ROWS = 64


def _conv_kernel(u_ref, um_ref, w_ref, cb_ref, lg_ref, lb_ref, o_ref, cat_ref, acc_ref):
    n_steps, n_chunks, _ = u_ref.shape
    cat_ref[CONV_PAD:CONV_PAD + n_steps] = u_ref[...]
    chunk = lax.broadcasted_iota(jnp.int32, (n_chunks, 1), 0)
    for i in range(CONV_PAD):
        s_prev = n_steps - CONV_PAD + i
        prev = pltpu.roll(u_ref[s_prev], 1, axis=0)
        meta_row = um_ref[N_META - CONV_PAD + i:N_META - CONV_PAD + i + 1, :]
        cat_ref[i] = jnp.where(chunk == 0, meta_row, prev)
        nxt = pltpu.roll(u_ref[i], n_chunks - 1, axis=0)
        cat_ref[n_steps + CONV_PAD + i] = jnp.where(chunk == n_chunks - 1, 0.0, nxt)

    def body(sg, carry):
        s0 = sg * CONV_GROUP
        for c in range(D_CONV // LANES):
            cols = slice(c * LANES, (c + 1) * LANES)
            n_sub = CONV_ROWS // SUBLANES

            def rows_body(rt, carry2):
                r0 = [pl.multiple_of(rt * CONV_ROWS + sub * SUBLANES, SUBLANES)
                      for sub in range(n_sub)]
                accs = [[jnp.zeros((SUBLANES, LANES), F32)] * CONV_GROUP for _ in range(n_sub)]
                for i in range(CONV_WIDTH + CONV_GROUP - 1):
                    ds = [cat_ref[s0 + i, pl.ds(r0[sub], SUBLANES), cols]
                          for sub in range(n_sub)]
                    for j in range(CONV_GROUP):
                        if 0 <= i - j < CONV_WIDTH:
                            tap = jnp.broadcast_to(w_ref[i - j:i - j + 1, cols],
                                                   (SUBLANES, LANES))
                            for sub in range(n_sub):
                                accs[sub][j] = accs[sub][j] + ds[sub] * tap
                for sub in range(n_sub):
                    for j in range(CONV_GROUP):
                        acc_ref[j, pl.ds(r0[sub], SUBLANES), cols] = accs[sub][j]
                return carry2

            lax.fori_loop(0, n_chunks // CONV_ROWS, rows_body, 0)
        for j in range(CONV_GROUP):
            y = acc_ref[j] + cb_ref[...]
            yc = y - jnp.mean(y, axis=-1, keepdims=True)
            yn = yc * lax.rsqrt(jnp.mean(yc * yc, axis=-1, keepdims=True) + NORM_EPS)
            yn = yn * lg_ref[...] + lb_ref[...]
            o_ref[s0 + j] = (yn * jax.nn.sigmoid(yn)).astype(BF16)
        return carry

    lax.fori_loop(0, n_steps // CONV_GROUP, body, 0)


def _conv_module(u_conv, u_conv_meta, conv_w, conv_b, ln_g, ln_b, bsz):
    n_steps, rows, _ = u_conv.shape
    n_chunks = rows // bsz
    vec = pl.BlockSpec((1, D_CONV), lambda b: (0, 0))
    blk = pl.BlockSpec((n_steps, n_chunks, D_CONV), lambda b: (0, b, 0))
    return pl.pallas_call(
        _conv_kernel,
        grid=(bsz,),
        in_specs=[
            blk,
            pl.BlockSpec((N_META, D_CONV), lambda b: (0, 0)),
            pl.BlockSpec((CONV_WIDTH, D_CONV), lambda b: (0, 0)),
            vec, vec, vec,
        ],
        out_specs=blk,
        out_shape=jax.ShapeDtypeStruct((n_steps, rows, D_CONV), BF16),
        scratch_shapes=[pltpu.VMEM((n_steps + 2 * CONV_PAD, n_chunks, D_CONV), F32),
                        pltpu.VMEM((CONV_GROUP, n_chunks, D_CONV), F32)],
        compiler_params=_params(),
        name="conv_module",
    )(u_conv, u_conv_meta, conv_w, conv_b, ln_g, ln_b)


def _zoh_kernel(lre_ref, lim_ref, ldt_ref, lbr_ref, lbi_ref, zr_ref, zi_ref):
    lre = lre_ref[...]
    lim = lim_ref[...]
    dt = jnp.exp(ldt_ref[...])
    ea = jnp.exp(lre * dt)
    lbr = ea * jnp.cos(lim * dt)
    lbi = ea * jnp.sin(lim * dt)
    nr = lbr - 1.0
    den = lre * lre + lim * lim
    lbr_ref[...] = lbr
    lbi_ref[...] = lbi
    zr_ref[...] = (nr * lre + lbi * lim) / den
    zi_ref[...] = (lbi * lre - nr * lim) / den


def _zoh(lam_re, lam_im, log_dt):
    shape = jax.ShapeDtypeStruct(lam_re.shape, F32)
    return pl.pallas_call(_zoh_kernel, out_shape=[shape] * 4, name="s5_zoh")(
        lam_re, lam_im, log_dt)


def _lanes(x, n):
    return jnp.concatenate([x] * (n // LANES), axis=1)


def _cmul(ar, ai, br, bi):
    return ar * br - ai * bi, ar * bi + ai * br


def _cpow(br, bi, expo, nbits):
    rr = jnp.ones(expo.shape, F32)
    ri = jnp.zeros(expo.shape, F32)
    for k in range(nbits):
        bit = ((expo >> k) & 1) == 1
        nr, ni = _cmul(rr, ri, br, bi)
        rr = jnp.where(bit, nr, rr)
        ri = jnp.where(bit, ni, ri)
        br, bi = _cmul(br, bi, br, bi)
    return rr, ri


def _build_chunk_ops(gi, csc_ref, cc_ref, bc_ref, cr_ref, um_ref, toep_ref, wend_ref,
                     wout_ref):
    t = CHUNK
    rows = STATE_ROWS
    n_cols = CHUNK_LANES // LANES
    lam = (jnp.broadcast_to(csc_ref[gi, :, 0:1], (rows, LANES)),
           jnp.broadcast_to(csc_ref[gi, :, 1:2], (rows, LANES)))
    zr = jnp.broadcast_to(csc_ref[gi, :, 2:3], (rows, LANES))
    zi = jnp.broadcast_to(csc_ref[gi, :, 3:4], (rows, LANES))
    sel = (lax.broadcasted_iota(jnp.int32, (SSM_GROUP, LANES), 1) % SSM_GROUP
           == lax.broadcasted_iota(jnp.int32, (SSM_GROUP, LANES), 0)).astype(F32)

    def tile(ref):
        return jnp.dot(ref[gi], sel, precision=lax.Precision.HIGHEST,
                       preferred_element_type=F32)

    def swap_re_im(a):
        p = SSM_STATE
        return jnp.concatenate([a[p:2 * p], a[0:p], a[3 * p:], a[2 * p:3 * p]], axis=0)

    c_same, b_same = tile(cc_ref), tile(bc_ref)
    c_swap, b_swap = swap_re_im(c_same), swap_re_im(b_same)
    row = lax.broadcasted_iota(jnp.int32, (rows, LANES), 0)
    is_re = (row // SSM_STATE) % 2 == 0
    sgn = jnp.where(is_re, -1.0, 1.0)
    ca = jnp.where(is_re, c_same, -c_same)
    cb = -c_swap
    bb_same = zr * b_same + sgn * zi * b_swap
    sbb_swap = sgn * (zr * b_swap - sgn * zi * b_same)

    pows = {1: lam}
    k = 1
    while k < t:
        pows[2 * k] = _cmul(*pows[k], *pows[k])
        k *= 2
    step_bits = STEPS_PER_VREG.bit_length() - 1
    fwd = slice(0, DIR_ROWS)
    bwd = slice(DIR_ROWS, rows)
    every = slice(0, rows)

    def rows_of(v, rs):
        return v[0][rs], v[1][rs]

    def column(base, e, rs):
        out = rows_of(base, rs)
        for bit, val in pows.items():
            if e & bit:
                out = _cmul(*out, *rows_of(val, rs))
        return out

    def times_b(p, rs):
        return p[0] * bb_same[rs] + p[1] * sbb_swap[rs]

    def times_c(p, rs):
        return ca[rs] * p[0] + cb[rs] * p[1]

    i8 = lax.broadcasted_iota(jnp.int32, (rows, LANES), 1) // SSM_GROUP
    asc0 = _cpow(*lam, i8, step_bits)
    desc0 = _cpow(*lam, STEPS_PER_VREG - 1 - i8, step_bits)
    asc1_0 = _cmul(*asc0, *lam)
    desc1_0 = _cmul(*desc0, *lam)

    wend_f, wend_b, lag_b, wout_f, wout_b = [], [], [], [], []
    for q in range(n_cols):
        e_asc = STEPS_PER_VREG * q
        e_desc = STEPS_PER_VREG * (n_cols - 1 - q)
        asc1_q = column(asc1_0, e_asc, every)
        wend_f.append(times_b(column(desc0, e_desc, fwd), fwd))
        wend_b.append(times_b(column(asc0, e_asc, bwd), bwd))
        lag_b.append(times_b(rows_of(asc1_q, bwd), bwd))
        wout_f.append(times_c(rows_of(asc1_q, fwd), fwd))
        wout_b.append(times_c(column(desc1_0, e_desc, bwd), bwd))
        cols = slice(q * LANES, (q + 1) * LANES)
        wend_ref[gi, fwd, cols] = wend_f[q].astype(BF16)
        wend_ref[gi, bwd, cols] = wend_b[q].astype(BF16)

    meta_cols = N_META * SSM_GROUP // LANES
    x0 = sum(jnp.sum(wend_f[n_cols - meta_cols + m] * um_ref[gi, :, m * LANES:(m + 1) * LANES],
                     axis=1, keepdims=True) for m in range(meta_cols))

    lane = lax.broadcasted_iota(jnp.int32, (DIR_ROWS, LANES), 1)
    lag0_b = jnp.where(lane >= LANES - SSM_GROUP, bb_same[bwd], 0.0)
    lane_r = lax.broadcasted_iota(jnp.int32, (SSM_GROUP, rows), 1)
    c2 = jnp.where((lane_r // SSM_STATE) % 2 == 0, cr_ref[gi], -cr_ref[gi])

    def lag_dot(c, m):
        return jnp.dot(c, m, precision=lax.Precision.HIGHEST, preferred_element_type=F32)

    kk_f = lag_dot(c2[:, fwd], jnp.concatenate(wend_f, axis=1))
    kk_0 = lag_dot(c2[:, bwd], lag0_b)
    kk_b = lag_dot(c2[:, bwd], jnp.concatenate(lag_b, axis=1))
    kk = jnp.concatenate([kk_f[:, :CHUNK_LANES - LANES], kk_f[:, CHUNK_LANES - LANES:] + kk_0,
                          kk_b], axis=1)
    for tt in range(t):
        off = (t - 1 - tt) * SSM_GROUP
        toep_ref[gi, tt * SSM_GROUP:(tt + 1) * SSM_GROUP, :] = (
            kk[:, off:off + CHUNK_LANES].astype(BF16))

    wout = jnp.concatenate([jnp.concatenate(wout_f, axis=1),
                            jnp.concatenate(wout_b, axis=1)], axis=0)
    wout_ref[gi] = wout.T.astype(BF16)
    return pows[t], x0


SSM_GROUPS_PER_STEP = 2


def _chunk_scan(er, ei, ar, ai, x0, chunk, n_chunks, forward):
    width = er.shape[1]
    xr, xi = er, ei
    if x0 is not None:
        fr, fi = _cmul(ar, ai, x0[0], x0[1])
        xr = xr + jnp.where(chunk == 0, _lanes(fr, width), 0.0)
        xi = xi + jnp.where(chunk == 0, _lanes(fi, width), 0.0)

    def shifted(v, sh):
        if forward:
            return jnp.where(chunk >= sh, pltpu.roll(v, sh, axis=1), 0.0)
        return jnp.where(chunk < n_chunks - sh, pltpu.roll(v, width - sh, axis=1), 0.0)

    sh = 1
    while sh < n_chunks:
        sr, si = shifted(xr, sh), shifted(xi, sh)
        wr, wi = _lanes(ar, width), _lanes(ai, width)
        xr, xi = xr + wr * sr - wi * si, xi + wr * si + wi * sr
        ar, ai = _cmul(ar, ai, ar, ai)
        sh *= 2
    inr, ini = shifted(xr, 1), shifted(xi, 1)
    if x0 is not None:
        inr = jnp.where(chunk == 0, x0[0], inr)
        ini = jnp.where(chunk == 0, x0[1], ini)
    return inr, ini


def _ssm_kernel(u_ref, um_ref, csc_ref, cc_ref, bc_ref, cr_ref, y_ref, toep_ref, wend_ref,
                wout_ref, yi_ref, *, n_chunks):
    for gi in range(SSM_GROUPS_PER_STEP):
        (lam_r, lam_i), x0 = _build_chunk_ops(gi, csc_ref, cc_ref, bc_ref, cr_ref, um_ref,
                                              toep_ref, wend_ref, wout_ref)
        u = u_ref[gi]
        width = u.shape[1]
        e = jnp.dot(wend_ref[gi], u, preferred_element_type=F32)
        yi_ref[gi] = jnp.dot(toep_ref[gi], u, preferred_element_type=F32)
        chunk =lax.broadcasted_iota(jnp.int32, (1, width), 1) % n_chunks
        p = SSM_STATE
        ar, ai = lam_r, lam_i
        f_in = _chunk_scan(e[0:p], e[p:2 * p], ar[0:p], ai[0:p], (x0[0:p], x0[p:2 * p]),
                           chunk, n_chunks, True)
        b_in = _chunk_scan(e[2 * p:3 * p], e[3 * p:], ar[2 * p:3 * p], ai[2 * p:3 * p],
                           None, chunk, n_chunks, False)
        xin = jnp.concatenate([f_in[0], f_in[1], b_in[0], b_in[1]], axis=0).astype(BF16)
        y = yi_ref[gi] + jnp.dot(wout_ref[gi], xin, preferred_element_type=F32)
        y_ref[gi] = y.astype(BF16)


def _ssm(u_col, u_meta, csc, c_col, b_col, c_row, n_chunks):
    g, _, width = u_col.shape
    per = SSM_GROUPS_PER_STEP
    return pl.pallas_call(
        functools.partial(_ssm_kernel, n_chunks=n_chunks),
        grid=(g // per,),
        in_specs=[
            pl.BlockSpec((per, CHUNK_LANES, width), lambda i: (i, 0, 0)),
            pl.BlockSpec((per, 1, N_META * SSM_GROUP), lambda i: (i, 0, 0)),
            pl.BlockSpec((per, STATE_ROWS, 4), lambda i: (i, 0, 0)),
            pl.BlockSpec((per, STATE_ROWS, SSM_GROUP), lambda i: (i, 0, 0)),
            pl.BlockSpec((per, STATE_ROWS, SSM_GROUP), lambda i: (i, 0, 0)),
            pl.BlockSpec((per, SSM_GROUP, STATE_ROWS), lambda i: (i, 0, 0)),
        ],
        out_specs=pl.BlockSpec((per, CHUNK_LANES, width), lambda i: (i, 0, 0)),
        out_shape=jax.ShapeDtypeStruct((g, CHUNK_LANES, width), BF16),
        scratch_shapes=[
            pltpu.VMEM((per, CHUNK_LANES, CHUNK_LANES), BF16),
            pltpu.VMEM((per, STATE_ROWS, CHUNK_LANES), BF16),
            pltpu.VMEM((per, CHUNK_LANES, STATE_ROWS), BF16),
            pltpu.VMEM((per, CHUNK_LANES, width), F32),
        ],
        compiler_params=_params(),
        name="s5_mixer",
    )(u_col, u_meta, csc, c_col, b_col, c_row)


FF_CHUNK = 1024
MIX_PER = 4


MIX_SLOTS = 3


def _mix_ffn_kernel(x_hbm, co_ref, y_ref, ut_ref, d_ref, gwt_ref, gb_ref, wo_ref,
                    gf_ref, w1_ref, w2_ref, gl_ref, o_hbm, buf, sem_in, sem_out,
                    mixed_ref, h_ref, z_ref, a_ref):
    n_chunks = x_hbm.shape[1]
    rows = SUBLANES * n_chunks
    step = pl.program_id(0)
    n_tiles = pl.num_programs(0) - 1
    tile = step - 1

    def mix_stages(dst):
        per = MIX_PER
        n_half = SUBLANES // per
        state = {}

        def col(ref, k):
            return jnp.concatenate(
                [ref[:, r * SSM_GROUP:(r + 1) * SSM_GROUP, :].reshape(D_SSM, n_chunks)
                 for r in range(k * per, (k + 1) * per)], axis=1).astype(F32)

        def gelu_stage(k):
            def run():
                state["ge", k] = jax.nn.gelu(col(y_ref, k) + d_ref[...] * col(ut_ref, k))
            return run

        def glu_stage(k):
            def run():
                ge = state["ge", k]
                gate = (jnp.dot(gwt_ref[...], ge.astype(BF16), preferred_element_type=F32)
                        + gb_ref[...])
                state["s5t", k] = ge * jax.nn.sigmoid(gate)
            return run

        def store_stage(k):
            def run():
                rs = slice(k * per * n_chunks, (k + 1) * per * n_chunks)
                mixed_ref[dst, rs, :D_CONV] = co_ref[k * per:(k + 1) * per].reshape(
                    per * n_chunks, D_CONV)
                mixed_ref[dst, rs, D_CONV:] = state["s5t", k].T.astype(BF16)
            return run

        return [f(k) for f in (gelu_stage, glu_stage, store_stage) for k in range(n_half)]

    @pl.when(step == 0)
    def _():
        for cp in _step_copies(x_hbm, buf, sem_in, 0, 0, False):
            cp.start()
        for stage in mix_stages(0):
            stage()

    @pl.when(step >= 1)
    def _():
        slot = tile % MIX_SLOTS
        nxt = (tile + 1) % MIX_SLOTS

        @pl.when(tile >= 2)
        def _():
            for cp in _step_copies(o_hbm, buf, sem_out, tile - 2, nxt, True):
                cp.wait()

        @pl.when(tile + 1 < n_tiles)
        def _():
            for cp in _step_copies(x_hbm, buf, sem_in, tile + 1, nxt, False):
                cp.start()

        for cp in _step_copies(x_hbm, buf, sem_in, tile, slot, False):
            cp.wait()

        stages = mix_stages(step % 2)
        half = D_FF // 2
        n_slots = D_FF // FF_CHUNK + 2
        n_stages = len(stages)
        done = [0, 0]

        def fill_slot():
            done[0] += 1
            while done[1] * n_slots < done[0] * n_stages:
                stages.pop(0)()
                done[1] += 1

        h = buf[slot].reshape(rows, D_MODEL) + jnp.dot(
            mixed_ref[tile % 2], wo_ref[...], preferred_element_type=F32)
        z_ref[...] = _rms(h, gf_ref[...]).astype(BF16)
        h_ref[...] = h
        for k in range(2):
            for j in range(half // FF_CHUNK):
                cols = slice(k * half + j * FF_CHUNK, k * half + (j + 1) * FF_CHUNK)
                a = jnp.dot(z_ref[...], w1_ref[:, cols], preferred_element_type=F32)
                a_ref[:, j * FF_CHUNK:(j + 1) * FF_CHUNK] = (
                    jnp.square(jnp.maximum(a, 0.0)).astype(BF16))
                fill_slot()
            h_ref[...] += jnp.dot(a_ref[...], w2_ref[k * half:(k + 1) * half, :],
                                  preferred_element_type=F32)
            fill_slot()
        assert not stages
        buf[slot] = _rms(h_ref[...], gl_ref[...]).reshape(SUBLANES, n_chunks, D_MODEL)

        for cp in _step_copies(o_hbm, buf, sem_out, tile, slot, True):
            cp.start()

        @pl.when(tile == n_tiles - 1)
        def _():
            for cp in (_step_copies(o_hbm, buf, sem_out, tile - 1, (tile - 1) % MIX_SLOTS, True)
                       + _step_copies(o_hbm, buf, sem_out, tile, slot, True)):
                cp.wait()


def _mix_ffn(x4, conv_out, y_col, u_col, d_col, glu_wt, glu_b_col, w_out, g_ffn, w1, w2,
             g_final):
    bsz, n_chunks = x4.shape[:2]

    def const(shape):
        return pl.BlockSpec(shape, lambda t: (0, 0), pipeline_mode=pl.Buffered(1))

    n_tiles = CHUNK // SUBLANES * bsz
    hbm = pl.BlockSpec(memory_space=pl.ANY)

    def slab(s):
        return jnp.minimum(s, n_tiles - 1) // bsz

    def batch(s):
        return jnp.minimum(s, n_tiles - 1) % bsz

    col_blk = pl.BlockSpec((N_GROUPS, SUBLANES * SSM_GROUP, n_chunks),
                           lambda s: (0, slab(s), batch(s)))
    step_sems = pltpu.SemaphoreType.DMA((MIX_SLOTS, SUBLANES))
    return pl.pallas_call(
        _mix_ffn_kernel,
        grid=(n_tiles + 1,),
        in_specs=[
            hbm,
            pl.BlockSpec((SUBLANES, n_chunks, D_CONV), lambda s: (slab(s), batch(s), 0)),
            col_blk,
            col_blk,
            const((D_SSM, 1)),
            const((D_SSM, D_SSM)),
            const((D_SSM, 1)),
            const((D_MODEL, D_MODEL)),
            const((1, D_MODEL)),
            const((D_MODEL, D_FF)),
            const((D_FF, D_MODEL)),
            const((1, D_MODEL)),
        ],
        out_specs=hbm,
        out_shape=jax.ShapeDtypeStruct(x4.shape, F32),
        scratch_shapes=[
            pltpu.VMEM((MIX_SLOTS, SUBLANES, n_chunks, D_MODEL), F32),
            step_sems, step_sems,
            pltpu.VMEM((2, SUBLANES * n_chunks, D_MODEL), BF16),
            pltpu.VMEM((SUBLANES * n_chunks, D_MODEL), F32),
            pltpu.VMEM((SUBLANES * n_chunks, D_MODEL), BF16),
            pltpu.VMEM((SUBLANES * n_chunks, D_FF // 2), BF16),
        ],
        compiler_params=_step_tile_params(),
        name="mix_ffn",
    )(x4, conv_out, y_col, u_col, d_col, glu_wt, glu_b_col, w_out, g_ffn, w1, w2, g_final)


def _s5_operator_inputs(lbr, lbi, zr, zi, b_re, b_im, c_re, c_im):
    g, p, h = N_GROUPS, SSM_STATE, SSM_GROUP

    sc = jnp.transpose(jnp.stack([lbr, lbi, zr, zi], axis=-1), (1, 0, 2, 3))
    csc = jnp.broadcast_to(sc[:, :, None], (g, 2, 2, p, 4)).reshape(g, 4 * p, 4)
    c = jnp.stack([c_re, c_im], axis=0)
    b = jnp.stack([b_re, b_im], axis=0)
    c_col = jnp.transpose(c, (2, 1, 0, 4, 3)).reshape(g, 4 * p, h)
    b_col = jnp.transpose(b, (2, 1, 0, 3, 4)).reshape(g, 4 * p, h)
    c_row = jnp.transpose(c, (2, 3, 1, 0, 4)).reshape(g, h, 4 * p)
    return csc, c_col, b_col, c_row


def kernel(x, meta_tokens, norm_mix_g, w_in, conv_w, conv_b, conv_ln_g, conv_ln_b,
           ssm_lam_re, ssm_lam_im, ssm_log_dt, ssm_b_re, ssm_b_im, ssm_c_re, ssm_c_im,
           ssm_d, ssm_glu_w, ssm_glu_b, w_out, norm_ffn_g, w_ff1, w_ff2, norm_final_g):
    assert w_in.shape[0] == 1, "single-layer block"
    bsz, seq, _ = x.shape
    assert seq % CHUNK == 0 and CHUNK >= N_META
    n_chunks = seq // CHUNK
    g, h = N_GROUPS, SSM_GROUP

    x4 = x.reshape(bsz, n_chunks, CHUNK, D_MODEL)
    g_mix = norm_mix_g[0][None, :]
    w_in_b = w_in[0].astype(BF16)
    u_conv, u_col, u_conv_m, u_ssm_m = _in_proj_step(x4, meta_tokens, g_mix, w_in_b)

    conv_out = _conv_module(u_conv, u_conv_m, conv_w[0], conv_b[0][None, :],
                            conv_ln_g[0][None, :], conv_ln_b[0][None, :], bsz)

    ldt = jnp.broadcast_to(ssm_log_dt[0][..., None], ssm_lam_re[0].shape)
    flat = lambda a: a.reshape(2 * g, SSM_STATE)
    lbr, lbi, zr, zi = [a.reshape(2, g, SSM_STATE) for a in
                        _zoh(flat(ssm_lam_re[0]), flat(ssm_lam_im[0]), flat(ldt))]
    csc, c_col, b_col, c_row = _s5_operator_inputs(
        lbr, lbi, zr, zi, ssm_b_re[0], ssm_b_im[0], ssm_c_re[0], ssm_c_im[0])

    u_meta = jnp.transpose(u_ssm_m.reshape(N_META, g, h), (1, 0, 2)).reshape(g, 1, N_META * h)
    y_col = _ssm(u_col, u_meta, csc, c_col, b_col, c_row, n_chunks)

    out = _mix_ffn(x4, conv_out, y_col, u_col, ssm_d[0][:, None],
                   ssm_glu_w[0].T.astype(BF16), ssm_glu_b[0][:, None],
                   w_out[0].astype(BF16), norm_ffn_g[0][None, :], w_ff1[0].astype(BF16),
                   w_ff2[0].astype(BF16), norm_final_g[None, :])
    return out.reshape(bsz, seq, D_MODEL)
```

```python
import functools

import jax
import jax.numpy as jnp
from jax import lax
from jax.experimental import pallas as pl
from jax.experimental.pallas import tpu as pltpu

F32 = jnp.float32
BF16 = jnp.bfloat16

D_MODEL = 1024
N_META = 16
D_CONV = 512
D_SSM = 512
CONV_WIDTH = 31
CONV_PAD = CONV_WIDTH // 2
SSM_GROUP = 16
N_GROUPS = D_SSM // SSM_GROUP
SSM_STATE = 64
D_FF = 4096
NORM_EPS = 1e-5
LANES = 128
SUBLANES = 8

CHUNK = 32
CHUNK_LANES = CHUNK * SSM_GROUP
STATE_ROWS = 4 * SSM_STATE
DIR_ROWS = 2 * SSM_STATE
STEPS_PER_VREG = LANES // SSM_GROUP

VMEM_LIMIT_BYTES = 60 * 1024 * 1024


def _rms(x, g):
    return x * lax.rsqrt(jnp.mean(x * x, axis=-1, keepdims=True) + NORM_EPS) * g


def _params(n_axes=1):
    return pltpu.CompilerParams(dimension_semantics=("parallel",) * n_axes,
                                vmem_limit_bytes=VMEM_LIMIT_BYTES)


def _step_copies(hbm_ref, buf_ref, sem_ref, tile, slot, to_hbm):
    bsz = hbm_ref.shape[0]
    slab, b = tile // bsz, tile % bsz
    copies = []
    for r in range(SUBLANES):
        hbm = hbm_ref.at[b, :, slab * SUBLANES + r, :]
        vmem = buf_ref.at[slot, r]
        src, dst = (vmem, hbm) if to_hbm else (hbm, vmem)
        copies.append(pltpu.make_async_copy(src, dst, sem_ref.at[slot, r]))
    return copies


def _fetch_steps(x_hbm, xbuf, sem):
    tile = pl.program_id(0)
    slot = tile % 2

    @pl.when(tile == 0)
    def _():
        for cp in _step_copies(x_hbm, xbuf, sem, tile, slot, False):
            cp.start()

    @pl.when(tile + 1 < pl.num_programs(0))
    def _():
        for cp in _step_copies(x_hbm, xbuf, sem, tile + 1, 1 - slot, False):
            cp.start()

    for cp in _step_copies(x_hbm, xbuf, sem, tile, slot, False):
        cp.wait()
    return slot


def _step_tile_params():
    return pltpu.CompilerParams(dimension_semantics=("arbitrary",),
                                vmem_limit_bytes=VMEM_LIMIT_BYTES)


def _in_proj_step_kernel(x_hbm, meta_ref, g_ref, w_ref, uc_ref, ut_ref, ucm_ref, usm_ref,
                         xbuf, sem):
    n_chunks = x_hbm.shape[1]
    slot = _fetch_steps(x_hbm, xbuf, sem)

    @pl.when(pl.program_id(0) == 0)
    def _():
        zm = _rms(meta_ref[...], g_ref[...]).astype(BF16)
        pm = jnp.dot(zm, w_ref[...], preferred_element_type=F32)
        ucm_ref[...] = pm[:, :D_CONV] * jax.nn.sigmoid(pm[:, D_CONV:2 * D_CONV])
        usm_ref[...] = pm[:, 2 * D_CONV:]

    x = xbuf[slot].reshape(SUBLANES * n_chunks, D_MODEL)
    z = _rms(x, g_ref[...]).astype(BF16)
    p = jnp.dot(z, w_ref[...], preferred_element_type=F32)
    uc = p[:, :D_CONV] * jax.nn.sigmoid(p[:, D_CONV:2 * D_CONV])
    uc_ref[...] = uc.reshape(SUBLANES, n_chunks, D_CONV)
    ut = p[:, 2 * D_CONV:].T.astype(BF16)
    for r in range(SUBLANES):
        ut_ref[:, r * SSM_GROUP:(r + 1) * SSM_GROUP, :] = (
            ut[:, r * n_chunks:(r + 1) * n_chunks].reshape(N_GROUPS, SSM_GROUP, n_chunks))


def _in_proj_step(x4, meta, g, w_in):
    bsz, n_chunks = x4.shape[:2]
    rows = bsz * n_chunks
    return pl.pallas_call(
        _in_proj_step_kernel,
        grid=(CHUNK // SUBLANES * bsz,),
        in_specs=[
            pl.BlockSpec(memory_space=pl.ANY),
            pl.BlockSpec((N_META, D_MODEL), lambda t: (0, 0)),
            pl.BlockSpec((1, D_MODEL), lambda t: (0, 0)),
            pl.BlockSpec((D_MODEL, 2 * D_CONV + D_SSM), lambda t: (0, 0)),
        ],
        out_specs=[
            pl.BlockSpec((SUBLANES, n_chunks, D_CONV), lambda t: (t // bsz, t % bsz, 0)),
            pl.BlockSpec((N_GROUPS, SUBLANES * SSM_GROUP, n_chunks),
                         lambda t: (0, t // bsz, t % bsz)),
            pl.BlockSpec((N_META, D_CONV), lambda t: (0, 0)),
            pl.BlockSpec((N_META, D_SSM), lambda t: (0, 0)),
        ],
        out_shape=[
            jax.ShapeDtypeStruct((CHUNK, rows, D_CONV), F32),
            jax.ShapeDtypeStruct((N_GROUPS, CHUNK_LANES, rows), BF16),
            jax.ShapeDtypeStruct((N_META, D_CONV), F32),
            jax.ShapeDtypeStruct((N_META, D_SSM), F32),
        ],
        scratch_shapes=[pltpu.VMEM((2, SUBLANES, n_chunks, D_MODEL), F32),
                        pltpu.SemaphoreType.DMA((2, SUBLANES))],
        compiler_params=_step_tile_params(),
        name="in_proj",
    )(x4, meta, g, w_in)


CONV_GROUP = 4
CONV_ROWS = 64


def _conv_kernel(u_ref, um_ref, w_ref, cb_ref, lg_ref, lb_ref, o_ref, cat_ref, acc_ref):
    n_steps, n_chunks, _ = u_ref.shape
    cat_ref[CONV_PAD:CONV_PAD + n_steps] = u_ref[...]
    chunk = lax.broadcasted_iota(jnp.int32, (n_chunks, 1), 0)
    for i in range(CONV_PAD):
        s_prev = n_steps - CONV_PAD + i
        prev = pltpu.roll(u_ref[s_prev], 1, axis=0)
        meta_row = um_ref[N_META - CONV_PAD + i:N_META - CONV_PAD + i + 1, :]
        cat_ref[i] = jnp.where(chunk == 0, meta_row, prev)
        nxt = pltpu.roll(u_ref[i], n_chunks - 1, axis=0)
        cat_ref[n_steps + CONV_PAD + i] = jnp.where(chunk == n_chunks - 1, 0.0, nxt)

    def body(sg, carry):
        s0 = sg * CONV_GROUP
        for c in range(D_CONV // LANES):
            cols = slice(c * LANES, (c + 1) * LANES)
            n_sub = CONV_ROWS // SUBLANES

            def rows_body(rt, carry2):
                r0 = [pl.multiple_of(rt * CONV_ROWS + sub * SUBLANES, SUBLANES)
                      for sub in range(n_sub)]
                accs = [[jnp.zeros((SUBLANES, LANES), F32)] * CONV_GROUP for _ in range(n_sub)]
                for i in range(CONV_WIDTH + CONV_GROUP - 1):
                    ds = [cat_ref[s0 + i, pl.ds(r0[sub], SUBLANES), cols]
                          for sub in range(n_sub)]
                    for j in range(CONV_GROUP):
                        if 0 <= i - j < CONV_WIDTH:
                            tap = jnp.broadcast_to(w_ref[i - j:i - j + 1, cols],
                                                   (SUBLANES, LANES))
                            for sub in range(n_sub):
                                accs[sub][j] = accs[sub][j] + ds[sub] * tap
                for sub in range(n_sub):
                    for j in range(CONV_GROUP):
                        acc_ref[j, pl.ds(r0[sub], SUBLANES), cols] = accs[sub][j]
                return carry2

            lax.fori_loop(0, n_chunks // CONV_ROWS, rows_body, 0)
        for j in range(CONV_GROUP):
            y = acc_ref[j] + cb_ref[...]
            yc = y - jnp.mean(y, axis=-1, keepdims=True)
            yn = yc * lax.rsqrt(jnp.mean(yc * yc, axis=-1, keepdims=True) + NORM_EPS)
            yn = yn * lg_ref[...] + lb_ref[...]
            o_ref[s0 + j] = (yn * jax.nn.sigmoid(yn)).astype(BF16)
        return carry

    lax.fori_loop(0, n_steps // CONV_GROUP, body, 0)


def _conv_module(u_conv, u_conv_meta, conv_w, conv_b, ln_g, ln_b, bsz):
    n_steps, rows, _ = u_conv.shape
    n_chunks = rows // bsz
    vec = pl.BlockSpec((1, D_CONV), lambda b: (0, 0))
    blk = pl.BlockSpec((n_steps, n_chunks, D_CONV), lambda b: (0, b, 0))
    return pl.pallas_call(
        _conv_kernel,
        grid=(bsz,),
        in_specs=[
            blk,
            pl.BlockSpec((N_META, D_CONV), lambda b: (0, 0)),
            pl.BlockSpec((CONV_WIDTH, D_CONV), lambda b: (0, 0)),
            vec, vec, vec,
        ],
        out_specs=blk,
        out_shape=jax.ShapeDtypeStruct((n_steps, rows, D_CONV), BF16),
        scratch_shapes=[pltpu.VMEM((n_steps + 2 * CONV_PAD, n_chunks, D_CONV), F32),
                        pltpu.VMEM((CONV_GROUP, n_chunks, D_CONV), F32)],
        compiler_params=_params(),
        name="conv_module",
    )(u_conv, u_conv_meta, conv_w, conv_b, ln_g, ln_b)


def _zoh_kernel(lre_ref, lim_ref, ldt_ref, lbr_ref, lbi_ref, zr_ref, zi_ref):
    lre = lre_ref[...]
    lim = lim_ref[...]
    dt = jnp.exp(ldt_ref[...])
    ea = jnp.exp(lre * dt)
    lbr = ea * jnp.cos(lim * dt)
    lbi = ea * jnp.sin(lim * dt)
    nr = lbr - 1.0
    den = lre * lre + lim * lim
    lbr_ref[...] = lbr
    lbi_ref[...] = lbi
    zr_ref[...] = (nr * lre + lbi * lim) / den
    zi_ref[...] = (lbi * lre - nr * lim) / den


def _zoh(lam_re, lam_im, log_dt):
    shape = jax.ShapeDtypeStruct(lam_re.shape, F32)
    return pl.pallas_call(_zoh_kernel, out_shape=[shape] * 4, name="s5_zoh")(
        lam_re, lam_im, log_dt)


def _lanes(x, n):
    return jnp.concatenate([x] * (n // LANES), axis=1)


def _cmul(ar, ai, br, bi):
    return ar * br - ai * bi, ar * bi + ai * br


def _cpow(br, bi, expo, nbits):
    rr = jnp.ones(expo.shape, F32)
    ri = jnp.zeros(expo.shape, F32)
    for k in range(nbits):
        bit = ((expo >> k) & 1) == 1
        nr, ni = _cmul(rr, ri, br, bi)
        rr = jnp.where(bit, nr, rr)
        ri = jnp.where(bit, ni, ri)
        br, bi = _cmul(br, bi, br, bi)
    return rr, ri


def _build_chunk_ops(gi, csc_ref, cc_ref, bc_ref, cr_ref, um_ref, toep_ref, wend_ref,
                     wout_ref):
    t = CHUNK
    rows = STATE_ROWS
    n_cols = CHUNK_LANES // LANES
    lam = (jnp.broadcast_to(csc_ref[gi, :, 0:1], (rows, LANES)),
           jnp.broadcast_to(csc_ref[gi, :, 1:2], (rows, LANES)))
    zr = jnp.broadcast_to(csc_ref[gi, :, 2:3], (rows, LANES))
    zi = jnp.broadcast_to(csc_ref[gi, :, 3:4], (rows, LANES))
    sel = (lax.broadcasted_iota(jnp.int32, (SSM_GROUP, LANES), 1) % SSM_GROUP
           == lax.broadcasted_iota(jnp.int32, (SSM_GROUP, LANES), 0)).astype(F32)

    def tile(ref):
        return jnp.dot(ref[gi], sel, precision=lax.Precision.HIGHEST,
                       preferred_element_type=F32)

    def swap_re_im(a):
        p = SSM_STATE
        return jnp.concatenate([a[p:2 * p], a[0:p], a[3 * p:], a[2 * p:3 * p]], axis=0)

    c_same, b_same = tile(cc_ref), tile(bc_ref)
    c_swap, b_swap = swap_re_im(c_same), swap_re_im(b_same)
    row = lax.broadcasted_iota(jnp.int32, (rows, LANES), 0)
    is_re = (row // SSM_STATE) % 2 == 0
    sgn = jnp.where(is_re, -1.0, 1.0)
    ca = jnp.where(is_re, c_same, -c_same)
    cb = -c_swap
    bb_same = zr * b_same + sgn * zi * b_swap
    sbb_swap = sgn * (zr * b_swap - sgn * zi * b_same)

    pows = {1: lam}
    k = 1
    while k < t:
        pows[2 * k] = _cmul(*pows[k], *pows[k])
        k *= 2
    step_bits = STEPS_PER_VREG.bit_length() - 1
    fwd = slice(0, DIR_ROWS)
    bwd = slice(DIR_ROWS, rows)
    every = slice(0, rows)

    def rows_of(v, rs):
        return v[0][rs], v[1][rs]

    def column(base, e, rs):
        out = rows_of(base, rs)
        for bit, val in pows.items():
            if e & bit:
                out = _cmul(*out, *rows_of(val, rs))
        return out

    def times_b(p, rs):
        return p[0] * bb_same[rs] + p[1] * sbb_swap[rs]

    def times_c(p, rs):
        return ca[rs] * p[0] + cb[rs] * p[1]

    i8 = lax.broadcasted_iota(jnp.int32, (rows, LANES), 1) // SSM_GROUP
    asc0 = _cpow(*lam, i8, step_bits)
    desc0 = _cpow(*lam, STEPS_PER_VREG - 1 - i8, step_bits)
    asc1_0 = _cmul(*asc0, *lam)
    desc1_0 = _cmul(*desc0, *lam)

    wend_f, wend_b, lag_b, wout_f, wout_b = [], [], [], [], []
    for q in range(n_cols):
        e_asc = STEPS_PER_VREG * q
        e_desc = STEPS_PER_VREG * (n_cols - 1 - q)
        asc1_q = column(asc1_0, e_asc, every)
        wend_f.append(times_b(column(desc0, e_desc, fwd), fwd))
        wend_b.append(times_b(column(asc0, e_asc, bwd), bwd))
        lag_b.append(times_b(rows_of(asc1_q, bwd), bwd))
        wout_f.append(times_c(rows_of(asc1_q, fwd), fwd))
        wout_b.append(times_c(column(desc1_0, e_desc, bwd), bwd))
        cols = slice(q * LANES, (q + 1) * LANES)
        wend_ref[gi, fwd, cols] = wend_f[q].astype(BF16)
        wend_ref[gi, bwd, cols] = wend_b[q].astype(BF16)

    meta_cols = N_META * SSM_GROUP // LANES
    x0 = sum(jnp.sum(wend_f[n_cols - meta_cols + m] * um_ref[gi, :, m * LANES:(m + 1) * LANES],
                     axis=1, keepdims=True) for m in range(meta_cols))

    lane = lax.broadcasted_iota(jnp.int32, (DIR_ROWS, LANES), 1)
    lag0_b = jnp.where(lane >= LANES - SSM_GROUP, bb_same[bwd], 0.0)
    zero = jnp.zeros((DIR_ROWS, LANES), F32)
    bcat = jnp.concatenate(
        [jnp.concatenate(wend_f + [zero] * n_cols, axis=1),
         jnp.concatenate([zero] * (n_cols - 1) + [lag0_b] + lag_b, axis=1)], axis=0)
    lane_r = lax.broadcasted_iota(jnp.int32, (SSM_GROUP, rows), 1)
    c2 = jnp.where((lane_r // SSM_STATE) % 2 == 0, cr_ref[gi], -cr_ref[gi])
    kk = jnp.dot(c2, bcat, precision=lax.Precision.HIGHEST,
                 preferred_element_type=F32)
    for tt in range(t):
        off = (t - 1 - tt) * SSM_GROUP
        toep_ref[gi, tt * SSM_GROUP:(tt + 1) * SSM_GROUP, :] = (
            kk[:, off:off + CHUNK_LANES].astype(BF16))

    wout = jnp.concatenate([jnp.concatenate(wout_f, axis=1),
                            jnp.concatenate(wout_b, axis=1)], axis=0)
    wout_ref[gi] = wout.T.astype(BF16)
    return pows[t], x0


SSM_GROUPS_PER_STEP = 2


def _chunk_scan(er, ei, ar, ai, x0, chunk, n_chunks, forward):
    width = er.shape[1]
    xr, xi = er, ei
    if x0 is not None:
        fr, fi = _cmul(ar, ai, x0[0], x0[1])
        xr = xr + jnp.where(chunk == 0, _lanes(fr, width), 0.0)
        xi = xi + jnp.where(chunk == 0, _lanes(fi, width), 0.0)

    def shifted(v, sh):
        if forward:
            return jnp.where(chunk >= sh, pltpu.roll(v, sh, axis=1), 0.0)
        return jnp.where(chunk < n_chunks - sh, pltpu.roll(v, width - sh, axis=1), 0.0)

    sh = 1
    while sh < n_chunks:
        sr, si = shifted(xr, sh), shifted(xi, sh)
        wr, wi = _lanes(ar, width), _lanes(ai, width)
        xr, xi = xr + wr * sr - wi * si, xi + wr * si + wi * sr
        ar, ai = _cmul(ar, ai, ar, ai)
        sh *= 2
    inr, ini = shifted(xr, 1), shifted(xi, 1)
    if x0 is not None:
        inr = jnp.where(chunk == 0, x0[0], inr)
        ini = jnp.where(chunk == 0, x0[1], ini)
    return inr, ini


def _ssm_kernel(u_ref, um_ref, csc_ref, cc_ref, bc_ref, cr_ref, y_ref, toep_ref, wend_ref,
                wout_ref, yi_ref, *, n_chunks):
    for gi in range(SSM_GROUPS_PER_STEP):
        (lam_r, lam_i), x0 = _build_chunk_ops(gi, csc_ref, cc_ref, bc_ref, cr_ref, um_ref,
                                              toep_ref, wend_ref, wout_ref)
        u = u_ref[gi]
        width = u.shape[1]
        e = jnp.dot(wend_ref[gi], u, preferred_element_type=F32)
        yi_ref[gi] = jnp.dot(toep_ref[gi], u, preferred_element_type=F32)
        chunk =lax.broadcasted_iota(jnp.int32, (1, width), 1) % n_chunks
        p = SSM_STATE
        ar, ai = lam_r, lam_i
        f_in = _chunk_scan(e[0:p], e[p:2 * p], ar[0:p], ai[0:p], (x0[0:p], x0[p:2 * p]),
                           chunk, n_chunks, True)
        b_in = _chunk_scan(e[2 * p:3 * p], e[3 * p:], ar[2 * p:3 * p], ai[2 * p:3 * p],
                           None, chunk, n_chunks, False)
        xin = jnp.concatenate([f_in[0], f_in[1], b_in[0], b_in[1]], axis=0).astype(BF16)
        y = yi_ref[gi] + jnp.dot(wout_ref[gi], xin, preferred_element_type=F32)
        y_ref[gi] = y.astype(BF16)


def _ssm(u_col, u_meta, csc, c_col, b_col, c_row, n_chunks):
    g, _, width = u_col.shape
    per = SSM_GROUPS_PER_STEP
    return pl.pallas_call(
        functools.partial(_ssm_kernel, n_chunks=n_chunks),
        grid=(g // per,),
        in_specs=[
            pl.BlockSpec((per, CHUNK_LANES, width), lambda i: (i, 0, 0)),
            pl.BlockSpec((per, 1, N_META * SSM_GROUP), lambda i: (i, 0, 0)),
            pl.BlockSpec((per, STATE_ROWS, 4), lambda i: (i, 0, 0)),
            pl.BlockSpec((per, STATE_ROWS, SSM_GROUP), lambda i: (i, 0, 0)),
            pl.BlockSpec((per, STATE_ROWS, SSM_GROUP), lambda i: (i, 0, 0)),
            pl.BlockSpec((per, SSM_GROUP, STATE_ROWS), lambda i: (i, 0, 0)),
        ],
        out_specs=pl.BlockSpec((per, CHUNK_LANES, width), lambda i: (i, 0, 0)),
        out_shape=jax.ShapeDtypeStruct((g, CHUNK_LANES, width), BF16),
        scratch_shapes=[
            pltpu.VMEM((per, CHUNK_LANES, CHUNK_LANES), BF16),
            pltpu.VMEM((per, STATE_ROWS, CHUNK_LANES), BF16),
            pltpu.VMEM((per, CHUNK_LANES, STATE_ROWS), BF16),
            pltpu.VMEM((per, CHUNK_LANES, width), F32),
        ],
        compiler_params=_params(),
        name="s5_mixer",
    )(u_col, u_meta, csc, c_col, b_col, c_row)


FF_CHUNK = 1024
MIX_PER = 4


MIX_SLOTS = 3


def _mix_ffn_kernel(x_hbm, co_ref, y_ref, ut_ref, d_ref, gwt_ref, gb_ref, wo_ref,
                    gf_ref, w1_ref, w2_ref, gl_ref, o_hbm, buf, sem_in, sem_out,
                    mixed_ref, h_ref, z_ref, a_ref):
    n_chunks = x_hbm.shape[1]
    rows = SUBLANES * n_chunks
    step = pl.program_id(0)
    n_tiles = pl.num_programs(0) - 1
    tile = step - 1

    def mix_stages(dst):
        per = MIX_PER
        n_half = SUBLANES // per
        state = {}

        def col(ref, k):
            return jnp.concatenate(
                [ref[:, r * SSM_GROUP:(r + 1) * SSM_GROUP, :].reshape(D_SSM, n_chunks)
                 for r in range(k * per, (k + 1) * per)], axis=1).astype(F32)

        def gelu_stage(k):
            def run():
                state["ge", k] = jax.nn.gelu(col(y_ref, k) + d_ref[...] * col(ut_ref, k))
            return run

        def glu_stage(k):
            def run():
                ge = state["ge", k]
                gate = (jnp.dot(gwt_ref[...], ge.astype(BF16), preferred_element_type=F32)
                        + gb_ref[...])
                state["s5t", k] = ge * jax.nn.sigmoid(gate)
            return run

        def store_stage(k):
            def run():
                rs = slice(k * per * n_chunks, (k + 1) * per * n_chunks)
                mixed_ref[dst, rs, :D_CONV] = co_ref[k * per:(k + 1) * per].reshape(
                    per * n_chunks, D_CONV)
                mixed_ref[dst, rs, D_CONV:] = state["s5t", k].T.astype(BF16)
            return run

        return [f(k) for f in (gelu_stage, glu_stage, store_stage) for k in range(n_half)]

    @pl.when(step == 0)
    def _():
        for cp in _step_copies(x_hbm, buf, sem_in, 0, 0, False):
            cp.start()
        for stage in mix_stages(0):
            stage()

    @pl.when(step >= 1)
    def _():
        slot = tile % MIX_SLOTS
        nxt = (tile + 1) % MIX_SLOTS

        @pl.when(tile >= 2)
        def _():
            for cp in _step_copies(o_hbm, buf, sem_out, tile - 2, nxt, True):
                cp.wait()

        @pl.when(tile + 1 < n_tiles)
        def _():
            for cp in _step_copies(x_hbm, buf, sem_in, tile + 1, nxt, False):
                cp.start()

        for cp in _step_copies(x_hbm, buf, sem_in, tile, slot, False):
            cp.wait()

        stages = mix_stages(step % 2)
        half = D_FF // 2
        n_slots = D_FF // FF_CHUNK + 2
        n_stages = len(stages)
        done = [0, 0]

        def fill_slot():
            done[0] += 1
            while done[1] * n_slots < done[0] * n_stages:
                stages.pop(0)()
                done[1] += 1

        h = buf[slot].reshape(rows, D_MODEL) + jnp.dot(
            mixed_ref[tile % 2], wo_ref[...], preferred_element_type=F32)
        z_ref[...] = _rms(h, gf_ref[...]).astype(BF16)
        h_ref[...] = h
        for k in range(2):
            for j in range(half // FF_CHUNK):
                cols = slice(k * half + j * FF_CHUNK, k * half + (j + 1) * FF_CHUNK)
                a = jnp.dot(z_ref[...], w1_ref[:, cols], preferred_element_type=F32)
                a_ref[:, j * FF_CHUNK:(j + 1) * FF_CHUNK] = (
                    jnp.square(jnp.maximum(a, 0.0)).astype(BF16))
                fill_slot()
            h_ref[...] += jnp.dot(a_ref[...], w2_ref[k * half:(k + 1) * half, :],
                                  preferred_element_type=F32)
            fill_slot()
        assert not stages
        buf[slot] = _rms(h_ref[...], gl_ref[...]).reshape(SUBLANES, n_chunks, D_MODEL)

        for cp in _step_copies(o_hbm, buf, sem_out, tile, slot, True):
            cp.start()

        @pl.when(tile == n_tiles - 1)
        def _():
            for cp in (_step_copies(o_hbm, buf, sem_out, tile - 1, (tile - 1) % MIX_SLOTS, True)
                       + _step_copies(o_hbm, buf, sem_out, tile, slot, True)):
                cp.wait()


def _mix_ffn(x4, conv_out, y_col, u_col, d_col, glu_wt, glu_b_col, w_out, g_ffn, w1, w2,
             g_final):
    bsz, n_chunks = x4.shape[:2]

    def const(shape):
        return pl.BlockSpec(shape, lambda t: (0, 0), pipeline_mode=pl.Buffered(1))

    n_tiles = CHUNK // SUBLANES * bsz
    hbm = pl.BlockSpec(memory_space=pl.ANY)

    def slab(s):
        return jnp.minimum(s, n_tiles - 1) // bsz

    def batch(s):
        return jnp.minimum(s, n_tiles - 1) % bsz

    col_blk = pl.BlockSpec((N_GROUPS, SUBLANES * SSM_GROUP, n_chunks),
                           lambda s: (0, slab(s), batch(s)))
    step_sems = pltpu.SemaphoreType.DMA((MIX_SLOTS, SUBLANES))
    return pl.pallas_call(
        _mix_ffn_kernel,
        grid=(n_tiles + 1,),
        in_specs=[
            hbm,
            pl.BlockSpec((SUBLANES, n_chunks, D_CONV), lambda s: (slab(s), batch(s), 0)),
            col_blk,
            col_blk,
            const((D_SSM, 1)),
            const((D_SSM, D_SSM)),
            const((D_SSM, 1)),
            const((D_MODEL, D_MODEL)),
            const((1, D_MODEL)),
            const((D_MODEL, D_FF)),
            const((D_FF, D_MODEL)),
            const((1, D_MODEL)),
        ],
        out_specs=hbm,
        out_shape=jax.ShapeDtypeStruct(x4.shape, F32),
        scratch_shapes=[
            pltpu.VMEM((MIX_SLOTS, SUBLANES, n_chunks, D_MODEL), F32),
            step_sems, step_sems,
            pltpu.VMEM((2, SUBLANES * n_chunks, D_MODEL), BF16),
            pltpu.VMEM((SUBLANES * n_chunks, D_MODEL), F32),
            pltpu.VMEM((SUBLANES * n_chunks, D_MODEL), BF16),
            pltpu.VMEM((SUBLANES * n_chunks, D_FF // 2), BF16),
        ],
        compiler_params=_step_tile_params(),
        name="mix_ffn",
    )(x4, conv_out, y_col, u_col, d_col, glu_wt, glu_b_col, w_out, g_ffn, w1, w2, g_final)


def _s5_operator_inputs(lbr, lbi, zr, zi, b_re, b_im, c_re, c_im):
    g, p, h = N_GROUPS, SSM_STATE, SSM_GROUP

    sc = jnp.transpose(jnp.stack([lbr, lbi, zr, zi], axis=-1), (1, 0, 2, 3))
    csc = jnp.broadcast_to(sc[:, :, None], (g, 2, 2, p, 4)).reshape(g, 4 * p, 4)
    c = jnp.stack([c_re, c_im], axis=0)
    b = jnp.stack([b_re, b_im], axis=0)
    c_col = jnp.transpose(c, (2, 1, 0, 4, 3)).reshape(g, 4 * p, h)
    b_col = jnp.transpose(b, (2, 1, 0, 3, 4)).reshape(g, 4 * p, h)
    c_row = jnp.transpose(c, (2, 3, 1, 0, 4)).reshape(g, h, 4 * p)
    return csc, c_col, b_col, c_row


def kernel(x, meta_tokens, norm_mix_g, w_in, conv_w, conv_b, conv_ln_g, conv_ln_b,
           ssm_lam_re, ssm_lam_im, ssm_log_dt, ssm_b_re, ssm_b_im, ssm_c_re, ssm_c_im,
           ssm_d, ssm_glu_w, ssm_glu_b, w_out, norm_ffn_g, w_ff1, w_ff2, norm_final_g):
    assert w_in.shape[0] == 1, "single-layer block"
    bsz, seq, _ = x.shape
    assert seq % CHUNK == 0 and CHUNK >= N_META
    n_chunks = seq // CHUNK
    g, h = N_GROUPS, SSM_GROUP

    x4 = x.reshape(bsz, n_chunks, CHUNK, D_MODEL)
    g_mix = norm_mix_g[0][None, :]
    w_in_b = w_in[0].astype(BF16)
    u_conv, u_col, u_conv_m, u_ssm_m = _in_proj_step(x4, meta_tokens, g_mix, w_in_b)

    conv_out = _conv_module(u_conv, u_conv_m, conv_w[0], conv_b[0][None, :],
                            conv_ln_g[0][None, :], conv_ln_b[0][None, :], bsz)

    ldt = jnp.broadcast_to(ssm_log_dt[0][..., None], ssm_lam_re[0].shape)
    flat = lambda a: a.reshape(2 * g, SSM_STATE)
    lbr, lbi, zr, zi = [a.reshape(2, g, SSM_STATE) for a in
                        _zoh(flat(ssm_lam_re[0]), flat(ssm_lam_im[0]), flat(ldt))]
    csc, c_col, b_col, c_row = _s5_operator_inputs(
        lbr, lbi, zr, zi, ssm_b_re[0], ssm_b_im[0], ssm_c_re[0], ssm_c_im[0])

    u_meta = jnp.transpose(u_ssm_m.reshape(N_META, g, h), (1, 0, 2)).reshape(g, 1, N_META * h)
    y_col = _ssm(u_col, u_meta, csc, c_col, b_col, c_row, n_chunks)

    out = _mix_ffn(x4, conv_out, y_col, u_col, ssm_d[0][:, None],
                   ssm_glu_w[0].T.astype(BF16), ssm_glu_b[0][:, None],
                   w_out[0].astype(BF16), norm_ffn_g[0][None, :], w_ff1[0].astype(BF16),
                   w_ff2[0].astype(BF16), norm_final_g[None, :])
    return out.reshape(bsz, seq, D_MODEL)
```

```python
import functools

import jax
import jax.numpy as jnp
from jax import lax
from jax.experimental import pallas as pl
from jax.experimental.pallas import tpu as pltpu

F32 = jnp.float32
BF16 = jnp.bfloat16

D_MODEL = 1024
N_META = 16
D_CONV = 512
D_SSM = 512
CONV_WIDTH = 31
CONV_PAD = CONV_WIDTH // 2
SSM_GROUP = 16
N_GROUPS = D_SSM // SSM_GROUP
SSM_STATE = 64
D_FF = 4096
NORM_EPS = 1e-5
LANES = 128
SUBLANES = 8

CHUNK = 32
CHUNK_LANES = CHUNK * SSM_GROUP
STATE_ROWS = 4 * SSM_STATE
DIR_ROWS = 2 * SSM_STATE
STEPS_PER_VREG = LANES // SSM_GROUP

VMEM_LIMIT_BYTES = 60 * 1024 * 1024


def _rms(x, g):
    return x * lax.rsqrt(jnp.mean(x * x, axis=-1, keepdims=True) + NORM_EPS) * g


def _params(n_axes=1):
    return pltpu.CompilerParams(dimension_semantics=("parallel",) * n_axes,
                                vmem_limit_bytes=VMEM_LIMIT_BYTES)


def _step_copies(hbm_ref, buf_ref, sem_ref, tile, slot, to_hbm):
    bsz = hbm_ref.shape[0]
    slab, b = tile // bsz, tile % bsz
    copies = []
    for r in range(SUBLANES):
        hbm = hbm_ref.at[b, :, slab * SUBLANES + r, :]
        vmem = buf_ref.at[slot, r]
        src, dst = (vmem, hbm) if to_hbm else (hbm, vmem)
        copies.append(pltpu.make_async_copy(src, dst, sem_ref.at[slot, r]))
    return copies


def _fetch_steps(x_hbm, xbuf, sem):
    tile = pl.program_id(0)
    slot = tile % 2

    @pl.when(tile == 0)
    def _():
        for cp in _step_copies(x_hbm, xbuf, sem, tile, slot, False):
            cp.start()

    @pl.when(tile + 1 < pl.num_programs(0))
    def _():
        for cp in _step_copies(x_hbm, xbuf, sem, tile + 1, 1 - slot, False):
            cp.start()

    for cp in _step_copies(x_hbm, xbuf, sem, tile, slot, False):
        cp.wait()
    return slot


def _step_tile_params():
    return pltpu.CompilerParams(dimension_semantics=("arbitrary",),
                                vmem_limit_bytes=VMEM_LIMIT_BYTES)


def _in_proj_step_kernel(x_hbm, meta_ref, g_ref, w_ref, uc_ref, ut_ref, ucm_ref, usm_ref,
                         xbuf, sem):
    n_chunks = x_hbm.shape[1]
    slot = _fetch_steps(x_hbm, xbuf, sem)

    @pl.when(pl.program_id(0) == 0)
    def _():
        zm = _rms(meta_ref[...], g_ref[...]).astype(BF16)
        pm = jnp.dot(zm, w_ref[...], preferred_element_type=F32)
        ucm_ref[...] = pm[:, :D_CONV] * jax.nn.sigmoid(pm[:, D_CONV:2 * D_CONV])
        usm_ref[...] = pm[:, 2 * D_CONV:]

    x = xbuf[slot].reshape(SUBLANES * n_chunks, D_MODEL)
    z = _rms(x, g_ref[...]).astype(BF16)
    p = jnp.dot(z, w_ref[...], preferred_element_type=F32)
    uc = p[:, :D_CONV] * jax.nn.sigmoid(p[:, D_CONV:2 * D_CONV])
    uc_ref[...] = uc.reshape(SUBLANES, n_chunks, D_CONV)
    ut = p[:, 2 * D_CONV:].T.astype(BF16)
    for r in range(SUBLANES):
        ut_ref[:, r * SSM_GROUP:(r + 1) * SSM_GROUP, :] = (
            ut[:, r * n_chunks:(r + 1) * n_chunks].reshape(N_GROUPS, SSM_GROUP, n_chunks))


def _in_proj_step(x4, meta, g, w_in):
    bsz, n_chunks = x4.shape[:2]
    rows = bsz * n_chunks
    return pl.pallas_call(
        _in_proj_step_kernel,
        grid=(CHUNK // SUBLANES * bsz,),
        in_specs=[
            pl.BlockSpec(memory_space=pl.ANY),
            pl.BlockSpec((N_META, D_MODEL), lambda t: (0, 0)),
            pl.BlockSpec((1, D_MODEL), lambda t: (0, 0)),
            pl.BlockSpec((D_MODEL, 2 * D_CONV + D_SSM), lambda t: (0, 0)),
        ],
        out_specs=[
            pl.BlockSpec((SUBLANES, n_chunks, D_CONV), lambda t: (t // bsz, t % bsz, 0)),
            pl.BlockSpec((N_GROUPS, SUBLANES * SSM_GROUP, n_chunks),
                         lambda t: (0, t // bsz, t % bsz)),
            pl.BlockSpec((N_META, D_CONV), lambda t: (0, 0)),
            pl.BlockSpec((N_META, D_SSM), lambda t: (0, 0)),
        ],
        out_shape=[
            jax.ShapeDtypeStruct((CHUNK, rows, D_CONV), F32),
            jax.ShapeDtypeStruct((N_GROUPS, CHUNK_LANES, rows), BF16),
            jax.ShapeDtypeStruct((N_META, D_CONV), F32),
            jax.ShapeDtypeStruct((N_META, D_SSM), F32),
        ],
        scratch_shapes=[pltpu.VMEM((2, SUBLANES, n_chunks, D_MODEL), F32),
                        pltpu.SemaphoreType.DMA((2, SUBLANES))],
        compiler_params=_step_tile_params(),
        name="in_proj",
    )(x4, meta, g, w_in)


CONV_GROUP = 4
CONV_ROWS = 64


def _conv_kernel(u_ref, um_ref, w_ref, cb_ref, lg_ref, lb_ref, o_ref, cat_ref, acc_ref):
    n_steps, n_chunks, _ = u_ref.shape
    cat_ref[CONV_PAD:CONV_PAD + n_steps] = u_ref[...]
    chunk = lax.broadcasted_iota(jnp.int32, (n_chunks, 1), 0)
    for i in range(CONV_PAD):
        s_prev = n_steps - CONV_PAD + i
        prev = pltpu.roll(u_ref[s_prev], 1, axis=0)
        meta_row = um_ref[N_META - CONV_PAD + i:N_META - CONV_PAD + i + 1, :]
        cat_ref[i] = jnp.where(chunk == 0, meta_row, prev)
        nxt = pltpu.roll(u_ref[i], n_chunks - 1, axis=0)
        cat_ref[n_steps + CONV_PAD + i] = jnp.where(chunk == n_chunks - 1, 0.0, nxt)

    def body(sg, carry):
        s0 = sg * CONV_GROUP
        for c in range(D_CONV // LANES):
            cols = slice(c * LANES, (c + 1) * LANES)
            n_sub = CONV_ROWS // SUBLANES

            def rows_body(rt, carry2):
                r0 = [pl.multiple_of(rt * CONV_ROWS + sub * SUBLANES, SUBLANES)
                      for sub in range(n_sub)]
                accs = [[jnp.zeros((SUBLANES, LANES), F32)] * CONV_GROUP for _ in range(n_sub)]
                for i in range(CONV_WIDTH + CONV_GROUP - 1):
                    ds = [cat_ref[s0 + i, pl.ds(r0[sub], SUBLANES), cols]
                          for sub in range(n_sub)]
                    for j in range(CONV_GROUP):
                        if 0 <= i - j < CONV_WIDTH:
                            tap = jnp.broadcast_to(w_ref[i - j:i - j + 1, cols],
                                                   (SUBLANES, LANES))
                            for sub in range(n_sub):
                                accs[sub][j] = accs[sub][j] + ds[sub] * tap
                for sub in range(n_sub):
                    for j in range(CONV_GROUP):
                        acc_ref[j, pl.ds(r0[sub], SUBLANES), cols] = accs[sub][j]
                return carry2

            lax.fori_loop(0, n_chunks // CONV_ROWS, rows_body, 0)
        for j in range(CONV_GROUP):
            y = acc_ref[j] + cb_ref[...]
            yc = y - jnp.mean(y, axis=-1, keepdims=True)
            yn = yc * lax.rsqrt(jnp.mean(yc * yc, axis=-1, keepdims=True) + NORM_EPS)
            yn = yn * lg_ref[...] + lb_ref[...]
            o_ref[s0 + j] = (yn * jax.nn.sigmoid(yn)).astype(BF16)
        return carry

    lax.fori_loop(0, n_steps // CONV_GROUP, body, 0)


def _conv_module(u_conv, u_conv_meta, conv_w, conv_b, ln_g, ln_b, bsz):
    n_steps, rows, _ = u_conv.shape
    n_chunks = rows // bsz
    vec = pl.BlockSpec((1, D_CONV), lambda b: (0, 0))
    blk = pl.BlockSpec((n_steps, n_chunks, D_CONV), lambda b: (0, b, 0))
    return pl.pallas_call(
        _conv_kernel,
        grid=(bsz,),
        in_specs=[
            blk,
            pl.BlockSpec((N_META, D_CONV), lambda b: (0, 0)),
            pl.BlockSpec((CONV_WIDTH, D_CONV), lambda b: (0, 0)),
            vec, vec, vec,
        ],
        out_specs=blk,
        out_shape=jax.ShapeDtypeStruct((n_steps, rows, D_CONV), BF16),
        scratch_shapes=[pltpu.VMEM((n_steps + 2 * CONV_PAD, n_chunks, D_CONV), F32),
                        pltpu.VMEM((CONV_GROUP, n_chunks, D_CONV), F32)],
        compiler_params=_params(),
        name="conv_module",
    )(u_conv, u_conv_meta, conv_w, conv_b, ln_g, ln_b)


def _zoh_kernel(lre_ref, lim_ref, ldt_ref, lbr_ref, lbi_ref, zr_ref, zi_ref):
    lre = lre_ref[...]
    lim = lim_ref[...]
    dt = jnp.exp(ldt_ref[...])
    ea = jnp.exp(lre * dt)
    lbr = ea * jnp.cos(lim * dt)
    lbi = ea * jnp.sin(lim * dt)
    nr = lbr - 1.0
    den = lre * lre + lim * lim
    lbr_ref[...] = lbr
    lbi_ref[...] = lbi
    zr_ref[...] = (nr * lre + lbi * lim) / den
    zi_ref[...] = (lbi * lre - nr * lim) / den


def _zoh(lam_re, lam_im, log_dt):
    shape = jax.ShapeDtypeStruct(lam_re.shape, F32)
    return pl.pallas_call(_zoh_kernel, out_shape=[shape] * 4, name="s5_zoh")(
        lam_re, lam_im, log_dt)


def _lanes(x, n):
    return jnp.concatenate([x] * (n // LANES), axis=1)


def _cmul(ar, ai, br, bi):
    return ar * br - ai * bi, ar * bi + ai * br


def _cpow(br, bi, expo, nbits):
    rr = jnp.ones(expo.shape, F32)
    ri = jnp.zeros(expo.shape, F32)
    for k in range(nbits):
        bit = ((expo >> k) & 1) == 1
        nr, ni = _cmul(rr, ri, br, bi)
        rr = jnp.where(bit, nr, rr)
        ri = jnp.where(bit, ni, ri)
        br, bi = _cmul(br, bi, br, bi)
    return rr, ri


def _build_chunk_ops(gi, csc_ref, cc_ref, bc_ref, cr_ref, um_ref, toep_ref, wend_ref,
                     wout_ref):
    t = CHUNK
    rows = STATE_ROWS
    n_cols = CHUNK_LANES // LANES
    lam = (jnp.broadcast_to(csc_ref[gi, :, 0:1], (rows, LANES)),
           jnp.broadcast_to(csc_ref[gi, :, 1:2], (rows, LANES)))
    zr = jnp.broadcast_to(csc_ref[gi, :, 2:3], (rows, LANES))
    zi = jnp.broadcast_to(csc_ref[gi, :, 3:4], (rows, LANES))
    sel = (lax.broadcasted_iota(jnp.int32, (SSM_GROUP, LANES), 1) % SSM_GROUP
           == lax.broadcasted_iota(jnp.int32, (SSM_GROUP, LANES), 0)).astype(F32)

    def tile(ref):
        return jnp.dot(ref[gi], sel, precision=lax.Precision.HIGHEST,
                       preferred_element_type=F32)

    def swap_re_im(a):
        p = SSM_STATE
        return jnp.concatenate([a[p:2 * p], a[0:p], a[3 * p:], a[2 * p:3 * p]], axis=0)

    c_same, b_same = tile(cc_ref), tile(bc_ref)
    c_swap, b_swap = swap_re_im(c_same), swap_re_im(b_same)
    row = lax.broadcasted_iota(jnp.int32, (rows, LANES), 0)
    is_re = (row // SSM_STATE) % 2 == 0
    sgn = jnp.where(is_re, -1.0, 1.0)
    ca = jnp.where(is_re, c_same, -c_same)
    cb = -c_swap
    bb_same = zr * b_same + sgn * zi * b_swap
    sbb_swap = sgn * (zr * b_swap - sgn * zi * b_same)

    pows = {1: lam}
    k = 1
    while k < t:
        pows[2 * k] = _cmul(*pows[k], *pows[k])
        k *= 2
    step_bits = STEPS_PER_VREG.bit_length() - 1
    fwd = slice(0, DIR_ROWS)
    bwd = slice(DIR_ROWS, rows)
    every = slice(0, rows)

    def rows_of(v, rs):
        return v[0][rs], v[1][rs]

    def column(base, e, rs):
        out = rows_of(base, rs)
        for bit, val in pows.items():
            if e & bit:
                out = _cmul(*out, *rows_of(val, rs))
        return out

    def times_b(p, rs):
        return p[0] * bb_same[rs] + p[1] * sbb_swap[rs]

    def times_c(p, rs):
        return ca[rs] * p[0] + cb[rs] * p[1]

    i8 = lax.broadcasted_iota(jnp.int32, (rows, LANES), 1) // SSM_GROUP
    asc0 = _cpow(*lam, i8, step_bits)
    desc0 = _cpow(*lam, STEPS_PER_VREG - 1 - i8, step_bits)
    asc1_0 = _cmul(*asc0, *lam)
    desc1_0 = _cmul(*desc0, *lam)

    wend_f, wend_b, lag_b, wout_f, wout_b = [], [], [], [], []
    for q in range(n_cols):
        e_asc = STEPS_PER_VREG * q
        e_desc = STEPS_PER_VREG * (n_cols - 1 - q)
        asc1_q = column(asc1_0, e_asc, every)
        wend_f.append(times_b(column(desc0, e_desc, fwd), fwd))
        wend_b.append(times_b(column(asc0, e_asc, bwd), bwd))
        lag_b.append(times_b(rows_of(asc1_q, bwd), bwd))
        wout_f.append(times_c(rows_of(asc1_q, fwd), fwd))
        wout_b.append(times_c(column(desc1_0, e_desc, bwd), bwd))
        cols = slice(q * LANES, (q + 1) * LANES)
        wend_ref[gi, fwd, cols] = wend_f[q].astype(BF16)
        wend_ref[gi, bwd, cols] = wend_b[q].astype(BF16)

    meta_cols = N_META * SSM_GROUP // LANES
    x0 = sum(jnp.sum(wend_f[n_cols - meta_cols + m] * um_ref[gi, :, m * LANES:(m + 1) * LANES],
                     axis=1, keepdims=True) for m in range(meta_cols))

    lane = lax.broadcasted_iota(jnp.int32, (DIR_ROWS, LANES), 1)
    lag0_b = jnp.where(lane >= LANES - SSM_GROUP, bb_same[bwd], 0.0)
    zero = jnp.zeros((DIR_ROWS, LANES), F32)
    bcat = jnp.concatenate(
        [jnp.concatenate(wend_f + [zero] * n_cols, axis=1),
         jnp.concatenate([zero] * (n_cols - 1) + [lag0_b] + lag_b, axis=1)], axis=0)
    lane_r = lax.broadcasted_iota(jnp.int32, (SSM_GROUP, rows), 1)
    c2 = jnp.where((lane_r // SSM_STATE) % 2 == 0, cr_ref[gi], -cr_ref[gi])
    kk = jnp.dot(c2, bcat, precision=lax.Precision.HIGHEST,
                 preferred_element_type=F32)
    for tt in range(t):
        off = (t - 1 - tt) * SSM_GROUP
        toep_ref[gi, tt * SSM_GROUP:(tt + 1) * SSM_GROUP, :] = (
            kk[:, off:off + CHUNK_LANES].astype(BF16))

    wout = jnp.concatenate([jnp.concatenate(wout_f, axis=1),
                            jnp.concatenate(wout_b, axis=1)], axis=0)
    wout_ref[gi] = wout.T.astype(BF16)
    return pows[t], x0


SSM_GROUPS_PER_STEP = 2


def _chunk_scan(er, ei, ar, ai, x0, chunk, n_chunks, forward):
    width = er.shape[1]
    xr, xi = er, ei
    if x0 is not None:
        fr, fi = _cmul(ar, ai, x0[0], x0[1])
        xr = xr + jnp.where(chunk == 0, _lanes(fr, width), 0.0)
        xi = xi + jnp.where(chunk == 0, _lanes(fi, width), 0.0)

    def shifted(v, sh):
        if forward:
            return jnp.where(chunk >= sh, pltpu.roll(v, sh, axis=1), 0.0)
        return jnp.where(chunk < n_chunks - sh, pltpu.roll(v, width - sh, axis=1), 0.0)

    sh = 1
    while sh < n_chunks:
        sr, si = shifted(xr, sh), shifted(xi, sh)
        wr, wi = _lanes(ar, width), _lanes(ai, width)
        xr, xi = xr + wr * sr - wi * si, xi + wr * si + wi * sr
        ar, ai = _cmul(ar, ai, ar, ai)
        sh *= 2
    inr, ini = shifted(xr, 1), shifted(xi, 1)
    if x0 is not None:
        inr = jnp.where(chunk == 0, x0[0], inr)
        ini = jnp.where(chunk == 0, x0[1], ini)
    return inr, ini


def _ssm_kernel(u_ref, um_ref, csc_ref, cc_ref, bc_ref, cr_ref, y_ref, toep_ref, wend_ref,
                wout_ref, yi_ref, *, n_chunks):
    for gi in range(SSM_GROUPS_PER_STEP):
        (lam_r, lam_i), x0 = _build_chunk_ops(gi, csc_ref, cc_ref, bc_ref, cr_ref, um_ref,
                                              toep_ref, wend_ref, wout_ref)
        u = u_ref[gi]
        width = u.shape[1]
        e = jnp.dot(wend_ref[gi], u, preferred_element_type=F32)
        yi_ref[gi] = jnp.dot(toep_ref[gi], u, preferred_element_type=F32)
        chunk =lax.broadcasted_iota(jnp.int32, (1, width), 1) % n_chunks
        p = SSM_STATE
        ar, ai = lam_r, lam_i
        f_in = _chunk_scan(e[0:p], e[p:2 * p], ar[0:p], ai[0:p], (x0[0:p], x0[p:2 * p]),
                           chunk, n_chunks, True)
        b_in = _chunk_scan(e[2 * p:3 * p], e[3 * p:], ar[2 * p:3 * p], ai[2 * p:3 * p],
                           None, chunk, n_chunks, False)
        xin = jnp.concatenate([f_in[0], f_in[1], b_in[0], b_in[1]], axis=0).astype(BF16)
        y = yi_ref[gi] + jnp.dot(wout_ref[gi], xin, preferred_element_type=F32)
        y_ref[gi] = y.astype(BF16)


def _ssm(u_col, u_meta, csc, c_col, b_col, c_row, n_chunks):
    g, _, width = u_col.shape
    per = SSM_GROUPS_PER_STEP
    return pl.pallas_call(
        functools.partial(_ssm_kernel, n_chunks=n_chunks),
        grid=(g // per,),
        in_specs=[
            pl.BlockSpec((per, CHUNK_LANES, width), lambda i: (i, 0, 0)),
            pl.BlockSpec((per, 1, N_META * SSM_GROUP), lambda i: (i, 0, 0)),
            pl.BlockSpec((per, STATE_ROWS, 4), lambda i: (i, 0, 0)),
            pl.BlockSpec((per, STATE_ROWS, SSM_GROUP), lambda i: (i, 0, 0)),
            pl.BlockSpec((per, STATE_ROWS, SSM_GROUP), lambda i: (i, 0, 0)),
            pl.BlockSpec((per, SSM_GROUP, STATE_ROWS), lambda i: (i, 0, 0)),
        ],
        out_specs=pl.BlockSpec((per, CHUNK_LANES, width), lambda i: (i, 0, 0)),
        out_shape=jax.ShapeDtypeStruct((g, CHUNK_LANES, width), BF16),
        scratch_shapes=[
            pltpu.VMEM((per, CHUNK_LANES, CHUNK_LANES), BF16),
            pltpu.VMEM((per, STATE_ROWS, CHUNK_LANES), BF16),
            pltpu.VMEM((per, CHUNK_LANES, STATE_ROWS), BF16),
            pltpu.VMEM((per, CHUNK_LANES, width), F32),
        ],
        compiler_params=_params(),
        name="s5_mixer",
    )(u_col, u_meta, csc, c_col, b_col, c_row)


FF_CHUNK = 1024
MIX_PER = 4


MIX_SLOTS = 3


def _mix_ffn_kernel(x_hbm, co_ref, y_ref, ut_ref, d_ref, gwt_ref, gb_ref, wo_ref,
                    gf_ref, w1_ref, w2_ref, gl_ref, o_hbm, buf, sem_in, sem_out,
                    mixed_ref, h_ref, z_ref, a_ref):
    n_chunks = x_hbm.shape[1]
    rows = SUBLANES * n_chunks
    step = pl.program_id(0)
    n_tiles = pl.num_programs(0) - 1
    tile = step - 1

    def mix_stages(dst):
        per = MIX_PER
        n_half = SUBLANES // per
        state = {}

        def col(ref, k):
            return jnp.concatenate(
                [ref[:, r * SSM_GROUP:(r + 1) * SSM_GROUP, :].reshape(D_SSM, n_chunks)
                 for r in range(k * per, (k + 1) * per)], axis=1).astype(F32)

        def gelu_stage(k):
            def run():
                state["ge", k] = jax.nn.gelu(col(y_ref, k) + d_ref[...] * col(ut_ref, k))
            return run

        def glu_stage(k):
            def run():
                ge = state["ge", k]
                gate = (jnp.dot(gwt_ref[...], ge.astype(BF16), preferred_element_type=F32)
                        + gb_ref[...])
                state["s5t", k] = ge * jax.nn.sigmoid(gate)
            return run

        def store_stage(k):
            def run():
                rs = slice(k * per * n_chunks, (k + 1) * per * n_chunks)
                mixed_ref[dst, rs, :D_CONV] = co_ref[k * per:(k + 1) * per].reshape(
                    per * n_chunks, D_CONV)
                mixed_ref[dst, rs, D_CONV:] = state["s5t", k].T.astype(BF16)
            return run

        return [f(k) for f in (gelu_stage, glu_stage, store_stage) for k in range(n_half)]

    @pl.when(step == 0)
    def _():
        for cp in _step_copies(x_hbm, buf, sem_in, 0, 0, False):
            cp.start()
        for stage in mix_stages(0):
            stage()

    @pl.when(step >= 1)
    def _():
        slot = tile % MIX_SLOTS
        nxt = (tile + 1) % MIX_SLOTS

        @pl.when(tile >= 2)
        def _():
            for cp in _step_copies(o_hbm, buf, sem_out, tile - 2, nxt, True):
                cp.wait()

        @pl.when(tile + 1 < n_tiles)
        def _():
            for cp in _step_copies(x_hbm, buf, sem_in, tile + 1, nxt, False):
                cp.start()

        for cp in _step_copies(x_hbm, buf, sem_in, tile, slot, False):
            cp.wait()

        stages = mix_stages(step % 2)
        half = D_FF // 2
        n_slots = D_FF // FF_CHUNK + 2
        n_stages = len(stages)
        done = [0, 0]

        def fill_slot():
            done[0] += 1
            while done[1] * n_slots < done[0] * n_stages:
                stages.pop(0)()
                done[1] += 1

        halves = [slice(r * rows // 2, (r + 1) * rows // 2) for r in range(2)]
        x = buf[slot].reshape(rows, D_MODEL)
        for rs in halves:
            hr = x[rs] + jnp.dot(mixed_ref[tile % 2, rs], wo_ref[...],
                                 preferred_element_type=F32)
            z_ref[rs] = _rms(hr, gf_ref[...]).astype(BF16)
            h_ref[rs] = hr
        for k in range(2):
            for j in range(half // FF_CHUNK):
                cols = slice(k * half + j * FF_CHUNK, k * half + (j + 1) * FF_CHUNK)
                for rs in (halves if (k, j) == (0, 0) else [slice(None)]):
                    a = jnp.dot(z_ref[rs], w1_ref[:, cols], preferred_element_type=F32)
                    a_ref[rs, j * FF_CHUNK:(j + 1) * FF_CHUNK] = (
                        jnp.square(jnp.maximum(a, 0.0)).astype(BF16))
                fill_slot()
            if k == 0:
                h_ref[...] += jnp.dot(a_ref[...], w2_ref[:half, :],
                                      preferred_element_type=F32)
                fill_slot()
        for r, rs in enumerate(halves):
            hr = h_ref[rs] + jnp.dot(a_ref[rs], w2_ref[half:, :], preferred_element_type=F32)
            buf[slot, r * SUBLANES // 2:(r + 1) * SUBLANES // 2] = _rms(
                hr, gl_ref[...]).reshape(SUBLANES // 2, n_chunks, D_MODEL)
            if r == 0:
                fill_slot()
        assert not stages

        for cp in _step_copies(o_hbm, buf, sem_out, tile, slot, True):
            cp.start()

        @pl.when(tile == n_tiles - 1)
        def _():
            for cp in (_step_copies(o_hbm, buf, sem_out, tile - 1, (tile - 1) % MIX_SLOTS, True)
                       + _step_copies(o_hbm, buf, sem_out, tile, slot, True)):
                cp.wait()


def _mix_ffn(x4, conv_out, y_col, u_col, d_col, glu_wt, glu_b_col, w_out, g_ffn, w1, w2,
             g_final):
    bsz, n_chunks = x4.shape[:2]

    def const(shape):
        return pl.BlockSpec(shape, lambda t: (0, 0), pipeline_mode=pl.Buffered(1))

    n_tiles = CHUNK // SUBLANES * bsz
    hbm = pl.BlockSpec(memory_space=pl.ANY)

    def slab(s):
        return jnp.minimum(s, n_tiles - 1) // bsz

    def batch(s):
        return jnp.minimum(s, n_tiles - 1) % bsz

    col_blk = pl.BlockSpec((N_GROUPS, SUBLANES * SSM_GROUP, n_chunks),
                           lambda s: (0, slab(s), batch(s)))
    step_sems = pltpu.SemaphoreType.DMA((MIX_SLOTS, SUBLANES))
    return pl.pallas_call(
        _mix_ffn_kernel,
        grid=(n_tiles + 1,),
        in_specs=[
            hbm,
            pl.BlockSpec((SUBLANES, n_chunks, D_CONV), lambda s: (slab(s), batch(s), 0)),
            col_blk,
            col_blk,
            const((D_SSM, 1)),
            const((D_SSM, D_SSM)),
            const((D_SSM, 1)),
            const((D_MODEL, D_MODEL)),
            const((1, D_MODEL)),
            const((D_MODEL, D_FF)),
            const((D_FF, D_MODEL)),
            const((1, D_MODEL)),
        ],
        out_specs=hbm,
        out_shape=jax.ShapeDtypeStruct(x4.shape, F32),
        scratch_shapes=[
            pltpu.VMEM((MIX_SLOTS, SUBLANES, n_chunks, D_MODEL), F32),
            step_sems, step_sems,
            pltpu.VMEM((2, SUBLANES * n_chunks, D_MODEL), BF16),
            pltpu.VMEM((SUBLANES * n_chunks, D_MODEL), F32),
            pltpu.VMEM((SUBLANES * n_chunks, D_MODEL), BF16),
            pltpu.VMEM((SUBLANES * n_chunks, D_FF // 2), BF16),
        ],
        compiler_params=_step_tile_params(),
        name="mix_ffn",
    )(x4, conv_out, y_col, u_col, d_col, glu_wt, glu_b_col, w_out, g_ffn, w1, w2, g_final)


def _s5_operator_inputs(lbr, lbi, zr, zi, b_re, b_im, c_re, c_im):
    g, p, h = N_GROUPS, SSM_STATE, SSM_GROUP

    sc = jnp.transpose(jnp.stack([lbr, lbi, zr, zi], axis=-1), (1, 0, 2, 3))
    csc = jnp.broadcast_to(sc[:, :, None], (g, 2, 2, p, 4)).reshape(g, 4 * p, 4)
    c = jnp.stack([c_re, c_im], axis=0)
    b = jnp.stack([b_re, b_im], axis=0)
    c_col = jnp.transpose(c, (2, 1, 0, 4, 3)).reshape(g, 4 * p, h)
    b_col = jnp.transpose(b, (2, 1, 0, 3, 4)).reshape(g, 4 * p, h)
    c_row = jnp.transpose(c, (2, 3, 1, 0, 4)).reshape(g, h, 4 * p)
    return csc, c_col, b_col, c_row


def kernel(x, meta_tokens, norm_mix_g, w_in, conv_w, conv_b, conv_ln_g, conv_ln_b,
           ssm_lam_re, ssm_lam_im, ssm_log_dt, ssm_b_re, ssm_b_im, ssm_c_re, ssm_c_im,
           ssm_d, ssm_glu_w, ssm_glu_b, w_out, norm_ffn_g, w_ff1, w_ff2, norm_final_g):
    assert w_in.shape[0] == 1, "single-layer block"
    bsz, seq, _ = x.shape
    assert seq % CHUNK == 0 and CHUNK >= N_META
    n_chunks = seq // CHUNK
    g, h = N_GROUPS, SSM_GROUP

    x4 = x.reshape(bsz, n_chunks, CHUNK, D_MODEL)
    g_mix = norm_mix_g[0][None, :]
    w_in_b = w_in[0].astype(BF16)
    u_conv, u_col, u_conv_m, u_ssm_m = _in_proj_step(x4, meta_tokens, g_mix, w_in_b)

    conv_out = _conv_module(u_conv, u_conv_m, conv_w[0], conv_b[0][None, :],
                            conv_ln_g[0][None, :], conv_ln_b[0][None, :], bsz)

    ldt = jnp.broadcast_to(ssm_log_dt[0][..., None], ssm_lam_re[0].shape)
    flat = lambda a: a.reshape(2 * g, SSM_STATE)
    lbr, lbi, zr, zi = [a.reshape(2, g, SSM_STATE) for a in
                        _zoh(flat(ssm_lam_re[0]), flat(ssm_lam_im[0]), flat(ldt))]
    csc, c_col, b_col, c_row = _s5_operator_inputs(
        lbr, lbi, zr, zi, ssm_b_re[0], ssm_b_im[0], ssm_c_re[0], ssm_c_im[0])

    u_meta = jnp.transpose(u_ssm_m.reshape(N_META, g, h), (1, 0, 2)).reshape(g, 1, N_META * h)
    y_col = _ssm(u_col, u_meta, csc, c_col, b_col, c_row, n_chunks)

    out = _mix_ffn(x4, conv_out, y_col, u_col, ssm_d[0][:, None],
                   ssm_glu_w[0].T.astype(BF16), ssm_glu_b[0][:, None],
                   w_out[0].astype(BF16), norm_ffn_g[0][None, :], w_ff1[0].astype(BF16),
                   w_ff2[0].astype(BF16), norm_final_g[None, :])
    return out.reshape(bsz, seq, D_MODEL)
```

```python
import functools

import jax
import jax.numpy as jnp
from jax import lax
from jax.experimental import pallas as pl
from jax.experimental.pallas import tpu as pltpu

F32 = jnp.float32
BF16 = jnp.bfloat16

D_MODEL = 1024
N_META = 16
D_CONV = 512
D_SSM = 512
CONV_WIDTH = 31
CONV_PAD = CONV_WIDTH // 2
SSM_GROUP = 16
N_GROUPS = D_SSM // SSM_GROUP
SSM_STATE = 64
D_FF = 4096
NORM_EPS = 1e-5
LANES = 128
SUBLANES = 8

CHUNK = 32
CHUNK_LANES = CHUNK * SSM_GROUP
STATE_ROWS = 4 * SSM_STATE
DIR_ROWS = 2 * SSM_STATE
STEPS_PER_VREG = LANES // SSM_GROUP

VMEM_LIMIT_BYTES = 60 * 1024 * 1024


def _rms(x, g):
    return x * lax.rsqrt(jnp.mean(x * x, axis=-1, keepdims=True) + NORM_EPS) * g


def _params(n_axes=1):
    return pltpu.CompilerParams(dimension_semantics=("parallel",) * n_axes,
                                vmem_limit_bytes=VMEM_LIMIT_BYTES)


def _step_copies(hbm_ref, buf_ref, sem_ref, tile, slot, to_hbm):
    bsz = hbm_ref.shape[0]
    slab, b = tile // bsz, tile % bsz
    copies = []
    for r in range(SUBLANES):
        hbm = hbm_ref.at[b, :, slab * SUBLANES + r, :]
        vmem = buf_ref.at[slot, r]
        src, dst = (vmem, hbm) if to_hbm else (hbm, vmem)
        copies.append(pltpu.make_async_copy(src, dst, sem_ref.at[slot, r]))
    return copies


def _fetch_steps(x_hbm, xbuf, sem):
    tile = pl.program_id(0)
    slot = tile % 2

    @pl.when(tile == 0)
    def _():
        for cp in _step_copies(x_hbm, xbuf, sem, tile, slot, False):
            cp.start()

    @pl.when(tile + 1 < pl.num_programs(0))
    def _():
        for cp in _step_copies(x_hbm, xbuf, sem, tile + 1, 1 - slot, False):
            cp.start()

    for cp in _step_copies(x_hbm, xbuf, sem, tile, slot, False):
        cp.wait()
    return slot


def _step_tile_params():
    return pltpu.CompilerParams(dimension_semantics=("arbitrary",),
                                vmem_limit_bytes=VMEM_LIMIT_BYTES)


def _in_proj_step_kernel(x_hbm, meta_ref, g_ref, w32_ref, uc_ref, ut_ref, ucm_ref, usm_ref,
                         xbuf, sem, w_ref):
    n_chunks = x_hbm.shape[1]
    slot = _fetch_steps(x_hbm, xbuf, sem)

    @pl.when(pl.program_id(0) == 0)
    def _():
        w_ref[...] = w32_ref[...].astype(BF16)
        zm = _rms(meta_ref[...], g_ref[...]).astype(BF16)
        pm = jnp.dot(zm, w_ref[...], preferred_element_type=F32)
        ucm_ref[...] = pm[:, :D_CONV] * jax.nn.sigmoid(pm[:, D_CONV:2 * D_CONV])
        usm_ref[...] = pm[:, 2 * D_CONV:]

    x = xbuf[slot].reshape(SUBLANES * n_chunks, D_MODEL)
    z = _rms(x, g_ref[...]).astype(BF16)
    p = jnp.dot(z, w_ref[...], preferred_element_type=F32)
    uc = p[:, :D_CONV] * jax.nn.sigmoid(p[:, D_CONV:2 * D_CONV])
    uc_ref[...] = uc.reshape(SUBLANES, n_chunks, D_CONV)
    ut = p[:, 2 * D_CONV:].T.astype(BF16)
    for r in range(SUBLANES):
        ut_ref[:, r * SSM_GROUP:(r + 1) * SSM_GROUP, :] = (
            ut[:, r * n_chunks:(r + 1) * n_chunks].reshape(N_GROUPS, SSM_GROUP, n_chunks))


def _in_proj_step(x4, meta, g, w_in):
    bsz, n_chunks = x4.shape[:2]
    rows = bsz * n_chunks
    return pl.pallas_call(
        _in_proj_step_kernel,
        grid=(CHUNK // SUBLANES * bsz,),
        in_specs=[
            pl.BlockSpec(memory_space=pl.ANY),
            pl.BlockSpec((N_META, D_MODEL), lambda t: (0, 0)),
            pl.BlockSpec((1, D_MODEL), lambda t: (0, 0)),
            pl.BlockSpec((D_MODEL, 2 * D_CONV + D_SSM), lambda t: (0, 0),
                         pipeline_mode=pl.Buffered(1)),
        ],
        out_specs=[
            pl.BlockSpec((SUBLANES, n_chunks, D_CONV), lambda t: (t // bsz, t % bsz, 0)),
            pl.BlockSpec((N_GROUPS, SUBLANES * SSM_GROUP, n_chunks),
                         lambda t: (0, t // bsz, t % bsz)),
            pl.BlockSpec((N_META, D_CONV), lambda t: (0, 0)),
            pl.BlockSpec((N_META, D_SSM), lambda t: (0, 0)),
        ],
        out_shape=[
            jax.ShapeDtypeStruct((CHUNK, rows, D_CONV), F32),
            jax.ShapeDtypeStruct((N_GROUPS, CHUNK_LANES, rows), BF16),
            jax.ShapeDtypeStruct((N_META, D_CONV), F32),
            jax.ShapeDtypeStruct((N_META, D_SSM), F32),
        ],
        scratch_shapes=[pltpu.VMEM((2, SUBLANES, n_chunks, D_MODEL), F32),
                        pltpu.SemaphoreType.DMA((2, SUBLANES)),
                        pltpu.VMEM((D_MODEL, 2 * D_CONV + D_SSM), BF16)],
        compiler_params=_step_tile_params(),
        name="in_proj",
    )(x4, meta, g, w_in)


CONV_GROUP = 4
CONV_ROWS = 64


def _conv_kernel(u_ref, um_ref, w_ref, cb_ref, lg_ref, lb_ref, o_ref, cat_ref, acc_ref):
    n_steps, n_chunks, _ = u_ref.shape
    cat_ref[CONV_PAD:CONV_PAD + n_steps] = u_ref[...]
    chunk = lax.broadcasted_iota(jnp.int32, (n_chunks, 1), 0)
    for i in range(CONV_PAD):
        s_prev = n_steps - CONV_PAD + i
        prev = pltpu.roll(u_ref[s_prev], 1, axis=0)
        meta_row = um_ref[N_META - CONV_PAD + i:N_META - CONV_PAD + i + 1, :]
        cat_ref[i] = jnp.where(chunk == 0, meta_row, prev)
        nxt = pltpu.roll(u_ref[i], n_chunks - 1, axis=0)
        cat_ref[n_steps + CONV_PAD + i] = jnp.where(chunk == n_chunks - 1, 0.0, nxt)

    def body(sg, carry):
        s0 = sg * CONV_GROUP
        for c in range(D_CONV // LANES):
            cols = slice(c * LANES, (c + 1) * LANES)
            n_sub = CONV_ROWS // SUBLANES

            def rows_body(rt, carry2):
                r0 = [pl.multiple_of(rt * CONV_ROWS + sub * SUBLANES, SUBLANES)
                      for sub in range(n_sub)]
                accs = [[jnp.zeros((SUBLANES, LANES), F32)] * CONV_GROUP for _ in range(n_sub)]
                for i in range(CONV_WIDTH + CONV_GROUP - 1):
                    ds = [cat_ref[s0 + i, pl.ds(r0[sub], SUBLANES), cols]
                          for sub in range(n_sub)]
                    for j in range(CONV_GROUP):
                        if 0 <= i - j < CONV_WIDTH:
                            tap = jnp.broadcast_to(w_ref[i - j:i - j + 1, cols],
                                                   (SUBLANES, LANES))
                            for sub in range(n_sub):
                                accs[sub][j] = accs[sub][j] + ds[sub] * tap
                for sub in range(n_sub):
                    for j in range(CONV_GROUP):
                        acc_ref[j, pl.ds(r0[sub], SUBLANES), cols] = accs[sub][j]
                return carry2

            lax.fori_loop(0, n_chunks // CONV_ROWS, rows_body, 0)
        for j in range(CONV_GROUP):
            y = acc_ref[j] + cb_ref[...]
            yc = y - jnp.mean(y, axis=-1, keepdims=True)
            yn = yc * lax.rsqrt(jnp.mean(yc * yc, axis=-1, keepdims=True) + NORM_EPS)
            yn = yn * lg_ref[...] + lb_ref[...]
            o_ref[s0 + j] = (yn * jax.nn.sigmoid(yn)).astype(BF16)
        return carry

    lax.fori_loop(0, n_steps // CONV_GROUP, body, 0)


def _conv_module(u_conv, u_conv_meta, conv_w, conv_b, ln_g, ln_b, bsz):
    n_steps, rows, _ = u_conv.shape
    n_chunks = rows // bsz
    vec = pl.BlockSpec((1, D_CONV), lambda b: (0, 0))
    blk = pl.BlockSpec((n_steps, n_chunks, D_CONV), lambda b: (0, b, 0))
    return pl.pallas_call(
        _conv_kernel,
        grid=(bsz,),
        in_specs=[
            blk,
            pl.BlockSpec((N_META, D_CONV), lambda b: (0, 0)),
            pl.BlockSpec((CONV_WIDTH, D_CONV), lambda b: (0, 0)),
            vec, vec, vec,
        ],
        out_specs=blk,
        out_shape=jax.ShapeDtypeStruct((n_steps, rows, D_CONV), BF16),
        scratch_shapes=[pltpu.VMEM((n_steps + 2 * CONV_PAD, n_chunks, D_CONV), F32),
                        pltpu.VMEM((CONV_GROUP, n_chunks, D_CONV), F32)],
        compiler_params=_params(),
        name="conv_module",
    )(u_conv, u_conv_meta, conv_w, conv_b, ln_g, ln_b)


def _zoh_kernel(lre_ref, lim_ref, ldt_ref, lbr_ref, lbi_ref, zr_ref, zi_ref):
    lre = lre_ref[...]
    lim = lim_ref[...]
    dt = jnp.exp(ldt_ref[...])
    ea = jnp.exp(lre * dt)
    lbr = ea * jnp.cos(lim * dt)
    lbi = ea * jnp.sin(lim * dt)
    nr = lbr - 1.0
    den = lre * lre + lim * lim
    lbr_ref[...] = lbr
    lbi_ref[...] = lbi
    zr_ref[...] = (nr * lre + lbi * lim) / den
    zi_ref[...] = (lbi * lre - nr * lim) / den


def _zoh(lam_re, lam_im, log_dt):
    shape = jax.ShapeDtypeStruct(lam_re.shape, F32)
    return pl.pallas_call(_zoh_kernel, out_shape=[shape] * 4, name="s5_zoh")(
        lam_re, lam_im, log_dt)


def _lanes(x, n):
    return jnp.concatenate([x] * (n // LANES), axis=1)


def _cmul(ar, ai, br, bi):
    return ar * br - ai * bi, ar * bi + ai * br


def _cpow(br, bi, expo, nbits):
    rr = jnp.ones(expo.shape, F32)
    ri = jnp.zeros(expo.shape, F32)
    for k in range(nbits):
        bit = ((expo >> k) & 1) == 1
        nr, ni = _cmul(rr, ri, br, bi)
        rr = jnp.where(bit, nr, rr)
        ri = jnp.where(bit, ni, ri)
        br, bi = _cmul(br, bi, br, bi)
    return rr, ri


def _build_chunk_ops(gi, csc_ref, cc_ref, bc_ref, cr_ref, um_ref, toep_ref, wend_ref,
                     wout_ref):
    t = CHUNK
    rows = STATE_ROWS
    n_cols = CHUNK_LANES // LANES
    lam = (jnp.broadcast_to(csc_ref[gi, :, 0:1], (rows, LANES)),
           jnp.broadcast_to(csc_ref[gi, :, 1:2], (rows, LANES)))
    zr = jnp.broadcast_to(csc_ref[gi, :, 2:3], (rows, LANES))
    zi = jnp.broadcast_to(csc_ref[gi, :, 3:4], (rows, LANES))
    sel = (lax.broadcasted_iota(jnp.int32, (SSM_GROUP, LANES), 1) % SSM_GROUP
           == lax.broadcasted_iota(jnp.int32, (SSM_GROUP, LANES), 0)).astype(F32)

    def tile(ref):
        return jnp.dot(ref[gi], sel, precision=lax.Precision.HIGHEST,
                       preferred_element_type=F32)

    def swap_re_im(a):
        p = SSM_STATE
        return jnp.concatenate([a[p:2 * p], a[0:p], a[3 * p:], a[2 * p:3 * p]], axis=0)

    c_same, b_same = tile(cc_ref), tile(bc_ref)
    c_swap, b_swap = swap_re_im(c_same), swap_re_im(b_same)
    row = lax.broadcasted_iota(jnp.int32, (rows, LANES), 0)
    is_re = (row // SSM_STATE) % 2 == 0
    sgn = jnp.where(is_re, -1.0, 1.0)
    ca = jnp.where(is_re, c_same, -c_same)
    cb = -c_swap
    bb_same = zr * b_same + sgn * zi * b_swap
    sbb_swap = sgn * (zr * b_swap - sgn * zi * b_same)

    pows = {1: lam}
    k = 1
    while k < t:
        pows[2 * k] = _cmul(*pows[k], *pows[k])
        k *= 2
    step_bits = STEPS_PER_VREG.bit_length() - 1
    fwd = slice(0, DIR_ROWS)
    bwd = slice(DIR_ROWS, rows)
    every = slice(0, rows)

    def rows_of(v, rs):
        return v[0][rs], v[1][rs]

    def column(base, e, rs):
        out = rows_of(base, rs)
        for bit, val in pows.items():
            if e & bit:
                out = _cmul(*out, *rows_of(val, rs))
        return out

    def times_b(p, rs):
        return p[0] * bb_same[rs] + p[1] * sbb_swap[rs]

    def times_c(p, rs):
        return ca[rs] * p[0] + cb[rs] * p[1]

    i8 = lax.broadcasted_iota(jnp.int32, (rows, LANES), 1) // SSM_GROUP
    asc0 = _cpow(*lam, i8, step_bits)
    desc0 = _cpow(*lam, STEPS_PER_VREG - 1 - i8, step_bits)
    asc1_0 = _cmul(*asc0, *lam)
    desc1_0 = _cmul(*desc0, *lam)

    wend_f, wend_b, lag_b, wout_f, wout_b = [], [], [], [], []
    for q in range(n_cols):
        e_asc = STEPS_PER_VREG * q
        e_desc = STEPS_PER_VREG * (n_cols - 1 - q)
        asc1_q = column(asc1_0, e_asc, every)
        wend_f.append(times_b(column(desc0, e_desc, fwd), fwd))
        wend_b.append(times_b(column(asc0, e_asc, bwd), bwd))
        lag_b.append(times_b(rows_of(asc1_q, bwd), bwd))
        wout_f.append(times_c(rows_of(asc1_q, fwd), fwd))
        wout_b.append(times_c(column(desc1_0, e_desc, bwd), bwd))
        cols = slice(q * LANES, (q + 1) * LANES)
        wend_ref[gi, fwd, cols] = wend_f[q].astype(BF16)
        wend_ref[gi, bwd, cols] = wend_b[q].astype(BF16)

    meta_cols = N_META * SSM_GROUP // LANES
    x0 = sum(jnp.sum(wend_f[n_cols - meta_cols + m] * um_ref[gi, :, m * LANES:(m + 1) * LANES],
                     axis=1, keepdims=True) for m in range(meta_cols))

    lane = lax.broadcasted_iota(jnp.int32, (DIR_ROWS, LANES), 1)
    lag0_b = jnp.where(lane >= LANES - SSM_GROUP, bb_same[bwd], 0.0)
    zero = jnp.zeros((DIR_ROWS, LANES), F32)
    bcat = jnp.concatenate(
        [jnp.concatenate(wend_f + [zero] * n_cols, axis=1),
         jnp.concatenate([zero] * (n_cols - 1) + [lag0_b] + lag_b, axis=1)], axis=0)
    lane_r = lax.broadcasted_iota(jnp.int32, (SSM_GROUP, rows), 1)
    c2 = jnp.where((lane_r // SSM_STATE) % 2 == 0, cr_ref[gi], -cr_ref[gi])
    kk = jnp.dot(c2, bcat, precision=lax.Precision.HIGHEST,
                 preferred_element_type=F32)
    for tt in range(t):
        off = (t - 1 - tt) * SSM_GROUP
        toep_ref[gi, tt * SSM_GROUP:(tt + 1) * SSM_GROUP, :] = (
            kk[:, off:off + CHUNK_LANES].astype(BF16))

    wout = jnp.concatenate([jnp.concatenate(wout_f, axis=1),
                            jnp.concatenate(wout_b, axis=1)], axis=0)
    wout_ref[gi] = wout.T.astype(BF16)
    return pows[t], x0


SSM_GROUPS_PER_STEP = 2


def _chunk_scan(er, ei, ar, ai, x0, chunk, n_chunks, forward):
    width = er.shape[1]
    xr, xi = er, ei
    if x0 is not None:
        fr, fi = _cmul(ar, ai, x0[0], x0[1])
        xr = xr + jnp.where(chunk == 0, _lanes(fr, width), 0.0)
        xi = xi + jnp.where(chunk == 0, _lanes(fi, width), 0.0)

    def shifted(v, sh):
        if forward:
            return jnp.where(chunk >= sh, pltpu.roll(v, sh, axis=1), 0.0)
        return jnp.where(chunk < n_chunks - sh, pltpu.roll(v, width - sh, axis=1), 0.0)

    sh = 1
    while sh < n_chunks:
        sr, si = shifted(xr, sh), shifted(xi, sh)
        wr, wi = _lanes(ar, width), _lanes(ai, width)
        xr, xi = xr + wr * sr - wi * si, xi + wr * si + wi * sr
        ar, ai = _cmul(ar, ai, ar, ai)
        sh *= 2
    inr, ini = shifted(xr, 1), shifted(xi, 1)
    if x0 is not None:
        inr = jnp.where(chunk == 0, x0[0], inr)
        ini = jnp.where(chunk == 0, x0[1], ini)
    return inr, ini


def _ssm_kernel(u_ref, um_ref, csc_ref, cc_ref, bc_ref, cr_ref, y_ref, toep_ref, wend_ref,
                wout_ref, yi_ref, *, n_chunks):
    for gi in range(SSM_GROUPS_PER_STEP):
        (lam_r, lam_i), x0 = _build_chunk_ops(gi, csc_ref, cc_ref, bc_ref, cr_ref, um_ref,
                                              toep_ref, wend_ref, wout_ref)
        u = u_ref[gi]
        width = u.shape[1]
        e = jnp.dot(wend_ref[gi], u, preferred_element_type=F32)
        yi_ref[gi] = jnp.dot(toep_ref[gi], u, preferred_element_type=F32)
        chunk =lax.broadcasted_iota(jnp.int32, (1, width), 1) % n_chunks
        p = SSM_STATE
        ar, ai = lam_r, lam_i
        f_in = _chunk_scan(e[0:p], e[p:2 * p], ar[0:p], ai[0:p], (x0[0:p], x0[p:2 * p]),
                           chunk, n_chunks, True)
        b_in = _chunk_scan(e[2 * p:3 * p], e[3 * p:], ar[2 * p:3 * p], ai[2 * p:3 * p],
                           None, chunk, n_chunks, False)
        xin = jnp.concatenate([f_in[0], f_in[1], b_in[0], b_in[1]], axis=0).astype(BF16)
        y = yi_ref[gi] + jnp.dot(wout_ref[gi], xin, preferred_element_type=F32)
        y_ref[gi] = y.astype(BF16)


def _ssm(u_col, u_meta, csc, c_col, b_col, c_row, n_chunks):
    g, _, width = u_col.shape
    per = SSM_GROUPS_PER_STEP
    return pl.pallas_call(
        functools.partial(_ssm_kernel, n_chunks=n_chunks),
        grid=(g // per,),
        in_specs=[
            pl.BlockSpec((per, CHUNK_LANES, width), lambda i: (i, 0, 0)),
            pl.BlockSpec((per, 1, N_META * SSM_GROUP), lambda i: (i, 0, 0)),
            pl.BlockSpec((per, STATE_ROWS, 4), lambda i: (i, 0, 0)),
            pl.BlockSpec((per, STATE_ROWS, SSM_GROUP), lambda i: (i, 0, 0)),
            pl.BlockSpec((per, STATE_ROWS, SSM_GROUP), lambda i: (i, 0, 0)),
            pl.BlockSpec((per, SSM_GROUP, STATE_ROWS), lambda i: (i, 0, 0)),
        ],
        out_specs=pl.BlockSpec((per, CHUNK_LANES, width), lambda i: (i, 0, 0)),
        out_shape=jax.ShapeDtypeStruct((g, CHUNK_LANES, width), BF16),
        scratch_shapes=[
            pltpu.VMEM((per, CHUNK_LANES, CHUNK_LANES), BF16),
            pltpu.VMEM((per, STATE_ROWS, CHUNK_LANES), BF16),
            pltpu.VMEM((per, CHUNK_LANES, STATE_ROWS), BF16),
            pltpu.VMEM((per, CHUNK_LANES, width), F32),
        ],
        compiler_params=_params(),
        name="s5_mixer",
    )(u_col, u_meta, csc, c_col, b_col, c_row)


FF_CHUNK = 1024
MIX_PER = 4


MIX_SLOTS = 3


def _mix_ffn_kernel(x_hbm, co_ref, y_ref, ut_ref, d_ref, gwt_ref, gb_ref, wo_ref,
                    gf_ref, w1_ref, w2_ref, gl_ref, o_hbm, buf, sem_in, sem_out,
                    mixed_ref, h_ref, z_ref, a_ref):
    n_chunks = x_hbm.shape[1]
    rows = SUBLANES * n_chunks
    step = pl.program_id(0)
    n_tiles = pl.num_programs(0) - 1
    tile = step - 1

    def mix_stages(dst):
        per = MIX_PER
        n_half = SUBLANES // per
        state = {}

        def col(ref, k):
            return jnp.concatenate(
                [ref[:, r * SSM_GROUP:(r + 1) * SSM_GROUP, :].reshape(D_SSM, n_chunks)
                 for r in range(k * per, (k + 1) * per)], axis=1).astype(F32)

        def gelu_stage(k):
            def run():
                state["ge", k] = jax.nn.gelu(col(y_ref, k) + d_ref[...] * col(ut_ref, k))
            return run

        def glu_stage(k):
            def run():
                ge = state["ge", k]
                gate = (jnp.dot(gwt_ref[...], ge.astype(BF16), preferred_element_type=F32)
                        + gb_ref[...])
                state["s5t", k] = ge * jax.nn.sigmoid(gate)
            return run

        def store_stage(k):
            def run():
                rs = slice(k * per * n_chunks, (k + 1) * per * n_chunks)
                mixed_ref[dst, rs, :D_CONV] = co_ref[k * per:(k + 1) * per].reshape(
                    per * n_chunks, D_CONV)
                mixed_ref[dst, rs, D_CONV:] = state["s5t", k].T.astype(BF16)
            return run

        return [f(k) for f in (gelu_stage, glu_stage, store_stage) for k in range(n_half)]

    @pl.when(step == 0)
    def _():
        for cp in _step_copies(x_hbm, buf, sem_in, 0, 0, False):
            cp.start()
        for stage in mix_stages(0):
            stage()

    @pl.when(step >= 1)
    def _():
        slot = tile % MIX_SLOTS
        nxt = (tile + 1) % MIX_SLOTS

        @pl.when(tile >= 2)
        def _():
            for cp in _step_copies(o_hbm, buf, sem_out, tile - 2, nxt, True):
                cp.wait()

        @pl.when(tile + 1 < n_tiles)
        def _():
            for cp in _step_copies(x_hbm, buf, sem_in, tile + 1, nxt, False):
                cp.start()

        for cp in _step_copies(x_hbm, buf, sem_in, tile, slot, False):
            cp.wait()

        stages = mix_stages(step % 2)
        half = D_FF // 2
        n_slots = D_FF // FF_CHUNK + 2
        n_stages = len(stages)
        done = [0, 0]

        def fill_slot():
            done[0] += 1
            while done[1] * n_slots < done[0] * n_stages:
                stages.pop(0)()
                done[1] += 1

        h = buf[slot].reshape(rows, D_MODEL) + jnp.dot(
            mixed_ref[tile % 2], wo_ref[...], preferred_element_type=F32)
        z_ref[...] = _rms(h, gf_ref[...]).astype(BF16)
        h_ref[...] = h
        for k in range(2):
            for j in range(half // FF_CHUNK):
                cols = slice(k * half + j * FF_CHUNK, k * half + (j + 1) * FF_CHUNK)
                a = jnp.dot(z_ref[...], w1_ref[:, cols], preferred_element_type=F32)
                a_ref[:, j * FF_CHUNK:(j + 1) * FF_CHUNK] = (
                    jnp.square(jnp.maximum(a, 0.0)).astype(BF16))
                fill_slot()
            h_ref[...] += jnp.dot(a_ref[...], w2_ref[k * half:(k + 1) * half, :],
                                  preferred_element_type=F32)
            fill_slot()
        assert not stages
        buf[slot] = _rms(h_ref[...], gl_ref[...]).reshape(SUBLANES, n_chunks, D_MODEL)

        for cp in _step_copies(o_hbm, buf, sem_out, tile, slot, True):
            cp.start()

        @pl.when(tile == n_tiles - 1)
        def _():
            for cp in (_step_copies(o_hbm, buf, sem_out, tile - 1, (tile - 1) % MIX_SLOTS, True)
                       + _step_copies(o_hbm, buf, sem_out, tile, slot, True)):
                cp.wait()


def _mix_ffn(x4, conv_out, y_col, u_col, d_col, glu_wt, glu_b_col, w_out, g_ffn, w1, w2,
             g_final):
    bsz, n_chunks = x4.shape[:2]

    def const(shape):
        return pl.BlockSpec(shape, lambda t: (0, 0), pipeline_mode=pl.Buffered(1))

    n_tiles = CHUNK // SUBLANES * bsz
    hbm = pl.BlockSpec(memory_space=pl.ANY)

    def slab(s):
        return jnp.minimum(s, n_tiles - 1) // bsz

    def batch(s):
        return jnp.minimum(s, n_tiles - 1) % bsz

    col_blk = pl.BlockSpec((N_GROUPS, SUBLANES * SSM_GROUP, n_chunks),
                           lambda s: (0, slab(s), batch(s)))
    step_sems = pltpu.SemaphoreType.DMA((MIX_SLOTS, SUBLANES))
    return pl.pallas_call(
        _mix_ffn_kernel,
        grid=(n_tiles + 1,),
        in_specs=[
            hbm,
            pl.BlockSpec((SUBLANES, n_chunks, D_CONV), lambda s: (slab(s), batch(s), 0)),
            col_blk,
            col_blk,
            const((D_SSM, 1)),
            const((D_SSM, D_SSM)),
            const((D_SSM, 1)),
            const((D_MODEL, D_MODEL)),
            const((1, D_MODEL)),
            const((D_MODEL, D_FF)),
            const((D_FF, D_MODEL)),
            const((1, D_MODEL)),
        ],
        out_specs=hbm,
        out_shape=jax.ShapeDtypeStruct(x4.shape, F32),
        scratch_shapes=[
            pltpu.VMEM((MIX_SLOTS, SUBLANES, n_chunks, D_MODEL), F32),
            step_sems, step_sems,
            pltpu.VMEM((2, SUBLANES * n_chunks, D_MODEL), BF16),
            pltpu.VMEM((SUBLANES * n_chunks, D_MODEL), F32),
            pltpu.VMEM((SUBLANES * n_chunks, D_MODEL), BF16),
            pltpu.VMEM((SUBLANES * n_chunks, D_FF // 2), BF16),
        ],
        compiler_params=_step_tile_params(),
        name="mix_ffn",
    )(x4, conv_out, y_col, u_col, d_col, glu_wt, glu_b_col, w_out, g_ffn, w1, w2, g_final)


def _s5_operator_inputs(lbr, lbi, zr, zi, b_re, b_im, c_re, c_im):
    g, p, h = N_GROUPS, SSM_STATE, SSM_GROUP

    sc = jnp.transpose(jnp.stack([lbr, lbi, zr, zi], axis=-1), (1, 0, 2, 3))
    csc = jnp.broadcast_to(sc[:, :, None], (g, 2, 2, p, 4)).reshape(g, 4 * p, 4)
    c = jnp.stack([c_re, c_im], axis=0)
    b = jnp.stack([b_re, b_im], axis=0)
    c_col = jnp.transpose(c, (2, 1, 0, 4, 3)).reshape(g, 4 * p, h)
    b_col = jnp.transpose(b, (2, 1, 0, 3, 4)).reshape(g, 4 * p, h)
    c_row = jnp.transpose(c, (2, 3, 1, 0, 4)).reshape(g, h, 4 * p)
    return csc, c_col, b_col, c_row


def kernel(x, meta_tokens, norm_mix_g, w_in, conv_w, conv_b, conv_ln_g, conv_ln_b,
           ssm_lam_re, ssm_lam_im, ssm_log_dt, ssm_b_re, ssm_b_im, ssm_c_re, ssm_c_im,
           ssm_d, ssm_glu_w, ssm_glu_b, w_out, norm_ffn_g, w_ff1, w_ff2, norm_final_g):
    assert w_in.shape[0] == 1, "single-layer block"
    bsz, seq, _ = x.shape
    assert seq % CHUNK == 0 and CHUNK >= N_META
    n_chunks = seq // CHUNK
    g, h = N_GROUPS, SSM_GROUP

    x4 = x.reshape(bsz, n_chunks, CHUNK, D_MODEL)
    g_mix = norm_mix_g[0][None, :]
    u_conv, u_col, u_conv_m, u_ssm_m = _in_proj_step(x4, meta_tokens, g_mix, w_in[0])

    conv_out = _conv_module(u_conv, u_conv_m, conv_w[0], conv_b[0][None, :],
                            conv_ln_g[0][None, :], conv_ln_b[0][None, :], bsz)

    ldt = jnp.broadcast_to(ssm_log_dt[0][..., None], ssm_lam_re[0].shape)
    flat = lambda a: a.reshape(2 * g, SSM_STATE)
    lbr, lbi, zr, zi = [a.reshape(2, g, SSM_STATE) for a in
                        _zoh(flat(ssm_lam_re[0]), flat(ssm_lam_im[0]), flat(ldt))]
    csc, c_col, b_col, c_row = _s5_operator_inputs(
        lbr, lbi, zr, zi, ssm_b_re[0], ssm_b_im[0], ssm_c_re[0], ssm_c_im[0])

    u_meta = jnp.transpose(u_ssm_m.reshape(N_META, g, h), (1, 0, 2)).reshape(g, 1, N_META * h)
    y_col = _ssm(u_col, u_meta, csc, c_col, b_col, c_row, n_chunks)

    out = _mix_ffn(x4, conv_out, y_col, u_col, ssm_d[0][:, None],
                   ssm_glu_w[0].T.astype(BF16), ssm_glu_b[0][:, None],
                   w_out[0].astype(BF16), norm_ffn_g[0][None, :], w_ff1[0].astype(BF16),
                   w_ff2[0].astype(BF16), norm_final_g[None, :])
    return out.reshape(bsz, seq, D_MODEL)
```

```python
import functools

import jax
import jax.numpy as jnp
from jax import lax
from jax.experimental import pallas as pl
from jax.experimental.pallas import tpu as pltpu

F32 = jnp.float32
BF16 = jnp.bfloat16

D_MODEL = 1024
N_META = 16
D_CONV = 512
D_SSM = 512
CONV_WIDTH = 31
CONV_PAD = CONV_WIDTH // 2
SSM_GROUP = 16
N_GROUPS = D_SSM // SSM_GROUP
SSM_STATE = 64
D_FF = 4096
NORM_EPS = 1e-5
LANES = 128
SUBLANES = 8

CHUNK = 32
CHUNK_LANES = CHUNK * SSM_GROUP
STATE_ROWS = 4 * SSM_STATE
DIR_ROWS = 2 * SSM_STATE
STEPS_PER_VREG = LANES // SSM_GROUP

VMEM_LIMIT_BYTES = 60 * 1024 * 1024


def _rms(x, g):
    return x * lax.rsqrt(jnp.mean(x * x, axis=-1, keepdims=True) + NORM_EPS) * g


def _params(n_axes=1):
    return pltpu.CompilerParams(dimension_semantics=("parallel",) * n_axes,
                                vmem_limit_bytes=VMEM_LIMIT_BYTES)


def _step_copies(hbm_ref, buf_ref, sem_ref, tile, slot, to_hbm):
    bsz = hbm_ref.shape[0]
    slab, b = tile // bsz, tile % bsz
    copies = []
    for r in range(SUBLANES):
        hbm = hbm_ref.at[b, :, slab * SUBLANES + r, :]
        vmem = buf_ref.at[slot, r]
        src, dst = (vmem, hbm) if to_hbm else (hbm, vmem)
        copies.append(pltpu.make_async_copy(src, dst, sem_ref.at[slot, r]))
    return copies


def _fetch_steps(x_hbm, xbuf, sem):
    tile = pl.program_id(0)
    slot = tile % 2

    @pl.when(tile == 0)
    def _():
        for cp in _step_copies(x_hbm, xbuf, sem, tile, slot, False):
            cp.start()

    @pl.when(tile + 1 < pl.num_programs(0))
    def _():
        for cp in _step_copies(x_hbm, xbuf, sem, tile + 1, 1 - slot, False):
            cp.start()

    for cp in _step_copies(x_hbm, xbuf, sem, tile, slot, False):
        cp.wait()
    return slot


def _step_tile_params():
    return pltpu.CompilerParams(dimension_semantics=("arbitrary",),
                                vmem_limit_bytes=VMEM_LIMIT_BYTES)


def _in_proj_step_kernel(x_hbm, meta_ref, g_ref, w32_ref, uc_ref, ut_ref, ucm_ref, usm_ref,
                         xbuf, sem, w_ref):
    n_chunks = x_hbm.shape[1]
    slot = _fetch_steps(x_hbm, xbuf, sem)

    @pl.when(pl.program_id(0) == 0)
    def _():
        w_ref[...] = w32_ref[...].astype(BF16)
        zm = _rms(meta_ref[...], g_ref[...]).astype(BF16)
        pm = jnp.dot(zm, w_ref[...], preferred_element_type=F32)
        ucm_ref[...] = pm[:, :D_CONV] * jax.nn.sigmoid(pm[:, D_CONV:2 * D_CONV])
        usm_ref[...] = pm[:, 2 * D_CONV:]

    x = xbuf[slot].reshape(SUBLANES * n_chunks, D_MODEL)
    z = _rms(x, g_ref[...]).astype(BF16)
    p = jnp.dot(z, w_ref[...], preferred_element_type=F32)
    uc = p[:, :D_CONV] * jax.nn.sigmoid(p[:, D_CONV:2 * D_CONV])
    uc_ref[...] = uc.reshape(SUBLANES, n_chunks, D_CONV)
    ut = p[:, 2 * D_CONV:].T.astype(BF16)
    for r in range(SUBLANES):
        ut_ref[:, r * SSM_GROUP:(r + 1) * SSM_GROUP, :] = (
            ut[:, r * n_chunks:(r + 1) * n_chunks].reshape(N_GROUPS, SSM_GROUP, n_chunks))


def _in_proj_step(x4, meta, g, w_in):
    bsz, n_chunks = x4.shape[:2]
    rows = bsz * n_chunks
    return pl.pallas_call(
        _in_proj_step_kernel,
        grid=(CHUNK // SUBLANES * bsz,),
        in_specs=[
            pl.BlockSpec(memory_space=pl.ANY),
            pl.BlockSpec((N_META, D_MODEL), lambda t: (0, 0)),
            pl.BlockSpec((1, D_MODEL), lambda t: (0, 0)),
            pl.BlockSpec((D_MODEL, 2 * D_CONV + D_SSM), lambda t: (0, 0),
                         pipeline_mode=pl.Buffered(1)),
        ],
        out_specs=[
            pl.BlockSpec((SUBLANES, n_chunks, D_CONV), lambda t: (t // bsz, t % bsz, 0)),
            pl.BlockSpec((N_GROUPS, SUBLANES * SSM_GROUP, n_chunks),
                         lambda t: (0, t // bsz, t % bsz)),
            pl.BlockSpec((N_META, D_CONV), lambda t: (0, 0)),
            pl.BlockSpec((N_META, D_SSM), lambda t: (0, 0)),
        ],
        out_shape=[
            jax.ShapeDtypeStruct((CHUNK, rows, D_CONV), F32),
            jax.ShapeDtypeStruct((N_GROUPS, CHUNK_LANES, rows), BF16),
            jax.ShapeDtypeStruct((N_META, D_CONV), F32),
            jax.ShapeDtypeStruct((N_META, D_SSM), F32),
        ],
        scratch_shapes=[pltpu.VMEM((2, SUBLANES, n_chunks, D_MODEL), F32),
                        pltpu.SemaphoreType.DMA((2, SUBLANES)),
                        pltpu.VMEM((D_MODEL, 2 * D_CONV + D_SSM), BF16)],
        compiler_params=_step_tile_params(),
        name="in_proj",
    )(x4, meta, g, w_in)


CONV_GROUP = 4
CONV_ROWS = 64


def _conv_kernel(u_ref, um_ref, w_ref, cb_ref, lg_ref, lb_ref, o_ref, cat_ref, acc_ref):
    n_steps, n_chunks, _ = u_ref.shape
    cat_ref[CONV_PAD:CONV_PAD + n_steps] = u_ref[...]
    chunk = lax.broadcasted_iota(jnp.int32, (n_chunks, 1), 0)
    for i in range(CONV_PAD):
        s_prev = n_steps - CONV_PAD + i
        prev = pltpu.roll(u_ref[s_prev], 1, axis=0)
        meta_row = um_ref[N_META - CONV_PAD + i:N_META - CONV_PAD + i + 1, :]
        cat_ref[i] = jnp.where(chunk == 0, meta_row, prev)
        nxt = pltpu.roll(u_ref[i], n_chunks - 1, axis=0)
        cat_ref[n_steps + CONV_PAD + i] = jnp.where(chunk == n_chunks - 1, 0.0, nxt)

    def body(sg, carry):
        s0 = sg * CONV_GROUP
        for c in range(D_CONV // LANES):
            cols = slice(c * LANES, (c + 1) * LANES)
            n_sub = CONV_ROWS // SUBLANES

            def rows_body(rt, carry2):
                r0 = [pl.multiple_of(rt * CONV_ROWS + sub * SUBLANES, SUBLANES)
                      for sub in range(n_sub)]
                accs = [[jnp.zeros((SUBLANES, LANES), F32)] * CONV_GROUP for _ in range(n_sub)]
                for i in range(CONV_WIDTH + CONV_GROUP - 1):
                    ds = [cat_ref[s0 + i, pl.ds(r0[sub], SUBLANES), cols]
                          for sub in range(n_sub)]
                    for j in range(CONV_GROUP):
                        if 0 <= i - j < CONV_WIDTH:
                            tap = jnp.broadcast_to(w_ref[i - j:i - j + 1, cols],
                                                   (SUBLANES, LANES))
                            for sub in range(n_sub):
                                accs[sub][j] = accs[sub][j] + ds[sub] * tap
                for sub in range(n_sub):
                    for j in range(CONV_GROUP):
                        acc_ref[j, pl.ds(r0[sub], SUBLANES), cols] = accs[sub][j]
                return carry2

            lax.fori_loop(0, n_chunks // CONV_ROWS, rows_body, 0)
        for j in range(CONV_GROUP):
            y = acc_ref[j] + cb_ref[...]
            yc = y - jnp.mean(y, axis=-1, keepdims=True)
            yn = yc * lax.rsqrt(jnp.mean(yc * yc, axis=-1, keepdims=True) + NORM_EPS)
            yn = yn * lg_ref[...] + lb_ref[...]
            o_ref[s0 + j] = (yn * jax.nn.sigmoid(yn)).astype(BF16)
        return carry

    lax.fori_loop(0, n_steps // CONV_GROUP, body, 0)


def _conv_module(u_conv, u_conv_meta, conv_w, conv_b, ln_g, ln_b, bsz):
    n_steps, rows, _ = u_conv.shape
    n_chunks = rows // bsz
    vec = pl.BlockSpec((1, D_CONV), lambda b: (0, 0))
    blk = pl.BlockSpec((n_steps, n_chunks, D_CONV), lambda b: (0, b, 0))
    return pl.pallas_call(
        _conv_kernel,
        grid=(bsz,),
        in_specs=[
            blk,
            pl.BlockSpec((N_META, D_CONV), lambda b: (0, 0)),
            pl.BlockSpec((CONV_WIDTH, D_CONV), lambda b: (0, 0)),
            vec, vec, vec,
        ],
        out_specs=blk,
        out_shape=jax.ShapeDtypeStruct((n_steps, rows, D_CONV), BF16),
        scratch_shapes=[pltpu.VMEM((n_steps + 2 * CONV_PAD, n_chunks, D_CONV), F32),
                        pltpu.VMEM((CONV_GROUP, n_chunks, D_CONV), F32)],
        compiler_params=_params(),
        name="conv_module",
    )(u_conv, u_conv_meta, conv_w, conv_b, ln_g, ln_b)


def _zoh_kernel(lre_ref, lim_ref, ldt_ref, lbr_ref, lbi_ref, zr_ref, zi_ref):
    lre = lre_ref[...]
    lim = lim_ref[...]
    dt = jnp.exp(ldt_ref[...])
    ea = jnp.exp(lre * dt)
    lbr = ea * jnp.cos(lim * dt)
    lbi = ea * jnp.sin(lim * dt)
    nr = lbr - 1.0
    den = lre * lre + lim * lim
    lbr_ref[...] = lbr
    lbi_ref[...] = lbi
    zr_ref[...] = (nr * lre + lbi * lim) / den
    zi_ref[...] = (lbi * lre - nr * lim) / den


def _zoh(lam_re, lam_im, log_dt):
    shape = jax.ShapeDtypeStruct(lam_re.shape, F32)
    return pl.pallas_call(_zoh_kernel, out_shape=[shape] * 4, name="s5_zoh")(
        lam_re, lam_im, log_dt)


def _lanes(x, n):
    return jnp.concatenate([x] * (n // LANES), axis=1)


def _cmul(ar, ai, br, bi):
    return ar * br - ai * bi, ar * bi + ai * br


def _cpow(br, bi, expo, nbits):
    rr = jnp.ones(expo.shape, F32)
    ri = jnp.zeros(expo.shape, F32)
    for k in range(nbits):
        bit = ((expo >> k) & 1) == 1
        nr, ni = _cmul(rr, ri, br, bi)
        rr = jnp.where(bit, nr, rr)
        ri = jnp.where(bit, ni, ri)
        br, bi = _cmul(br, bi, br, bi)
    return rr, ri


def _build_chunk_ops(gi, csc_ref, cc_ref, bc_ref, cr_ref, um_ref, toep_ref, wend_ref,
                     wout_ref):
    t = CHUNK
    rows = STATE_ROWS
    n_cols = CHUNK_LANES // LANES
    sc_t = csc_ref[gi].T
    lam = (jnp.broadcast_to(sc_t[:, 0:1], (rows, LANES)),
           jnp.broadcast_to(sc_t[:, 1:2], (rows, LANES)))
    zr = jnp.broadcast_to(sc_t[:, 2:3], (rows, LANES))
    zi = jnp.broadcast_to(sc_t[:, 3:4], (rows, LANES))
    sel = (lax.broadcasted_iota(jnp.int32, (SSM_GROUP, LANES), 1) % SSM_GROUP
           == lax.broadcasted_iota(jnp.int32, (SSM_GROUP, LANES), 0)).astype(F32)

    def tile(ref):
        return jnp.dot(ref[gi], sel, precision=lax.Precision.HIGHEST,
                       preferred_element_type=F32)

    def swap_re_im(a):
        p = SSM_STATE
        return jnp.concatenate([a[p:2 * p], a[0:p], a[3 * p:], a[2 * p:3 * p]], axis=0)

    c_same, b_same = tile(cc_ref), tile(bc_ref)
    c_swap, b_swap = swap_re_im(c_same), swap_re_im(b_same)
    row = lax.broadcasted_iota(jnp.int32, (rows, LANES), 0)
    is_re = (row // SSM_STATE) % 2 == 0
    sgn = jnp.where(is_re, -1.0, 1.0)
    ca = jnp.where(is_re, c_same, -c_same)
    cb = -c_swap
    bb_same = zr * b_same + sgn * zi * b_swap
    sbb_swap = sgn * (zr * b_swap - sgn * zi * b_same)

    pows = {1: lam}
    k = 1
    while k < t:
        pows[2 * k] = _cmul(*pows[k], *pows[k])
        k *= 2
    step_bits = STEPS_PER_VREG.bit_length() - 1
    fwd = slice(0, DIR_ROWS)
    bwd = slice(DIR_ROWS, rows)
    every = slice(0, rows)

    def rows_of(v, rs):
        return v[0][rs], v[1][rs]

    def column(base, e, rs):
        out = rows_of(base, rs)
        for bit, val in pows.items():
            if e & bit:
                out = _cmul(*out, *rows_of(val, rs))
        return out

    def times_b(p, rs):
        return p[0] * bb_same[rs] + p[1] * sbb_swap[rs]

    def times_c(p, rs):
        return ca[rs] * p[0] + cb[rs] * p[1]

    i8 = lax.broadcasted_iota(jnp.int32, (rows, LANES), 1) // SSM_GROUP
    asc0 = _cpow(*lam, i8, step_bits)
    desc0 = _cpow(*lam, STEPS_PER_VREG - 1 - i8, step_bits)
    asc1_0 = _cmul(*asc0, *lam)
    desc1_0 = _cmul(*desc0, *lam)

    wend_f, wend_b, lag_b, wout_f, wout_b = [], [], [], [], []
    for q in range(n_cols):
        e_asc = STEPS_PER_VREG * q
        e_desc = STEPS_PER_VREG * (n_cols - 1 - q)
        asc1_q = column(asc1_0, e_asc, every)
        wend_f.append(times_b(column(desc0, e_desc, fwd), fwd))
        wend_b.append(times_b(column(asc0, e_asc, bwd), bwd))
        lag_b.append(times_b(rows_of(asc1_q, bwd), bwd))
        wout_f.append(times_c(rows_of(asc1_q, fwd), fwd))
        wout_b.append(times_c(column(desc1_0, e_desc, bwd), bwd))
        cols = slice(q * LANES, (q + 1) * LANES)
        wend_ref[gi, fwd, cols] = wend_f[q].astype(BF16)
        wend_ref[gi, bwd, cols] = wend_b[q].astype(BF16)

    meta_cols = N_META * SSM_GROUP // LANES
    x0 = sum(jnp.sum(wend_f[n_cols - meta_cols + m] * um_ref[gi, :, m * LANES:(m + 1) * LANES],
                     axis=1, keepdims=True) for m in range(meta_cols))

    lane = lax.broadcasted_iota(jnp.int32, (DIR_ROWS, LANES), 1)
    lag0_b = jnp.where(lane >= LANES - SSM_GROUP, bb_same[bwd], 0.0)
    zero = jnp.zeros((DIR_ROWS, LANES), F32)
    bcat = jnp.concatenate(
        [jnp.concatenate(wend_f + [zero] * n_cols, axis=1),
         jnp.concatenate([zero] * (n_cols - 1) + [lag0_b] + lag_b, axis=1)], axis=0)
    lane_r = lax.broadcasted_iota(jnp.int32, (SSM_GROUP, rows), 1)
    c2 = jnp.where((lane_r // SSM_STATE) % 2 == 0, cr_ref[gi], -cr_ref[gi])
    kk = jnp.dot(c2, bcat, precision=lax.Precision.HIGHEST,
                 preferred_element_type=F32)
    for tt in range(t):
        off = (t - 1 - tt) * SSM_GROUP
        toep_ref[gi, tt * SSM_GROUP:(tt + 1) * SSM_GROUP, :] = (
            kk[:, off:off + CHUNK_LANES].astype(BF16))

    wout = jnp.concatenate([jnp.concatenate(wout_f, axis=1),
                            jnp.concatenate(wout_b, axis=1)], axis=0)
    wout_ref[gi] = wout.T.astype(BF16)
    return pows[t], x0


SSM_GROUPS_PER_STEP = 4


def _chunk_scan(er, ei, ar, ai, x0, chunk, n_chunks, forward):
    width = er.shape[1]
    xr, xi = er, ei
    if x0 is not None:
        fr, fi = _cmul(ar, ai, x0[0], x0[1])
        xr = xr + jnp.where(chunk == 0, _lanes(fr, width), 0.0)
        xi = xi + jnp.where(chunk == 0, _lanes(fi, width), 0.0)

    def shifted(v, sh):
        if forward:
            return jnp.where(chunk >= sh, pltpu.roll(v, sh, axis=1), 0.0)
        return jnp.where(chunk < n_chunks - sh, pltpu.roll(v, width - sh, axis=1), 0.0)

    sh = 1
    while sh < n_chunks:
        sr, si = shifted(xr, sh), shifted(xi, sh)
        wr, wi = _lanes(ar, width), _lanes(ai, width)
        xr, xi = xr + wr * sr - wi * si, xi + wr * si + wi * sr
        ar, ai = _cmul(ar, ai, ar, ai)
        sh *= 2
    inr, ini = shifted(xr, 1), shifted(xi, 1)
    if x0 is not None:
        inr = jnp.where(chunk == 0, x0[0], inr)
        ini = jnp.where(chunk == 0, x0[1], ini)
    return inr, ini


def _ssm_kernel(u_ref, um_ref, csc_ref, cc_ref, bc_ref, cr_ref, y_ref, toep_ref, wend_ref,
                wout_ref, yi_ref, *, n_chunks):
    for gi in range(SSM_GROUPS_PER_STEP):
        (lam_r, lam_i), x0 = _build_chunk_ops(gi, csc_ref, cc_ref, bc_ref, cr_ref, um_ref,
                                              toep_ref, wend_ref, wout_ref)
        u = u_ref[gi]
        width = u.shape[1]
        e = jnp.dot(wend_ref[gi], u, preferred_element_type=F32)
        yi_ref[gi] = jnp.dot(toep_ref[gi], u, preferred_element_type=F32)
        chunk =lax.broadcasted_iota(jnp.int32, (1, width), 1) % n_chunks
        p = SSM_STATE
        ar, ai = lam_r, lam_i
        f_in = _chunk_scan(e[0:p], e[p:2 * p], ar[0:p], ai[0:p], (x0[0:p], x0[p:2 * p]),
                           chunk, n_chunks, True)
        b_in = _chunk_scan(e[2 * p:3 * p], e[3 * p:], ar[2 * p:3 * p], ai[2 * p:3 * p],
                           None, chunk, n_chunks, False)
        xin = jnp.concatenate([f_in[0], f_in[1], b_in[0], b_in[1]], axis=0).astype(BF16)
        y = yi_ref[gi] + jnp.dot(wout_ref[gi], xin, preferred_element_type=F32)
        y_ref[gi] = y.astype(BF16)


def _ssm(u_col, u_meta, csc, c_col, b_col, c_row, n_chunks):
    g, _, width = u_col.shape
    per = SSM_GROUPS_PER_STEP
    return pl.pallas_call(
        functools.partial(_ssm_kernel, n_chunks=n_chunks),
        grid=(g // per,),
        in_specs=[
            pl.BlockSpec((per, CHUNK_LANES, width), lambda i: (i, 0, 0)),
            pl.BlockSpec((per, 1, N_META * SSM_GROUP), lambda i: (i, 0, 0)),
            pl.BlockSpec((per, SUBLANES, STATE_ROWS), lambda i: (i, 0, 0)),
            pl.BlockSpec((per, STATE_ROWS, SSM_GROUP), lambda i: (i, 0, 0)),
            pl.BlockSpec((per, STATE_ROWS, SSM_GROUP), lambda i: (i, 0, 0)),
            pl.BlockSpec((per, SSM_GROUP, STATE_ROWS), lambda i: (i, 0, 0)),
        ],
        out_specs=pl.BlockSpec((per, CHUNK_LANES, width), lambda i: (i, 0, 0)),
        out_shape=jax.ShapeDtypeStruct((g, CHUNK_LANES, width), BF16),
        scratch_shapes=[
            pltpu.VMEM((per, CHUNK_LANES, CHUNK_LANES), BF16),
            pltpu.VMEM((per, STATE_ROWS, CHUNK_LANES), BF16),
            pltpu.VMEM((per, CHUNK_LANES, STATE_ROWS), BF16),
            pltpu.VMEM((per, CHUNK_LANES, width), F32),
        ],
        compiler_params=_params(),
        name="s5_mixer",
    )(u_col, u_meta, csc, c_col, b_col, c_row)


FF_CHUNK = 1024
MIX_PER = 4


MIX_SLOTS = 3


def _mix_ffn_kernel(x_hbm, co_ref, y_ref, ut_ref, d_ref, gwt_ref, gb_ref, wo_ref,
                    gf_ref, w1_ref, w2_ref, gl_ref, o_hbm, buf, sem_in, sem_out,
                    mixed_ref, h_ref, z_ref, a_ref):
    n_chunks = x_hbm.shape[1]
    rows = SUBLANES * n_chunks
    step = pl.program_id(0)
    n_tiles = pl.num_programs(0) - 1
    tile = step - 1

    def mix_stages(dst):
        per = MIX_PER
        n_half = SUBLANES // per
        state = {}

        def col(ref, k):
            return jnp.concatenate(
                [ref[:, r * SSM_GROUP:(r + 1) * SSM_GROUP, :].reshape(D_SSM, n_chunks)
                 for r in range(k * per, (k + 1) * per)], axis=1).astype(F32)

        def gelu_stage(k):
            def run():
                state["ge", k] = jax.nn.gelu(col(y_ref, k) + d_ref[...] * col(ut_ref, k))
            return run

        def glu_stage(k):
            def run():
                ge = state["ge", k]
                gate = (jnp.dot(gwt_ref[...], ge.astype(BF16), preferred_element_type=F32)
                        + gb_ref[...])
                state["s5t", k] = ge * jax.nn.sigmoid(gate)
            return run

        def store_stage(k):
            def run():
                rs = slice(k * per * n_chunks, (k + 1) * per * n_chunks)
                mixed_ref[dst, rs, :D_CONV] = co_ref[k * per:(k + 1) * per].reshape(
                    per * n_chunks, D_CONV)
                mixed_ref[dst, rs, D_CONV:] = state["s5t", k].T.astype(BF16)
            return run

        return [f(k) for f in (gelu_stage, glu_stage, store_stage) for k in range(n_half)]

    @pl.when(step == 0)
    def _():
        for cp in _step_copies(x_hbm, buf, sem_in, 0, 0, False):
            cp.start()
        for stage in mix_stages(0):
            stage()

    @pl.when(step >= 1)
    def _():
        slot = tile % MIX_SLOTS
        nxt = (tile + 1) % MIX_SLOTS

        @pl.when(tile >= 2)
        def _():
            for cp in _step_copies(o_hbm, buf, sem_out, tile - 2, nxt, True):
                cp.wait()

        @pl.when(tile + 1 < n_tiles)
        def _():
            for cp in _step_copies(x_hbm, buf, sem_in, tile + 1, nxt, False):
                cp.start()

        for cp in _step_copies(x_hbm, buf, sem_in, tile, slot, False):
            cp.wait()

        stages = mix_stages(step % 2)
        half = D_FF // 2
        n_slots = D_FF // FF_CHUNK + 2
        n_stages = len(stages)
        done = [0, 0]

        def fill_slot():
            done[0] += 1
            while done[1] * n_slots < done[0] * n_stages:
                stages.pop(0)()
                done[1] += 1

        h = buf[slot].reshape(rows, D_MODEL) + jnp.dot(
            mixed_ref[tile % 2], wo_ref[...], preferred_element_type=F32)
        z_ref[...] = _rms(h, gf_ref[...]).astype(BF16)
        h_ref[...] = h
        for k in range(2):
            for j in range(half // FF_CHUNK):
                cols = slice(k * half + j * FF_CHUNK, k * half + (j + 1) * FF_CHUNK)
                a = jnp.dot(z_ref[...], w1_ref[:, cols], preferred_element_type=F32)
                a_ref[:, j * FF_CHUNK:(j + 1) * FF_CHUNK] = (
                    jnp.square(jnp.maximum(a, 0.0)).astype(BF16))
                fill_slot()
            h_ref[...] += jnp.dot(a_ref[...], w2_ref[k * half:(k + 1) * half, :],
                                  preferred_element_type=F32)
            fill_slot()
        assert not stages
        buf[slot] = _rms(h_ref[...], gl_ref[...]).reshape(SUBLANES, n_chunks, D_MODEL)

        for cp in _step_copies(o_hbm, buf, sem_out, tile, slot, True):
            cp.start()

        @pl.when(tile == n_tiles - 1)
        def _():
            for cp in (_step_copies(o_hbm, buf, sem_out, tile - 1, (tile - 1) % MIX_SLOTS, True)
                       + _step_copies(o_hbm, buf, sem_out, tile, slot, True)):
                cp.wait()


def _mix_ffn(x4, conv_out, y_col, u_col, d_col, glu_wt, glu_b_col, w_out, g_ffn, w1, w2,
             g_final):
    bsz, n_chunks = x4.shape[:2]

    def const(shape):
        return pl.BlockSpec(shape, lambda t: (0, 0), pipeline_mode=pl.Buffered(1))

    n_tiles = CHUNK // SUBLANES * bsz
    hbm = pl.BlockSpec(memory_space=pl.ANY)

    def slab(s):
        return jnp.minimum(s, n_tiles - 1) // bsz

    def batch(s):
        return jnp.minimum(s, n_tiles - 1) % bsz

    col_blk = pl.BlockSpec((N_GROUPS, SUBLANES * SSM_GROUP, n_chunks),
                           lambda s: (0, slab(s), batch(s)))
    step_sems = pltpu.SemaphoreType.DMA((MIX_SLOTS, SUBLANES))
    return pl.pallas_call(
        _mix_ffn_kernel,
        grid=(n_tiles + 1,),
        in_specs=[
            hbm,
            pl.BlockSpec((SUBLANES, n_chunks, D_CONV), lambda s: (slab(s), batch(s), 0)),
            col_blk,
            col_blk,
            const((D_SSM, 1)),
            const((D_SSM, D_SSM)),
            const((D_SSM, 1)),
            const((D_MODEL, D_MODEL)),
            const((1, D_MODEL)),
            const((D_MODEL, D_FF)),
            const((D_FF, D_MODEL)),
            const((1, D_MODEL)),
        ],
        out_specs=hbm,
        out_shape=jax.ShapeDtypeStruct(x4.shape, F32),
        scratch_shapes=[
            pltpu.VMEM((MIX_SLOTS, SUBLANES, n_chunks, D_MODEL), F32),
            step_sems, step_sems,
            pltpu.VMEM((2, SUBLANES * n_chunks, D_MODEL), BF16),
            pltpu.VMEM((SUBLANES * n_chunks, D_MODEL), F32),
            pltpu.VMEM((SUBLANES * n_chunks, D_MODEL), BF16),
            pltpu.VMEM((SUBLANES * n_chunks, D_FF // 2), BF16),
        ],
        compiler_params=_step_tile_params(),
        name="mix_ffn",
    )(x4, conv_out, y_col, u_col, d_col, glu_wt, glu_b_col, w_out, g_ffn, w1, w2, g_final)


def _s5_operator_inputs(lbr, lbi, zr, zi, b_re, b_im, c_re, c_im):
    g, p, h = N_GROUPS, SSM_STATE, SSM_GROUP

    sc = jnp.transpose(jnp.stack([lbr, lbi, zr, zi] + [jnp.zeros_like(zr)] * 4, axis=0),
                       (2, 0, 1, 3))
    csc = jnp.broadcast_to(sc[:, :, :, None], (g, SUBLANES, 2, 2, p)).reshape(g, SUBLANES, 4 * p)
    c = jnp.stack([c_re, c_im], axis=0)
    b = jnp.stack([b_re, b_im], axis=0)
    c_col = jnp.transpose(c, (2, 1, 0, 4, 3)).reshape(g, 4 * p, h)
    b_col = jnp.transpose(b, (2, 1, 0, 3, 4)).reshape(g, 4 * p, h)
    c_row = jnp.transpose(c, (2, 3, 1, 0, 4)).reshape(g, h, 4 * p)
    return csc, c_col, b_col, c_row


def kernel(x, meta_tokens, norm_mix_g, w_in, conv_w, conv_b, conv_ln_g, conv_ln_b,
           ssm_lam_re, ssm_lam_im, ssm_log_dt, ssm_b_re, ssm_b_im, ssm_c_re, ssm_c_im,
           ssm_d, ssm_glu_w, ssm_glu_b, w_out, norm_ffn_g, w_ff1, w_ff2, norm_final_g):
    assert w_in.shape[0] == 1, "single-layer block"
    bsz, seq, _ = x.shape
    assert seq % CHUNK == 0 and CHUNK >= N_META
    n_chunks = seq // CHUNK
    g, h = N_GROUPS, SSM_GROUP

    x4 = x.reshape(bsz, n_chunks, CHUNK, D_MODEL)
    g_mix = norm_mix_g[0][None, :]
    u_conv, u_col, u_conv_m, u_ssm_m = _in_proj_step(x4, meta_tokens, g_mix, w_in[0])

    conv_out = _conv_module(u_conv, u_conv_m, conv_w[0], conv_b[0][None, :],
                            conv_ln_g[0][None, :], conv_ln_b[0][None, :], bsz)

    ldt = jnp.broadcast_to(ssm_log_dt[0][..., None], ssm_lam_re[0].shape)
    flat = lambda a: a.reshape(2 * g, SSM_STATE)
    lbr, lbi, zr, zi = [a.reshape(2, g, SSM_STATE) for a in
                        _zoh(flat(ssm_lam_re[0]), flat(ssm_lam_im[0]), flat(ldt))]
    csc, c_col, b_col, c_row = _s5_operator_inputs(
        lbr, lbi, zr, zi, ssm_b_re[0], ssm_b_im[0], ssm_c_re[0], ssm_c_im[0])

    u_meta = jnp.transpose(u_ssm_m.reshape(N_META, g, h), (1, 0, 2)).reshape(g, 1, N_META * h)
    y_col = _ssm(u_col, u_meta, csc, c_col, b_col, c_row, n_chunks)

    out = _mix_ffn(x4, conv_out, y_col, u_col, ssm_d[0][:, None],
                   ssm_glu_w[0].T.astype(BF16), ssm_glu_b[0][:, None],
                   w_out[0].astype(BF16), norm_ffn_g[0][None, :], w_ff1[0].astype(BF16),
                   w_ff2[0].astype(BF16), norm_final_g[None, :])
    return out.reshape(bsz, seq, D_MODEL)
```

```python
import functools

import jax
import jax.numpy as jnp
from jax import lax
from jax.experimental import pallas as pl
from jax.experimental.pallas import tpu as pltpu

F32 = jnp.float32
BF16 = jnp.bfloat16

D_MODEL = 1024
N_META = 16
D_CONV = 512
D_SSM = 512
CONV_WIDTH = 31
CONV_PAD = CONV_WIDTH // 2
SSM_GROUP = 16
N_GROUPS = D_SSM // SSM_GROUP
SSM_STATE = 64
D_FF = 4096
NORM_EPS = 1e-5
LANES = 128
SUBLANES = 8

CHUNK = 32
CHUNK_LANES = CHUNK * SSM_GROUP
STATE_ROWS = 4 * SSM_STATE
DIR_ROWS = 2 * SSM_STATE
STEPS_PER_VREG = LANES // SSM_GROUP

VMEM_LIMIT_BYTES = 60 * 1024 * 1024


def _rms(x, g):
    return x * lax.rsqrt(jnp.mean(x * x, axis=-1, keepdims=True) + NORM_EPS) * g


def _params(n_axes=1):
    return pltpu.CompilerParams(dimension_semantics=("parallel",) * n_axes,
                                vmem_limit_bytes=VMEM_LIMIT_BYTES)


def _step_copies(hbm_ref, buf_ref, sem_ref, tile, slot, to_hbm):
    bsz = hbm_ref.shape[0]
    slab, b = tile // bsz, tile % bsz
    copies = []
    for r in range(SUBLANES):
        hbm = hbm_ref.at[b, :, slab * SUBLANES + r, :]
        vmem = buf_ref.at[slot, r]
        src, dst = (vmem, hbm) if to_hbm else (hbm, vmem)
        copies.append(pltpu.make_async_copy(src, dst, sem_ref.at[slot, r]))
    return copies


def _fetch_steps(x_hbm, xbuf, sem):
    tile = pl.program_id(0)
    slot = tile % 2

    @pl.when(tile == 0)
    def _():
        for cp in _step_copies(x_hbm, xbuf, sem, tile, slot, False):
            cp.start()

    @pl.when(tile + 1 < pl.num_programs(0))
    def _():
        for cp in _step_copies(x_hbm, xbuf, sem, tile + 1, 1 - slot, False):
            cp.start()

    for cp in _step_copies(x_hbm, xbuf, sem, tile, slot, False):
        cp.wait()
    return slot


def _step_tile_params():
    return pltpu.CompilerParams(dimension_semantics=("arbitrary",),
                                vmem_limit_bytes=VMEM_LIMIT_BYTES)


def _in_proj_step_kernel(x_hbm, meta_ref, g_ref, w32_ref, uc_ref, ut_ref, ucm_ref, usm_ref,
                         xbuf, sem, w_ref):
    n_chunks = x_hbm.shape[1]
    slot = _fetch_steps(x_hbm, xbuf, sem)

    @pl.when(pl.program_id(0) == 0)
    def _():
        w_ref[...] = w32_ref[...].astype(BF16)
        zm = _rms(meta_ref[...], g_ref[...]).astype(BF16)
        pm = jnp.dot(zm, w_ref[...], preferred_element_type=F32)
        ucm_ref[...] = pm[:, :D_CONV] * jax.nn.sigmoid(pm[:, D_CONV:2 * D_CONV])
        usm_ref[...] = pm[:, 2 * D_CONV:]

    x = xbuf[slot].reshape(SUBLANES * n_chunks, D_MODEL)
    z = _rms(x, g_ref[...]).astype(BF16)
    p = jnp.dot(z, w_ref[...], preferred_element_type=F32)
    uc = p[:, :D_CONV] * jax.nn.sigmoid(p[:, D_CONV:2 * D_CONV])
    uc_ref[...] = uc.reshape(SUBLANES, n_chunks, D_CONV)
    ut = p[:, 2 * D_CONV:].T.astype(BF16)
    for r in range(SUBLANES):
        ut_ref[:, r * SSM_GROUP:(r + 1) * SSM_GROUP, :] = (
            ut[:, r * n_chunks:(r + 1) * n_chunks].reshape(N_GROUPS, SSM_GROUP, n_chunks))


def _in_proj_step(x4, meta, g, w_in):
    bsz, n_chunks = x4.shape[:2]
    rows = bsz * n_chunks
    return pl.pallas_call(
        _in_proj_step_kernel,
        grid=(CHUNK // SUBLANES * bsz,),
        in_specs=[
            pl.BlockSpec(memory_space=pl.ANY),
            pl.BlockSpec((N_META, D_MODEL), lambda t: (0, 0)),
            pl.BlockSpec((1, D_MODEL), lambda t: (0, 0)),
            pl.BlockSpec((D_MODEL, 2 * D_CONV + D_SSM), lambda t: (0, 0),
                         pipeline_mode=pl.Buffered(1)),
        ],
        out_specs=[
            pl.BlockSpec((SUBLANES, n_chunks, D_CONV), lambda t: (t // bsz, t % bsz, 0)),
            pl.BlockSpec((N_GROUPS, SUBLANES * SSM_GROUP, n_chunks),
                         lambda t: (0, t // bsz, t % bsz)),
            pl.BlockSpec((N_META, D_CONV), lambda t: (0, 0)),
            pl.BlockSpec((N_META, D_SSM), lambda t: (0, 0)),
        ],
        out_shape=[
            jax.ShapeDtypeStruct((CHUNK, rows, D_CONV), F32),
            jax.ShapeDtypeStruct((N_GROUPS, CHUNK_LANES, rows), BF16),
            jax.ShapeDtypeStruct((N_META, D_CONV), F32),
            jax.ShapeDtypeStruct((N_META, D_SSM), F32),
        ],
        scratch_shapes=[pltpu.VMEM((2, SUBLANES, n_chunks, D_MODEL), F32),
                        pltpu.SemaphoreType.DMA((2, SUBLANES)),
                        pltpu.VMEM((D_MODEL, 2 * D_CONV + D_SSM), BF16)],
        compiler_params=_step_tile_params(),
        name="in_proj",
    )(x4, meta, g, w_in)


CONV_GROUP = 4
CONV_ROWS = 64


def _conv_kernel(u_ref, um_ref, w_ref, cb_ref, lg_ref, lb_ref, o_ref, cat_ref, acc_ref):
    n_steps, n_chunks, _ = u_ref.shape
    cat_ref[CONV_PAD:CONV_PAD + n_steps] = u_ref[...]
    chunk = lax.broadcasted_iota(jnp.int32, (n_chunks, 1), 0)
    for i in range(CONV_PAD):
        s_prev = n_steps - CONV_PAD + i
        prev = pltpu.roll(u_ref[s_prev], 1, axis=0)
        meta_row = um_ref[N_META - CONV_PAD + i:N_META - CONV_PAD + i + 1, :]
        cat_ref[i] = jnp.where(chunk == 0, meta_row, prev)
        nxt = pltpu.roll(u_ref[i], n_chunks - 1, axis=0)
        cat_ref[n_steps + CONV_PAD + i] = jnp.where(chunk == n_chunks - 1, 0.0, nxt)

    def body(sg, carry):
        s0 = sg * CONV_GROUP
        for c in range(D_CONV // LANES):
            cols = slice(c * LANES, (c + 1) * LANES)
            n_sub = CONV_ROWS // SUBLANES

            def rows_body(rt, carry2):
                r0 = [pl.multiple_of(rt * CONV_ROWS + sub * SUBLANES, SUBLANES)
                      for sub in range(n_sub)]
                accs = [[jnp.zeros((SUBLANES, LANES), F32)] * CONV_GROUP for _ in range(n_sub)]
                for i in range(CONV_WIDTH + CONV_GROUP - 1):
                    ds = [cat_ref[s0 + i, pl.ds(r0[sub], SUBLANES), cols]
                          for sub in range(n_sub)]
                    for j in range(CONV_GROUP):
                        if 0 <= i - j < CONV_WIDTH:
                            tap = jnp.broadcast_to(w_ref[i - j:i - j + 1, cols],
                                                   (SUBLANES, LANES))
                            for sub in range(n_sub):
                                accs[sub][j] = accs[sub][j] + ds[sub] * tap
                for sub in range(n_sub):
                    for j in range(CONV_GROUP):
                        acc_ref[j, pl.ds(r0[sub], SUBLANES), cols] = accs[sub][j]
                return carry2

            lax.fori_loop(0, n_chunks // CONV_ROWS, rows_body, 0)
        for j in range(CONV_GROUP):
            y = acc_ref[j] + cb_ref[...]
            yc = y - jnp.mean(y, axis=-1, keepdims=True)
            yn = yc * lax.rsqrt(jnp.mean(yc * yc, axis=-1, keepdims=True) + NORM_EPS)
            yn = yn * lg_ref[...] + lb_ref[...]
            o_ref[s0 + j] = (yn * jax.nn.sigmoid(yn)).astype(BF16)
        return carry

    lax.fori_loop(0, n_steps // CONV_GROUP, body, 0)


def _conv_module(u_conv, u_conv_meta, conv_w, conv_b, ln_g, ln_b, bsz):
    n_steps, rows, _ = u_conv.shape
    n_chunks = rows // bsz
    vec = pl.BlockSpec((1, D_CONV), lambda b: (0, 0))
    blk = pl.BlockSpec((n_steps, n_chunks, D_CONV), lambda b: (0, b, 0))
    return pl.pallas_call(
        _conv_kernel,
        grid=(bsz,),
        in_specs=[
            blk,
            pl.BlockSpec((N_META, D_CONV), lambda b: (0, 0)),
            pl.BlockSpec((CONV_WIDTH, D_CONV), lambda b: (0, 0)),
            vec, vec, vec,
        ],
        out_specs=blk,
        out_shape=jax.ShapeDtypeStruct((n_steps, rows, D_CONV), BF16),
        scratch_shapes=[pltpu.VMEM((n_steps + 2 * CONV_PAD, n_chunks, D_CONV), F32),
                        pltpu.VMEM((CONV_GROUP, n_chunks, D_CONV), F32)],
        compiler_params=_params(),
        name="conv_module",
    )(u_conv, u_conv_meta, conv_w, conv_b, ln_g, ln_b)


def _zoh_kernel(lre_ref, lim_ref, ldt_ref, lbr_ref, lbi_ref, zr_ref, zi_ref):
    lre = lre_ref[...]
    lim = lim_ref[...]
    dt = jnp.exp(ldt_ref[...])
    ea = jnp.exp(lre * dt)
    lbr = ea * jnp.cos(lim * dt)
    lbi = ea * jnp.sin(lim * dt)
    nr = lbr - 1.0
    den = lre * lre + lim * lim
    lbr_ref[...] = lbr
    lbi_ref[...] = lbi
    zr_ref[...] = (nr * lre + lbi * lim) / den
    zi_ref[...] = (lbi * lre - nr * lim) / den


def _zoh(lam_re, lam_im, log_dt):
    shape = jax.ShapeDtypeStruct(lam_re.shape, F32)
    return pl.pallas_call(_zoh_kernel, out_shape=[shape] * 4, name="s5_zoh")(
        lam_re, lam_im, log_dt)


def _lanes(x, n):
    return jnp.concatenate([x] * (n // LANES), axis=1)


def _cmul(ar, ai, br, bi):
    return ar * br - ai * bi, ar * bi + ai * br


def _cpow(br, bi, expo, nbits):
    rr = jnp.ones(expo.shape, F32)
    ri = jnp.zeros(expo.shape, F32)
    for k in range(nbits):
        bit = ((expo >> k) & 1) == 1
        nr, ni = _cmul(rr, ri, br, bi)
        rr = jnp.where(bit, nr, rr)
        ri = jnp.where(bit, ni, ri)
        br, bi = _cmul(br, bi, br, bi)
    return rr, ri


def _build_chunk_ops(gi, csc_ref, cc_ref, bc_ref, cr_ref, um_ref, toep_ref, wend_ref,
                     wout_ref):
    t = CHUNK
    rows = STATE_ROWS
    n_cols = CHUNK_LANES // LANES
    lam = (jnp.broadcast_to(csc_ref[gi, :, 0:1], (rows, LANES)),
           jnp.broadcast_to(csc_ref[gi, :, 1:2], (rows, LANES)))
    zr = jnp.broadcast_to(csc_ref[gi, :, 2:3], (rows, LANES))
    zi = jnp.broadcast_to(csc_ref[gi, :, 3:4], (rows, LANES))
    sel = (lax.broadcasted_iota(jnp.int32, (SSM_GROUP, LANES), 1) % SSM_GROUP
           == lax.broadcasted_iota(jnp.int32, (SSM_GROUP, LANES), 0)).astype(F32)

    def tile(ref):
        return jnp.dot(ref[gi], sel, precision=lax.Precision.HIGHEST,
                       preferred_element_type=F32)

    def swap_re_im(a):
        p = SSM_STATE
        return jnp.concatenate([a[p:2 * p], a[0:p], a[3 * p:], a[2 * p:3 * p]], axis=0)

    c_same, b_same = tile(cc_ref), tile(bc_ref)
    c_swap, b_swap = swap_re_im(c_same), swap_re_im(b_same)
    row = lax.broadcasted_iota(jnp.int32, (rows, LANES), 0)
    is_re = (row // SSM_STATE) % 2 == 0
    sgn = jnp.where(is_re, -1.0, 1.0)
    ca = jnp.where(is_re, c_same, -c_same)
    cb = -c_swap
    bb_same = zr * b_same + sgn * zi * b_swap
    sbb_swap = sgn * (zr * b_swap - sgn * zi * b_same)

    pows = {1: lam}
    k = 1
    while k < t:
        pows[2 * k] = _cmul(*pows[k], *pows[k])
        k *= 2
    step_bits = STEPS_PER_VREG.bit_length() - 1
    fwd = slice(0, DIR_ROWS)
    bwd = slice(DIR_ROWS, rows)
    every = slice(0, rows)

    def rows_of(v, rs):
        return v[0][rs], v[1][rs]

    def column(base, e, rs):
        out = rows_of(base, rs)
        for bit, val in pows.items():
            if e & bit:
                out = _cmul(*out, *rows_of(val, rs))
        return out

    def times_b(p, rs):
        return p[0] * bb_same[rs] + p[1] * sbb_swap[rs]

    def times_c(p, rs):
        return ca[rs] * p[0] + cb[rs] * p[1]

    i8 = lax.broadcasted_iota(jnp.int32, (rows, LANES), 1) // SSM_GROUP
    asc0 = _cpow(*lam, i8, step_bits)
    desc0 = _cpow(*lam, STEPS_PER_VREG - 1 - i8, step_bits)
    asc1_0 = _cmul(*asc0, *lam)
    desc1_0 = _cmul(*desc0, *lam)

    wend_f, wend_b, lag_b, wout_f, wout_b = [], [], [], [], []
    for q in range(n_cols):
        e_asc = STEPS_PER_VREG * q
        e_desc = STEPS_PER_VREG * (n_cols - 1 - q)
        asc1_q = column(asc1_0, e_asc, every)
        wend_f.append(times_b(column(desc0, e_desc, fwd), fwd))
        wend_b.append(times_b(column(asc0, e_asc, bwd), bwd))
        lag_b.append(times_b(rows_of(asc1_q, bwd), bwd))
        wout_f.append(times_c(rows_of(asc1_q, fwd), fwd))
        wout_b.append(times_c(column(desc1_0, e_desc, bwd), bwd))
        cols = slice(q * LANES, (q + 1) * LANES)
        wend_ref[gi, fwd, cols] = wend_f[q].astype(BF16)
        wend_ref[gi, bwd, cols] = wend_b[q].astype(BF16)

    meta_cols = N_META * SSM_GROUP // LANES
    x0 = sum(jnp.sum(wend_f[n_cols - meta_cols + m] * um_ref[gi, :, m * LANES:(m + 1) * LANES],
                     axis=1, keepdims=True) for m in range(meta_cols))

    lane = lax.broadcasted_iota(jnp.int32, (DIR_ROWS, LANES), 1)
    lag0_b = jnp.where(lane >= LANES - SSM_GROUP, bb_same[bwd], 0.0)
    zero = jnp.zeros((DIR_ROWS, LANES), F32)
    bcat = jnp.concatenate(
        [jnp.concatenate(wend_f + [zero] * n_cols, axis=1),
         jnp.concatenate([zero] * (n_cols - 1) + [lag0_b] + lag_b, axis=1)], axis=0)
    lane_r = lax.broadcasted_iota(jnp.int32, (SSM_GROUP, rows), 1)
    c2 = jnp.where((lane_r // SSM_STATE) % 2 == 0, cr_ref[gi], -cr_ref[gi])
    kk = jnp.dot(c2, bcat, precision=lax.Precision.HIGHEST,
                 preferred_element_type=F32)
    for tt in range(t):
        off = (t - 1 - tt) * SSM_GROUP
        toep_ref[gi, tt * SSM_GROUP:(tt + 1) * SSM_GROUP, :] = (
            kk[:, off:off + CHUNK_LANES].astype(BF16))

    wout = jnp.concatenate([jnp.concatenate(wout_f, axis=1),
                            jnp.concatenate(wout_b, axis=1)], axis=0)
    wout_ref[gi] = wout.T.astype(BF16)
    return pows[t], x0


SSM_GROUPS_PER_STEP = 2


def _chunk_scan(er, ei, ar, ai, x0, chunk, n_chunks, forward):
    width = er.shape[1]
    xr, xi = er, ei
    if x0 is not None:
        fr, fi = _cmul(ar, ai, x0[0], x0[1])
        xr = xr + jnp.where(chunk == 0, _lanes(fr, width), 0.0)
        xi = xi + jnp.where(chunk == 0, _lanes(fi, width), 0.0)

    def shifted(v, sh):
        if forward:
            return jnp.where(chunk >= sh, pltpu.roll(v, sh, axis=1), 0.0)
        return jnp.where(chunk < n_chunks - sh, pltpu.roll(v, width - sh, axis=1), 0.0)

    sh = 1
    while sh < n_chunks:
        sr, si = shifted(xr, sh), shifted(xi, sh)
        wr, wi = _lanes(ar, width), _lanes(ai, width)
        xr, xi = xr + wr * sr - wi * si, xi + wr * si + wi * sr
        ar, ai = _cmul(ar, ai, ar, ai)
        sh *= 2
    inr, ini = shifted(xr, 1), shifted(xi, 1)
    if x0 is not None:
        inr = jnp.where(chunk == 0, x0[0], inr)
        ini = jnp.where(chunk == 0, x0[1], ini)
    return inr, ini


def _ssm_kernel(u_ref, um_ref, csc_ref, cc_ref, bc_ref, cr_ref, y_ref, toep_ref, wend_ref,
                wout_ref, yi_ref, *, n_chunks):
    for gi in range(SSM_GROUPS_PER_STEP):
        (lam_r, lam_i), x0 = _build_chunk_ops(gi, csc_ref, cc_ref, bc_ref, cr_ref, um_ref,
                                              toep_ref, wend_ref, wout_ref)
        u = u_ref[gi]
        width = u.shape[1]
        e = jnp.dot(wend_ref[gi], u, preferred_element_type=F32)
        yi_ref[gi] = jnp.dot(toep_ref[gi], u, preferred_element_type=F32)
        chunk =lax.broadcasted_iota(jnp.int32, (1, width), 1) % n_chunks
        p = SSM_STATE
        ar, ai = lam_r, lam_i
        f_in = _chunk_scan(e[0:p], e[p:2 * p], ar[0:p], ai[0:p], (x0[0:p], x0[p:2 * p]),
                           chunk, n_chunks, True)
        b_in = _chunk_scan(e[2 * p:3 * p], e[3 * p:], ar[2 * p:3 * p], ai[2 * p:3 * p],
                           None, chunk, n_chunks, False)
        xin = jnp.concatenate([f_in[0], f_in[1], b_in[0], b_in[1]], axis=0).astype(BF16)
        y = yi_ref[gi] + jnp.dot(wout_ref[gi], xin, preferred_element_type=F32)
        y_ref[gi] = y.astype(BF16)


def _ssm(u_col, u_meta, csc, c_col, b_col, c_row, n_chunks):
    g, _, width = u_col.shape
    per = SSM_GROUPS_PER_STEP
    return pl.pallas_call(
        functools.partial(_ssm_kernel, n_chunks=n_chunks),
        grid=(g // per,),
        in_specs=[
            pl.BlockSpec((per, CHUNK_LANES, width), lambda i: (i, 0, 0)),
            pl.BlockSpec((per, 1, N_META * SSM_GROUP), lambda i: (i, 0, 0)),
            pl.BlockSpec((per, STATE_ROWS, 4), lambda i: (i, 0, 0)),
            pl.BlockSpec((per, STATE_ROWS, SSM_GROUP), lambda i: (i, 0, 0)),
            pl.BlockSpec((per, STATE_ROWS, SSM_GROUP), lambda i: (i, 0, 0)),
            pl.BlockSpec((per, SSM_GROUP, STATE_ROWS), lambda i: (i, 0, 0)),
        ],
        out_specs=pl.BlockSpec((per, CHUNK_LANES, width), lambda i: (i, 0, 0)),
        out_shape=jax.ShapeDtypeStruct((g, CHUNK_LANES, width), BF16),
        scratch_shapes=[
            pltpu.VMEM((per, CHUNK_LANES, CHUNK_LANES), BF16),
            pltpu.VMEM((per, STATE_ROWS, CHUNK_LANES), BF16),
            pltpu.VMEM((per, CHUNK_LANES, STATE_ROWS), BF16),
            pltpu.VMEM((per, CHUNK_LANES, width), F32),
        ],
        compiler_params=_params(),
        name="s5_mixer",
    )(u_col, u_meta, csc, c_col, b_col, c_row)


FF_CHUNK = 1024
MIX_PER = 4


MIX_SLOTS = 3


def _mix_ffn_kernel(x_hbm, co_ref, y_ref, ut_ref, d_ref, gwt_ref, gb_ref, wo_ref,
                    gf_ref, w1_ref, w2_ref, gl_ref, o_hbm, buf, sem_in, sem_out,
                    mixed_ref, h_ref, z_ref, a_ref):
    n_chunks = x_hbm.shape[1]
    rows = SUBLANES * n_chunks
    step = pl.program_id(0)
    n_tiles = pl.num_programs(0) - 1
    tile = step - 1

    def mix_stages(dst):
        per = MIX_PER
        n_half = SUBLANES // per
        state = {}

        def col(ref, k):
            return jnp.concatenate(
                [ref[:, r * SSM_GROUP:(r + 1) * SSM_GROUP, :].reshape(D_SSM, n_chunks)
                 for r in range(k * per, (k + 1) * per)], axis=1).astype(F32)

        def gelu_stage(k):
            def run():
                state["ge", k] = jax.nn.gelu(col(y_ref, k) + d_ref[...] * col(ut_ref, k))
            return run

        def glu_stage(k):
            def run():
                ge = state["ge", k]
                gate = (jnp.dot(gwt_ref[...], ge.astype(BF16), preferred_element_type=F32)
                        + gb_ref[...])
                state["s5t", k] = ge * jax.nn.sigmoid(gate)
            return run

        def store_stage(k):
            def run():
                rs = slice(k * per * n_chunks, (k + 1) * per * n_chunks)
                mixed_ref[dst, rs, :D_CONV] = co_ref[k * per:(k + 1) * per].reshape(
                    per * n_chunks, D_CONV)
                mixed_ref[dst, rs, D_CONV:] = state["s5t", k].T.astype(BF16)
            return run

        return [f(k) for f in (gelu_stage, glu_stage, store_stage) for k in range(n_half)]

    @pl.when(step == 0)
    def _():
        for cp in _step_copies(x_hbm, buf, sem_in, 0, 0, False):
            cp.start()
        for stage in mix_stages(0):
            stage()

    @pl.when(step >= 1)
    def _():
        slot = tile % MIX_SLOTS
        nxt = (tile + 1) % MIX_SLOTS

        @pl.when(tile >= 2)
        def _():
            for cp in _step_copies(o_hbm, buf, sem_out, tile - 2, nxt, True):
                cp.wait()

        @pl.when(tile + 1 < n_tiles)
        def _():
            for cp in _step_copies(x_hbm, buf, sem_in, tile + 1, nxt, False):
                cp.start()

        for cp in _step_copies(x_hbm, buf, sem_in, tile, slot, False):
            cp.wait()

        stages = mix_stages(step % 2)
        half = D_FF // 2
        n_slots = D_FF // FF_CHUNK + 2
        n_stages = len(stages)
        done = [0, 0]

        def fill_slot():
            done[0] += 1
            while done[1] * n_slots < done[0] * n_stages:
                stages.pop(0)()
                done[1] += 1

        h = buf[slot].reshape(rows, D_MODEL) + jnp.dot(
            mixed_ref[tile % 2], wo_ref[...], preferred_element_type=F32)
        z_ref[...] = _rms(h, gf_ref[...]).astype(BF16)
        h_ref[...] = h
        for k in range(2):
            for j in range(half // FF_CHUNK):
                cols = slice(k * half + j * FF_CHUNK, k * half + (j + 1) * FF_CHUNK)
                a = jnp.dot(z_ref[...], w1_ref[:, cols], preferred_element_type=F32)
                a_ref[:, j * FF_CHUNK:(j + 1) * FF_CHUNK] = (
                    jnp.square(jnp.maximum(a, 0.0)).astype(BF16))
                fill_slot()
            h_ref[...] += jnp.dot(a_ref[...], w2_ref[k * half:(k + 1) * half, :],
                                  preferred_element_type=F32)
            fill_slot()
        assert not stages
        buf[slot] = _rms(h_ref[...], gl_ref[...]).reshape(SUBLANES, n_chunks, D_MODEL)

        for cp in _step_copies(o_hbm, buf, sem_out, tile, slot, True):
            cp.start()

        @pl.when(tile == n_tiles - 1)
        def _():
            for cp in (_step_copies(o_hbm, buf, sem_out, tile - 1, (tile - 1) % MIX_SLOTS, True)
                       + _step_copies(o_hbm, buf, sem_out, tile, slot, True)):
                cp.wait()


def _mix_ffn(x4, conv_out, y_col, u_col, d_col, glu_wt, glu_b_col, w_out, g_ffn, w1, w2,
             g_final):
    bsz, n_chunks = x4.shape[:2]

    def const(shape):
        return pl.BlockSpec(shape, lambda t: (0, 0), pipeline_mode=pl.Buffered(1))

    n_tiles = CHUNK // SUBLANES * bsz
    hbm = pl.BlockSpec(memory_space=pl.ANY)

    def slab(s):
        return jnp.minimum(s, n_tiles - 1) // bsz

    def batch(s):
        return jnp.minimum(s, n_tiles - 1) % bsz

    col_blk = pl.BlockSpec((N_GROUPS, SUBLANES * SSM_GROUP, n_chunks),
                           lambda s: (0, slab(s), batch(s)))
    step_sems = pltpu.SemaphoreType.DMA((MIX_SLOTS, SUBLANES))
    return pl.pallas_call(
        _mix_ffn_kernel,
        grid=(n_tiles + 1,),
        in_specs=[
            hbm,
            pl.BlockSpec((SUBLANES, n_chunks, D_CONV), lambda s: (slab(s), batch(s), 0)),
            col_blk,
            col_blk,
            const((D_SSM, 1)),
            const((D_SSM, D_SSM)),
            const((D_SSM, 1)),
            const((D_MODEL, D_MODEL)),
            const((1, D_MODEL)),
            const((D_MODEL, D_FF)),
            const((D_FF, D_MODEL)),
            const((1, D_MODEL)),
        ],
        out_specs=hbm,
        out_shape=jax.ShapeDtypeStruct(x4.shape, F32),
        scratch_shapes=[
            pltpu.VMEM((MIX_SLOTS, SUBLANES, n_chunks, D_MODEL), F32),
            step_sems, step_sems,
            pltpu.VMEM((2, SUBLANES * n_chunks, D_MODEL), BF16),
            pltpu.VMEM((SUBLANES * n_chunks, D_MODEL), F32),
            pltpu.VMEM((SUBLANES * n_chunks, D_MODEL), BF16),
            pltpu.VMEM((SUBLANES * n_chunks, D_FF // 2), BF16),
        ],
        compiler_params=_step_tile_params(),
        name="mix_ffn",
    )(x4, conv_out, y_col, u_col, d_col, glu_wt, glu_b_col, w_out, g_ffn, w1, w2, g_final)


def _s5_operator_inputs(lbr, lbi, zr, zi, b_re, b_im, c_re, c_im):
    g, p, h = N_GROUPS, SSM_STATE, SSM_GROUP

    sc = jnp.transpose(jnp.stack([lbr, lbi, zr, zi], axis=-1), (1, 0, 2, 3))
    csc = jnp.broadcast_to(sc[:, :, None], (g, 2, 2, p, 4)).reshape(g, 4 * p, 4)
    c = jnp.stack([c_re, c_im], axis=0)
    b = jnp.stack([b_re, b_im], axis=0)
    c_col = jnp.transpose(c, (2, 1, 0, 4, 3)).reshape(g, 4 * p, h)
    b_col = jnp.transpose(b, (2, 1, 0, 3, 4)).reshape(g, 4 * p, h)
    c_row = jnp.transpose(c, (2, 3, 1, 0, 4)).reshape(g, h, 4 * p)
    return csc, c_col, b_col, c_row


def kernel(x, meta_tokens, norm_mix_g, w_in, conv_w, conv_b, conv_ln_g, conv_ln_b,
           ssm_lam_re, ssm_lam_im, ssm_log_dt, ssm_b_re, ssm_b_im, ssm_c_re, ssm_c_im,
           ssm_d, ssm_glu_w, ssm_glu_b, w_out, norm_ffn_g, w_ff1, w_ff2, norm_final_g):
    assert w_in.shape[0] == 1, "single-layer block"
    bsz, seq, _ = x.shape
    assert seq % CHUNK == 0 and CHUNK >= N_META
    n_chunks = seq // CHUNK
    g, h = N_GROUPS, SSM_GROUP

    x4 = x.reshape(bsz, n_chunks, CHUNK, D_MODEL)
    g_mix = norm_mix_g[0][None, :]
    u_conv, u_col, u_conv_m, u_ssm_m = _in_proj_step(x4, meta_tokens, g_mix, w_in[0])

    conv_out = _conv_module(u_conv, u_conv_m, conv_w[0], conv_b[0][None, :],
                            conv_ln_g[0][None, :], conv_ln_b[0][None, :], bsz)

    ldt = jnp.broadcast_to(ssm_log_dt[0][..., None], ssm_lam_re[0].shape)
    flat = lambda a: a.reshape(2 * g, SSM_STATE)
    lbr, lbi, zr, zi = [a.reshape(2, g, SSM_STATE) for a in
                        _zoh(flat(ssm_lam_re[0]), flat(ssm_lam_im[0]), flat(ldt))]
    csc, c_col, b_col, c_row = _s5_operator_inputs(
        lbr, lbi, zr, zi, ssm_b_re[0], ssm_b_im[0], ssm_c_re[0], ssm_c_im[0])

    u_meta = jnp.transpose(u_ssm_m.reshape(N_META, g, h), (1, 0, 2)).reshape(g, 1, N_META * h)
    y_col = _ssm(u_col, u_meta, csc, c_col, b_col, c_row, n_chunks)

    out = _mix_ffn(x4, conv_out, y_col, u_col, ssm_d[0][:, None],
                   ssm_glu_w[0].T.astype(BF16), ssm_glu_b[0][:, None],
                   w_out[0].astype(BF16), norm_ffn_g[0][None, :], w_ff1[0].astype(BF16),
                   w_ff2[0].astype(BF16), norm_final_g[None, :])
    return out.reshape(bsz, seq, D_MODEL)
```

```python
import functools

import jax
import jax.numpy as jnp
from jax import lax
from jax.experimental import pallas as pl
from jax.experimental.pallas import tpu as pltpu

F32 = jnp.float32
BF16 = jnp.bfloat16

D_MODEL = 1024
N_META = 16
D_CONV = 512
D_SSM = 512
CONV_WIDTH = 31
CONV_PAD = CONV_WIDTH // 2
SSM_GROUP = 16
N_GROUPS = D_SSM // SSM_GROUP
SSM_STATE = 64
D_FF = 4096
NORM_EPS = 1e-5
LANES = 128
SUBLANES = 8

CHUNK = 32
CHUNK_LANES = CHUNK * SSM_GROUP
STATE_ROWS = 4 * SSM_STATE
DIR_ROWS = 2 * SSM_STATE
STEPS_PER_VREG = LANES // SSM_GROUP

VMEM_LIMIT_BYTES = 60 * 1024 * 1024


def _rms(x, g):
    return x * lax.rsqrt(jnp.mean(x * x, axis=-1, keepdims=True) + NORM_EPS) * g


def _params(n_axes=1):
    return pltpu.CompilerParams(dimension_semantics=("parallel",) * n_axes,
                                vmem_limit_bytes=VMEM_LIMIT_BYTES)


def _step_copies(hbm_ref, buf_ref, sem_ref, tile, slot, to_hbm):
    bsz = hbm_ref.shape[0]
    slab, b = tile // bsz, tile % bsz
    copies = []
    for r in range(SUBLANES):
        hbm = hbm_ref.at[b, :, slab * SUBLANES + r, :]
        vmem = buf_ref.at[slot, r]
        src, dst = (vmem, hbm) if to_hbm else (hbm, vmem)
        copies.append(pltpu.make_async_copy(src, dst, sem_ref.at[slot, r]))
    return copies


def _fetch_steps(x_hbm, xbuf, sem):
    tile = pl.program_id(0)
    slot = tile % 2

    @pl.when(tile == 0)
    def _():
        for cp in _step_copies(x_hbm, xbuf, sem, tile, slot, False):
            cp.start()

    @pl.when(tile + 1 < pl.num_programs(0))
    def _():
        for cp in _step_copies(x_hbm, xbuf, sem, tile + 1, 1 - slot, False):
            cp.start()

    for cp in _step_copies(x_hbm, xbuf, sem, tile, slot, False):
        cp.wait()
    return slot


def _step_tile_params():
    return pltpu.CompilerParams(dimension_semantics=("arbitrary",),
                                vmem_limit_bytes=VMEM_LIMIT_BYTES)


def _in_proj_step_kernel(x_hbm, meta_ref, g_ref, w32_ref, uc_ref, ut_ref, ucm_ref, usm_ref,
                         xbuf, sem, w_ref):
    n_chunks = x_hbm.shape[1]
    slot = _fetch_steps(x_hbm, xbuf, sem)

    @pl.when(pl.program_id(0) == 0)
    def _():
        w_ref[...] = w32_ref[...].astype(BF16)
        zm = _rms(meta_ref[...], g_ref[...]).astype(BF16)
        pm = jnp.dot(zm, w_ref[...], preferred_element_type=F32)
        ucm_ref[...] = pm[:, :D_CONV] * jax.nn.sigmoid(pm[:, D_CONV:2 * D_CONV])
        usm_ref[...] = pm[:, 2 * D_CONV:]

    x = xbuf[slot].reshape(SUBLANES * n_chunks, D_MODEL)
    z = _rms(x, g_ref[...]).astype(BF16)
    p = jnp.dot(z, w_ref[...], preferred_element_type=F32)
    uc = p[:, :D_CONV] * jax.nn.sigmoid(p[:, D_CONV:2 * D_CONV])
    uc_ref[...] = uc.reshape(SUBLANES, n_chunks, D_CONV)
    ut = p[:, 2 * D_CONV:].T.astype(BF16)
    for r in range(SUBLANES):
        ut_ref[:, r * SSM_GROUP:(r + 1) * SSM_GROUP, :] = (
            ut[:, r * n_chunks:(r + 1) * n_chunks].reshape(N_GROUPS, SSM_GROUP, n_chunks))


def _in_proj_step(x4, meta, g, w_in):
    bsz, n_chunks = x4.shape[:2]
    rows = bsz * n_chunks
    return pl.pallas_call(
        _in_proj_step_kernel,
        grid=(CHUNK // SUBLANES * bsz,),
        in_specs=[
            pl.BlockSpec(memory_space=pl.ANY),
            pl.BlockSpec((N_META, D_MODEL), lambda t: (0, 0)),
            pl.BlockSpec((1, D_MODEL), lambda t: (0, 0)),
            pl.BlockSpec((D_MODEL, 2 * D_CONV + D_SSM), lambda t: (0, 0),
                         pipeline_mode=pl.Buffered(1)),
        ],
        out_specs=[
            pl.BlockSpec((SUBLANES, n_chunks, D_CONV), lambda t: (t // bsz, t % bsz, 0)),
            pl.BlockSpec((N_GROUPS, SUBLANES * SSM_GROUP, n_chunks),
                         lambda t: (0, t // bsz, t % bsz)),
            pl.BlockSpec((N_META, D_CONV), lambda t: (0, 0)),
            pl.BlockSpec((N_META, D_SSM), lambda t: (0, 0)),
        ],
        out_shape=[
            jax.ShapeDtypeStruct((CHUNK, rows, D_CONV), F32),
            jax.ShapeDtypeStruct((N_GROUPS, CHUNK_LANES, rows), BF16),
            jax.ShapeDtypeStruct((N_META, D_CONV), F32),
            jax.ShapeDtypeStruct((N_META, D_SSM), F32),
        ],
        scratch_shapes=[pltpu.VMEM((2, SUBLANES, n_chunks, D_MODEL), F32),
                        pltpu.SemaphoreType.DMA((2, SUBLANES)),
                        pltpu.VMEM((D_MODEL, 2 * D_CONV + D_SSM), BF16)],
        compiler_params=_step_tile_params(),
        name="in_proj",
    )(x4, meta, g, w_in)


CONV_GROUP = 4
CONV_ROWS = 64


def _conv_kernel(u_ref, um_ref, w_ref, cb_ref, lg_ref, lb_ref, o_ref, cat_ref, acc_ref):
    n_steps, n_chunks, _ = u_ref.shape
    cat_ref[CONV_PAD:CONV_PAD + n_steps] = u_ref[...]
    chunk = lax.broadcasted_iota(jnp.int32, (n_chunks, 1), 0)
    for i in range(CONV_PAD):
        s_prev = n_steps - CONV_PAD + i
        prev = pltpu.roll(u_ref[s_prev], 1, axis=0)
        meta_row = um_ref[N_META - CONV_PAD + i:N_META - CONV_PAD + i + 1, :]
        cat_ref[i] = jnp.where(chunk == 0, meta_row, prev)
        nxt = pltpu.roll(u_ref[i], n_chunks - 1, axis=0)
        cat_ref[n_steps + CONV_PAD + i] = jnp.where(chunk == n_chunks - 1, 0.0, nxt)

    def body(sg, carry):
        s0 = sg * CONV_GROUP
        for c in range(D_CONV // LANES):
            cols = slice(c * LANES, (c + 1) * LANES)
            n_sub = CONV_ROWS // SUBLANES

            def rows_body(rt, carry2):
                r0 = [pl.multiple_of(rt * CONV_ROWS + sub * SUBLANES, SUBLANES)
                      for sub in range(n_sub)]
                accs = [[jnp.zeros((SUBLANES, LANES), F32)] * CONV_GROUP for _ in range(n_sub)]
                for i in range(CONV_WIDTH + CONV_GROUP - 1):
                    ds = [cat_ref[s0 + i, pl.ds(r0[sub], SUBLANES), cols]
                          for sub in range(n_sub)]
                    for j in range(CONV_GROUP):
                        if 0 <= i - j < CONV_WIDTH:
                            tap = jnp.broadcast_to(w_ref[i - j:i - j + 1, cols],
                                                   (SUBLANES, LANES))
                            for sub in range(n_sub):
                                accs[sub][j] = accs[sub][j] + ds[sub] * tap
                for sub in range(n_sub):
                    for j in range(CONV_GROUP):
                        acc_ref[j, pl.ds(r0[sub], SUBLANES), cols] = accs[sub][j]
                return carry2

            lax.fori_loop(0, n_chunks // CONV_ROWS, rows_body, 0)
        for j in range(CONV_GROUP):
            y = acc_ref[j] + cb_ref[...]
            yc = y - jnp.mean(y, axis=-1, keepdims=True)
            yn = yc * lax.rsqrt(jnp.mean(yc * yc, axis=-1, keepdims=True) + NORM_EPS)
            yn = yn * lg_ref[...] + lb_ref[...]
            o_ref[s0 + j] = (yn * jax.nn.sigmoid(yn)).astype(BF16)
        return carry

    lax.fori_loop(0, n_steps // CONV_GROUP, body, 0)


def _conv_module(u_conv, u_conv_meta, conv_w, conv_b, ln_g, ln_b, bsz):
    n_steps, rows, _ = u_conv.shape
    n_chunks = rows // bsz
    vec = pl.BlockSpec((1, D_CONV), lambda b: (0, 0))
    blk = pl.BlockSpec((n_steps, n_chunks, D_CONV), lambda b: (0, b, 0))
    return pl.pallas_call(
        _conv_kernel,
        grid=(bsz,),
        in_specs=[
            blk,
            pl.BlockSpec((N_META, D_CONV), lambda b: (0, 0)),
            pl.BlockSpec((CONV_WIDTH, D_CONV), lambda b: (0, 0)),
            vec, vec, vec,
        ],
        out_specs=blk,
        out_shape=jax.ShapeDtypeStruct((n_steps, rows, D_CONV), BF16),
        scratch_shapes=[pltpu.VMEM((n_steps + 2 * CONV_PAD, n_chunks, D_CONV), F32),
                        pltpu.VMEM((CONV_GROUP, n_chunks, D_CONV), F32)],
        compiler_params=_params(),
        name="conv_module",
    )(u_conv, u_conv_meta, conv_w, conv_b, ln_g, ln_b)


def _zoh_kernel(lre_ref, lim_ref, ldt_ref, lbr_ref, lbi_ref, zr_ref, zi_ref):
    lre = lre_ref[...]
    lim = lim_ref[...]
    dt = jnp.exp(ldt_ref[...])
    ea = jnp.exp(lre * dt)
    lbr = ea * jnp.cos(lim * dt)
    lbi = ea * jnp.sin(lim * dt)
    nr = lbr - 1.0
    den = lre * lre + lim * lim
    lbr_ref[...] = lbr
    lbi_ref[...] = lbi
    zr_ref[...] = (nr * lre + lbi * lim) / den
    zi_ref[...] = (lbi * lre - nr * lim) / den


def _zoh(lam_re, lam_im, log_dt):
    shape = jax.ShapeDtypeStruct(lam_re.shape, F32)
    return pl.pallas_call(_zoh_kernel, out_shape=[shape] * 4, name="s5_zoh")(
        lam_re, lam_im, log_dt)


def _lanes(x, n):
    return jnp.concatenate([x] * (n // LANES), axis=1)


def _cmul(ar, ai, br, bi):
    return ar * br - ai * bi, ar * bi + ai * br


def _cpow(br, bi, expo, nbits):
    rr = jnp.ones(expo.shape, F32)
    ri = jnp.zeros(expo.shape, F32)
    for k in range(nbits):
        bit = ((expo >> k) & 1) == 1
        nr, ni = _cmul(rr, ri, br, bi)
        rr = jnp.where(bit, nr, rr)
        ri = jnp.where(bit, ni, ri)
        br, bi = _cmul(br, bi, br, bi)
    return rr, ri


def _build_chunk_ops(gi, csc_ref, cc_ref, bc_ref, cr_ref, um_ref, toep_ref, wend_ref,
                     wout_ref):
    t = CHUNK
    rows = STATE_ROWS
    n_cols = CHUNK_LANES // LANES
    lam = (jnp.broadcast_to(csc_ref[gi, :, 0:1], (rows, LANES)),
           jnp.broadcast_to(csc_ref[gi, :, 1:2], (rows, LANES)))
    zr = jnp.broadcast_to(csc_ref[gi, :, 2:3], (rows, LANES))
    zi = jnp.broadcast_to(csc_ref[gi, :, 3:4], (rows, LANES))
    sel = (lax.broadcasted_iota(jnp.int32, (SSM_GROUP, LANES), 1) % SSM_GROUP
           == lax.broadcasted_iota(jnp.int32, (SSM_GROUP, LANES), 0)).astype(F32)

    def tile(ref):
        return jnp.dot(ref[gi], sel, precision=lax.Precision.HIGHEST,
                       preferred_element_type=F32)

    def swap_re_im(a):
        p = SSM_STATE
        return jnp.concatenate([a[p:2 * p], a[0:p], a[3 * p:], a[2 * p:3 * p]], axis=0)

    c_same, b_same = tile(cc_ref), tile(bc_ref)
    c_swap, b_swap = swap_re_im(c_same), swap_re_im(b_same)
    row = lax.broadcasted_iota(jnp.int32, (rows, LANES), 0)
    is_re = (row // SSM_STATE) % 2 == 0
    sgn = jnp.where(is_re, -1.0, 1.0)
    ca = jnp.where(is_re, c_same, -c_same)
    cb = -c_swap
    bb_same = zr * b_same + sgn * zi * b_swap
    sbb_swap = sgn * (zr * b_swap - sgn * zi * b_same)

    pows = {1: lam}
    k = 1
    while k < t:
        pows[2 * k] = _cmul(*pows[k], *pows[k])
        k *= 2
    step_bits = STEPS_PER_VREG.bit_length() - 1
    fwd = slice(0, DIR_ROWS)
    bwd = slice(DIR_ROWS, rows)
    every = slice(0, rows)

    def rows_of(v, rs):
        return v[0][rs], v[1][rs]

    def column(base, e, rs):
        out = rows_of(base, rs)
        for bit, val in pows.items():
            if e & bit:
                out = _cmul(*out, *rows_of(val, rs))
        return out

    def times_b(p, rs):
        return p[0] * bb_same[rs] + p[1] * sbb_swap[rs]

    def times_c(p, rs):
        return ca[rs] * p[0] + cb[rs] * p[1]

    i8 = lax.broadcasted_iota(jnp.int32, (rows, LANES), 1) // SSM_GROUP
    asc0 = _cpow(*lam, i8, step_bits)
    desc0 = _cpow(*lam, STEPS_PER_VREG - 1 - i8, step_bits)
    asc1_0 = _cmul(*asc0, *lam)
    desc1_0 = _cmul(*desc0, *lam)

    wend_f, wend_b, lag_b, wout_f, wout_b = [], [], [], [], []
    for q in range(n_cols):
        e_asc = STEPS_PER_VREG * q
        e_desc = STEPS_PER_VREG * (n_cols - 1 - q)
        asc1_q = column(asc1_0, e_asc, every)
        wend_f.append(times_b(column(desc0, e_desc, fwd), fwd))
        wend_b.append(times_b(column(asc0, e_asc, bwd), bwd))
        lag_b.append(times_b(rows_of(asc1_q, bwd), bwd))
        wout_f.append(times_c(rows_of(asc1_q, fwd), fwd))
        wout_b.append(times_c(column(desc1_0, e_desc, bwd), bwd))
        cols = slice(q * LANES, (q + 1) * LANES)
        wend_ref[gi, fwd, cols] = wend_f[q].astype(BF16)
        wend_ref[gi, bwd, cols] = wend_b[q].astype(BF16)

    meta_cols = N_META * SSM_GROUP // LANES
    x0 = sum(jnp.sum(wend_f[n_cols - meta_cols + m] * um_ref[gi, :, m * LANES:(m + 1) * LANES],
                     axis=1, keepdims=True) for m in range(meta_cols))

    lane = lax.broadcasted_iota(jnp.int32, (DIR_ROWS, LANES), 1)
    lag0_b = jnp.where(lane >= LANES - SSM_GROUP, bb_same[bwd], 0.0)
    zero = jnp.zeros((DIR_ROWS, LANES), F32)
    bcat = jnp.concatenate(
        [jnp.concatenate(wend_f + [zero] * n_cols, axis=1),
         jnp.concatenate([zero] * (n_cols - 1) + [lag0_b] + lag_b, axis=1)], axis=0)
    lane_r = lax.broadcasted_iota(jnp.int32, (SSM_GROUP, rows), 1)
    c2 = jnp.where((lane_r // SSM_STATE) % 2 == 0, cr_ref[gi], -cr_ref[gi])
    kk = jnp.dot(c2, bcat, precision=lax.Precision.HIGHEST,
                 preferred_element_type=F32)
    for tt in range(t):
        off = (t - 1 - tt) * SSM_GROUP
        toep_ref[gi, tt * SSM_GROUP:(tt + 1) * SSM_GROUP, :] = (
            kk[:, off:off + CHUNK_LANES].astype(BF16))

    wout = jnp.concatenate([jnp.concatenate(wout_f, axis=1),
                            jnp.concatenate(wout_b, axis=1)], axis=0)
    wout_ref[gi] = wout.T.astype(BF16)
    return pows[t], x0


SSM_GROUPS_PER_STEP = 2


def _chunk_scan(er, ei, ar, ai, x0, chunk, n_chunks, forward):
    width = er.shape[1]
    xr, xi = er, ei
    if x0 is not None:
        fr, fi = _cmul(ar, ai, x0[0], x0[1])
        xr = xr + jnp.where(chunk == 0, _lanes(fr, width), 0.0)
        xi = xi + jnp.where(chunk == 0, _lanes(fi, width), 0.0)

    def shifted(v, sh):
        if forward:
            return jnp.where(chunk >= sh, pltpu.roll(v, sh, axis=1), 0.0)
        return jnp.where(chunk < n_chunks - sh, pltpu.roll(v, width - sh, axis=1), 0.0)

    sh = 1
    while sh < n_chunks:
        sr, si = shifted(xr, sh), shifted(xi, sh)
        wr, wi = _lanes(ar, width), _lanes(ai, width)
        xr, xi = xr + wr * sr - wi * si, xi + wr * si + wi * sr
        ar, ai = _cmul(ar, ai, ar, ai)
        sh *= 2
    inr, ini = shifted(xr, 1), shifted(xi, 1)
    if x0 is not None:
        inr = jnp.where(chunk == 0, x0[0], inr)
        ini = jnp.where(chunk == 0, x0[1], ini)
    return inr, ini


def _ssm_kernel(u_ref, um_ref, csc_ref, cc_ref, bc_ref, cr_ref, y_ref, toep_ref, wend_ref,
                wout_ref, yi_ref, *, n_chunks):
    for gi in range(SSM_GROUPS_PER_STEP):
        (lam_r, lam_i), x0 = _build_chunk_ops(gi, csc_ref, cc_ref, bc_ref, cr_ref, um_ref,
                                              toep_ref, wend_ref, wout_ref)
        u = u_ref[gi]
        width = u.shape[1]
        e = jnp.dot(wend_ref[gi], u, preferred_element_type=F32)
        yi_ref[gi] = jnp.dot(toep_ref[gi], u, preferred_element_type=F32)
        chunk =lax.broadcasted_iota(jnp.int32, (1, width), 1) % n_chunks
        p = SSM_STATE
        ar, ai = lam_r, lam_i
        f_in = _chunk_scan(e[0:p], e[p:2 * p], ar[0:p], ai[0:p], (x0[0:p], x0[p:2 * p]),
                           chunk, n_chunks, True)
        b_in = _chunk_scan(e[2 * p:3 * p], e[3 * p:], ar[2 * p:3 * p], ai[2 * p:3 * p],
                           None, chunk, n_chunks, False)
        xin = jnp.concatenate([f_in[0], f_in[1], b_in[0], b_in[1]], axis=0).astype(BF16)
        y = yi_ref[gi] + jnp.dot(wout_ref[gi], xin, preferred_element_type=F32)
        y_ref[gi] = y.astype(BF16)


def _ssm(u_col, u_meta, csc, c_col, b_col, c_row, n_chunks):
    g, _, width = u_col.shape
    per = SSM_GROUPS_PER_STEP
    return pl.pallas_call(
        functools.partial(_ssm_kernel, n_chunks=n_chunks),
        grid=(g // per,),
        in_specs=[
            pl.BlockSpec((per, CHUNK_LANES, width), lambda i: (i, 0, 0)),
            pl.BlockSpec((per, 1, N_META * SSM_GROUP), lambda i: (i, 0, 0)),
            pl.BlockSpec((per, STATE_ROWS, 4), lambda i: (i, 0, 0)),
            pl.BlockSpec((per, STATE_ROWS, SSM_GROUP), lambda i: (i, 0, 0)),
            pl.BlockSpec((per, STATE_ROWS, SSM_GROUP), lambda i: (i, 0, 0)),
            pl.BlockSpec((per, SSM_GROUP, STATE_ROWS), lambda i: (i, 0, 0)),
        ],
        out_specs=pl.BlockSpec((per, CHUNK_LANES, width), lambda i: (i, 0, 0)),
        out_shape=jax.ShapeDtypeStruct((g, CHUNK_LANES, width), BF16),
        scratch_shapes=[
            pltpu.VMEM((per, CHUNK_LANES, CHUNK_LANES), BF16),
            pltpu.VMEM((per, STATE_ROWS, CHUNK_LANES), BF16),
            pltpu.VMEM((per, CHUNK_LANES, STATE_ROWS), BF16),
            pltpu.VMEM((per, CHUNK_LANES, width), F32),
        ],
        compiler_params=_params(),
        name="s5_mixer",
    )(u_col, u_meta, csc, c_col, b_col, c_row)


FF_CHUNK = 1024
MIX_PER = 4


MIX_SLOTS = 3
W_STAGE_BYTES = 512 * 1024
F32_BYTES = 4


def _load_bf16(w_hbm, w_ref, stage, sem):
    chunk = stage.shape[1]
    n = w_hbm.shape[0] // chunk

    def copy(i, slot):
        return pltpu.make_async_copy(w_hbm.at[pl.ds(i * chunk, chunk)], stage.at[slot],
                                     sem.at[slot])

    copy(0, 0).start()

    def body(i, carry):
        slot = i % 2

        @pl.when(i + 1 < n)
        def _():
            copy(i + 1, 1 - slot).start()

        copy(i, slot).wait()
        w_ref[pl.ds(pl.multiple_of(i * chunk, chunk), chunk), :] = stage[slot].astype(BF16)
        return carry

    lax.fori_loop(0, n, body, 0)


def _mix_ffn_kernel(x_hbm, co_ref, y_ref, ut_ref, d_ref, gwt_ref, gb_ref, wo_ref,
                    gf_ref, w1_hbm, w2_hbm, gl_ref, o_hbm, buf, sem_in, sem_out,
                    mixed_ref, h_ref, z_ref, a_ref, w1_ref, w2_ref, stage1, stage2, sem_w):
    n_chunks = x_hbm.shape[1]
    rows = SUBLANES * n_chunks
    step = pl.program_id(0)
    n_tiles = pl.num_programs(0) - 1
    tile = step - 1

    def mix_stages(dst):
        per = MIX_PER
        n_half = SUBLANES // per
        state = {}

        def col(ref, k):
            return jnp.concatenate(
                [ref[:, r * SSM_GROUP:(r + 1) * SSM_GROUP, :].reshape(D_SSM, n_chunks)
                 for r in range(k * per, (k + 1) * per)], axis=1).astype(F32)

        def gelu_stage(k):
            def run():
                state["ge", k] = jax.nn.gelu(col(y_ref, k) + d_ref[...] * col(ut_ref, k))
            return run

        def glu_stage(k):
            def run():
                ge = state["ge", k]
                gate = (jnp.dot(gwt_ref[...], ge.astype(BF16), preferred_element_type=F32)
                        + gb_ref[...])
                state["s5t", k] = ge * jax.nn.sigmoid(gate)
            return run

        def store_stage(k):
            def run():
                rs = slice(k * per * n_chunks, (k + 1) * per * n_chunks)
                mixed_ref[dst, rs, :D_CONV] = co_ref[k * per:(k + 1) * per].reshape(
                    per * n_chunks, D_CONV)
                mixed_ref[dst, rs, D_CONV:] = state["s5t", k].T.astype(BF16)
            return run

        return [f(k) for f in (gelu_stage, glu_stage, store_stage) for k in range(n_half)]

    @pl.when(step == 0)
    def _():
        for cp in _step_copies(x_hbm, buf, sem_in, 0, 0, False):
            cp.start()
        _load_bf16(w1_hbm, w1_ref, stage1, sem_w.at[0])
        _load_bf16(w2_hbm, w2_ref, stage2, sem_w.at[1])
        for stage in mix_stages(0):
            stage()

    @pl.when(step >= 1)
    def _():
        slot = tile % MIX_SLOTS
        nxt = (tile + 1) % MIX_SLOTS

        @pl.when(tile >= 2)
        def _():
            for cp in _step_copies(o_hbm, buf, sem_out, tile - 2, nxt, True):
                cp.wait()

        @pl.when(tile + 1 < n_tiles)
        def _():
            for cp in _step_copies(x_hbm, buf, sem_in, tile + 1, nxt, False):
                cp.start()

        for cp in _step_copies(x_hbm, buf, sem_in, tile, slot, False):
            cp.wait()

        stages = mix_stages(step % 2)
        half = D_FF // 2
        n_slots = D_FF // FF_CHUNK + 2
        n_stages = len(stages)
        done = [0, 0]

        def fill_slot():
            done[0] += 1
            while done[1] * n_slots < done[0] * n_stages:
                stages.pop(0)()
                done[1] += 1

        h = buf[slot].reshape(rows, D_MODEL) + jnp.dot(
            mixed_ref[tile % 2], wo_ref[...], preferred_element_type=F32)
        z_ref[...] = _rms(h, gf_ref[...]).astype(BF16)
        h_ref[...] = h
        for k in range(2):
            for j in range(half // FF_CHUNK):
                cols = slice(k * half + j * FF_CHUNK, k * half + (j + 1) * FF_CHUNK)
                a = jnp.dot(z_ref[...], w1_ref[:, cols], preferred_element_type=F32)
                a_ref[:, j * FF_CHUNK:(j + 1) * FF_CHUNK] = (
                    jnp.square(jnp.maximum(a, 0.0)).astype(BF16))
                fill_slot()
            h_ref[...] += jnp.dot(a_ref[...], w2_ref[k * half:(k + 1) * half, :],
                                  preferred_element_type=F32)
            fill_slot()
        assert not stages
        buf[slot] = _rms(h_ref[...], gl_ref[...]).reshape(SUBLANES, n_chunks, D_MODEL)

        for cp in _step_copies(o_hbm, buf, sem_out, tile, slot, True):
            cp.start()

        @pl.when(tile == n_tiles - 1)
        def _():
            for cp in (_step_copies(o_hbm, buf, sem_out, tile - 1, (tile - 1) % MIX_SLOTS, True)
                       + _step_copies(o_hbm, buf, sem_out, tile, slot, True)):
                cp.wait()


def _mix_ffn(x4, conv_out, y_col, u_col, d_col, glu_wt, glu_b_col, w_out, g_ffn, w1, w2,
             g_final):
    bsz, n_chunks = x4.shape[:2]

    def const(shape):
        return pl.BlockSpec(shape, lambda t: (0, 0), pipeline_mode=pl.Buffered(1))

    n_tiles = CHUNK // SUBLANES * bsz
    hbm = pl.BlockSpec(memory_space=pl.ANY)

    def slab(s):
        return jnp.minimum(s, n_tiles - 1) // bsz

    def batch(s):
        return jnp.minimum(s, n_tiles - 1) % bsz

    col_blk = pl.BlockSpec((N_GROUPS, SUBLANES * SSM_GROUP, n_chunks),
                           lambda s: (0, slab(s), batch(s)))
    step_sems = pltpu.SemaphoreType.DMA((MIX_SLOTS, SUBLANES))
    return pl.pallas_call(
        _mix_ffn_kernel,
        grid=(n_tiles + 1,),
        in_specs=[
            hbm,
            pl.BlockSpec((SUBLANES, n_chunks, D_CONV), lambda s: (slab(s), batch(s), 0)),
            col_blk,
            col_blk,
            const((D_SSM, 1)),
            const((D_SSM, D_SSM)),
            const((D_SSM, 1)),
            const((D_MODEL, D_MODEL)),
            const((1, D_MODEL)),
            hbm,
            hbm,
            const((1, D_MODEL)),
        ],
        out_specs=hbm,
        out_shape=jax.ShapeDtypeStruct(x4.shape, F32),
        scratch_shapes=[
            pltpu.VMEM((MIX_SLOTS, SUBLANES, n_chunks, D_MODEL), F32),
            step_sems, step_sems,
            pltpu.VMEM((2, SUBLANES * n_chunks, D_MODEL), BF16),
            pltpu.VMEM((SUBLANES * n_chunks, D_MODEL), F32),
            pltpu.VMEM((SUBLANES * n_chunks, D_MODEL), BF16),
            pltpu.VMEM((SUBLANES * n_chunks, D_FF // 2), BF16),
            pltpu.VMEM((D_MODEL, D_FF), BF16),
            pltpu.VMEM((D_FF, D_MODEL), BF16),
            pltpu.VMEM((2, W_STAGE_BYTES // (F32_BYTES * D_FF), D_FF), F32),
            pltpu.VMEM((2, W_STAGE_BYTES // (F32_BYTES * D_MODEL), D_MODEL), F32),
            pltpu.SemaphoreType.DMA((2, 2)),
        ],
        compiler_params=_step_tile_params(),
        name="mix_ffn",
    )(x4, conv_out, y_col, u_col, d_col, glu_wt, glu_b_col, w_out, g_ffn, w1, w2, g_final)


def _s5_operator_inputs(lbr, lbi, zr, zi, b_re, b_im, c_re, c_im):
    g, p, h = N_GROUPS, SSM_STATE, SSM_GROUP

    sc = jnp.transpose(jnp.stack([lbr, lbi, zr, zi], axis=-1), (1, 0, 2, 3))
    csc = jnp.broadcast_to(sc[:, :, None], (g, 2, 2, p, 4)).reshape(g, 4 * p, 4)
    c = jnp.stack([c_re, c_im], axis=0)
    b = jnp.stack([b_re, b_im], axis=0)
    c_col = jnp.transpose(c, (2, 1, 0, 4, 3)).reshape(g, 4 * p, h)
    b_col = jnp.transpose(b, (2, 1, 0, 3, 4)).reshape(g, 4 * p, h)
    c_row = jnp.transpose(c, (2, 3, 1, 0, 4)).reshape(g, h, 4 * p)
    return csc, c_col, b_col, c_row


def kernel(x, meta_tokens, norm_mix_g, w_in, conv_w, conv_b, conv_ln_g, conv_ln_b,
           ssm_lam_re, ssm_lam_im, ssm_log_dt, ssm_b_re, ssm_b_im, ssm_c_re, ssm_c_im,
           ssm_d, ssm_glu_w, ssm_glu_b, w_out, norm_ffn_g, w_ff1, w_ff2, norm_final_g):
    assert w_in.shape[0] == 1, "single-layer block"
    bsz, seq, _ = x.shape
    assert seq % CHUNK == 0 and CHUNK >= N_META
    n_chunks = seq // CHUNK
    g, h = N_GROUPS, SSM_GROUP

    x4 = x.reshape(bsz, n_chunks, CHUNK, D_MODEL)
    g_mix = norm_mix_g[0][None, :]
    u_conv, u_col, u_conv_m, u_ssm_m = _in_proj_step(x4, meta_tokens, g_mix, w_in[0])

    conv_out = _conv_module(u_conv, u_conv_m, conv_w[0], conv_b[0][None, :],
                            conv_ln_g[0][None, :], conv_ln_b[0][None, :], bsz)

    ldt = jnp.broadcast_to(ssm_log_dt[0][..., None], ssm_lam_re[0].shape)
    flat = lambda a: a.reshape(2 * g, SSM_STATE)
    lbr, lbi, zr, zi = [a.reshape(2, g, SSM_STATE) for a in
                        _zoh(flat(ssm_lam_re[0]), flat(ssm_lam_im[0]), flat(ldt))]
    csc, c_col, b_col, c_row = _s5_operator_inputs(
        lbr, lbi, zr, zi, ssm_b_re[0], ssm_b_im[0], ssm_c_re[0], ssm_c_im[0])

    u_meta = jnp.transpose(u_ssm_m.reshape(N_META, g, h), (1, 0, 2)).reshape(g, 1, N_META * h)
    y_col = _ssm(u_col, u_meta, csc, c_col, b_col, c_row, n_chunks)

    out = _mix_ffn(x4, conv_out, y_col, u_col, ssm_d[0][:, None],
                   ssm_glu_w[0].T.astype(BF16), ssm_glu_b[0][:, None],
                   w_out[0].astype(BF16), norm_ffn_g[0][None, :], w_ff1[0], w_ff2[0],
                   norm_final_g[None, :])
    return out.reshape(bsz, seq, D_MODEL)
```

```python
import functools

import jax
import jax.numpy as jnp
from jax import lax
from jax.experimental import pallas as pl
from jax.experimental.pallas import tpu as pltpu

F32 = jnp.float32
BF16 = jnp.bfloat16

D_MODEL = 1024
N_META = 16
D_CONV = 512
D_SSM = 512
CONV_WIDTH = 31
CONV_PAD = CONV_WIDTH // 2
SSM_GROUP = 16
N_GROUPS = D_SSM // SSM_GROUP
SSM_STATE = 64
D_FF = 4096
NORM_EPS = 1e-5
LANES = 128
SUBLANES = 8

CHUNK = 32
CHUNK_LANES = CHUNK * SSM_GROUP
STATE_ROWS = 4 * SSM_STATE
DIR_ROWS = 2 * SSM_STATE
STEPS_PER_VREG = LANES // SSM_GROUP

VMEM_LIMIT_BYTES = 60 * 1024 * 1024


def _rms(x, g):
    return x * lax.rsqrt(jnp.mean(x * x, axis=-1, keepdims=True) + NORM_EPS) * g


def _params(n_axes=1):
    return pltpu.CompilerParams(dimension_semantics=("parallel",) * n_axes,
                                vmem_limit_bytes=VMEM_LIMIT_BYTES)


def _step_copies(hbm_ref, buf_ref, sem_ref, tile, slot, to_hbm):
    bsz = hbm_ref.shape[0]
    slab, b = tile // bsz, tile % bsz
    copies = []
    for r in range(SUBLANES):
        hbm = hbm_ref.at[b, :, slab * SUBLANES + r, :]
        vmem = buf_ref.at[slot, r]
        src, dst = (vmem, hbm) if to_hbm else (hbm, vmem)
        copies.append(pltpu.make_async_copy(src, dst, sem_ref.at[slot, r]))
    return copies


def _fetch_steps(x_hbm, xbuf, sem):
    tile = pl.program_id(0)
    slot = tile % 2

    @pl.when(tile == 0)
    def _():
        for cp in _step_copies(x_hbm, xbuf, sem, tile, slot, False):
            cp.start()

    @pl.when(tile + 1 < pl.num_programs(0))
    def _():
        for cp in _step_copies(x_hbm, xbuf, sem, tile + 1, 1 - slot, False):
            cp.start()

    for cp in _step_copies(x_hbm, xbuf, sem, tile, slot, False):
        cp.wait()
    return slot


def _step_tile_params():
    return pltpu.CompilerParams(dimension_semantics=("arbitrary",),
                                vmem_limit_bytes=VMEM_LIMIT_BYTES)


def _in_proj_step_kernel(x_hbm, meta_ref, g_ref, w32_ref, uc_ref, ut_ref, ucm_ref, usm_ref,
                         xbuf, sem, w_ref):
    n_chunks = x_hbm.shape[1]
    slot = _fetch_steps(x_hbm, xbuf, sem)

    @pl.when(pl.program_id(0) == 0)
    def _():
        w_ref[...] = w32_ref[...].astype(BF16)
        zm = _rms(meta_ref[...], g_ref[...]).astype(BF16)
        pm = jnp.dot(zm, w_ref[...], preferred_element_type=F32)
        ucm_ref[...] = pm[:, :D_CONV] * jax.nn.sigmoid(pm[:, D_CONV:2 * D_CONV])
        usm_ref[...] = pm[:, 2 * D_CONV:]

    x = xbuf[slot].reshape(SUBLANES * n_chunks, D_MODEL)
    z = _rms(x, g_ref[...]).astype(BF16)
    p = jnp.dot(z, w_ref[...], preferred_element_type=F32)
    uc = p[:, :D_CONV] * jax.nn.sigmoid(p[:, D_CONV:2 * D_CONV])
    uc_ref[...] = uc.reshape(SUBLANES, n_chunks, D_CONV)
    ut = p[:, 2 * D_CONV:].T.astype(BF16)
    for r in range(SUBLANES):
        ut_ref[:, r * SSM_GROUP:(r + 1) * SSM_GROUP, :] = (
            ut[:, r * n_chunks:(r + 1) * n_chunks].reshape(N_GROUPS, SSM_GROUP, n_chunks))


def _in_proj_step(x4, meta, g, w_in):
    bsz, n_chunks = x4.shape[:2]
    rows = bsz * n_chunks
    return pl.pallas_call(
        _in_proj_step_kernel,
        grid=(CHUNK // SUBLANES * bsz,),
        in_specs=[
            pl.BlockSpec(memory_space=pl.ANY),
            pl.BlockSpec((N_META, D_MODEL), lambda t: (0, 0)),
            pl.BlockSpec((1, D_MODEL), lambda t: (0, 0)),
            pl.BlockSpec((D_MODEL, 2 * D_CONV + D_SSM), lambda t: (0, 0),
                         pipeline_mode=pl.Buffered(1)),
        ],
        out_specs=[
            pl.BlockSpec((SUBLANES, n_chunks, D_CONV), lambda t: (t // bsz, t % bsz, 0)),
            pl.BlockSpec((N_GROUPS, SUBLANES * SSM_GROUP, n_chunks),
                         lambda t: (0, t // bsz, t % bsz)),
            pl.BlockSpec((N_META, D_CONV), lambda t: (0, 0)),
            pl.BlockSpec((N_META, D_SSM), lambda t: (0, 0)),
        ],
        out_shape=[
            jax.ShapeDtypeStruct((CHUNK, rows, D_CONV), F32),
            jax.ShapeDtypeStruct((N_GROUPS, CHUNK_LANES, rows), BF16),
            jax.ShapeDtypeStruct((N_META, D_CONV), F32),
            jax.ShapeDtypeStruct((N_META, D_SSM), F32),
        ],
        scratch_shapes=[pltpu.VMEM((2, SUBLANES, n_chunks, D_MODEL), F32),
                        pltpu.SemaphoreType.DMA((2, SUBLANES)),
                        pltpu.VMEM((D_MODEL, 2 * D_CONV + D_SSM), BF16)],
        compiler_params=_step_tile_params(),
        name="in_proj",
    )(x4, meta, g, w_in)


CONV_GROUP = 4
CONV_ROWS = 64


def _conv_kernel(u_ref, um_ref, w_ref, cb_ref, lg_ref, lb_ref, o_ref, cat_ref, acc_ref):
    n_steps, n_chunks, _ = u_ref.shape
    cat_ref[CONV_PAD:CONV_PAD + n_steps] = u_ref[...]
    chunk = lax.broadcasted_iota(jnp.int32, (n_chunks, 1), 0)
    for i in range(CONV_PAD):
        s_prev = n_steps - CONV_PAD + i
        prev = pltpu.roll(u_ref[s_prev], 1, axis=0)
        meta_row = um_ref[N_META - CONV_PAD + i:N_META - CONV_PAD + i + 1, :]
        cat_ref[i] = jnp.where(chunk == 0, meta_row, prev)
        nxt = pltpu.roll(u_ref[i], n_chunks - 1, axis=0)
        cat_ref[n_steps + CONV_PAD + i] = jnp.where(chunk == n_chunks - 1, 0.0, nxt)

    def body(sg, carry):
        s0 = sg * CONV_GROUP
        for c in range(D_CONV // LANES):
            cols = slice(c * LANES, (c + 1) * LANES)
            n_sub = CONV_ROWS // SUBLANES

            def rows_body(rt, carry2):
                r0 = [pl.multiple_of(rt * CONV_ROWS + sub * SUBLANES, SUBLANES)
                      for sub in range(n_sub)]
                accs = [[jnp.zeros((SUBLANES, LANES), F32)] * CONV_GROUP for _ in range(n_sub)]
                for i in range(CONV_WIDTH + CONV_GROUP - 1):
                    ds = [cat_ref[s0 + i, pl.ds(r0[sub], SUBLANES), cols]
                          for sub in range(n_sub)]
                    for j in range(CONV_GROUP):
                        if 0 <= i - j < CONV_WIDTH:
                            tap = jnp.broadcast_to(w_ref[i - j:i - j + 1, cols],
                                                   (SUBLANES, LANES))
                            for sub in range(n_sub):
                                accs[sub][j] = accs[sub][j] + ds[sub] * tap
                for sub in range(n_sub):
                    for j in range(CONV_GROUP):
                        acc_ref[j, pl.ds(r0[sub], SUBLANES), cols] = accs[sub][j]
                return carry2

            lax.fori_loop(0, n_chunks // CONV_ROWS, rows_body, 0)
        for j in range(CONV_GROUP):
            y = acc_ref[j] + cb_ref[...]
            yc = y - jnp.mean(y, axis=-1, keepdims=True)
            yn = yc * lax.rsqrt(jnp.mean(yc * yc, axis=-1, keepdims=True) + NORM_EPS)
            yn = yn * lg_ref[...] + lb_ref[...]
            o_ref[s0 + j] = (yn * jax.nn.sigmoid(yn)).astype(BF16)
        return carry

    lax.fori_loop(0, n_steps // CONV_GROUP, body, 0)


def _conv_module(u_conv, u_conv_meta, conv_w, conv_b, ln_g, ln_b, bsz):
    n_steps, rows, _ = u_conv.shape
    n_chunks = rows // bsz
    vec = pl.BlockSpec((1, D_CONV), lambda b: (0, 0))
    blk = pl.BlockSpec((n_steps, n_chunks, D_CONV), lambda b: (0, b, 0))
    return pl.pallas_call(
        _conv_kernel,
        grid=(bsz,),
        in_specs=[
            blk,
            pl.BlockSpec((N_META, D_CONV), lambda b: (0, 0)),
            pl.BlockSpec((CONV_WIDTH, D_CONV), lambda b: (0, 0)),
            vec, vec, vec,
        ],
        out_specs=blk,
        out_shape=jax.ShapeDtypeStruct((n_steps, rows, D_CONV), BF16),
        scratch_shapes=[pltpu.VMEM((n_steps + 2 * CONV_PAD, n_chunks, D_CONV), F32),
                        pltpu.VMEM((CONV_GROUP, n_chunks, D_CONV), F32)],
        compiler_params=_params(),
        name="conv_module",
    )(u_conv, u_conv_meta, conv_w, conv_b, ln_g, ln_b)


def _zoh_kernel(lre_ref, lim_ref, ldt_ref, lbr_ref, lbi_ref, zr_ref, zi_ref):
    lre = lre_ref[...]
    lim = lim_ref[...]
    dt = jnp.exp(ldt_ref[...])
    ea = jnp.exp(lre * dt)
    lbr = ea * jnp.cos(lim * dt)
    lbi = ea * jnp.sin(lim * dt)
    nr = lbr - 1.0
    den = lre * lre + lim * lim
    lbr_ref[...] = lbr
    lbi_ref[...] = lbi
    zr_ref[...] = (nr * lre + lbi * lim) / den
    zi_ref[...] = (lbi * lre - nr * lim) / den


def _zoh(lam_re, lam_im, log_dt):
    shape = jax.ShapeDtypeStruct(lam_re.shape, F32)
    return pl.pallas_call(_zoh_kernel, out_shape=[shape] * 4, name="s5_zoh")(
        lam_re, lam_im, log_dt)


def _lanes(x, n):
    return jnp.concatenate([x] * (n // LANES), axis=1)


def _cmul(ar, ai, br, bi):
    return ar * br - ai * bi, ar * bi + ai * br


def _cpow(br, bi, expo, nbits):
    rr = jnp.ones(expo.shape, F32)
    ri = jnp.zeros(expo.shape, F32)
    for k in range(nbits):
        bit = ((expo >> k) & 1) == 1
        nr, ni = _cmul(rr, ri, br, bi)
        rr = jnp.where(bit, nr, rr)
        ri = jnp.where(bit, ni, ri)
        br, bi = _cmul(br, bi, br, bi)
    return rr, ri


def _build_chunk_ops(gi, csc_ref, cc_ref, bc_ref, cr_ref, um_ref, toep_ref, wend_ref,
                     wout_ref):
    t = CHUNK
    rows = STATE_ROWS
    n_cols = CHUNK_LANES // LANES
    lam = (jnp.broadcast_to(csc_ref[gi, :, 0:1], (rows, LANES)),
           jnp.broadcast_to(csc_ref[gi, :, 1:2], (rows, LANES)))
    zr = jnp.broadcast_to(csc_ref[gi, :, 2:3], (rows, LANES))
    zi = jnp.broadcast_to(csc_ref[gi, :, 3:4], (rows, LANES))
    sel = (lax.broadcasted_iota(jnp.int32, (SSM_GROUP, LANES), 1) % SSM_GROUP
           == lax.broadcasted_iota(jnp.int32, (SSM_GROUP, LANES), 0)).astype(F32)

    def tile(ref):
        return jnp.dot(ref[gi], sel, precision=lax.Precision.HIGHEST,
                       preferred_element_type=F32)

    def swap_re_im(a):
        p = SSM_STATE
        return jnp.concatenate([a[p:2 * p], a[0:p], a[3 * p:], a[2 * p:3 * p]], axis=0)

    c_same, b_same = tile(cc_ref), tile(bc_ref)
    c_swap, b_swap = swap_re_im(c_same), swap_re_im(b_same)
    row = lax.broadcasted_iota(jnp.int32, (rows, LANES), 0)
    is_re = (row // SSM_STATE) % 2 == 0
    sgn = jnp.where(is_re, -1.0, 1.0)
    ca = jnp.where(is_re, c_same, -c_same)
    cb = -c_swap
    bb_same = zr * b_same + sgn * zi * b_swap
    sbb_swap = sgn * (zr * b_swap - sgn * zi * b_same)

    pows = {1: lam}
    k = 1
    while k < t:
        pows[2 * k] = _cmul(*pows[k], *pows[k])
        k *= 2
    step_bits = STEPS_PER_VREG.bit_length() - 1
    fwd = slice(0, DIR_ROWS)
    bwd = slice(DIR_ROWS, rows)
    every = slice(0, rows)

    def rows_of(v, rs):
        return v[0][rs], v[1][rs]

    def column(base, e, rs):
        out = rows_of(base, rs)
        for bit, val in pows.items():
            if e & bit:
                out = _cmul(*out, *rows_of(val, rs))
        return out

    def times_b(p, rs):
        return p[0] * bb_same[rs] + p[1] * sbb_swap[rs]

    def times_c(p, rs):
        return ca[rs] * p[0] + cb[rs] * p[1]

    i8 = lax.broadcasted_iota(jnp.int32, (rows, LANES), 1) // SSM_GROUP
    asc0 = _cpow(*lam, i8, step_bits)
    desc0 = _cpow(*lam, STEPS_PER_VREG - 1 - i8, step_bits)
    asc1_0 = _cmul(*asc0, *lam)
    desc1_0 = _cmul(*desc0, *lam)

    wend_f, wend_b, lag_b, wout_f, wout_b = [], [], [], [], []
    for q in range(n_cols):
        e_asc = STEPS_PER_VREG * q
        e_desc = STEPS_PER_VREG * (n_cols - 1 - q)
        asc1_q = column(asc1_0, e_asc, every)
        wend_f.append(times_b(column(desc0, e_desc, fwd), fwd))
        wend_b.append(times_b(column(asc0, e_asc, bwd), bwd))
        lag_b.append(times_b(rows_of(asc1_q, bwd), bwd))
        wout_f.append(times_c(rows_of(asc1_q, fwd), fwd))
        wout_b.append(times_c(column(desc1_0, e_desc, bwd), bwd))
        cols = slice(q * LANES, (q + 1) * LANES)
        wend_ref[gi, fwd, cols] = wend_f[q].astype(BF16)
        wend_ref[gi, bwd, cols] = wend_b[q].astype(BF16)

    meta_cols = N_META * SSM_GROUP // LANES
    x0 = sum(jnp.sum(wend_f[n_cols - meta_cols + m] * um_ref[gi, :, m * LANES:(m + 1) * LANES],
                     axis=1, keepdims=True) for m in range(meta_cols))

    lane = lax.broadcasted_iota(jnp.int32, (DIR_ROWS, LANES), 1)
    lag0_b = jnp.where(lane >= LANES - SSM_GROUP, bb_same[bwd], 0.0)
    zero = jnp.zeros((DIR_ROWS, LANES), F32)
    bcat = jnp.concatenate(
        [jnp.concatenate(wend_f + [zero] * n_cols, axis=1),
         jnp.concatenate([zero] * (n_cols - 1) + [lag0_b] + lag_b, axis=1)], axis=0)
    lane_r = lax.broadcasted_iota(jnp.int32, (SSM_GROUP, rows), 1)
    c2 = jnp.where((lane_r // SSM_STATE) % 2 == 0, cr_ref[gi], -cr_ref[gi])
    kk = jnp.dot(c2, bcat, precision=lax.Precision.HIGHEST,
                 preferred_element_type=F32)
    for tt in range(t):
        off = (t - 1 - tt) * SSM_GROUP
        toep_ref[gi, tt * SSM_GROUP:(tt + 1) * SSM_GROUP, :] = (
            kk[:, off:off + CHUNK_LANES].astype(BF16))

    wout = jnp.concatenate([jnp.concatenate(wout_f, axis=1),
                            jnp.concatenate(wout_b, axis=1)], axis=0)
    wout_ref[gi] = wout.T.astype(BF16)
    return pows[t], x0


SSM_GROUPS_PER_STEP = 2


def _chunk_scan(er, ei, ar, ai, x0, chunk, n_chunks, forward):
    width = er.shape[1]
    xr, xi = er, ei
    if x0 is not None:
        fr, fi = _cmul(ar, ai, x0[0], x0[1])
        xr = xr + jnp.where(chunk == 0, _lanes(fr, width), 0.0)
        xi = xi + jnp.where(chunk == 0, _lanes(fi, width), 0.0)

    def shifted(v, sh):
        if forward:
            return jnp.where(chunk >= sh, pltpu.roll(v, sh, axis=1), 0.0)
        return jnp.where(chunk < n_chunks - sh, pltpu.roll(v, width - sh, axis=1), 0.0)

    sh = 1
    while sh < n_chunks:
        sr, si = shifted(xr, sh), shifted(xi, sh)
        wr, wi = _lanes(ar, width), _lanes(ai, width)
        xr, xi = xr + wr * sr - wi * si, xi + wr * si + wi * sr
        ar, ai = _cmul(ar, ai, ar, ai)
        sh *= 2
    inr, ini = shifted(xr, 1), shifted(xi, 1)
    if x0 is not None:
        inr = jnp.where(chunk == 0, x0[0], inr)
        ini = jnp.where(chunk == 0, x0[1], ini)
    return inr, ini


def _ssm_kernel(u_ref, um_ref, csc_ref, cc_ref, bc_ref, cr_ref, y_ref, toep_ref, wend_ref,
                wout_ref, yi_ref, *, n_chunks):
    for gi in range(SSM_GROUPS_PER_STEP):
        (lam_r, lam_i), x0 = _build_chunk_ops(gi, csc_ref, cc_ref, bc_ref, cr_ref, um_ref,
                                              toep_ref, wend_ref, wout_ref)
        u = u_ref[gi]
        width = u.shape[1]
        e = jnp.dot(wend_ref[gi], u, preferred_element_type=F32)
        yi_ref[gi] = jnp.dot(toep_ref[gi], u, preferred_element_type=F32)
        chunk =lax.broadcasted_iota(jnp.int32, (1, width), 1) % n_chunks
        p = SSM_STATE
        ar, ai = lam_r, lam_i
        f_in = _chunk_scan(e[0:p], e[p:2 * p], ar[0:p], ai[0:p], (x0[0:p], x0[p:2 * p]),
                           chunk, n_chunks, True)
        b_in = _chunk_scan(e[2 * p:3 * p], e[3 * p:], ar[2 * p:3 * p], ai[2 * p:3 * p],
                           None, chunk, n_chunks, False)
        xin = jnp.concatenate([f_in[0], f_in[1], b_in[0], b_in[1]], axis=0).astype(BF16)
        y = yi_ref[gi] + jnp.dot(wout_ref[gi], xin, preferred_element_type=F32)
        y_ref[gi] = y.astype(BF16)


def _ssm(u_col, u_meta, csc, c_col, b_col, c_row, n_chunks):
    g, _, width = u_col.shape
    per = SSM_GROUPS_PER_STEP
    return pl.pallas_call(
        functools.partial(_ssm_kernel, n_chunks=n_chunks),
        grid=(g // per,),
        in_specs=[
            pl.BlockSpec((per, CHUNK_LANES, width), lambda i: (i, 0, 0)),
            pl.BlockSpec((per, 1, N_META * SSM_GROUP), lambda i: (i, 0, 0)),
            pl.BlockSpec((per, STATE_ROWS, 4), lambda i: (i, 0, 0)),
            pl.BlockSpec((per, STATE_ROWS, SSM_GROUP), lambda i: (i, 0, 0)),
            pl.BlockSpec((per, STATE_ROWS, SSM_GROUP), lambda i: (i, 0, 0)),
            pl.BlockSpec((per, SSM_GROUP, STATE_ROWS), lambda i: (i, 0, 0)),
        ],
        out_specs=pl.BlockSpec((per, CHUNK_LANES, width), lambda i: (i, 0, 0)),
        out_shape=jax.ShapeDtypeStruct((g, CHUNK_LANES, width), BF16),
        scratch_shapes=[
            pltpu.VMEM((per, CHUNK_LANES, CHUNK_LANES), BF16),
            pltpu.VMEM((per, STATE_ROWS, CHUNK_LANES), BF16),
            pltpu.VMEM((per, CHUNK_LANES, STATE_ROWS), BF16),
            pltpu.VMEM((per, CHUNK_LANES, width), F32),
        ],
        compiler_params=_params(),
        name="s5_mixer",
    )(u_col, u_meta, csc, c_col, b_col, c_row)


FF_CHUNK = 1024
MIX_PER = 4


MIX_SLOTS = 3


def _mix_ffn_kernel(x_hbm, co_ref, y_ref, ut_ref, d_ref, gwt_ref, gb_ref, wo_ref,
                    gf_ref, w1_hbm, w2_hbm, gl_ref, o_hbm, buf, sem_in, sem_out,
                    mixed_ref, h_ref, z_ref, a_ref, w1_ref, w2_ref, sem_w):
    n_chunks = x_hbm.shape[1]
    rows = SUBLANES * n_chunks
    step = pl.program_id(0)
    n_tiles = pl.num_programs(0) - 1
    tile = step - 1

    def weight_copies():
        return [pltpu.make_async_copy(w1_hbm, w1_ref, sem_w.at[0]),
                pltpu.make_async_copy(w2_hbm, w2_ref, sem_w.at[1])]

    def mix_stages(dst):
        per = MIX_PER
        n_half = SUBLANES // per
        state = {}

        def col(ref, k):
            return jnp.concatenate(
                [ref[:, r * SSM_GROUP:(r + 1) * SSM_GROUP, :].reshape(D_SSM, n_chunks)
                 for r in range(k * per, (k + 1) * per)], axis=1).astype(F32)

        def gelu_stage(k):
            def run():
                state["ge", k] = jax.nn.gelu(col(y_ref, k) + d_ref[...] * col(ut_ref, k))
            return run

        def glu_stage(k):
            def run():
                ge = state["ge", k]
                gate = (jnp.dot(gwt_ref[...], ge.astype(BF16), preferred_element_type=F32)
                        + gb_ref[...])
                state["s5t", k] = ge * jax.nn.sigmoid(gate)
            return run

        def store_stage(k):
            def run():
                rs = slice(k * per * n_chunks, (k + 1) * per * n_chunks)
                mixed_ref[dst, rs, :D_CONV] = co_ref[k * per:(k + 1) * per].reshape(
                    per * n_chunks, D_CONV)
                mixed_ref[dst, rs, D_CONV:] = state["s5t", k].T.astype(BF16)
            return run

        return [f(k) for f in (gelu_stage, glu_stage, store_stage) for k in range(n_half)]

    @pl.when(step == 0)
    def _():
        for cp in _step_copies(x_hbm, buf, sem_in, 0, 0, False) + weight_copies():
            cp.start()
        for stage in mix_stages(0):
            stage()

    @pl.when(step >= 1)
    def _():
        slot = tile % MIX_SLOTS
        nxt = (tile + 1) % MIX_SLOTS

        @pl.when(tile == 0)
        def _():
            for cp in weight_copies():
                cp.wait()

        @pl.when(tile >= 2)
        def _():
            for cp in _step_copies(o_hbm, buf, sem_out, tile - 2, nxt, True):
                cp.wait()

        @pl.when(tile + 1 < n_tiles)
        def _():
            for cp in _step_copies(x_hbm, buf, sem_in, tile + 1, nxt, False):
                cp.start()

        for cp in _step_copies(x_hbm, buf, sem_in, tile, slot, False):
            cp.wait()

        stages = mix_stages(step % 2)
        half = D_FF // 2
        n_slots = D_FF // FF_CHUNK + 2
        n_stages = len(stages)
        done = [0, 0]

        def fill_slot():
            done[0] += 1
            while done[1] * n_slots < done[0] * n_stages:
                stages.pop(0)()
                done[1] += 1

        h = buf[slot].reshape(rows, D_MODEL) + jnp.dot(
            mixed_ref[tile % 2], wo_ref[...], preferred_element_type=F32)
        z_ref[...] = _rms(h, gf_ref[...]).astype(BF16)
        h_ref[...] = h
        for k in range(2):
            for j in range(half // FF_CHUNK):
                cols = slice(k * half + j * FF_CHUNK, k * half + (j + 1) * FF_CHUNK)
                a = jnp.dot(z_ref[...], w1_ref[:, cols], preferred_element_type=F32)
                a_ref[:, j * FF_CHUNK:(j + 1) * FF_CHUNK] = (
                    jnp.square(jnp.maximum(a, 0.0)).astype(BF16))
                fill_slot()
            h_ref[...] += jnp.dot(a_ref[...], w2_ref[k * half:(k + 1) * half, :],
                                  preferred_element_type=F32)
            fill_slot()
        assert not stages
        buf[slot] = _rms(h_ref[...], gl_ref[...]).reshape(SUBLANES, n_chunks, D_MODEL)

        for cp in _step_copies(o_hbm, buf, sem_out, tile, slot, True):
            cp.start()

        @pl.when(tile == n_tiles - 1)
        def _():
            for cp in (_step_copies(o_hbm, buf, sem_out, tile - 1, (tile - 1) % MIX_SLOTS, True)
                       + _step_copies(o_hbm, buf, sem_out, tile, slot, True)):
                cp.wait()


def _mix_ffn(x4, conv_out, y_col, u_col, d_col, glu_wt, glu_b_col, w_out, g_ffn, w1, w2,
             g_final):
    bsz, n_chunks = x4.shape[:2]

    def const(shape):
        return pl.BlockSpec(shape, lambda t: (0, 0), pipeline_mode=pl.Buffered(1))

    n_tiles = CHUNK // SUBLANES * bsz
    hbm = pl.BlockSpec(memory_space=pl.ANY)

    def slab(s):
        return jnp.minimum(s, n_tiles - 1) // bsz

    def batch(s):
        return jnp.minimum(s, n_tiles - 1) % bsz

    col_blk = pl.BlockSpec((N_GROUPS, SUBLANES * SSM_GROUP, n_chunks),
                           lambda s: (0, slab(s), batch(s)))
    step_sems = pltpu.SemaphoreType.DMA((MIX_SLOTS, SUBLANES))
    return pl.pallas_call(
        _mix_ffn_kernel,
        grid=(n_tiles + 1,),
        in_specs=[
            hbm,
            pl.BlockSpec((SUBLANES, n_chunks, D_CONV), lambda s: (slab(s), batch(s), 0)),
            col_blk,
            col_blk,
            const((D_SSM, 1)),
            const((D_SSM, D_SSM)),
            const((D_SSM, 1)),
            const((D_MODEL, D_MODEL)),
            const((1, D_MODEL)),
            hbm,
            hbm,
            const((1, D_MODEL)),
        ],
        out_specs=hbm,
        out_shape=jax.ShapeDtypeStruct(x4.shape, F32),
        scratch_shapes=[
            pltpu.VMEM((MIX_SLOTS, SUBLANES, n_chunks, D_MODEL), F32),
            step_sems, step_sems,
            pltpu.VMEM((2, SUBLANES * n_chunks, D_MODEL), BF16),
            pltpu.VMEM((SUBLANES * n_chunks, D_MODEL), F32),
            pltpu.VMEM((SUBLANES * n_chunks, D_MODEL), BF16),
            pltpu.VMEM((SUBLANES * n_chunks, D_FF // 2), BF16),
            pltpu.VMEM((D_MODEL, D_FF), BF16),
            pltpu.VMEM((D_FF, D_MODEL), BF16),
            pltpu.SemaphoreType.DMA((2,)),
        ],
        compiler_params=_step_tile_params(),
        name="mix_ffn",
    )(x4, conv_out, y_col, u_col, d_col, glu_wt, glu_b_col, w_out, g_ffn, w1, w2, g_final)


def _s5_operator_inputs(lbr, lbi, zr, zi, b_re, b_im, c_re, c_im):
    g, p, h = N_GROUPS, SSM_STATE, SSM_GROUP

    sc = jnp.transpose(jnp.stack([lbr, lbi, zr, zi], axis=-1), (1, 0, 2, 3))
    csc = jnp.broadcast_to(sc[:, :, None], (g, 2, 2, p, 4)).reshape(g, 4 * p, 4)
    c = jnp.stack([c_re, c_im], axis=0)
    b = jnp.stack([b_re, b_im], axis=0)
    c_col = jnp.transpose(c, (2, 1, 0, 4, 3)).reshape(g, 4 * p, h)
    b_col = jnp.transpose(b, (2, 1, 0, 3, 4)).reshape(g, 4 * p, h)
    c_row = jnp.transpose(c, (2, 3, 1, 0, 4)).reshape(g, h, 4 * p)
    return csc, c_col, b_col, c_row


def kernel(x, meta_tokens, norm_mix_g, w_in, conv_w, conv_b, conv_ln_g, conv_ln_b,
           ssm_lam_re, ssm_lam_im, ssm_log_dt, ssm_b_re, ssm_b_im, ssm_c_re, ssm_c_im,
           ssm_d, ssm_glu_w, ssm_glu_b, w_out, norm_ffn_g, w_ff1, w_ff2, norm_final_g):
    assert w_in.shape[0] == 1, "single-layer block"
    bsz, seq, _ = x.shape
    assert seq % CHUNK == 0 and CHUNK >= N_META
    n_chunks = seq // CHUNK
    g, h = N_GROUPS, SSM_GROUP

    x4 = x.reshape(bsz, n_chunks, CHUNK, D_MODEL)
    g_mix = norm_mix_g[0][None, :]
    u_conv, u_col, u_conv_m, u_ssm_m = _in_proj_step(x4, meta_tokens, g_mix, w_in[0])

    conv_out = _conv_module(u_conv, u_conv_m, conv_w[0], conv_b[0][None, :],
                            conv_ln_g[0][None, :], conv_ln_b[0][None, :], bsz)

    ldt = jnp.broadcast_to(ssm_log_dt[0][..., None], ssm_lam_re[0].shape)
    flat = lambda a: a.reshape(2 * g, SSM_STATE)
    lbr, lbi, zr, zi = [a.reshape(2, g, SSM_STATE) for a in
                        _zoh(flat(ssm_lam_re[0]), flat(ssm_lam_im[0]), flat(ldt))]
    csc, c_col, b_col, c_row = _s5_operator_inputs(
        lbr, lbi, zr, zi, ssm_b_re[0], ssm_b_im[0], ssm_c_re[0], ssm_c_im[0])

    u_meta = jnp.transpose(u_ssm_m.reshape(N_META, g, h), (1, 0, 2)).reshape(g, 1, N_META * h)
    y_col = _ssm(u_col, u_meta, csc, c_col, b_col, c_row, n_chunks)

    out = _mix_ffn(x4, conv_out, y_col, u_col, ssm_d[0][:, None],
                   ssm_glu_w[0].T.astype(BF16), ssm_glu_b[0][:, None],
                   w_out[0].astype(BF16), norm_ffn_g[0][None, :], w_ff1[0].astype(BF16),
                   w_ff2[0].astype(BF16), norm_final_g[None, :])
    return out.reshape(bsz, seq, D_MODEL)
```

```python
import functools

import jax
import jax.numpy as jnp
from jax import lax
from jax.experimental import pallas as pl
from jax.experimental.pallas import tpu as pltpu

F32 = jnp.float32
BF16 = jnp.bfloat16

D_MODEL = 1024
N_META = 16
D_CONV = 512
D_SSM = 512
CONV_WIDTH = 31
CONV_PAD = CONV_WIDTH // 2
SSM_GROUP = 16
N_GROUPS = D_SSM // SSM_GROUP
SSM_STATE = 64
D_FF = 4096
NORM_EPS = 1e-5
LANES = 128
SUBLANES = 8

CHUNK = 32
CHUNK_LANES = CHUNK * SSM_GROUP
STATE_ROWS = 4 * SSM_STATE
DIR_ROWS = 2 * SSM_STATE
STEPS_PER_VREG = LANES // SSM_GROUP

VMEM_LIMIT_BYTES = 60 * 1024 * 1024


def _rms(x, g):
    return x * lax.rsqrt(jnp.mean(x * x, axis=-1, keepdims=True) + NORM_EPS) * g


def _params(n_axes=1):
    return pltpu.CompilerParams(dimension_semantics=("parallel",) * n_axes,
                                vmem_limit_bytes=VMEM_LIMIT_BYTES)


def _step_copies(hbm_ref, buf_ref, sem_ref, tile, slot, to_hbm):
    bsz = hbm_ref.shape[0]
    slab, b = tile // bsz, tile % bsz
    copies = []
    for r in range(SUBLANES):
        hbm = hbm_ref.at[b, :, slab * SUBLANES + r, :]
        vmem = buf_ref.at[slot, r]
        src, dst = (vmem, hbm) if to_hbm else (hbm, vmem)
        copies.append(pltpu.make_async_copy(src, dst, sem_ref.at[slot, r]))
    return copies


def _fetch_steps(x_hbm, xbuf, sem):
    tile = pl.program_id(0)
    slot = tile % 2

    @pl.when(tile == 0)
    def _():
        for cp in _step_copies(x_hbm, xbuf, sem, tile, slot, False):
            cp.start()

    @pl.when(tile + 1 < pl.num_programs(0))
    def _():
        for cp in _step_copies(x_hbm, xbuf, sem, tile + 1, 1 - slot, False):
            cp.start()

    for cp in _step_copies(x_hbm, xbuf, sem, tile, slot, False):
        cp.wait()
    return slot


def _step_tile_params():
    return pltpu.CompilerParams(dimension_semantics=("arbitrary",),
                                vmem_limit_bytes=VMEM_LIMIT_BYTES)


def _in_proj_step_kernel(x_hbm, meta_ref, g_ref, w32_ref, uc_ref, ut_ref, ucm_ref, usm_ref,
                         xbuf, sem, w_ref):
    n_chunks = x_hbm.shape[1]
    slot = _fetch_steps(x_hbm, xbuf, sem)

    @pl.when(pl.program_id(0) == 0)
    def _():
        w_ref[...] = w32_ref[...].astype(BF16)
        zm = _rms(meta_ref[...], g_ref[...]).astype(BF16)
        pm = jnp.dot(zm, w_ref[...], preferred_element_type=F32)
        ucm_ref[...] = pm[:, :D_CONV] * jax.nn.sigmoid(pm[:, D_CONV:2 * D_CONV])
        usm_ref[...] = pm[:, 2 * D_CONV:]

    x = xbuf[slot].reshape(SUBLANES * n_chunks, D_MODEL)
    z = _rms(x, g_ref[...]).astype(BF16)
    p = jnp.dot(z, w_ref[...], preferred_element_type=F32)
    uc = p[:, :D_CONV] * jax.nn.sigmoid(p[:, D_CONV:2 * D_CONV])
    uc_ref[...] = uc.reshape(SUBLANES, n_chunks, D_CONV)
    ut = p[:, 2 * D_CONV:].T.astype(BF16)
    for r in range(SUBLANES):
        ut_ref[:, r * SSM_GROUP:(r + 1) * SSM_GROUP, :] = (
            ut[:, r * n_chunks:(r + 1) * n_chunks].reshape(N_GROUPS, SSM_GROUP, n_chunks))


def _in_proj_step(x4, meta, g, w_in):
    bsz, n_chunks = x4.shape[:2]
    rows = bsz * n_chunks
    return pl.pallas_call(
        _in_proj_step_kernel,
        grid=(CHUNK // SUBLANES * bsz,),
        in_specs=[
            pl.BlockSpec(memory_space=pl.ANY),
            pl.BlockSpec((N_META, D_MODEL), lambda t: (0, 0)),
            pl.BlockSpec((1, D_MODEL), lambda t: (0, 0)),
            pl.BlockSpec((D_MODEL, 2 * D_CONV + D_SSM), lambda t: (0, 0),
                         pipeline_mode=pl.Buffered(1)),
        ],
        out_specs=[
            pl.BlockSpec((SUBLANES, n_chunks, D_CONV), lambda t: (t // bsz, t % bsz, 0)),
            pl.BlockSpec((N_GROUPS, SUBLANES * SSM_GROUP, n_chunks),
                         lambda t: (0, t // bsz, t % bsz)),
            pl.BlockSpec((N_META, D_CONV), lambda t: (0, 0)),
            pl.BlockSpec((N_META, D_SSM), lambda t: (0, 0)),
        ],
        out_shape=[
            jax.ShapeDtypeStruct((CHUNK, rows, D_CONV), F32),
            jax.ShapeDtypeStruct((N_GROUPS, CHUNK_LANES, rows), BF16),
            jax.ShapeDtypeStruct((N_META, D_CONV), F32),
            jax.ShapeDtypeStruct((N_META, D_SSM), F32),
        ],
        scratch_shapes=[pltpu.VMEM((2, SUBLANES, n_chunks, D_MODEL), F32),
                        pltpu.SemaphoreType.DMA((2, SUBLANES)),
                        pltpu.VMEM((D_MODEL, 2 * D_CONV + D_SSM), BF16)],
        compiler_params=_step_tile_params(),
        name="in_proj",
    )(x4, meta, g, w_in)


CONV_GROUP = 4
CONV_ROWS = 64


def _conv_kernel(u_ref, um_ref, w_ref, cb_ref, lg_ref, lb_ref, o_ref, cat_ref, acc_ref):
    n_steps, n_chunks, _ = u_ref.shape
    cat_ref[CONV_PAD:CONV_PAD + n_steps] = u_ref[...]
    chunk = lax.broadcasted_iota(jnp.int32, (n_chunks, 1), 0)
    for i in range(CONV_PAD):
        s_prev = n_steps - CONV_PAD + i
        prev = pltpu.roll(u_ref[s_prev], 1, axis=0)
        meta_row = um_ref[N_META - CONV_PAD + i:N_META - CONV_PAD + i + 1, :]
        cat_ref[i] = jnp.where(chunk == 0, meta_row, prev)
        nxt = pltpu.roll(u_ref[i], n_chunks - 1, axis=0)
        cat_ref[n_steps + CONV_PAD + i] = jnp.where(chunk == n_chunks - 1, 0.0, nxt)

    def body(sg, carry):
        s0 = sg * CONV_GROUP
        for c in range(D_CONV // LANES):
            cols = slice(c * LANES, (c + 1) * LANES)
            n_sub = CONV_ROWS // SUBLANES

            def rows_body(rt, carry2):
                r0 = [pl.multiple_of(rt * CONV_ROWS + sub * SUBLANES, SUBLANES)
                      for sub in range(n_sub)]
                accs = [[jnp.zeros((SUBLANES, LANES), F32)] * CONV_GROUP for _ in range(n_sub)]
                for i in range(CONV_WIDTH + CONV_GROUP - 1):
                    ds = [cat_ref[s0 + i, pl.ds(r0[sub], SUBLANES), cols]
                          for sub in range(n_sub)]
                    for j in range(CONV_GROUP):
                        if 0 <= i - j < CONV_WIDTH:
                            tap = jnp.broadcast_to(w_ref[i - j:i - j + 1, cols],
                                                   (SUBLANES, LANES))
                            for sub in range(n_sub):
                                accs[sub][j] = accs[sub][j] + ds[sub] * tap
                for sub in range(n_sub):
                    for j in range(CONV_GROUP):
                        acc_ref[j, pl.ds(r0[sub], SUBLANES), cols] = accs[sub][j]
                return carry2

            lax.fori_loop(0, n_chunks // CONV_ROWS, rows_body, 0)
        for j in range(CONV_GROUP):
            y = acc_ref[j] + cb_ref[...]
            yc = y - jnp.mean(y, axis=-1, keepdims=True)
            yn = yc * lax.rsqrt(jnp.mean(yc * yc, axis=-1, keepdims=True) + NORM_EPS)
            yn = yn * lg_ref[...] + lb_ref[...]
            o_ref[s0 + j] = (yn * jax.nn.sigmoid(yn)).astype(BF16)
        return carry

    lax.fori_loop(0, n_steps // CONV_GROUP, body, 0)


def _conv_module(u_conv, u_conv_meta, conv_w, conv_b, ln_g, ln_b, bsz):
    n_steps, rows, _ = u_conv.shape
    n_chunks = rows // bsz
    vec = pl.BlockSpec((1, D_CONV), lambda b: (0, 0))
    blk = pl.BlockSpec((n_steps, n_chunks, D_CONV), lambda b: (0, b, 0))
    return pl.pallas_call(
        _conv_kernel,
        grid=(bsz,),
        in_specs=[
            blk,
            pl.BlockSpec((N_META, D_CONV), lambda b: (0, 0)),
            pl.BlockSpec((CONV_WIDTH, D_CONV), lambda b: (0, 0)),
            vec, vec, vec,
        ],
        out_specs=blk,
        out_shape=jax.ShapeDtypeStruct((n_steps, rows, D_CONV), BF16),
        scratch_shapes=[pltpu.VMEM((n_steps + 2 * CONV_PAD, n_chunks, D_CONV), F32),
                        pltpu.VMEM((CONV_GROUP, n_chunks, D_CONV), F32)],
        compiler_params=_params(),
        name="conv_module",
    )(u_conv, u_conv_meta, conv_w, conv_b, ln_g, ln_b)


def _zoh_kernel(lre_ref, lim_ref, ldt_ref, lbr_ref, lbi_ref, zr_ref, zi_ref):
    lre = lre_ref[...]
    lim = lim_ref[...]
    dt = jnp.exp(ldt_ref[...])
    ea = jnp.exp(lre * dt)
    lbr = ea * jnp.cos(lim * dt)
    lbi = ea * jnp.sin(lim * dt)
    nr = lbr - 1.0
    den = lre * lre + lim * lim
    lbr_ref[...] = lbr
    lbi_ref[...] = lbi
    zr_ref[...] = (nr * lre + lbi * lim) / den
    zi_ref[...] = (lbi * lre - nr * lim) / den


def _zoh(lam_re, lam_im, log_dt):
    shape = jax.ShapeDtypeStruct(lam_re.shape, F32)
    return pl.pallas_call(_zoh_kernel, out_shape=[shape] * 4, name="s5_zoh")(
        lam_re, lam_im, log_dt)


def _lanes(x, n):
    return jnp.concatenate([x] * (n // LANES), axis=1)


def _cmul(ar, ai, br, bi):
    return ar * br - ai * bi, ar * bi + ai * br


def _cpow(br, bi, expo, nbits):
    rr = jnp.ones(expo.shape, F32)
    ri = jnp.zeros(expo.shape, F32)
    for k in range(nbits):
        bit = ((expo >> k) & 1) == 1
        nr, ni = _cmul(rr, ri, br, bi)
        rr = jnp.where(bit, nr, rr)
        ri = jnp.where(bit, ni, ri)
        br, bi = _cmul(br, bi, br, bi)
    return rr, ri


def _build_chunk_ops(gi, csc_ref, cc_ref, bc_ref, cr_ref, um_ref, toep_ref, wend_ref,
                     wout_ref):
    t = CHUNK
    rows = STATE_ROWS
    n_cols = CHUNK_LANES // LANES
    lam = (jnp.broadcast_to(csc_ref[gi, :, 0:1], (rows, LANES)),
           jnp.broadcast_to(csc_ref[gi, :, 1:2], (rows, LANES)))
    zr = jnp.broadcast_to(csc_ref[gi, :, 2:3], (rows, LANES))
    zi = jnp.broadcast_to(csc_ref[gi, :, 3:4], (rows, LANES))
    sel = (lax.broadcasted_iota(jnp.int32, (SSM_GROUP, LANES), 1) % SSM_GROUP
           == lax.broadcasted_iota(jnp.int32, (SSM_GROUP, LANES), 0)).astype(F32)

    def tile(ref):
        return jnp.dot(ref[gi], sel, precision=lax.Precision.HIGHEST,
                       preferred_element_type=F32)

    def swap_re_im(a):
        p = SSM_STATE
        return jnp.concatenate([a[p:2 * p], a[0:p], a[3 * p:], a[2 * p:3 * p]], axis=0)

    c_same, b_same = tile(cc_ref), tile(bc_ref)
    c_swap, b_swap = swap_re_im(c_same), swap_re_im(b_same)
    row = lax.broadcasted_iota(jnp.int32, (rows, LANES), 0)
    is_re = (row // SSM_STATE) % 2 == 0
    sgn = jnp.where(is_re, -1.0, 1.0)
    ca = jnp.where(is_re, c_same, -c_same)
    cb = -c_swap
    bb_same = zr * b_same + sgn * zi * b_swap
    sbb_swap = sgn * (zr * b_swap - sgn * zi * b_same)

    pows = {1: lam}
    k = 1
    while k < t:
        pows[2 * k] = _cmul(*pows[k], *pows[k])
        k *= 2
    step_bits = STEPS_PER_VREG.bit_length() - 1
    fwd = slice(0, DIR_ROWS)
    bwd = slice(DIR_ROWS, rows)
    every = slice(0, rows)

    def rows_of(v, rs):
        return v[0][rs], v[1][rs]

    def column(base, e, rs):
        out = rows_of(base, rs)
        for bit, val in pows.items():
            if e & bit:
                out = _cmul(*out, *rows_of(val, rs))
        return out

    def times_b(p, rs):
        return p[0] * bb_same[rs] + p[1] * sbb_swap[rs]

    def times_c(p, rs):
        return ca[rs] * p[0] + cb[rs] * p[1]

    i8 = lax.broadcasted_iota(jnp.int32, (rows, LANES), 1) // SSM_GROUP
    asc0 = _cpow(*lam, i8, step_bits)
    desc0 = _cpow(*lam, STEPS_PER_VREG - 1 - i8, step_bits)
    asc1_0 = _cmul(*asc0, *lam)
    desc1_0 = _cmul(*desc0, *lam)

    wend_f, wend_b, lag_b, wout_f, wout_b = [], [], [], [], []
    for q in range(n_cols):
        e_asc = STEPS_PER_VREG * q
        e_desc = STEPS_PER_VREG * (n_cols - 1 - q)
        asc1_q = column(asc1_0, e_asc, every)
        wend_f.append(times_b(column(desc0, e_desc, fwd), fwd))
        wend_b.append(times_b(column(asc0, e_asc, bwd), bwd))
        lag_b.append(times_b(rows_of(asc1_q, bwd), bwd))
        wout_f.append(times_c(rows_of(asc1_q, fwd), fwd))
        wout_b.append(times_c(column(desc1_0, e_desc, bwd), bwd))
        cols = slice(q * LANES, (q + 1) * LANES)
        wend_ref[gi, fwd, cols] = wend_f[q].astype(BF16)
        wend_ref[gi, bwd, cols] = wend_b[q].astype(BF16)

    meta_cols = N_META * SSM_GROUP // LANES
    x0 = sum(jnp.sum(wend_f[n_cols - meta_cols + m] * um_ref[gi, :, m * LANES:(m + 1) * LANES],
                     axis=1, keepdims=True) for m in range(meta_cols))

    lane = lax.broadcasted_iota(jnp.int32, (DIR_ROWS, LANES), 1)
    lag0_b = jnp.where(lane >= LANES - SSM_GROUP, bb_same[bwd], 0.0)
    zero = jnp.zeros((DIR_ROWS, LANES), F32)
    bcat = jnp.concatenate(
        [jnp.concatenate(wend_f + [zero] * n_cols, axis=1),
         jnp.concatenate([zero] * (n_cols - 1) + [lag0_b] + lag_b, axis=1)], axis=0)
    lane_r = lax.broadcasted_iota(jnp.int32, (SSM_GROUP, rows), 1)
    c2 = jnp.where((lane_r // SSM_STATE) % 2 == 0, cr_ref[gi], -cr_ref[gi])
    kk = jnp.dot(c2, bcat, precision=lax.Precision.HIGHEST,
                 preferred_element_type=F32)
    for tt in range(t):
        off = (t - 1 - tt) * SSM_GROUP
        toep_ref[gi, tt * SSM_GROUP:(tt + 1) * SSM_GROUP, :] = (
            kk[:, off:off + CHUNK_LANES].astype(BF16))

    wout = jnp.concatenate([jnp.concatenate(wout_f, axis=1),
                            jnp.concatenate(wout_b, axis=1)], axis=0)
    wout_ref[gi] = wout.T.astype(BF16)
    return pows[t], x0


SSM_GROUPS_PER_STEP = 4


def _chunk_scan(er, ei, ar, ai, x0, chunk, n_chunks, forward):
    width = er.shape[1]
    xr, xi = er, ei
    if x0 is not None:
        fr, fi = _cmul(ar, ai, x0[0], x0[1])
        xr = xr + jnp.where(chunk == 0, _lanes(fr, width), 0.0)
        xi = xi + jnp.where(chunk == 0, _lanes(fi, width), 0.0)

    def shifted(v, sh):
        if forward:
            return jnp.where(chunk >= sh, pltpu.roll(v, sh, axis=1), 0.0)
        return jnp.where(chunk < n_chunks - sh, pltpu.roll(v, width - sh, axis=1), 0.0)

    sh = 1
    while sh < n_chunks:
        sr, si = shifted(xr, sh), shifted(xi, sh)
        wr, wi = _lanes(ar, width), _lanes(ai, width)
        xr, xi = xr + wr * sr - wi * si, xi + wr * si + wi * sr
        ar, ai = _cmul(ar, ai, ar, ai)
        sh *= 2
    inr, ini = shifted(xr, 1), shifted(xi, 1)
    if x0 is not None:
        inr = jnp.where(chunk == 0, x0[0], inr)
        ini = jnp.where(chunk == 0, x0[1], ini)
    return inr, ini


def _ssm_kernel(u_ref, um_ref, csc_ref, cc_ref, bc_ref, cr_ref, y_ref, toep_ref, wend_ref,
                wout_ref, yi_ref, *, n_chunks):
    for gi in range(SSM_GROUPS_PER_STEP):
        (lam_r, lam_i), x0 = _build_chunk_ops(gi, csc_ref, cc_ref, bc_ref, cr_ref, um_ref,
                                              toep_ref, wend_ref, wout_ref)
        u = u_ref[gi]
        width = u.shape[1]
        e = jnp.dot(wend_ref[gi], u, preferred_element_type=F32)
        yi_ref[gi] = jnp.dot(toep_ref[gi], u, preferred_element_type=F32)
        chunk =lax.broadcasted_iota(jnp.int32, (1, width), 1) % n_chunks
        p = SSM_STATE
        ar, ai = lam_r, lam_i
        f_in = _chunk_scan(e[0:p], e[p:2 * p], ar[0:p], ai[0:p], (x0[0:p], x0[p:2 * p]),
                           chunk, n_chunks, True)
        b_in = _chunk_scan(e[2 * p:3 * p], e[3 * p:], ar[2 * p:3 * p], ai[2 * p:3 * p],
                           None, chunk, n_chunks, False)
        xin = jnp.concatenate([f_in[0], f_in[1], b_in[0], b_in[1]], axis=0).astype(BF16)
        y = yi_ref[gi] + jnp.dot(wout_ref[gi], xin, preferred_element_type=F32)
        y_ref[gi] = y.astype(BF16)


def _ssm(u_col, u_meta, csc, c_col, b_col, c_row, n_chunks):
    g, _, width = u_col.shape
    per = SSM_GROUPS_PER_STEP
    return pl.pallas_call(
        functools.partial(_ssm_kernel, n_chunks=n_chunks),
        grid=(g // per,),
        in_specs=[
            pl.BlockSpec((per, CHUNK_LANES, width), lambda i: (i, 0, 0)),
            pl.BlockSpec((per, 1, N_META * SSM_GROUP), lambda i: (i, 0, 0)),
            pl.BlockSpec((per, STATE_ROWS, 4), lambda i: (i, 0, 0)),
            pl.BlockSpec((per, STATE_ROWS, SSM_GROUP), lambda i: (i, 0, 0)),
            pl.BlockSpec((per, STATE_ROWS, SSM_GROUP), lambda i: (i, 0, 0)),
            pl.BlockSpec((per, SSM_GROUP, STATE_ROWS), lambda i: (i, 0, 0)),
        ],
        out_specs=pl.BlockSpec((per, CHUNK_LANES, width), lambda i: (i, 0, 0)),
        out_shape=jax.ShapeDtypeStruct((g, CHUNK_LANES, width), BF16),
        scratch_shapes=[
            pltpu.VMEM((per, CHUNK_LANES, CHUNK_LANES), BF16),
            pltpu.VMEM((per, STATE_ROWS, CHUNK_LANES), BF16),
            pltpu.VMEM((per, CHUNK_LANES, STATE_ROWS), BF16),
            pltpu.VMEM((per, CHUNK_LANES, width), F32),
        ],
        compiler_params=_params(),
        name="s5_mixer",
    )(u_col, u_meta, csc, c_col, b_col, c_row)


FF_CHUNK = 1024
MIX_PER = 4


MIX_SLOTS = 3


def _mix_ffn_kernel(x_hbm, co_ref, y_ref, ut_ref, d_ref, gwt_ref, gb_ref, wo_ref,
                    gf_ref, w1_ref, w2_ref, gl_ref, o_hbm, buf, sem_in, sem_out,
                    mixed_ref, h_ref, z_ref, a_ref):
    n_chunks = x_hbm.shape[1]
    rows = SUBLANES * n_chunks
    step = pl.program_id(0)
    n_tiles = pl.num_programs(0) - 1
    tile = step - 1

    def mix_stages(dst):
        per = MIX_PER
        n_half = SUBLANES // per
        state = {}

        def col(ref, k):
            return jnp.concatenate(
                [ref[:, r * SSM_GROUP:(r + 1) * SSM_GROUP, :].reshape(D_SSM, n_chunks)
                 for r in range(k * per, (k + 1) * per)], axis=1).astype(F32)

        def gelu_stage(k):
            def run():
                state["ge", k] = jax.nn.gelu(col(y_ref, k) + d_ref[...] * col(ut_ref, k))
            return run

        def glu_stage(k):
            def run():
                ge = state["ge", k]
                gate = (jnp.dot(gwt_ref[...], ge.astype(BF16), preferred_element_type=F32)
                        + gb_ref[...])
                state["s5t", k] = ge * jax.nn.sigmoid(gate)
            return run

        def store_stage(k):
            def run():
                rs = slice(k * per * n_chunks, (k + 1) * per * n_chunks)
                mixed_ref[dst, rs, :D_CONV] = co_ref[k * per:(k + 1) * per].reshape(
                    per * n_chunks, D_CONV)
                mixed_ref[dst, rs, D_CONV:] = state["s5t", k].T.astype(BF16)
            return run

        return [f(k) for f in (gelu_stage, glu_stage, store_stage) for k in range(n_half)]

    @pl.when(step == 0)
    def _():
        for cp in _step_copies(x_hbm, buf, sem_in, 0, 0, False):
            cp.start()
        for stage in mix_stages(0):
            stage()

    @pl.when(step >= 1)
    def _():
        slot = tile % MIX_SLOTS
        nxt = (tile + 1) % MIX_SLOTS

        @pl.when(tile >= 2)
        def _():
            for cp in _step_copies(o_hbm, buf, sem_out, tile - 2, nxt, True):
                cp.wait()

        @pl.when(tile + 1 < n_tiles)
        def _():
            for cp in _step_copies(x_hbm, buf, sem_in, tile + 1, nxt, False):
                cp.start()

        for cp in _step_copies(x_hbm, buf, sem_in, tile, slot, False):
            cp.wait()

        stages = mix_stages(step % 2)
        half = D_FF // 2
        n_slots = D_FF // FF_CHUNK + 2
        n_stages = len(stages)
        done = [0, 0]

        def fill_slot():
            done[0] += 1
            while done[1] * n_slots < done[0] * n_stages:
                stages.pop(0)()
                done[1] += 1

        h = buf[slot].reshape(rows, D_MODEL) + jnp.dot(
            mixed_ref[tile % 2], wo_ref[...], preferred_element_type=F32)
        z_ref[...] = _rms(h, gf_ref[...]).astype(BF16)
        h_ref[...] = h
        for k in range(2):
            for j in range(half // FF_CHUNK):
                cols = slice(k * half + j * FF_CHUNK, k * half + (j + 1) * FF_CHUNK)
                a = jnp.dot(z_ref[...], w1_ref[:, cols], preferred_element_type=F32)
                a_ref[:, j * FF_CHUNK:(j + 1) * FF_CHUNK] = (
                    jnp.square(jnp.maximum(a, 0.0)).astype(BF16))
                fill_slot()
            h_ref[...] += jnp.dot(a_ref[...], w2_ref[k * half:(k + 1) * half, :],
                                  preferred_element_type=F32)
            fill_slot()
        assert not stages
        buf[slot] = _rms(h_ref[...], gl_ref[...]).reshape(SUBLANES, n_chunks, D_MODEL)

        for cp in _step_copies(o_hbm, buf, sem_out, tile, slot, True):
            cp.start()

        @pl.when(tile == n_tiles - 1)
        def _():
            for cp in (_step_copies(o_hbm, buf, sem_out, tile - 1, (tile - 1) % MIX_SLOTS, True)
                       + _step_copies(o_hbm, buf, sem_out, tile, slot, True)):
                cp.wait()


def _mix_ffn(x4, conv_out, y_col, u_col, d_col, glu_wt, glu_b_col, w_out, g_ffn, w1, w2,
             g_final):
    bsz, n_chunks = x4.shape[:2]

    def const(shape):
        return pl.BlockSpec(shape, lambda t: (0, 0), pipeline_mode=pl.Buffered(1))

    n_tiles = CHUNK // SUBLANES * bsz
    hbm = pl.BlockSpec(memory_space=pl.ANY)

    def slab(s):
        return jnp.minimum(s, n_tiles - 1) // bsz

    def batch(s):
        return jnp.minimum(s, n_tiles - 1) % bsz

    col_blk = pl.BlockSpec((N_GROUPS, SUBLANES * SSM_GROUP, n_chunks),
                           lambda s: (0, slab(s), batch(s)))
    step_sems = pltpu.SemaphoreType.DMA((MIX_SLOTS, SUBLANES))
    return pl.pallas_call(
        _mix_ffn_kernel,
        grid=(n_tiles + 1,),
        in_specs=[
            hbm,
            pl.BlockSpec((SUBLANES, n_chunks, D_CONV), lambda s: (slab(s), batch(s), 0)),
            col_blk,
            col_blk,
            const((D_SSM, 1)),
            const((D_SSM, D_SSM)),
            const((D_SSM, 1)),
            const((D_MODEL, D_MODEL)),
            const((1, D_MODEL)),
            const((D_MODEL, D_FF)),
            const((D_FF, D_MODEL)),
            const((1, D_MODEL)),
        ],
        out_specs=hbm,
        out_shape=jax.ShapeDtypeStruct(x4.shape, F32),
        scratch_shapes=[
            pltpu.VMEM((MIX_SLOTS, SUBLANES, n_chunks, D_MODEL), F32),
            step_sems, step_sems,
            pltpu.VMEM((2, SUBLANES * n_chunks, D_MODEL), BF16),
            pltpu.VMEM((SUBLANES * n_chunks, D_MODEL), F32),
            pltpu.VMEM((SUBLANES * n_chunks, D_MODEL), BF16),
            pltpu.VMEM((SUBLANES * n_chunks, D_FF // 2), BF16),
        ],
        compiler_params=_step_tile_params(),
        name="mix_ffn",
    )(x4, conv_out, y_col, u_col, d_col, glu_wt, glu_b_col, w_out, g_ffn, w1, w2, g_final)


def _s5_operator_inputs(lbr, lbi, zr, zi, b_re, b_im, c_re, c_im):
    g, p, h = N_GROUPS, SSM_STATE, SSM_GROUP

    sc = jnp.transpose(jnp.stack([lbr, lbi, zr, zi], axis=-1), (1, 0, 2, 3))
    csc = jnp.broadcast_to(sc[:, :, None], (g, 2, 2, p, 4)).reshape(g, 4 * p, 4)
    c = jnp.stack([c_re, c_im], axis=0)
    b = jnp.stack([b_re, b_im], axis=0)
    c_col = jnp.transpose(c, (2, 1, 0, 4, 3)).reshape(g, 4 * p, h)
    b_col = jnp.transpose(b, (2, 1, 0, 3, 4)).reshape(g, 4 * p, h)
    c_row = jnp.transpose(c, (2, 3, 1, 0, 4)).reshape(g, h, 4 * p)
    return csc, c_col, b_col, c_row


def kernel(x, meta_tokens, norm_mix_g, w_in, conv_w, conv_b, conv_ln_g, conv_ln_b,
           ssm_lam_re, ssm_lam_im, ssm_log_dt, ssm_b_re, ssm_b_im, ssm_c_re, ssm_c_im,
           ssm_d, ssm_glu_w, ssm_glu_b, w_out, norm_ffn_g, w_ff1, w_ff2, norm_final_g):
    assert w_in.shape[0] == 1, "single-layer block"
    bsz, seq, _ = x.shape
    assert seq % CHUNK == 0 and CHUNK >= N_META
    n_chunks = seq // CHUNK
    g, h = N_GROUPS, SSM_GROUP

    x4 = x.reshape(bsz, n_chunks, CHUNK, D_MODEL)
    g_mix = norm_mix_g[0][None, :]
    u_conv, u_col, u_conv_m, u_ssm_m = _in_proj_step(x4, meta_tokens, g_mix, w_in[0])

    conv_out = _conv_module(u_conv, u_conv_m, conv_w[0], conv_b[0][None, :],
                            conv_ln_g[0][None, :], conv_ln_b[0][None, :], bsz)

    ldt = jnp.broadcast_to(ssm_log_dt[0][..., None], ssm_lam_re[0].shape)
    flat = lambda a: a.reshape(2 * g, SSM_STATE)
    lbr, lbi, zr, zi = [a.reshape(2, g, SSM_STATE) for a in
                        _zoh(flat(ssm_lam_re[0]), flat(ssm_lam_im[0]), flat(ldt))]
    csc, c_col, b_col, c_row = _s5_operator_inputs(
        lbr, lbi, zr, zi, ssm_b_re[0], ssm_b_im[0], ssm_c_re[0], ssm_c_im[0])

    u_meta = jnp.transpose(u_ssm_m.reshape(N_META, g, h), (1, 0, 2)).reshape(g, 1, N_META * h)
    y_col = _ssm(u_col, u_meta, csc, c_col, b_col, c_row, n_chunks)

    out = _mix_ffn(x4, conv_out, y_col, u_col, ssm_d[0][:, None],
                   ssm_glu_w[0].T.astype(BF16), ssm_glu_b[0][:, None],
                   w_out[0].astype(BF16), norm_ffn_g[0][None, :], w_ff1[0].astype(BF16),
                   w_ff2[0].astype(BF16), norm_final_g[None, :])
    return out.reshape(bsz, seq, D_MODEL)
```

```python
import functools

import jax
import jax.numpy as jnp
from jax import lax
from jax.experimental import pallas as pl
from jax.experimental.pallas import tpu as pltpu

F32 = jnp.float32
BF16 = jnp.bfloat16

D_MODEL = 1024
N_META = 16
D_CONV = 512
D_SSM = 512
CONV_WIDTH = 31
CONV_PAD = CONV_WIDTH // 2
SSM_GROUP = 16
N_GROUPS = D_SSM // SSM_GROUP
SSM_STATE = 64
D_FF = 4096
NORM_EPS = 1e-5
LANES = 128
SUBLANES = 8

CHUNK = 32
CHUNK_LANES = CHUNK * SSM_GROUP
STATE_ROWS = 4 * SSM_STATE
DIR_ROWS = 2 * SSM_STATE
STEPS_PER_VREG = LANES // SSM_GROUP

VMEM_LIMIT_BYTES = 60 * 1024 * 1024


def _rms(x, g):
    return x * lax.rsqrt(jnp.mean(x * x, axis=-1, keepdims=True) + NORM_EPS) * g


def _params(n_axes=1):
    return pltpu.CompilerParams(dimension_semantics=("parallel",) * n_axes,
                                vmem_limit_bytes=VMEM_LIMIT_BYTES)


def _step_copies(hbm_ref, buf_ref, sem_ref, tile, slot, to_hbm):
    bsz = hbm_ref.shape[0]
    slab, b = tile // bsz, tile % bsz
    copies = []
    for r in range(SUBLANES):
        hbm = hbm_ref.at[b, :, slab * SUBLANES + r, :]
        vmem = buf_ref.at[slot, r]
        src, dst = (vmem, hbm) if to_hbm else (hbm, vmem)
        copies.append(pltpu.make_async_copy(src, dst, sem_ref.at[slot, r]))
    return copies


def _fetch_steps(x_hbm, xbuf, sem):
    tile = pl.program_id(0)
    slot = tile % 2

    @pl.when(tile == 0)
    def _():
        for cp in _step_copies(x_hbm, xbuf, sem, tile, slot, False):
            cp.start()

    @pl.when(tile + 1 < pl.num_programs(0))
    def _():
        for cp in _step_copies(x_hbm, xbuf, sem, tile + 1, 1 - slot, False):
            cp.start()

    for cp in _step_copies(x_hbm, xbuf, sem, tile, slot, False):
        cp.wait()
    return slot


def _step_tile_params():
    return pltpu.CompilerParams(dimension_semantics=("arbitrary",),
                                vmem_limit_bytes=VMEM_LIMIT_BYTES)


def _in_proj_step_kernel(x_hbm, meta_ref, g_ref, w32_ref, uc_ref, ut_ref, ucm_ref, usm_ref,
                         xbuf, sem, w_ref):
    n_chunks = x_hbm.shape[1]
    slot = _fetch_steps(x_hbm, xbuf, sem)

    @pl.when(pl.program_id(0) == 0)
    def _():
        w_ref[...] = w32_ref[...].astype(BF16)
        zm = _rms(meta_ref[...], g_ref[...]).astype(BF16)
        pm = jnp.dot(zm, w_ref[...], preferred_element_type=F32)
        ucm_ref[...] = pm[:, :D_CONV] * jax.nn.sigmoid(pm[:, D_CONV:2 * D_CONV])
        usm_ref[...] = pm[:, 2 * D_CONV:]

    x = xbuf[slot].reshape(SUBLANES * n_chunks, D_MODEL)
    z = _rms(x, g_ref[...]).astype(BF16)
    p = jnp.dot(z, w_ref[...], preferred_element_type=F32)
    uc = p[:, :D_CONV] * jax.nn.sigmoid(p[:, D_CONV:2 * D_CONV])
    uc_ref[...] = uc.reshape(SUBLANES, n_chunks, D_CONV)
    ut = p[:, 2 * D_CONV:].T.astype(BF16)
    for r in range(SUBLANES):
        ut_ref[:, r * SSM_GROUP:(r + 1) * SSM_GROUP, :] = (
            ut[:, r * n_chunks:(r + 1) * n_chunks].reshape(N_GROUPS, SSM_GROUP, n_chunks))


def _in_proj_step(x4, meta, g, w_in):
    bsz, n_chunks = x4.shape[:2]
    rows = bsz * n_chunks
    return pl.pallas_call(
        _in_proj_step_kernel,
        grid=(CHUNK // SUBLANES * bsz,),
        in_specs=[
            pl.BlockSpec(memory_space=pl.ANY),
            pl.BlockSpec((N_META, D_MODEL), lambda t: (0, 0)),
            pl.BlockSpec((1, D_MODEL), lambda t: (0, 0)),
            pl.BlockSpec((D_MODEL, 2 * D_CONV + D_SSM), lambda t: (0, 0),
                         pipeline_mode=pl.Buffered(1)),
        ],
        out_specs=[
            pl.BlockSpec((SUBLANES, n_chunks, D_CONV), lambda t: (t // bsz, t % bsz, 0)),
            pl.BlockSpec((N_GROUPS, SUBLANES * SSM_GROUP, n_chunks),
                         lambda t: (0, t // bsz, t % bsz)),
            pl.BlockSpec((N_META, D_CONV), lambda t: (0, 0)),
            pl.BlockSpec((N_META, D_SSM), lambda t: (0, 0)),
        ],
        out_shape=[
            jax.ShapeDtypeStruct((CHUNK, rows, D_CONV), F32),
            jax.ShapeDtypeStruct((N_GROUPS, CHUNK_LANES, rows), BF16),
            jax.ShapeDtypeStruct((N_META, D_CONV), F32),
            jax.ShapeDtypeStruct((N_META, D_SSM), F32),
        ],
        scratch_shapes=[pltpu.VMEM((2, SUBLANES, n_chunks, D_MODEL), F32),
                        pltpu.SemaphoreType.DMA((2, SUBLANES)),
                        pltpu.VMEM((D_MODEL, 2 * D_CONV + D_SSM), BF16)],
        compiler_params=_step_tile_params(),
        name="in_proj",
    )(x4, meta, g, w_in)


CONV_GROUP = 4
CONV_ROWS = 32


def _conv_kernel(u_ref, um_ref, w_ref, cb_ref, lg_ref, lb_ref, o_ref, cat_ref, acc_ref,
                 ws_ref, carry_ref):
    n_steps, n_chunks, _ = u_ref.shape
    cat_ref[CONV_PAD:CONV_PAD + n_steps] = u_ref[...]
    chunk = lax.broadcasted_iota(jnp.int32, (n_chunks, 1), 0)
    for i in range(CONV_PAD):
        s_prev = n_steps - CONV_PAD + i
        prev = pltpu.roll(u_ref[s_prev], 1, axis=0)
        meta_row = um_ref[N_META - CONV_PAD + i:N_META - CONV_PAD + i + 1, :]
        cat_ref[i] = jnp.where(chunk == 0, meta_row, prev)
        nxt = pltpu.roll(u_ref[i], n_chunks - 1, axis=0)
        cat_ref[n_steps + CONV_PAD + i] = jnp.where(chunk == n_chunks - 1, 0.0, nxt)

    n_half = (CONV_WIDTH + 1) // 2
    cat_ref[n_steps + 2 * CONV_PAD] = jnp.zeros((n_chunks, D_CONV), F32)
    for i in range(n_half):
        odd_tap = w_ref[2 * i + 1:2 * i + 2, :] if 2 * i + 1 < CONV_WIDTH else 0.0
        ws_ref[i:i + 1, :] = w_ref[2 * i:2 * i + 1, :] + odd_tap
    a_first = jnp.zeros((n_chunks, D_CONV), F32)
    for i in range(n_half):
        a_first = a_first + cat_ref[2 * i] * w_ref[2 * i:2 * i + 1, :]
    carry_ref[...] = a_first

    def body(sg, carry):
        s0 = sg * CONV_GROUP
        for c in range(D_CONV // LANES):
            cols = slice(c * LANES, (c + 1) * LANES)
            n_sub = CONV_ROWS // SUBLANES
            pairs = CONV_GROUP // 2

            def tap(ref, k):
                return jnp.broadcast_to(ref[k:k + 1, cols], (SUBLANES, LANES))

            def rows_body(rt, carry2):
                r0 = [pl.multiple_of(rt * CONV_ROWS + sub * SUBLANES, SUBLANES)
                      for sub in range(n_sub)]
                zero = jnp.zeros((SUBLANES, LANES), F32)
                acc_a = [[zero] * pairs for _ in range(n_sub)]
                acc_b = [[zero] * pairs for _ in range(n_sub)]
                acc_p = [[zero] * pairs for _ in range(n_sub)]
                for j in range(n_half + pairs - 1):
                    odd = [cat_ref[s0 + 2 * j + 1, pl.ds(r0[sub], SUBLANES), cols]
                           for sub in range(n_sub)]
                    even = [cat_ref[s0 + 2 * j + 2, pl.ds(r0[sub], SUBLANES), cols]
                            for sub in range(n_sub)]
                    both = [odd[sub] + even[sub] for sub in range(n_sub)]
                    for m in range(pairs):
                        i = j - m
                        if not 0 <= i < n_half:
                            continue
                        t_a, t_p = tap(w_ref, 2 * i), tap(ws_ref, i)
                        for sub in range(n_sub):
                            acc_a[sub][m] = acc_a[sub][m] + even[sub] * t_a
                            acc_p[sub][m] = acc_p[sub][m] + both[sub] * t_p
                        if 2 * i + 1 < CONV_WIDTH:
                            t_b = tap(w_ref, 2 * i + 1)
                            for sub in range(n_sub):
                                acc_b[sub][m] = acc_b[sub][m] + odd[sub] * t_b
                for sub in range(n_sub):
                    rows = pl.ds(r0[sub], SUBLANES)
                    a_prev = carry_ref[rows, cols]
                    for m in range(pairs):
                        acc_ref[2 * m, rows, cols] = a_prev + acc_b[sub][m]
                        acc_ref[2 * m + 1, rows, cols] = (
                            acc_p[sub][m] - acc_a[sub][m] - acc_b[sub][m])
                        a_prev = acc_a[sub][m]
                    carry_ref[rows, cols] = a_prev
                return carry2

            lax.fori_loop(0, n_chunks // CONV_ROWS, rows_body, 0)
        for j in range(CONV_GROUP):
            y = acc_ref[j] + cb_ref[...]
            yc = y - jnp.mean(y, axis=-1, keepdims=True)
            yn = yc * lax.rsqrt(jnp.mean(yc * yc, axis=-1, keepdims=True) + NORM_EPS)
            yn = yn * lg_ref[...] + lb_ref[...]
            o_ref[s0 + j] = (yn * jax.nn.sigmoid(yn)).astype(BF16)
        return carry

    lax.fori_loop(0, n_steps // CONV_GROUP, body, 0)


def _conv_module(u_conv, u_conv_meta, conv_w, conv_b, ln_g, ln_b, bsz):
    n_steps, rows, _ = u_conv.shape
    n_chunks = rows // bsz
    vec = pl.BlockSpec((1, D_CONV), lambda b: (0, 0))
    blk = pl.BlockSpec((n_steps, n_chunks, D_CONV), lambda b: (0, b, 0))
    return pl.pallas_call(
        _conv_kernel,
        grid=(bsz,),
        in_specs=[
            blk,
            pl.BlockSpec((N_META, D_CONV), lambda b: (0, 0)),
            pl.BlockSpec((CONV_WIDTH, D_CONV), lambda b: (0, 0)),
            vec, vec, vec,
        ],
        out_specs=blk,
        out_shape=jax.ShapeDtypeStruct((n_steps, rows, D_CONV), BF16),
        scratch_shapes=[pltpu.VMEM((n_steps + 2 * CONV_PAD + 1, n_chunks, D_CONV), F32),
                        pltpu.VMEM((CONV_GROUP, n_chunks, D_CONV), F32),
                        pltpu.VMEM(((CONV_WIDTH + 1) // 2, D_CONV), F32),
                        pltpu.VMEM((n_chunks, D_CONV), F32)],

        compiler_params=_params(),
        name="conv_module",
    )(u_conv, u_conv_meta, conv_w, conv_b, ln_g, ln_b)


def _zoh_kernel(lre_ref, lim_ref, ldt_ref, lbr_ref, lbi_ref, zr_ref, zi_ref):
    lre = lre_ref[...]
    lim = lim_ref[...]
    dt = jnp.exp(ldt_ref[...])
    ea = jnp.exp(lre * dt)
    lbr = ea * jnp.cos(lim * dt)
    lbi = ea * jnp.sin(lim * dt)
    nr = lbr - 1.0
    den = lre * lre + lim * lim
    lbr_ref[...] = lbr
    lbi_ref[...] = lbi
    zr_ref[...] = (nr * lre + lbi * lim) / den
    zi_ref[...] = (lbi * lre - nr * lim) / den


def _zoh(lam_re, lam_im, log_dt):
    shape = jax.ShapeDtypeStruct(lam_re.shape, F32)
    return pl.pallas_call(_zoh_kernel, out_shape=[shape] * 4, name="s5_zoh")(
        lam_re, lam_im, log_dt)


def _lanes(x, n):
    return jnp.concatenate([x] * (n // LANES), axis=1)


def _cmul(ar, ai, br, bi):
    return ar * br - ai * bi, ar * bi + ai * br


def _cpow(br, bi, expo, nbits):
    rr = jnp.ones(expo.shape, F32)
    ri = jnp.zeros(expo.shape, F32)
    for k in range(nbits):
        bit = ((expo >> k) & 1) == 1
        nr, ni = _cmul(rr, ri, br, bi)
        rr = jnp.where(bit, nr, rr)
        ri = jnp.where(bit, ni, ri)
        br, bi = _cmul(br, bi, br, bi)
    return rr, ri


def _build_chunk_ops(gi, csc_ref, cc_ref, bc_ref, cr_ref, um_ref, toep_ref, wend_ref,
                     wout_ref):
    t = CHUNK
    rows = STATE_ROWS
    n_cols = CHUNK_LANES // LANES
    lam = (jnp.broadcast_to(csc_ref[gi, :, 0:1], (rows, LANES)),
           jnp.broadcast_to(csc_ref[gi, :, 1:2], (rows, LANES)))
    zr = jnp.broadcast_to(csc_ref[gi, :, 2:3], (rows, LANES))
    zi = jnp.broadcast_to(csc_ref[gi, :, 3:4], (rows, LANES))
    sel = (lax.broadcasted_iota(jnp.int32, (SSM_GROUP, LANES), 1) % SSM_GROUP
           == lax.broadcasted_iota(jnp.int32, (SSM_GROUP, LANES), 0)).astype(F32)

    def tile(ref):
        return jnp.dot(ref[gi], sel, precision=lax.Precision.HIGHEST,
                       preferred_element_type=F32)

    def swap_re_im(a):
        p = SSM_STATE
        return jnp.concatenate([a[p:2 * p], a[0:p], a[3 * p:], a[2 * p:3 * p]], axis=0)

    c_same, b_same = tile(cc_ref), tile(bc_ref)
    c_swap, b_swap = swap_re_im(c_same), swap_re_im(b_same)
    row = lax.broadcasted_iota(jnp.int32, (rows, LANES), 0)
    is_re = (row // SSM_STATE) % 2 == 0
    sgn = jnp.where(is_re, -1.0, 1.0)
    ca = jnp.where(is_re, c_same, -c_same)
    cb = -c_swap
    bb_same = zr * b_same + sgn * zi * b_swap
    sbb_swap = sgn * (zr * b_swap - sgn * zi * b_same)

    pows = {1: lam}
    k = 1
    while k < t:
        pows[2 * k] = _cmul(*pows[k], *pows[k])
        k *= 2
    step_bits = STEPS_PER_VREG.bit_length() - 1
    fwd = slice(0, DIR_ROWS)
    bwd = slice(DIR_ROWS, rows)
    every = slice(0, rows)

    def rows_of(v, rs):
        return v[0][rs], v[1][rs]

    def column(base, e, rs):
        out = rows_of(base, rs)
        for bit, val in pows.items():
            if e & bit:
                out = _cmul(*out, *rows_of(val, rs))
        return out

    def times_b(p, rs):
        return p[0] * bb_same[rs] + p[1] * sbb_swap[rs]

    def times_c(p, rs):
        return ca[rs] * p[0] + cb[rs] * p[1]

    i8 = lax.broadcasted_iota(jnp.int32, (rows, LANES), 1) // SSM_GROUP
    asc0 = _cpow(*lam, i8, step_bits)
    desc0 = _cpow(*lam, STEPS_PER_VREG - 1 - i8, step_bits)
    asc1_0 = _cmul(*asc0, *lam)
    desc1_0 = _cmul(*desc0, *lam)

    wend_f, wend_b, lag_b, wout_f, wout_b = [], [], [], [], []
    for q in range(n_cols):
        e_asc = STEPS_PER_VREG * q
        e_desc = STEPS_PER_VREG * (n_cols - 1 - q)
        asc1_q = column(asc1_0, e_asc, every)
        wend_f.append(times_b(column(desc0, e_desc, fwd), fwd))
        wend_b.append(times_b(column(asc0, e_asc, bwd), bwd))
        lag_b.append(times_b(rows_of(asc1_q, bwd), bwd))
        wout_f.append(times_c(rows_of(asc1_q, fwd), fwd))
        wout_b.append(times_c(column(desc1_0, e_desc, bwd), bwd))
        cols = slice(q * LANES, (q + 1) * LANES)
        wend_ref[gi, fwd, cols] = wend_f[q].astype(BF16)
        wend_ref[gi, bwd, cols] = wend_b[q].astype(BF16)

    meta_cols = N_META * SSM_GROUP // LANES
    x0 = sum(jnp.sum(wend_f[n_cols - meta_cols + m] * um_ref[gi, :, m * LANES:(m + 1) * LANES],
                     axis=1, keepdims=True) for m in range(meta_cols))

    lane = lax.broadcasted_iota(jnp.int32, (DIR_ROWS, LANES), 1)
    lag0_b = jnp.where(lane >= LANES - SSM_GROUP, bb_same[bwd], 0.0)
    zero = jnp.zeros((DIR_ROWS, LANES), F32)
    bcat = jnp.concatenate(
        [jnp.concatenate(wend_f + [zero] * n_cols, axis=1),
         jnp.concatenate([zero] * (n_cols - 1) + [lag0_b] + lag_b, axis=1)], axis=0)
    lane_r = lax.broadcasted_iota(jnp.int32, (SSM_GROUP, rows), 1)
    c2 = jnp.where((lane_r // SSM_STATE) % 2 == 0, cr_ref[gi], -cr_ref[gi])
    kk = jnp.dot(c2, bcat, precision=lax.Precision.HIGHEST,
                 preferred_element_type=F32)
    for tt in range(t):
        off = (t - 1 - tt) * SSM_GROUP
        toep_ref[gi, tt * SSM_GROUP:(tt + 1) * SSM_GROUP, :] = (
            kk[:, off:off + CHUNK_LANES].astype(BF16))

    wout = jnp.concatenate([jnp.concatenate(wout_f, axis=1),
                            jnp.concatenate(wout_b, axis=1)], axis=0)
    wout_ref[gi] = wout.T.astype(BF16)
    return pows[t], x0


SSM_GROUPS_PER_STEP = 2


def _chunk_scan(er, ei, ar, ai, x0, chunk, n_chunks, forward):
    width = er.shape[1]
    xr, xi = er, ei
    if x0 is not None:
        fr, fi = _cmul(ar, ai, x0[0], x0[1])
        xr = xr + jnp.where(chunk == 0, _lanes(fr, width), 0.0)
        xi = xi + jnp.where(chunk == 0, _lanes(fi, width), 0.0)

    def shifted(v, sh):
        if forward:
            return jnp.where(chunk >= sh, pltpu.roll(v, sh, axis=1), 0.0)
        return jnp.where(chunk < n_chunks - sh, pltpu.roll(v, width - sh, axis=1), 0.0)

    sh = 1
    while sh < n_chunks:
        sr, si = shifted(xr, sh), shifted(xi, sh)
        wr, wi = _lanes(ar, width), _lanes(ai, width)
        xr, xi = xr + wr * sr - wi * si, xi + wr * si + wi * sr
        ar, ai = _cmul(ar, ai, ar, ai)
        sh *= 2
    inr, ini = shifted(xr, 1), shifted(xi, 1)
    if x0 is not None:
        inr = jnp.where(chunk == 0, x0[0], inr)
        ini = jnp.where(chunk == 0, x0[1], ini)
    return inr, ini


def _ssm_kernel(u_ref, um_ref, csc_ref, cc_ref, bc_ref, cr_ref, y_ref, toep_ref, wend_ref,
                wout_ref, yi_ref, *, n_chunks):
    for gi in range(SSM_GROUPS_PER_STEP):
        (lam_r, lam_i), x0 = _build_chunk_ops(gi, csc_ref, cc_ref, bc_ref, cr_ref, um_ref,
                                              toep_ref, wend_ref, wout_ref)
        u = u_ref[gi]
        width = u.shape[1]
        e = jnp.dot(wend_ref[gi], u, preferred_element_type=F32)
        yi_ref[gi] = jnp.dot(toep_ref[gi], u, preferred_element_type=F32)
        chunk =lax.broadcasted_iota(jnp.int32, (1, width), 1) % n_chunks
        p = SSM_STATE
        ar, ai = lam_r, lam_i
        f_in = _chunk_scan(e[0:p], e[p:2 * p], ar[0:p], ai[0:p], (x0[0:p], x0[p:2 * p]),
                           chunk, n_chunks, True)
        b_in = _chunk_scan(e[2 * p:3 * p], e[3 * p:], ar[2 * p:3 * p], ai[2 * p:3 * p],
                           None, chunk, n_chunks, False)
        xin = jnp.concatenate([f_in[0], f_in[1], b_in[0], b_in[1]], axis=0).astype(BF16)
        y = yi_ref[gi] + jnp.dot(wout_ref[gi], xin, preferred_element_type=F32)
        y_ref[gi] = y.astype(BF16)


def _ssm(u_col, u_meta, csc, c_col, b_col, c_row, n_chunks):
    g, _, width = u_col.shape
    per = SSM_GROUPS_PER_STEP
    return pl.pallas_call(
        functools.partial(_ssm_kernel, n_chunks=n_chunks),
        grid=(g // per,),
        in_specs=[
            pl.BlockSpec((per, CHUNK_LANES, width), lambda i: (i, 0, 0)),
            pl.BlockSpec((per, 1, N_META * SSM_GROUP), lambda i: (i, 0, 0)),
            pl.BlockSpec((per, STATE_ROWS, 4), lambda i: (i, 0, 0)),
            pl.BlockSpec((per, STATE_ROWS, SSM_GROUP), lambda i: (i, 0, 0)),
            pl.BlockSpec((per, STATE_ROWS, SSM_GROUP), lambda i: (i, 0, 0)),
            pl.BlockSpec((per, SSM_GROUP, STATE_ROWS), lambda i: (i, 0, 0)),
        ],
        out_specs=pl.BlockSpec((per, CHUNK_LANES, width), lambda i: (i, 0, 0)),
        out_shape=jax.ShapeDtypeStruct((g, CHUNK_LANES, width), BF16),
        scratch_shapes=[
            pltpu.VMEM((per, CHUNK_LANES, CHUNK_LANES), BF16),
            pltpu.VMEM((per, STATE_ROWS, CHUNK_LANES), BF16),
            pltpu.VMEM((per, CHUNK_LANES, STATE_ROWS), BF16),
            pltpu.VMEM((per, CHUNK_LANES, width), F32),
        ],
        compiler_params=_params(),
        name="s5_mixer",
    )(u_col, u_meta, csc, c_col, b_col, c_row)


FF_CHUNK = 1024
MIX_PER = 4


MIX_SLOTS = 3


def _mix_ffn_kernel(x_hbm, co_ref, y_ref, ut_ref, d_ref, gwt_ref, gb_ref, wo_ref,
                    gf_ref, w1_ref, w2_ref, gl_ref, o_hbm, buf, sem_in, sem_out,
                    mixed_ref, h_ref, z_ref, a_ref):
    n_chunks = x_hbm.shape[1]
    rows = SUBLANES * n_chunks
    step = pl.program_id(0)
    n_tiles = pl.num_programs(0) - 1
    tile = step - 1

    def mix_stages(dst):
        per = MIX_PER
        n_half = SUBLANES // per
        state = {}

        def col(ref, k):
            return jnp.concatenate(
                [ref[:, r * SSM_GROUP:(r + 1) * SSM_GROUP, :].reshape(D_SSM, n_chunks)
                 for r in range(k * per, (k + 1) * per)], axis=1).astype(F32)

        def gelu_stage(k):
            def run():
                state["ge", k] = jax.nn.gelu(col(y_ref, k) + d_ref[...] * col(ut_ref, k))
            return run

        def glu_stage(k):
            def run():
                ge = state["ge", k]
                gate = (jnp.dot(gwt_ref[...], ge.astype(BF16), preferred_element_type=F32)
                        + gb_ref[...])
                state["s5t", k] = ge * jax.nn.sigmoid(gate)
            return run

        def store_stage(k):
            def run():
                rs = slice(k * per * n_chunks, (k + 1) * per * n_chunks)
                mixed_ref[dst, rs, :D_CONV] = co_ref[k * per:(k + 1) * per].reshape(
                    per * n_chunks, D_CONV)
                mixed_ref[dst, rs, D_CONV:] = state["s5t", k].T.astype(BF16)
            return run

        return [f(k) for f in (gelu_stage, glu_stage, store_stage) for k in range(n_half)]

    @pl.when(step == 0)
    def _():
        for cp in _step_copies(x_hbm, buf, sem_in, 0, 0, False):
            cp.start()
        for stage in mix_stages(0):
            stage()

    @pl.when(step >= 1)
    def _():
        slot = tile % MIX_SLOTS
        nxt = (tile + 1) % MIX_SLOTS

        @pl.when(tile >= 2)
        def _():
            for cp in _step_copies(o_hbm, buf, sem_out, tile - 2, nxt, True):
                cp.wait()

        @pl.when(tile + 1 < n_tiles)
        def _():
            for cp in _step_copies(x_hbm, buf, sem_in, tile + 1, nxt, False):
                cp.start()

        for cp in _step_copies(x_hbm, buf, sem_in, tile, slot, False):
            cp.wait()

        stages = mix_stages(step % 2)
        half = D_FF // 2
        n_slots = D_FF // FF_CHUNK + 2
        n_stages = len(stages)
        done = [0, 0]

        def fill_slot():
            done[0] += 1
            while done[1] * n_slots < done[0] * n_stages:
                stages.pop(0)()
                done[1] += 1

        h = buf[slot].reshape(rows, D_MODEL) + jnp.dot(
            mixed_ref[tile % 2], wo_ref[...], preferred_element_type=F32)
        z_ref[...] = _rms(h, gf_ref[...]).astype(BF16)
        h_ref[...] = h
        for k in range(2):
            for j in range(half // FF_CHUNK):
                cols = slice(k * half + j * FF_CHUNK, k * half + (j + 1) * FF_CHUNK)
                a = jnp.dot(z_ref[...], w1_ref[:, cols], preferred_element_type=F32)
                a_ref[:, j * FF_CHUNK:(j + 1) * FF_CHUNK] = (
                    jnp.square(jnp.maximum(a, 0.0)).astype(BF16))
                fill_slot()
            h_ref[...] += jnp.dot(a_ref[...], w2_ref[k * half:(k + 1) * half, :],
                                  preferred_element_type=F32)
            fill_slot()
        assert not stages
        buf[slot] = _rms(h_ref[...], gl_ref[...]).reshape(SUBLANES, n_chunks, D_MODEL)

        for cp in _step_copies(o_hbm, buf, sem_out, tile, slot, True):
            cp.start()

        @pl.when(tile == n_tiles - 1)
        def _():
            for cp in (_step_copies(o_hbm, buf, sem_out, tile - 1, (tile - 1) % MIX_SLOTS, True)
                       + _step_copies(o_hbm, buf, sem_out, tile, slot, True)):
                cp.wait()


def _mix_ffn(x4, conv_out, y_col, u_col, d_col, glu_wt, glu_b_col, w_out, g_ffn, w1, w2,
             g_final):
    bsz, n_chunks = x4.shape[:2]

    def const(shape):
        return pl.BlockSpec(shape, lambda t: (0, 0), pipeline_mode=pl.Buffered(1))

    n_tiles = CHUNK // SUBLANES * bsz
    hbm = pl.BlockSpec(memory_space=pl.ANY)

    def slab(s):
        return jnp.minimum(s, n_tiles - 1) // bsz

    def batch(s):
        return jnp.minimum(s, n_tiles - 1) % bsz

    col_blk = pl.BlockSpec((N_GROUPS, SUBLANES * SSM_GROUP, n_chunks),
                           lambda s: (0, slab(s), batch(s)))
    step_sems = pltpu.SemaphoreType.DMA((MIX_SLOTS, SUBLANES))
    return pl.pallas_call(
        _mix_ffn_kernel,
        grid=(n_tiles + 1,),
        in_specs=[
            hbm,
            pl.BlockSpec((SUBLANES, n_chunks, D_CONV), lambda s: (slab(s), batch(s), 0)),
            col_blk,
            col_blk,
            const((D_SSM, 1)),
            const((D_SSM, D_SSM)),
            const((D_SSM, 1)),
            const((D_MODEL, D_MODEL)),
            const((1, D_MODEL)),
            const((D_MODEL, D_FF)),
            const((D_FF, D_MODEL)),
            const((1, D_MODEL)),
        ],
        out_specs=hbm,
        out_shape=jax.ShapeDtypeStruct(x4.shape, F32),
        scratch_shapes=[
            pltpu.VMEM((MIX_SLOTS, SUBLANES, n_chunks, D_MODEL), F32),
            step_sems, step_sems,
            pltpu.VMEM((2, SUBLANES * n_chunks, D_MODEL), BF16),
            pltpu.VMEM((SUBLANES * n_chunks, D_MODEL), F32),
            pltpu.VMEM((SUBLANES * n_chunks, D_MODEL), BF16),
            pltpu.VMEM((SUBLANES * n_chunks, D_FF // 2), BF16),
        ],
        compiler_params=_step_tile_params(),
        name="mix_ffn",
    )(x4, conv_out, y_col, u_col, d_col, glu_wt, glu_b_col, w_out, g_ffn, w1, w2, g_final)


def _s5_operator_inputs(lbr, lbi, zr, zi, b_re, b_im, c_re, c_im):
    g, p, h = N_GROUPS, SSM_STATE, SSM_GROUP

    sc = jnp.transpose(jnp.stack([lbr, lbi, zr, zi], axis=-1), (1, 0, 2, 3))
    csc = jnp.broadcast_to(sc[:, :, None], (g, 2, 2, p, 4)).reshape(g, 4 * p, 4)
    c = jnp.stack([c_re, c_im], axis=0)
    b = jnp.stack([b_re, b_im], axis=0)
    c_col = jnp.transpose(c, (2, 1, 0, 4, 3)).reshape(g, 4 * p, h)
    b_col = jnp.transpose(b, (2, 1, 0, 3, 4)).reshape(g, 4 * p, h)
    c_row = jnp.transpose(c, (2, 3, 1, 0, 4)).reshape(g, h, 4 * p)
    return csc, c_col, b_col, c_row


def kernel(x, meta_tokens, norm_mix_g, w_in, conv_w, conv_b, conv_ln_g, conv_ln_b,
           ssm_lam_re, ssm_lam_im, ssm_log_dt, ssm_b_re, ssm_b_im, ssm_c_re, ssm_c_im,
           ssm_d, ssm_glu_w, ssm_glu_b, w_out, norm_ffn_g, w_ff1, w_ff2, norm_final_g):
    assert w_in.shape[0] == 1, "single-layer block"
    bsz, seq, _ = x.shape
    assert seq % CHUNK == 0 and CHUNK >= N_META
    n_chunks = seq // CHUNK
    g, h = N_GROUPS, SSM_GROUP

    x4 = x.reshape(bsz, n_chunks, CHUNK, D_MODEL)
    g_mix = norm_mix_g[0][None, :]
    u_conv, u_col, u_conv_m, u_ssm_m = _in_proj_step(x4, meta_tokens, g_mix, w_in[0])

    conv_out = _conv_module(u_conv, u_conv_m, conv_w[0], conv_b[0][None, :],
                            conv_ln_g[0][None, :], conv_ln_b[0][None, :], bsz)

    ldt = jnp.broadcast_to(ssm_log_dt[0][..., None], ssm_lam_re[0].shape)
    flat = lambda a: a.reshape(2 * g, SSM_STATE)
    lbr, lbi, zr, zi = [a.reshape(2, g, SSM_STATE) for a in
                        _zoh(flat(ssm_lam_re[0]), flat(ssm_lam_im[0]), flat(ldt))]
    csc, c_col, b_col, c_row = _s5_operator_inputs(
        lbr, lbi, zr, zi, ssm_b_re[0], ssm_b_im[0], ssm_c_re[0], ssm_c_im[0])

    u_meta = jnp.transpose(u_ssm_m.reshape(N_META, g, h), (1, 0, 2)).reshape(g, 1, N_META * h)
    y_col = _ssm(u_col, u_meta, csc, c_col, b_col, c_row, n_chunks)

    out = _mix_ffn(x4, conv_out, y_col, u_col, ssm_d[0][:, None],
                   ssm_glu_w[0].T.astype(BF16), ssm_glu_b[0][:, None],
                   w_out[0].astype(BF16), norm_ffn_g[0][None, :], w_ff1[0].astype(BF16),
                   w_ff2[0].astype(BF16), norm_final_g[None, :])
    return out.reshape(bsz, seq, D_MODEL)
```

```python
import functools

import jax
import jax.numpy as jnp
from jax import lax
from jax.experimental import pallas as pl
from jax.experimental.pallas import tpu as pltpu

F32 = jnp.float32
BF16 = jnp.bfloat16

D_MODEL = 1024
N_META = 16
D_CONV = 512
D_SSM = 512
CONV_WIDTH = 31
CONV_PAD = CONV_WIDTH // 2
SSM_GROUP = 16
N_GROUPS = D_SSM // SSM_GROUP
SSM_STATE = 64
D_FF = 4096
NORM_EPS = 1e-5
LANES = 128
SUBLANES = 8

CHUNK = 32
CHUNK_LANES = CHUNK * SSM_GROUP
STATE_ROWS = 4 * SSM_STATE
DIR_ROWS = 2 * SSM_STATE
STEPS_PER_VREG = LANES // SSM_GROUP

VMEM_LIMIT_BYTES = 60 * 1024 * 1024


def _rms(x, g):
    return x * lax.rsqrt(jnp.mean(x * x, axis=-1, keepdims=True) + NORM_EPS) * g


def _params(n_axes=1):
    return pltpu.CompilerParams(dimension_semantics=("parallel",) * n_axes,
                                vmem_limit_bytes=VMEM_LIMIT_BYTES)


def _step_copies(hbm_ref, buf_ref, sem_ref, tile, slot, to_hbm):
    bsz = hbm_ref.shape[0]
    slab, b = tile // bsz, tile % bsz
    copies = []
    for r in range(SUBLANES):
        hbm = hbm_ref.at[b, :, slab * SUBLANES + r, :]
        vmem = buf_ref.at[slot, r]
        src, dst = (vmem, hbm) if to_hbm else (hbm, vmem)
        copies.append(pltpu.make_async_copy(src, dst, sem_ref.at[slot, r]))
    return copies


def _fetch_steps(x_hbm, xbuf, sem):
    tile = pl.program_id(0)
    slot = tile % 2

    @pl.when(tile == 0)
    def _():
        for cp in _step_copies(x_hbm, xbuf, sem, tile, slot, False):
            cp.start()

    @pl.when(tile + 1 < pl.num_programs(0))
    def _():
        for cp in _step_copies(x_hbm, xbuf, sem, tile + 1, 1 - slot, False):
            cp.start()

    for cp in _step_copies(x_hbm, xbuf, sem, tile, slot, False):
        cp.wait()
    return slot


def _step_tile_params():
    return pltpu.CompilerParams(dimension_semantics=("arbitrary",),
                                vmem_limit_bytes=VMEM_LIMIT_BYTES)


def _in_proj_step_kernel(x_hbm, meta_ref, g_ref, w32_ref, uc_ref, ut_ref, ucm_ref, usm_ref,
                         xbuf, sem, w_ref):
    n_chunks = x_hbm.shape[1]
    slot = _fetch_steps(x_hbm, xbuf, sem)

    @pl.when(pl.program_id(0) == 0)
    def _():
        w_ref[...] = w32_ref[...].astype(BF16)
        zm = _rms(meta_ref[...], g_ref[...]).astype(BF16)
        pm = jnp.dot(zm, w_ref[...], preferred_element_type=F32)
        ucm_ref[...] = pm[:, :D_CONV] * jax.nn.sigmoid(pm[:, D_CONV:2 * D_CONV])
        usm_ref[...] = pm[:, 2 * D_CONV:]

    x = xbuf[slot].reshape(SUBLANES * n_chunks, D_MODEL)
    z = _rms(x, g_ref[...]).astype(BF16)
    p = jnp.dot(z, w_ref[...], preferred_element_type=F32)
    uc = p[:, :D_CONV] * jax.nn.sigmoid(p[:, D_CONV:2 * D_CONV])
    uc_ref[...] = uc.reshape(SUBLANES, n_chunks, D_CONV)
    ut = p[:, 2 * D_CONV:].T.astype(BF16)
    for r in range(SUBLANES):
        ut_ref[:, r * SSM_GROUP:(r + 1) * SSM_GROUP, :] = (
            ut[:, r * n_chunks:(r + 1) * n_chunks].reshape(N_GROUPS, SSM_GROUP, n_chunks))


def _in_proj_step(x4, meta, g, w_in):
    bsz, n_chunks = x4.shape[:2]
    rows = bsz * n_chunks
    return pl.pallas_call(
        _in_proj_step_kernel,
        grid=(CHUNK // SUBLANES * bsz,),
        in_specs=[
            pl.BlockSpec(memory_space=pl.ANY),
            pl.BlockSpec((N_META, D_MODEL), lambda t: (0, 0)),
            pl.BlockSpec((1, D_MODEL), lambda t: (0, 0)),
            pl.BlockSpec((D_MODEL, 2 * D_CONV + D_SSM), lambda t: (0, 0),
                         pipeline_mode=pl.Buffered(1)),
        ],
        out_specs=[
            pl.BlockSpec((SUBLANES, n_chunks, D_CONV), lambda t: (t // bsz, t % bsz, 0)),
            pl.BlockSpec((N_GROUPS, SUBLANES * SSM_GROUP, n_chunks),
                         lambda t: (0, t // bsz, t % bsz)),
            pl.BlockSpec((N_META, D_CONV), lambda t: (0, 0)),
            pl.BlockSpec((N_META, D_SSM), lambda t: (0, 0)),
        ],
        out_shape=[
            jax.ShapeDtypeStruct((CHUNK, rows, D_CONV), F32),
            jax.ShapeDtypeStruct((N_GROUPS, CHUNK_LANES, rows), BF16),
            jax.ShapeDtypeStruct((N_META, D_CONV), F32),
            jax.ShapeDtypeStruct((N_META, D_SSM), F32),
        ],
        scratch_shapes=[pltpu.VMEM((2, SUBLANES, n_chunks, D_MODEL), F32),
                        pltpu.SemaphoreType.DMA((2, SUBLANES)),
                        pltpu.VMEM((D_MODEL, 2 * D_CONV + D_SSM), BF16)],
        compiler_params=_step_tile_params(),
        name="in_proj",
    )(x4, meta, g, w_in)


CONV_GROUP = 8
CONV_ROWS = 32


def _conv_kernel(u_ref, um_ref, w_ref, cb_ref, lg_ref, lb_ref, o_ref, cat_ref, acc_ref,
                 ws_ref, carry_ref):
    n_steps, n_chunks, _ = u_ref.shape
    cat_ref[CONV_PAD:CONV_PAD + n_steps] = u_ref[...]
    chunk = lax.broadcasted_iota(jnp.int32, (n_chunks, 1), 0)
    for i in range(CONV_PAD):
        s_prev = n_steps - CONV_PAD + i
        prev = pltpu.roll(u_ref[s_prev], 1, axis=0)
        meta_row = um_ref[N_META - CONV_PAD + i:N_META - CONV_PAD + i + 1, :]
        cat_ref[i] = jnp.where(chunk == 0, meta_row, prev)
        nxt = pltpu.roll(u_ref[i], n_chunks - 1, axis=0)
        cat_ref[n_steps + CONV_PAD + i] = jnp.where(chunk == n_chunks - 1, 0.0, nxt)

    n_half = (CONV_WIDTH + 1) // 2
    cat_ref[n_steps + 2 * CONV_PAD] = jnp.zeros((n_chunks, D_CONV), F32)
    for i in range(n_half):
        odd_tap = w_ref[2 * i + 1:2 * i + 2, :] if 2 * i + 1 < CONV_WIDTH else 0.0
        ws_ref[i:i + 1, :] = w_ref[2 * i:2 * i + 1, :] + odd_tap
    a_first = jnp.zeros((n_chunks, D_CONV), F32)
    for i in range(n_half):
        a_first = a_first + cat_ref[2 * i] * w_ref[2 * i:2 * i + 1, :]
    carry_ref[...] = a_first

    def body(sg, carry):
        s0 = sg * CONV_GROUP
        for c in range(D_CONV // LANES):
            cols = slice(c * LANES, (c + 1) * LANES)
            n_sub = CONV_ROWS // SUBLANES
            pairs = CONV_GROUP // 2

            def tap(ref, k):
                return jnp.broadcast_to(ref[k:k + 1, cols], (SUBLANES, LANES))

            def rows_body(rt, carry2):
                r0 = [pl.multiple_of(rt * CONV_ROWS + sub * SUBLANES, SUBLANES)
                      for sub in range(n_sub)]
                zero = jnp.zeros((SUBLANES, LANES), F32)
                acc_a = [[zero] * pairs for _ in range(n_sub)]
                acc_b = [[zero] * pairs for _ in range(n_sub)]
                acc_p = [[zero] * pairs for _ in range(n_sub)]
                for j in range(n_half + pairs - 1):
                    odd = [cat_ref[s0 + 2 * j + 1, pl.ds(r0[sub], SUBLANES), cols]
                           for sub in range(n_sub)]
                    even = [cat_ref[s0 + 2 * j + 2, pl.ds(r0[sub], SUBLANES), cols]
                            for sub in range(n_sub)]
                    both = [odd[sub] + even[sub] for sub in range(n_sub)]
                    for m in range(pairs):
                        i = j - m
                        if not 0 <= i < n_half:
                            continue
                        t_a, t_p = tap(w_ref, 2 * i), tap(ws_ref, i)
                        for sub in range(n_sub):
                            acc_a[sub][m] = acc_a[sub][m] + even[sub] * t_a
                            acc_p[sub][m] = acc_p[sub][m] + both[sub] * t_p
                        if 2 * i + 1 < CONV_WIDTH:
                            t_b = tap(w_ref, 2 * i + 1)
                            for sub in range(n_sub):
                                acc_b[sub][m] = acc_b[sub][m] + odd[sub] * t_b
                for sub in range(n_sub):
                    rows = pl.ds(r0[sub], SUBLANES)
                    a_prev = carry_ref[rows, cols]
                    for m in range(pairs):
                        acc_ref[2 * m, rows, cols] = a_prev + acc_b[sub][m]
                        acc_ref[2 * m + 1, rows, cols] = (
                            acc_p[sub][m] - acc_a[sub][m] - acc_b[sub][m])
                        a_prev = acc_a[sub][m]
                    carry_ref[rows, cols] = a_prev
                return carry2

            lax.fori_loop(0, n_chunks // CONV_ROWS, rows_body, 0)
        for j in range(CONV_GROUP):
            y = acc_ref[j] + cb_ref[...]
            yc = y - jnp.mean(y, axis=-1, keepdims=True)
            yn = yc * lax.rsqrt(jnp.mean(yc * yc, axis=-1, keepdims=True) + NORM_EPS)
            yn = yn * lg_ref[...] + lb_ref[...]
            o_ref[s0 + j] = (yn * jax.nn.sigmoid(yn)).astype(BF16)
        return carry

    lax.fori_loop(0, n_steps // CONV_GROUP, body, 0)


def _conv_module(u_conv, u_conv_meta, conv_w, conv_b, ln_g, ln_b, bsz):
    n_steps, rows, _ = u_conv.shape
    n_chunks = rows // bsz
    vec = pl.BlockSpec((1, D_CONV), lambda b: (0, 0))
    blk = pl.BlockSpec((n_steps, n_chunks, D_CONV), lambda b: (0, b, 0))
    return pl.pallas_call(
        _conv_kernel,
        grid=(bsz,),
        in_specs=[
            blk,
            pl.BlockSpec((N_META, D_CONV), lambda b: (0, 0)),
            pl.BlockSpec((CONV_WIDTH, D_CONV), lambda b: (0, 0)),
            vec, vec, vec,
        ],
        out_specs=blk,
        out_shape=jax.ShapeDtypeStruct((n_steps, rows, D_CONV), BF16),
        scratch_shapes=[pltpu.VMEM((n_steps + 2 * CONV_PAD + 1, n_chunks, D_CONV), F32),
                        pltpu.VMEM((CONV_GROUP, n_chunks, D_CONV), F32),
                        pltpu.VMEM(((CONV_WIDTH + 1) // 2, D_CONV), F32),
                        pltpu.VMEM((n_chunks, D_CONV), F32)],

        compiler_params=_params(),
        name="conv_module",
    )(u_conv, u_conv_meta, conv_w, conv_b, ln_g, ln_b)


def _zoh_kernel(lre_ref, lim_ref, ldt_ref, lbr_ref, lbi_ref, zr_ref, zi_ref):
    lre = lre_ref[...]
    lim = lim_ref[...]
    dt = jnp.exp(ldt_ref[...])
    ea = jnp.exp(lre * dt)
    lbr = ea * jnp.cos(lim * dt)
    lbi = ea * jnp.sin(lim * dt)
    nr = lbr - 1.0
    den = lre * lre + lim * lim
    lbr_ref[...] = lbr
    lbi_ref[...] = lbi
    zr_ref[...] = (nr * lre + lbi * lim) / den
    zi_ref[...] = (lbi * lre - nr * lim) / den


def _zoh(lam_re, lam_im, log_dt):
    shape = jax.ShapeDtypeStruct(lam_re.shape, F32)
    return pl.pallas_call(_zoh_kernel, out_shape=[shape] * 4, name="s5_zoh")(
        lam_re, lam_im, log_dt)


def _lanes(x, n):
    return jnp.concatenate([x] * (n // LANES), axis=1)


def _cmul(ar, ai, br, bi):
    return ar * br - ai * bi, ar * bi + ai * br


def _cpow(br, bi, expo, nbits):
    rr = jnp.ones(expo.shape, F32)
    ri = jnp.zeros(expo.shape, F32)
    for k in range(nbits):
        bit = ((expo >> k) & 1) == 1
        nr, ni = _cmul(rr, ri, br, bi)
        rr = jnp.where(bit, nr, rr)
        ri = jnp.where(bit, ni, ri)
        br, bi = _cmul(br, bi, br, bi)
    return rr, ri


def _build_chunk_ops(gi, csc_ref, cc_ref, bc_ref, cr_ref, um_ref, toep_ref, wend_ref,
                     wout_ref):
    t = CHUNK
    rows = STATE_ROWS
    n_cols = CHUNK_LANES // LANES
    lam = (jnp.broadcast_to(csc_ref[gi, :, 0:1], (rows, LANES)),
           jnp.broadcast_to(csc_ref[gi, :, 1:2], (rows, LANES)))
    zr = jnp.broadcast_to(csc_ref[gi, :, 2:3], (rows, LANES))
    zi = jnp.broadcast_to(csc_ref[gi, :, 3:4], (rows, LANES))
    sel = (lax.broadcasted_iota(jnp.int32, (SSM_GROUP, LANES), 1) % SSM_GROUP
           == lax.broadcasted_iota(jnp.int32, (SSM_GROUP, LANES), 0)).astype(F32)

    def tile(ref):
        return jnp.dot(ref[gi], sel, precision=lax.Precision.HIGHEST,
                       preferred_element_type=F32)

    def swap_re_im(a):
        p = SSM_STATE
        return jnp.concatenate([a[p:2 * p], a[0:p], a[3 * p:], a[2 * p:3 * p]], axis=0)

    c_same, b_same = tile(cc_ref), tile(bc_ref)
    c_swap, b_swap = swap_re_im(c_same), swap_re_im(b_same)
    row = lax.broadcasted_iota(jnp.int32, (rows, LANES), 0)
    is_re = (row // SSM_STATE) % 2 == 0
    sgn = jnp.where(is_re, -1.0, 1.0)
    ca = jnp.where(is_re, c_same, -c_same)
    cb = -c_swap
    bb_same = zr * b_same + sgn * zi * b_swap
    sbb_swap = sgn * (zr * b_swap - sgn * zi * b_same)

    pows = {1: lam}
    k = 1
    while k < t:
        pows[2 * k] = _cmul(*pows[k], *pows[k])
        k *= 2
    step_bits = STEPS_PER_VREG.bit_length() - 1
    fwd = slice(0, DIR_ROWS)
    bwd = slice(DIR_ROWS, rows)
    every = slice(0, rows)

    def rows_of(v, rs):
        return v[0][rs], v[1][rs]

    def column(base, e, rs):
        out = rows_of(base, rs)
        for bit, val in pows.items():
            if e & bit:
                out = _cmul(*out, *rows_of(val, rs))
        return out

    def times_b(p, rs):
        return p[0] * bb_same[rs] + p[1] * sbb_swap[rs]

    def times_c(p, rs):
        return ca[rs] * p[0] + cb[rs] * p[1]

    i8 = lax.broadcasted_iota(jnp.int32, (rows, LANES), 1) // SSM_GROUP
    asc0 = _cpow(*lam, i8, step_bits)
    desc0 = _cpow(*lam, STEPS_PER_VREG - 1 - i8, step_bits)
    asc1_0 = _cmul(*asc0, *lam)
    desc1_0 = _cmul(*desc0, *lam)

    wend_f, wend_b, lag_b, wout_f, wout_b = [], [], [], [], []
    for q in range(n_cols):
        e_asc = STEPS_PER_VREG * q
        e_desc = STEPS_PER_VREG * (n_cols - 1 - q)
        asc1_q = column(asc1_0, e_asc, every)
        wend_f.append(times_b(column(desc0, e_desc, fwd), fwd))
        wend_b.append(times_b(column(asc0, e_asc, bwd), bwd))
        lag_b.append(times_b(rows_of(asc1_q, bwd), bwd))
        wout_f.append(times_c(rows_of(asc1_q, fwd), fwd))
        wout_b.append(times_c(column(desc1_0, e_desc, bwd), bwd))
        cols = slice(q * LANES, (q + 1) * LANES)
        wend_ref[gi, fwd, cols] = wend_f[q].astype(BF16)
        wend_ref[gi, bwd, cols] = wend_b[q].astype(BF16)

    meta_cols = N_META * SSM_GROUP // LANES
    x0 = sum(jnp.sum(wend_f[n_cols - meta_cols + m] * um_ref[gi, :, m * LANES:(m + 1) * LANES],
                     axis=1, keepdims=True) for m in range(meta_cols))

    lane = lax.broadcasted_iota(jnp.int32, (DIR_ROWS, LANES), 1)
    lag0_b = jnp.where(lane >= LANES - SSM_GROUP, bb_same[bwd], 0.0)
    zero = jnp.zeros((DIR_ROWS, LANES), F32)
    bcat = jnp.concatenate(
        [jnp.concatenate(wend_f + [zero] * n_cols, axis=1),
         jnp.concatenate([zero] * (n_cols - 1) + [lag0_b] + lag_b, axis=1)], axis=0)
    lane_r = lax.broadcasted_iota(jnp.int32, (SSM_GROUP, rows), 1)
    c2 = jnp.where((lane_r // SSM_STATE) % 2 == 0, cr_ref[gi], -cr_ref[gi])
    kk = jnp.dot(c2, bcat, precision=lax.Precision.HIGHEST,
                 preferred_element_type=F32)
    for tt in range(t):
        off = (t - 1 - tt) * SSM_GROUP
        toep_ref[gi, tt * SSM_GROUP:(tt + 1) * SSM_GROUP, :] = (
            kk[:, off:off + CHUNK_LANES].astype(BF16))

    wout = jnp.concatenate([jnp.concatenate(wout_f, axis=1),
                            jnp.concatenate(wout_b, axis=1)], axis=0)
    wout_ref[gi] = wout.T.astype(BF16)
    return pows[t], x0


SSM_GROUPS_PER_STEP = 2


def _chunk_scan(er, ei, ar, ai, x0, chunk, n_chunks, forward):
    width = er.shape[1]
    xr, xi = er, ei
    if x0 is not None:
        fr, fi = _cmul(ar, ai, x0[0], x0[1])
        xr = xr + jnp.where(chunk == 0, _lanes(fr, width), 0.0)
        xi = xi + jnp.where(chunk == 0, _lanes(fi, width), 0.0)

    def shifted(v, sh):
        if forward:
            return jnp.where(chunk >= sh, pltpu.roll(v, sh, axis=1), 0.0)
        return jnp.where(chunk < n_chunks - sh, pltpu.roll(v, width - sh, axis=1), 0.0)

    sh = 1
    while sh < n_chunks:
        sr, si = shifted(xr, sh), shifted(xi, sh)
        wr, wi = _lanes(ar, width), _lanes(ai, width)
        xr, xi = xr + wr * sr - wi * si, xi + wr * si + wi * sr
        ar, ai = _cmul(ar, ai, ar, ai)
        sh *= 2
    inr, ini = shifted(xr, 1), shifted(xi, 1)
    if x0 is not None:
        inr = jnp.where(chunk == 0, x0[0], inr)
        ini = jnp.where(chunk == 0, x0[1], ini)
    return inr, ini


def _ssm_kernel(u_ref, um_ref, csc_ref, cc_ref, bc_ref, cr_ref, y_ref, toep_ref, wend_ref,
                wout_ref, yi_ref, *, n_chunks):
    for gi in range(SSM_GROUPS_PER_STEP):
        (lam_r, lam_i), x0 = _build_chunk_ops(gi, csc_ref, cc_ref, bc_ref, cr_ref, um_ref,
                                              toep_ref, wend_ref, wout_ref)
        u = u_ref[gi]
        width = u.shape[1]
        e = jnp.dot(wend_ref[gi], u, preferred_element_type=F32)
        yi_ref[gi] = jnp.dot(toep_ref[gi], u, preferred_element_type=F32)
        chunk =lax.broadcasted_iota(jnp.int32, (1, width), 1) % n_chunks
        p = SSM_STATE
        ar, ai = lam_r, lam_i
        f_in = _chunk_scan(e[0:p], e[p:2 * p], ar[0:p], ai[0:p], (x0[0:p], x0[p:2 * p]),
                           chunk, n_chunks, True)
        b_in = _chunk_scan(e[2 * p:3 * p], e[3 * p:], ar[2 * p:3 * p], ai[2 * p:3 * p],
                           None, chunk, n_chunks, False)
        xin = jnp.concatenate([f_in[0], f_in[1], b_in[0], b_in[1]], axis=0).astype(BF16)
        y = yi_ref[gi] + jnp.dot(wout_ref[gi], xin, preferred_element_type=F32)
        y_ref[gi] = y.astype(BF16)


def _ssm(u_col, u_meta, csc, c_col, b_col, c_row, n_chunks):
    g, _, width = u_col.shape
    per = SSM_GROUPS_PER_STEP
    return pl.pallas_call(
        functools.partial(_ssm_kernel, n_chunks=n_chunks),
        grid=(g // per,),
        in_specs=[
            pl.BlockSpec((per, CHUNK_LANES, width), lambda i: (i, 0, 0)),
            pl.BlockSpec((per, 1, N_META * SSM_GROUP), lambda i: (i, 0, 0)),
            pl.BlockSpec((per, STATE_ROWS, 4), lambda i: (i, 0, 0)),
            pl.BlockSpec((per, STATE_ROWS, SSM_GROUP), lambda i: (i, 0, 0)),
            pl.BlockSpec((per, STATE_ROWS, SSM_GROUP), lambda i: (i, 0, 0)),
            pl.BlockSpec((per, SSM_GROUP, STATE_ROWS), lambda i: (i, 0, 0)),
        ],
        out_specs=pl.BlockSpec((per, CHUNK_LANES, width), lambda i: (i, 0, 0)),
        out_shape=jax.ShapeDtypeStruct((g, CHUNK_LANES, width), BF16),
        scratch_shapes=[
            pltpu.VMEM((per, CHUNK_LANES, CHUNK_LANES), BF16),
            pltpu.VMEM((per, STATE_ROWS, CHUNK_LANES), BF16),
            pltpu.VMEM((per, CHUNK_LANES, STATE_ROWS), BF16),
            pltpu.VMEM((per, CHUNK_LANES, width), F32),
        ],
        compiler_params=_params(),
        name="s5_mixer",
    )(u_col, u_meta, csc, c_col, b_col, c_row)


FF_CHUNK = 1024
MIX_PER = 4


MIX_SLOTS = 3


def _mix_ffn_kernel(x_hbm, co_ref, y_ref, ut_ref, d_ref, gwt_ref, gb_ref, wo_ref,
                    gf_ref, w1_ref, w2_ref, gl_ref, o_hbm, buf, sem_in, sem_out,
                    mixed_ref, h_ref, z_ref, a_ref):
    n_chunks = x_hbm.shape[1]
    rows = SUBLANES * n_chunks
    step = pl.program_id(0)
    n_tiles = pl.num_programs(0) - 1
    tile = step - 1

    def mix_stages(dst):
        per = MIX_PER
        n_half = SUBLANES // per
        state = {}

        def col(ref, k):
            return jnp.concatenate(
                [ref[:, r * SSM_GROUP:(r + 1) * SSM_GROUP, :].reshape(D_SSM, n_chunks)
                 for r in range(k * per, (k + 1) * per)], axis=1).astype(F32)

        def gelu_stage(k):
            def run():
                state["ge", k] = jax.nn.gelu(col(y_ref, k) + d_ref[...] * col(ut_ref, k))
            return run

        def glu_stage(k):
            def run():
                ge = state["ge", k]
                gate = (jnp.dot(gwt_ref[...], ge.astype(BF16), preferred_element_type=F32)
                        + gb_ref[...])
                state["s5t", k] = ge * jax.nn.sigmoid(gate)
            return run

        def store_stage(k):
            def run():
                rs = slice(k * per * n_chunks, (k + 1) * per * n_chunks)
                mixed_ref[dst, rs, :D_CONV] = co_ref[k * per:(k + 1) * per].reshape(
                    per * n_chunks, D_CONV)
                mixed_ref[dst, rs, D_CONV:] = state["s5t", k].T.astype(BF16)
            return run

        return [f(k) for f in (gelu_stage, glu_stage, store_stage) for k in range(n_half)]

    @pl.when(step == 0)
    def _():
        for cp in _step_copies(x_hbm, buf, sem_in, 0, 0, False):
            cp.start()
        for stage in mix_stages(0):
            stage()

    @pl.when(step >= 1)
    def _():
        slot = tile % MIX_SLOTS
        nxt = (tile + 1) % MIX_SLOTS

        @pl.when(tile >= 2)
        def _():
            for cp in _step_copies(o_hbm, buf, sem_out, tile - 2, nxt, True):
                cp.wait()

        @pl.when(tile + 1 < n_tiles)
        def _():
            for cp in _step_copies(x_hbm, buf, sem_in, tile + 1, nxt, False):
                cp.start()

        for cp in _step_copies(x_hbm, buf, sem_in, tile, slot, False):
            cp.wait()

        stages = mix_stages(step % 2)
        half = D_FF // 2
        n_slots = D_FF // FF_CHUNK + 2
        n_stages = len(stages)
        done = [0, 0]

        def fill_slot():
            done[0] += 1
            while done[1] * n_slots < done[0] * n_stages:
                stages.pop(0)()
                done[1] += 1

        h = buf[slot].reshape(rows, D_MODEL) + jnp.dot(
            mixed_ref[tile % 2], wo_ref[...], preferred_element_type=F32)
        z_ref[...] = _rms(h, gf_ref[...]).astype(BF16)
        h_ref[...] = h
        for k in range(2):
            for j in range(half // FF_CHUNK):
                cols = slice(k * half + j * FF_CHUNK, k * half + (j + 1) * FF_CHUNK)
                a = jnp.dot(z_ref[...], w1_ref[:, cols], preferred_element_type=F32)
                a_ref[:, j * FF_CHUNK:(j + 1) * FF_CHUNK] = (
                    jnp.square(jnp.maximum(a, 0.0)).astype(BF16))
                fill_slot()
            h_ref[...] += jnp.dot(a_ref[...], w2_ref[k * half:(k + 1) * half, :],
                                  preferred_element_type=F32)
            fill_slot()
        assert not stages
        buf[slot] = _rms(h_ref[...], gl_ref[...]).reshape(SUBLANES, n_chunks, D_MODEL)

        for cp in _step_copies(o_hbm, buf, sem_out, tile, slot, True):
            cp.start()

        @pl.when(tile == n_tiles - 1)
        def _():
            for cp in (_step_copies(o_hbm, buf, sem_out, tile - 1, (tile - 1) % MIX_SLOTS, True)
                       + _step_copies(o_hbm, buf, sem_out, tile, slot, True)):
                cp.wait()


def _mix_ffn(x4, conv_out, y_col, u_col, d_col, glu_wt, glu_b_col, w_out, g_ffn, w1, w2,
             g_final):
    bsz, n_chunks = x4.shape[:2]

    def const(shape):
        return pl.BlockSpec(shape, lambda t: (0, 0), pipeline_mode=pl.Buffered(1))

    n_tiles = CHUNK // SUBLANES * bsz
    hbm = pl.BlockSpec(memory_space=pl.ANY)

    def slab(s):
        return jnp.minimum(s, n_tiles - 1) // bsz

    def batch(s):
        return jnp.minimum(s, n_tiles - 1) % bsz

    col_blk = pl.BlockSpec((N_GROUPS, SUBLANES * SSM_GROUP, n_chunks),
                           lambda s: (0, slab(s), batch(s)))
    step_sems = pltpu.SemaphoreType.DMA((MIX_SLOTS, SUBLANES))
    return pl.pallas_call(
        _mix_ffn_kernel,
        grid=(n_tiles + 1,),
        in_specs=[
            hbm,
            pl.BlockSpec((SUBLANES, n_chunks, D_CONV), lambda s: (slab(s), batch(s), 0)),
            col_blk,
            col_blk,
            const((D_SSM, 1)),
            const((D_SSM, D_SSM)),
            const((D_SSM, 1)),
            const((D_MODEL, D_MODEL)),
            const((1, D_MODEL)),
            const((D_MODEL, D_FF)),
            const((D_FF, D_MODEL)),
            const((1, D_MODEL)),
        ],
        out_specs=hbm,
        out_shape=jax.ShapeDtypeStruct(x4.shape, F32),
        scratch_shapes=[
            pltpu.VMEM((MIX_SLOTS, SUBLANES, n_chunks, D_MODEL), F32),
            step_sems, step_sems,
            pltpu.VMEM((2, SUBLANES * n_chunks, D_MODEL), BF16),
            pltpu.VMEM((SUBLANES * n_chunks, D_MODEL), F32),
            pltpu.VMEM((SUBLANES * n_chunks, D_MODEL), BF16),
            pltpu.VMEM((SUBLANES * n_chunks, D_FF // 2), BF16),
        ],
        compiler_params=_step_tile_params(),
        name="mix_ffn",
    )(x4, conv_out, y_col, u_col, d_col, glu_wt, glu_b_col, w_out, g_ffn, w1, w2, g_final)


def _s5_operator_inputs(lbr, lbi, zr, zi, b_re, b_im, c_re, c_im):
    g, p, h = N_GROUPS, SSM_STATE, SSM_GROUP

    sc = jnp.transpose(jnp.stack([lbr, lbi, zr, zi], axis=-1), (1, 0, 2, 3))
    csc = jnp.broadcast_to(sc[:, :, None], (g, 2, 2, p, 4)).reshape(g, 4 * p, 4)
    c = jnp.stack([c_re, c_im], axis=0)
    b = jnp.stack([b_re, b_im], axis=0)
    c_col = jnp.transpose(c, (2, 1, 0, 4, 3)).reshape(g, 4 * p, h)
    b_col = jnp.transpose(b, (2, 1, 0, 3, 4)).reshape(g, 4 * p, h)
    c_row = jnp.transpose(c, (2, 3, 1, 0, 4)).reshape(g, h, 4 * p)
    return csc, c_col, b_col, c_row


def kernel(x, meta_tokens, norm_mix_g, w_in, conv_w, conv_b, conv_ln_g, conv_ln_b,
           ssm_lam_re, ssm_lam_im, ssm_log_dt, ssm_b_re, ssm_b_im, ssm_c_re, ssm_c_im,
           ssm_d, ssm_glu_w, ssm_glu_b, w_out, norm_ffn_g, w_ff1, w_ff2, norm_final_g):
    assert w_in.shape[0] == 1, "single-layer block"
    bsz, seq, _ = x.shape
    assert seq % CHUNK == 0 and CHUNK >= N_META
    n_chunks = seq // CHUNK
    g, h = N_GROUPS, SSM_GROUP

    x4 = x.reshape(bsz, n_chunks, CHUNK, D_MODEL)
    g_mix = norm_mix_g[0][None, :]
    u_conv, u_col, u_conv_m, u_ssm_m = _in_proj_step(x4, meta_tokens, g_mix, w_in[0])

    conv_out = _conv_module(u_conv, u_conv_m, conv_w[0], conv_b[0][None, :],
                            conv_ln_g[0][None, :], conv_ln_b[0][None, :], bsz)

    ldt = jnp.broadcast_to(ssm_log_dt[0][..., None], ssm_lam_re[0].shape)
    flat = lambda a: a.reshape(2 * g, SSM_STATE)
    lbr, lbi, zr, zi = [a.reshape(2, g, SSM_STATE) for a in
                        _zoh(flat(ssm_lam_re[0]), flat(ssm_lam_im[0]), flat(ldt))]
    csc, c_col, b_col, c_row = _s5_operator_inputs(
        lbr, lbi, zr, zi, ssm_b_re[0], ssm_b_im[0], ssm_c_re[0], ssm_c_im[0])

    u_meta = jnp.transpose(u_ssm_m.reshape(N_META, g, h), (1, 0, 2)).reshape(g, 1, N_META * h)
    y_col = _ssm(u_col, u_meta, csc, c_col, b_col, c_row, n_chunks)

    out = _mix_ffn(x4, conv_out, y_col, u_col, ssm_d[0][:, None],
                   ssm_glu_w[0].T.astype(BF16), ssm_glu_b[0][:, None],
                   w_out[0].astype(BF16), norm_ffn_g[0][None, :], w_ff1[0].astype(BF16),
                   w_ff2[0].astype(BF16), norm_final_g[None, :])
    return out.reshape(bsz, seq, D_MODEL)
```

```python
import functools

import jax
import jax.numpy as jnp
from jax import lax
from jax.experimental import pallas as pl
from jax.experimental.pallas import tpu as pltpu

F32 = jnp.float32
BF16 = jnp.bfloat16

D_MODEL = 1024
N_META = 16
D_CONV = 512
D_SSM = 512
CONV_WIDTH = 31
CONV_PAD = CONV_WIDTH // 2
SSM_GROUP = 16
N_GROUPS = D_SSM // SSM_GROUP
SSM_STATE = 64
D_FF = 4096
NORM_EPS = 1e-5
LANES = 128
SUBLANES = 8

CHUNK = 32
CHUNK_LANES = CHUNK * SSM_GROUP
STATE_ROWS = 4 * SSM_STATE
DIR_ROWS = 2 * SSM_STATE
STEPS_PER_VREG = LANES // SSM_GROUP

VMEM_LIMIT_BYTES = 60 * 1024 * 1024


def _rms(x, g):
    return x * lax.rsqrt(jnp.mean(x * x, axis=-1, keepdims=True) + NORM_EPS) * g


def _params(n_axes=1):
    return pltpu.CompilerParams(dimension_semantics=("parallel",) * n_axes,
                                vmem_limit_bytes=VMEM_LIMIT_BYTES)


def _step_copies(hbm_ref, buf_ref, sem_ref, tile, slot, to_hbm):
    bsz = hbm_ref.shape[0]
    slab, b = tile // bsz, tile % bsz
    copies = []
    for r in range(SUBLANES):
        hbm = hbm_ref.at[b, :, slab * SUBLANES + r, :]
        vmem = buf_ref.at[slot, r]
        src, dst = (vmem, hbm) if to_hbm else (hbm, vmem)
        copies.append(pltpu.make_async_copy(src, dst, sem_ref.at[slot, r]))
    return copies


def _fetch_steps(x_hbm, xbuf, sem):
    tile = pl.program_id(0)
    slot = tile % 2

    @pl.when(tile == 0)
    def _():
        for cp in _step_copies(x_hbm, xbuf, sem, tile, slot, False):
            cp.start()

    @pl.when(tile + 1 < pl.num_programs(0))
    def _():
        for cp in _step_copies(x_hbm, xbuf, sem, tile + 1, 1 - slot, False):
            cp.start()

    for cp in _step_copies(x_hbm, xbuf, sem, tile, slot, False):
        cp.wait()
    return slot


def _step_tile_params():
    return pltpu.CompilerParams(dimension_semantics=("arbitrary",),
                                vmem_limit_bytes=VMEM_LIMIT_BYTES)


def _in_proj_step_kernel(x_hbm, meta_ref, g_ref, w32_ref, uc_ref, ut_ref, ucm_ref, usm_ref,
                         xbuf, sem, w_ref):
    n_chunks = x_hbm.shape[1]
    slot = _fetch_steps(x_hbm, xbuf, sem)

    @pl.when(pl.program_id(0) == 0)
    def _():
        w_ref[...] = w32_ref[...].astype(BF16)
        zm = _rms(meta_ref[...], g_ref[...]).astype(BF16)
        pm = jnp.dot(zm, w_ref[...], preferred_element_type=F32)
        ucm_ref[...] = pm[:, :D_CONV] * jax.nn.sigmoid(pm[:, D_CONV:2 * D_CONV])
        usm_ref[...] = pm[:, 2 * D_CONV:]

    x = xbuf[slot].reshape(SUBLANES * n_chunks, D_MODEL)
    z = _rms(x, g_ref[...]).astype(BF16)
    p = jnp.dot(z, w_ref[...], preferred_element_type=F32)
    uc = p[:, :D_CONV] * jax.nn.sigmoid(p[:, D_CONV:2 * D_CONV])
    uc_ref[...] = uc.reshape(SUBLANES, n_chunks, D_CONV)
    ut = p[:, 2 * D_CONV:].T.astype(BF16)
    for r in range(SUBLANES):
        ut_ref[:, r * SSM_GROUP:(r + 1) * SSM_GROUP, :] = (
            ut[:, r * n_chunks:(r + 1) * n_chunks].reshape(N_GROUPS, SSM_GROUP, n_chunks))


def _in_proj_step(x4, meta, g, w_in):
    bsz, n_chunks = x4.shape[:2]
    rows = bsz * n_chunks
    return pl.pallas_call(
        _in_proj_step_kernel,
        grid=(CHUNK // SUBLANES * bsz,),
        in_specs=[
            pl.BlockSpec(memory_space=pl.ANY),
            pl.BlockSpec((N_META, D_MODEL), lambda t: (0, 0)),
            pl.BlockSpec((1, D_MODEL), lambda t: (0, 0)),
            pl.BlockSpec((D_MODEL, 2 * D_CONV + D_SSM), lambda t: (0, 0),
                         pipeline_mode=pl.Buffered(1)),
        ],
        out_specs=[
            pl.BlockSpec((SUBLANES, n_chunks, D_CONV), lambda t: (t // bsz, t % bsz, 0)),
            pl.BlockSpec((N_GROUPS, SUBLANES * SSM_GROUP, n_chunks),
                         lambda t: (0, t // bsz, t % bsz)),
            pl.BlockSpec((N_META, D_CONV), lambda t: (0, 0)),
            pl.BlockSpec((N_META, D_SSM), lambda t: (0, 0)),
        ],
        out_shape=[
            jax.ShapeDtypeStruct((CHUNK, rows, D_CONV), F32),
            jax.ShapeDtypeStruct((N_GROUPS, CHUNK_LANES, rows), BF16),
            jax.ShapeDtypeStruct((N_META, D_CONV), F32),
            jax.ShapeDtypeStruct((N_META, D_SSM), F32),
        ],
        scratch_shapes=[pltpu.VMEM((2, SUBLANES, n_chunks, D_MODEL), F32),
                        pltpu.SemaphoreType.DMA((2, SUBLANES)),
                        pltpu.VMEM((D_MODEL, 2 * D_CONV + D_SSM), BF16)],
        compiler_params=_step_tile_params(),
        name="in_proj",
    )(x4, meta, g, w_in)


CONV_GROUP = 8
CONV_ROWS = 32


def _conv_kernel(u_ref, um_ref, w_ref, cb_ref, lg_ref, lb_ref, o_ref, cat_ref, acc_ref,
                 ws_ref, carry_ref):
    n_steps, n_chunks, _ = u_ref.shape
    chunk = lax.broadcasted_iota(jnp.int32, (n_chunks, 1), 0)
    for i in range(CONV_PAD):
        s_prev = n_steps - CONV_PAD + i
        prev = pltpu.roll(u_ref[s_prev], 1, axis=0)
        meta_row = um_ref[N_META - CONV_PAD + i:N_META - CONV_PAD + i + 1, :]
        cat_ref[i] = jnp.where(chunk == 0, meta_row, prev)
        nxt = pltpu.roll(u_ref[i], n_chunks - 1, axis=0)
        cat_ref[CONV_PAD + i] = jnp.where(chunk == n_chunks - 1, 0.0, nxt)

    def tile(j, *idx):
        if j < CONV_PAD:
            return cat_ref[(j,) + idx] if idx else cat_ref[j]
        if j < CONV_PAD + n_steps:
            return u_ref[(j - CONV_PAD,) + idx] if idx else u_ref[j - CONV_PAD]
        return cat_ref[(j - n_steps,) + idx] if idx else cat_ref[j - n_steps]

    n_half = (CONV_WIDTH + 1) // 2
    cat_ref[2 * CONV_PAD] = jnp.zeros((n_chunks, D_CONV), F32)
    for i in range(n_half):
        odd_tap = w_ref[2 * i + 1:2 * i + 2, :] if 2 * i + 1 < CONV_WIDTH else 0.0
        ws_ref[i:i + 1, :] = w_ref[2 * i:2 * i + 1, :] + odd_tap
    a_first = jnp.zeros((n_chunks, D_CONV), F32)
    for i in range(n_half):
        a_first = a_first + tile(2 * i) * w_ref[2 * i:2 * i + 1, :]
    carry_ref[...] = a_first

    def body(sg):
        s0 = sg * CONV_GROUP
        for c in range(D_CONV // LANES):
            cols = slice(c * LANES, (c + 1) * LANES)
            n_sub = CONV_ROWS // SUBLANES
            pairs = CONV_GROUP // 2

            def tap(ref, k):
                return jnp.broadcast_to(ref[k:k + 1, cols], (SUBLANES, LANES))

            def rows_body(rt, carry2):
                r0 = [pl.multiple_of(rt * CONV_ROWS + sub * SUBLANES, SUBLANES)
                      for sub in range(n_sub)]
                zero = jnp.zeros((SUBLANES, LANES), F32)
                acc_a = [[zero] * pairs for _ in range(n_sub)]
                acc_b = [[zero] * pairs for _ in range(n_sub)]
                acc_p = [[zero] * pairs for _ in range(n_sub)]
                for j in range(n_half + pairs - 1):
                    odd = [tile(s0 + 2 * j + 1, pl.ds(r0[sub], SUBLANES), cols)
                           for sub in range(n_sub)]
                    even = [tile(s0 + 2 * j + 2, pl.ds(r0[sub], SUBLANES), cols)
                            for sub in range(n_sub)]
                    both = [odd[sub] + even[sub] for sub in range(n_sub)]
                    for m in range(pairs):
                        i = j - m
                        if not 0 <= i < n_half:
                            continue
                        t_a, t_p = tap(w_ref, 2 * i), tap(ws_ref, i)
                        for sub in range(n_sub):
                            acc_a[sub][m] = acc_a[sub][m] + even[sub] * t_a
                            acc_p[sub][m] = acc_p[sub][m] + both[sub] * t_p
                        if 2 * i + 1 < CONV_WIDTH:
                            t_b = tap(w_ref, 2 * i + 1)
                            for sub in range(n_sub):
                                acc_b[sub][m] = acc_b[sub][m] + odd[sub] * t_b
                for sub in range(n_sub):
                    rows = pl.ds(r0[sub], SUBLANES)
                    a_prev = carry_ref[rows, cols]
                    for m in range(pairs):
                        acc_ref[2 * m, rows, cols] = a_prev + acc_b[sub][m]
                        acc_ref[2 * m + 1, rows, cols] = (
                            acc_p[sub][m] - acc_a[sub][m] - acc_b[sub][m])
                        a_prev = acc_a[sub][m]
                    carry_ref[rows, cols] = a_prev
                return carry2

            lax.fori_loop(0, n_chunks // CONV_ROWS, rows_body, 0)
        for j in range(CONV_GROUP):
            y = acc_ref[j] + cb_ref[...]
            yc = y - jnp.mean(y, axis=-1, keepdims=True)
            yn = yc * lax.rsqrt(jnp.mean(yc * yc, axis=-1, keepdims=True) + NORM_EPS)
            yn = yn * lg_ref[...] + lb_ref[...]
            o_ref[s0 + j] = (yn * jax.nn.sigmoid(yn)).astype(BF16)

    for sg in range(n_steps // CONV_GROUP):
        body(sg)


def _conv_module(u_conv, u_conv_meta, conv_w, conv_b, ln_g, ln_b, bsz):
    n_steps, rows, _ = u_conv.shape
    n_chunks = rows // bsz
    vec = pl.BlockSpec((1, D_CONV), lambda b: (0, 0))
    blk = pl.BlockSpec((n_steps, n_chunks, D_CONV), lambda b: (0, b, 0))
    return pl.pallas_call(
        _conv_kernel,
        grid=(bsz,),
        in_specs=[
            blk,
            pl.BlockSpec((N_META, D_CONV), lambda b: (0, 0)),
            pl.BlockSpec((CONV_WIDTH, D_CONV), lambda b: (0, 0)),
            vec, vec, vec,
        ],
        out_specs=blk,
        out_shape=jax.ShapeDtypeStruct((n_steps, rows, D_CONV), BF16),
        scratch_shapes=[pltpu.VMEM((2 * CONV_PAD + 1, n_chunks, D_CONV), F32),
                        pltpu.VMEM((CONV_GROUP, n_chunks, D_CONV), F32),
                        pltpu.VMEM(((CONV_WIDTH + 1) // 2, D_CONV), F32),
                        pltpu.VMEM((n_chunks, D_CONV), F32)],

        compiler_params=_params(),
        name="conv_module",
    )(u_conv, u_conv_meta, conv_w, conv_b, ln_g, ln_b)


def _zoh_kernel(lre_ref, lim_ref, ldt_ref, lbr_ref, lbi_ref, zr_ref, zi_ref):
    lre = lre_ref[...]
    lim = lim_ref[...]
    dt = jnp.exp(ldt_ref[...])
    ea = jnp.exp(lre * dt)
    lbr = ea * jnp.cos(lim * dt)
    lbi = ea * jnp.sin(lim * dt)
    nr = lbr - 1.0
    den = lre * lre + lim * lim
    lbr_ref[...] = lbr
    lbi_ref[...] = lbi
    zr_ref[...] = (nr * lre + lbi * lim) / den
    zi_ref[...] = (lbi * lre - nr * lim) / den


def _zoh(lam_re, lam_im, log_dt):
    shape = jax.ShapeDtypeStruct(lam_re.shape, F32)
    return pl.pallas_call(_zoh_kernel, out_shape=[shape] * 4, name="s5_zoh")(
        lam_re, lam_im, log_dt)


def _lanes(x, n):
    return jnp.concatenate([x] * (n // LANES), axis=1)


def _cmul(ar, ai, br, bi):
    return ar * br - ai * bi, ar * bi + ai * br


def _cpow(br, bi, expo, nbits):
    rr = jnp.ones(expo.shape, F32)
    ri = jnp.zeros(expo.shape, F32)
    for k in range(nbits):
        bit = ((expo >> k) & 1) == 1
        nr, ni = _cmul(rr, ri, br, bi)
        rr = jnp.where(bit, nr, rr)
        ri = jnp.where(bit, ni, ri)
        br, bi = _cmul(br, bi, br, bi)
    return rr, ri


def _build_chunk_ops(gi, csc_ref, cc_ref, bc_ref, cr_ref, um_ref, toep_ref, wend_ref,
                     wout_ref):
    t = CHUNK
    rows = STATE_ROWS
    n_cols = CHUNK_LANES // LANES
    lam = (jnp.broadcast_to(csc_ref[gi, :, 0:1], (rows, LANES)),
           jnp.broadcast_to(csc_ref[gi, :, 1:2], (rows, LANES)))
    zr = jnp.broadcast_to(csc_ref[gi, :, 2:3], (rows, LANES))
    zi = jnp.broadcast_to(csc_ref[gi, :, 3:4], (rows, LANES))
    sel = (lax.broadcasted_iota(jnp.int32, (SSM_GROUP, LANES), 1) % SSM_GROUP
           == lax.broadcasted_iota(jnp.int32, (SSM_GROUP, LANES), 0)).astype(F32)

    def tile(ref):
        return jnp.dot(ref[gi], sel, precision=lax.Precision.HIGHEST,
                       preferred_element_type=F32)

    def swap_re_im(a):
        p = SSM_STATE
        return jnp.concatenate([a[p:2 * p], a[0:p], a[3 * p:], a[2 * p:3 * p]], axis=0)

    c_same, b_same = tile(cc_ref), tile(bc_ref)
    c_swap, b_swap = swap_re_im(c_same), swap_re_im(b_same)
    row = lax.broadcasted_iota(jnp.int32, (rows, LANES), 0)
    is_re = (row // SSM_STATE) % 2 == 0
    sgn = jnp.where(is_re, -1.0, 1.0)
    ca = jnp.where(is_re, c_same, -c_same)
    cb = -c_swap
    bb_same = zr * b_same + sgn * zi * b_swap
    sbb_swap = sgn * (zr * b_swap - sgn * zi * b_same)

    pows = {1: lam}
    k = 1
    while k < t:
        pows[2 * k] = _cmul(*pows[k], *pows[k])
        k *= 2
    step_bits = STEPS_PER_VREG.bit_length() - 1
    fwd = slice(0, DIR_ROWS)
    bwd = slice(DIR_ROWS, rows)
    every = slice(0, rows)

    def rows_of(v, rs):
        return v[0][rs], v[1][rs]

    def column(base, e, rs):
        out = rows_of(base, rs)
        for bit, val in pows.items():
            if e & bit:
                out = _cmul(*out, *rows_of(val, rs))
        return out

    def times_b(p, rs):
        return p[0] * bb_same[rs] + p[1] * sbb_swap[rs]

    def times_c(p, rs):
        return ca[rs] * p[0] + cb[rs] * p[1]

    i8 = lax.broadcasted_iota(jnp.int32, (rows, LANES), 1) // SSM_GROUP
    asc0 = _cpow(*lam, i8, step_bits)
    desc0 = _cpow(*lam, STEPS_PER_VREG - 1 - i8, step_bits)
    asc1_0 = _cmul(*asc0, *lam)
    desc1_0 = _cmul(*desc0, *lam)

    wend_f, wend_b, lag_b, wout_f, wout_b = [], [], [], [], []
    for q in range(n_cols):
        e_asc = STEPS_PER_VREG * q
        e_desc = STEPS_PER_VREG * (n_cols - 1 - q)
        asc1_q = column(asc1_0, e_asc, every)
        wend_f.append(times_b(column(desc0, e_desc, fwd), fwd))
        wend_b.append(times_b(column(asc0, e_asc, bwd), bwd))
        lag_b.append(times_b(rows_of(asc1_q, bwd), bwd))
        wout_f.append(times_c(rows_of(asc1_q, fwd), fwd))
        wout_b.append(times_c(column(desc1_0, e_desc, bwd), bwd))
        cols = slice(q * LANES, (q + 1) * LANES)
        wend_ref[gi, fwd, cols] = wend_f[q].astype(BF16)
        wend_ref[gi, bwd, cols] = wend_b[q].astype(BF16)

    meta_cols = N_META * SSM_GROUP // LANES
    x0 = sum(jnp.sum(wend_f[n_cols - meta_cols + m] * um_ref[gi, :, m * LANES:(m + 1) * LANES],
                     axis=1, keepdims=True) for m in range(meta_cols))

    lane = lax.broadcasted_iota(jnp.int32, (DIR_ROWS, LANES), 1)
    lag0_b = jnp.where(lane >= LANES - SSM_GROUP, bb_same[bwd], 0.0)
    zero = jnp.zeros((DIR_ROWS, LANES), F32)
    bcat = jnp.concatenate(
        [jnp.concatenate(wend_f + [zero] * n_cols, axis=1),
         jnp.concatenate([zero] * (n_cols - 1) + [lag0_b] + lag_b, axis=1)], axis=0)
    lane_r = lax.broadcasted_iota(jnp.int32, (SSM_GROUP, rows), 1)
    c2 = jnp.where((lane_r // SSM_STATE) % 2 == 0, cr_ref[gi], -cr_ref[gi])
    kk = jnp.dot(c2, bcat, precision=lax.Precision.HIGHEST,
                 preferred_element_type=F32)
    for tt in range(t):
        off = (t - 1 - tt) * SSM_GROUP
        toep_ref[gi, tt * SSM_GROUP:(tt + 1) * SSM_GROUP, :] = (
            kk[:, off:off + CHUNK_LANES].astype(BF16))

    wout = jnp.concatenate([jnp.concatenate(wout_f, axis=1),
                            jnp.concatenate(wout_b, axis=1)], axis=0)
    wout_ref[gi] = wout.T.astype(BF16)
    return pows[t], x0


SSM_GROUPS_PER_STEP = 2


def _chunk_scan(er, ei, ar, ai, x0, chunk, n_chunks, forward):
    width = er.shape[1]
    xr, xi = er, ei
    if x0 is not None:
        fr, fi = _cmul(ar, ai, x0[0], x0[1])
        xr = xr + jnp.where(chunk == 0, _lanes(fr, width), 0.0)
        xi = xi + jnp.where(chunk == 0, _lanes(fi, width), 0.0)

    def shifted(v, sh):
        if forward:
            return jnp.where(chunk >= sh, pltpu.roll(v, sh, axis=1), 0.0)
        return jnp.where(chunk < n_chunks - sh, pltpu.roll(v, width - sh, axis=1), 0.0)

    sh = 1
    while sh < n_chunks:
        sr, si = shifted(xr, sh), shifted(xi, sh)
        wr, wi = _lanes(ar, width), _lanes(ai, width)
        xr, xi = xr + wr * sr - wi * si, xi + wr * si + wi * sr
        ar, ai = _cmul(ar, ai, ar, ai)
        sh *= 2
    inr, ini = shifted(xr, 1), shifted(xi, 1)
    if x0 is not None:
        inr = jnp.where(chunk == 0, x0[0], inr)
        ini = jnp.where(chunk == 0, x0[1], ini)
    return inr, ini


def _ssm_kernel(u_ref, um_ref, csc_ref, cc_ref, bc_ref, cr_ref, y_ref, toep_ref, wend_ref,
                wout_ref, yi_ref, *, n_chunks):
    for gi in range(SSM_GROUPS_PER_STEP):
        (lam_r, lam_i), x0 = _build_chunk_ops(gi, csc_ref, cc_ref, bc_ref, cr_ref, um_ref,
                                              toep_ref, wend_ref, wout_ref)
        u = u_ref[gi]
        width = u.shape[1]
        e = jnp.dot(wend_ref[gi], u, preferred_element_type=F32)
        yi_ref[gi] = jnp.dot(toep_ref[gi], u, preferred_element_type=F32)
        chunk =lax.broadcasted_iota(jnp.int32, (1, width), 1) % n_chunks
        p = SSM_STATE
        ar, ai = lam_r, lam_i
        f_in = _chunk_scan(e[0:p], e[p:2 * p], ar[0:p], ai[0:p], (x0[0:p], x0[p:2 * p]),
                           chunk, n_chunks, True)
        b_in = _chunk_scan(e[2 * p:3 * p], e[3 * p:], ar[2 * p:3 * p], ai[2 * p:3 * p],
                           None, chunk, n_chunks, False)
        xin = jnp.concatenate([f_in[0], f_in[1], b_in[0], b_in[1]], axis=0).astype(BF16)
        y = yi_ref[gi] + jnp.dot(wout_ref[gi], xin, preferred_element_type=F32)
        y_ref[gi] = y.astype(BF16)


def _ssm(u_col, u_meta, csc, c_col, b_col, c_row, n_chunks):
    g, _, width = u_col.shape
    per = SSM_GROUPS_PER_STEP
    return pl.pallas_call(
        functools.partial(_ssm_kernel, n_chunks=n_chunks),
        grid=(g // per,),
        in_specs=[
            pl.BlockSpec((per, CHUNK_LANES, width), lambda i: (i, 0, 0)),
            pl.BlockSpec((per, 1, N_META * SSM_GROUP), lambda i: (i, 0, 0)),
            pl.BlockSpec((per, STATE_ROWS, 4), lambda i: (i, 0, 0)),
            pl.BlockSpec((per, STATE_ROWS, SSM_GROUP), lambda i: (i, 0, 0)),
            pl.BlockSpec((per, STATE_ROWS, SSM_GROUP), lambda i: (i, 0, 0)),
            pl.BlockSpec((per, SSM_GROUP, STATE_ROWS), lambda i: (i, 0, 0)),
        ],
        out_specs=pl.BlockSpec((per, CHUNK_LANES, width), lambda i: (i, 0, 0)),
        out_shape=jax.ShapeDtypeStruct((g, CHUNK_LANES, width), BF16),
        scratch_shapes=[
            pltpu.VMEM((per, CHUNK_LANES, CHUNK_LANES), BF16),
            pltpu.VMEM((per, STATE_ROWS, CHUNK_LANES), BF16),
            pltpu.VMEM((per, CHUNK_LANES, STATE_ROWS), BF16),
            pltpu.VMEM((per, CHUNK_LANES, width), F32),
        ],
        compiler_params=_params(),
        name="s5_mixer",
    )(u_col, u_meta, csc, c_col, b_col, c_row)


FF_CHUNK = 1024
MIX_PER = 4


MIX_SLOTS = 3


def _mix_ffn_kernel(x_hbm, co_ref, y_ref, ut_ref, d_ref, gwt_ref, gb_ref, wo_ref,
                    gf_ref, w1_ref, w2_ref, gl_ref, o_hbm, buf, sem_in, sem_out,
                    mixed_ref, h_ref, z_ref, a_ref):
    n_chunks = x_hbm.shape[1]
    rows = SUBLANES * n_chunks
    step = pl.program_id(0)
    n_tiles = pl.num_programs(0) - 1
    tile = step - 1

    def mix_stages(dst):
        per = MIX_PER
        n_half = SUBLANES // per
        state = {}

        def col(ref, k):
            return jnp.concatenate(
                [ref[:, r * SSM_GROUP:(r + 1) * SSM_GROUP, :].reshape(D_SSM, n_chunks)
                 for r in range(k * per, (k + 1) * per)], axis=1).astype(F32)

        def gelu_stage(k):
            def run():
                state["ge", k] = jax.nn.gelu(col(y_ref, k) + d_ref[...] * col(ut_ref, k))
            return run

        def glu_stage(k):
            def run():
                ge = state["ge", k]
                gate = (jnp.dot(gwt_ref[...], ge.astype(BF16), preferred_element_type=F32)
                        + gb_ref[...])
                state["s5t", k] = ge * jax.nn.sigmoid(gate)
            return run

        def store_stage(k):
            def run():
                rs = slice(k * per * n_chunks, (k + 1) * per * n_chunks)
                mixed_ref[dst, rs, :D_CONV] = co_ref[k * per:(k + 1) * per].reshape(
                    per * n_chunks, D_CONV)
                mixed_ref[dst, rs, D_CONV:] = state["s5t", k].T.astype(BF16)
            return run

        return [f(k) for f in (gelu_stage, glu_stage, store_stage) for k in range(n_half)]

    @pl.when(step == 0)
    def _():
        for cp in _step_copies(x_hbm, buf, sem_in, 0, 0, False):
            cp.start()
        for stage in mix_stages(0):
            stage()

    @pl.when(step >= 1)
    def _():
        slot = tile % MIX_SLOTS
        nxt = (tile + 1) % MIX_SLOTS

        @pl.when(tile >= 2)
        def _():
            for cp in _step_copies(o_hbm, buf, sem_out, tile - 2, nxt, True):
                cp.wait()

        @pl.when(tile + 1 < n_tiles)
        def _():
            for cp in _step_copies(x_hbm, buf, sem_in, tile + 1, nxt, False):
                cp.start()

        for cp in _step_copies(x_hbm, buf, sem_in, tile, slot, False):
            cp.wait()

        stages = mix_stages(step % 2)
        half = D_FF // 2
        n_slots = D_FF // FF_CHUNK + 2
        n_stages = len(stages)
        done = [0, 0]

        def fill_slot():
            done[0] += 1
            while done[1] * n_slots < done[0] * n_stages:
                stages.pop(0)()
                done[1] += 1

        h = buf[slot].reshape(rows, D_MODEL) + jnp.dot(
            mixed_ref[tile % 2], wo_ref[...], preferred_element_type=F32)
        z_ref[...] = _rms(h, gf_ref[...]).astype(BF16)
        h_ref[...] = h
        for k in range(2):
            for j in range(half // FF_CHUNK):
                cols = slice(k * half + j * FF_CHUNK, k * half + (j + 1) * FF_CHUNK)
                a = jnp.dot(z_ref[...], w1_ref[:, cols], preferred_element_type=F32)
                a_ref[:, j * FF_CHUNK:(j + 1) * FF_CHUNK] = (
                    jnp.square(jnp.maximum(a, 0.0)).astype(BF16))
                fill_slot()
            h_ref[...] += jnp.dot(a_ref[...], w2_ref[k * half:(k + 1) * half, :],
                                  preferred_element_type=F32)
            fill_slot()
        assert not stages
        buf[slot] = _rms(h_ref[...], gl_ref[...]).reshape(SUBLANES, n_chunks, D_MODEL)

        for cp in _step_copies(o_hbm, buf, sem_out, tile, slot, True):
            cp.start()

        @pl.when(tile == n_tiles - 1)
        def _():
            for cp in (_step_copies(o_hbm, buf, sem_out, tile - 1, (tile - 1) % MIX_SLOTS, True)
                       + _step_copies(o_hbm, buf, sem_out, tile, slot, True)):
                cp.wait()


def _mix_ffn(x4, conv_out, y_col, u_col, d_col, glu_wt, glu_b_col, w_out, g_ffn, w1, w2,
             g_final):
    bsz, n_chunks = x4.shape[:2]

    def const(shape):
        return pl.BlockSpec(shape, lambda t: (0, 0), pipeline_mode=pl.Buffered(1))

    n_tiles = CHUNK // SUBLANES * bsz
    hbm = pl.BlockSpec(memory_space=pl.ANY)

    def slab(s):
        return jnp.minimum(s, n_tiles - 1) // bsz

    def batch(s):
        return jnp.minimum(s, n_tiles - 1) % bsz

    col_blk = pl.BlockSpec((N_GROUPS, SUBLANES * SSM_GROUP, n_chunks),
                           lambda s: (0, slab(s), batch(s)))
    step_sems = pltpu.SemaphoreType.DMA((MIX_SLOTS, SUBLANES))
    return pl.pallas_call(
        _mix_ffn_kernel,
        grid=(n_tiles + 1,),
        in_specs=[
            hbm,
            pl.BlockSpec((SUBLANES, n_chunks, D_CONV), lambda s: (slab(s), batch(s), 0)),
            col_blk,
            col_blk,
            const((D_SSM, 1)),
            const((D_SSM, D_SSM)),
            const((D_SSM, 1)),
            const((D_MODEL, D_MODEL)),
            const((1, D_MODEL)),
            const((D_MODEL, D_FF)),
            const((D_FF, D_MODEL)),
            const((1, D_MODEL)),
        ],
        out_specs=hbm,
        out_shape=jax.ShapeDtypeStruct(x4.shape, F32),
        scratch_shapes=[
            pltpu.VMEM((MIX_SLOTS, SUBLANES, n_chunks, D_MODEL), F32),
            step_sems, step_sems,
            pltpu.VMEM((2, SUBLANES * n_chunks, D_MODEL), BF16),
            pltpu.VMEM((SUBLANES * n_chunks, D_MODEL), F32),
            pltpu.VMEM((SUBLANES * n_chunks, D_MODEL), BF16),
            pltpu.VMEM((SUBLANES * n_chunks, D_FF // 2), BF16),
        ],
        compiler_params=_step_tile_params(),
        name="mix_ffn",
    )(x4, conv_out, y_col, u_col, d_col, glu_wt, glu_b_col, w_out, g_ffn, w1, w2, g_final)


def _s5_operator_inputs(lbr, lbi, zr, zi, b_re, b_im, c_re, c_im):
    g, p, h = N_GROUPS, SSM_STATE, SSM_GROUP

    sc = jnp.transpose(jnp.stack([lbr, lbi, zr, zi], axis=-1), (1, 0, 2, 3))
    csc = jnp.broadcast_to(sc[:, :, None], (g, 2, 2, p, 4)).reshape(g, 4 * p, 4)
    c = jnp.stack([c_re, c_im], axis=0)
    b = jnp.stack([b_re, b_im], axis=0)
    c_col = jnp.transpose(c, (2, 1, 0, 4, 3)).reshape(g, 4 * p, h)
    b_col = jnp.transpose(b, (2, 1, 0, 3, 4)).reshape(g, 4 * p, h)
    c_row = jnp.transpose(c, (2, 3, 1, 0, 4)).reshape(g, h, 4 * p)
    return csc, c_col, b_col, c_row


def kernel(x, meta_tokens, norm_mix_g, w_in, conv_w, conv_b, conv_ln_g, conv_ln_b,
           ssm_lam_re, ssm_lam_im, ssm_log_dt, ssm_b_re, ssm_b_im, ssm_c_re, ssm_c_im,
           ssm_d, ssm_glu_w, ssm_glu_b, w_out, norm_ffn_g, w_ff1, w_ff2, norm_final_g):
    assert w_in.shape[0] == 1, "single-layer block"
    bsz, seq, _ = x.shape
    assert seq % CHUNK == 0 and CHUNK >= N_META
    n_chunks = seq // CHUNK
    g, h = N_GROUPS, SSM_GROUP

    x4 = x.reshape(bsz, n_chunks, CHUNK, D_MODEL)
    g_mix = norm_mix_g[0][None, :]
    u_conv, u_col, u_conv_m, u_ssm_m = _in_proj_step(x4, meta_tokens, g_mix, w_in[0])

    conv_out = _conv_module(u_conv, u_conv_m, conv_w[0], conv_b[0][None, :],
                            conv_ln_g[0][None, :], conv_ln_b[0][None, :], bsz)

    ldt = jnp.broadcast_to(ssm_log_dt[0][..., None], ssm_lam_re[0].shape)
    flat = lambda a: a.reshape(2 * g, SSM_STATE)
    lbr, lbi, zr, zi = [a.reshape(2, g, SSM_STATE) for a in
                        _zoh(flat(ssm_lam_re[0]), flat(ssm_lam_im[0]), flat(ldt))]
    csc, c_col, b_col, c_row = _s5_operator_inputs(
        lbr, lbi, zr, zi, ssm_b_re[0], ssm_b_im[0], ssm_c_re[0], ssm_c_im[0])

    u_meta = jnp.transpose(u_ssm_m.reshape(N_META, g, h), (1, 0, 2)).reshape(g, 1, N_META * h)
    y_col = _ssm(u_col, u_meta, csc, c_col, b_col, c_row, n_chunks)

    out = _mix_ffn(x4, conv_out, y_col, u_col, ssm_d[0][:, None],
                   ssm_glu_w[0].T.astype(BF16), ssm_glu_b[0][:, None],
                   w_out[0].astype(BF16), norm_ffn_g[0][None, :], w_ff1[0].astype(BF16),
                   w_ff2[0].astype(BF16), norm_final_g[None, :])
    return out.reshape(bsz, seq, D_MODEL)
```
